```python
import jax, jax.numpy as jnp
from jax import lax
import numpy as np

D_MODEL = 1024
BATCH = 8
SEQ = 8192
DEPTH = 4

DN_HEADS = 8
DN_DK = 128
DN_DV = 128
DN_CONV = 4
DN_CHUNK = 64
SW_Q_HEADS = 16
SW_KV_HEADS = 2
SW_HEAD_DIM = 64
SW_WINDOW = 128
SW_BLOCK = 128
ROPE_THETA = 500000.0
ROT_DIM = SW_HEAD_DIM // 4
D_FF = 4 * D_MODEL
EPS = 1e-6

DN_QK_W = DN_HEADS * DN_DK
DN_V_W = DN_HEADS * DN_DV
SW_Q_W = SW_Q_HEADS * SW_HEAD_DIM
SW_KV_W = SW_KV_HEADS * SW_HEAD_DIM
IN_SPLITS = [DN_QK_W, DN_QK_W, DN_V_W, DN_V_W, DN_HEADS, DN_HEADS,
             SW_Q_W, SW_KV_W, SW_KV_W, D_MODEL, D_MODEL]
D_IN = sum(IN_SPLITS)
IN_OFFSETS = np.cumsum(IN_SPLITS)[:-1].tolist()

kernel_name = 'hybrid_gdn_swa_sink_parallel_block'


def rmsnorm(x, g):
    xf = x.astype(jnp.float32)
    y = xf * lax.rsqrt(jnp.mean(xf * xf, axis=-1, keepdims=True) + EPS)
    return (y * g.astype(jnp.float32)).astype(x.dtype)


def l2norm(t):
    tf = t.astype(jnp.float32)
    return tf * lax.rsqrt(jnp.sum(tf * tf, axis=-1, keepdims=True) + EPS)


def causal_conv_silu(x, w):
    S = x.shape[1]
    K = w.shape[0]
    xp = jnp.pad(x, ((0, 0), (K - 1, 0), (0, 0)))
    y = sum(xp[:, j:j + S] * w[j] for j in range(K))
    return jax.nn.silu(y)


def gated_delta_rule(q, k, v, g, beta):
    B, S, H, dk = q.shape
    dv = v.shape[-1]
    C = DN_CHUNK
    N = S // C

    def chunks(t):
        return t.reshape(B, N, C, H, -1).transpose(0, 3, 1, 2, 4)

    q, k, v = chunks(q), chunks(k), chunks(v)
    g = g.reshape(B, N, C, H).transpose(0, 3, 1, 2)
    beta = beta.reshape(B, N, C, H).transpose(0, 3, 1, 2)
    g = jnp.cumsum(g, axis=-1)

    idx = jnp.arange(C)
    causal = idx[:, None] >= idx[None, :]
    strict = idx[:, None] > idx[None, :]
    diff = g[..., :, None] - g[..., None, :]
    decay = jnp.where(causal, jnp.exp(jnp.where(causal, diff, 0.0)), 0.0)

    kb = k * beta[..., None]
    L = jnp.where(strict, jnp.einsum('bhnid,bhnjd->bhnij', kb, k) * decay, 0.0)
    u = lax.linalg.triangular_solve(L, v * beta[..., None], left_side=True,
                                    lower=True, unit_diagonal=True)
    w = lax.linalg.triangular_solve(L, kb * jnp.exp(g)[..., None], left_side=True,
                                    lower=True, unit_diagonal=True)
    a_intra = jnp.einsum('bhnid,bhnjd->bhnij', q, k) * decay
    q_dec = q * jnp.exp(g)[..., None]
    g_last = g[..., -1]
    k_dec = k * jnp.exp(g_last[..., None] - g)[..., None]

    def to_front(t):
        return jnp.moveaxis(t, 2, 0)

    xs = (to_front(q_dec), to_front(k_dec), to_front(u), to_front(w),
          to_front(a_intra), jnp.moveaxis(g_last, 2, 0))

    def step(state, inp):
        qd, kd, u_c, w_c, a_c, gl = inp
        v_new = u_c - jnp.einsum('bhcd,bhde->bhce', w_c, state)
        o = (jnp.einsum('bhcd,bhde->bhce', qd, state)
             + jnp.einsum('bhij,bhje->bhie', a_c, v_new))
        state = state * jnp.exp(gl)[..., None, None] + jnp.einsum('bhcd,bhce->bhde', kd, v_new)
        return state, o

    state0 = jnp.zeros((B, H, dk, dv), jnp.float32)
    _, o = lax.scan(step, state0, xs)
    return o.transpose(1, 0, 3, 2, 4).reshape(B, S, H, dv)


def deltanet_branch(q_in, k_in, v_in, z, b_in, a_in, conv_w, a_log, dt_bias, norm_g):
    B, S, _ = q_in.shape
    qkv = causal_conv_silu(jnp.concatenate([q_in, k_in, v_in], axis=-1), conv_w)
    q, k, v = jnp.split(qkv, [DN_QK_W, 2 * DN_QK_W], axis=-1)
    q = l2norm(q.reshape(B, S, DN_HEADS, DN_DK)) * (DN_DK ** -0.5)
    k = l2norm(k.reshape(B, S, DN_HEADS, DN_DK))
    v = v.reshape(B, S, DN_HEADS, DN_DV).astype(jnp.float32)
    beta = jax.nn.sigmoid(b_in.astype(jnp.float32))
    g = -jnp.exp(a_log.astype(jnp.float32)) * jax.nn.softplus(
        a_in.astype(jnp.float32) + dt_bias.astype(jnp.float32))
    o = gated_delta_rule(q, k, v, g, beta)
    o = rmsnorm(o, norm_g) * jax.nn.silu(z.reshape(B, S, DN_HEADS, DN_DV).astype(jnp.float32))
    return o.reshape(B, S, DN_V_W).astype(q_in.dtype)


def partial_rope(x, positions):
    half = ROT_DIM // 2
    inv_freq = ROPE_THETA ** (-jnp.arange(half, dtype=jnp.float32) * (2.0 / ROT_DIM))
    ang = positions.astype(jnp.float32)[..., None] * inv_freq
    cos = jnp.cos(ang)[:, :, None, :]
    sin = jnp.sin(ang)[:, :, None, :]
    xr = x[..., :ROT_DIM].astype(jnp.float32)
    x1, x2 = xr[..., :half], xr[..., half:]
    rot = jnp.concatenate([x1 * cos - x2 * sin, x2 * cos + x1 * sin], axis=-1)
    return jnp.concatenate([rot.astype(x.dtype), x[..., ROT_DIM:]], axis=-1)


def swa_sink_branch(q_in, k_in, v_in, positions, sinks):
    B, S, _ = q_in.shape
    G = SW_Q_HEADS // SW_KV_HEADS
    nb = S // SW_BLOCK
    q = partial_rope(q_in.reshape(B, S, SW_Q_HEADS, SW_HEAD_DIM), positions)
    k = partial_rope(k_in.reshape(B, S, SW_KV_HEADS, SW_HEAD_DIM), positions)
    v = v_in.reshape(B, S, SW_KV_HEADS, SW_HEAD_DIM)

    qb = q.reshape(B, nb, SW_BLOCK, SW_KV_HEADS, G, SW_HEAD_DIM).astype(jnp.float32)

    def band(t):
        tp = jnp.pad(t, ((0, 0), (SW_BLOCK, 0), (0, 0), (0, 0)))
        tb = tp.reshape(B, nb + 1, SW_BLOCK, SW_KV_HEADS, SW_HEAD_DIM)
        return jnp.concatenate([tb[:, :-1], tb[:, 1:]], axis=2)

    kw = band(k).astype(jnp.float32)
    vw = band(v)
    scores = jnp.einsum('bnqhgd,bnkhd->bnhgqk', qb, kw) * (SW_HEAD_DIM ** -0.5)

    qi = jnp.arange(SW_BLOCK)[:, None] + SW_BLOCK
    ki = jnp.arange(2 * SW_BLOCK)[None, :]
    off = qi - ki
    in_band = (off >= 0) & (off < SW_WINDOW)
    blk = jnp.arange(nb)[:, None, None]
    valid = (blk * SW_BLOCK + ki[None] - SW_BLOCK) >= 0
    mask = (in_band[None] & valid)[None, :, None, None]
    scores = jnp.where(mask, scores, -jnp.inf)

    sink = sinks.astype(jnp.float32).reshape(SW_KV_HEADS, G)[None, None, :, :, None, None]
    m = jnp.maximum(jnp.max(scores, axis=-1, keepdims=True), sink)
    p = jnp.exp(scores - m)
    probs = p / (jnp.sum(p, axis=-1, keepdims=True) + jnp.exp(sink - m))
    o = jnp.einsum('bnhgqk,bnkhd->bnqhgd', probs.astype(vw.dtype), vw)
    return o.reshape(B, S, SW_Q_W)


def hybrid_layer(x, positions, pre_mix_g, w_in, dn_conv_w, dn_a_log, dn_dt_bias, dn_norm_g,
                 sw_sinks, w_up_dn, w_up_sw, w_o, post_mix_g, pre_mlp_g, w_ff1, w_ff2,
                 post_mlp_g):
    h = rmsnorm(x, pre_mix_g)
    proj = h @ w_in
    (dn_q, dn_k, dn_v, dn_z, dn_b, dn_a, sw_q, sw_k, sw_v,
     gate_a, gate_b) = jnp.split(proj, IN_OFFSETS, axis=-1)
    y_a = deltanet_branch(dn_q, dn_k, dn_v, dn_z, dn_b, dn_a, dn_conv_w, dn_a_log,
                          dn_dt_bias, dn_norm_g) @ w_up_dn
    y_b = swa_sink_branch(sw_q, sw_k, sw_v, positions, sw_sinks) @ w_up_sw
    mix = (jax.nn.sigmoid(gate_a) * y_a + jax.nn.sigmoid(gate_b) * y_b) @ w_o
    x = x + rmsnorm(mix, post_mix_g)

    h2 = rmsnorm(x, pre_mlp_g)
    ff = jnp.square(jax.nn.relu(h2 @ w_ff1)) @ w_ff2
    return x + rmsnorm(ff, post_mlp_g)


def _fwd_setup_inputs(seed: int = 0) -> dict:
    key = jax.random.key(seed)
    ks = jax.random.split(key, 20)
    f32 = jnp.float32

    def nrm(k, shape, scale):
        return jax.random.normal(k, shape, f32) * scale

    def gain(k, shape):
        return 1.0 + 0.02 * jax.random.normal(k, shape, f32)

    x = jax.random.normal(ks[0], (BATCH, SEQ, D_MODEL), f32)
    positions = jnp.broadcast_to(jnp.arange(SEQ, dtype=jnp.int32), (BATCH, SEQ))
    dt = jnp.exp(jax.random.uniform(ks[5], (DEPTH, DN_HEADS), f32,
                                    np.log(1e-3), np.log(1e-1)))
    return {
        'x': x,
        'positions': positions,
        'pre_mix_g': gain(ks[1], (DEPTH, D_MODEL)),
        'w_in': nrm(ks[2], (DEPTH, D_MODEL, D_IN), D_MODEL ** -0.5),
        'dn_conv_w': nrm(ks[3], (DEPTH, DN_CONV, 2 * DN_QK_W + DN_V_W), DN_CONV ** -0.5),
        'dn_a_log': jnp.log(jax.random.uniform(ks[4], (DEPTH, DN_HEADS), f32, 1.0, 16.0)),
        'dn_dt_bias': dt + jnp.log(-jnp.expm1(-dt)),
        'dn_norm_g': gain(ks[6], (DEPTH, DN_DV)),
        'sw_sinks': nrm(ks[7], (DEPTH, SW_Q_HEADS), 0.5),
        'w_up_dn': nrm(ks[8], (DEPTH, DN_V_W, D_MODEL), DN_V_W ** -0.5),
        'w_up_sw': nrm(ks[9], (DEPTH, SW_Q_W, D_MODEL), SW_Q_W ** -0.5),
        'w_o': nrm(ks[10], (DEPTH, D_MODEL, D_MODEL), D_MODEL ** -0.5),
        'post_mix_g': gain(ks[11], (DEPTH, D_MODEL)),
        'pre_mlp_g': gain(ks[12], (DEPTH, D_MODEL)),
        'w_ff1': nrm(ks[13], (DEPTH, D_MODEL, D_FF), D_MODEL ** -0.5),
        'w_ff2': nrm(ks[14], (DEPTH, D_FF, D_MODEL), D_FF ** -0.5),
        'post_mlp_g': gain(ks[15], (DEPTH, D_MODEL)),
    }


def _fwd_reference(x, positions, pre_mix_g, w_in, dn_conv_w, dn_a_log, dn_dt_bias, dn_norm_g,
              sw_sinks, w_up_dn, w_up_sw, w_o, post_mix_g, pre_mlp_g, w_ff1, w_ff2,
              post_mlp_g):
    for l in range(DEPTH):
        x = hybrid_layer(x, positions, pre_mix_g[l], w_in[l], dn_conv_w[l], dn_a_log[l],
                         dn_dt_bias[l], dn_norm_g[l], sw_sinks[l], w_up_dn[l], w_up_sw[l],
                         w_o[l], post_mix_g[l], pre_mlp_g[l], w_ff1[l], w_ff2[l],
                         post_mlp_g[l])
    return x


import jax as _jax
import jax.numpy as _jnp

TWIN_FORMAT = 'train_step'
FWD_PARAMS = ['x', 'positions', 'pre_mix_g', 'w_in', 'dn_conv_w', 'dn_a_log', 'dn_dt_bias', 'dn_norm_g', 'sw_sinks', 'w_up_dn', 'w_up_sw', 'w_o', 'post_mix_g', 'pre_mlp_g', 'w_ff1', 'w_ff2', 'post_mlp_g']
TWIN_WEIGHTS = ['pre_mix_g', 'w_in', 'dn_conv_w', 'dn_a_log', 'dn_dt_bias', 'dn_norm_g', 'sw_sinks', 'w_up_dn', 'w_up_sw', 'w_o', 'post_mix_g', 'pre_mlp_g', 'w_ff1', 'w_ff2', 'post_mlp_g']
TWIN_DIFF_INPUT = 'x'
TWIN_INPUTS = ['x', 'positions', 'pre_mix_g', 'w_in', 'dn_conv_w', 'dn_a_log', 'dn_dt_bias', 'dn_norm_g', 'sw_sinks', 'w_up_dn', 'w_up_sw', 'w_o', 'post_mix_g', 'pre_mlp_g', 'w_ff1', 'w_ff2', 'post_mlp_g', 'loss_target', 'm_pre_mix_g', 'm_w_in', 'm_dn_conv_w', 'm_dn_a_log', 'm_dn_dt_bias', 'm_dn_norm_g', 'm_sw_sinks', 'm_w_up_dn', 'm_w_up_sw', 'm_w_o', 'm_post_mix_g', 'm_pre_mlp_g', 'm_w_ff1', 'm_w_ff2', 'm_post_mlp_g', 'v_pre_mix_g', 'v_w_in', 'v_dn_conv_w', 'v_dn_a_log', 'v_dn_dt_bias', 'v_dn_norm_g', 'v_sw_sinks', 'v_w_up_dn', 'v_w_up_sw', 'v_w_o', 'v_post_mix_g', 'v_pre_mlp_g', 'v_w_ff1', 'v_w_ff2', 'v_post_mlp_g']
TWIN_OUTPUTS = ['loss', 'grad_x', 'grad_pre_mix_g', 'grad_w_in', 'grad_dn_conv_w', 'grad_dn_a_log', 'grad_dn_dt_bias', 'grad_dn_norm_g', 'grad_sw_sinks', 'grad_w_up_dn', 'grad_w_up_sw', 'grad_w_o', 'grad_post_mix_g', 'grad_pre_mlp_g', 'grad_w_ff1', 'grad_w_ff2', 'grad_post_mlp_g', 'delta_pre_mix_g', 'delta_w_in', 'delta_dn_conv_w', 'delta_dn_a_log', 'delta_dn_dt_bias', 'delta_dn_norm_g', 'delta_sw_sinks', 'delta_w_up_dn', 'delta_w_up_sw', 'delta_w_o', 'delta_post_mix_g', 'delta_pre_mlp_g', 'delta_w_ff1', 'delta_w_ff2', 'delta_post_mlp_g', 'new_m_pre_mix_g', 'new_m_w_in', 'new_m_dn_conv_w', 'new_m_dn_a_log', 'new_m_dn_dt_bias', 'new_m_dn_norm_g', 'new_m_sw_sinks', 'new_m_w_up_dn', 'new_m_w_up_sw', 'new_m_w_o', 'new_m_post_mix_g', 'new_m_pre_mlp_g', 'new_m_w_ff1', 'new_m_w_ff2', 'new_m_post_mlp_g', 'new_v_pre_mix_g', 'new_v_w_in', 'new_v_dn_conv_w', 'new_v_dn_a_log', 'new_v_dn_dt_bias', 'new_v_dn_norm_g', 'new_v_sw_sinks', 'new_v_w_up_dn', 'new_v_w_up_sw', 'new_v_w_o', 'new_v_post_mix_g', 'new_v_pre_mlp_g', 'new_v_w_ff1', 'new_v_w_ff2', 'new_v_post_mlp_g']
TWIN_LEAF_KINDS = {'loss': 'loss', 'grad_x': 'grad_x', 'grad_pre_mix_g': 'grad_w', 'grad_w_in': 'grad_w', 'grad_dn_conv_w': 'grad_w', 'grad_dn_a_log': 'grad_w', 'grad_dn_dt_bias': 'grad_w', 'grad_dn_norm_g': 'grad_w', 'grad_sw_sinks': 'grad_w', 'grad_w_up_dn': 'grad_w', 'grad_w_up_sw': 'grad_w', 'grad_w_o': 'grad_w', 'grad_post_mix_g': 'grad_w', 'grad_pre_mlp_g': 'grad_w', 'grad_w_ff1': 'grad_w', 'grad_w_ff2': 'grad_w', 'grad_post_mlp_g': 'grad_w', 'delta_pre_mix_g': 'delta_w', 'delta_w_in': 'delta_w', 'delta_dn_conv_w': 'delta_w', 'delta_dn_a_log': 'delta_w', 'delta_dn_dt_bias': 'delta_w', 'delta_dn_norm_g': 'delta_w', 'delta_sw_sinks': 'delta_w', 'delta_w_up_dn': 'delta_w', 'delta_w_up_sw': 'delta_w', 'delta_w_o': 'delta_w', 'delta_post_mix_g': 'delta_w', 'delta_pre_mlp_g': 'delta_w', 'delta_w_ff1': 'delta_w', 'delta_w_ff2': 'delta_w', 'delta_post_mlp_g': 'delta_w', 'new_m_pre_mix_g': 'new_m', 'new_m_w_in': 'new_m', 'new_m_dn_conv_w': 'new_m', 'new_m_dn_a_log': 'new_m', 'new_m_dn_dt_bias': 'new_m', 'new_m_dn_norm_g': 'new_m', 'new_m_sw_sinks': 'new_m', 'new_m_w_up_dn': 'new_m', 'new_m_w_up_sw': 'new_m', 'new_m_w_o': 'new_m', 'new_m_post_mix_g': 'new_m', 'new_m_pre_mlp_g': 'new_m', 'new_m_w_ff1': 'new_m', 'new_m_w_ff2': 'new_m', 'new_m_post_mlp_g': 'new_m', 'new_v_pre_mix_g': 'new_v', 'new_v_w_in': 'new_v', 'new_v_dn_conv_w': 'new_v', 'new_v_dn_a_log': 'new_v', 'new_v_dn_dt_bias': 'new_v', 'new_v_dn_norm_g': 'new_v', 'new_v_sw_sinks': 'new_v', 'new_v_w_up_dn': 'new_v', 'new_v_w_up_sw': 'new_v', 'new_v_w_o': 'new_v', 'new_v_post_mix_g': 'new_v', 'new_v_pre_mlp_g': 'new_v', 'new_v_w_ff1': 'new_v', 'new_v_w_ff2': 'new_v', 'new_v_post_mlp_g': 'new_v'}


def _forward(args):
    return _fwd_reference(*[args[k] for k in FWD_PARAMS])


def _output_shape():
    def fwd():
        inp = _fwd_setup_inputs(0)
        return _fwd_reference(*[inp[k] for k in FWD_PARAMS])
    out = _jax.eval_shape(fwd)
    return out.shape, out.dtype

N_MICROBATCH = 1
ADAM_LR = 0.001
ADAM_B1 = 0.9
ADAM_B2 = 0.999
ADAM_EPS = 1e-08
ADAM_WD = 0.01
ADAM_STEP = 10
PER_EXAMPLE_BATCH_AXIS = {'x': 0, 'positions': 0, 'loss_target': 0}
SHARED_INPUTS = []
_WEIGHT_DTYPES = {'pre_mix_g': _jnp.float32, 'w_in': _jnp.float32, 'dn_conv_w': _jnp.float32, 'dn_a_log': _jnp.float32, 'dn_dt_bias': _jnp.float32, 'dn_norm_g': _jnp.float32, 'sw_sinks': _jnp.float32, 'w_up_dn': _jnp.float32, 'w_up_sw': _jnp.float32, 'w_o': _jnp.float32, 'post_mix_g': _jnp.float32, 'pre_mlp_g': _jnp.float32, 'w_ff1': _jnp.float32, 'w_ff2': _jnp.float32, 'post_mlp_g': _jnp.float32}
MOMENT_SCALE = {'pre_mix_g': 3.704684e+01, 'w_in': 1.364924e+01, 'dn_conv_w': 9.612530e+00, 'dn_a_log': 2.301312e+01, 'dn_dt_bias': 2.186736e+01, 'dn_norm_g': 6.166285e+01, 'sw_sinks': 1.488846e+00, 'w_up_dn': 2.389382e+01, 'w_up_sw': 3.193776e+01, 'w_o': 3.870800e+01, 'post_mix_g': 7.718714e+01, 'pre_mlp_g': 1.972487e+01, 'w_ff1': 9.865685e+00, 'w_ff2': 4.827968e+01, 'post_mlp_g': 8.366531e+01}


def _to_microbatches(a, axis):
    t = _jnp.moveaxis(a, axis, 0)
    t = t.reshape((N_MICROBATCH, t.shape[0] // N_MICROBATCH) + t.shape[1:])
    return _jnp.moveaxis(t, 1, axis + 1)


def setup_inputs(seed: int = 0) -> dict:
    inp = _fwd_setup_inputs(seed)
    key = _jax.random.fold_in(_jax.random.key(seed), 7919)
    shape, _ = _output_shape()
    out = dict(inp)
    out["loss_target"] = _jax.random.normal(_jax.random.fold_in(key, 0), shape, _jnp.float32)
    for i, name in enumerate(TWIN_WEIGHTS):
        w = inp[name].astype(_jnp.float32)
        if MOMENT_SCALE is None:
            s = _jnp.sqrt(_jnp.mean(_jnp.square(w)) + 1e-30)
        else:
            s = MOMENT_SCALE[name]
        km, kv = _jax.random.split(_jax.random.fold_in(key, i + 1))
        out[name] = w
        out["m_" + name] = s * _jax.random.normal(km, w.shape, _jnp.float32)
        out["v_" + name] = (s * s) * _jax.random.uniform(kv, w.shape, _jnp.float32, 0.5, 1.5)
    if N_MICROBATCH > 1:
        for name, axis in PER_EXAMPLE_BATCH_AXIS.items():
            out[name] = _to_microbatches(out[name], axis)
    return {'x': out['x'], 'positions': out['positions'], 'pre_mix_g': out['pre_mix_g'], 'w_in': out['w_in'], 'dn_conv_w': out['dn_conv_w'], 'dn_a_log': out['dn_a_log'], 'dn_dt_bias': out['dn_dt_bias'], 'dn_norm_g': out['dn_norm_g'], 'sw_sinks': out['sw_sinks'], 'w_up_dn': out['w_up_dn'], 'w_up_sw': out['w_up_sw'], 'w_o': out['w_o'], 'post_mix_g': out['post_mix_g'], 'pre_mlp_g': out['pre_mlp_g'], 'w_ff1': out['w_ff1'], 'w_ff2': out['w_ff2'], 'post_mlp_g': out['post_mlp_g'], 'loss_target': out['loss_target'], 'm_pre_mix_g': out['m_pre_mix_g'], 'm_w_in': out['m_w_in'], 'm_dn_conv_w': out['m_dn_conv_w'], 'm_dn_a_log': out['m_dn_a_log'], 'm_dn_dt_bias': out['m_dn_dt_bias'], 'm_dn_norm_g': out['m_dn_norm_g'], 'm_sw_sinks': out['m_sw_sinks'], 'm_w_up_dn': out['m_w_up_dn'], 'm_w_up_sw': out['m_w_up_sw'], 'm_w_o': out['m_w_o'], 'm_post_mix_g': out['m_post_mix_g'], 'm_pre_mlp_g': out['m_pre_mlp_g'], 'm_w_ff1': out['m_w_ff1'], 'm_w_ff2': out['m_w_ff2'], 'm_post_mlp_g': out['m_post_mlp_g'], 'v_pre_mix_g': out['v_pre_mix_g'], 'v_w_in': out['v_w_in'], 'v_dn_conv_w': out['v_dn_conv_w'], 'v_dn_a_log': out['v_dn_a_log'], 'v_dn_dt_bias': out['v_dn_dt_bias'], 'v_dn_norm_g': out['v_dn_norm_g'], 'v_sw_sinks': out['v_sw_sinks'], 'v_w_up_dn': out['v_w_up_dn'], 'v_w_up_sw': out['v_w_up_sw'], 'v_w_o': out['v_w_o'], 'v_post_mix_g': out['v_post_mix_g'], 'v_pre_mlp_g': out['v_pre_mlp_g'], 'v_w_ff1': out['v_w_ff1'], 'v_w_ff2': out['v_w_ff2'], 'v_post_mlp_g': out['v_post_mlp_g']}


def _loss(weights, diff, rest, loss_target):
    with _jax.named_scope("forward"):
        args = {**rest, TWIN_DIFF_INPUT: diff, **{k: w.astype(_WEIGHT_DTYPES[k]) for k, w in weights.items()}}
        y = _forward(args)
    with _jax.named_scope("loss_head"):
        err = _jnp.square(y.astype(_jnp.float32) - loss_target)
        return 0.5 * _jnp.sum(_jnp.mean(err, axis=-1)) if err.ndim else 0.5 * err


def _adamw(w, g, m, v):
    m = ADAM_B1 * m + (1.0 - ADAM_B1) * g
    v = ADAM_B2 * v + (1.0 - ADAM_B2) * _jnp.square(g)
    m_hat = m / (1.0 - ADAM_B1 ** ADAM_STEP)
    v_hat = v / (1.0 - ADAM_B2 ** ADAM_STEP)
    delta = -ADAM_LR * (m_hat / (_jnp.sqrt(v_hat) + ADAM_EPS) + ADAM_WD * w)
    return delta, m, v


def reference(x, positions, pre_mix_g, w_in, dn_conv_w, dn_a_log, dn_dt_bias, dn_norm_g, sw_sinks, w_up_dn, w_up_sw, w_o, post_mix_g, pre_mlp_g, w_ff1, w_ff2, post_mlp_g, loss_target, m_pre_mix_g, m_w_in, m_dn_conv_w, m_dn_a_log, m_dn_dt_bias, m_dn_norm_g, m_sw_sinks, m_w_up_dn, m_w_up_sw, m_w_o, m_post_mix_g, m_pre_mlp_g, m_w_ff1, m_w_ff2, m_post_mlp_g, v_pre_mix_g, v_w_in, v_dn_conv_w, v_dn_a_log, v_dn_dt_bias, v_dn_norm_g, v_sw_sinks, v_w_up_dn, v_w_up_sw, v_w_o, v_post_mix_g, v_pre_mlp_g, v_w_ff1, v_w_ff2, v_post_mlp_g):
    given = dict(x=x, positions=positions, pre_mix_g=pre_mix_g, w_in=w_in, dn_conv_w=dn_conv_w, dn_a_log=dn_a_log, dn_dt_bias=dn_dt_bias, dn_norm_g=dn_norm_g, sw_sinks=sw_sinks, w_up_dn=w_up_dn, w_up_sw=w_up_sw, w_o=w_o, post_mix_g=post_mix_g, pre_mlp_g=pre_mlp_g, w_ff1=w_ff1, w_ff2=w_ff2, post_mlp_g=post_mlp_g, loss_target=loss_target, m_pre_mix_g=m_pre_mix_g, m_w_in=m_w_in, m_dn_conv_w=m_dn_conv_w, m_dn_a_log=m_dn_a_log, m_dn_dt_bias=m_dn_dt_bias, m_dn_norm_g=m_dn_norm_g, m_sw_sinks=m_sw_sinks, m_w_up_dn=m_w_up_dn, m_w_up_sw=m_w_up_sw, m_w_o=m_w_o, m_post_mix_g=m_post_mix_g, m_pre_mlp_g=m_pre_mlp_g, m_w_ff1=m_w_ff1, m_w_ff2=m_w_ff2, m_post_mlp_g=m_post_mlp_g, v_pre_mix_g=v_pre_mix_g, v_w_in=v_w_in, v_dn_conv_w=v_dn_conv_w, v_dn_a_log=v_dn_a_log, v_dn_dt_bias=v_dn_dt_bias, v_dn_norm_g=v_dn_norm_g, v_sw_sinks=v_sw_sinks, v_w_up_dn=v_w_up_dn, v_w_up_sw=v_w_up_sw, v_w_o=v_w_o, v_post_mix_g=v_post_mix_g, v_pre_mlp_g=v_pre_mlp_g, v_w_ff1=v_w_ff1, v_w_ff2=v_w_ff2, v_post_mlp_g=v_post_mlp_g)
    weights = {n: given[n] for n in TWIN_WEIGHTS}
    shared = {n: given[n] for n in SHARED_INPUTS}
    per_example = {n: given[n] for n in ['x', 'positions']}
    grad_fn = _jax.value_and_grad(_loss, argnums=(0, 1))

    def one_microbatch(ex, loss_target):
        ex = dict(ex)
        diff = ex.pop(TWIN_DIFF_INPUT)
        return grad_fn(weights, diff, {**shared, **ex}, loss_target)

    if N_MICROBATCH == 1:
        loss, (grad_w, grad_x) = one_microbatch(per_example, given["loss_target"])
    else:
        def body(carry, xs):
            loss_sum, grad_sum = carry
            l_k, (gw_k, gx_k) = one_microbatch(xs[0], xs[1])
            with _jax.named_scope("update"):
                return (loss_sum + l_k, _jax.tree.map(_jnp.add, grad_sum, gw_k)), gx_k

        init = (_jnp.zeros((), _jnp.float32), _jax.tree.map(_jnp.zeros_like, weights))
        (loss, grad_w), grad_x = _jax.lax.scan(body, init, (per_example, given["loss_target"]))
    with _jax.named_scope("update"):
        delta_w, new_m, new_v = {}, {}, {}
        for n in TWIN_WEIGHTS:
            delta_w[n], new_m[n], new_v[n] = _adamw(weights[n], grad_w[n], given["m_" + n], given["v_" + n])
    return (loss, grad_x, *[grad_w[n] for n in TWIN_WEIGHTS], *[delta_w[n] for n in TWIN_WEIGHTS],
            *[new_m[n] for n in TWIN_WEIGHTS], *[new_v[n] for n in TWIN_WEIGHTS])
```

```python
import functools

import numpy as np
import jax
import jax.numpy as jnp
from jax import lax
from jax.experimental import pallas as pl
from jax.experimental.pallas import tpu as pltpu

F32, BF16 = jnp.float32, jnp.bfloat16
MESH = pl.DeviceIdType.MESH

DN_HEADS = 8
DN_DK = 128
DN_CONV = 4
DN_CHUNK = 64
SW_Q_HEADS = 16
SW_KV_HEADS = 2
SW_HD = 64
SW_BLOCK = 128
ROPE_THETA = 500000.0
ROT_DIM = SW_HD // 4
EPS = 1e-6
ADAM_LR, ADAM_B1, ADAM_B2, ADAM_EPS, ADAM_WD, ADAM_STEP = 0.001, 0.9, 0.999, 1e-08, 0.01, 10

LANES = 128
SUBLANES = 8
VMEM_LIMIT = 48 * 1024 * 1024
KV_W = SW_KV_HEADS * SW_HD
BA_W = 256
PACK_ROWS = 512
PACK_W = 1024
DELTA_CB = 4


def _pcall(body, **kw):
    return pl.pallas_call(body, **kw)


def _cp(*sem):
    return pltpu.CompilerParams(dimension_semantics=sem, vmem_limit_bytes=VMEM_LIMIT)


def _tile(n, pref, unit=LANES):
    if n <= pref:
        return n
    t = (pref // unit) * unit
    while t > unit and n % t:
        t -= unit
    assert n % t == 0, (n, pref)
    return t


def _sds(shape, dtype):
    return jax.ShapeDtypeStruct(tuple(shape), dtype)


_DIMS = {"nn": ((1,), (0,)), "nt": ((1,), (1,)), "tn": ((0,), (0,))}


def _mm(name, a, b, mode, out_dtype, tm=1024, tn=1024, tk=1024):
    if mode == "nn":
        (M, K), (_, N) = a.shape, b.shape
    elif mode == "nt":
        (M, K), (N, _) = a.shape, b.shape
    else:
        (K, M), (_, N) = a.shape, b.shape
    tm, tn, tk = _tile(M, tm), _tile(N, tn), _tile(K, tk)
    nk = K // tk
    a_spec = {"nn": pl.BlockSpec((tm, tk), lambda i, j, k: (i, k)),
              "nt": pl.BlockSpec((tm, tk), lambda i, j, k: (i, k)),
              "tn": pl.BlockSpec((tk, tm), lambda i, j, k: (k, i))}[mode]
    b_spec = {"nn": pl.BlockSpec((tk, tn), lambda i, j, k: (k, j)),
              "nt": pl.BlockSpec((tn, tk), lambda i, j, k: (j, k)),
              "tn": pl.BlockSpec((tk, tn), lambda i, j, k: (k, j))}[mode]
    dims = (_DIMS[mode], ((), ()))

    def body(a_ref, b_ref, o_ref, acc_ref):
        k = pl.program_id(2)
        part = lax.dot_general(a_ref[...], b_ref[...], dims, preferred_element_type=F32)

        @pl.when(k == 0)
        def _():
            acc_ref[...] = part

        @pl.when(k > 0)
        def _():
            acc_ref[...] += part

        @pl.when(k == nk - 1)
        def _():
            o_ref[...] = acc_ref[...].astype(out_dtype)

    return _pcall(body, name=name, grid=(M // tm, N // tn, nk), in_specs=[a_spec, b_spec],
                  out_specs=pl.BlockSpec((tm, tn), lambda i, j, k: (i, j)), out_shape=_sds((M, N), out_dtype),
                  scratch_shapes=[pltpu.VMEM((tm, tn), F32)],
                  compiler_params=_cp("parallel", "parallel", "arbitrary"))(a, b)


def _rows(name, fn, n_rows, tq, ins, in_specs, out_shapes, out_specs):
    def body(*refs):
        fn(pl.program_id(0), *refs)

    return _pcall(body, name=name, grid=(n_rows // tq,), in_specs=in_specs, out_specs=out_specs,
                  out_shape=out_shapes, compiler_params=_cp("arbitrary"))(*ins)


def _rb(tq, w, cb=0):
    return pl.BlockSpec((tq, w), lambda i: (i, cb))


def _full(shape):
    return pl.BlockSpec(tuple(shape), lambda *_: (0,) * len(shape))


def _rms_fwd(x, g):
    r = lax.rsqrt(jnp.mean(x * x, axis=-1, keepdims=True) + EPS)
    return x * r * g


def _rms_bwd(x, g, dy):
    r = lax.rsqrt(jnp.mean(x * x, axis=-1, keepdims=True) + EPS)
    xh = x * r
    t = dy * g
    dx = r * (t - xh * jnp.mean(t * xh, axis=-1, keepdims=True))
    return dx, jnp.sum(dy * xh, axis=0, keepdims=True)


def _acc(i, ref, val):
    @pl.when(i == 0)
    def _():
        ref[...] = val

    @pl.when(i > 0)
    def _():
        ref[...] += val


def _sigmoid(x):
    return 1.0 / (1.0 + jnp.exp(-x))


def _pre_norm(x, g):
    S, D = x.shape
    tq = _tile(S, 512, SUBLANES)

    def fn(i, x_ref, g_ref, h_ref):
        h_ref[...] = _rms_fwd(x_ref[...], g_ref[...]).astype(BF16)

    return _rows("pre_norm", fn, S, tq, (x, g), [_rb(tq, D), _full((1, D))], _sds((S, D), BF16), _rb(tq, D))


def _post_mix(x, mix, g2, g3):
    S, D = x.shape
    tq = _tile(S, 512, SUBLANES)

    def fn(i, x_ref, m_ref, g2_ref, g3_ref, x1_ref, h2_ref):
        x1 = x_ref[...] + _rms_fwd(m_ref[...], g2_ref[...])
        x1_ref[...] = x1
        h2_ref[...] = _rms_fwd(x1, g3_ref[...]).astype(BF16)

    return _rows("post_mix", fn, S, tq, (x, mix, g2, g3), [_rb(tq, D), _rb(tq, D), _full((1, D)), _full((1, D))],
                 (_sds((S, D), F32), _sds((S, D), BF16)), (_rb(tq, D), _rb(tq, D)))


def _post_mlp(x1, ff, g4):
    S, D = x1.shape
    tq = _tile(S, 512, SUBLANES)

    def fn(i, x_ref, f_ref, g_ref, o_ref):
        o_ref[...] = x_ref[...] + _rms_fwd(f_ref[...], g_ref[...])

    return _rows("post_mlp", fn, S, tq, (x1, ff, g4), [_rb(tq, D), _rb(tq, D), _full((1, D))], _sds((S, D), F32), _rb(tq, D))


def _relu2(f1):
    S, F = f1.shape
    tq = _tile(S, 256, SUBLANES)

    def fn(i, f_ref, o_ref):
        r = jnp.maximum(f_ref[...], 0.0)
        o_ref[...] = (r * r).astype(BF16)

    return _rows("relu2", fn, S, tq, (f1,), [_rb(tq, F)], _sds((S, F), BF16), _rb(tq, F))


def _relu2_bwd(dact, f1):
    S, F = f1.shape
    tq = _tile(S, 256, SUBLANES)

    def fn(i, d_ref, f_ref, o_ref):
        o_ref[...] = (d_ref[...] * 2.0 * jnp.maximum(f_ref[...], 0.0)).astype(BF16)

    return _rows("relu2_bwd", fn, S, tq, (dact, f1), [_rb(tq, F), _rb(tq, F)], _sds((S, F), BF16), _rb(tq, F))


def _loss_head(y, target):
    S, D = y.shape
    tq = _tile(S, 512, SUBLANES)

    def fn(i, y_ref, t_ref, l_ref, d_ref):
        e = y_ref[...] - t_ref[...]
        d_ref[...] = e * (1.0 / D)
        part = jnp.sum(jnp.sum(e * e, axis=1, keepdims=True), axis=0, keepdims=True) * (0.5 / D)
        _acc(i, l_ref, jnp.broadcast_to(part, (1, LANES)))

    return _rows("loss_head", fn, S, tq, (y, target), [_rb(tq, D), _rb(tq, D)],
                 (_sds((1, LANES), F32), _sds((S, D), F32)), (_full((1, LANES)), _rb(tq, D)))


def _post_mlp_bwd(ff, g4, dx2):
    S, D = ff.shape
    tq = _tile(S, 512, SUBLANES)

    def fn(i, f_ref, g_ref, d_ref, o_ref, dg_ref):
        dx, dg = _rms_bwd(f_ref[...], g_ref[...], d_ref[...])
        o_ref[...] = dx.astype(BF16)
        _acc(i, dg_ref, dg)

    return _rows("post_mlp_bwd", fn, S, tq, (ff, g4, dx2), [_rb(tq, D), _full((1, D)), _rb(tq, D)],
                 (_sds((S, D), BF16), _sds((1, D), F32)), (_rb(tq, D), _full((1, D))))


def _mid_bwd(x1, g3, dh2, dx2, mix, g2):
    S, D = x1.shape
    tq = _tile(S, 256, SUBLANES)

    def fn(i, x_ref, g3_ref, dh_ref, dx2_ref, m_ref, g2_ref, dx1_ref, dm_ref, dg3_ref, dg2_ref):
        d, dg3 = _rms_bwd(x_ref[...], g3_ref[...], dh_ref[...])
        dx1 = dx2_ref[...] + d
        dx1_ref[...] = dx1
        dm, dg2 = _rms_bwd(m_ref[...], g2_ref[...], dx1)
        dm_ref[...] = dm.astype(BF16)
        _acc(i, dg3_ref, dg3)
        _acc(i, dg2_ref, dg2)

    r, f = _rb(tq, D), _full((1, D))
    return _rows("mid_bwd", fn, S, tq, (x1, g3, dh2, dx2, mix, g2), [r, f, r, r, r, f],
                 (_sds((S, D), F32), _sds((S, D), BF16), _sds((1, D), F32), _sds((1, D), F32)), (r, r, f, f))


def _pre_norm_bwd(x, g1, dh, dx1):
    S, D = x.shape
    tq = _tile(S, 512, SUBLANES)

    def fn(i, x_ref, g_ref, dh_ref, dx1_ref, dx_ref, dg_ref):
        d, dg = _rms_bwd(x_ref[...], g_ref[...], dh_ref[...])
        dx_ref[...] = dx1_ref[...] + d
        _acc(i, dg_ref, dg)

    r, f = _rb(tq, D), _full((1, D))
    return _rows("pre_norm_bwd", fn, S, tq, (x, g1, dh, dx1), [r, f, r, r], (_sds((S, D), F32), _sds((1, D), F32)), (r, f))


def _mix(proj, ya, yb, D, cb_a):
    S = ya.shape[0]
    tq = _tile(S, 256, SUBLANES)

    def fn(i, ga_ref, gb_ref, ya_ref, yb_ref, o_ref):
        o_ref[...] = (_sigmoid(ga_ref[...]) * ya_ref[...] + _sigmoid(gb_ref[...]) * yb_ref[...]).astype(BF16)

    return _rows("mix", fn, S, tq, (proj, proj, ya, yb), [_rb(tq, D, cb_a), _rb(tq, D, cb_a + 1), _rb(tq, D), _rb(tq, D)],
                 _sds((S, D), BF16), _rb(tq, D))


def _mix_bwd(proj, ya, yb, dmixin, D, cb_a):
    S = ya.shape[0]
    tq = _tile(S, 256, SUBLANES)

    def fn(i, ga_ref, gb_ref, ya_ref, yb_ref, d_ref, dya_ref, dyb_ref, dga_ref, dgb_ref):
        d = d_ref[...]
        sa, sb = _sigmoid(ga_ref[...]), _sigmoid(gb_ref[...])
        dya_ref[...] = (d * sa).astype(BF16)
        dyb_ref[...] = (d * sb).astype(BF16)
        dga_ref[...] = (d * ya_ref[...] * sa * (1.0 - sa)).astype(BF16)
        dgb_ref[...] = (d * yb_ref[...] * sb * (1.0 - sb)).astype(BF16)

    r = _rb(tq, D)
    o = _sds((S, D), BF16)
    return _rows("mix_bwd", fn, S, tq, (proj, proj, ya, yb, dmixin), [_rb(tq, D, cb_a), _rb(tq, D, cb_a + 1), r, r, r],
                 (o, o, o, o), (r, r, r, r))


def _shift_down(xe, k, tq):
    return pltpu.roll(xe, k, 0)[SUBLANES:SUBLANES + tq]


def _conv_pre(cur_ref, halo_ref, w_ref, i, tq):
    x = cur_ref[...]
    halo = jnp.where(i > 0, halo_ref[...], 0.0)
    xe = jnp.concatenate([halo, x], axis=0)
    xs = [x] + [_shift_down(xe, k, tq) for k in range(1, DN_CONV)]
    w = w_ref[...]
    c = sum(w[DN_CONV - 1 - k:DN_CONV - k, :] * xs[k] for k in range(DN_CONV))
    return c, xs


def _dn_prep(proj, conv_w, W):
    S = proj.shape[0]
    tq = _tile(S, 256, SUBLANES)
    hb = tq // SUBLANES

    def body(cur_ref, halo_ref, w_ref, o_ref):
        j, i = pl.program_id(0), pl.program_id(1)
        c, _ = _conv_pre(cur_ref, halo_ref, w_ref, i, tq)
        y = c * _sigmoid(c)
        scale = jnp.where(j == 0, DN_DK ** -0.5, 1.0)
        for h in range(W // DN_DK):
            sl = slice(h * DN_DK, (h + 1) * DN_DK)
            yh = y[:, sl]
            rs = lax.rsqrt(jnp.sum(yh * yh, axis=-1, keepdims=True) + EPS)
            o_ref[:, sl] = jnp.where(j == 2, yh, yh * rs * scale)

    return _pcall(body, name="dn_prep", grid=(3, S // tq),
                  in_specs=[pl.BlockSpec((tq, W), lambda j, i: (i, j)),
                            pl.BlockSpec((SUBLANES, W), lambda j, i: (jnp.maximum(i * hb - 1, 0), j)),
                            pl.BlockSpec((DN_CONV, W), lambda j, i: (0, j))],
                  out_specs=pl.BlockSpec((tq, W), lambda j, i: (i, j)), out_shape=_sds((S, 3 * W), F32),
                  compiler_params=_cp("arbitrary", "arbitrary"))(proj, proj, conv_w)


def _dn_prep_bwd_a(proj, conv_w, dqkv, W):
    S = proj.shape[0]
    tq = _tile(S, 256, SUBLANES)
    hb = tq // SUBLANES

    def body(cur_ref, halo_ref, w_ref, d_ref, dc_ref, dw_ref):
        j, i = pl.program_id(0), pl.program_id(1)
        c, xs = _conv_pre(cur_ref, halo_ref, w_ref, i, tq)
        sg = _sigmoid(c)
        y = c * sg
        scale = jnp.where(j == 0, DN_DK ** -0.5, 1.0)
        dout = d_ref[0]
        dys = []
        for h in range(W // DN_DK):
            sl = slice(h * DN_DK, (h + 1) * DN_DK)
            yh, dh = y[:, sl], dout[:, sl]
            rs = lax.rsqrt(jnp.sum(yh * yh, axis=-1, keepdims=True) + EPS)
            yn = yh * rs
            dn = scale * rs * (dh - yn * jnp.sum(dh * yn, axis=-1, keepdims=True))
            dys.append(jnp.where(j == 2, dh, dn))
        dy = jnp.concatenate(dys, axis=1)
        dc = dy * (sg * (1.0 + c * (1.0 - sg)))
        dc_ref[...] = dc
        dw = jnp.concatenate([jnp.sum(dc * xs[DN_CONV - 1 - r], axis=0, keepdims=True) for r in range(DN_CONV)], axis=0)
        _acc(i, dw_ref, dw)

    return _pcall(body, name="dn_prep_bwd_a", grid=(3, S // tq),
                  in_specs=[pl.BlockSpec((tq, W), lambda j, i: (i, j)),
                            pl.BlockSpec((SUBLANES, W), lambda j, i: (jnp.maximum(i * hb - 1, 0), j)),
                            pl.BlockSpec((DN_CONV, W), lambda j, i: (0, j)),
                            pl.BlockSpec((1, tq, W), lambda j, i: (j, i, 0))],
                  out_specs=(pl.BlockSpec((tq, W), lambda j, i: (i, j)), pl.BlockSpec((DN_CONV, W), lambda j, i: (0, j))),
                  out_shape=(_sds((S, 3 * W), F32), _sds((DN_CONV, 3 * W), F32)),
                  compiler_params=_cp("arbitrary", "arbitrary"))(proj, proj, conv_w, dqkv)


def _dn_prep_bwd_b(dc, conv_w, W):
    S = dc.shape[0]
    tq = _tile(S, 256, SUBLANES)
    hb = tq // SUBLANES
    nblk = S // tq

    def body(cur_ref, nxt_ref, w_ref, o_ref):
        i = pl.program_id(1)
        d = cur_ref[...]
        nxt = jnp.where(i < nblk - 1, nxt_ref[...], 0.0)
        de = jnp.concatenate([d, nxt], axis=0)
        w = w_ref[...]
        out = w[DN_CONV - 1:DN_CONV, :] * d
        for k in range(1, DN_CONV):
            out = out + w[DN_CONV - 1 - k:DN_CONV - k, :] * pltpu.roll(de, tq + SUBLANES - k, 0)[0:tq]
        o_ref[...] = out.astype(BF16)

    return _pcall(body, name="dn_prep_bwd_b", grid=(3, nblk),
                  in_specs=[pl.BlockSpec((tq, W), lambda j, i: (i, j)),
                            pl.BlockSpec((SUBLANES, W), lambda j, i: (jnp.minimum((i + 1) * hb, S // SUBLANES - 1), j)),
                            pl.BlockSpec((DN_CONV, W), lambda j, i: (0, j))],
                  out_specs=pl.BlockSpec((tq, W), lambda j, i: (i, j)), out_shape=_sds((S, 3 * W), BF16),
                  compiler_params=_cp("arbitrary", "arbitrary"))(dc, dc, conv_w)


def _gate_terms(ba, al, dt):
    u = ba + dt
    sp = jnp.maximum(u, 0.0) + jnp.log(1.0 + jnp.exp(-jnp.abs(u)))
    return _sigmoid(ba), -jnp.exp(al) * sp, u


def _dn_gates(proj, alog_row, dt_row, H, cb_ba):
    S = proj.shape[0]
    tq = _tile(S, 512, SUBLANES)
    W = H * DN_DK

    def fn(i, ba_ref, al_ref, dt_ref, be_ref, g_ref):
        bet, gg, _ = _gate_terms(ba_ref[...], al_ref[...], dt_ref[...])
        for h in range(H):
            sl = slice(h * DN_DK, (h + 1) * DN_DK)
            be_ref[:, sl] = jnp.broadcast_to(bet[:, h:h + 1], (tq, DN_DK))
            g_ref[:, sl] = jnp.broadcast_to(gg[:, H + h:H + h + 1], (tq, DN_DK))

    return _rows("dn_gates", fn, S, tq, (proj, alog_row, dt_row), [_rb(tq, LANES, cb_ba), _full((1, LANES)), _full((1, LANES))],
                 (_sds((S, W), F32), _sds((S, W), F32)), (_rb(tq, W), _rb(tq, W)))


def _dn_gates_bwd(proj, alog_row, dt_row, dbeta_b, dg_b, H, cb_ba):
    S = proj.shape[0]
    tq = _tile(S, 512, SUBLANES)
    W = H * DN_DK

    def fn(i, ba_ref, al_ref, dt_ref, db_ref, dg_ref, o_ref, dal_ref, ddt_ref):
        bet, gg, u = _gate_terms(ba_ref[...], al_ref[...], dt_ref[...])
        lane = lax.broadcasted_iota(jnp.int32, (tq, LANES), 1)
        d = jnp.zeros((tq, LANES), F32)
        for h in range(H):
            d = jnp.where(lane == h, db_ref[:, h * DN_DK:h * DN_DK + 1], d)
            d = jnp.where(lane == H + h, dg_ref[:, h * DN_DK:h * DN_DK + 1], d)
        is_a = (lane >= H) & (lane < 2 * H)
        da = jnp.where(is_a, d * (-jnp.exp(al_ref[...]) * _sigmoid(u)), 0.0)
        dlog = jnp.where(lane < H, d * bet * (1.0 - bet), da)
        o_ref[...] = jnp.concatenate([dlog, jnp.zeros((tq, BA_W - LANES), F32)], axis=1).astype(BF16)
        _acc(i, dal_ref, jnp.sum(jnp.where(is_a, d * gg, 0.0), axis=0, keepdims=True))
        _acc(i, ddt_ref, jnp.sum(da, axis=0, keepdims=True))

    f = _full((1, LANES))
    return _rows("dn_gates_bwd", fn, S, tq, (proj, alog_row, dt_row, dbeta_b, dg_b),
                 [_rb(tq, LANES, cb_ba), f, f, _rb(tq, W), _rb(tq, W)],
                 (_sds((S, BA_W), BF16), _sds((1, LANES), F32), _sds((1, LANES), F32)), (_rb(tq, BA_W), f, f))


def _dn_out(o, proj, ng, W, cb_z):
    S = o.shape[0]
    tq = _tile(S, 256, SUBLANES)

    def fn(i, o_ref, z_ref, g_ref, y_ref):
        for h in range(W // DN_DK):
            sl = slice(h * DN_DK, (h + 1) * DN_DK)
            z = z_ref[:, sl]
            y_ref[:, sl] = (_rms_fwd(o_ref[:, sl], g_ref[...]) * (z * _sigmoid(z))).astype(BF16)

    return _rows("dn_out", fn, S, tq, (o, proj, ng), [_rb(tq, W), _rb(tq, W, cb_z), _full((1, DN_DK))], _sds((S, W), BF16), _rb(tq, W))


def _dn_out_bwd(o, proj, ng, dy, W, cb_z):
    S = o.shape[0]
    tq = _tile(S, 256, SUBLANES)

    def fn(i, o_ref, z_ref, g_ref, d_ref, do_ref, dz_ref, dg_ref):
        g = g_ref[...]
        dg = jnp.zeros((1, DN_DK), F32)
        for h in range(W // DN_DK):
            sl = slice(h * DN_DK, (h + 1) * DN_DK)
            oh, z, d = o_ref[:, sl], z_ref[:, sl], d_ref[:, sl]
            sg = _sigmoid(z)
            dn = d * (z * sg)
            dz_ref[:, sl] = (d * _rms_fwd(oh, g) * (sg * (1.0 + z * (1.0 - sg)))).astype(BF16)
            dx, dgh = _rms_bwd(oh, g, dn)
            do_ref[:, sl] = dx
            dg = dg + dgh
        _acc(i, dg_ref, dg)

    r = _rb(tq, W)
    return _rows("dn_out_bwd", fn, S, tq, (o, proj, ng, dy), [r, _rb(tq, W, cb_z), _full((1, DN_DK)), r],
                 (_sds((S, W), F32), _sds((S, W), BF16), _sds((1, DN_DK), F32)), (r, r, _full((1, DN_DK))))


def _hdot(a, b, mode="nn"):
    return lax.dot_general(a, b, (_DIMS[mode], ((), ())), preferred_element_type=F32, precision=lax.Precision.HIGHEST)


def _bdot(a, b, mode="nn"):
    return lax.dot_general(a.astype(BF16), b.astype(BF16), (_DIMS[mode], ((), ())), preferred_element_type=F32)


def _rsum(x):
    return jnp.broadcast_to(jnp.sum(x, axis=-1, keepdims=True), x.shape)


def _chunk_local(q, k, v, be, gr):
    C = DN_CHUNK
    ri = lax.broadcasted_iota(jnp.int32, (C, C), 0)
    ci = lax.broadcasted_iota(jnp.int32, (C, C), 1)
    causal, strict = ri >= ci, ri > ci
    gc = _hdot(causal.astype(F32), gr)
    g_col, g_row = gc[:, :C], gc.T[:C, :]
    decay = jnp.where(causal, jnp.exp(jnp.where(causal, g_col - g_row, 0.0)), 0.0)
    eg = jnp.exp(gc)
    glast = gc[C - 1:C, :]
    ek = jnp.exp(glast - gc)
    gam = jnp.exp(glast)
    kb = k * be
    kk = _hdot(kb, k, "nt")
    nl = jnp.where(strict, -kk * decay, 0.0)
    t = (ri == ci).astype(F32) + nl
    pw = _hdot(nl, nl)
    for s in range(5):
        t = t + _hdot(t, pw)
        if s < 4:
            pw = _hdot(pw, pw)
    vb, kbe = v * be, kb * eg
    u, w = _hdot(t, vb), _hdot(t, kbe)
    qk = _bdot(q, k, "nt")
    a = qk * decay
    return dict(gc=gc, decay=decay, eg=eg, ek=ek, gam=gam, kb=kb, kk=kk, t=t, vb=vb, kbe=kbe, u=u, w=w, qk=qk, a=a,
                qd=q * eg, kd=k * ek, strict=strict, ri=ri, ci=ci)


def _delta_fwd(qkv, beta_b, g_b, H, CB):
    S = qkv.shape[0]
    C, dk = DN_CHUNK, DN_DK
    N = S // C
    R = CB * C

    def body(q_ref, k_ref, v_ref, b_ref, g_ref, o_ref, st_ref, s_ref):
        @pl.when(pl.program_id(1) == 0)
        def _():
            s_ref[...] = jnp.zeros((dk, dk), F32)

        for c in range(CB):
            sl = slice(c * C, (c + 1) * C)
            L = _chunk_local(q_ref[sl, :], k_ref[sl, :], v_ref[sl, :], b_ref[sl, :], g_ref[sl, :])
            s0 = s_ref[...]
            st_ref[0, c] = s0
            vn = L["u"] - _bdot(L["w"], s0)
            o_ref[sl, :] = _bdot(L["qd"], s0) + _bdot(L["a"], vn)
            s_ref[...] = s0 * L["gam"] + _bdot(L["kd"], vn, "tn")

    blk = lambda off: pl.BlockSpec((R, dk), lambda h, n: (n, off + h))
    return _pcall(body, name="delta_fwd", grid=(H, N // CB),
                  in_specs=[blk(0), blk(H), blk(2 * H), blk(0), blk(0)],
                  out_specs=(blk(0), pl.BlockSpec((1, CB, dk, dk), lambda h, n: (h, n, 0, 0))),
                  out_shape=(_sds((S, H * dk), F32), _sds((H, N, dk, dk), F32)),
                  scratch_shapes=[pltpu.VMEM((dk, dk), F32)],
                  compiler_params=_cp("arbitrary", "arbitrary"))(qkv, qkv, qkv, beta_b, g_b)


def _delta_bwd(qkv, beta_b, g_b, states, do, H, CB):
    S = qkv.shape[0]
    C, dk = DN_CHUNK, DN_DK
    N = S // C
    R = CB * C
    NB = N // CB

    def body(q_ref, k_ref, v_ref, b_ref, g_ref, st_ref, do_ref, dqkv_ref, db_ref, dg_ref, ds_ref):
        @pl.when(pl.program_id(1) == 0)
        def _():
            ds_ref[...] = jnp.zeros((dk, dk), F32)

        dq_ref, dk_ref, dv_ref = dqkv_ref.at[0], dqkv_ref.at[1], dqkv_ref.at[2]

        ones = jnp.ones((C, dk), F32)
        for c in reversed(range(CB)):
            sl = slice(c * C, (c + 1) * C)
            q, k, v, be = q_ref[sl, :], k_ref[sl, :], v_ref[sl, :], b_ref[sl, :]
            L = _chunk_local(q, k, v, be, g_ref[sl, :])
            t, decay, kb, eg, ek, gam, qd, kd = L["t"], L["decay"], L["kb"], L["eg"], L["ek"], L["gam"], L["qd"], L["kd"]
            s0 = st_ref[0, c]
            d_o = do_ref[sl, :]
            ds1 = ds_ref[...]
            vn = L["u"] - _bdot(L["w"], s0)
            dvn = _bdot(L["a"], d_o, "tn") + _bdot(kd, ds1)
            da = _bdot(d_o, vn, "nt")
            dqd = _bdot(d_o, s0, "nt")
            dkd = _bdot(vn, ds1, "nt")
            ds_ref[...] = _bdot(qd, d_o, "tn") + ds1 * gam - _bdot(L["w"], dvn, "tn")
            dgam = jnp.sum(jnp.sum(s0 * ds1, axis=1, keepdims=True), axis=0, keepdims=True)
            dw = -_bdot(dvn, s0, "nt")
            dvb, dkbe = _hdot(t, dvn, "tn"), _hdot(t, dw, "tn")
            dt = _hdot(dvn, L["vb"], "nt") + _hdot(dw, L["kbe"], "nt")
            dl = -_hdot(t, _hdot(dt, t, "nt"), "tn")
            m = jnp.where(L["strict"], dl * decay, 0.0)
            ma = da * decay
            dkb = _hdot(m, k) + dkbe * eg
            dk_ref[sl, :] = _hdot(m, kb, "tn") + _bdot(ma, q, "tn") + dkd * ek + dkb * be
            dq_ref[sl, :] = _bdot(ma, k) + dqd * eg
            e = m * L["kk"] + ma * L["qk"]
            t_kd = _rsum(dkd * kd)
            dgc = _hdot(e, ones) - _hdot(e, ones, "tn") + _rsum(dqd * qd) - t_kd + _rsum(dkbe * L["kbe"])
            dglast = jnp.sum(t_kd[:, 0:1], axis=0, keepdims=True) + dgam * gam[:, 0:1]
            row = lax.broadcasted_iota(jnp.int32, (C, dk), 0)
            dgc = dgc + jnp.where(row == C - 1, dglast, 0.0)
            db_ref[sl, :] = _rsum(dkb * k) + _rsum(dvb * v)
            dv_ref[sl, :] = dvb * be
            dg_ref[sl, :] = _hdot((L["ri"] <= L["ci"]).astype(F32), dgc)

    blk = lambda off: pl.BlockSpec((R, dk), lambda h, n: (NB - 1 - n, off + h))
    W = H * dk
    return _pcall(body, name="delta_bwd", grid=(H, NB),
                  in_specs=[blk(0), blk(H), blk(2 * H), blk(0), blk(0),
                            pl.BlockSpec((1, CB, dk, dk), lambda h, n: (h, NB - 1 - n, 0, 0)), blk(0)],
                  out_specs=(pl.BlockSpec((3, R, dk), lambda h, n: (0, NB - 1 - n, h)), blk(0), blk(0)),
                  out_shape=(_sds((3, S, W), F32), _sds((S, W), F32), _sds((S, W), F32)),
                  scratch_shapes=[pltpu.VMEM((dk, dk), F32)],
                  compiler_params=_cp("arbitrary", "arbitrary"))(qkv, qkv, qkv, beta_b, g_b, states, do)


def _rope_consts():
    lane = np.arange(LANES) % SW_HD
    half = ROT_DIM // 2
    inv = (ROPE_THETA ** (-np.arange(half, dtype=np.float32) * np.float32(2.0 / ROT_DIM))).astype(np.float32)
    freq = np.where(lane < ROT_DIM, inv[lane % half], 0.0).astype(np.float32)
    lo = (lane < half).astype(np.float32)
    hi = ((lane >= half) & (lane < ROT_DIM)).astype(np.float32)
    return jnp.asarray(np.stack([freq, -lo, hi] + [np.zeros(LANES, np.float32)] * 5))


def _rope_tables(pos_col):
    S = pos_col.shape[0]
    tq = _tile(S, 1024, SUBLANES)

    def fn(i, p_ref, c_ref, cos_ref, s1_ref, s2_ref):
        ang = p_ref[...].astype(F32) * c_ref[0:1, :]
        sn = jnp.sin(ang)
        cos_ref[...] = jnp.cos(ang)
        s1_ref[...] = sn * c_ref[1:2, :]
        s2_ref[...] = sn * c_ref[2:3, :]

    o, r = _sds((S, LANES), F32), _rb(tq, LANES)
    return _rows("rope_tables", fn, S, tq, (pos_col, _rope_consts()), [_rb(tq, 1), _full((SUBLANES, LANES))], (o, o, o), (r, r, r))


def _wide(a, w):
    return a if w == LANES else jnp.tile(a, (1, w // LANES))


def _rope(x, cos, s1, s2):
    w, h = x.shape[1], ROT_DIM // 2
    return x * _wide(cos, w) + pltpu.roll(x, w - h, 1) * _wide(s1, w) + pltpu.roll(x, h, 1) * _wide(s2, w)


def _unrope(d, cos, s1, s2):
    w, h = d.shape[1], ROT_DIM // 2
    return d * _wide(cos, w) + pltpu.roll(d * _wide(s1, w), h, 1) + pltpu.roll(d * _wide(s2, w), w - h, 1)


def _swa_setup(n, q_ref, kc_ref, kp_ref, vc_ref, vp_ref, tc, tp):
    B = SW_BLOCK
    qr = _rope(q_ref[...], tc[0][...], tc[1][...], tc[2][...])
    kw = jnp.concatenate([_rope(kp_ref[...], tp[0][...], tp[1][...], tp[2][...]),
                          _rope(kc_ref[...], tc[0][...], tc[1][...], tc[2][...])], axis=0)
    vw = jnp.concatenate([vp_ref[...], vc_ref[...]], axis=0)
    lane = lax.broadcasted_iota(jnp.int32, (2 * B, LANES), 1)
    heads = []
    for hk in range(SW_KV_HEADS):
        kh, vh = kw[:, hk * SW_HD:(hk + 1) * SW_HD], vw[:, hk * SW_HD:(hk + 1) * SW_HD]
        kk, vv = jnp.concatenate([kh, kh], axis=1), jnp.concatenate([vh, vh], axis=1)
        heads.append(tuple(jnp.where(sel, t, 0.0).astype(BF16) for t in (kk, vv) for sel in (lane < SW_HD, lane >= SW_HD)))
    qi = lax.broadcasted_iota(jnp.int32, (B, 2 * B), 0) + B
    ki = lax.broadcasted_iota(jnp.int32, (B, 2 * B), 1)
    off = qi - ki
    ok = (off >= 0) & (off < SW_BLOCK) & ((ki >= B) | (n > 0))
    return qr, heads, ok, lane


def _swa_probs(qj, kh, ok, sk):
    s = _bdot(qj, kh, "nt") * (SW_HD ** -0.5)
    s = jnp.where(ok, s, -1e30)
    m = jnp.maximum(jnp.max(s, axis=-1, keepdims=True), sk)
    p = jnp.exp(s - m)
    es = jnp.exp(sk - m)
    den = jnp.sum(p, axis=-1, keepdims=True) + es
    return p / den, es / den


def _swa_specs(W, cb_q, cb_k):
    B = SW_BLOCK
    cur = lambda w, cb: pl.BlockSpec((B, w), lambda n: (n, cb))
    prv = lambda w, cb: pl.BlockSpec((B, w), lambda n: (jnp.maximum(n - 1, 0), cb))
    specs = [cur(W, cb_q), cur(LANES, cb_k), prv(LANES, cb_k), cur(LANES, cb_k + 1), prv(LANES, cb_k + 1)]
    return specs + [cur(LANES, 0)] * 3 + [prv(LANES, 0)] * 3 + [_full((1, LANES))]


def _swa_fwd(proj, tabs, sinks_row, W, cb_q, cb_k):
    S = proj.shape[0]
    G2 = SW_Q_HEADS // SW_KV_HEADS // 2

    def body(q_ref, kc_ref, kp_ref, vc_ref, vp_ref, c0, c1, c2, p0, p1, p2, sk_ref, o_ref):
        n = pl.program_id(0)
        qr, heads, ok, _ = _swa_setup(n, q_ref, kc_ref, kp_ref, vc_ref, vp_ref, (c0, c1, c2), (p0, p1, p2))
        for j in range(W // LANES):
            k_lo, k_hi, v_lo, v_hi = heads[j // G2]
            qj = qr[:, j * LANES:(j + 1) * LANES]
            p_lo, _ = _swa_probs(qj, k_lo, ok, sk_ref[0:1, 2 * j:2 * j + 1])
            p_hi, _ = _swa_probs(qj, k_hi, ok, sk_ref[0:1, 2 * j + 1:2 * j + 2])
            o_ref[:, j * LANES:(j + 1) * LANES] = (_bdot(p_lo, v_lo) + _bdot(p_hi, v_hi)).astype(BF16)

    t = tuple(tabs)
    return _pcall(body, name="swa_fwd", grid=(S // SW_BLOCK,), in_specs=_swa_specs(W, cb_q, cb_k),
                  out_specs=pl.BlockSpec((SW_BLOCK, W), lambda n: (n, 0)), out_shape=_sds((S, W), BF16),
                  compiler_params=_cp("arbitrary"))(proj, proj, proj, proj, proj, *t, *t, sinks_row)


def _swa_bwd(proj, tabs, sinks_row, do, W, cb_q, cb_k):
    S = proj.shape[0]
    B = SW_BLOCK
    G2 = SW_Q_HEADS // SW_KV_HEADS // 2
    SKR = -(-SW_Q_HEADS // SUBLANES) * SUBLANES

    def body(q_ref, kc_ref, kp_ref, vc_ref, vp_ref, c0, c1, c2, p0, p1, p2, sk_ref, do_ref,
             dq_ref, dkc_ref, dkp_ref, dvc_ref, dvp_ref, dsk_ref):
        n = pl.program_id(0)
        qr, heads, ok, lane = _swa_setup(n, q_ref, kc_ref, kp_ref, vc_ref, vp_ref, (c0, c1, c2), (p0, p1, p2))

        @pl.when(n == 0)
        def _():
            dsk_ref[...] = jnp.zeros((SKR, LANES), F32)

        acc_k = [jnp.zeros((2 * B, LANES), F32) for _ in range(SW_KV_HEADS)]
        acc_v = [jnp.zeros((2 * B, LANES), F32) for _ in range(SW_KV_HEADS)]
        dqs = []
        for j in range(W // LANES):
            hk = j // G2
            qj = qr[:, j * LANES:(j + 1) * LANES]
            dj = do_ref[:, j * LANES:(j + 1) * LANES]
            dqj = jnp.zeros((B, LANES), F32)
            for half in range(2):
                kh, vh = heads[hk][half], heads[hk][2 + half]
                h = 2 * j + half
                p, ps = _swa_probs(qj, kh, ok, sk_ref[0:1, h:h + 1])
                dp = _bdot(dj, vh, "nt")
                delta = jnp.sum(p * dp, axis=-1, keepdims=True)
                ds = p * (dp - delta) * (SW_HD ** -0.5)
                dqj = dqj + _bdot(ds, kh)
                sel = (lane < SW_HD) if half == 0 else (lane >= SW_HD)
                acc_k[hk] = acc_k[hk] + jnp.where(sel, _bdot(ds, qj, "tn"), 0.0)
                acc_v[hk] = acc_v[hk] + jnp.where(sel, _bdot(p, dj, "tn"), 0.0)
                dsk_ref[h:h + 1, :] += jnp.broadcast_to(-jnp.sum(ps * delta, axis=0, keepdims=True), (1, LANES))
            dqs.append(dqj)
        dq_ref[...] = _unrope(jnp.concatenate(dqs, axis=1), c0[...], c1[...], c2[...]).astype(BF16)
        fold = lambda a: a[:, :SW_HD] + a[:, SW_HD:]
        dkw = jnp.concatenate([fold(a) for a in acc_k], axis=1)
        dvw = jnp.concatenate([fold(a) for a in acc_v], axis=1)
        dkp_ref[...], dkc_ref[...] = dkw[:B], dkw[B:]
        dvp_ref[...], dvc_ref[...] = dvw[:B], dvw[B:]

    t = tuple(tabs)
    blk = lambda w: pl.BlockSpec((B, w), lambda n: (n, 0))
    o = _sds((S, LANES), F32)
    return _pcall(body, name="swa_bwd", grid=(S // B,), in_specs=_swa_specs(W, cb_q, cb_k) + [blk(W)],
                  out_specs=(blk(W), blk(LANES), blk(LANES), blk(LANES), blk(LANES), _full((SKR, LANES))),
                  out_shape=(_sds((S, W), BF16), o, o, o, o, _sds((SKR, LANES), F32)),
                  compiler_params=_cp("arbitrary"))(proj, proj, proj, proj, proj, *t, *t, sinks_row, do)


def _swa_kv_combine(dkc, dkp, dvc, dvp, tabs):
    S = dkc.shape[0]
    B = SW_BLOCK
    nb = S // B

    def fn(n, kc_ref, kp_ref, vc_ref, vp_ref, c0, c1, c2, o_ref):
        more = n < nb - 1
        dk = kc_ref[...] + jnp.where(more, kp_ref[...], 0.0)
        dv = vc_ref[...] + jnp.where(more, vp_ref[...], 0.0)
        o_ref[...] = jnp.concatenate([_unrope(dk, c0[...], c1[...], c2[...]), dv], axis=1).astype(BF16)

    cur = _rb(B, LANES)
    nxt = pl.BlockSpec((B, LANES), lambda n: (jnp.minimum(n + 1, nb - 1), 0))
    return _rows("swa_kv_combine", fn, S, B, (dkc, dkp, dvc, dvp, *tabs), [cur, nxt, cur, nxt, cur, cur, cur],
                 _sds((S, 2 * LANES), BF16), _rb(B, 2 * LANES))


ANY = pl.BlockSpec(memory_space=pl.ANY)


def _place():
    x, y, c = lax.axis_index("x"), lax.axis_index("y"), lax.axis_index("c")
    return x, y, c, [(1 - x, y), (x, 1 - y), (1 - x, 1 - y)]


def _comm_call(name, body, out_shape, n_sems, *ins):
    return _pcall(body, name=name, out_shape=out_shape, in_specs=[ANY] * len(ins), out_specs=ANY,
                  scratch_shapes=[pltpu.SemaphoreType.DMA((n_sems,)), pltpu.SemaphoreType.DMA((n_sems,)), pltpu.SemaphoreType.DMA])(*ins)


def _gather_chips(name, w):
    R, C = w.shape
    Rh = R // 2
    assert Rh % 16 == 0

    def body(w_ref, o_ref, send, recv, lsem):
        x, y, c, chips = _place()
        sib = (x, y, 1 - c)
        rows = pl.ds(pl.multiple_of(c * Rh, 16), Rh)

        def copy(k, chip, to, src):
            return pltpu.make_async_remote_copy(src_ref=src, dst_ref=o_ref.at[2 * chip[0] + chip[1], rows],
                                                send_sem=send.at[k], recv_sem=recv.at[k], device_id=to, device_id_type=MESH)

        mine = pltpu.make_async_copy(w_ref, o_ref.at[2 * x + y], lsem)
        mine.start()
        first = [copy(j, (x, y), (*chip, c), w_ref.at[rows]) for j, chip in enumerate(chips)]
        for cp in first:
            cp.start()
        passed = [copy(3 + j, chip, sib, o_ref.at[2 * chip[0] + chip[1], rows]) for j, chip in enumerate(chips)]
        for j, chip in enumerate(chips):
            copy(j, chip, (x, y, c), w_ref.at[rows]).wait_recv()
            passed[j].start()
        other = pl.ds(pl.multiple_of((1 - c) * Rh, 16), Rh)
        for j, chip in enumerate(chips):
            pltpu.make_async_remote_copy(src_ref=w_ref.at[other], dst_ref=o_ref.at[2 * chip[0] + chip[1], other],
                                         send_sem=send.at[3 + j], recv_sem=recv.at[3 + j], device_id=sib,
                                         device_id_type=MESH).wait_recv()
        for cp in first + passed:
            cp.wait_send()
        mine.wait()

    return _comm_call(name, body, _sds((4, R, C), w.dtype), 6, w)


def _pair_swap_half(name, g):
    _, R, C = g.shape
    Rh = R // 2

    def body(g_ref, o_ref, send, recv, lsem):
        x, y, c, _ = _place()
        other = pl.ds(pl.multiple_of((1 - c) * Rh, 16), Rh)
        cp = pltpu.make_async_remote_copy(src_ref=g_ref.at[:, other], dst_ref=o_ref, send_sem=send.at[0], recv_sem=recv.at[0],
                                          device_id=(x, y, 1 - c), device_id_type=MESH)
        cp.start()
        cp.wait()

    return _comm_call(name, body, _sds((4, Rh, C), g.dtype), 1, g)


def _scatter_chips(name, p):
    _, Rh, C = p.shape

    def body(p_ref, o_ref, send, recv, lsem):
        x, y, c, chips = _place()
        me = 2 * x + y
        mine = pltpu.make_async_copy(p_ref.at[me], o_ref.at[me], lsem)
        mine.start()
        cps = [pltpu.make_async_remote_copy(src_ref=p_ref.at[2 * chip[0] + chip[1]], dst_ref=o_ref.at[me], send_sem=send.at[j],
                                            recv_sem=recv.at[j], device_id=(*chip, c), device_id_type=MESH)
               for j, chip in enumerate(chips)]
        for cp in cps:
            cp.start()
        for j, chip in enumerate(chips):
            pltpu.make_async_remote_copy(src_ref=p_ref.at[me], dst_ref=o_ref.at[2 * chip[0] + chip[1]], send_sem=send.at[j],
                                         recv_sem=recv.at[j], device_id=(*chip, c), device_id_type=MESH).wait_recv()
        for cp in cps:
            cp.wait_send()
        mine.wait()

    return _comm_call(name, body, _sds(p.shape, p.dtype), 3, p)


def _pair_gather(name, h):
    Rh, C = h.shape

    def body(h_ref, o_ref, send, recv, lsem):
        x, y, c, _ = _place()
        rows = pl.ds(pl.multiple_of(c * Rh, 16), Rh)
        mine = pltpu.make_async_copy(h_ref, o_ref.at[rows], lsem)
        mine.start()
        cp = pltpu.make_async_remote_copy(src_ref=h_ref, dst_ref=o_ref.at[rows], send_sem=send.at[0], recv_sem=recv.at[0],
                                          device_id=(x, y, 1 - c), device_id_type=MESH)
        cp.start()
        cp.wait()
        mine.wait()

    return _comm_call(name, body, _sds((2 * Rh, C), h.dtype), 1, h)


def _gather_all(name, b):
    R, C = b.shape
    flips = [(dx, dy, dc) for dx in (0, 1) for dy in (0, 1) for dc in (0, 1)][1:]

    def body(b_ref, o_ref, send, recv, lsem):
        x, y, c, _ = _place()
        me = 4 * x + 2 * y + c
        peers = [(x ^ dx, y ^ dy, c ^ dc) for dx, dy, dc in flips]
        mine = pltpu.make_async_copy(b_ref, o_ref.at[me], lsem)
        mine.start()
        cps = [pltpu.make_async_remote_copy(src_ref=b_ref, dst_ref=o_ref.at[me], send_sem=send.at[k], recv_sem=recv.at[k],
                                            device_id=peer, device_id_type=MESH) for k, peer in enumerate(peers)]
        for cp in cps:
            cp.start()
        for k, (px, py, pc) in enumerate(peers):
            pltpu.make_async_remote_copy(src_ref=b_ref, dst_ref=o_ref.at[4 * px + 2 * py + pc], send_sem=send.at[k],
                                         recv_sem=recv.at[k], device_id=(px, py, pc), device_id_type=MESH).wait_recv()
        for cp in cps:
            cp.wait_send()
        mine.wait()

    return _comm_call(name, body, _sds((8, R, C), b.dtype), 7, b)


def _add_own_half(g, got):
    _, R, C = g.shape
    Rh = R // 2
    tq = _tile(Rh, 512, PACK_ROWS // 2)
    nb = Rh // tq

    def body(c_ref, g_ref, r_ref, o_ref):
        o_ref[...] = (g_ref[...] + r_ref[...]).astype(BF16)

    spec = pltpu.PrefetchScalarGridSpec(
        num_scalar_prefetch=1, grid=(4, nb),
        in_specs=[pl.BlockSpec((1, tq, C), lambda s, i, c_ref: (s, c_ref[0] * nb + i, 0)),
                  pl.BlockSpec((1, tq, C), lambda s, i, c_ref: (s, i, 0))],
        out_specs=pl.BlockSpec((1, tq, C), lambda s, i, c_ref: (s, i, 0)))
    return _pcall(body, name="add_own_half", grid_spec=spec, out_shape=_sds((4, Rh, C), BF16),
                  compiler_params=_cp("arbitrary", "arbitrary"))(lax.axis_index("c").reshape(1).astype(jnp.int32), g, got)


def _sum_slots(name, a):
    n, R, C = a.shape
    tq = _tile(R, 512, PACK_ROWS // 2)

    def fn(i, a_ref, o_ref):
        t = a_ref[0].astype(F32)
        for s in range(1, n):
            t = t + a_ref[s].astype(F32)
        o_ref[...] = t

    return _rows(name, fn, R, tq, (a,), [pl.BlockSpec((n, tq, C), lambda i: (0, i, 0))], _sds((R, C), F32), _rb(tq, C))


def _adamw(name, w, g, m, v):
    R, C = w.shape
    tq = _tile(R, 256, SUBLANES)

    def fn(i, w_ref, g_ref, m_ref, v_ref, d_ref, mo_ref, vo_ref):
        gg = g_ref[...]
        mn = ADAM_B1 * m_ref[...] + (1.0 - ADAM_B1) * gg
        vn = ADAM_B2 * v_ref[...] + (1.0 - ADAM_B2) * (gg * gg)
        mo_ref[...] = mn
        vo_ref[...] = vn
        m_hat = mn / (1.0 - ADAM_B1 ** ADAM_STEP)
        v_hat = vn / (1.0 - ADAM_B2 ** ADAM_STEP)
        d_ref[...] = -ADAM_LR * (m_hat / (jnp.sqrt(v_hat) + ADAM_EPS) + ADAM_WD * w_ref[...])

    r, o = _rb(tq, C), _sds((R, C), F32)
    return _rows(name, fn, R, tq, (w, g, m, v), [r, r, r, r], (o, o, o), (r, r, r))


def _pack(arrs, width, lead=()):
    nl = len(lead)
    flat = jnp.concatenate([a.reshape(lead + (-1,)) for a in arrs], axis=nl)
    n = flat.shape[-1]
    unit = PACK_ROWS * width
    tot = -(-n // unit) * unit
    flat = jnp.pad(flat, [(0, 0)] * nl + [(0, tot - n)])
    return flat.reshape(lead + (tot // width, width))


def _unpack(buf, shapes, lead=()):
    flat = buf.reshape(lead + (-1,))
    out, off = [], 0
    for s in shapes:
        n = int(np.prod(s))
        out.append(flat[..., off:off + n].reshape(lead + tuple(s)))
        off += n
    return out


def _relayout_in(w, W, H):
    o_sq = 4 * W + 2 * H
    o_k = o_sq + W
    o_g = o_k + 2 * KV_W
    pad = jnp.zeros((w.shape[0], BA_W - 2 * H), w.dtype)
    return jnp.concatenate([w[:, :4 * W], w[:, o_sq:o_k], w[:, o_g:o_g + 2 * W], w[:, o_k:o_g], w[:, 4 * W:o_sq], pad], axis=1)


def _unrelayout_in(d, W, H):
    o_kv = 7 * W
    o_ba = o_kv + 2 * KV_W
    return jnp.concatenate([d[:, :4 * W], d[:, o_ba:o_ba + 2 * H], d[:, 4 * W:5 * W], d[:, o_kv:o_ba], d[:, 5 * W:7 * W]], axis=1)


def _to_shards(g, axis):
    s = g.shape
    g = g.reshape(s[:axis] + (4, s[axis] // 4) + s[axis + 1:])
    return jnp.moveaxis(g, axis, 0)


def _from_shards(g, axis):
    g = jnp.moveaxis(g, 0, axis)
    s = g.shape
    return g.reshape(s[:axis] + (4 * s[axis + 1],) + s[axis + 2:])


def _lane_row(vals, at):
    return jnp.zeros((1, LANES), F32).at[0, at:at + vals.shape[0]].set(vals)


def _layer_fwd(x, lw, tabs, W, H):
    D = x.shape[1]
    cbk = 7 * W // LANES
    h = _pre_norm(x, lw["g1"])
    proj = _mm("mm_in", h, lw["win"], "nn", F32, tn=768)
    qkv = _dn_prep(proj, lw["conv"], W)
    beta_b, g_b = _dn_gates(proj, lw["alog"], lw["dt"], H, cbk + 2)
    o, st = _delta_fwd(qkv, beta_b, g_b, H, DELTA_CB)
    oa = _dn_out(o, proj, lw["ng"], W, 3)
    ob = _swa_fwd(proj, tabs, lw["sinks"], W, 4, cbk)
    ya = _mm("mm_up_dn", oa, lw["wup_dn"], "nn", F32)
    yb = _mm("mm_up_sw", ob, lw["wup_sw"], "nn", F32)
    mixin = _mix(proj, ya, yb, D, 5)
    mix = _mm("mm_o", mixin, lw["wo"], "nn", F32)
    x1, h2 = _post_mix(x, mix, lw["g2"], lw["g3"])
    f1 = _mm("mm_ff1", h2, lw["wff1"], "nn", F32)
    act = _relu2(f1)
    ff = _mm("mm_ff2", act, lw["wff2"], "nn", F32)
    x2 = _post_mlp(x1, ff, lw["g4"])
    saved = dict(x=x, h=h, proj=proj, qkv=qkv, beta_b=beta_b, g_b=g_b, o=o, st=st, oa=oa, ob=ob, ya=ya, yb=yb,
                 mixin=mixin, mix=mix, x1=x1, h2=h2, f1=f1, act=act, ff=ff)
    return x2, saved


def _layer_bwd(dx2, lw, sv, tabs, W, H):
    D = dx2.shape[1]
    cbk = 7 * W // LANES
    dff, dg4 = _post_mlp_bwd(sv["ff"], lw["g4"], dx2)
    dact = _mm("mm_ff2_dx", dff, lw["wff2"], "nt", F32)
    dwff2 = _mm("mm_ff2_dw", sv["act"], dff, "tn", F32)
    df1 = _relu2_bwd(dact, sv["f1"])
    dh2 = _mm("mm_ff1_dx", df1, lw["wff1"], "nt", F32)
    dwff1 = _mm("mm_ff1_dw", sv["h2"], df1, "tn", F32)
    dx1, dmix, dg3, dg2 = _mid_bwd(sv["x1"], lw["g3"], dh2, dx2, sv["mix"], lw["g2"])
    dmixin = _mm("mm_o_dx", dmix, lw["wo"], "nt", F32)
    dwo = _mm("mm_o_dw", sv["mixin"], dmix, "tn", F32)
    dya, dyb, dga, dgb = _mix_bwd(sv["proj"], sv["ya"], sv["yb"], dmixin, D, 5)
    doa = _mm("mm_up_dn_dx", dya, lw["wup_dn"], "nt", F32)
    dwup_dn = _mm("mm_up_dn_dw", sv["oa"], dya, "tn", F32)
    dob = _mm("mm_up_sw_dx", dyb, lw["wup_sw"], "nt", F32)
    dwup_sw = _mm("mm_up_sw_dw", sv["ob"], dyb, "tn", F32)
    do, dz, dng = _dn_out_bwd(sv["o"], sv["proj"], lw["ng"], doa, W, 3)
    dqkvn, dbeta_b, dg_b = _delta_bwd(sv["qkv"], sv["beta_b"], sv["g_b"], sv["st"], do, H, DELTA_CB)
    dba, dalog, ddt = _dn_gates_bwd(sv["proj"], lw["alog"], lw["dt"], dbeta_b, dg_b, H, cbk + 2)
    dc, dconv = _dn_prep_bwd_a(sv["proj"], lw["conv"], dqkvn, W)
    dqkv = _dn_prep_bwd_b(dc, lw["conv"], W)
    dq_sw, dkc, dkp, dvc, dvp, dsk = _swa_bwd(sv["proj"], tabs, lw["sinks"], dob, W, 4, cbk)
    dkv = _swa_kv_combine(dkc, dkp, dvc, dvp, tabs)
    dproj = jnp.concatenate([dqkv, dz, dq_sw, dga, dgb, dkv, dba], axis=1)
    dh = _mm("mm_in_dx", dproj, lw["win"], "nt", F32, tk=768)
    dwin = _mm("mm_in_dw", sv["h"], dproj, "tn", F32, tn=768)
    dx, dg1 = _pre_norm_bwd(sv["x"], lw["g1"], dh, dx1)
    grads = dict(pre_mix_g=dg1[0], w_in=_unrelayout_in(dwin, W, H), dn_conv_w=dconv, dn_a_log=dalog[0, H:2 * H],
                 dn_dt_bias=ddt[0, H:2 * H], dn_norm_g=dng[0], sw_sinks=dsk[:SW_Q_HEADS, 0], w_up_dn=dwup_dn, w_up_sw=dwup_sw,
                 w_o=dwo, post_mix_g=dg2[0], pre_mlp_g=dg3[0], w_ff1=dwff1, w_ff2=dwff2, post_mlp_g=dg4[0])
    return dx, grads


_WEIGHTS = ["pre_mix_g", "w_in", "dn_conv_w", "dn_a_log", "dn_dt_bias", "dn_norm_g", "sw_sinks", "w_up_dn", "w_up_sw", "w_o",
            "post_mix_g", "pre_mlp_g", "w_ff1", "w_ff2", "post_mlp_g"]
_BIG = {"w_in": 2, "w_up_dn": 1, "w_up_sw": 1, "w_o": 1, "w_ff1": 2, "w_ff2": 1}
_SMALL = [n for n in _WEIGHTS if n not in _BIG]


def _step(P):
    x, target = P["x"][0], P["loss_target"][0]
    S, D = x.shape
    L = P["pre_mix_g"].shape[0]
    H, W = DN_HEADS, DN_HEADS * DN_DK
    assert W == D == SW_Q_HEADS * SW_HD and KV_W == LANES
    me = 2 * lax.axis_index("x") + lax.axis_index("y")

    big_shapes = [P[n].shape for n in _BIG]
    wall = _gather_chips("weights_gather", _pack([P[n].astype(BF16) for n in _BIG], PACK_W))
    full = {n: _from_shards(a, ax) for (n, ax), a in zip(_BIG.items(), _unpack(wall, big_shapes, lead=(4,)))}
    call = _gather_chips("conv_gather", _pack([P["dn_conv_w"]], LANES))
    conv_full = _from_shards(_unpack(call, [P["dn_conv_w"].shape], lead=(4,))[0], 2)

    tabs = _rope_tables(P["positions"].reshape(S, 1))
    lws = []
    for l in range(L):
        lws.append(dict(
            g1=P["pre_mix_g"][l][None], win=_relayout_in(full["w_in"][l], W, H), conv=conv_full[l],
            alog=_lane_row(P["dn_a_log"][l], H), dt=_lane_row(P["dn_dt_bias"][l], H), ng=P["dn_norm_g"][l][None],
            sinks=_lane_row(P["sw_sinks"][l], 0), wup_dn=full["w_up_dn"][l], wup_sw=full["w_up_sw"][l], wo=full["w_o"][l],
            g2=P["post_mix_g"][l][None], g3=P["pre_mlp_g"][l][None], wff1=full["w_ff1"][l], wff2=full["w_ff2"][l],
            g4=P["post_mlp_g"][l][None]))

    saved = []
    for l in range(L):
        x, sv = _layer_fwd(x, lws[l], tabs, W, H)
        saved.append(sv)
    loss_row, dx = _loss_head(x, target)
    layer_grads = [None] * L
    for l in reversed(range(L)):
        dx, layer_grads[l] = _layer_bwd(dx, lws[l], saved[l], tabs, W, H)
    grads = {n: jnp.stack([layer_grads[l][n] for l in range(L)]) for n in _WEIGHTS}

    g = _pack([_to_shards(grads[n], ax) for n, ax in _BIG.items()], PACK_W, lead=(4,))
    part = _add_own_half(g, _pair_swap_half("grad_pair_swap", g))
    half = _sum_slots("grad_chip_sum", _scatter_chips("grad_chip_scatter", part))
    red = _pair_gather("grad_pair_gather", half)
    gsum = dict(zip(_BIG, _unpack(red, big_shapes)))
    small_shapes = [(1,)] + [grads[n].shape for n in _SMALL]
    tot = _sum_slots("small_sum", _gather_all("small_gather", _pack([loss_row[0, :1]] + [grads[n] for n in _SMALL], LANES)))
    small = _unpack(tot, small_shapes)
    loss = small[0][0]
    gsum.update(zip(_SMALL, small[1:]))
    cw = P["dn_conv_w"].shape[2]
    gsum["dn_conv_w"] = lax.dynamic_slice_in_dim(gsum["dn_conv_w"], me * cw, cw, axis=2)

    delta, new_m, new_v = {}, {}, {}
    for n in _BIG:
        s = P[n].shape
        two_d = lambda a: a.reshape(s[0] * s[1], s[2])
        outs = _adamw("adamw_" + n, two_d(P[n]), two_d(gsum[n]), two_d(P["m_" + n]), two_d(P["v_" + n]))
        delta[n], new_m[n], new_v[n] = (o.reshape(s) for o in outs)
    sm_shapes = [P[n].shape for n in _SMALL]
    outs = _adamw("adamw_small", *(_pack([src[pre + n] for n in _SMALL], LANES)
                                   for src, pre in ((P, ""), (gsum, ""), (P, "m_"), (P, "v_"))))
    for d, o in zip((delta, new_m, new_v), outs):
        d.update(zip(_SMALL, _unpack(o, sm_shapes)))

    return (loss, dx[None], *[gsum[n] for n in _WEIGHTS], *[delta[n] for n in _WEIGHTS],
            *[new_m[n] for n in _WEIGHTS], *[new_v[n] for n in _WEIGHTS])


def kernel(x, positions, pre_mix_g, w_in, dn_conv_w, dn_a_log, dn_dt_bias, dn_norm_g, sw_sinks, w_up_dn, w_up_sw, w_o, post_mix_g, pre_mlp_g, w_ff1, w_ff2, post_mlp_g, loss_target, m_pre_mix_g, m_w_in, m_dn_conv_w, m_dn_a_log, m_dn_dt_bias, m_dn_norm_g, m_sw_sinks, m_w_up_dn, m_w_up_sw, m_w_o, m_post_mix_g, m_pre_mlp_g, m_w_ff1, m_w_ff2, m_post_mlp_g, v_pre_mix_g, v_w_in, v_dn_conv_w, v_dn_a_log, v_dn_dt_bias, v_dn_norm_g, v_sw_sinks, v_w_up_dn, v_w_up_sw, v_w_o, v_post_mix_g, v_pre_mlp_g, v_w_ff1, v_w_ff2, v_post_mlp_g):
    vals = (x, positions, pre_mix_g, w_in, dn_conv_w, dn_a_log, dn_dt_bias, dn_norm_g, sw_sinks, w_up_dn, w_up_sw, w_o, post_mix_g, pre_mlp_g, w_ff1, w_ff2, post_mlp_g, loss_target, m_pre_mix_g, m_w_in, m_dn_conv_w, m_dn_a_log, m_dn_dt_bias, m_dn_norm_g, m_sw_sinks, m_w_up_dn, m_w_up_sw, m_w_o, m_post_mix_g, m_pre_mlp_g, m_w_ff1, m_w_ff2, m_post_mlp_g, v_pre_mix_g, v_w_in, v_dn_conv_w, v_dn_a_log, v_dn_dt_bias, v_dn_norm_g, v_sw_sinks, v_w_up_dn, v_w_up_sw, v_w_o, v_post_mix_g, v_pre_mlp_g, v_w_ff1, v_w_ff2, v_post_mlp_g)
    names = ["x", "positions"] + _WEIGHTS + ["loss_target"] + ["m_" + n for n in _WEIGHTS] + ["v_" + n for n in _WEIGHTS]
    return _step(dict(zip(names, vals)))
```

```python
import functools

import numpy as np
import jax
import jax.numpy as jnp
from jax import lax
from jax.experimental import pallas as pl
from jax.experimental.pallas import tpu as pltpu

F32, BF16 = jnp.float32, jnp.bfloat16
MESH = pl.DeviceIdType.MESH

DN_HEADS = 8
DN_DK = 128
DN_CONV = 4
DN_CHUNK = 64
SW_Q_HEADS = 16
SW_KV_HEADS = 2
SW_HD = 64
SW_BLOCK = 128
ROPE_THETA = 500000.0
ROT_DIM = SW_HD // 4
EPS = 1e-6
ADAM_LR, ADAM_B1, ADAM_B2, ADAM_EPS, ADAM_WD, ADAM_STEP = 0.001, 0.9, 0.999, 1e-08, 0.01, 10

LANES = 128
SUBLANES = 8
VMEM_LIMIT = 48 * 1024 * 1024
KV_W = SW_KV_HEADS * SW_HD
BA_W = 256
PACK_ROWS = 512
DELTA_CB = 4
DELTA_HB = 2


def _pcall(body, **kw):
    return pl.pallas_call(body, **kw)


def _cp(*sem):
    return pltpu.CompilerParams(dimension_semantics=sem, vmem_limit_bytes=VMEM_LIMIT)


def _tile(n, pref, unit=LANES):
    if n <= pref:
        return n
    t = (pref // unit) * unit
    while t > unit and n % t:
        t -= unit
    assert n % t == 0, (n, pref)
    return t


def _sds(shape, dtype):
    return jax.ShapeDtypeStruct(tuple(shape), dtype)


_DIMS = {"nn": ((1,), (0,)), "nt": ((1,), (1,)), "tn": ((0,), (0,))}


def _mm(name, a, b, mode, out_dtype, tm=1024, tn=1024, tk=1024):
    if mode == "nn":
        (M, K), (_, N) = a.shape, b.shape
    elif mode == "nt":
        (M, K), (N, _) = a.shape, b.shape
    else:
        (K, M), (_, N) = a.shape, b.shape
    tm, tn, tk = _tile(M, tm), _tile(N, tn), _tile(K, tk)
    nk = K // tk
    a_spec = {"nn": pl.BlockSpec((tm, tk), lambda i, j, k: (i, k)),
              "nt": pl.BlockSpec((tm, tk), lambda i, j, k: (i, k)),
              "tn": pl.BlockSpec((tk, tm), lambda i, j, k: (k, i))}[mode]
    b_spec = {"nn": pl.BlockSpec((tk, tn), lambda i, j, k: (k, j)),
              "nt": pl.BlockSpec((tn, tk), lambda i, j, k: (j, k)),
              "tn": pl.BlockSpec((tk, tn), lambda i, j, k: (k, j))}[mode]
    dims = (_DIMS[mode], ((), ()))

    def body(a_ref, b_ref, o_ref, acc_ref):
        k = pl.program_id(2)
        part = lax.dot_general(a_ref[...], b_ref[...], dims, preferred_element_type=F32)

        @pl.when(k == 0)
        def _():
            acc_ref[...] = part

        @pl.when(k > 0)
        def _():
            acc_ref[...] += part

        @pl.when(k == nk - 1)
        def _():
            o_ref[...] = acc_ref[...].astype(out_dtype)

    return _pcall(body, name=name, grid=(M // tm, N // tn, nk), in_specs=[a_spec, b_spec],
                  out_specs=pl.BlockSpec((tm, tn), lambda i, j, k: (i, j)), out_shape=_sds((M, N), out_dtype),
                  scratch_shapes=[pltpu.VMEM((tm, tn), F32)],
                  compiler_params=_cp("parallel", "parallel", "arbitrary"))(a, b)


def _rows(name, fn, n_rows, tq, ins, in_specs, out_shapes, out_specs):
    def body(*refs):
        fn(pl.program_id(0), *refs)

    return _pcall(body, name=name, grid=(n_rows // tq,), in_specs=in_specs, out_specs=out_specs,
                  out_shape=out_shapes, compiler_params=_cp("arbitrary"))(*ins)


def _rb(tq, w, cb=0):
    return pl.BlockSpec((tq, w), lambda i: (i, cb))


def _full(shape):
    return pl.BlockSpec(tuple(shape), lambda *_: (0,) * len(shape))


def _rms_fwd(x, g):
    r = lax.rsqrt(jnp.mean(x * x, axis=-1, keepdims=True) + EPS)
    return x * r * g


def _rms_bwd(x, g, dy):
    r = lax.rsqrt(jnp.mean(x * x, axis=-1, keepdims=True) + EPS)
    xh = x * r
    t = dy * g
    dx = r * (t - xh * jnp.mean(t * xh, axis=-1, keepdims=True))
    return dx, jnp.sum(dy * xh, axis=0, keepdims=True)


def _acc(i, ref, val):
    @pl.when(i == 0)
    def _():
        ref[...] = val

    @pl.when(i > 0)
    def _():
        ref[...] += val


def _sigmoid(x):
    return 1.0 / (1.0 + jnp.exp(-x))


def _pre_norm(x, g):
    S, D = x.shape
    tq = _tile(S, 512, SUBLANES)

    def fn(i, x_ref, g_ref, h_ref):
        h_ref[...] = _rms_fwd(x_ref[...], g_ref[...]).astype(BF16)

    return _rows("pre_norm", fn, S, tq, (x, g), [_rb(tq, D), _full((1, D))], _sds((S, D), BF16), _rb(tq, D))


def _post_mix(x, mix, g2, g3):
    S, D = x.shape
    tq = _tile(S, 512, SUBLANES)

    def fn(i, x_ref, m_ref, g2_ref, g3_ref, x1_ref, h2_ref):
        x1 = x_ref[...] + _rms_fwd(m_ref[...], g2_ref[...])
        x1_ref[...] = x1
        h2_ref[...] = _rms_fwd(x1, g3_ref[...]).astype(BF16)

    return _rows("post_mix", fn, S, tq, (x, mix, g2, g3), [_rb(tq, D), _rb(tq, D), _full((1, D)), _full((1, D))],
                 (_sds((S, D), F32), _sds((S, D), BF16)), (_rb(tq, D), _rb(tq, D)))


def _post_mlp(x1, ff, g4):
    S, D = x1.shape
    tq = _tile(S, 512, SUBLANES)

    def fn(i, x_ref, f_ref, g_ref, o_ref):
        o_ref[...] = x_ref[...] + _rms_fwd(f_ref[...], g_ref[...])

    return _rows("post_mlp", fn, S, tq, (x1, ff, g4), [_rb(tq, D), _rb(tq, D), _full((1, D))], _sds((S, D), F32), _rb(tq, D))


def _relu2(f1):
    S, F = f1.shape
    tq = _tile(S, 256, SUBLANES)

    def fn(i, f_ref, o_ref):
        r = jnp.maximum(f_ref[...], 0.0)
        o_ref[...] = (r * r).astype(BF16)

    return _rows("relu2", fn, S, tq, (f1,), [_rb(tq, F)], _sds((S, F), BF16), _rb(tq, F))


def _relu2_bwd(dact, f1):
    S, F = f1.shape
    tq = _tile(S, 256, SUBLANES)

    def fn(i, d_ref, f_ref, o_ref):
        o_ref[...] = (d_ref[...] * 2.0 * jnp.maximum(f_ref[...], 0.0)).astype(BF16)

    return _rows("relu2_bwd", fn, S, tq, (dact, f1), [_rb(tq, F), _rb(tq, F)], _sds((S, F), BF16), _rb(tq, F))


def _loss_head(y, target):
    S, D = y.shape
    tq = _tile(S, 512, SUBLANES)

    def fn(i, y_ref, t_ref, l_ref, d_ref):
        e = y_ref[...] - t_ref[...]
        d_ref[...] = e * (1.0 / D)
        part = jnp.sum(jnp.sum(e * e, axis=1, keepdims=True), axis=0, keepdims=True) * (0.5 / D)
        _acc(i, l_ref, jnp.broadcast_to(part, (1, LANES)))

    return _rows("loss_head", fn, S, tq, (y, target), [_rb(tq, D), _rb(tq, D)],
                 (_sds((1, LANES), F32), _sds((S, D), F32)), (_full((1, LANES)), _rb(tq, D)))


def _post_mlp_bwd(ff, g4, dx2):
    S, D = ff.shape
    tq = _tile(S, 512, SUBLANES)

    def fn(i, f_ref, g_ref, d_ref, o_ref, dg_ref):
        dx, dg = _rms_bwd(f_ref[...], g_ref[...], d_ref[...])
        o_ref[...] = dx.astype(BF16)
        _acc(i, dg_ref, dg)

    return _rows("post_mlp_bwd", fn, S, tq, (ff, g4, dx2), [_rb(tq, D), _full((1, D)), _rb(tq, D)],
                 (_sds((S, D), BF16), _sds((1, D), F32)), (_rb(tq, D), _full((1, D))))


def _mid_bwd(x1, g3, dh2, dx2, mix, g2):
    S, D = x1.shape
    tq = _tile(S, 256, SUBLANES)

    def fn(i, x_ref, g3_ref, dh_ref, dx2_ref, m_ref, g2_ref, dx1_ref, dm_ref, dg3_ref, dg2_ref):
        d, dg3 = _rms_bwd(x_ref[...], g3_ref[...], dh_ref[...])
        dx1 = dx2_ref[...] + d
        dx1_ref[...] = dx1
        dm, dg2 = _rms_bwd(m_ref[...], g2_ref[...], dx1)
        dm_ref[...] = dm.astype(BF16)
        _acc(i, dg3_ref, dg3)
        _acc(i, dg2_ref, dg2)

    r, f = _rb(tq, D), _full((1, D))
    return _rows("mid_bwd", fn, S, tq, (x1, g3, dh2, dx2, mix, g2), [r, f, r, r, r, f],
                 (_sds((S, D), F32), _sds((S, D), BF16), _sds((1, D), F32), _sds((1, D), F32)), (r, r, f, f))


def _pre_norm_bwd(x, g1, dh, dx1):
    S, D = x.shape
    tq = _tile(S, 512, SUBLANES)

    def fn(i, x_ref, g_ref, dh_ref, dx1_ref, dx_ref, dg_ref):
        d, dg = _rms_bwd(x_ref[...], g_ref[...], dh_ref[...])
        dx_ref[...] = dx1_ref[...] + d
        _acc(i, dg_ref, dg)

    r, f = _rb(tq, D), _full((1, D))
    return _rows("pre_norm_bwd", fn, S, tq, (x, g1, dh, dx1), [r, f, r, r], (_sds((S, D), F32), _sds((1, D), F32)), (r, f))


def _mix(proj, ya, yb, D, cb_a):
    S = ya.shape[0]
    tq = _tile(S, 256, SUBLANES)

    def fn(i, ga_ref, gb_ref, ya_ref, yb_ref, o_ref):
        o_ref[...] = (_sigmoid(ga_ref[...]) * ya_ref[...] + _sigmoid(gb_ref[...]) * yb_ref[...]).astype(BF16)

    return _rows("mix", fn, S, tq, (proj, proj, ya, yb), [_rb(tq, D, cb_a), _rb(tq, D, cb_a + 1), _rb(tq, D), _rb(tq, D)],
                 _sds((S, D), BF16), _rb(tq, D))


def _mix_bwd(proj, ya, yb, dmixin, D, cb_a):
    S = ya.shape[0]
    tq = _tile(S, 256, SUBLANES)

    def fn(i, ga_ref, gb_ref, ya_ref, yb_ref, d_ref, dya_ref, dyb_ref, dga_ref, dgb_ref):
        d = d_ref[...]
        sa, sb = _sigmoid(ga_ref[...]), _sigmoid(gb_ref[...])
        dya_ref[...] = (d * sa).astype(BF16)
        dyb_ref[...] = (d * sb).astype(BF16)
        dga_ref[...] = (d * ya_ref[...] * sa * (1.0 - sa)).astype(BF16)
        dgb_ref[...] = (d * yb_ref[...] * sb * (1.0 - sb)).astype(BF16)

    r = _rb(tq, D)
    o = _sds((S, D), BF16)
    return _rows("mix_bwd", fn, S, tq, (proj, proj, ya, yb, dmixin), [_rb(tq, D, cb_a), _rb(tq, D, cb_a + 1), r, r, r],
                 (o, o, o, o), (r, r, r, r))


def _shift_down(xe, k, tq):
    return pltpu.roll(xe, k, 0)[SUBLANES:SUBLANES + tq]


def _conv_pre(cur_ref, halo_ref, w_ref, i, tq):
    x = cur_ref[...]
    halo = jnp.where(i > 0, halo_ref[...], 0.0)
    xe = jnp.concatenate([halo, x], axis=0)
    xs = [x] + [_shift_down(xe, k, tq) for k in range(1, DN_CONV)]
    w = w_ref[...]
    c = sum(w[DN_CONV - 1 - k:DN_CONV - k, :] * xs[k] for k in range(DN_CONV))
    return c, xs


def _dn_prep(proj, conv_w, W):
    S = proj.shape[0]
    tq = _tile(S, 256, SUBLANES)
    hb = tq // SUBLANES

    def body(cur_ref, halo_ref, w_ref, o_ref):
        j, i = pl.program_id(0), pl.program_id(1)
        c, _ = _conv_pre(cur_ref, halo_ref, w_ref, i, tq)
        y = c * _sigmoid(c)
        scale = jnp.where(j == 0, DN_DK ** -0.5, 1.0)
        for h in range(W // DN_DK):
            sl = slice(h * DN_DK, (h + 1) * DN_DK)
            yh = y[:, sl]
            rs = lax.rsqrt(jnp.sum(yh * yh, axis=-1, keepdims=True) + EPS)
            o_ref[:, sl] = jnp.where(j == 2, yh, yh * rs * scale)

    return _pcall(body, name="dn_prep", grid=(3, S // tq),
                  in_specs=[pl.BlockSpec((tq, W), lambda j, i: (i, j)),
                            pl.BlockSpec((SUBLANES, W), lambda j, i: (jnp.maximum(i * hb - 1, 0), j)),
                            pl.BlockSpec((DN_CONV, W), lambda j, i: (0, j))],
                  out_specs=pl.BlockSpec((tq, W), lambda j, i: (i, j)), out_shape=_sds((S, 3 * W), F32),
                  compiler_params=_cp("arbitrary", "arbitrary"))(proj, proj, conv_w)


def _dn_prep_bwd_a(proj, conv_w, dqkv, W):
    S = proj.shape[0]
    tq = _tile(S, 256, SUBLANES)
    hb = tq // SUBLANES

    def body(cur_ref, halo_ref, w_ref, d_ref, dc_ref, dw_ref):
        j, i = pl.program_id(0), pl.program_id(1)
        c, xs = _conv_pre(cur_ref, halo_ref, w_ref, i, tq)
        sg = _sigmoid(c)
        y = c * sg
        scale = jnp.where(j == 0, DN_DK ** -0.5, 1.0)
        dout = d_ref[0]
        dys = []
        for h in range(W // DN_DK):
            sl = slice(h * DN_DK, (h + 1) * DN_DK)
            yh, dh = y[:, sl], dout[:, sl]
            rs = lax.rsqrt(jnp.sum(yh * yh, axis=-1, keepdims=True) + EPS)
            yn = yh * rs
            dn = scale * rs * (dh - yn * jnp.sum(dh * yn, axis=-1, keepdims=True))
            dys.append(jnp.where(j == 2, dh, dn))
        dy = jnp.concatenate(dys, axis=1)
        dc = dy * (sg * (1.0 + c * (1.0 - sg)))
        dc_ref[...] = dc
        dw = jnp.concatenate([jnp.sum(dc * xs[DN_CONV - 1 - r], axis=0, keepdims=True) for r in range(DN_CONV)], axis=0)
        _acc(i, dw_ref, dw)

    return _pcall(body, name="dn_prep_bwd_a", grid=(3, S // tq),
                  in_specs=[pl.BlockSpec((tq, W), lambda j, i: (i, j)),
                            pl.BlockSpec((SUBLANES, W), lambda j, i: (jnp.maximum(i * hb - 1, 0), j)),
                            pl.BlockSpec((DN_CONV, W), lambda j, i: (0, j)),
                            pl.BlockSpec((1, tq, W), lambda j, i: (j, i, 0))],
                  out_specs=(pl.BlockSpec((tq, W), lambda j, i: (i, j)), pl.BlockSpec((DN_CONV, W), lambda j, i: (0, j))),
                  out_shape=(_sds((S, 3 * W), F32), _sds((DN_CONV, 3 * W), F32)),
                  compiler_params=_cp("arbitrary", "arbitrary"))(proj, proj, conv_w, dqkv)


def _dn_prep_bwd_b(dc, conv_w, W):
    S = dc.shape[0]
    tq = _tile(S, 256, SUBLANES)
    hb = tq // SUBLANES
    nblk = S // tq

    def body(cur_ref, nxt_ref, w_ref, o_ref):
        i = pl.program_id(1)
        d = cur_ref[...]
        nxt = jnp.where(i < nblk - 1, nxt_ref[...], 0.0)
        de = jnp.concatenate([d, nxt], axis=0)
        w = w_ref[...]
        out = w[DN_CONV - 1:DN_CONV, :] * d
        for k in range(1, DN_CONV):
            out = out + w[DN_CONV - 1 - k:DN_CONV - k, :] * pltpu.roll(de, tq + SUBLANES - k, 0)[0:tq]
        o_ref[...] = out.astype(BF16)

    return _pcall(body, name="dn_prep_bwd_b", grid=(3, nblk),
                  in_specs=[pl.BlockSpec((tq, W), lambda j, i: (i, j)),
                            pl.BlockSpec((SUBLANES, W), lambda j, i: (jnp.minimum((i + 1) * hb, S // SUBLANES - 1), j)),
                            pl.BlockSpec((DN_CONV, W), lambda j, i: (0, j))],
                  out_specs=pl.BlockSpec((tq, W), lambda j, i: (i, j)), out_shape=_sds((S, 3 * W), BF16),
                  compiler_params=_cp("arbitrary", "arbitrary"))(dc, dc, conv_w)


def _gate_terms(ba, al, dt):
    u = ba + dt
    sp = jnp.maximum(u, 0.0) + jnp.log(1.0 + jnp.exp(-jnp.abs(u)))
    return _sigmoid(ba), -jnp.exp(al) * sp, u


def _dn_gates(proj, alog_row, dt_row, H, cb_ba):
    S = proj.shape[0]
    tq = _tile(S, 512, SUBLANES)
    W = H * DN_DK

    def fn(i, ba_ref, al_ref, dt_ref, be_ref, g_ref):
        bet, gg, _ = _gate_terms(ba_ref[...], al_ref[...], dt_ref[...])
        for h in range(H):
            sl = slice(h * DN_DK, (h + 1) * DN_DK)
            be_ref[:, sl] = jnp.broadcast_to(bet[:, h:h + 1], (tq, DN_DK))
            g_ref[:, sl] = jnp.broadcast_to(gg[:, H + h:H + h + 1], (tq, DN_DK))

    return _rows("dn_gates", fn, S, tq, (proj, alog_row, dt_row), [_rb(tq, LANES, cb_ba), _full((1, LANES)), _full((1, LANES))],
                 (_sds((S, W), F32), _sds((S, W), F32)), (_rb(tq, W), _rb(tq, W)))


def _dn_gates_bwd(proj, alog_row, dt_row, dbeta_b, dg_b, H, cb_ba):
    S = proj.shape[0]
    tq = _tile(S, 512, SUBLANES)
    W = H * DN_DK

    def fn(i, ba_ref, al_ref, dt_ref, db_ref, dg_ref, o_ref, dal_ref, ddt_ref):
        bet, gg, u = _gate_terms(ba_ref[...], al_ref[...], dt_ref[...])
        lane = lax.broadcasted_iota(jnp.int32, (tq, LANES), 1)
        d = jnp.zeros((tq, LANES), F32)
        for h in range(H):
            d = jnp.where(lane == h, db_ref[:, h * DN_DK:h * DN_DK + 1], d)
            d = jnp.where(lane == H + h, dg_ref[:, h * DN_DK:h * DN_DK + 1], d)
        is_a = (lane >= H) & (lane < 2 * H)
        da = jnp.where(is_a, d * (-jnp.exp(al_ref[...]) * _sigmoid(u)), 0.0)
        dlog = jnp.where(lane < H, d * bet * (1.0 - bet), da)
        o_ref[...] = jnp.concatenate([dlog, jnp.zeros((tq, BA_W - LANES), F32)], axis=1).astype(BF16)
        _acc(i, dal_ref, jnp.sum(jnp.where(is_a, d * gg, 0.0), axis=0, keepdims=True))
        _acc(i, ddt_ref, jnp.sum(da, axis=0, keepdims=True))

    f = _full((1, LANES))
    return _rows("dn_gates_bwd", fn, S, tq, (proj, alog_row, dt_row, dbeta_b, dg_b),
                 [_rb(tq, LANES, cb_ba), f, f, _rb(tq, W), _rb(tq, W)],
                 (_sds((S, BA_W), BF16), _sds((1, LANES), F32), _sds((1, LANES), F32)), (_rb(tq, BA_W), f, f))


def _dn_out(o, proj, ng, W, cb_z):
    S = o.shape[0]
    tq = _tile(S, 256, SUBLANES)

    def fn(i, o_ref, z_ref, g_ref, y_ref):
        for h in range(W // DN_DK):
            sl = slice(h * DN_DK, (h + 1) * DN_DK)
            z = z_ref[:, sl]
            y_ref[:, sl] = (_rms_fwd(o_ref[:, sl], g_ref[...]) * (z * _sigmoid(z))).astype(BF16)

    return _rows("dn_out", fn, S, tq, (o, proj, ng), [_rb(tq, W), _rb(tq, W, cb_z), _full((1, DN_DK))], _sds((S, W), BF16), _rb(tq, W))


def _dn_out_bwd(o, proj, ng, dy, W, cb_z):
    S = o.shape[0]
    tq = _tile(S, 256, SUBLANES)

    def fn(i, o_ref, z_ref, g_ref, d_ref, do_ref, dz_ref, dg_ref):
        g = g_ref[...]
        dg = jnp.zeros((1, DN_DK), F32)
        for h in range(W // DN_DK):
            sl = slice(h * DN_DK, (h + 1) * DN_DK)
            oh, z, d = o_ref[:, sl], z_ref[:, sl], d_ref[:, sl]
            sg = _sigmoid(z)
            dn = d * (z * sg)
            dz_ref[:, sl] = (d * _rms_fwd(oh, g) * (sg * (1.0 + z * (1.0 - sg)))).astype(BF16)
            dx, dgh = _rms_bwd(oh, g, dn)
            do_ref[:, sl] = dx
            dg = dg + dgh
        _acc(i, dg_ref, dg)

    r = _rb(tq, W)
    return _rows("dn_out_bwd", fn, S, tq, (o, proj, ng, dy), [r, _rb(tq, W, cb_z), _full((1, DN_DK)), r],
                 (_sds((S, W), F32), _sds((S, W), BF16), _sds((1, DN_DK), F32)), (r, r, _full((1, DN_DK))))


def _bdot(a, b, mode="nn"):
    return lax.dot_general(a.astype(BF16), b.astype(BF16), (_DIMS[mode], ((), ())), preferred_element_type=F32)


def _rsum(x):
    return jnp.broadcast_to(jnp.sum(x, axis=-1, keepdims=True), x.shape)


def _dot3(a, b, mode="nn"):
    ah, bh = a.astype(BF16), b.astype(BF16)
    al, bl = (a - ah.astype(F32)).astype(BF16), (b - bh.astype(F32)).astype(BF16)
    d = lambda x, y: lax.dot_general(x, y, (_DIMS[mode], ((), ())), preferred_element_type=F32)
    return d(ah, bh) + (d(al, bh) + d(ah, bl))


def _cumsum_rows(x, reverse=False):
    n = x.shape[0]
    row = lax.broadcasted_iota(jnp.int32, x.shape, 0)
    s = 1
    while s < n:
        if reverse:
            x = x + jnp.where(row < n - s, pltpu.roll(x, n - s, 0), 0.0)
        else:
            x = x + jnp.where(row >= s, pltpu.roll(x, s, 0), 0.0)
        s *= 2
    return x


def _each(f, *lists):
    return [f(*a) for a in zip(*lists)]


def _delta_local(qs, ks, vs, bes, grs):
    C = DN_CHUNK
    ri = lax.broadcasted_iota(jnp.int32, (C, C), 0)
    ci = lax.broadcasted_iota(jnp.int32, (C, C), 1)
    causal, strict = ri >= ci, ri > ci
    gcs = [_cumsum_rows(g) for g in grs]
    decays = [jnp.where(causal, jnp.exp(jnp.where(causal, gc[:, :C] - gc.T[:C, :], 0.0)), 0.0) for gc in gcs]
    egs = [jnp.exp(gc) for gc in gcs]
    eks = [jnp.exp(gc[C - 1:C, :] - gc) for gc in gcs]
    gams = [jnp.exp(gc[C - 1:C, :]) for gc in gcs]
    kbs = _each(lambda k, be: k * be, ks, bes)
    kks = _each(lambda kb, k: _dot3(kb, k, "nt"), kbs, ks)
    nls = _each(lambda kk, dc: jnp.where(strict, -kk * dc, 0.0), kks, decays)
    eye = (ri == ci).astype(F32)
    ts = [eye + nl for nl in nls]
    pws = [_dot3(nl, nl) for nl in nls]
    for s in range(5):
        ups = _each(_dot3, ts, pws)
        ts = _each(lambda t, u: t + u, ts, ups)
        if s < 4:
            pws = [_dot3(pw, pw) for pw in pws]
    vbs = _each(lambda v, be: v * be, vs, bes)
    kbes = _each(lambda kb, eg: kb * eg, kbs, egs)
    us, ws = _each(_dot3, ts, vbs), _each(_dot3, ts, kbes)
    qks = _each(lambda q, k: _bdot(q, k, "nt"), qs, ks)
    return dict(decay=decays, eg=egs, ek=eks, gam=gams, kb=kbs, kk=kks, t=ts, vb=vbs, kbe=kbes, u=us, w=ws, qk=qks,
                a=_each(lambda qk, dc: qk * dc, qks, decays), qd=_each(lambda q, eg: q * eg, qs, egs),
                kd=_each(lambda k, ek: k * ek, ks, eks), strict=strict)


def _delta_items(refs, CB, HB):
    C, dk = DN_CHUNK, DN_DK
    return [[r[c * C:(c + 1) * C, h * dk:(h + 1) * dk] for h in range(HB) for c in range(CB)] for r in refs]


def _delta_fwd(qkv, beta_b, g_b, H, CB, HB):
    S = qkv.shape[0]
    C, dk = DN_CHUNK, DN_DK
    N = S // C
    R = CB * C
    G = H // HB

    def body(q_ref, k_ref, v_ref, b_ref, g_ref, o_ref, st_ref, s_ref):
        @pl.when(pl.program_id(1) == 0)
        def _():
            s_ref[...] = jnp.zeros((HB, dk, dk), F32)

        L = _delta_local(*_delta_items((q_ref, k_ref, v_ref, b_ref, g_ref), CB, HB))
        ss = [s_ref[h] for h in range(HB)]
        for c in range(CB):
            it = [h * CB + c for h in range(HB)]
            for h in range(HB):
                st_ref[h, c] = ss[h]
            vns = [L["u"][i] - _bdot(L["w"][i], s) for i, s in zip(it, ss)]
            outs = [_bdot(L["qd"][i], s) + _bdot(L["a"][i], vn) for i, s, vn in zip(it, ss, vns)]
            ss = [s * L["gam"][i] + _bdot(L["kd"][i], vn, "tn") for i, s, vn in zip(it, ss, vns)]
            for h in range(HB):
                o_ref[c * C:(c + 1) * C, h * dk:(h + 1) * dk] = outs[h]
        for h in range(HB):
            s_ref[h] = ss[h]

    blk = lambda off: pl.BlockSpec((R, HB * dk), lambda h, n: (n, off + h))
    return _pcall(body, name="delta_fwd", grid=(G, N // CB),
                  in_specs=[blk(0), blk(G), blk(2 * G), blk(0), blk(0)],
                  out_specs=(blk(0), pl.BlockSpec((HB, CB, dk, dk), lambda h, n: (h, n, 0, 0))),
                  out_shape=(_sds((S, H * dk), F32), _sds((H, N, dk, dk), F32)),
                  scratch_shapes=[pltpu.VMEM((HB, dk, dk), F32)],
                  compiler_params=_cp("arbitrary", "arbitrary"))(qkv, qkv, qkv, beta_b, g_b)


def _delta_bwd(qkv, beta_b, g_b, states, do, H, CB, HB):
    S = qkv.shape[0]
    C, dk = DN_CHUNK, DN_DK
    N = S // C
    R = CB * C
    NB = N // CB
    G = H // HB

    def body(q_ref, k_ref, v_ref, b_ref, g_ref, st_ref, do_ref, dqkv_ref, db_ref, dg_ref, ds_ref):
        @pl.when(pl.program_id(1) == 0)
        def _():
            ds_ref[...] = jnp.zeros((HB, dk, dk), F32)

        qs, ks, vs, bes, grs, dos = _delta_items((q_ref, k_ref, v_ref, b_ref, g_ref, do_ref), CB, HB)
        L = _delta_local(qs, ks, vs, bes, grs)
        ts, decays, kbs, egs, eks, gams, qds, kds = (L[n] for n in ("t", "decay", "kb", "eg", "ek", "gam", "qd", "kd"))
        s0s = [st_ref[h, c] for h in range(HB) for c in range(CB)]
        vns = _each(lambda u, w, s0: u - _bdot(w, s0), L["u"], L["w"], s0s)
        pre_dvn = _each(lambda a, d: _bdot(a, d, "tn"), L["a"], dos)
        pre_ds = _each(lambda qd, d: _bdot(qd, d, "tn"), qds, dos)
        das = _each(lambda d, vn: _bdot(d, vn, "nt"), dos, vns)
        dqds = _each(lambda d, s0: _bdot(d, s0, "nt"), dos, s0s)
        ds = [ds_ref[h] for h in range(HB)]
        ds1s, dvns = [None] * (HB * CB), [None] * (HB * CB)
        for c in reversed(range(CB)):
            it = [h * CB + c for h in range(HB)]
            new = [pre_dvn[i] + _bdot(kds[i], d) for i, d in zip(it, ds)]
            for i, d, dv in zip(it, ds, new):
                ds1s[i], dvns[i] = d, dv
            ds = [pre_ds[i] + d * gams[i] - _bdot(L["w"][i], dv, "tn") for i, d, dv in zip(it, ds, new)]
        for h in range(HB):
            ds_ref[h] = ds[h]
        dkds = _each(lambda vn, d1: _bdot(vn, d1, "nt"), vns, ds1s)
        dgams = _each(lambda s0, d1: jnp.sum(jnp.sum(s0 * d1, axis=1, keepdims=True), axis=0, keepdims=True), s0s, ds1s)
        dws = _each(lambda dv, s0: -_bdot(dv, s0, "nt"), dvns, s0s)
        dvbs = _each(lambda t, dv: _dot3(t, dv, "tn"), ts, dvns)
        dkbes = _each(lambda t, dw: _dot3(t, dw, "tn"), ts, dws)
        dts = _each(lambda dv, vb, dw, kbe: _dot3(dv, vb, "nt") + _dot3(dw, kbe, "nt"), dvns, L["vb"], dws, L["kbe"])
        tmp = _each(lambda dt, t: _dot3(dt, t, "nt"), dts, ts)
        dls = _each(lambda t, x: -_dot3(t, x, "tn"), ts, tmp)
        ms = _each(lambda dl, dc: jnp.where(L["strict"], dl * dc, 0.0), dls, decays)
        mas = _each(lambda da, dc: da * dc, das, decays)
        dkbs = _each(lambda m, k, dkbe, eg: _bdot(m, k) + dkbe * eg, ms, ks, dkbes, egs)
        dks = _each(lambda m, kb, ma, q, dkd, ek, dkb, be: _bdot(m, kb, "tn") + _bdot(ma, q, "tn") + dkd * ek + dkb * be,
                    ms, kbs, mas, qs, dkds, eks, dkbs, bes)
        dqs = _each(lambda ma, k, dqd, eg: _bdot(ma, k) + dqd * eg, mas, ks, dqds, egs)
        es = _each(lambda m, kk, ma, qk: m * kk + ma * qk, ms, L["kk"], mas, L["qk"])
        ones = jnp.ones((C, dk), BF16)
        row = lax.broadcasted_iota(jnp.int32, (C, dk), 0)
        for i in range(HB * CB):
            h, c = divmod(i, CB)
            rs, cs = slice(c * C, (c + 1) * C), slice(h * dk, (h + 1) * dk)
            e = es[i]
            e_hi = e.astype(BF16)
            col = _bdot(e_hi, ones, "tn") + _bdot(e - e_hi.astype(F32), ones, "tn")
            t_kd = _rsum(dkds[i] * kds[i])
            dgc = (jnp.broadcast_to(jnp.sum(e, axis=1, keepdims=True), (C, dk)) - col + _rsum(dqds[i] * qds[i]) - t_kd
                   + _rsum(dkbes[i] * L["kbe"][i]))
            dglast = jnp.sum(t_kd[:, 0:1], axis=0, keepdims=True) + dgams[i] * gams[i][:, 0:1]
            dgc = dgc + jnp.where(row == C - 1, dglast, 0.0)
            dqkv_ref[0, rs, cs] = dqs[i]
            dqkv_ref[1, rs, cs] = dks[i]
            dqkv_ref[2, rs, cs] = dvbs[i] * bes[i]
            db_ref[rs, cs] = _rsum(dkbs[i] * ks[i]) + _rsum(dvbs[i] * vs[i])
            dg_ref[rs, cs] = _cumsum_rows(dgc, reverse=True)

    blk = lambda off: pl.BlockSpec((R, HB * dk), lambda h, n: (NB - 1 - n, off + h))
    W = H * dk
    return _pcall(body, name="delta_bwd", grid=(G, NB),
                  in_specs=[blk(0), blk(G), blk(2 * G), blk(0), blk(0),
                            pl.BlockSpec((HB, CB, dk, dk), lambda h, n: (h, NB - 1 - n, 0, 0)), blk(0)],
                  out_specs=(pl.BlockSpec((3, R, HB * dk), lambda h, n: (0, NB - 1 - n, h)), blk(0), blk(0)),
                  out_shape=(_sds((3, S, W), F32), _sds((S, W), F32), _sds((S, W), F32)),
                  scratch_shapes=[pltpu.VMEM((HB, dk, dk), F32)],
                  compiler_params=_cp("arbitrary", "arbitrary"))(qkv, qkv, qkv, beta_b, g_b, states, do)


def _rope_consts():
    lane = np.arange(LANES) % SW_HD
    half = ROT_DIM // 2
    inv = (ROPE_THETA ** (-np.arange(half, dtype=np.float32) * np.float32(2.0 / ROT_DIM))).astype(np.float32)
    freq = np.where(lane < ROT_DIM, inv[lane % half], 0.0).astype(np.float32)
    lo = (lane < half).astype(np.float32)
    hi = ((lane >= half) & (lane < ROT_DIM)).astype(np.float32)
    return jnp.asarray(np.stack([freq, -lo, hi] + [np.zeros(LANES, np.float32)] * 5))


def _rope_tables(pos_col):
    S = pos_col.shape[0]
    tq = _tile(S, 1024, SUBLANES)

    def fn(i, p_ref, c_ref, cos_ref, s1_ref, s2_ref):
        ang = p_ref[...].astype(F32) * c_ref[0:1, :]
        sn = jnp.sin(ang)
        cos_ref[...] = jnp.cos(ang)
        s1_ref[...] = sn * c_ref[1:2, :]
        s2_ref[...] = sn * c_ref[2:3, :]

    o, r = _sds((S, LANES), F32), _rb(tq, LANES)
    return _rows("rope_tables", fn, S, tq, (pos_col, _rope_consts()), [_rb(tq, 1), _full((SUBLANES, LANES))], (o, o, o), (r, r, r))


def _wide(a, w):
    return a if w == LANES else jnp.tile(a, (1, w // LANES))


def _rope(x, cos, s1, s2):
    w, h = x.shape[1], ROT_DIM // 2
    return x * _wide(cos, w) + pltpu.roll(x, w - h, 1) * _wide(s1, w) + pltpu.roll(x, h, 1) * _wide(s2, w)


def _unrope(d, cos, s1, s2):
    w, h = d.shape[1], ROT_DIM // 2
    return d * _wide(cos, w) + pltpu.roll(d * _wide(s1, w), h, 1) + pltpu.roll(d * _wide(s2, w), w - h, 1)


def _swa_setup(n, q_ref, kc_ref, kp_ref, vc_ref, vp_ref, tc, tp):
    B = SW_BLOCK
    qr = _rope(q_ref[...], tc[0][...], tc[1][...], tc[2][...])
    kw = jnp.concatenate([_rope(kp_ref[...], tp[0][...], tp[1][...], tp[2][...]),
                          _rope(kc_ref[...], tc[0][...], tc[1][...], tc[2][...])], axis=0)
    vw = jnp.concatenate([vp_ref[...], vc_ref[...]], axis=0)
    lane = lax.broadcasted_iota(jnp.int32, (2 * B, LANES), 1)
    heads = []
    for hk in range(SW_KV_HEADS):
        kh, vh = kw[:, hk * SW_HD:(hk + 1) * SW_HD], vw[:, hk * SW_HD:(hk + 1) * SW_HD]
        kk, vv = jnp.concatenate([kh, kh], axis=1), jnp.concatenate([vh, vh], axis=1)
        heads.append(tuple(jnp.where(sel, t, 0.0).astype(BF16) for t in (kk, vv) for sel in (lane < SW_HD, lane >= SW_HD)))
    qi = lax.broadcasted_iota(jnp.int32, (B, 2 * B), 0) + B
    ki = lax.broadcasted_iota(jnp.int32, (B, 2 * B), 1)
    off = qi - ki
    ok = (off >= 0) & (off < SW_BLOCK) & ((ki >= B) | (n > 0))
    return qr, heads, ok, lane


def _swa_probs(qj, kh, ok, sk):
    s = _bdot(qj, kh, "nt") * (SW_HD ** -0.5)
    s = jnp.where(ok, s, -1e30)
    m = jnp.maximum(jnp.max(s, axis=-1, keepdims=True), sk)
    p = jnp.exp(s - m)
    es = jnp.exp(sk - m)
    den = jnp.sum(p, axis=-1, keepdims=True) + es
    return p / den, es / den


def _swa_specs(W, cb_q, cb_k):
    B = SW_BLOCK
    cur = lambda w, cb: pl.BlockSpec((B, w), lambda n: (n, cb))
    prv = lambda w, cb: pl.BlockSpec((B, w), lambda n: (jnp.maximum(n - 1, 0), cb))
    specs = [cur(W, cb_q), cur(LANES, cb_k), prv(LANES, cb_k), cur(LANES, cb_k + 1), prv(LANES, cb_k + 1)]
    return specs + [cur(LANES, 0)] * 3 + [prv(LANES, 0)] * 3 + [_full((1, LANES))]


def _swa_fwd(proj, tabs, sinks_row, W, cb_q, cb_k):
    S = proj.shape[0]
    G2 = SW_Q_HEADS // SW_KV_HEADS // 2

    def body(q_ref, kc_ref, kp_ref, vc_ref, vp_ref, c0, c1, c2, p0, p1, p2, sk_ref, o_ref):
        n = pl.program_id(0)
        qr, heads, ok, _ = _swa_setup(n, q_ref, kc_ref, kp_ref, vc_ref, vp_ref, (c0, c1, c2), (p0, p1, p2))
        for j in range(W // LANES):
            k_lo, k_hi, v_lo, v_hi = heads[j // G2]
            qj = qr[:, j * LANES:(j + 1) * LANES]
            p_lo, _ = _swa_probs(qj, k_lo, ok, sk_ref[0:1, 2 * j:2 * j + 1])
            p_hi, _ = _swa_probs(qj, k_hi, ok, sk_ref[0:1, 2 * j + 1:2 * j + 2])
            o_ref[:, j * LANES:(j + 1) * LANES] = (_bdot(p_lo, v_lo) + _bdot(p_hi, v_hi)).astype(BF16)

    t = tuple(tabs)
    return _pcall(body, name="swa_fwd", grid=(S // SW_BLOCK,), in_specs=_swa_specs(W, cb_q, cb_k),
                  out_specs=pl.BlockSpec((SW_BLOCK, W), lambda n: (n, 0)), out_shape=_sds((S, W), BF16),
                  compiler_params=_cp("arbitrary"))(proj, proj, proj, proj, proj, *t, *t, sinks_row)


def _swa_bwd(proj, tabs, sinks_row, do, W, cb_q, cb_k):
    S = proj.shape[0]
    B = SW_BLOCK
    G2 = SW_Q_HEADS // SW_KV_HEADS // 2
    SKR = -(-SW_Q_HEADS // SUBLANES) * SUBLANES

    def body(q_ref, kc_ref, kp_ref, vc_ref, vp_ref, c0, c1, c2, p0, p1, p2, sk_ref, do_ref,
             dq_ref, dkc_ref, dkp_ref, dvc_ref, dvp_ref, dsk_ref):
        n = pl.program_id(0)
        qr, heads, ok, lane = _swa_setup(n, q_ref, kc_ref, kp_ref, vc_ref, vp_ref, (c0, c1, c2), (p0, p1, p2))

        @pl.when(n == 0)
        def _():
            dsk_ref[...] = jnp.zeros((SKR, LANES), F32)

        acc_k = [jnp.zeros((2 * B, LANES), F32) for _ in range(SW_KV_HEADS)]
        acc_v = [jnp.zeros((2 * B, LANES), F32) for _ in range(SW_KV_HEADS)]
        dqs = []
        for j in range(W // LANES):
            hk = j // G2
            qj = qr[:, j * LANES:(j + 1) * LANES]
            dj = do_ref[:, j * LANES:(j + 1) * LANES]
            dqj = jnp.zeros((B, LANES), F32)
            for half in range(2):
                kh, vh = heads[hk][half], heads[hk][2 + half]
                h = 2 * j + half
                p, ps = _swa_probs(qj, kh, ok, sk_ref[0:1, h:h + 1])
                dp = _bdot(dj, vh, "nt")
                delta = jnp.sum(p * dp, axis=-1, keepdims=True)
                ds = p * (dp - delta) * (SW_HD ** -0.5)
                dqj = dqj + _bdot(ds, kh)
                sel = (lane < SW_HD) if half == 0 else (lane >= SW_HD)
                acc_k[hk] = acc_k[hk] + jnp.where(sel, _bdot(ds, qj, "tn"), 0.0)
                acc_v[hk] = acc_v[hk] + jnp.where(sel, _bdot(p, dj, "tn"), 0.0)
                dsk_ref[h:h + 1, :] += jnp.broadcast_to(-jnp.sum(ps * delta, axis=0, keepdims=True), (1, LANES))
            dqs.append(dqj)
        dq_ref[...] = _unrope(jnp.concatenate(dqs, axis=1), c0[...], c1[...], c2[...]).astype(BF16)
        fold = lambda a: a[:, :SW_HD] + a[:, SW_HD:]
        dkw = jnp.concatenate([fold(a) for a in acc_k], axis=1)
        dvw = jnp.concatenate([fold(a) for a in acc_v], axis=1)
        dkp_ref[...], dkc_ref[...] = dkw[:B], dkw[B:]
        dvp_ref[...], dvc_ref[...] = dvw[:B], dvw[B:]

    t = tuple(tabs)
    blk = lambda w: pl.BlockSpec((B, w), lambda n: (n, 0))
    o = _sds((S, LANES), F32)
    return _pcall(body, name="swa_bwd", grid=(S // B,), in_specs=_swa_specs(W, cb_q, cb_k) + [blk(W)],
                  out_specs=(blk(W), blk(LANES), blk(LANES), blk(LANES), blk(LANES), _full((SKR, LANES))),
                  out_shape=(_sds((S, W), BF16), o, o, o, o, _sds((SKR, LANES), F32)),
                  compiler_params=_cp("arbitrary"))(proj, proj, proj, proj, proj, *t, *t, sinks_row, do)


def _swa_kv_combine(dkc, dkp, dvc, dvp, tabs):
    S = dkc.shape[0]
    B = SW_BLOCK
    nb = S // B

    def fn(n, kc_ref, kp_ref, vc_ref, vp_ref, c0, c1, c2, o_ref):
        more = n < nb - 1
        dk = kc_ref[...] + jnp.where(more, kp_ref[...], 0.0)
        dv = vc_ref[...] + jnp.where(more, vp_ref[...], 0.0)
        o_ref[...] = jnp.concatenate([_unrope(dk, c0[...], c1[...], c2[...]), dv], axis=1).astype(BF16)

    cur = _rb(B, LANES)
    nxt = pl.BlockSpec((B, LANES), lambda n: (jnp.minimum(n + 1, nb - 1), 0))
    return _rows("swa_kv_combine", fn, S, B, (dkc, dkp, dvc, dvp, *tabs), [cur, nxt, cur, nxt, cur, cur, cur],
                 _sds((S, 2 * LANES), BF16), _rb(B, 2 * LANES))


ANY = pl.BlockSpec(memory_space=pl.ANY)


def _place():
    x, y, c = lax.axis_index("x"), lax.axis_index("y"), lax.axis_index("c")
    return x, y, c, [(1 - x, y), (x, 1 - y), (1 - x, 1 - y)]


def _comm_call(name, body, out_shapes, n_sems, n_local, *ins):
    return _pcall(body, name=name, out_shape=tuple(out_shapes), in_specs=[ANY] * len(ins), out_specs=tuple(ANY for _ in out_shapes),
                  scratch_shapes=[pltpu.SemaphoreType.DMA((n_sems,)), pltpu.SemaphoreType.DMA((n_sems,)),
                                  pltpu.SemaphoreType.DMA((n_local,))])(*ins)


def _remote(src, dst, send, recv, k, to):
    return pltpu.make_async_remote_copy(src_ref=src, dst_ref=dst, send_sem=send.at[k], recv_sem=recv.at[k], device_id=to,
                                        device_id_type=MESH)


def _gather_chips(name, arrs):
    n = len(arrs)
    Lh = arrs[0].shape[0] // 2

    def body(*refs):
        w, o, (send, recv, lsem) = refs[:n], refs[n:2 * n], refs[2 * n:]
        x, y, c, chips = _place()
        me, sib = 2 * x + y, (x, y, 1 - c)
        own, other = pl.ds(c * Lh, Lh), pl.ds((1 - c) * Lh, Lh)
        idx = [2 * cx + cy for cx, cy in chips]
        local = [pltpu.make_async_copy(w[a], o[a].at[me], lsem.at[a]) for a in range(n)]
        first = [[_remote(w[a].at[own], o[a].at[me, own], send, recv, 6 * a + j, (*chips[j], c)) for j in range(3)] for a in range(n)]
        passed = [[_remote(o[a].at[idx[j], own], o[a].at[idx[j], own], send, recv, 6 * a + 3 + j, sib) for j in range(3)] for a in range(n)]
        for cp in local + [cp for row in first for cp in row]:
            cp.start()
        for j in range(3):
            for a in range(n):
                _remote(w[a].at[own], o[a].at[idx[j], own], send, recv, 6 * a + j, (*chips[j], c)).wait_recv()
                passed[a][j].start()
        for j in range(3):
            for a in range(n):
                _remote(w[a].at[other], o[a].at[idx[j], other], send, recv, 6 * a + 3 + j, sib).wait_recv()
        for cp in [cp for row in first + passed for cp in row]:
            cp.wait_send()
        for cp in local:
            cp.wait()

    return _comm_call(name, body, [_sds((4,) + a.shape, a.dtype) for a in arrs], 6 * n, n, *arrs)


def _pair_swap(name, arrs):
    n = len(arrs)
    Lh = arrs[0].shape[0] // 2

    def body(*refs):
        g, o, (send, recv, _) = refs[:n], refs[n:2 * n], refs[2 * n:]
        x, y, c, _ = _place()
        cps = [_remote(g[a].at[pl.ds((1 - c) * Lh, Lh)], o[a], send, recv, a, (x, y, 1 - c)) for a in range(n)]
        for cp in cps:
            cp.start()
        for cp in cps:
            cp.wait()

    return _comm_call(name, body, [_sds((Lh,) + a.shape[1:], a.dtype) for a in arrs], n, 1, *arrs)


def _chip_slice(ref, axis, s):
    if axis is None:
        return ref.at[s]
    q = ref.shape[axis] // 4
    start = s * q if isinstance(s, int) else pl.multiple_of(s * q, q)
    return ref.at[tuple([slice(None)] * axis + [pl.ds(start, q)])]


def _scatter_chips(name, items):
    n = len(items)
    part = lambda a, ax: a.shape[1:] if ax is None else tuple(d // 4 if i == ax else d for i, d in enumerate(a.shape))

    def body(*refs):
        p, o, (send, recv, lsem) = refs[:n], refs[n:2 * n], refs[2 * n:]
        x, y, c, chips = _place()
        me = 2 * x + y
        idx = [2 * cx + cy for cx, cy in chips]
        local = [pltpu.make_async_copy(_chip_slice(p[a], items[a][1], me), o[a].at[me], lsem.at[a]) for a in range(n)]
        cps = [_remote(_chip_slice(p[a], items[a][1], idx[j]), o[a].at[me], send, recv, 3 * a + j, (*chips[j], c))
               for a in range(n) for j in range(3)]
        for cp in local + cps:
            cp.start()
        for a in range(n):
            for j in range(3):
                _remote(_chip_slice(p[a], items[a][1], me), o[a].at[idx[j]], send, recv, 3 * a + j, (*chips[j], c)).wait_recv()
        for cp in cps:
            cp.wait_send()
        for cp in local:
            cp.wait()

    return _comm_call(name, body, [_sds((4,) + part(a, ax), a.dtype) for a, ax in items], 3 * n, n, *[a for a, _ in items])


def _pair_gather(name, arrs):
    n = len(arrs)
    Lh = arrs[0].shape[0]

    def body(*refs):
        h, o, (send, recv, lsem) = refs[:n], refs[n:2 * n], refs[2 * n:]
        x, y, c, _ = _place()
        own = pl.ds(c * Lh, Lh)
        local = [pltpu.make_async_copy(h[a], o[a].at[own], lsem.at[a]) for a in range(n)]
        cps = [_remote(h[a], o[a].at[own], send, recv, a, (x, y, 1 - c)) for a in range(n)]
        for cp in local + cps:
            cp.start()
        for cp in cps:
            cp.wait()
        for cp in local:
            cp.wait()

    return _comm_call(name, body, [_sds((2 * Lh,) + a.shape[1:], a.dtype) for a in arrs], n, n, *arrs)


def _gather_all(name, b):
    R, C = b.shape
    flips = [(dx, dy, dc) for dx in (0, 1) for dy in (0, 1) for dc in (0, 1)][1:]

    def body(b_ref, o_ref, send, recv, lsem):
        x, y, c, _ = _place()
        me = 4 * x + 2 * y + c
        peers = [(x ^ dx, y ^ dy, c ^ dc) for dx, dy, dc in flips]
        mine = pltpu.make_async_copy(b_ref, o_ref.at[me], lsem.at[0])
        mine.start()
        cps = [_remote(b_ref, o_ref.at[me], send, recv, k, peer) for k, peer in enumerate(peers)]
        for cp in cps:
            cp.start()
        for k, (px, py, pc) in enumerate(peers):
            _remote(b_ref, o_ref.at[4 * px + 2 * py + pc], send, recv, k, (px, py, pc)).wait_recv()
        for cp in cps:
            cp.wait_send()
        mine.wait()

    return _comm_call(name, body, [_sds((8, R, C), b.dtype)], 7, 1, b)[0]


def _block_rows(rows, width):
    return _tile(rows, max(SUBLANES, (1 << 19) // width), SUBLANES)


def _add_half(name, g, got):
    L, A, B = g.shape
    Lh = L // 2
    tq = _block_rows(A, B)

    def body(c_ref, g_ref, r_ref, o_ref):
        o_ref[...] = (g_ref[...] + r_ref[...]).astype(BF16)

    spec = pltpu.PrefetchScalarGridSpec(
        num_scalar_prefetch=1, grid=(Lh, A // tq),
        in_specs=[pl.BlockSpec((1, tq, B), lambda l, i, c_ref: (c_ref[0] * Lh + l, i, 0)),
                  pl.BlockSpec((1, tq, B), lambda l, i, c_ref: (l, i, 0))],
        out_specs=pl.BlockSpec((1, tq, B), lambda l, i, c_ref: (l, i, 0)))
    return _pcall(body, name=name, grid_spec=spec, out_shape=_sds((Lh, A, B), BF16),
                  compiler_params=_cp("arbitrary", "arbitrary"))(lax.axis_index("c").reshape(1).astype(jnp.int32), g, got)


def _sum_slots(name, a):
    n, R, C = a.shape
    tq = _block_rows(R, n * C)

    def fn(i, a_ref, o_ref):
        t = a_ref[0].astype(F32)
        for s in range(1, n):
            t = t + a_ref[s].astype(F32)
        o_ref[...] = t

    return _rows(name, fn, R, tq, (a,), [pl.BlockSpec((n, tq, C), lambda i: (0, i, 0))], _sds((R, C), F32), _rb(tq, C))


def _adamw(name, w, g, m, v):
    R, C = w.shape
    tq = _tile(R, 256, SUBLANES)

    def fn(i, w_ref, g_ref, m_ref, v_ref, d_ref, mo_ref, vo_ref):
        gg = g_ref[...]
        mn = ADAM_B1 * m_ref[...] + (1.0 - ADAM_B1) * gg
        vn = ADAM_B2 * v_ref[...] + (1.0 - ADAM_B2) * (gg * gg)
        mo_ref[...] = mn
        vo_ref[...] = vn
        m_hat = mn / (1.0 - ADAM_B1 ** ADAM_STEP)
        v_hat = vn / (1.0 - ADAM_B2 ** ADAM_STEP)
        d_ref[...] = -ADAM_LR * (m_hat / (jnp.sqrt(v_hat) + ADAM_EPS) + ADAM_WD * w_ref[...])

    r, o = _rb(tq, C), _sds((R, C), F32)
    return _rows(name, fn, R, tq, (w, g, m, v), [r, r, r, r], (o, o, o), (r, r, r))


def _pack(arrs, width, lead=()):
    nl = len(lead)
    flat = jnp.concatenate([a.reshape(lead + (-1,)) for a in arrs], axis=nl)
    n = flat.shape[-1]
    unit = PACK_ROWS * width
    tot = -(-n // unit) * unit
    flat = jnp.pad(flat, [(0, 0)] * nl + [(0, tot - n)])
    return flat.reshape(lead + (tot // width, width))


def _unpack(buf, shapes, lead=()):
    flat = buf.reshape(lead + (-1,))
    out, off = [], 0
    for s in shapes:
        n = int(np.prod(s))
        out.append(flat[..., off:off + n].reshape(lead + tuple(s)))
        off += n
    return out


def _in_groups(W, H):
    o_sq = 4 * W + 2 * H
    o_k = o_sq + W
    o_g = o_k + 2 * KV_W
    return [(0, 4 * W), (o_sq, o_k), (o_g, o_g + 2 * W), (o_k, o_g), (4 * W, o_sq)]


def _relayout_in(shards, W, H):
    c4 = shards.shape[2]
    parts = []
    for lo, hi in _in_groups(W, H):
        for s in range(4):
            a, b = max(lo, s * c4), min(hi, (s + 1) * c4)
            if a < b:
                parts.append(shards[s][:, a - s * c4:b - s * c4])
    parts.append(jnp.zeros((shards.shape[1], BA_W - 2 * H), shards.dtype))
    return jnp.concatenate(parts, axis=1)


def _shard_in(d, W, H):
    groups = _in_groups(W, H)
    starts = [sum(hi - lo for lo, hi in groups[:i]) for i in range(len(groups))]
    stored = sorted(zip(groups, starts))
    c4 = sum(hi - lo for lo, hi in groups) // 4
    out = []
    for s in range(4):
        parts = []
        for (lo, hi), at in stored:
            a, b = max(lo, s * c4), min(hi, (s + 1) * c4)
            if a < b:
                parts.append(d[:, :, at + a - lo:at + b - lo])
        out.append(jnp.concatenate(parts, axis=2))
    return jnp.stack(out)


def _lane_row(vals, at):
    return jnp.pad(vals, (at, LANES - at - vals.shape[0]))[None]


def _layer_fwd(x, lw, tabs, W, H):
    D = x.shape[1]
    cbk = 7 * W // LANES
    h = _pre_norm(x, lw["g1"])
    proj = _mm("mm_in", h, lw["win"], "nn", F32, tn=768)
    qkv = _dn_prep(proj, lw["conv"], W)
    beta_b, g_b = _dn_gates(proj, lw["alog"], lw["dt"], H, cbk + 2)
    o, st = _delta_fwd(qkv, beta_b, g_b, H, DELTA_CB, DELTA_HB)
    oa = _dn_out(o, proj, lw["ng"], W, 3)
    ob = _swa_fwd(proj, tabs, lw["sinks"], W, 4, cbk)
    ya = _mm("mm_up_dn", oa, lw["wup_dn"], "nn", F32)
    yb = _mm("mm_up_sw", ob, lw["wup_sw"], "nn", F32)
    mixin = _mix(proj, ya, yb, D, 5)
    mix = _mm("mm_o", mixin, lw["wo"], "nn", F32)
    x1, h2 = _post_mix(x, mix, lw["g2"], lw["g3"])
    f1 = _mm("mm_ff1", h2, lw["wff1"], "nn", F32)
    act = _relu2(f1)
    ff = _mm("mm_ff2", act, lw["wff2"], "nn", F32)
    x2 = _post_mlp(x1, ff, lw["g4"])
    saved = dict(x=x, h=h, proj=proj, qkv=qkv, beta_b=beta_b, g_b=g_b, o=o, st=st, oa=oa, ob=ob, ya=ya, yb=yb,
                 mixin=mixin, mix=mix, x1=x1, h2=h2, f1=f1, act=act, ff=ff)
    return x2, saved


def _layer_bwd(dx2, lw, sv, tabs, W, H):
    D = dx2.shape[1]
    cbk = 7 * W // LANES
    dff, dg4 = _post_mlp_bwd(sv["ff"], lw["g4"], dx2)
    dact = _mm("mm_ff2_dx", dff, lw["wff2"], "nt", F32)
    dwff2 = _mm("mm_ff2_dw", sv["act"], dff, "tn", F32)
    df1 = _relu2_bwd(dact, sv["f1"])
    dh2 = _mm("mm_ff1_dx", df1, lw["wff1"], "nt", F32)
    dwff1 = _mm("mm_ff1_dw", sv["h2"], df1, "tn", F32)
    dx1, dmix, dg3, dg2 = _mid_bwd(sv["x1"], lw["g3"], dh2, dx2, sv["mix"], lw["g2"])
    dmixin = _mm("mm_o_dx", dmix, lw["wo"], "nt", F32)
    dwo = _mm("mm_o_dw", sv["mixin"], dmix, "tn", F32)
    dya, dyb, dga, dgb = _mix_bwd(sv["proj"], sv["ya"], sv["yb"], dmixin, D, 5)
    doa = _mm("mm_up_dn_dx", dya, lw["wup_dn"], "nt", F32)
    dwup_dn = _mm("mm_up_dn_dw", sv["oa"], dya, "tn", F32)
    dob = _mm("mm_up_sw_dx", dyb, lw["wup_sw"], "nt", F32)
    dwup_sw = _mm("mm_up_sw_dw", sv["ob"], dyb, "tn", F32)
    do, dz, dng = _dn_out_bwd(sv["o"], sv["proj"], lw["ng"], doa, W, 3)
    dqkvn, dbeta_b, dg_b = _delta_bwd(sv["qkv"], sv["beta_b"], sv["g_b"], sv["st"], do, H, DELTA_CB, DELTA_HB)
    dba, dalog, ddt = _dn_gates_bwd(sv["proj"], lw["alog"], lw["dt"], dbeta_b, dg_b, H, cbk + 2)
    dc, dconv = _dn_prep_bwd_a(sv["proj"], lw["conv"], dqkvn, W)
    dqkv = _dn_prep_bwd_b(dc, lw["conv"], W)
    dq_sw, dkc, dkp, dvc, dvp, dsk = _swa_bwd(sv["proj"], tabs, lw["sinks"], dob, W, 4, cbk)
    dkv = _swa_kv_combine(dkc, dkp, dvc, dvp, tabs)
    dproj = jnp.concatenate([dqkv, dz, dq_sw, dga, dgb, dkv, dba], axis=1)
    dh = _mm("mm_in_dx", dproj, lw["win"], "nt", F32, tk=768)
    dwin = _mm("mm_in_dw", sv["h"], dproj, "tn", F32, tn=768)
    dx, dg1 = _pre_norm_bwd(sv["x"], lw["g1"], dh, dx1)
    grads = dict(pre_mix_g=dg1[0], w_in=dwin, dn_conv_w=dconv, dn_a_log=dalog[0, H:2 * H],
                 dn_dt_bias=ddt[0, H:2 * H], dn_norm_g=dng[0], sw_sinks=dsk[:SW_Q_HEADS, 0], w_up_dn=dwup_dn, w_up_sw=dwup_sw,
                 w_o=dwo, post_mix_g=dg2[0], pre_mlp_g=dg3[0], w_ff1=dwff1, w_ff2=dwff2, post_mlp_g=dg4[0])
    return dx, grads


_WEIGHTS = ["pre_mix_g", "w_in", "dn_conv_w", "dn_a_log", "dn_dt_bias", "dn_norm_g", "sw_sinks", "w_up_dn", "w_up_sw", "w_o",
            "post_mix_g", "pre_mlp_g", "w_ff1", "w_ff2", "post_mlp_g"]
_BIG = {"w_in": 2, "w_up_dn": 1, "w_up_sw": 1, "w_o": 1, "w_ff1": 2, "w_ff2": 1}
_SMALL = [n for n in _WEIGHTS if n not in _BIG]


def _step(P):
    x, target = P["x"][0], P["loss_target"][0]
    S, D = x.shape
    L = P["pre_mix_g"].shape[0]
    H, W = DN_HEADS, DN_HEADS * DN_DK
    assert W == D == SW_Q_HEADS * SW_HD and KV_W == LANES
    me = 2 * lax.axis_index("x") + lax.axis_index("y")

    assert L % 2 == 0
    gathered = _gather_chips("weights_gather", [P[n].astype(BF16) for n in _BIG] + [P["dn_conv_w"]])
    full = dict(zip(list(_BIG) + ["dn_conv_w"], gathered))
    rows = lambda n, l: full[n][:, l].reshape(-1, full[n].shape[-1])
    cols = lambda n, l: jnp.concatenate([full[n][s, l] for s in range(4)], axis=-1)

    tabs = _rope_tables(P["positions"].reshape(S, 1))
    lws = []
    for l in range(L):
        lws.append(dict(
            g1=P["pre_mix_g"][l][None], win=_relayout_in(full["w_in"][:, l], W, H), conv=cols("dn_conv_w", l),
            alog=_lane_row(P["dn_a_log"][l], H), dt=_lane_row(P["dn_dt_bias"][l], H), ng=P["dn_norm_g"][l][None],
            sinks=_lane_row(P["sw_sinks"][l], 0), wup_dn=rows("w_up_dn", l), wup_sw=rows("w_up_sw", l), wo=rows("w_o", l),
            g2=P["post_mix_g"][l][None], g3=P["pre_mlp_g"][l][None], wff1=cols("w_ff1", l), wff2=rows("w_ff2", l),
            g4=P["post_mlp_g"][l][None]))

    saved = []
    for l in range(L):
        x, sv = _layer_fwd(x, lws[l], tabs, W, H)
        saved.append(sv)
    loss_row, dx = _loss_head(x, target)
    layer_grads = [None] * L
    for l in reversed(range(L)):
        dx, layer_grads[l] = _layer_bwd(dx, lws[l], saved[l], tabs, W, H)
    grads = {n: jnp.stack([layer_grads[l][n] for l in range(L)]) for n in _WEIGHTS}

    got = _pair_swap("grad_pair_swap", [grads[n] for n in _BIG])
    part = {n: _add_half("grad_pair_add_" + n, grads[n], r) for n, r in zip(_BIG, got)}
    items = [(_shard_in(part[n], W, H), None) if n == "w_in" else (part[n], ax) for n, ax in _BIG.items()]
    slots = _scatter_chips("grad_chip_scatter", items)
    halves = []
    for n, s in zip(_BIG, slots):
        flat = s.reshape(4, -1, s.shape[-1])
        halves.append(_sum_slots("grad_chip_sum_" + n, flat).reshape(s.shape[1:]))
    gsum = dict(zip(_BIG, _pair_gather("grad_pair_gather", halves)))
    small_shapes = [(1,)] + [grads[n].shape for n in _SMALL]
    tot = _sum_slots("small_sum", _gather_all("small_gather", _pack([loss_row[0, :1]] + [grads[n] for n in _SMALL], LANES)))
    small = _unpack(tot, small_shapes)
    loss = small[0][0]
    gsum.update(zip(_SMALL, small[1:]))
    cw = P["dn_conv_w"].shape[2]
    gsum["dn_conv_w"] = lax.dynamic_slice_in_dim(gsum["dn_conv_w"], me * cw, cw, axis=2)

    delta, new_m, new_v = {}, {}, {}
    for n in _BIG:
        s = P[n].shape
        two_d = lambda a: a.reshape(s[0] * s[1], s[2])
        outs = _adamw("adamw_" + n, two_d(P[n]), two_d(gsum[n]), two_d(P["m_" + n]), two_d(P["v_" + n]))
        delta[n], new_m[n], new_v[n] = (o.reshape(s) for o in outs)
    sm_shapes = [P[n].shape for n in _SMALL]
    outs = _adamw("adamw_small", *(_pack([src[pre + n] for n in _SMALL], LANES)
                                   for src, pre in ((P, ""), (gsum, ""), (P, "m_"), (P, "v_"))))
    for d, o in zip((delta, new_m, new_v), outs):
        d.update(zip(_SMALL, _unpack(o, sm_shapes)))

    return (loss, dx[None], *[gsum[n] for n in _WEIGHTS], *[delta[n] for n in _WEIGHTS],
            *[new_m[n] for n in _WEIGHTS], *[new_v[n] for n in _WEIGHTS])


def kernel(x, positions, pre_mix_g, w_in, dn_conv_w, dn_a_log, dn_dt_bias, dn_norm_g, sw_sinks, w_up_dn, w_up_sw, w_o, post_mix_g, pre_mlp_g, w_ff1, w_ff2, post_mlp_g, loss_target, m_pre_mix_g, m_w_in, m_dn_conv_w, m_dn_a_log, m_dn_dt_bias, m_dn_norm_g, m_sw_sinks, m_w_up_dn, m_w_up_sw, m_w_o, m_post_mix_g, m_pre_mlp_g, m_w_ff1, m_w_ff2, m_post_mlp_g, v_pre_mix_g, v_w_in, v_dn_conv_w, v_dn_a_log, v_dn_dt_bias, v_dn_norm_g, v_sw_sinks, v_w_up_dn, v_w_up_sw, v_w_o, v_post_mix_g, v_pre_mlp_g, v_w_ff1, v_w_ff2, v_post_mlp_g):
    vals = (x, positions, pre_mix_g, w_in, dn_conv_w, dn_a_log, dn_dt_bias, dn_norm_g, sw_sinks, w_up_dn, w_up_sw, w_o, post_mix_g, pre_mlp_g, w_ff1, w_ff2, post_mlp_g, loss_target, m_pre_mix_g, m_w_in, m_dn_conv_w, m_dn_a_log, m_dn_dt_bias, m_dn_norm_g, m_sw_sinks, m_w_up_dn, m_w_up_sw, m_w_o, m_post_mix_g, m_pre_mlp_g, m_w_ff1, m_w_ff2, m_post_mlp_g, v_pre_mix_g, v_w_in, v_dn_conv_w, v_dn_a_log, v_dn_dt_bias, v_dn_norm_g, v_sw_sinks, v_w_up_dn, v_w_up_sw, v_w_o, v_post_mix_g, v_pre_mlp_g, v_w_ff1, v_w_ff2, v_post_mlp_g)
    names = ["x", "positions"] + _WEIGHTS + ["loss_target"] + ["m_" + n for n in _WEIGHTS] + ["v_" + n for n in _WEIGHTS]
    return _step(dict(zip(names, vals)))
```

```python
import functools

import numpy as np
import jax
import jax.numpy as jnp
from jax import lax
from jax.experimental import pallas as pl
from jax.experimental.pallas import tpu as pltpu

F32, BF16 = jnp.float32, jnp.bfloat16
MESH = pl.DeviceIdType.MESH

DN_HEADS = 8
DN_DK = 128
DN_CONV = 4
DN_CHUNK = 64
SW_Q_HEADS = 16
SW_KV_HEADS = 2
SW_HD = 64
SW_BLOCK = 128
ROPE_THETA = 500000.0
ROT_DIM = SW_HD // 4
EPS = 1e-6
ADAM_LR, ADAM_B1, ADAM_B2, ADAM_EPS, ADAM_WD, ADAM_STEP = 0.001, 0.9, 0.999, 1e-08, 0.01, 10

LANES = 128
SUBLANES = 8
VMEM_LIMIT = 48 * 1024 * 1024
KV_W = SW_KV_HEADS * SW_HD
BA_W = 256
PACK_ROWS = 512
DELTA_CB = 4
DELTA_HB = 4


def _pcall(body, **kw):
    return pl.pallas_call(body, **kw)


def _cp(*sem):
    return pltpu.CompilerParams(dimension_semantics=sem, vmem_limit_bytes=VMEM_LIMIT)


def _tile(n, pref, unit=LANES):
    if n <= pref:
        return n
    t = (pref // unit) * unit
    while t > unit and n % t:
        t -= unit
    assert n % t == 0, (n, pref)
    return t


def _sds(shape, dtype):
    return jax.ShapeDtypeStruct(tuple(shape), dtype)


_DIMS = {"nn": ((1,), (0,)), "nt": ((1,), (1,)), "tn": ((0,), (0,))}


def _mm(name, a, b, mode, out_dtype=F32, tm=1024, tn=1024, tk=1024, extras=(), epi=None, out_dtypes=None, slab=None):
    if mode == "nn":
        (M, K), (_, N) = a.shape, b.shape
    elif mode == "nt":
        (M, K), (N, _) = a.shape, b.shape
    else:
        (K, M), (_, N) = a.shape, b.shape
    tm, tn, tk = _tile(M, tm), _tile(N, tn), _tile(K, tk)
    nk = K // tk
    a_spec = {"nn": pl.BlockSpec((tm, tk), lambda i, j, k: (i, k)),
              "nt": pl.BlockSpec((tm, tk), lambda i, j, k: (i, k)),
              "tn": pl.BlockSpec((tk, tm), lambda i, j, k: (k, i))}[mode]
    b_spec = {"nn": pl.BlockSpec((tk, tn), lambda i, j, k: (k, j)),
              "nt": pl.BlockSpec((tn, tk), lambda i, j, k: (j, k)),
              "tn": pl.BlockSpec((tk, tn), lambda i, j, k: (k, j))}[mode]
    dims = (_DIMS[mode], ((), ()))
    out_dtypes = tuple(out_dtypes or (out_dtype,))
    ne, no = len(extras), len(out_dtypes)
    o_spec = pl.BlockSpec((tm, tn), lambda i, j, k: (i, j))

    def body(*refs):
        a_ref, b_ref, ex = refs[0], refs[1], refs[2:2 + ne]
        outs, acc_ref = refs[-1 - no:-1], refs[-1]
        k = pl.program_id(2)
        part = lax.dot_general(a_ref[...], b_ref[...], dims, preferred_element_type=F32)

        @pl.when(k == 0)
        def _():
            acc_ref[...] = part

        @pl.when(k > 0)
        def _():
            acc_ref[...] += part

        @pl.when(k == nk - 1)
        def _():
            res = epi(acc_ref[...], *[e[...] for e in ex]) if epi else (acc_ref[...],)
            for o, r, dt in zip(outs, res, out_dtypes):
                if slab is None:
                    o[...] = r.astype(dt)
                else:
                    o[0] = r.astype(dt)

    kw = dict(name=name, grid=(M // tm, N // tn, nk), scratch_shapes=[pltpu.VMEM((tm, tn), F32)],
              compiler_params=_cp("parallel", "parallel", "arbitrary"))
    if slab is not None:
        buf, l = slab
        return _pcall(body, in_specs=[a_spec, b_spec, ANY], out_specs=pl.BlockSpec((1, tm, tn), lambda i, j, k: (l, i, j)),
                      out_shape=_sds(buf.shape, buf.dtype), input_output_aliases={2: 0}, **kw)(a, b, buf)
    out = _pcall(body, in_specs=[a_spec, b_spec] + [o_spec] * ne, out_specs=tuple(o_spec for _ in out_dtypes),
                 out_shape=tuple(_sds((M, N), dt) for dt in out_dtypes), **kw)(a, b, *extras)
    return out if no > 1 else out[0]


def _rows(name, fn, n_rows, tq, ins, in_specs, out_shapes, out_specs):
    def body(*refs):
        fn(pl.program_id(0), *refs)

    return _pcall(body, name=name, grid=(n_rows // tq,), in_specs=in_specs, out_specs=out_specs,
                  out_shape=out_shapes, compiler_params=_cp("arbitrary"))(*ins)


def _rb(tq, w, cb=0):
    return pl.BlockSpec((tq, w), lambda i: (i, cb))


def _full(shape):
    return pl.BlockSpec(tuple(shape), lambda *_: (0,) * len(shape))


def _rms_fwd(x, g):
    r = lax.rsqrt(jnp.mean(x * x, axis=-1, keepdims=True) + EPS)
    return x * r * g


def _rms_bwd(x, g, dy):
    r = lax.rsqrt(jnp.mean(x * x, axis=-1, keepdims=True) + EPS)
    xh = x * r
    t = dy * g
    dx = r * (t - xh * jnp.mean(t * xh, axis=-1, keepdims=True))
    return dx, jnp.sum(dy * xh, axis=0, keepdims=True)


def _acc(i, ref, val):
    @pl.when(i == 0)
    def _():
        ref[...] = val

    @pl.when(i > 0)
    def _():
        ref[...] += val


def _sigmoid(x):
    return 1.0 / (1.0 + jnp.exp(-x))


def _pre_norm(x, g):
    S, D = x.shape
    tq = _tile(S, 512, SUBLANES)

    def fn(i, x_ref, g_ref, h_ref):
        h_ref[...] = _rms_fwd(x_ref[...], g_ref[...]).astype(BF16)

    return _rows("pre_norm", fn, S, tq, (x, g), [_rb(tq, D), _full((1, D))], _sds((S, D), BF16), _rb(tq, D))


def _post_mix(x, mix, g2, g3):
    S, D = x.shape
    tq = _tile(S, 512, SUBLANES)

    def fn(i, x_ref, m_ref, g2_ref, g3_ref, x1_ref, h2_ref):
        x1 = x_ref[...] + _rms_fwd(m_ref[...], g2_ref[...])
        x1_ref[...] = x1
        h2_ref[...] = _rms_fwd(x1, g3_ref[...]).astype(BF16)

    return _rows("post_mix", fn, S, tq, (x, mix, g2, g3), [_rb(tq, D), _rb(tq, D), _full((1, D)), _full((1, D))],
                 (_sds((S, D), F32), _sds((S, D), BF16)), (_rb(tq, D), _rb(tq, D)))


def _post_mlp(x1, ff, g4):
    S, D = x1.shape
    tq = _tile(S, 512, SUBLANES)

    def fn(i, x_ref, f_ref, g_ref, o_ref):
        o_ref[...] = x_ref[...] + _rms_fwd(f_ref[...], g_ref[...])

    return _rows("post_mlp", fn, S, tq, (x1, ff, g4), [_rb(tq, D), _rb(tq, D), _full((1, D))], _sds((S, D), F32), _rb(tq, D))


def _loss_head(y, target):
    S, D = y.shape
    tq = _tile(S, 512, SUBLANES)

    def fn(i, y_ref, t_ref, l_ref, d_ref):
        e = y_ref[...] - t_ref[...]
        d_ref[...] = e * (1.0 / D)
        part = jnp.sum(jnp.sum(e * e, axis=1, keepdims=True), axis=0, keepdims=True) * (0.5 / D)
        _acc(i, l_ref, jnp.broadcast_to(part, (1, LANES)))

    return _rows("loss_head", fn, S, tq, (y, target), [_rb(tq, D), _rb(tq, D)],
                 (_sds((1, LANES), F32), _sds((S, D), F32)), (_full((1, LANES)), _rb(tq, D)))


def _post_mlp_bwd(ff, g4, dx2):
    S, D = ff.shape
    tq = _tile(S, 512, SUBLANES)

    def fn(i, f_ref, g_ref, d_ref, o_ref, dg_ref):
        dx, dg = _rms_bwd(f_ref[...], g_ref[...], d_ref[...])
        o_ref[...] = dx.astype(BF16)
        _acc(i, dg_ref, dg)

    return _rows("post_mlp_bwd", fn, S, tq, (ff, g4, dx2), [_rb(tq, D), _full((1, D)), _rb(tq, D)],
                 (_sds((S, D), BF16), _sds((1, D), F32)), (_rb(tq, D), _full((1, D))))


def _mid_bwd(x1, g3, dh2, dx2, mix, g2):
    S, D = x1.shape
    tq = _tile(S, 256, SUBLANES)

    def fn(i, x_ref, g3_ref, dh_ref, dx2_ref, m_ref, g2_ref, dx1_ref, dm_ref, dg3_ref, dg2_ref):
        d, dg3 = _rms_bwd(x_ref[...], g3_ref[...], dh_ref[...])
        dx1 = dx2_ref[...] + d
        dx1_ref[...] = dx1
        dm, dg2 = _rms_bwd(m_ref[...], g2_ref[...], dx1)
        dm_ref[...] = dm.astype(BF16)
        _acc(i, dg3_ref, dg3)
        _acc(i, dg2_ref, dg2)

    r, f = _rb(tq, D), _full((1, D))
    return _rows("mid_bwd", fn, S, tq, (x1, g3, dh2, dx2, mix, g2), [r, f, r, r, r, f],
                 (_sds((S, D), F32), _sds((S, D), BF16), _sds((1, D), F32), _sds((1, D), F32)), (r, r, f, f))


def _pre_norm_bwd(x, g1, dh, dx1):
    S, D = x.shape
    tq = _tile(S, 512, SUBLANES)

    def fn(i, x_ref, g_ref, dh_ref, dx1_ref, dx_ref, dg_ref):
        d, dg = _rms_bwd(x_ref[...], g_ref[...], dh_ref[...])
        dx_ref[...] = dx1_ref[...] + d
        _acc(i, dg_ref, dg)

    r, f = _rb(tq, D), _full((1, D))
    return _rows("pre_norm_bwd", fn, S, tq, (x, g1, dh, dx1), [r, f, r, r], (_sds((S, D), F32), _sds((1, D), F32)), (r, f))


def _mix(proj, ya, yb, D, cb_a):
    S = ya.shape[0]
    tq = _tile(S, 256, SUBLANES)

    def fn(i, ga_ref, gb_ref, ya_ref, yb_ref, o_ref):
        o_ref[...] = (_sigmoid(ga_ref[...]) * ya_ref[...] + _sigmoid(gb_ref[...]) * yb_ref[...]).astype(BF16)

    return _rows("mix", fn, S, tq, (proj, proj, ya, yb), [_rb(tq, D, cb_a), _rb(tq, D, cb_a + 1), _rb(tq, D), _rb(tq, D)],
                 _sds((S, D), BF16), _rb(tq, D))


def _mix_bwd(proj, ya, yb, dmixin, D, cb_a):
    S = ya.shape[0]
    tq = _tile(S, 256, SUBLANES)

    def fn(i, ga_ref, gb_ref, ya_ref, yb_ref, d_ref, dya_ref, dyb_ref, dga_ref, dgb_ref):
        d = d_ref[...]
        sa, sb = _sigmoid(ga_ref[...]), _sigmoid(gb_ref[...])
        dya_ref[...] = (d * sa).astype(BF16)
        dyb_ref[...] = (d * sb).astype(BF16)
        dga_ref[...] = (d * ya_ref[...] * sa * (1.0 - sa)).astype(BF16)
        dgb_ref[...] = (d * yb_ref[...] * sb * (1.0 - sb)).astype(BF16)

    r = _rb(tq, D)
    o = _sds((S, D), BF16)
    return _rows("mix_bwd", fn, S, tq, (proj, proj, ya, yb, dmixin), [_rb(tq, D, cb_a), _rb(tq, D, cb_a + 1), r, r, r],
                 (o, o, o, o), (r, r, r, r))


def _shift_down(xe, k, tq):
    return pltpu.roll(xe, k, 0)[SUBLANES:SUBLANES + tq]


def _conv_pre(cur_ref, halo_ref, w_ref, i, tq):
    x = cur_ref[...]
    halo = jnp.where(i > 0, halo_ref[...], 0.0)
    xe = jnp.concatenate([halo, x], axis=0)
    xs = [x] + [_shift_down(xe, k, tq) for k in range(1, DN_CONV)]
    w = w_ref[...]
    c = sum(w[DN_CONV - 1 - k:DN_CONV - k, :] * xs[k] for k in range(DN_CONV))
    return c, xs


def _dn_prep(proj, conv_w, W):
    S = proj.shape[0]
    tq = _tile(S, 256, SUBLANES)
    hb = tq // SUBLANES

    def body(cur_ref, halo_ref, w_ref, o_ref):
        j, i = pl.program_id(0), pl.program_id(1)
        c, _ = _conv_pre(cur_ref, halo_ref, w_ref, i, tq)
        y = c * _sigmoid(c)
        scale = jnp.where(j == 0, DN_DK ** -0.5, 1.0)
        for h in range(W // DN_DK):
            sl = slice(h * DN_DK, (h + 1) * DN_DK)
            yh = y[:, sl]
            rs = lax.rsqrt(jnp.sum(yh * yh, axis=-1, keepdims=True) + EPS)
            o_ref[:, sl] = jnp.where(j == 2, yh, yh * rs * scale)

    return _pcall(body, name="dn_prep", grid=(3, S // tq),
                  in_specs=[pl.BlockSpec((tq, W), lambda j, i: (i, j)),
                            pl.BlockSpec((SUBLANES, W), lambda j, i: (jnp.maximum(i * hb - 1, 0), j)),
                            pl.BlockSpec((DN_CONV, W), lambda j, i: (0, j))],
                  out_specs=pl.BlockSpec((tq, W), lambda j, i: (i, j)), out_shape=_sds((S, 3 * W), F32),
                  compiler_params=_cp("arbitrary", "arbitrary"))(proj, proj, conv_w)


def _dn_prep_bwd_a(proj, conv_w, dqkv, W):
    S = proj.shape[0]
    tq = _tile(S, 256, SUBLANES)
    hb = tq // SUBLANES

    def body(cur_ref, halo_ref, w_ref, d_ref, dc_ref, dw_ref):
        j, i = pl.program_id(0), pl.program_id(1)
        c, xs = _conv_pre(cur_ref, halo_ref, w_ref, i, tq)
        sg = _sigmoid(c)
        y = c * sg
        scale = jnp.where(j == 0, DN_DK ** -0.5, 1.0)
        dout = d_ref[0]
        dys = []
        for h in range(W // DN_DK):
            sl = slice(h * DN_DK, (h + 1) * DN_DK)
            yh, dh = y[:, sl], dout[:, sl]
            rs = lax.rsqrt(jnp.sum(yh * yh, axis=-1, keepdims=True) + EPS)
            yn = yh * rs
            dn = scale * rs * (dh - yn * jnp.sum(dh * yn, axis=-1, keepdims=True))
            dys.append(jnp.where(j == 2, dh, dn))
        dy = jnp.concatenate(dys, axis=1)
        dc = dy * (sg * (1.0 + c * (1.0 - sg)))
        dc_ref[...] = dc
        dw = jnp.concatenate([jnp.sum(dc * xs[DN_CONV - 1 - r], axis=0, keepdims=True) for r in range(DN_CONV)], axis=0)
        _acc(i, dw_ref, dw)

    return _pcall(body, name="dn_prep_bwd_a", grid=(3, S // tq),
                  in_specs=[pl.BlockSpec((tq, W), lambda j, i: (i, j)),
                            pl.BlockSpec((SUBLANES, W), lambda j, i: (jnp.maximum(i * hb - 1, 0), j)),
                            pl.BlockSpec((DN_CONV, W), lambda j, i: (0, j)),
                            pl.BlockSpec((1, tq, W), lambda j, i: (j, i, 0))],
                  out_specs=(pl.BlockSpec((tq, W), lambda j, i: (i, j)), pl.BlockSpec((DN_CONV, W), lambda j, i: (0, j))),
                  out_shape=(_sds((S, 3 * W), F32), _sds((DN_CONV, 3 * W), F32)),
                  compiler_params=_cp("arbitrary", "arbitrary"))(proj, proj, conv_w, dqkv)


def _dn_prep_bwd_b(dc, conv_w, W):
    S = dc.shape[0]
    tq = _tile(S, 256, SUBLANES)
    hb = tq // SUBLANES
    nblk = S // tq

    def body(cur_ref, nxt_ref, w_ref, o_ref):
        i = pl.program_id(1)
        d = cur_ref[...]
        nxt = jnp.where(i < nblk - 1, nxt_ref[...], 0.0)
        de = jnp.concatenate([d, nxt], axis=0)
        w = w_ref[...]
        out = w[DN_CONV - 1:DN_CONV, :] * d
        for k in range(1, DN_CONV):
            out = out + w[DN_CONV - 1 - k:DN_CONV - k, :] * pltpu.roll(de, tq + SUBLANES - k, 0)[0:tq]
        o_ref[...] = out.astype(BF16)

    return _pcall(body, name="dn_prep_bwd_b", grid=(3, nblk),
                  in_specs=[pl.BlockSpec((tq, W), lambda j, i: (i, j)),
                            pl.BlockSpec((SUBLANES, W), lambda j, i: (jnp.minimum((i + 1) * hb, S // SUBLANES - 1), j)),
                            pl.BlockSpec((DN_CONV, W), lambda j, i: (0, j))],
                  out_specs=pl.BlockSpec((tq, W), lambda j, i: (i, j)), out_shape=_sds((S, 3 * W), BF16),
                  compiler_params=_cp("arbitrary", "arbitrary"))(dc, dc, conv_w)


def _gate_terms(ba, al, dt):
    u = ba + dt
    sp = jnp.maximum(u, 0.0) + jnp.log(1.0 + jnp.exp(-jnp.abs(u)))
    return _sigmoid(ba), -jnp.exp(al) * sp, u


def _dn_gates(proj, alog_row, dt_row, H, cb_ba):
    S = proj.shape[0]
    tq = _tile(S, 512, SUBLANES)
    W = H * DN_DK

    def fn(i, ba_ref, al_ref, dt_ref, be_ref, g_ref):
        bet, gg, _ = _gate_terms(ba_ref[...], al_ref[...], dt_ref[...])
        for h in range(H):
            sl = slice(h * DN_DK, (h + 1) * DN_DK)
            be_ref[:, sl] = jnp.broadcast_to(bet[:, h:h + 1], (tq, DN_DK))
            g_ref[:, sl] = jnp.broadcast_to(gg[:, H + h:H + h + 1], (tq, DN_DK))

    return _rows("dn_gates", fn, S, tq, (proj, alog_row, dt_row), [_rb(tq, LANES, cb_ba), _full((1, LANES)), _full((1, LANES))],
                 (_sds((S, W), F32), _sds((S, W), F32)), (_rb(tq, W), _rb(tq, W)))


def _dn_gates_bwd(proj, alog_row, dt_row, dbeta_b, dg_b, H, cb_ba):
    S = proj.shape[0]
    tq = _tile(S, 512, SUBLANES)
    W = H * DN_DK

    def fn(i, ba_ref, al_ref, dt_ref, db_ref, dg_ref, o_ref, dal_ref, ddt_ref):
        bet, gg, u = _gate_terms(ba_ref[...], al_ref[...], dt_ref[...])
        lane = lax.broadcasted_iota(jnp.int32, (tq, LANES), 1)
        d = jnp.zeros((tq, LANES), F32)
        for h in range(H):
            d = jnp.where(lane == h, db_ref[:, h * DN_DK:h * DN_DK + 1], d)
            d = jnp.where(lane == H + h, dg_ref[:, h * DN_DK:h * DN_DK + 1], d)
        is_a = (lane >= H) & (lane < 2 * H)
        da = jnp.where(is_a, d * (-jnp.exp(al_ref[...]) * _sigmoid(u)), 0.0)
        dlog = jnp.where(lane < H, d * bet * (1.0 - bet), da)
        o_ref[...] = jnp.concatenate([dlog, jnp.zeros((tq, BA_W - LANES), F32)], axis=1).astype(BF16)
        _acc(i, dal_ref, jnp.sum(jnp.where(is_a, d * gg, 0.0), axis=0, keepdims=True))
        _acc(i, ddt_ref, jnp.sum(da, axis=0, keepdims=True))

    f = _full((1, LANES))
    return _rows("dn_gates_bwd", fn, S, tq, (proj, alog_row, dt_row, dbeta_b, dg_b),
                 [_rb(tq, LANES, cb_ba), f, f, _rb(tq, W), _rb(tq, W)],
                 (_sds((S, BA_W), BF16), _sds((1, LANES), F32), _sds((1, LANES), F32)), (_rb(tq, BA_W), f, f))


def _dn_out(o, proj, ng, W, cb_z):
    S = o.shape[0]
    tq = _tile(S, 256, SUBLANES)

    def fn(i, o_ref, z_ref, g_ref, y_ref):
        for h in range(W // DN_DK):
            sl = slice(h * DN_DK, (h + 1) * DN_DK)
            z = z_ref[:, sl]
            y_ref[:, sl] = (_rms_fwd(o_ref[:, sl], g_ref[...]) * (z * _sigmoid(z))).astype(BF16)

    return _rows("dn_out", fn, S, tq, (o, proj, ng), [_rb(tq, W), _rb(tq, W, cb_z), _full((1, DN_DK))], _sds((S, W), BF16), _rb(tq, W))


def _dn_out_bwd(o, proj, ng, dy, W, cb_z):
    S = o.shape[0]
    tq = _tile(S, 256, SUBLANES)

    def fn(i, o_ref, z_ref, g_ref, d_ref, do_ref, dz_ref, dg_ref):
        g = g_ref[...]
        dg = jnp.zeros((1, DN_DK), F32)
        for h in range(W // DN_DK):
            sl = slice(h * DN_DK, (h + 1) * DN_DK)
            oh, z, d = o_ref[:, sl], z_ref[:, sl], d_ref[:, sl]
            sg = _sigmoid(z)
            dn = d * (z * sg)
            dz_ref[:, sl] = (d * _rms_fwd(oh, g) * (sg * (1.0 + z * (1.0 - sg)))).astype(BF16)
            dx, dgh = _rms_bwd(oh, g, dn)
            do_ref[:, sl] = dx
            dg = dg + dgh
        _acc(i, dg_ref, dg)

    r = _rb(tq, W)
    return _rows("dn_out_bwd", fn, S, tq, (o, proj, ng, dy), [r, _rb(tq, W, cb_z), _full((1, DN_DK)), r],
                 (_sds((S, W), F32), _sds((S, W), BF16), _sds((1, DN_DK), F32)), (r, r, _full((1, DN_DK))))


def _bdot(a, b, mode="nn"):
    return lax.dot_general(a.astype(BF16), b.astype(BF16), (_DIMS[mode], ((), ())), preferred_element_type=F32)


def _rsum(x):
    return jnp.broadcast_to(jnp.sum(x, axis=-1, keepdims=True), x.shape)


def _dot3(a, b, mode="nn"):
    ah, bh = a.astype(BF16), b.astype(BF16)
    al, bl = (a - ah.astype(F32)).astype(BF16), (b - bh.astype(F32)).astype(BF16)
    d = lambda x, y: lax.dot_general(x, y, (_DIMS[mode], ((), ())), preferred_element_type=F32)
    return d(ah, bh) + (d(al, bh) + d(ah, bl))


def _cumsum_rows(x, reverse=False):
    n = x.shape[0]
    row = lax.broadcasted_iota(jnp.int32, x.shape, 0)
    s = 1
    while s < n:
        if reverse:
            x = x + jnp.where(row < n - s, pltpu.roll(x, n - s, 0), 0.0)
        else:
            x = x + jnp.where(row >= s, pltpu.roll(x, s, 0), 0.0)
        s *= 2
    return x


def _each(f, *lists):
    return [f(*a) for a in zip(*lists)]


def _delta_local(qs, ks, vs, bes, grs):
    C = DN_CHUNK
    ri = lax.broadcasted_iota(jnp.int32, (C, C), 0)
    ci = lax.broadcasted_iota(jnp.int32, (C, C), 1)
    causal, strict = ri >= ci, ri > ci
    gcs = [_cumsum_rows(g) for g in grs]
    decays = [jnp.where(causal, jnp.exp(jnp.where(causal, gc[:, :C] - gc.T[:C, :], 0.0)), 0.0) for gc in gcs]
    egs = [jnp.exp(gc) for gc in gcs]
    eks = [jnp.exp(gc[C - 1:C, :] - gc) for gc in gcs]
    gams = [jnp.exp(gc[C - 1:C, :]) for gc in gcs]
    kbs = _each(lambda k, be: k * be, ks, bes)
    kks = _each(lambda kb, k: _dot3(kb, k, "nt"), kbs, ks)
    nls = _each(lambda kk, dc: jnp.where(strict, -kk * dc, 0.0), kks, decays)
    eye = (ri == ci).astype(F32)
    ts = [eye + nl for nl in nls]
    pws = [_dot3(nl, nl) for nl in nls]
    for s in range(5):
        ups = _each(_dot3, ts, pws)
        ts = _each(lambda t, u: t + u, ts, ups)
        if s < 4:
            pws = [_dot3(pw, pw) for pw in pws]
    vbs = _each(lambda v, be: v * be, vs, bes)
    kbes = _each(lambda kb, eg: kb * eg, kbs, egs)
    us, ws = _each(_dot3, ts, vbs), _each(_dot3, ts, kbes)
    qks = _each(lambda q, k: _bdot(q, k, "nt"), qs, ks)
    return dict(decay=decays, eg=egs, ek=eks, gam=gams, kb=kbs, kk=kks, t=ts, vb=vbs, kbe=kbes, u=us, w=ws, qk=qks,
                a=_each(lambda qk, dc: qk * dc, qks, decays), qd=_each(lambda q, eg: q * eg, qs, egs),
                kd=_each(lambda k, ek: k * ek, ks, eks), strict=strict)


def _delta_items(refs, CB, HB):
    C, dk = DN_CHUNK, DN_DK
    return [[r[c * C:(c + 1) * C, h * dk:(h + 1) * dk] for h in range(HB) for c in range(CB)] for r in refs]


def _delta_fwd(qkv, beta_b, g_b, H, CB, HB):
    S = qkv.shape[0]
    C, dk = DN_CHUNK, DN_DK
    N = S // C
    R = CB * C
    G = H // HB

    def body(q_ref, k_ref, v_ref, b_ref, g_ref, o_ref, st_ref, s_ref):
        @pl.when(pl.program_id(1) == 0)
        def _():
            s_ref[...] = jnp.zeros((HB, dk, dk), F32)

        L = _delta_local(*_delta_items((q_ref, k_ref, v_ref, b_ref, g_ref), CB, HB))
        ss = [s_ref[h] for h in range(HB)]
        for c in range(CB):
            it = [h * CB + c for h in range(HB)]
            for h in range(HB):
                st_ref[h, c] = ss[h]
            vns = [L["u"][i] - _bdot(L["w"][i], s) for i, s in zip(it, ss)]
            outs = [_bdot(L["qd"][i], s) + _bdot(L["a"][i], vn) for i, s, vn in zip(it, ss, vns)]
            ss = [s * L["gam"][i] + _bdot(L["kd"][i], vn, "tn") for i, s, vn in zip(it, ss, vns)]
            for h in range(HB):
                o_ref[c * C:(c + 1) * C, h * dk:(h + 1) * dk] = outs[h]
        for h in range(HB):
            s_ref[h] = ss[h]

    blk = lambda off: pl.BlockSpec((R, HB * dk), lambda h, n: (n, off + h))
    return _pcall(body, name="delta_fwd", grid=(G, N // CB),
                  in_specs=[blk(0), blk(G), blk(2 * G), blk(0), blk(0)],
                  out_specs=(blk(0), pl.BlockSpec((HB, CB, dk, dk), lambda h, n: (h, n, 0, 0))),
                  out_shape=(_sds((S, H * dk), F32), _sds((H, N, dk, dk), F32)),
                  scratch_shapes=[pltpu.VMEM((HB, dk, dk), F32)],
                  compiler_params=_cp("arbitrary", "arbitrary"))(qkv, qkv, qkv, beta_b, g_b)


def _delta_bwd(qkv, beta_b, g_b, states, do, H, CB, HB):
    S = qkv.shape[0]
    C, dk = DN_CHUNK, DN_DK
    N = S // C
    R = CB * C
    NB = N // CB
    G = H // HB

    def body(q_ref, k_ref, v_ref, b_ref, g_ref, st_ref, do_ref, dqkv_ref, db_ref, dg_ref, ds_ref):
        @pl.when(pl.program_id(1) == 0)
        def _():
            ds_ref[...] = jnp.zeros((HB, dk, dk), F32)

        qs, ks, vs, bes, grs, dos = _delta_items((q_ref, k_ref, v_ref, b_ref, g_ref, do_ref), CB, HB)
        L = _delta_local(qs, ks, vs, bes, grs)
        ts, decays, kbs, egs, eks, gams, qds, kds = (L[n] for n in ("t", "decay", "kb", "eg", "ek", "gam", "qd", "kd"))
        s0s = [st_ref[h, c] for h in range(HB) for c in range(CB)]
        vns = _each(lambda u, w, s0: u - _bdot(w, s0), L["u"], L["w"], s0s)
        pre_dvn = _each(lambda a, d: _bdot(a, d, "tn"), L["a"], dos)
        pre_ds = _each(lambda qd, d: _bdot(qd, d, "tn"), qds, dos)
        das = _each(lambda d, vn: _bdot(d, vn, "nt"), dos, vns)
        dqds = _each(lambda d, s0: _bdot(d, s0, "nt"), dos, s0s)
        ds = [ds_ref[h] for h in range(HB)]
        ds1s, dvns = [None] * (HB * CB), [None] * (HB * CB)
        for c in reversed(range(CB)):
            it = [h * CB + c for h in range(HB)]
            new = [pre_dvn[i] + _bdot(kds[i], d) for i, d in zip(it, ds)]
            for i, d, dv in zip(it, ds, new):
                ds1s[i], dvns[i] = d, dv
            ds = [pre_ds[i] + d * gams[i] - _bdot(L["w"][i], dv, "tn") for i, d, dv in zip(it, ds, new)]
        for h in range(HB):
            ds_ref[h] = ds[h]
        dkds = _each(lambda vn, d1: _bdot(vn, d1, "nt"), vns, ds1s)
        dgams = _each(lambda s0, d1: jnp.sum(jnp.sum(s0 * d1, axis=1, keepdims=True), axis=0, keepdims=True), s0s, ds1s)
        dws = _each(lambda dv, s0: -_bdot(dv, s0, "nt"), dvns, s0s)
        dvbs = _each(lambda t, dv: _dot3(t, dv, "tn"), ts, dvns)
        dkbes = _each(lambda t, dw: _dot3(t, dw, "tn"), ts, dws)
        dts = _each(lambda dv, vb, dw, kbe: _dot3(dv, vb, "nt") + _dot3(dw, kbe, "nt"), dvns, L["vb"], dws, L["kbe"])
        tmp = _each(lambda dt, t: _dot3(dt, t, "nt"), dts, ts)
        dls = _each(lambda t, x: -_dot3(t, x, "tn"), ts, tmp)
        ms = _each(lambda dl, dc: jnp.where(L["strict"], dl * dc, 0.0), dls, decays)
        mas = _each(lambda da, dc: da * dc, das, decays)
        dkbs = _each(lambda m, k, dkbe, eg: _bdot(m, k) + dkbe * eg, ms, ks, dkbes, egs)
        dks = _each(lambda m, kb, ma, q, dkd, ek, dkb, be: _bdot(m, kb, "tn") + _bdot(ma, q, "tn") + dkd * ek + dkb * be,
                    ms, kbs, mas, qs, dkds, eks, dkbs, bes)
        dqs = _each(lambda ma, k, dqd, eg: _bdot(ma, k) + dqd * eg, mas, ks, dqds, egs)
        es = _each(lambda m, kk, ma, qk: m * kk + ma * qk, ms, L["kk"], mas, L["qk"])
        ones = jnp.ones((C, dk), BF16)
        row = lax.broadcasted_iota(jnp.int32, (C, dk), 0)
        for i in range(HB * CB):
            h, c = divmod(i, CB)
            rs, cs = slice(c * C, (c + 1) * C), slice(h * dk, (h + 1) * dk)
            e = es[i]
            e_hi = e.astype(BF16)
            col = _bdot(e_hi, ones, "tn") + _bdot(e - e_hi.astype(F32), ones, "tn")
            t_kd = _rsum(dkds[i] * kds[i])
            dgc = (jnp.broadcast_to(jnp.sum(e, axis=1, keepdims=True), (C, dk)) - col + _rsum(dqds[i] * qds[i]) - t_kd
                   + _rsum(dkbes[i] * L["kbe"][i]))
            dglast = jnp.sum(t_kd[:, 0:1], axis=0, keepdims=True) + dgams[i] * gams[i][:, 0:1]
            dgc = dgc + jnp.where(row == C - 1, dglast, 0.0)
            dqkv_ref[0, rs, cs] = dqs[i]
            dqkv_ref[1, rs, cs] = dks[i]
            dqkv_ref[2, rs, cs] = dvbs[i] * bes[i]
            db_ref[rs, cs] = _rsum(dkbs[i] * ks[i]) + _rsum(dvbs[i] * vs[i])
            dg_ref[rs, cs] = _cumsum_rows(dgc, reverse=True)

    blk = lambda off: pl.BlockSpec((R, HB * dk), lambda h, n: (NB - 1 - n, off + h))
    W = H * dk
    return _pcall(body, name="delta_bwd", grid=(G, NB),
                  in_specs=[blk(0), blk(G), blk(2 * G), blk(0), blk(0),
                            pl.BlockSpec((HB, CB, dk, dk), lambda h, n: (h, NB - 1 - n, 0, 0)), blk(0)],
                  out_specs=(pl.BlockSpec((3, R, HB * dk), lambda h, n: (0, NB - 1 - n, h)), blk(0), blk(0)),
                  out_shape=(_sds((3, S, W), F32), _sds((S, W), F32), _sds((S, W), F32)),
                  scratch_shapes=[pltpu.VMEM((HB, dk, dk), F32)],
                  compiler_params=_cp("arbitrary", "arbitrary"))(qkv, qkv, qkv, beta_b, g_b, states, do)


def _rope_consts():
    lane = np.arange(LANES) % SW_HD
    half = ROT_DIM // 2
    inv = (ROPE_THETA ** (-np.arange(half, dtype=np.float32) * np.float32(2.0 / ROT_DIM))).astype(np.float32)
    freq = np.where(lane < ROT_DIM, inv[lane % half], 0.0).astype(np.float32)
    lo = (lane < half).astype(np.float32)
    hi = ((lane >= half) & (lane < ROT_DIM)).astype(np.float32)
    return jnp.asarray(np.stack([freq, -lo, hi] + [np.zeros(LANES, np.float32)] * 5))


def _rope_tables(pos_col):
    S = pos_col.shape[0]
    tq = _tile(S, 1024, SUBLANES)

    def fn(i, p_ref, c_ref, cos_ref, s1_ref, s2_ref):
        ang = p_ref[...].astype(F32) * c_ref[0:1, :]
        sn = jnp.sin(ang)
        cos_ref[...] = jnp.cos(ang)
        s1_ref[...] = sn * c_ref[1:2, :]
        s2_ref[...] = sn * c_ref[2:3, :]

    o, r = _sds((S, LANES), F32), _rb(tq, LANES)
    return _rows("rope_tables", fn, S, tq, (pos_col, _rope_consts()), [_rb(tq, 1), _full((SUBLANES, LANES))], (o, o, o), (r, r, r))


def _wide(a, w):
    return a if w == LANES else jnp.tile(a, (1, w // LANES))


def _rope(x, cos, s1, s2):
    w, h = x.shape[1], ROT_DIM // 2
    return x * _wide(cos, w) + pltpu.roll(x, w - h, 1) * _wide(s1, w) + pltpu.roll(x, h, 1) * _wide(s2, w)


def _unrope(d, cos, s1, s2):
    w, h = d.shape[1], ROT_DIM // 2
    return d * _wide(cos, w) + pltpu.roll(d * _wide(s1, w), h, 1) + pltpu.roll(d * _wide(s2, w), w - h, 1)


def _swa_setup(n, q_ref, kc_ref, kp_ref, vc_ref, vp_ref, tc, tp):
    B = SW_BLOCK
    qr = _rope(q_ref[...], tc[0][...], tc[1][...], tc[2][...]) * (SW_HD ** -0.5)
    kw = jnp.concatenate([_rope(kp_ref[...], tp[0][...], tp[1][...], tp[2][...]),
                          _rope(kc_ref[...], tc[0][...], tc[1][...], tc[2][...])], axis=0)
    vw = jnp.concatenate([vp_ref[...], vc_ref[...]], axis=0)
    lane = lax.broadcasted_iota(jnp.int32, (2 * B, LANES), 1)
    heads = []
    for hk in range(SW_KV_HEADS):
        kh, vh = kw[:, hk * SW_HD:(hk + 1) * SW_HD], vw[:, hk * SW_HD:(hk + 1) * SW_HD]
        kk, vv = jnp.concatenate([kh, kh], axis=1), jnp.concatenate([vh, vh], axis=1)
        heads.append(tuple(jnp.where(sel, t, 0.0).astype(BF16) for t in (kk, vv) for sel in (lane < SW_HD, lane >= SW_HD)))
    qi = lax.broadcasted_iota(jnp.int32, (B, 2 * B), 0) + B
    ki = lax.broadcasted_iota(jnp.int32, (B, 2 * B), 1)
    off = qi - ki
    ok = (off >= 0) & (off < SW_BLOCK) & ((ki >= B) | (n > 0))
    return qr, heads, jnp.where(ok, 0.0, -1e30), lane


SWA_GROUPS = 2


def _swa_probs(items, qs, heads, bias, sk_ref, G2):
    ss = [_bdot(qs[j], heads[j // G2][half], "nt") + bias for j, half in items]
    sks = [sk_ref[0:1, 2 * j + half:2 * j + half + 1] for j, half in items]
    ms = [jnp.maximum(jnp.max(s, axis=-1, keepdims=True), sk) for s, sk in zip(ss, sks)]
    ps = [jnp.exp(s - m) for s, m in zip(ss, ms)]
    es = [jnp.exp(sk - m) for sk, m in zip(sks, ms)]
    inv = [1.0 / (jnp.sum(p, axis=-1, keepdims=True) + e) for p, e in zip(ps, es)]
    return [p * i for p, i in zip(ps, inv)], [e * i for e, i in zip(es, inv)]


def _swa_specs(W, cb_q, cb_k):
    B = SW_BLOCK
    cur = lambda w, cb: pl.BlockSpec((B, w), lambda n: (n, cb))
    prv = lambda w, cb: pl.BlockSpec((B, w), lambda n: (jnp.maximum(n - 1, 0), cb))
    specs = [cur(W, cb_q), cur(LANES, cb_k), prv(LANES, cb_k), cur(LANES, cb_k + 1), prv(LANES, cb_k + 1)]
    return specs + [cur(LANES, 0)] * 3 + [prv(LANES, 0)] * 3 + [_full((1, LANES))]


def _swa_fwd(proj, tabs, sinks_row, W, cb_q, cb_k):
    S = proj.shape[0]
    G2 = SW_Q_HEADS // SW_KV_HEADS // 2

    def body(q_ref, kc_ref, kp_ref, vc_ref, vp_ref, c0, c1, c2, p0, p1, p2, sk_ref, o_ref):
        n = pl.program_id(0)
        qr, heads, bias, _ = _swa_setup(n, q_ref, kc_ref, kp_ref, vc_ref, vp_ref, (c0, c1, c2), (p0, p1, p2))
        qs = [qr[:, j * LANES:(j + 1) * LANES].astype(BF16) for j in range(W // LANES)]
        for j0 in range(0, W // LANES, SWA_GROUPS):
            items = [(j, half) for j in range(j0, j0 + SWA_GROUPS) for half in range(2)]
            probs, _ = _swa_probs(items, qs, heads, bias, sk_ref, G2)
            pv = [_bdot(p, heads[j // G2][2 + half]) for p, (j, half) in zip(probs, items)]
            for g in range(SWA_GROUPS):
                o_ref[:, (j0 + g) * LANES:(j0 + g + 1) * LANES] = (pv[2 * g] + pv[2 * g + 1]).astype(BF16)

    t = tuple(tabs)
    return _pcall(body, name="swa_fwd", grid=(S // SW_BLOCK,), in_specs=_swa_specs(W, cb_q, cb_k),
                  out_specs=pl.BlockSpec((SW_BLOCK, W), lambda n: (n, 0)), out_shape=_sds((S, W), BF16),
                  compiler_params=_cp("arbitrary"))(proj, proj, proj, proj, proj, *t, *t, sinks_row)


def _swa_bwd(proj, tabs, sinks_row, do, W, cb_q, cb_k):
    S = proj.shape[0]
    B = SW_BLOCK
    G2 = SW_Q_HEADS // SW_KV_HEADS // 2
    SKR = -(-SW_Q_HEADS // SUBLANES) * SUBLANES

    def body(q_ref, kc_ref, kp_ref, vc_ref, vp_ref, c0, c1, c2, p0, p1, p2, sk_ref, do_ref,
             dq_ref, dkc_ref, dkp_ref, dvc_ref, dvp_ref, dsk_ref):
        n = pl.program_id(0)
        qr, heads, bias, lane = _swa_setup(n, q_ref, kc_ref, kp_ref, vc_ref, vp_ref, (c0, c1, c2), (p0, p1, p2))

        @pl.when(n == 0)
        def _():
            dsk_ref[...] = jnp.zeros((SKR, LANES), F32)

        acc_k = [jnp.zeros((2 * B, LANES), F32) for _ in range(SW_KV_HEADS)]
        acc_v = [jnp.zeros((2 * B, LANES), F32) for _ in range(SW_KV_HEADS)]
        qs = [qr[:, j * LANES:(j + 1) * LANES].astype(BF16) for j in range(W // LANES)]
        dos = [do_ref[:, j * LANES:(j + 1) * LANES].astype(BF16) for j in range(W // LANES)]
        dqs = []
        for j0 in range(0, W // LANES, SWA_GROUPS):
            items = [(j, half) for j in range(j0, j0 + SWA_GROUPS) for half in range(2)]
            probs, psinks = _swa_probs(items, qs, heads, bias, sk_ref, G2)
            dps = [_bdot(dos[j], heads[j // G2][2 + half], "nt") for j, half in items]
            deltas = [jnp.sum(p * dp, axis=-1, keepdims=True) for p, dp in zip(probs, dps)]
            dss = [(p * (dp - dl)).astype(BF16) for p, dp, dl in zip(probs, dps, deltas)]
            pbs = [p.astype(BF16) for p in probs]
            dqp = [_bdot(ds, heads[j // G2][half]) for ds, (j, half) in zip(dss, items)]
            dkk = [_bdot(ds, qs[j], "tn") for ds, (j, half) in zip(dss, items)]
            dvv = [_bdot(p, dos[j], "tn") for p, (j, half) in zip(pbs, items)]
            for i, (j, half) in enumerate(items):
                hk, h = j // G2, 2 * j + half
                sel = (lane < SW_HD) if half == 0 else (lane >= SW_HD)
                acc_k[hk] = acc_k[hk] + jnp.where(sel, dkk[i], 0.0)
                acc_v[hk] = acc_v[hk] + jnp.where(sel, dvv[i], 0.0)
                dsk_ref[h:h + 1, :] += jnp.broadcast_to(-jnp.sum(psinks[i] * deltas[i], axis=0, keepdims=True), (1, LANES))
            dqs += [dqp[2 * g] + dqp[2 * g + 1] for g in range(SWA_GROUPS)]
        dq = jnp.concatenate(dqs, axis=1) * (SW_HD ** -0.5)
        dq_ref[...] = _unrope(dq, c0[...], c1[...], c2[...]).astype(BF16)
        fold = lambda a: a[:, :SW_HD] + a[:, SW_HD:]
        dkw = jnp.concatenate([fold(a) for a in acc_k], axis=1)
        dvw = jnp.concatenate([fold(a) for a in acc_v], axis=1)
        dkp_ref[...], dkc_ref[...] = dkw[:B], dkw[B:]
        dvp_ref[...], dvc_ref[...] = dvw[:B], dvw[B:]

    t = tuple(tabs)
    blk = lambda w: pl.BlockSpec((B, w), lambda n: (n, 0))
    o = _sds((S, LANES), F32)
    return _pcall(body, name="swa_bwd", grid=(S // B,), in_specs=_swa_specs(W, cb_q, cb_k) + [blk(W)],
                  out_specs=(blk(W), blk(LANES), blk(LANES), blk(LANES), blk(LANES), _full((SKR, LANES))),
                  out_shape=(_sds((S, W), BF16), o, o, o, o, _sds((SKR, LANES), F32)),
                  compiler_params=_cp("arbitrary"))(proj, proj, proj, proj, proj, *t, *t, sinks_row, do)


def _swa_kv_combine(dkc, dkp, dvc, dvp, tabs):
    S = dkc.shape[0]
    B = SW_BLOCK
    nb = S // B

    def fn(n, kc_ref, kp_ref, vc_ref, vp_ref, c0, c1, c2, o_ref):
        more = n < nb - 1
        dk = kc_ref[...] + jnp.where(more, kp_ref[...], 0.0)
        dv = vc_ref[...] + jnp.where(more, vp_ref[...], 0.0)
        o_ref[...] = jnp.concatenate([_unrope(dk, c0[...], c1[...], c2[...]), dv], axis=1).astype(BF16)

    cur = _rb(B, LANES)
    nxt = pl.BlockSpec((B, LANES), lambda n: (jnp.minimum(n + 1, nb - 1), 0))
    return _rows("swa_kv_combine", fn, S, B, (dkc, dkp, dvc, dvp, *tabs), [cur, nxt, cur, nxt, cur, cur, cur],
                 _sds((S, 2 * LANES), BF16), _rb(B, 2 * LANES))


ANY = pl.BlockSpec(memory_space=pl.ANY)


def _place():
    x, y, c = lax.axis_index("x"), lax.axis_index("y"), lax.axis_index("c")
    return x, y, c, [(1 - x, y), (x, 1 - y), (1 - x, 1 - y)]


def _comm_call(name, body, out_shapes, n_sems, n_local, *ins):
    return _pcall(body, name=name, out_shape=tuple(out_shapes), in_specs=[ANY] * len(ins), out_specs=tuple(ANY for _ in out_shapes),
                  scratch_shapes=[pltpu.SemaphoreType.DMA((n_sems,)), pltpu.SemaphoreType.DMA((n_sems,)),
                                  pltpu.SemaphoreType.DMA((n_local,))])(*ins)


def _remote(src, dst, send, recv, k, to):
    return pltpu.make_async_remote_copy(src_ref=src, dst_ref=dst, send_sem=send.at[k], recv_sem=recv.at[k], device_id=to,
                                        device_id_type=MESH)


def _gather_chips(name, arrs):
    n = len(arrs)
    Lh = arrs[0].shape[0] // 2

    def body(*refs):
        w, o, (send, recv, lsem) = refs[:n], refs[n:2 * n], refs[2 * n:]
        x, y, c, chips = _place()
        me, sib = 2 * x + y, (x, y, 1 - c)
        own, other = pl.ds(c * Lh, Lh), pl.ds((1 - c) * Lh, Lh)
        idx = [2 * cx + cy for cx, cy in chips]
        local = [pltpu.make_async_copy(w[a], o[a].at[me], lsem.at[a]) for a in range(n)]
        first = [[_remote(w[a].at[own], o[a].at[me, own], send, recv, 6 * a + j, (*chips[j], c)) for j in range(3)] for a in range(n)]
        passed = [[_remote(o[a].at[idx[j], own], o[a].at[idx[j], own], send, recv, 6 * a + 3 + j, sib) for j in range(3)] for a in range(n)]
        for cp in local + [cp for row in first for cp in row]:
            cp.start()
        for j in range(3):
            for a in range(n):
                _remote(w[a].at[own], o[a].at[idx[j], own], send, recv, 6 * a + j, (*chips[j], c)).wait_recv()
                passed[a][j].start()
        for j in range(3):
            for a in range(n):
                _remote(w[a].at[other], o[a].at[idx[j], other], send, recv, 6 * a + 3 + j, sib).wait_recv()
        for cp in [cp for row in first + passed for cp in row]:
            cp.wait_send()
        for cp in local:
            cp.wait()

    return _comm_call(name, body, [_sds((4,) + a.shape, a.dtype) for a in arrs], 6 * n, n, *arrs)


def _pair_swap(name, arrs):
    n = len(arrs)
    Lh = arrs[0].shape[0] // 2

    def body(*refs):
        g, o, (send, recv, _) = refs[:n], refs[n:2 * n], refs[2 * n:]
        x, y, c, _ = _place()
        cps = [_remote(g[a].at[pl.ds((1 - c) * Lh, Lh)], o[a], send, recv, a, (x, y, 1 - c)) for a in range(n)]
        for cp in cps:
            cp.start()
        for cp in cps:
            cp.wait()

    return _comm_call(name, body, [_sds((Lh,) + a.shape[1:], a.dtype) for a in arrs], n, 1, *arrs)


def _chip_slice(ref, axis, s):
    if axis is None:
        return ref.at[s]
    q = ref.shape[axis] // 4
    start = s * q if isinstance(s, int) else pl.multiple_of(s * q, q)
    return ref.at[tuple([slice(None)] * axis + [pl.ds(start, q)])]


def _scatter_chips(name, items):
    n = len(items)
    part = lambda a, ax: a.shape[1:] if ax is None else tuple(d // 4 if i == ax else d for i, d in enumerate(a.shape))

    def body(*refs):
        p, o, (send, recv, lsem) = refs[:n], refs[n:2 * n], refs[2 * n:]
        x, y, c, chips = _place()
        me = 2 * x + y
        idx = [2 * cx + cy for cx, cy in chips]
        local = [pltpu.make_async_copy(_chip_slice(p[a], items[a][1], me), o[a].at[me], lsem.at[a]) for a in range(n)]
        cps = [_remote(_chip_slice(p[a], items[a][1], idx[j]), o[a].at[me], send, recv, 3 * a + j, (*chips[j], c))
               for a in range(n) for j in range(3)]
        for cp in local + cps:
            cp.start()
        for a in range(n):
            for j in range(3):
                _remote(_chip_slice(p[a], items[a][1], me), o[a].at[idx[j]], send, recv, 3 * a + j, (*chips[j], c)).wait_recv()
        for cp in cps:
            cp.wait_send()
        for cp in local:
            cp.wait()

    return _comm_call(name, body, [_sds((4,) + part(a, ax), a.dtype) for a, ax in items], 3 * n, n, *[a for a, _ in items])


def _pair_gather(name, arrs):
    n = len(arrs)
    Lh = arrs[0].shape[0]

    def body(*refs):
        h, o, (send, recv, lsem) = refs[:n], refs[n:2 * n], refs[2 * n:]
        x, y, c, _ = _place()
        own = pl.ds(c * Lh, Lh)
        local = [pltpu.make_async_copy(h[a], o[a].at[own], lsem.at[a]) for a in range(n)]
        cps = [_remote(h[a], o[a].at[own], send, recv, a, (x, y, 1 - c)) for a in range(n)]
        for cp in local + cps:
            cp.start()
        for cp in cps:
            cp.wait()
        for cp in local:
            cp.wait()

    return _comm_call(name, body, [_sds((2 * Lh,) + a.shape[1:], a.dtype) for a in arrs], n, n, *arrs)


def _gather_all(name, b):
    R, C = b.shape
    flips = [(dx, dy, dc) for dx in (0, 1) for dy in (0, 1) for dc in (0, 1)][1:]

    def body(b_ref, o_ref, send, recv, lsem):
        x, y, c, _ = _place()
        me = 4 * x + 2 * y + c
        peers = [(x ^ dx, y ^ dy, c ^ dc) for dx, dy, dc in flips]
        mine = pltpu.make_async_copy(b_ref, o_ref.at[me], lsem.at[0])
        mine.start()
        cps = [_remote(b_ref, o_ref.at[me], send, recv, k, peer) for k, peer in enumerate(peers)]
        for cp in cps:
            cp.start()
        for k, (px, py, pc) in enumerate(peers):
            _remote(b_ref, o_ref.at[4 * px + 2 * py + pc], send, recv, k, (px, py, pc)).wait_recv()
        for cp in cps:
            cp.wait_send()
        mine.wait()

    return _comm_call(name, body, [_sds((8, R, C), b.dtype)], 7, 1, b)[0]


def _block_rows(rows, width):
    return _tile(rows, max(SUBLANES, (1 << 19) // width), SUBLANES)


def _add_half(name, g, got):
    L, A, B = g.shape
    Lh = L // 2
    tq = _block_rows(A, B)

    def body(c_ref, g_ref, r_ref, o_ref):
        o_ref[...] = (g_ref[...] + r_ref[...]).astype(BF16)

    spec = pltpu.PrefetchScalarGridSpec(
        num_scalar_prefetch=1, grid=(Lh, A // tq),
        in_specs=[pl.BlockSpec((1, tq, B), lambda l, i, c_ref: (c_ref[0] * Lh + l, i, 0)),
                  pl.BlockSpec((1, tq, B), lambda l, i, c_ref: (l, i, 0))],
        out_specs=pl.BlockSpec((1, tq, B), lambda l, i, c_ref: (l, i, 0)))
    return _pcall(body, name=name, grid_spec=spec, out_shape=_sds((Lh, A, B), BF16),
                  compiler_params=_cp("arbitrary", "arbitrary"))(lax.axis_index("c").reshape(1).astype(jnp.int32), g, got)


def _sum_slots(name, a):
    n, R, C = a.shape
    tq = _block_rows(R, n * C)

    def fn(i, a_ref, o_ref):
        t = a_ref[0].astype(F32)
        for s in range(1, n):
            t = t + a_ref[s].astype(F32)
        o_ref[...] = t

    return _rows(name, fn, R, tq, (a,), [pl.BlockSpec((n, tq, C), lambda i: (0, i, 0))], _sds((R, C), F32), _rb(tq, C))


def _adamw(name, w, g, m, v):
    R, C = w.shape
    tq = _tile(R, 256, SUBLANES)

    def fn(i, w_ref, g_ref, m_ref, v_ref, d_ref, mo_ref, vo_ref):
        gg = g_ref[...]
        mn = ADAM_B1 * m_ref[...] + (1.0 - ADAM_B1) * gg
        vn = ADAM_B2 * v_ref[...] + (1.0 - ADAM_B2) * (gg * gg)
        mo_ref[...] = mn
        vo_ref[...] = vn
        m_hat = mn / (1.0 - ADAM_B1 ** ADAM_STEP)
        v_hat = vn / (1.0 - ADAM_B2 ** ADAM_STEP)
        d_ref[...] = -ADAM_LR * (m_hat / (jnp.sqrt(v_hat) + ADAM_EPS) + ADAM_WD * w_ref[...])

    r, o = _rb(tq, C), _sds((R, C), F32)
    return _rows(name, fn, R, tq, (w, g, m, v), [r, r, r, r], (o, o, o), (r, r, r))


def _pack(arrs, width, lead=()):
    nl = len(lead)
    flat = jnp.concatenate([a.reshape(lead + (-1,)) for a in arrs], axis=nl)
    n = flat.shape[-1]
    unit = PACK_ROWS * width
    tot = -(-n // unit) * unit
    flat = jnp.pad(flat, [(0, 0)] * nl + [(0, tot - n)])
    return flat.reshape(lead + (tot // width, width))


def _unpack(buf, shapes, lead=()):
    flat = buf.reshape(lead + (-1,))
    out, off = [], 0
    for s in shapes:
        n = int(np.prod(s))
        out.append(flat[..., off:off + n].reshape(lead + tuple(s)))
        off += n
    return out


def _in_groups(W, H):
    o_sq = 4 * W + 2 * H
    o_k = o_sq + W
    o_g = o_k + 2 * KV_W
    return [(0, 4 * W), (o_sq, o_k), (o_g, o_g + 2 * W), (o_k, o_g), (4 * W, o_sq)]


def _relayout_in(shards, W, H):
    c4 = sum(hi - lo for lo, hi in _in_groups(W, H)) // 4
    parts = []
    for lo, hi in _in_groups(W, H):
        for s in range(4):
            a, b = max(lo, s * c4), min(hi, (s + 1) * c4)
            if a < b:
                parts.append(shards[s][:, a - s * c4:b - s * c4])
    parts.append(jnp.zeros((shards.shape[1], BA_W - 2 * H), shards.dtype))
    return jnp.concatenate(parts, axis=1)


def _lane_pad(n):
    return -(-n // LANES) * LANES


def _shard_in(d, W, H):
    groups = _in_groups(W, H)
    starts = [sum(hi - lo for lo, hi in groups[:i]) for i in range(len(groups))]
    stored = sorted(zip(groups, starts))
    c4 = sum(hi - lo for lo, hi in groups) // 4
    out = []
    for s in range(4):
        parts = []
        for (lo, hi), at in stored:
            a, b = max(lo, s * c4), min(hi, (s + 1) * c4)
            if a < b:
                parts.append(d[:, :, at + a - lo:at + b - lo])
        parts.append(jnp.zeros(d.shape[:2] + (_lane_pad(c4) - c4,), d.dtype))
        out.append(jnp.concatenate(parts, axis=2))
    return jnp.stack(out)


def _lane_row(vals, at):
    return jnp.pad(vals, (at, LANES - at - vals.shape[0]))[None]


def _layer_fwd(x, lw, tabs, W, H):
    D = x.shape[1]
    cbk = 7 * W // LANES
    h = _pre_norm(x, lw["g1"])
    proj = _mm("mm_in", h, lw["win"], "nn", F32, tn=768)
    qkv = _dn_prep(proj, lw["conv"], W)
    beta_b, g_b = _dn_gates(proj, lw["alog"], lw["dt"], H, cbk + 2)
    o, st = _delta_fwd(qkv, beta_b, g_b, H, DELTA_CB, DELTA_HB)
    oa = _dn_out(o, proj, lw["ng"], W, 3)
    ob = _swa_fwd(proj, tabs, lw["sinks"], W, 4, cbk)
    ya = _mm("mm_up_dn", oa, lw["wup_dn"], "nn", F32)
    yb = _mm("mm_up_sw", ob, lw["wup_sw"], "nn", F32)
    mixin = _mix(proj, ya, yb, D, 5)
    mix = _mm("mm_o", mixin, lw["wo"], "nn", F32)
    x1, h2 = _post_mix(x, mix, lw["g2"], lw["g3"])
    f1, act = _mm("mm_ff1", h2, lw["wff1"], "nn", out_dtypes=(F32, BF16), epi=lambda acc: (acc, jnp.square(jnp.maximum(acc, 0.0))))
    ff = _mm("mm_ff2", act, lw["wff2"], "nn", F32)
    x2 = _post_mlp(x1, ff, lw["g4"])
    saved = dict(x=x, h=h, proj=proj, qkv=qkv, beta_b=beta_b, g_b=g_b, o=o, st=st, oa=oa, ob=ob, ya=ya, yb=yb,
                 mixin=mixin, mix=mix, x1=x1, h2=h2, f1=f1, act=act, ff=ff)
    return x2, saved


def _layer_bwd(dx2, lw, sv, tabs, W, H, l, big):
    D = dx2.shape[1]
    cbk = 7 * W // LANES
    big = dict(big)
    dff, dg4 = _post_mlp_bwd(sv["ff"], lw["g4"], dx2)
    df1 = _mm("mm_ff2_dx", dff, lw["wff2"], "nt", BF16, extras=(sv["f1"],), epi=lambda acc, f1: (acc * 2.0 * jnp.maximum(f1, 0.0),))
    big["w_ff2"] = _mm("mm_ff2_dw", sv["act"], dff, "tn", slab=(big["w_ff2"], l))
    dh2 = _mm("mm_ff1_dx", df1, lw["wff1"], "nt", F32)
    big["w_ff1"] = _mm("mm_ff1_dw", sv["h2"], df1, "tn", slab=(big["w_ff1"], l))
    dx1, dmix, dg3, dg2 = _mid_bwd(sv["x1"], lw["g3"], dh2, dx2, sv["mix"], lw["g2"])
    dmixin = _mm("mm_o_dx", dmix, lw["wo"], "nt", F32)
    big["w_o"] = _mm("mm_o_dw", sv["mixin"], dmix, "tn", slab=(big["w_o"], l))
    dya, dyb, dga, dgb = _mix_bwd(sv["proj"], sv["ya"], sv["yb"], dmixin, D, 5)
    doa = _mm("mm_up_dn_dx", dya, lw["wup_dn"], "nt", F32)
    big["w_up_dn"] = _mm("mm_up_dn_dw", sv["oa"], dya, "tn", slab=(big["w_up_dn"], l))
    dob = _mm("mm_up_sw_dx", dyb, lw["wup_sw"], "nt", F32)
    big["w_up_sw"] = _mm("mm_up_sw_dw", sv["ob"], dyb, "tn", slab=(big["w_up_sw"], l))
    do, dz, dng = _dn_out_bwd(sv["o"], sv["proj"], lw["ng"], doa, W, 3)
    dqkvn, dbeta_b, dg_b = _delta_bwd(sv["qkv"], sv["beta_b"], sv["g_b"], sv["st"], do, H, DELTA_CB, DELTA_HB)
    dba, dalog, ddt = _dn_gates_bwd(sv["proj"], lw["alog"], lw["dt"], dbeta_b, dg_b, H, cbk + 2)
    dc, dconv = _dn_prep_bwd_a(sv["proj"], lw["conv"], dqkvn, W)
    dqkv = _dn_prep_bwd_b(dc, lw["conv"], W)
    dq_sw, dkc, dkp, dvc, dvp, dsk = _swa_bwd(sv["proj"], tabs, lw["sinks"], dob, W, 4, cbk)
    dkv = _swa_kv_combine(dkc, dkp, dvc, dvp, tabs)
    dproj = jnp.concatenate([dqkv, dz, dq_sw, dga, dgb, dkv, dba], axis=1)
    dh = _mm("mm_in_dx", dproj, lw["win"], "nt", F32, tk=768)
    big["w_in"] = _mm("mm_in_dw", sv["h"], dproj, "tn", tn=768, slab=(big["w_in"], l))
    dx, dg1 = _pre_norm_bwd(sv["x"], lw["g1"], dh, dx1)
    grads = dict(pre_mix_g=dg1[0], dn_conv_w=dconv, dn_a_log=dalog[0, H:2 * H], dn_dt_bias=ddt[0, H:2 * H], dn_norm_g=dng[0],
                 sw_sinks=dsk[:SW_Q_HEADS, 0], post_mix_g=dg2[0], pre_mlp_g=dg3[0], post_mlp_g=dg4[0])
    return dx, grads, big


_WEIGHTS = ["pre_mix_g", "w_in", "dn_conv_w", "dn_a_log", "dn_dt_bias", "dn_norm_g", "sw_sinks", "w_up_dn", "w_up_sw", "w_o",
            "post_mix_g", "pre_mlp_g", "w_ff1", "w_ff2", "post_mlp_g"]
_BIG = {"w_in": 2, "w_up_dn": 1, "w_up_sw": 1, "w_o": 1, "w_ff1": 2, "w_ff2": 1}
_SMALL = [n for n in _WEIGHTS if n not in _BIG]


def _step(P):
    x, target = P["x"][0], P["loss_target"][0]
    S, D = x.shape
    L = P["pre_mix_g"].shape[0]
    H, W = DN_HEADS, DN_HEADS * DN_DK
    assert W == D == SW_Q_HEADS * SW_HD and KV_W == LANES
    me = 2 * lax.axis_index("x") + lax.axis_index("y")

    assert L % 2 == 0
    c4 = P["w_in"].shape[2]
    lane_padded = lambda a: jnp.pad(a, ((0, 0), (0, 0), (0, _lane_pad(a.shape[2]) - a.shape[2])))
    gathered = _gather_chips("weights_gather", [lane_padded(P[n].astype(BF16)) for n in _BIG] + [P["dn_conv_w"]])
    full = dict(zip(list(_BIG) + ["dn_conv_w"], gathered))
    rows = lambda n, l: full[n][:, l].reshape(-1, full[n].shape[-1])
    cols = lambda n, l: jnp.concatenate([full[n][s, l] for s in range(4)], axis=-1)

    tabs = _rope_tables(P["positions"].reshape(S, 1))
    lws = []
    for l in range(L):
        lws.append(dict(
            g1=P["pre_mix_g"][l][None], win=_relayout_in(full["w_in"][:, l], W, H), conv=cols("dn_conv_w", l),
            alog=_lane_row(P["dn_a_log"][l], H), dt=_lane_row(P["dn_dt_bias"][l], H), ng=P["dn_norm_g"][l][None],
            sinks=_lane_row(P["sw_sinks"][l], 0), wup_dn=rows("w_up_dn", l), wup_sw=rows("w_up_sw", l), wo=rows("w_o", l),
            g2=P["post_mix_g"][l][None], g3=P["pre_mlp_g"][l][None], wff1=cols("w_ff1", l), wff2=rows("w_ff2", l),
            g4=P["post_mlp_g"][l][None]))

    saved = []
    for l in range(L):
        x, sv = _layer_fwd(x, lws[l], tabs, W, H)
        saved.append(sv)
    loss_row, dx = _loss_head(x, target)
    layer_grads = [None] * L
    F = 4 * P["w_ff1"].shape[2]
    per_layer = dict(w_in=(D, 7 * W + 2 * KV_W + BA_W), w_up_dn=(W, D), w_up_sw=(W, D), w_o=(D, D), w_ff1=(D, F), w_ff2=(F, D))
    grads = {n: lax.empty((L,) + per_layer[n], F32) for n in _BIG}
    for l in reversed(range(L)):
        dx, layer_grads[l], grads = _layer_bwd(dx, lws[l], saved[l], tabs, W, H, l, grads)
    grads.update({n: jnp.stack([layer_grads[l][n] for l in range(L)]) for n in _SMALL})

    got = _pair_swap("grad_pair_swap", [grads[n] for n in _BIG])
    part = {n: _add_half("grad_pair_add_" + n, grads[n], r) for n, r in zip(_BIG, got)}
    items = [(_shard_in(part[n], W, H), None) if n == "w_in" else (part[n], ax) for n, ax in _BIG.items()]
    slots = _scatter_chips("grad_chip_scatter", items)
    halves = []
    for n, s in zip(_BIG, slots):
        flat = s.reshape(4, -1, s.shape[-1])
        halves.append(_sum_slots("grad_chip_sum_" + n, flat).reshape(s.shape[1:]))
    gsum = dict(zip(_BIG, _pair_gather("grad_pair_gather", halves)))
    gsum["w_in"] = gsum["w_in"][:, :, :c4]
    small_shapes = [(1,)] + [grads[n].shape for n in _SMALL]
    tot = _sum_slots("small_sum", _gather_all("small_gather", _pack([loss_row[0, :1]] + [grads[n] for n in _SMALL], LANES)))
    small = _unpack(tot, small_shapes)
    loss = small[0][0]
    gsum.update(zip(_SMALL, small[1:]))
    cw = P["dn_conv_w"].shape[2]
    gsum["dn_conv_w"] = lax.dynamic_slice_in_dim(gsum["dn_conv_w"], me * cw, cw, axis=2)

    delta, new_m, new_v = {}, {}, {}
    for n in _BIG:
        s = P[n].shape
        two_d = lambda a: a.reshape(s[0] * s[1], s[2])
        outs = _adamw("adamw_" + n, two_d(P[n]), two_d(gsum[n]), two_d(P["m_" + n]), two_d(P["v_" + n]))
        delta[n], new_m[n], new_v[n] = (o.reshape(s) for o in outs)
    sm_shapes = [P[n].shape for n in _SMALL]
    outs = _adamw("adamw_small", *(_pack([src[pre + n] for n in _SMALL], LANES)
                                   for src, pre in ((P, ""), (gsum, ""), (P, "m_"), (P, "v_"))))
    for d, o in zip((delta, new_m, new_v), outs):
        d.update(zip(_SMALL, _unpack(o, sm_shapes)))

    return (loss, dx[None], *[gsum[n] for n in _WEIGHTS], *[delta[n] for n in _WEIGHTS],
            *[new_m[n] for n in _WEIGHTS], *[new_v[n] for n in _WEIGHTS])


def kernel(x, positions, pre_mix_g, w_in, dn_conv_w, dn_a_log, dn_dt_bias, dn_norm_g, sw_sinks, w_up_dn, w_up_sw, w_o, post_mix_g, pre_mlp_g, w_ff1, w_ff2, post_mlp_g, loss_target, m_pre_mix_g, m_w_in, m_dn_conv_w, m_dn_a_log, m_dn_dt_bias, m_dn_norm_g, m_sw_sinks, m_w_up_dn, m_w_up_sw, m_w_o, m_post_mix_g, m_pre_mlp_g, m_w_ff1, m_w_ff2, m_post_mlp_g, v_pre_mix_g, v_w_in, v_dn_conv_w, v_dn_a_log, v_dn_dt_bias, v_dn_norm_g, v_sw_sinks, v_w_up_dn, v_w_up_sw, v_w_o, v_post_mix_g, v_pre_mlp_g, v_w_ff1, v_w_ff2, v_post_mlp_g):
    vals = (x, positions, pre_mix_g, w_in, dn_conv_w, dn_a_log, dn_dt_bias, dn_norm_g, sw_sinks, w_up_dn, w_up_sw, w_o, post_mix_g, pre_mlp_g, w_ff1, w_ff2, post_mlp_g, loss_target, m_pre_mix_g, m_w_in, m_dn_conv_w, m_dn_a_log, m_dn_dt_bias, m_dn_norm_g, m_sw_sinks, m_w_up_dn, m_w_up_sw, m_w_o, m_post_mix_g, m_pre_mlp_g, m_w_ff1, m_w_ff2, m_post_mlp_g, v_pre_mix_g, v_w_in, v_dn_conv_w, v_dn_a_log, v_dn_dt_bias, v_dn_norm_g, v_sw_sinks, v_w_up_dn, v_w_up_sw, v_w_o, v_post_mix_g, v_pre_mlp_g, v_w_ff1, v_w_ff2, v_post_mlp_g)
    names = ["x", "positions"] + _WEIGHTS + ["loss_target"] + ["m_" + n for n in _WEIGHTS] + ["v_" + n for n in _WEIGHTS]
    return _step(dict(zip(names, vals)))
```

```python
import functools

import numpy as np
import jax
import jax.numpy as jnp
from jax import lax
from jax.experimental import pallas as pl
from jax.experimental.pallas import tpu as pltpu

F32, BF16 = jnp.float32, jnp.bfloat16
MESH = pl.DeviceIdType.MESH

DN_HEADS = 8
DN_DK = 128
DN_CONV = 4
DN_CHUNK = 64
SW_Q_HEADS = 16
SW_KV_HEADS = 2
SW_HD = 64
SW_BLOCK = 128
ROPE_THETA = 500000.0
ROT_DIM = SW_HD // 4
EPS = 1e-6
ADAM_LR, ADAM_B1, ADAM_B2, ADAM_EPS, ADAM_WD, ADAM_STEP = 0.001, 0.9, 0.999, 1e-08, 0.01, 10

LANES = 128
SUBLANES = 8
VMEM_LIMIT = 48 * 1024 * 1024
KV_W = SW_KV_HEADS * SW_HD
BA_W = 256
PACK_ROWS = 512
DELTA_CB = 4
DELTA_HB = 4


def _pcall(body, **kw):
    return pl.pallas_call(body, **kw)


def _cp(*sem):
    return pltpu.CompilerParams(dimension_semantics=sem, vmem_limit_bytes=VMEM_LIMIT)


def _tile(n, pref, unit=LANES):
    if n <= pref:
        return n
    t = (pref // unit) * unit
    while t > unit and n % t:
        t -= unit
    assert n % t == 0, (n, pref)
    return t


def _sds(shape, dtype):
    return jax.ShapeDtypeStruct(tuple(shape), dtype)


_DIMS = {"nn": ((1,), (0,)), "nt": ((1,), (1,)), "tn": ((0,), (0,))}


def _mm(name, a, b, mode, out_dtype=F32, tm=1024, tn=1024, tk=1024, extras=(), epi=None, out_dtypes=None, slab=None):
    if mode == "nn":
        (M, K), (_, N) = a.shape, b.shape
    elif mode == "nt":
        (M, K), (N, _) = a.shape, b.shape
    else:
        (K, M), (_, N) = a.shape, b.shape
    tm, tn, tk = _tile(M, tm), _tile(N, tn), _tile(K, tk)
    nk = K // tk
    a_spec = {"nn": pl.BlockSpec((tm, tk), lambda i, j, k: (i, k)),
              "nt": pl.BlockSpec((tm, tk), lambda i, j, k: (i, k)),
              "tn": pl.BlockSpec((tk, tm), lambda i, j, k: (k, i))}[mode]
    b_spec = {"nn": pl.BlockSpec((tk, tn), lambda i, j, k: (k, j)),
              "nt": pl.BlockSpec((tn, tk), lambda i, j, k: (j, k)),
              "tn": pl.BlockSpec((tk, tn), lambda i, j, k: (k, j))}[mode]
    dims = (_DIMS[mode], ((), ()))
    out_dtypes = tuple(out_dtypes or (out_dtype,))
    ne, no = len(extras), len(out_dtypes)
    o_spec = pl.BlockSpec((tm, tn), lambda i, j, k: (i, j))

    def body(*refs):
        a_ref, b_ref, ex = refs[0], refs[1], refs[2:2 + ne]
        outs = refs[-no:] if nk == 1 else refs[-1 - no:-1]
        part = lax.dot_general(a_ref[...], b_ref[...], dims, preferred_element_type=F32)

        def finish(acc):
            res = epi(acc, *[e[...] for e in ex]) if epi else (acc,)
            for o, r, dt in zip(outs, res, out_dtypes):
                if slab is None:
                    o[...] = r.astype(dt)
                else:
                    o[0] = r.astype(dt)

        if nk == 1:
            finish(part)
            return
        acc_ref, k = refs[-1], pl.program_id(2)

        @pl.when(k == 0)
        def _():
            acc_ref[...] = part

        @pl.when((k > 0) & (k < nk - 1))
        def _():
            acc_ref[...] += part

        @pl.when(k == nk - 1)
        def _():
            finish(acc_ref[...] + part)

    kw = dict(name=name, grid=(M // tm, N // tn, nk), scratch_shapes=[] if nk == 1 else [pltpu.VMEM((tm, tn), F32)],
              compiler_params=_cp("parallel", "parallel", "arbitrary"))
    if slab is not None:
        buf, l = slab
        return _pcall(body, in_specs=[a_spec, b_spec, ANY], out_specs=pl.BlockSpec((1, tm, tn), lambda i, j, k: (l, i, j)),
                      out_shape=_sds(buf.shape, buf.dtype), input_output_aliases={2: 0}, **kw)(a, b, buf)
    out = _pcall(body, in_specs=[a_spec, b_spec] + [o_spec] * ne, out_specs=tuple(o_spec for _ in out_dtypes),
                 out_shape=tuple(_sds((M, N), dt) for dt in out_dtypes), **kw)(a, b, *extras)
    return out if no > 1 else out[0]


def _rows(name, fn, n_rows, tq, ins, in_specs, out_shapes, out_specs):
    def body(*refs):
        fn(pl.program_id(0), *refs)

    return _pcall(body, name=name, grid=(n_rows // tq,), in_specs=in_specs, out_specs=out_specs,
                  out_shape=out_shapes, compiler_params=_cp("arbitrary"))(*ins)


def _rb(tq, w, cb=0):
    return pl.BlockSpec((tq, w), lambda i: (i, cb))


def _full(shape):
    return pl.BlockSpec(tuple(shape), lambda *_: (0,) * len(shape))


def _rms_fwd(x, g):
    r = lax.rsqrt(jnp.mean(x * x, axis=-1, keepdims=True) + EPS)
    return x * r * g


def _rms_bwd(x, g, dy):
    r = lax.rsqrt(jnp.mean(x * x, axis=-1, keepdims=True) + EPS)
    xh = x * r
    t = dy * g
    dx = r * (t - xh * jnp.mean(t * xh, axis=-1, keepdims=True))
    return dx, jnp.sum(dy * xh, axis=0, keepdims=True)


def _acc(i, ref, val):
    @pl.when(i == 0)
    def _():
        ref[...] = val

    @pl.when(i > 0)
    def _():
        ref[...] += val


def _sigmoid(x):
    return 0.5 * jnp.tanh(0.5 * x) + 0.5


def _pre_norm(x, g):
    S, D = x.shape
    tq = _tile(S, 512, SUBLANES)

    def fn(i, x_ref, g_ref, h_ref):
        h_ref[...] = _rms_fwd(x_ref[...], g_ref[...]).astype(BF16)

    return _rows("pre_norm", fn, S, tq, (x, g), [_rb(tq, D), _full((1, D))], _sds((S, D), BF16), _rb(tq, D))


def _post_mix(x, mix, g2, g3):
    S, D = x.shape
    tq = _tile(S, 512, SUBLANES)

    def fn(i, x_ref, m_ref, g2_ref, g3_ref, x1_ref, h2_ref):
        x1 = x_ref[...] + _rms_fwd(m_ref[...], g2_ref[...])
        x1_ref[...] = x1
        h2_ref[...] = _rms_fwd(x1, g3_ref[...]).astype(BF16)

    return _rows("post_mix", fn, S, tq, (x, mix, g2, g3), [_rb(tq, D), _rb(tq, D), _full((1, D)), _full((1, D))],
                 (_sds((S, D), F32), _sds((S, D), BF16)), (_rb(tq, D), _rb(tq, D)))


def _post_mlp(x1, ff, g4):
    S, D = x1.shape
    tq = _tile(S, 512, SUBLANES)

    def fn(i, x_ref, f_ref, g_ref, o_ref):
        o_ref[...] = x_ref[...] + _rms_fwd(f_ref[...], g_ref[...])

    return _rows("post_mlp", fn, S, tq, (x1, ff, g4), [_rb(tq, D), _rb(tq, D), _full((1, D))], _sds((S, D), F32), _rb(tq, D))


def _loss_head(y, target):
    S, D = y.shape
    tq = _tile(S, 512, SUBLANES)

    def fn(i, y_ref, t_ref, l_ref, d_ref):
        e = y_ref[...] - t_ref[...]
        d_ref[...] = e * (1.0 / D)
        part = jnp.sum(jnp.sum(e * e, axis=1, keepdims=True), axis=0, keepdims=True) * (0.5 / D)
        _acc(i, l_ref, jnp.broadcast_to(part, (1, LANES)))

    return _rows("loss_head", fn, S, tq, (y, target), [_rb(tq, D), _rb(tq, D)],
                 (_sds((1, LANES), F32), _sds((S, D), F32)), (_full((1, LANES)), _rb(tq, D)))


def _post_mlp_bwd(ff, g4, dx2):
    S, D = ff.shape
    tq = _tile(S, 512, SUBLANES)

    def fn(i, f_ref, g_ref, d_ref, o_ref, dg_ref):
        dx, dg = _rms_bwd(f_ref[...], g_ref[...], d_ref[...])
        o_ref[...] = dx.astype(BF16)
        _acc(i, dg_ref, dg)

    return _rows("post_mlp_bwd", fn, S, tq, (ff, g4, dx2), [_rb(tq, D), _full((1, D)), _rb(tq, D)],
                 (_sds((S, D), BF16), _sds((1, D), F32)), (_rb(tq, D), _full((1, D))))


def _mid_bwd(x1, g3, dh2, dx2, mix, g2):
    S, D = x1.shape
    tq = _tile(S, 256, SUBLANES)

    def fn(i, x_ref, g3_ref, dh_ref, dx2_ref, m_ref, g2_ref, dx1_ref, dm_ref, dg3_ref, dg2_ref):
        d, dg3 = _rms_bwd(x_ref[...], g3_ref[...], dh_ref[...])
        dx1 = dx2_ref[...] + d
        dx1_ref[...] = dx1
        dm, dg2 = _rms_bwd(m_ref[...], g2_ref[...], dx1)
        dm_ref[...] = dm.astype(BF16)
        _acc(i, dg3_ref, dg3)
        _acc(i, dg2_ref, dg2)

    r, f = _rb(tq, D), _full((1, D))
    return _rows("mid_bwd", fn, S, tq, (x1, g3, dh2, dx2, mix, g2), [r, f, r, r, r, f],
                 (_sds((S, D), F32), _sds((S, D), BF16), _sds((1, D), F32), _sds((1, D), F32)), (r, r, f, f))


def _pre_norm_bwd(x, g1, dh, dx1):
    S, D = x.shape
    tq = _tile(S, 512, SUBLANES)

    def fn(i, x_ref, g_ref, dh_ref, dx1_ref, dx_ref, dg_ref):
        d, dg = _rms_bwd(x_ref[...], g_ref[...], dh_ref[...])
        dx_ref[...] = dx1_ref[...] + d
        _acc(i, dg_ref, dg)

    r, f = _rb(tq, D), _full((1, D))
    return _rows("pre_norm_bwd", fn, S, tq, (x, g1, dh, dx1), [r, f, r, r], (_sds((S, D), F32), _sds((1, D), F32)), (r, f))


def _mix(proj, ya, yb, D, cb_a):
    S = ya.shape[0]
    tq = _tile(S, 256, SUBLANES)

    def fn(i, ga_ref, gb_ref, ya_ref, yb_ref, o_ref):
        o_ref[...] = (_sigmoid(ga_ref[...]) * ya_ref[...] + _sigmoid(gb_ref[...]) * yb_ref[...]).astype(BF16)

    return _rows("mix", fn, S, tq, (proj, proj, ya, yb), [_rb(tq, D, cb_a), _rb(tq, D, cb_a + 1), _rb(tq, D), _rb(tq, D)],
                 _sds((S, D), BF16), _rb(tq, D))


def _mix_bwd(proj, ya, yb, dmixin, D, cb_a):
    S = ya.shape[0]
    tq = _tile(S, 256, SUBLANES)

    def fn(i, ga_ref, gb_ref, ya_ref, yb_ref, d_ref, dya_ref, dyb_ref, dga_ref, dgb_ref):
        d = d_ref[...]
        sa, sb = _sigmoid(ga_ref[...]), _sigmoid(gb_ref[...])
        dya_ref[...] = (d * sa).astype(BF16)
        dyb_ref[...] = (d * sb).astype(BF16)
        dga_ref[...] = (d * ya_ref[...] * sa * (1.0 - sa)).astype(BF16)
        dgb_ref[...] = (d * yb_ref[...] * sb * (1.0 - sb)).astype(BF16)

    r = _rb(tq, D)
    o = _sds((S, D), BF16)
    return _rows("mix_bwd", fn, S, tq, (proj, proj, ya, yb, dmixin), [_rb(tq, D, cb_a), _rb(tq, D, cb_a + 1), r, r, r],
                 (o, o, o, o), (r, r, r, r))


def _shift_down(xe, k, tq):
    return pltpu.roll(xe, k, 0)[SUBLANES:SUBLANES + tq]


def _conv_pre(cur_ref, halo_ref, w_ref, i, tq):
    x = cur_ref[...]
    halo = jnp.where(i > 0, halo_ref[...], 0.0)
    xe = jnp.concatenate([halo, x], axis=0)
    xs = [x] + [_shift_down(xe, k, tq) for k in range(1, DN_CONV)]
    w = w_ref[...]
    c = sum(w[DN_CONV - 1 - k:DN_CONV - k, :] * xs[k] for k in range(DN_CONV))
    return c, xs


def _dn_prep(proj, conv_w, W):
    S = proj.shape[0]
    tq = _tile(S, 256, SUBLANES)
    hb = tq // SUBLANES

    def body(cur_ref, halo_ref, w_ref, o_ref):
        j, i = pl.program_id(0), pl.program_id(1)
        c, _ = _conv_pre(cur_ref, halo_ref, w_ref, i, tq)
        y = c * _sigmoid(c)
        scale = jnp.where(j == 0, DN_DK ** -0.5, 1.0)
        for h in range(W // DN_DK):
            sl = slice(h * DN_DK, (h + 1) * DN_DK)
            yh = y[:, sl]
            rs = lax.rsqrt(jnp.sum(yh * yh, axis=-1, keepdims=True) + EPS)
            o_ref[:, sl] = jnp.where(j == 2, yh, yh * rs * scale)

    return _pcall(body, name="dn_prep", grid=(3, S // tq),
                  in_specs=[pl.BlockSpec((tq, W), lambda j, i: (i, j)),
                            pl.BlockSpec((SUBLANES, W), lambda j, i: (jnp.maximum(i * hb - 1, 0), j)),
                            pl.BlockSpec((DN_CONV, W), lambda j, i: (0, j))],
                  out_specs=pl.BlockSpec((tq, W), lambda j, i: (i, j)), out_shape=_sds((S, 3 * W), F32),
                  compiler_params=_cp("arbitrary", "arbitrary"))(proj, proj, conv_w)


def _dn_prep_bwd_a(proj, conv_w, dqkv, W):
    S = proj.shape[0]
    tq = _tile(S, 256, SUBLANES)
    hb = tq // SUBLANES

    def body(cur_ref, halo_ref, w_ref, d_ref, dc_ref, dw_ref):
        j, i = pl.program_id(0), pl.program_id(1)
        c, xs = _conv_pre(cur_ref, halo_ref, w_ref, i, tq)
        sg = _sigmoid(c)
        y = c * sg
        scale = jnp.where(j == 0, DN_DK ** -0.5, 1.0)
        dout = d_ref[0]
        dys = []
        for h in range(W // DN_DK):
            sl = slice(h * DN_DK, (h + 1) * DN_DK)
            yh, dh = y[:, sl], dout[:, sl]
            rs = lax.rsqrt(jnp.sum(yh * yh, axis=-1, keepdims=True) + EPS)
            yn = yh * rs
            dn = scale * rs * (dh - yn * jnp.sum(dh * yn, axis=-1, keepdims=True))
            dys.append(jnp.where(j == 2, dh, dn))
        dy = jnp.concatenate(dys, axis=1)
        dc = dy * (sg * (1.0 + c * (1.0 - sg)))
        dc_ref[...] = dc
        dw = jnp.concatenate([jnp.sum(dc * xs[DN_CONV - 1 - r], axis=0, keepdims=True) for r in range(DN_CONV)], axis=0)
        _acc(i, dw_ref, dw)

    return _pcall(body, name="dn_prep_bwd_a", grid=(3, S // tq),
                  in_specs=[pl.BlockSpec((tq, W), lambda j, i: (i, j)),
                            pl.BlockSpec((SUBLANES, W), lambda j, i: (jnp.maximum(i * hb - 1, 0), j)),
                            pl.BlockSpec((DN_CONV, W), lambda j, i: (0, j)),
                            pl.BlockSpec((1, tq, W), lambda j, i: (j, i, 0))],
                  out_specs=(pl.BlockSpec((tq, W), lambda j, i: (i, j)), pl.BlockSpec((DN_CONV, W), lambda j, i: (0, j))),
                  out_shape=(_sds((S, 3 * W), F32), _sds((DN_CONV, 3 * W), F32)),
                  compiler_params=_cp("arbitrary", "arbitrary"))(proj, proj, conv_w, dqkv)


def _dn_prep_bwd_b(dc, conv_w, W):
    S = dc.shape[0]
    tq = _tile(S, 256, SUBLANES)
    hb = tq // SUBLANES
    nblk = S // tq

    def body(cur_ref, nxt_ref, w_ref, o_ref):
        i = pl.program_id(1)
        d = cur_ref[...]
        nxt = jnp.where(i < nblk - 1, nxt_ref[...], 0.0)
        de = jnp.concatenate([d, nxt], axis=0)
        w = w_ref[...]
        out = w[DN_CONV - 1:DN_CONV, :] * d
        for k in range(1, DN_CONV):
            out = out + w[DN_CONV - 1 - k:DN_CONV - k, :] * pltpu.roll(de, tq + SUBLANES - k, 0)[0:tq]
        o_ref[...] = out.astype(BF16)

    return _pcall(body, name="dn_prep_bwd_b", grid=(3, nblk),
                  in_specs=[pl.BlockSpec((tq, W), lambda j, i: (i, j)),
                            pl.BlockSpec((SUBLANES, W), lambda j, i: (jnp.minimum((i + 1) * hb, S // SUBLANES - 1), j)),
                            pl.BlockSpec((DN_CONV, W), lambda j, i: (0, j))],
                  out_specs=pl.BlockSpec((tq, W), lambda j, i: (i, j)), out_shape=_sds((S, 3 * W), BF16),
                  compiler_params=_cp("arbitrary", "arbitrary"))(dc, dc, conv_w)


def _gate_terms(ba, al, dt):
    u = ba + dt
    sp = jnp.maximum(u, 0.0) + jnp.log(1.0 + jnp.exp(-jnp.abs(u)))
    return _sigmoid(ba), -jnp.exp(al) * sp, u


def _dn_gates(proj, alog_row, dt_row, H, cb_ba):
    S = proj.shape[0]
    tq = _tile(S, 512, SUBLANES)
    W = H * DN_DK

    def fn(i, ba_ref, al_ref, dt_ref, be_ref, g_ref):
        bet, gg, _ = _gate_terms(ba_ref[...], al_ref[...], dt_ref[...])
        for h in range(H):
            sl = slice(h * DN_DK, (h + 1) * DN_DK)
            be_ref[:, sl] = jnp.broadcast_to(bet[:, h:h + 1], (tq, DN_DK))
            g_ref[:, sl] = jnp.broadcast_to(gg[:, H + h:H + h + 1], (tq, DN_DK))

    return _rows("dn_gates", fn, S, tq, (proj, alog_row, dt_row), [_rb(tq, LANES, cb_ba), _full((1, LANES)), _full((1, LANES))],
                 (_sds((S, W), F32), _sds((S, W), F32)), (_rb(tq, W), _rb(tq, W)))


def _dn_gates_bwd(proj, alog_row, dt_row, dbeta_b, dg_b, H, cb_ba):
    S = proj.shape[0]
    tq = _tile(S, 512, SUBLANES)
    W = H * DN_DK

    def fn(i, ba_ref, al_ref, dt_ref, db_ref, dg_ref, o_ref, dal_ref, ddt_ref):
        bet, gg, u = _gate_terms(ba_ref[...], al_ref[...], dt_ref[...])
        lane = lax.broadcasted_iota(jnp.int32, (tq, LANES), 1)
        d = jnp.zeros((tq, LANES), F32)
        for h in range(H):
            d = jnp.where(lane == h, db_ref[:, h * DN_DK:h * DN_DK + 1], d)
            d = jnp.where(lane == H + h, dg_ref[:, h * DN_DK:h * DN_DK + 1], d)
        is_a = (lane >= H) & (lane < 2 * H)
        da = jnp.where(is_a, d * (-jnp.exp(al_ref[...]) * _sigmoid(u)), 0.0)
        dlog = jnp.where(lane < H, d * bet * (1.0 - bet), da)
        o_ref[...] = jnp.concatenate([dlog, jnp.zeros((tq, BA_W - LANES), F32)], axis=1).astype(BF16)
        _acc(i, dal_ref, jnp.sum(jnp.where(is_a, d * gg, 0.0), axis=0, keepdims=True))
        _acc(i, ddt_ref, jnp.sum(da, axis=0, keepdims=True))

    f = _full((1, LANES))
    return _rows("dn_gates_bwd", fn, S, tq, (proj, alog_row, dt_row, dbeta_b, dg_b),
                 [_rb(tq, LANES, cb_ba), f, f, _rb(tq, W), _rb(tq, W)],
                 (_sds((S, BA_W), BF16), _sds((1, LANES), F32), _sds((1, LANES), F32)), (_rb(tq, BA_W), f, f))


def _dn_out(o, proj, ng, W, cb_z):
    S = o.shape[0]
    tq = _tile(S, 256, SUBLANES)

    def fn(i, o_ref, z_ref, g_ref, y_ref):
        for h in range(W // DN_DK):
            sl = slice(h * DN_DK, (h + 1) * DN_DK)
            z = z_ref[:, sl]
            y_ref[:, sl] = (_rms_fwd(o_ref[:, sl], g_ref[...]) * (z * _sigmoid(z))).astype(BF16)

    return _rows("dn_out", fn, S, tq, (o, proj, ng), [_rb(tq, W), _rb(tq, W, cb_z), _full((1, DN_DK))], _sds((S, W), BF16), _rb(tq, W))


def _dn_out_bwd(o, proj, ng, dy, W, cb_z):
    S = o.shape[0]
    tq = _tile(S, 256, SUBLANES)

    def fn(i, o_ref, z_ref, g_ref, d_ref, do_ref, dz_ref, dg_ref):
        g = g_ref[...]
        dg = jnp.zeros((1, DN_DK), F32)
        for h in range(W // DN_DK):
            sl = slice(h * DN_DK, (h + 1) * DN_DK)
            oh, z, d = o_ref[:, sl], z_ref[:, sl], d_ref[:, sl]
            sg = _sigmoid(z)
            dn = d * (z * sg)
            dz_ref[:, sl] = (d * _rms_fwd(oh, g) * (sg * (1.0 + z * (1.0 - sg)))).astype(BF16)
            dx, dgh = _rms_bwd(oh, g, dn)
            do_ref[:, sl] = dx
            dg = dg + dgh
        _acc(i, dg_ref, dg)

    r = _rb(tq, W)
    return _rows("dn_out_bwd", fn, S, tq, (o, proj, ng, dy), [r, _rb(tq, W, cb_z), _full((1, DN_DK)), r],
                 (_sds((S, W), F32), _sds((S, W), BF16), _sds((1, DN_DK), F32)), (r, r, _full((1, DN_DK))))


def _bdot(a, b, mode="nn"):
    return lax.dot_general(a.astype(BF16), b.astype(BF16), (_DIMS[mode], ((), ())), preferred_element_type=F32)


def _rsum(x):
    return jnp.broadcast_to(jnp.sum(x, axis=-1, keepdims=True), x.shape)


def _dot3(a, b, mode="nn"):
    ah, bh = a.astype(BF16), b.astype(BF16)
    al, bl = (a - ah.astype(F32)).astype(BF16), (b - bh.astype(F32)).astype(BF16)
    d = lambda x, y: lax.dot_general(x, y, (_DIMS[mode], ((), ())), preferred_element_type=F32)
    return d(ah, bh) + (d(al, bh) + d(ah, bl))


def _cumsum_rows(x, reverse=False):
    n = x.shape[0]
    row = lax.broadcasted_iota(jnp.int32, x.shape, 0)
    s = 1
    while s < n:
        if reverse:
            x = x + jnp.where(row < n - s, pltpu.roll(x, n - s, 0), 0.0)
        else:
            x = x + jnp.where(row >= s, pltpu.roll(x, s, 0), 0.0)
        s *= 2
    return x


def _each(f, *lists):
    return [f(*a) for a in zip(*lists)]


def _delta_local(qs, ks, vs, bes, grs):
    C = DN_CHUNK
    ri = lax.broadcasted_iota(jnp.int32, (C, C), 0)
    ci = lax.broadcasted_iota(jnp.int32, (C, C), 1)
    causal, strict = ri >= ci, ri > ci
    gcs = [_cumsum_rows(g) for g in grs]
    decays = [jnp.where(causal, jnp.exp(jnp.where(causal, gc[:, :C] - gc.T[:C, :], 0.0)), 0.0) for gc in gcs]
    egs = [jnp.exp(gc) for gc in gcs]
    eks = [jnp.exp(gc[C - 1:C, :] - gc) for gc in gcs]
    gams = [jnp.exp(gc[C - 1:C, :]) for gc in gcs]
    kbs = _each(lambda k, be: k * be, ks, bes)
    kks = _each(lambda kb, k: _bdot(kb, k, "nt"), kbs, ks)
    nls = _each(lambda kk, dc: jnp.where(strict, -kk * dc, 0.0), kks, decays)
    eye = (ri == ci).astype(F32)
    ts = [eye + nl for nl in nls]
    pws = [_dot3(nl, nl) for nl in nls]
    for s in range(4):
        both = _each(lambda t, pw: _dot3(jnp.concatenate([t, pw], axis=0), pw), ts, pws)
        ts = _each(lambda t, b: t + b[:C], ts, both)
        pws = [b[C:] for b in both]
    ts = _each(lambda t, pw: t + _dot3(t, pw), ts, pws)
    vbs = _each(lambda v, be: v * be, vs, bes)
    kbes = _each(lambda kb, eg: kb * eg, kbs, egs)
    uws = _each(lambda t, vb, kbe: _dot3(t, jnp.concatenate([vb, kbe], axis=1)), ts, vbs, kbes)
    us, ws = [uw[:, :DN_DK] for uw in uws], [uw[:, DN_DK:] for uw in uws]
    qks = _each(lambda q, k: _bdot(q, k, "nt"), qs, ks)
    return dict(decay=decays, eg=egs, ek=eks, gam=gams, kb=kbs, kk=kks, t=ts, vb=vbs, kbe=kbes, u=us, w=ws, qk=qks,
                a=_each(lambda qk, dc: qk * dc, qks, decays), qd=_each(lambda q, eg: q * eg, qs, egs),
                kd=_each(lambda k, ek: k * ek, ks, eks), strict=strict)


def _delta_items(refs, CB, HB):
    C, dk = DN_CHUNK, DN_DK
    return [[r[c * C:(c + 1) * C, h * dk:(h + 1) * dk] for h in range(HB) for c in range(CB)] for r in refs]


def _delta_fwd(qkv, beta_b, g_b, H, CB, HB):
    S = qkv.shape[0]
    C, dk = DN_CHUNK, DN_DK
    N = S // C
    R = CB * C
    G = H // HB

    def body(q_ref, k_ref, v_ref, b_ref, g_ref, o_ref, st_ref, s_ref):
        @pl.when(pl.program_id(1) == 0)
        def _():
            s_ref[...] = jnp.zeros((HB, dk, dk), F32)

        L = _delta_local(*_delta_items((q_ref, k_ref, v_ref, b_ref, g_ref), CB, HB))
        ss = [s_ref[h] for h in range(HB)]
        for c in range(CB):
            it = [h * CB + c for h in range(HB)]
            for h in range(HB):
                st_ref[h, c] = ss[h]
            wq = [_bdot(jnp.concatenate([L["w"][i], L["qd"][i]], axis=0), s) for i, s in zip(it, ss)]
            vns = [L["u"][i] - x[:C] for i, x in zip(it, wq)]
            outs = [x[C:] + _bdot(L["a"][i], vn) for i, x, vn in zip(it, wq, vns)]
            ss = [s * L["gam"][i] + _bdot(L["kd"][i], vn, "tn") for i, s, vn in zip(it, ss, vns)]
            for h in range(HB):
                o_ref[c * C:(c + 1) * C, h * dk:(h + 1) * dk] = outs[h]
        for h in range(HB):
            s_ref[h] = ss[h]

    blk = lambda off: pl.BlockSpec((R, HB * dk), lambda h, n: (n, off + h))
    return _pcall(body, name="delta_fwd", grid=(G, N // CB),
                  in_specs=[blk(0), blk(G), blk(2 * G), blk(0), blk(0)],
                  out_specs=(blk(0), pl.BlockSpec((HB, CB, dk, dk), lambda h, n: (h, n, 0, 0))),
                  out_shape=(_sds((S, H * dk), F32), _sds((H, N, dk, dk), F32)),
                  scratch_shapes=[pltpu.VMEM((HB, dk, dk), F32)],
                  compiler_params=_cp("arbitrary", "arbitrary"))(qkv, qkv, qkv, beta_b, g_b)


def _delta_bwd(qkv, beta_b, g_b, states, do, H, CB, HB):
    S = qkv.shape[0]
    C, dk = DN_CHUNK, DN_DK
    N = S // C
    R = CB * C
    NB = N // CB
    G = H // HB

    def body(q_ref, k_ref, v_ref, b_ref, g_ref, st_ref, do_ref, dqkv_ref, db_ref, dg_ref, ds_ref):
        @pl.when(pl.program_id(1) == 0)
        def _():
            ds_ref[...] = jnp.zeros((HB, dk, dk), F32)

        qs, ks, vs, bes, grs, dos = _delta_items((q_ref, k_ref, v_ref, b_ref, g_ref, do_ref), CB, HB)
        L = _delta_local(qs, ks, vs, bes, grs)
        ts, decays, kbs, egs, eks, gams, qds, kds = (L[n] for n in ("t", "decay", "kb", "eg", "ek", "gam", "qd", "kd"))
        s0s = [st_ref[h, c] for h in range(HB) for c in range(CB)]
        vns = _each(lambda u, w, s0: u - _bdot(w, s0), L["u"], L["w"], s0s)
        pre_dvn = _each(lambda a, d: _bdot(a, d, "tn"), L["a"], dos)
        pre_ds = _each(lambda qd, d: _bdot(qd, d, "tn"), qds, dos)
        das = _each(lambda d, vn: _bdot(d, vn, "nt"), dos, vns)
        ds = [ds_ref[h] for h in range(HB)]
        ds1s, dvns = [None] * (HB * CB), [None] * (HB * CB)
        for c in reversed(range(CB)):
            it = [h * CB + c for h in range(HB)]
            new = [pre_dvn[i] + _bdot(kds[i], d) for i, d in zip(it, ds)]
            for i, d, dv in zip(it, ds, new):
                ds1s[i], dvns[i] = d, dv
            ds = [pre_ds[i] + d * gams[i] - _bdot(L["w"][i], dv, "tn") for i, d, dv in zip(it, ds, new)]
        for h in range(HB):
            ds_ref[h] = ds[h]
        dkds = _each(lambda vn, d1: _bdot(vn, d1, "nt"), vns, ds1s)
        dgams = _each(lambda s0, d1: jnp.sum(jnp.sum(s0 * d1, axis=1, keepdims=True), axis=0, keepdims=True), s0s, ds1s)
        ost = _each(lambda d, dv, s0: _bdot(jnp.concatenate([d, dv], axis=0), s0, "nt"), dos, dvns, s0s)
        dqds, dws = [x[:C] for x in ost], [-x[C:] for x in ost]
        dvw = _each(lambda dv, dw: jnp.concatenate([dv, dw], axis=1), dvns, dws)
        tdvw = _each(lambda t, x: _dot3(t, x, "tn"), ts, dvw)
        dvbs, dkbes = [x[:, :dk] for x in tdvw], [x[:, dk:] for x in tdvw]
        dts = _each(lambda x, vb, kbe: _dot3(x, jnp.concatenate([vb, kbe], axis=1), "nt"), dvw, L["vb"], L["kbe"])
        tmp = _each(lambda dt, t: _dot3(dt, t, "nt"), dts, ts)
        dls = _each(lambda t, x: -_dot3(t, x, "tn"), ts, tmp)
        ms = _each(lambda dl, dc: jnp.where(L["strict"], dl * dc, 0.0), dls, decays)
        mas = _each(lambda da, dc: da * dc, das, decays)
        dkbs = _each(lambda m, k, dkbe, eg: _bdot(m, k) + dkbe * eg, ms, ks, dkbes, egs)
        dks = _each(lambda m, kb, ma, q, dkd, ek, dkb, be: _bdot(m, kb, "tn") + _bdot(ma, q, "tn") + dkd * ek + dkb * be,
                    ms, kbs, mas, qs, dkds, eks, dkbs, bes)
        dqs = _each(lambda ma, k, dqd, eg: _bdot(ma, k) + dqd * eg, mas, ks, dqds, egs)
        es = _each(lambda m, kk, ma, qk: m * kk + ma * qk, ms, L["kk"], mas, L["qk"])
        ones = jnp.ones((C, dk), BF16)
        row = lax.broadcasted_iota(jnp.int32, (C, dk), 0)
        for i in range(HB * CB):
            h, c = divmod(i, CB)
            rs, cs = slice(c * C, (c + 1) * C), slice(h * dk, (h + 1) * dk)
            e = es[i]
            e_hi = e.astype(BF16)
            col = _bdot(e_hi, ones, "tn") + _bdot(e - e_hi.astype(F32), ones, "tn")
            t_kd = _rsum(dkds[i] * kds[i])
            dgc = (jnp.broadcast_to(jnp.sum(e, axis=1, keepdims=True), (C, dk)) - col + _rsum(dqds[i] * qds[i]) - t_kd
                   + _rsum(dkbes[i] * L["kbe"][i]))
            dglast = jnp.sum(t_kd[:, 0:1], axis=0, keepdims=True) + dgams[i] * gams[i][:, 0:1]
            dgc = dgc + jnp.where(row == C - 1, dglast, 0.0)
            dqkv_ref[0, rs, cs] = dqs[i]
            dqkv_ref[1, rs, cs] = dks[i]
            dqkv_ref[2, rs, cs] = dvbs[i] * bes[i]
            db_ref[rs, cs] = _rsum(dkbs[i] * ks[i]) + _rsum(dvbs[i] * vs[i])
            dg_ref[rs, cs] = _cumsum_rows(dgc, reverse=True)

    blk = lambda off: pl.BlockSpec((R, HB * dk), lambda h, n: (NB - 1 - n, off + h))
    W = H * dk
    return _pcall(body, name="delta_bwd", grid=(G, NB),
                  in_specs=[blk(0), blk(G), blk(2 * G), blk(0), blk(0),
                            pl.BlockSpec((HB, CB, dk, dk), lambda h, n: (h, NB - 1 - n, 0, 0)), blk(0)],
                  out_specs=(pl.BlockSpec((3, R, HB * dk), lambda h, n: (0, NB - 1 - n, h)), blk(0), blk(0)),
                  out_shape=(_sds((3, S, W), F32), _sds((S, W), F32), _sds((S, W), F32)),
                  scratch_shapes=[pltpu.VMEM((HB, dk, dk), F32)],
                  compiler_params=_cp("arbitrary", "arbitrary"))(qkv, qkv, qkv, beta_b, g_b, states, do)


def _rope_consts():
    lane = np.arange(LANES) % SW_HD
    half = ROT_DIM // 2
    inv = (ROPE_THETA ** (-np.arange(half, dtype=np.float32) * np.float32(2.0 / ROT_DIM))).astype(np.float32)
    freq = np.where(lane < ROT_DIM, inv[lane % half], 0.0).astype(np.float32)
    lo = (lane < half).astype(np.float32)
    hi = ((lane >= half) & (lane < ROT_DIM)).astype(np.float32)
    return jnp.asarray(np.stack([freq, -lo, hi] + [np.zeros(LANES, np.float32)] * 5))


def _rope_tables(pos_col):
    S = pos_col.shape[0]
    tq = _tile(S, 1024, SUBLANES)

    def fn(i, p_ref, c_ref, cos_ref, s1_ref, s2_ref):
        ang = p_ref[...].astype(F32) * c_ref[0:1, :]
        sn = jnp.sin(ang)
        cos_ref[...] = jnp.cos(ang)
        s1_ref[...] = sn * c_ref[1:2, :]
        s2_ref[...] = sn * c_ref[2:3, :]

    o, r = _sds((S, LANES), F32), _rb(tq, LANES)
    return _rows("rope_tables", fn, S, tq, (pos_col, _rope_consts()), [_rb(tq, 1), _full((SUBLANES, LANES))], (o, o, o), (r, r, r))


def _wide(a, w):
    return a if w == LANES else jnp.tile(a, (1, w // LANES))


def _rope(x, cos, s1, s2):
    w, h = x.shape[1], ROT_DIM // 2
    return x * _wide(cos, w) + pltpu.roll(x, w - h, 1) * _wide(s1, w) + pltpu.roll(x, h, 1) * _wide(s2, w)


def _unrope(d, cos, s1, s2):
    w, h = d.shape[1], ROT_DIM // 2
    return d * _wide(cos, w) + pltpu.roll(d * _wide(s1, w), h, 1) + pltpu.roll(d * _wide(s2, w), w - h, 1)


def _swa_setup(n, q_ref, kc_ref, kp_ref, vc_ref, vp_ref, tc, tp):
    B = SW_BLOCK
    qr = _rope(q_ref[...], tc[0][...], tc[1][...], tc[2][...]) * (SW_HD ** -0.5)
    kw = jnp.concatenate([_rope(kp_ref[...], tp[0][...], tp[1][...], tp[2][...]),
                          _rope(kc_ref[...], tc[0][...], tc[1][...], tc[2][...])], axis=0)
    vw = jnp.concatenate([vp_ref[...], vc_ref[...]], axis=0)
    lane = lax.broadcasted_iota(jnp.int32, (2 * B, LANES), 1)
    heads = []
    for hk in range(SW_KV_HEADS):
        kh, vh = kw[:, hk * SW_HD:(hk + 1) * SW_HD], vw[:, hk * SW_HD:(hk + 1) * SW_HD]
        kk, vv = jnp.concatenate([kh, kh], axis=1), jnp.concatenate([vh, vh], axis=1)
        heads.append(tuple(jnp.where(sel, t, 0.0).astype(BF16) for t in (kk, vv) for sel in (lane < SW_HD, lane >= SW_HD)))
    qi = lax.broadcasted_iota(jnp.int32, (B, 2 * B), 0) + B
    ki = lax.broadcasted_iota(jnp.int32, (B, 2 * B), 1)
    off = qi - ki
    ok = (off >= 0) & (off < SW_BLOCK) & ((ki >= B) | (n > 0))
    return qr, heads, jnp.where(ok, 0.0, -1e30), lane


SWA_GROUPS = 2


def _swa_probs(items, qs, heads, bias, sk_ref, G2):
    ss = [_bdot(qs[j], heads[j // G2][half], "nt") + bias for j, half in items]
    sks = [sk_ref[0:1, 2 * j + half:2 * j + half + 1] for j, half in items]
    ms = [jnp.maximum(jnp.max(s, axis=-1, keepdims=True), sk) for s, sk in zip(ss, sks)]
    ps = [jnp.exp(s - m) for s, m in zip(ss, ms)]
    es = [jnp.exp(sk - m) for sk, m in zip(sks, ms)]
    inv = [1.0 / (jnp.sum(p, axis=-1, keepdims=True) + e) for p, e in zip(ps, es)]
    return [p * i for p, i in zip(ps, inv)], [e * i for e, i in zip(es, inv)]


def _swa_specs(W, cb_q, cb_k):
    B = SW_BLOCK
    cur = lambda w, cb: pl.BlockSpec((B, w), lambda n: (n, cb))
    prv = lambda w, cb: pl.BlockSpec((B, w), lambda n: (jnp.maximum(n - 1, 0), cb))
    specs = [cur(W, cb_q), cur(LANES, cb_k), prv(LANES, cb_k), cur(LANES, cb_k + 1), prv(LANES, cb_k + 1)]
    return specs + [cur(LANES, 0)] * 3 + [prv(LANES, 0)] * 3 + [_full((1, LANES))]


def _swa_fwd(proj, tabs, sinks_row, W, cb_q, cb_k):
    S = proj.shape[0]
    G2 = SW_Q_HEADS // SW_KV_HEADS // 2

    def body(q_ref, kc_ref, kp_ref, vc_ref, vp_ref, c0, c1, c2, p0, p1, p2, sk_ref, o_ref):
        n = pl.program_id(0)
        qr, heads, bias, _ = _swa_setup(n, q_ref, kc_ref, kp_ref, vc_ref, vp_ref, (c0, c1, c2), (p0, p1, p2))
        qs = [qr[:, j * LANES:(j + 1) * LANES].astype(BF16) for j in range(W // LANES)]
        for j0 in range(0, W // LANES, SWA_GROUPS):
            items = [(j, half) for j in range(j0, j0 + SWA_GROUPS) for half in range(2)]
            probs, _ = _swa_probs(items, qs, heads, bias, sk_ref, G2)
            pv = [_bdot(p, heads[j // G2][2 + half]) for p, (j, half) in zip(probs, items)]
            for g in range(SWA_GROUPS):
                o_ref[:, (j0 + g) * LANES:(j0 + g + 1) * LANES] = (pv[2 * g] + pv[2 * g + 1]).astype(BF16)

    t = tuple(tabs)
    return _pcall(body, name="swa_fwd", grid=(S // SW_BLOCK,), in_specs=_swa_specs(W, cb_q, cb_k),
                  out_specs=pl.BlockSpec((SW_BLOCK, W), lambda n: (n, 0)), out_shape=_sds((S, W), BF16),
                  compiler_params=_cp("arbitrary"))(proj, proj, proj, proj, proj, *t, *t, sinks_row)


def _swa_bwd(proj, tabs, sinks_row, do, W, cb_q, cb_k):
    S = proj.shape[0]
    B = SW_BLOCK
    G2 = SW_Q_HEADS // SW_KV_HEADS // 2
    SKR = -(-SW_Q_HEADS // SUBLANES) * SUBLANES

    def body(q_ref, kc_ref, kp_ref, vc_ref, vp_ref, c0, c1, c2, p0, p1, p2, sk_ref, do_ref,
             dq_ref, dkc_ref, dkp_ref, dvc_ref, dvp_ref, dsk_ref):
        n = pl.program_id(0)
        qr, heads, bias, lane = _swa_setup(n, q_ref, kc_ref, kp_ref, vc_ref, vp_ref, (c0, c1, c2), (p0, p1, p2))

        @pl.when(n == 0)
        def _():
            dsk_ref[...] = jnp.zeros((SKR, LANES), F32)

        acc_k = [jnp.zeros((2 * B, LANES), F32) for _ in range(SW_KV_HEADS)]
        acc_v = [jnp.zeros((2 * B, LANES), F32) for _ in range(SW_KV_HEADS)]
        qs = [qr[:, j * LANES:(j + 1) * LANES].astype(BF16) for j in range(W // LANES)]
        dos = [do_ref[:, j * LANES:(j + 1) * LANES].astype(BF16) for j in range(W // LANES)]
        dqs = []
        for j0 in range(0, W // LANES, SWA_GROUPS):
            items = [(j, half) for j in range(j0, j0 + SWA_GROUPS) for half in range(2)]
            probs, psinks = _swa_probs(items, qs, heads, bias, sk_ref, G2)
            dps = [_bdot(dos[j], heads[j // G2][2 + half], "nt") for j, half in items]
            deltas = [jnp.sum(p * dp, axis=-1, keepdims=True) for p, dp in zip(probs, dps)]
            dss = [(p * (dp - dl)).astype(BF16) for p, dp, dl in zip(probs, dps, deltas)]
            pbs = [p.astype(BF16) for p in probs]
            dqp = [_bdot(ds, heads[j // G2][half]) for ds, (j, half) in zip(dss, items)]
            dkk = [_bdot(ds, qs[j], "tn") for ds, (j, half) in zip(dss, items)]
            dvv = [_bdot(p, dos[j], "tn") for p, (j, half) in zip(pbs, items)]
            for i, (j, half) in enumerate(items):
                hk, h = j // G2, 2 * j + half
                sel = (lane < SW_HD) if half == 0 else (lane >= SW_HD)
                acc_k[hk] = acc_k[hk] + jnp.where(sel, dkk[i], 0.0)
                acc_v[hk] = acc_v[hk] + jnp.where(sel, dvv[i], 0.0)
                dsk_ref[h:h + 1, :] += jnp.broadcast_to(-jnp.sum(psinks[i] * deltas[i], axis=0, keepdims=True), (1, LANES))
            dqs += [dqp[2 * g] + dqp[2 * g + 1] for g in range(SWA_GROUPS)]
        dq = jnp.concatenate(dqs, axis=1) * (SW_HD ** -0.5)
        dq_ref[...] = _unrope(dq, c0[...], c1[...], c2[...]).astype(BF16)
        fold = lambda a: a[:, :SW_HD] + a[:, SW_HD:]
        dkw = jnp.concatenate([fold(a) for a in acc_k], axis=1)
        dvw = jnp.concatenate([fold(a) for a in acc_v], axis=1)
        dkp_ref[...], dkc_ref[...] = dkw[:B], dkw[B:]
        dvp_ref[...], dvc_ref[...] = dvw[:B], dvw[B:]

    t = tuple(tabs)
    blk = lambda w: pl.BlockSpec((B, w), lambda n: (n, 0))
    o = _sds((S, LANES), F32)
    return _pcall(body, name="swa_bwd", grid=(S // B,), in_specs=_swa_specs(W, cb_q, cb_k) + [blk(W)],
                  out_specs=(blk(W), blk(LANES), blk(LANES), blk(LANES), blk(LANES), _full((SKR, LANES))),
                  out_shape=(_sds((S, W), BF16), o, o, o, o, _sds((SKR, LANES), F32)),
                  compiler_params=_cp("arbitrary"))(proj, proj, proj, proj, proj, *t, *t, sinks_row, do)


def _swa_kv_combine(dkc, dkp, dvc, dvp, tabs):
    S = dkc.shape[0]
    B = SW_BLOCK
    nb = S // B

    def fn(n, kc_ref, kp_ref, vc_ref, vp_ref, c0, c1, c2, o_ref):
        more = n < nb - 1
        dk = kc_ref[...] + jnp.where(more, kp_ref[...], 0.0)
        dv = vc_ref[...] + jnp.where(more, vp_ref[...], 0.0)
        o_ref[...] = jnp.concatenate([_unrope(dk, c0[...], c1[...], c2[...]), dv], axis=1).astype(BF16)

    cur = _rb(B, LANES)
    nxt = pl.BlockSpec((B, LANES), lambda n: (jnp.minimum(n + 1, nb - 1), 0))
    return _rows("swa_kv_combine", fn, S, B, (dkc, dkp, dvc, dvp, *tabs), [cur, nxt, cur, nxt, cur, cur, cur],
                 _sds((S, 2 * LANES), BF16), _rb(B, 2 * LANES))


ANY = pl.BlockSpec(memory_space=pl.ANY)


def _place():
    x, y, c = lax.axis_index("x"), lax.axis_index("y"), lax.axis_index("c")
    return x, y, c, [(1 - x, y), (x, 1 - y), (1 - x, 1 - y)]


def _comm_call(name, body, out_shapes, n_sems, n_local, *ins):
    return _pcall(body, name=name, out_shape=tuple(out_shapes), in_specs=[ANY] * len(ins), out_specs=tuple(ANY for _ in out_shapes),
                  scratch_shapes=[pltpu.SemaphoreType.DMA((n_sems,)), pltpu.SemaphoreType.DMA((n_sems,)),
                                  pltpu.SemaphoreType.DMA((n_local,))])(*ins)


def _remote(src, dst, send, recv, k, to):
    return pltpu.make_async_remote_copy(src_ref=src, dst_ref=dst, send_sem=send.at[k], recv_sem=recv.at[k], device_id=to,
                                        device_id_type=MESH)


def _gather_chips(name, arrs):
    n = len(arrs)
    Lh = arrs[0].shape[0] // 2

    def body(*refs):
        w, o, (send, recv, _) = refs[:n], refs[n:2 * n], refs[2 * n:]
        x, y, c, chips = _place()
        me, sib = 2 * x + y, (x, y, 1 - c)
        own, other = pl.ds(c * Lh, Lh), pl.ds((1 - c) * Lh, Lh)
        idx = [2 * cx + cy for cx, cy in chips]
        first = [[_remote(w[a].at[own], o[a].at[me, own], send, recv, 6 * a + j, (*chips[j], c)) for j in range(3)] for a in range(n)]
        passed = [[_remote(o[a].at[idx[j], own], o[a].at[idx[j], own], send, recv, 6 * a + 3 + j, sib) for j in range(3)] for a in range(n)]
        for cp in [cp for row in first for cp in row]:
            cp.start()
        for j in range(3):
            for a in range(n):
                _remote(w[a].at[own], o[a].at[idx[j], own], send, recv, 6 * a + j, (*chips[j], c)).wait_recv()
                passed[a][j].start()
        for j in range(3):
            for a in range(n):
                _remote(w[a].at[other], o[a].at[idx[j], other], send, recv, 6 * a + 3 + j, sib).wait_recv()
        for cp in [cp for row in first + passed for cp in row]:
            cp.wait_send()

    return _comm_call(name, body, [_sds((4,) + a.shape, a.dtype) for a in arrs], 6 * n, 1, *arrs)


def _pair_swap(name, arrs, whole=False):
    n = len(arrs)
    Lh = arrs[0].shape[0] if whole else arrs[0].shape[0] // 2

    def body(*refs):
        g, o, (send, recv, _) = refs[:n], refs[n:2 * n], refs[2 * n:]
        x, y, c, _ = _place()
        cps = [_remote(g[a] if whole else g[a].at[pl.ds((1 - c) * Lh, Lh)], o[a], send, recv, a, (x, y, 1 - c)) for a in range(n)]
        for cp in cps:
            cp.start()
        for cp in cps:
            cp.wait()

    return _comm_call(name, body, [_sds((Lh,) + a.shape[1:], a.dtype) for a in arrs], n, 1, *arrs)


def _join_halves(mine, theirs):
    c, Lh = lax.axis_index("c"), mine.shape[0]
    out = lax.dynamic_update_slice_in_dim(lax.empty((2 * Lh,) + mine.shape[1:], mine.dtype), mine, c * Lh, 0)
    return lax.dynamic_update_slice_in_dim(out, theirs, (1 - c) * Lh, 0)


def _chip_slice(ref, axis, s):
    if axis is None:
        return ref.at[s]
    q = ref.shape[axis] // 4
    start = s * q if isinstance(s, int) else pl.multiple_of(s * q, q)
    return ref.at[tuple([slice(None)] * axis + [pl.ds(start, q)])]


def _scatter_chips(name, items):
    n = len(items)
    part = lambda a, ax: a.shape[1:] if ax is None else tuple(d // 4 if i == ax else d for i, d in enumerate(a.shape))

    def body(*refs):
        p, o, (send, recv, _) = refs[:n], refs[n:2 * n], refs[2 * n:]
        x, y, c, chips = _place()
        me = 2 * x + y
        idx = [2 * cx + cy for cx, cy in chips]
        cps = [_remote(_chip_slice(p[a], items[a][1], idx[j]), o[a].at[me], send, recv, 3 * a + j, (*chips[j], c))
               for a in range(n) for j in range(3)]
        for cp in cps:
            cp.start()
        for a in range(n):
            for j in range(3):
                _remote(_chip_slice(p[a], items[a][1], me), o[a].at[idx[j]], send, recv, 3 * a + j, (*chips[j], c)).wait_recv()
        for cp in cps:
            cp.wait_send()

    return _comm_call(name, body, [_sds((4,) + part(a, ax), a.dtype) for a, ax in items], 3 * n, 1, *[a for a, _ in items])


def _own_part(a, axis, me):
    if axis is None:
        return lax.dynamic_index_in_dim(a, me, 0, keepdims=False)
    q = a.shape[axis] // 4
    return lax.dynamic_slice_in_dim(a, me * q, q, axis)


def _gather_all(name, b):
    R, C = b.shape
    flips = [(dx, dy, dc) for dx in (0, 1) for dy in (0, 1) for dc in (0, 1)][1:]

    def body(b_ref, o_ref, send, recv, lsem):
        x, y, c, _ = _place()
        me = 4 * x + 2 * y + c
        peers = [(x ^ dx, y ^ dy, c ^ dc) for dx, dy, dc in flips]
        mine = pltpu.make_async_copy(b_ref, o_ref.at[me], lsem.at[0])
        mine.start()
        cps = [_remote(b_ref, o_ref.at[me], send, recv, k, peer) for k, peer in enumerate(peers)]
        for cp in cps:
            cp.start()
        for k, (px, py, pc) in enumerate(peers):
            _remote(b_ref, o_ref.at[4 * px + 2 * py + pc], send, recv, k, (px, py, pc)).wait_recv()
        for cp in cps:
            cp.wait_send()
        mine.wait()

    return _comm_call(name, body, [_sds((8, R, C), b.dtype)], 7, 1, b)[0]


def _block_rows(rows, width):
    return _tile(rows, max(SUBLANES, (1 << 19) // width), SUBLANES)


def _add_half(name, g, got):
    L, A, B = g.shape
    Lh = L // 2
    tq = _block_rows(A, B)

    def body(c_ref, g_ref, r_ref, o_ref):
        o_ref[...] = (g_ref[...] + r_ref[...]).astype(BF16)

    spec = pltpu.PrefetchScalarGridSpec(
        num_scalar_prefetch=1, grid=(Lh, A // tq),
        in_specs=[pl.BlockSpec((1, tq, B), lambda l, i, c_ref: (c_ref[0] * Lh + l, i, 0)),
                  pl.BlockSpec((1, tq, B), lambda l, i, c_ref: (l, i, 0))],
        out_specs=pl.BlockSpec((1, tq, B), lambda l, i, c_ref: (l, i, 0)))
    return _pcall(body, name=name, grid_spec=spec, out_shape=_sds((Lh, A, B), BF16),
                  compiler_params=_cp("arbitrary", "arbitrary"))(lax.axis_index("c").reshape(1).astype(jnp.int32), g, got)


def _sum_slots(name, a):
    n, R, C = a.shape
    tq = _block_rows(R, n * C)

    def fn(i, a_ref, o_ref):
        t = a_ref[0].astype(F32)
        for s in range(1, n):
            t = t + a_ref[s].astype(F32)
        o_ref[...] = t

    return _rows(name, fn, R, tq, (a,), [pl.BlockSpec((n, tq, C), lambda i: (0, i, 0))], _sds((R, C), F32), _rb(tq, C))


def _adamw(name, w, g, m, v):
    R, C = w.shape
    tq = _tile(R, 256, SUBLANES)

    def fn(i, w_ref, g_ref, m_ref, v_ref, d_ref, mo_ref, vo_ref):
        gg = g_ref[...]
        mn = ADAM_B1 * m_ref[...] + (1.0 - ADAM_B1) * gg
        vn = ADAM_B2 * v_ref[...] + (1.0 - ADAM_B2) * (gg * gg)
        mo_ref[...] = mn
        vo_ref[...] = vn
        m_hat = mn / (1.0 - ADAM_B1 ** ADAM_STEP)
        v_hat = vn / (1.0 - ADAM_B2 ** ADAM_STEP)
        d_ref[...] = -ADAM_LR * (m_hat / (jnp.sqrt(v_hat) + ADAM_EPS) + ADAM_WD * w_ref[...])

    r, o = _rb(tq, C), _sds((R, C), F32)
    return _rows(name, fn, R, tq, (w, g, m, v), [r, r, r, r], (o, o, o), (r, r, r))


def _pack(arrs, width, lead=()):
    nl = len(lead)
    flat = jnp.concatenate([a.reshape(lead + (-1,)) for a in arrs], axis=nl)
    n = flat.shape[-1]
    unit = PACK_ROWS * width
    tot = -(-n // unit) * unit
    flat = jnp.pad(flat, [(0, 0)] * nl + [(0, tot - n)])
    return flat.reshape(lead + (tot // width, width))


def _unpack(buf, shapes, lead=()):
    flat = buf.reshape(lead + (-1,))
    out, off = [], 0
    for s in shapes:
        n = int(np.prod(s))
        out.append(flat[..., off:off + n].reshape(lead + tuple(s)))
        off += n
    return out


def _in_groups(W, H):
    o_sq = 4 * W + 2 * H
    o_k = o_sq + W
    o_g = o_k + 2 * KV_W
    return [(0, 4 * W), (o_sq, o_k), (o_g, o_g + 2 * W), (o_k, o_g), (4 * W, o_sq)]


def _relayout_in(shards, W, H):
    c4 = sum(hi - lo for lo, hi in _in_groups(W, H)) // 4
    parts = []
    for lo, hi in _in_groups(W, H):
        for s in range(4):
            a, b = max(lo, s * c4), min(hi, (s + 1) * c4)
            if a < b:
                parts.append(shards[s][:, a - s * c4:b - s * c4])
    parts.append(jnp.zeros((shards.shape[1], BA_W - 2 * H), shards.dtype))
    return jnp.concatenate(parts, axis=1)


def _lane_pad(n):
    return -(-n // LANES) * LANES


def _shard_in(d, W, H):
    groups = _in_groups(W, H)
    starts = [sum(hi - lo for lo, hi in groups[:i]) for i in range(len(groups))]
    stored = sorted(zip(groups, starts))
    c4 = sum(hi - lo for lo, hi in groups) // 4
    out = []
    for s in range(4):
        parts = []
        for (lo, hi), at in stored:
            a, b = max(lo, s * c4), min(hi, (s + 1) * c4)
            if a < b:
                parts.append(d[:, :, at + a - lo:at + b - lo])
        parts.append(jnp.zeros(d.shape[:2] + (_lane_pad(c4) - c4,), d.dtype))
        out.append(jnp.concatenate(parts, axis=2))
    return jnp.stack(out)


def _lane_row(vals, at):
    return jnp.pad(vals, (at, LANES - at - vals.shape[0]))[None]


def _layer_fwd(x, lw, tabs, W, H):
    D = x.shape[1]
    cbk = 7 * W // LANES
    h = _pre_norm(x, lw["g1"])
    proj = _mm("mm_in", h, lw["win"], "nn", F32, tn=768)
    qkv = _dn_prep(proj, lw["conv"], W)
    beta_b, g_b = _dn_gates(proj, lw["alog"], lw["dt"], H, cbk + 2)
    o, st = _delta_fwd(qkv, beta_b, g_b, H, DELTA_CB, DELTA_HB)
    oa = _dn_out(o, proj, lw["ng"], W, 3)
    ob = _swa_fwd(proj, tabs, lw["sinks"], W, 4, cbk)
    ya = _mm("mm_up_dn", oa, lw["wup_dn"], "nn", F32)
    yb = _mm("mm_up_sw", ob, lw["wup_sw"], "nn", F32)
    mixin = _mix(proj, ya, yb, D, 5)
    mix = _mm("mm_o", mixin, lw["wo"], "nn", F32)
    x1, h2 = _post_mix(x, mix, lw["g2"], lw["g3"])
    f1, act = _mm("mm_ff1", h2, lw["wff1"], "nn", out_dtypes=(F32, BF16), epi=lambda acc: (acc, jnp.square(jnp.maximum(acc, 0.0))))
    ff = _mm("mm_ff2", act, lw["wff2"], "nn", F32)
    x2 = _post_mlp(x1, ff, lw["g4"])
    saved = dict(x=x, h=h, proj=proj, qkv=qkv, beta_b=beta_b, g_b=g_b, o=o, st=st, oa=oa, ob=ob, ya=ya, yb=yb,
                 mixin=mixin, mix=mix, x1=x1, h2=h2, f1=f1, act=act, ff=ff)
    return x2, saved


def _layer_bwd(dx2, lw, sv, tabs, W, H, l, big):
    D = dx2.shape[1]
    cbk = 7 * W // LANES
    big = dict(big)
    dff, dg4 = _post_mlp_bwd(sv["ff"], lw["g4"], dx2)
    df1 = _mm("mm_ff2_dx", dff, lw["wff2"], "nt", BF16, extras=(sv["f1"],), epi=lambda acc, f1: (acc * 2.0 * jnp.maximum(f1, 0.0),))
    big["w_ff2"] = _mm("mm_ff2_dw", sv["act"], dff, "tn", slab=(big["w_ff2"], l))
    dh2 = _mm("mm_ff1_dx", df1, lw["wff1"], "nt", F32)
    big["w_ff1"] = _mm("mm_ff1_dw", sv["h2"], df1, "tn", slab=(big["w_ff1"], l))
    dx1, dmix, dg3, dg2 = _mid_bwd(sv["x1"], lw["g3"], dh2, dx2, sv["mix"], lw["g2"])
    dmixin = _mm("mm_o_dx", dmix, lw["wo"], "nt", F32)
    big["w_o"] = _mm("mm_o_dw", sv["mixin"], dmix, "tn", slab=(big["w_o"], l))
    dya, dyb, dga, dgb = _mix_bwd(sv["proj"], sv["ya"], sv["yb"], dmixin, D, 5)
    doa = _mm("mm_up_dn_dx", dya, lw["wup_dn"], "nt", F32)
    big["w_up_dn"] = _mm("mm_up_dn_dw", sv["oa"], dya, "tn", slab=(big["w_up_dn"], l))
    dob = _mm("mm_up_sw_dx", dyb, lw["wup_sw"], "nt", F32)
    big["w_up_sw"] = _mm("mm_up_sw_dw", sv["ob"], dyb, "tn", slab=(big["w_up_sw"], l))
    do, dz, dng = _dn_out_bwd(sv["o"], sv["proj"], lw["ng"], doa, W, 3)
    dqkvn, dbeta_b, dg_b = _delta_bwd(sv["qkv"], sv["beta_b"], sv["g_b"], sv["st"], do, H, DELTA_CB, DELTA_HB)
    dba, dalog, ddt = _dn_gates_bwd(sv["proj"], lw["alog"], lw["dt"], dbeta_b, dg_b, H, cbk + 2)
    dc, dconv = _dn_prep_bwd_a(sv["proj"], lw["conv"], dqkvn, W)
    dqkv = _dn_prep_bwd_b(dc, lw["conv"], W)
    dq_sw, dkc, dkp, dvc, dvp, dsk = _swa_bwd(sv["proj"], tabs, lw["sinks"], dob, W, 4, cbk)
    dkv = _swa_kv_combine(dkc, dkp, dvc, dvp, tabs)
    dproj = jnp.concatenate([dqkv, dz, dq_sw, dga, dgb, dkv, dba], axis=1)
    dh = _mm("mm_in_dx", dproj, lw["win"], "nt", F32, tk=768)
    big["w_in"] = _mm("mm_in_dw", sv["h"], dproj, "tn", tn=768, slab=(big["w_in"], l))
    dx, dg1 = _pre_norm_bwd(sv["x"], lw["g1"], dh, dx1)
    grads = dict(pre_mix_g=dg1[0], dn_conv_w=dconv, dn_a_log=dalog[0, H:2 * H], dn_dt_bias=ddt[0, H:2 * H], dn_norm_g=dng[0],
                 sw_sinks=dsk[:SW_Q_HEADS, 0], post_mix_g=dg2[0], pre_mlp_g=dg3[0], post_mlp_g=dg4[0])
    return dx, grads, big


_WEIGHTS = ["pre_mix_g", "w_in", "dn_conv_w", "dn_a_log", "dn_dt_bias", "dn_norm_g", "sw_sinks", "w_up_dn", "w_up_sw", "w_o",
            "post_mix_g", "pre_mlp_g", "w_ff1", "w_ff2", "post_mlp_g"]
_BIG = {"w_in": 2, "w_up_dn": 1, "w_up_sw": 1, "w_o": 1, "w_ff1": 2, "w_ff2": 1}
_SMALL = [n for n in _WEIGHTS if n not in _BIG]


def _step(P):
    x, target = P["x"][0], P["loss_target"][0]
    S, D = x.shape
    L = P["pre_mix_g"].shape[0]
    H, W = DN_HEADS, DN_HEADS * DN_DK
    assert W == D == SW_Q_HEADS * SW_HD and KV_W == LANES
    me = 2 * lax.axis_index("x") + lax.axis_index("y")

    assert L % 2 == 0
    c4 = P["w_in"].shape[2]
    lane_padded = lambda a: jnp.pad(a, ((0, 0), (0, 0), (0, _lane_pad(a.shape[2]) - a.shape[2])))
    local = [lane_padded(P[n].astype(BF16)) for n in _BIG] + [P["dn_conv_w"]]
    gathered = _gather_chips("weights_gather", local)
    full = {n: lax.dynamic_update_slice_in_dim(g, w[None], me, 0) for n, g, w in zip(list(_BIG) + ["dn_conv_w"], gathered, local)}
    rows = lambda n, l: full[n][:, l].reshape(-1, full[n].shape[-1])
    cols = lambda n, l: jnp.concatenate([full[n][s, l] for s in range(4)], axis=-1)

    tabs = _rope_tables(P["positions"].reshape(S, 1))
    lws = []
    for l in range(L):
        lws.append(dict(
            g1=P["pre_mix_g"][l][None], win=_relayout_in(full["w_in"][:, l], W, H), conv=cols("dn_conv_w", l),
            alog=_lane_row(P["dn_a_log"][l], H), dt=_lane_row(P["dn_dt_bias"][l], H), ng=P["dn_norm_g"][l][None],
            sinks=_lane_row(P["sw_sinks"][l], 0), wup_dn=rows("w_up_dn", l), wup_sw=rows("w_up_sw", l), wo=rows("w_o", l),
            g2=P["post_mix_g"][l][None], g3=P["pre_mlp_g"][l][None], wff1=cols("w_ff1", l), wff2=rows("w_ff2", l),
            g4=P["post_mlp_g"][l][None]))

    saved = []
    for l in range(L):
        x, sv = _layer_fwd(x, lws[l], tabs, W, H)
        saved.append(sv)
    loss_row, dx = _loss_head(x, target)
    layer_grads = [None] * L
    F = 4 * P["w_ff1"].shape[2]
    per_layer = dict(w_in=(D, 7 * W + 2 * KV_W + BA_W), w_up_dn=(W, D), w_up_sw=(W, D), w_o=(D, D), w_ff1=(D, F), w_ff2=(F, D))
    grads = {n: lax.empty((L,) + per_layer[n], F32) for n in _BIG}
    for l in reversed(range(L)):
        dx, layer_grads[l], grads = _layer_bwd(dx, lws[l], saved[l], tabs, W, H, l, grads)
    grads.update({n: jnp.stack([layer_grads[l][n] for l in range(L)]) for n in _SMALL})

    got = _pair_swap("grad_pair_swap", [grads[n] for n in _BIG])
    part = {n: _add_half("grad_pair_add_" + n, grads[n], r) for n, r in zip(_BIG, got)}
    items = [(_shard_in(part[n], W, H), None) if n == "w_in" else (part[n], ax) for n, ax in _BIG.items()]
    slots = _scatter_chips("grad_chip_scatter", items)
    halves = []
    for n, s, (a, ax) in zip(_BIG, slots, items):
        s = lax.dynamic_update_slice_in_dim(s, _own_part(a, ax, me)[None], me, 0)
        flat = s.reshape(4, -1, s.shape[-1])
        halves.append(_sum_slots("grad_chip_sum_" + n, flat).reshape(s.shape[1:]))
    theirs = _pair_swap("grad_pair_share", halves, whole=True)
    gsum = {n: _join_halves(h, t) for n, h, t in zip(_BIG, halves, theirs)}
    gsum["w_in"] = gsum["w_in"][:, :, :c4]
    small_shapes = [(1,)] + [grads[n].shape for n in _SMALL]
    tot = _sum_slots("small_sum", _gather_all("small_gather", _pack([loss_row[0, :1]] + [grads[n] for n in _SMALL], LANES)))
    small = _unpack(tot, small_shapes)
    loss = small[0][0]
    gsum.update(zip(_SMALL, small[1:]))
    cw = P["dn_conv_w"].shape[2]
    gsum["dn_conv_w"] = lax.dynamic_slice_in_dim(gsum["dn_conv_w"], me * cw, cw, axis=2)

    delta, new_m, new_v = {}, {}, {}
    for n in _BIG:
        s = P[n].shape
        two_d = lambda a: a.reshape(s[0] * s[1], s[2])
        outs = _adamw("adamw_" + n, two_d(P[n]), two_d(gsum[n]), two_d(P["m_" + n]), two_d(P["v_" + n]))
        delta[n], new_m[n], new_v[n] = (o.reshape(s) for o in outs)
    sm_shapes = [P[n].shape for n in _SMALL]
    outs = _adamw("adamw_small", *(_pack([src[pre + n] for n in _SMALL], LANES)
                                   for src, pre in ((P, ""), (gsum, ""), (P, "m_"), (P, "v_"))))
    for d, o in zip((delta, new_m, new_v), outs):
        d.update(zip(_SMALL, _unpack(o, sm_shapes)))

    return (loss, dx[None], *[gsum[n] for n in _WEIGHTS], *[delta[n] for n in _WEIGHTS],
            *[new_m[n] for n in _WEIGHTS], *[new_v[n] for n in _WEIGHTS])


def kernel(x, positions, pre_mix_g, w_in, dn_conv_w, dn_a_log, dn_dt_bias, dn_norm_g, sw_sinks, w_up_dn, w_up_sw, w_o, post_mix_g, pre_mlp_g, w_ff1, w_ff2, post_mlp_g, loss_target, m_pre_mix_g, m_w_in, m_dn_conv_w, m_dn_a_log, m_dn_dt_bias, m_dn_norm_g, m_sw_sinks, m_w_up_dn, m_w_up_sw, m_w_o, m_post_mix_g, m_pre_mlp_g, m_w_ff1, m_w_ff2, m_post_mlp_g, v_pre_mix_g, v_w_in, v_dn_conv_w, v_dn_a_log, v_dn_dt_bias, v_dn_norm_g, v_sw_sinks, v_w_up_dn, v_w_up_sw, v_w_o, v_post_mix_g, v_pre_mlp_g, v_w_ff1, v_w_ff2, v_post_mlp_g):
    vals = (x, positions, pre_mix_g, w_in, dn_conv_w, dn_a_log, dn_dt_bias, dn_norm_g, sw_sinks, w_up_dn, w_up_sw, w_o, post_mix_g, pre_mlp_g, w_ff1, w_ff2, post_mlp_g, loss_target, m_pre_mix_g, m_w_in, m_dn_conv_w, m_dn_a_log, m_dn_dt_bias, m_dn_norm_g, m_sw_sinks, m_w_up_dn, m_w_up_sw, m_w_o, m_post_mix_g, m_pre_mlp_g, m_w_ff1, m_w_ff2, m_post_mlp_g, v_pre_mix_g, v_w_in, v_dn_conv_w, v_dn_a_log, v_dn_dt_bias, v_dn_norm_g, v_sw_sinks, v_w_up_dn, v_w_up_sw, v_w_o, v_post_mix_g, v_pre_mlp_g, v_w_ff1, v_w_ff2, v_post_mlp_g)
    names = ["x", "positions"] + _WEIGHTS + ["loss_target"] + ["m_" + n for n in _WEIGHTS] + ["v_" + n for n in _WEIGHTS]
    return _step(dict(zip(names, vals)))
```

```python
import functools

import numpy as np
import jax
import jax.numpy as jnp
from jax import lax
from jax.experimental import pallas as pl
from jax.experimental.pallas import tpu as pltpu

F32, BF16 = jnp.float32, jnp.bfloat16
MESH = pl.DeviceIdType.MESH

DN_HEADS = 8
DN_DK = 128
DN_CONV = 4
DN_CHUNK = 64
SW_Q_HEADS = 16
SW_KV_HEADS = 2
SW_HD = 64
SW_BLOCK = 128
ROPE_THETA = 500000.0
ROT_DIM = SW_HD // 4
EPS = 1e-6
ADAM_LR, ADAM_B1, ADAM_B2, ADAM_EPS, ADAM_WD, ADAM_STEP = 0.001, 0.9, 0.999, 1e-08, 0.01, 10

LANES = 128
SUBLANES = 8
VMEM_LIMIT = 48 * 1024 * 1024
KV_W = SW_KV_HEADS * SW_HD
BA_W = 256
PACK_ROWS = 512
DELTA_CB = 4
DELTA_HB = 4


def _pcall(body, **kw):
    return pl.pallas_call(body, **kw)


def _cp(*sem):
    return pltpu.CompilerParams(dimension_semantics=sem, vmem_limit_bytes=VMEM_LIMIT)


def _tile(n, pref, unit=LANES):
    if n <= pref:
        return n
    t = (pref // unit) * unit
    while t > unit and n % t:
        t -= unit
    assert n % t == 0, (n, pref)
    return t


def _sds(shape, dtype):
    return jax.ShapeDtypeStruct(tuple(shape), dtype)


_DIMS = {"nn": ((1,), (0,)), "nt": ((1,), (1,)), "tn": ((0,), (0,))}


def _mm(name, a, b, mode, out_dtype=F32, tm=1024, tn=1024, tk=1024, extras=(), epi=None, out_dtypes=None, slab=None):
    if mode == "nn":
        (M, K), (_, N) = a.shape, b.shape
    elif mode == "nt":
        (M, K), (N, _) = a.shape, b.shape
    else:
        (K, M), (_, N) = a.shape, b.shape
    tm, tn, tk = _tile(M, tm), _tile(N, tn), _tile(K, tk)
    nk = K // tk
    a_spec = {"nn": pl.BlockSpec((tm, tk), lambda i, j, k: (i, k)),
              "nt": pl.BlockSpec((tm, tk), lambda i, j, k: (i, k)),
              "tn": pl.BlockSpec((tk, tm), lambda i, j, k: (k, i))}[mode]
    b_spec = {"nn": pl.BlockSpec((tk, tn), lambda i, j, k: (k, j)),
              "nt": pl.BlockSpec((tn, tk), lambda i, j, k: (j, k)),
              "tn": pl.BlockSpec((tk, tn), lambda i, j, k: (k, j))}[mode]
    dims = (_DIMS[mode], ((), ()))
    out_dtypes = tuple(out_dtypes or (out_dtype,))
    ne, no = len(extras), len(out_dtypes)
    o_spec = pl.BlockSpec((tm, tn), lambda i, j, k: (i, j))

    def body(*refs):
        a_ref, b_ref, ex = refs[0], refs[1], refs[2:2 + ne]
        outs = refs[-no:] if nk == 1 else refs[-1 - no:-1]
        part = lax.dot_general(a_ref[...], b_ref[...], dims, preferred_element_type=F32)

        def finish(acc):
            res = epi(acc, *[e[...] for e in ex]) if epi else (acc,)
            for o, r, dt in zip(outs, res, out_dtypes):
                if slab is None:
                    o[...] = r.astype(dt)
                else:
                    o[0] = r.astype(dt)

        if nk == 1:
            finish(part)
            return
        acc_ref, k = refs[-1], pl.program_id(2)

        @pl.when(k == 0)
        def _():
            acc_ref[...] = part

        @pl.when((k > 0) & (k < nk - 1))
        def _():
            acc_ref[...] += part

        @pl.when(k == nk - 1)
        def _():
            finish(acc_ref[...] + part)

    kw = dict(name=name, grid=(M // tm, N // tn, nk), scratch_shapes=[] if nk == 1 else [pltpu.VMEM((tm, tn), F32)],
              compiler_params=_cp("parallel", "parallel", "arbitrary"))
    if slab is not None:
        buf, l = slab
        return _pcall(body, in_specs=[a_spec, b_spec, ANY], out_specs=pl.BlockSpec((1, tm, tn), lambda i, j, k: (l, i, j)),
                      out_shape=_sds(buf.shape, buf.dtype), input_output_aliases={2: 0}, **kw)(a, b, buf)
    out = _pcall(body, in_specs=[a_spec, b_spec] + [o_spec] * ne, out_specs=tuple(o_spec for _ in out_dtypes),
                 out_shape=tuple(_sds((M, N), dt) for dt in out_dtypes), **kw)(a, b, *extras)
    return out if no > 1 else out[0]


def _rows(name, fn, n_rows, tq, ins, in_specs, out_shapes, out_specs):
    def body(*refs):
        fn(pl.program_id(0), *refs)

    return _pcall(body, name=name, grid=(n_rows // tq,), in_specs=in_specs, out_specs=out_specs,
                  out_shape=out_shapes, compiler_params=_cp("arbitrary"))(*ins)


def _rb(tq, w, cb=0):
    return pl.BlockSpec((tq, w), lambda i: (i, cb))


def _full(shape):
    return pl.BlockSpec(tuple(shape), lambda *_: (0,) * len(shape))


def _rms_fwd(x, g):
    r = lax.rsqrt(jnp.mean(x * x, axis=-1, keepdims=True) + EPS)
    return x * r * g


def _rms_bwd(x, g, dy):
    r = lax.rsqrt(jnp.mean(x * x, axis=-1, keepdims=True) + EPS)
    xh = x * r
    t = dy * g
    dx = r * (t - xh * jnp.mean(t * xh, axis=-1, keepdims=True))
    return dx, jnp.sum(dy * xh, axis=0, keepdims=True)


def _acc(i, ref, val):
    @pl.when(i == 0)
    def _():
        ref[...] = val

    @pl.when(i > 0)
    def _():
        ref[...] += val


def _sigmoid(x):
    return 0.5 * jnp.tanh(0.5 * x) + 0.5


def _pre_norm(x, g):
    S, D = x.shape
    tq = _tile(S, 512, SUBLANES)

    def fn(i, x_ref, g_ref, h_ref):
        h_ref[...] = _rms_fwd(x_ref[...], g_ref[...]).astype(BF16)

    return _rows("pre_norm", fn, S, tq, (x, g), [_rb(tq, D), _full((1, D))], _sds((S, D), BF16), _rb(tq, D))


def _post_mix(x, mix, g2, g3):
    S, D = x.shape
    tq = _tile(S, 512, SUBLANES)

    def fn(i, x_ref, m_ref, g2_ref, g3_ref, x1_ref, h2_ref):
        x1 = x_ref[...] + _rms_fwd(m_ref[...], g2_ref[...])
        x1_ref[...] = x1
        h2_ref[...] = _rms_fwd(x1, g3_ref[...]).astype(BF16)

    return _rows("post_mix", fn, S, tq, (x, mix, g2, g3), [_rb(tq, D), _rb(tq, D), _full((1, D)), _full((1, D))],
                 (_sds((S, D), F32), _sds((S, D), BF16)), (_rb(tq, D), _rb(tq, D)))


def _post_mlp(x1, ff, g4):
    S, D = x1.shape
    tq = _tile(S, 512, SUBLANES)

    def fn(i, x_ref, f_ref, g_ref, o_ref):
        o_ref[...] = x_ref[...] + _rms_fwd(f_ref[...], g_ref[...])

    return _rows("post_mlp", fn, S, tq, (x1, ff, g4), [_rb(tq, D), _rb(tq, D), _full((1, D))], _sds((S, D), F32), _rb(tq, D))


def _loss_head(y, target):
    S, D = y.shape
    tq = _tile(S, 512, SUBLANES)

    def fn(i, y_ref, t_ref, l_ref, d_ref):
        e = y_ref[...] - t_ref[...]
        d_ref[...] = e * (1.0 / D)
        part = jnp.sum(jnp.sum(e * e, axis=1, keepdims=True), axis=0, keepdims=True) * (0.5 / D)
        _acc(i, l_ref, jnp.broadcast_to(part, (1, LANES)))

    return _rows("loss_head", fn, S, tq, (y, target), [_rb(tq, D), _rb(tq, D)],
                 (_sds((1, LANES), F32), _sds((S, D), F32)), (_full((1, LANES)), _rb(tq, D)))


def _post_mlp_bwd(ff, g4, dx2):
    S, D = ff.shape
    tq = _tile(S, 512, SUBLANES)

    def fn(i, f_ref, g_ref, d_ref, o_ref, dg_ref):
        dx, dg = _rms_bwd(f_ref[...], g_ref[...], d_ref[...])
        o_ref[...] = dx.astype(BF16)
        _acc(i, dg_ref, dg)

    return _rows("post_mlp_bwd", fn, S, tq, (ff, g4, dx2), [_rb(tq, D), _full((1, D)), _rb(tq, D)],
                 (_sds((S, D), BF16), _sds((1, D), F32)), (_rb(tq, D), _full((1, D))))


def _mid_bwd(x1, g3, dh2, dx2, mix, g2):
    S, D = x1.shape
    tq = _tile(S, 256, SUBLANES)

    def fn(i, x_ref, g3_ref, dh_ref, dx2_ref, m_ref, g2_ref, dx1_ref, dm_ref, dg3_ref, dg2_ref):
        d, dg3 = _rms_bwd(x_ref[...], g3_ref[...], dh_ref[...])
        dx1 = dx2_ref[...] + d
        dx1_ref[...] = dx1
        dm, dg2 = _rms_bwd(m_ref[...], g2_ref[...], dx1)
        dm_ref[...] = dm.astype(BF16)
        _acc(i, dg3_ref, dg3)
        _acc(i, dg2_ref, dg2)

    r, f = _rb(tq, D), _full((1, D))
    return _rows("mid_bwd", fn, S, tq, (x1, g3, dh2, dx2, mix, g2), [r, f, r, r, r, f],
                 (_sds((S, D), F32), _sds((S, D), BF16), _sds((1, D), F32), _sds((1, D), F32)), (r, r, f, f))


def _pre_norm_bwd(x, g1, dh, dx1):
    S, D = x.shape
    tq = _tile(S, 512, SUBLANES)

    def fn(i, x_ref, g_ref, dh_ref, dx1_ref, dx_ref, dg_ref):
        d, dg = _rms_bwd(x_ref[...], g_ref[...], dh_ref[...])
        dx_ref[...] = dx1_ref[...] + d
        _acc(i, dg_ref, dg)

    r, f = _rb(tq, D), _full((1, D))
    return _rows("pre_norm_bwd", fn, S, tq, (x, g1, dh, dx1), [r, f, r, r], (_sds((S, D), F32), _sds((1, D), F32)), (r, f))


def _mix(proj, ya, yb, D, cb_a):
    S = ya.shape[0]
    tq = _tile(S, 256, SUBLANES)

    def fn(i, ga_ref, gb_ref, ya_ref, yb_ref, o_ref):
        ga, gb, ya, yb = (r[...].astype(F32) for r in (ga_ref, gb_ref, ya_ref, yb_ref))
        o_ref[...] = (_sigmoid(ga) * ya + _sigmoid(gb) * yb).astype(BF16)

    return _rows("mix", fn, S, tq, (proj, proj, ya, yb), [_rb(tq, D, cb_a), _rb(tq, D, cb_a + 1), _rb(tq, D), _rb(tq, D)],
                 _sds((S, D), BF16), _rb(tq, D))


def _mix_bwd(proj, ya, yb, dmixin, D, cb_a):
    S = ya.shape[0]
    tq = _tile(S, 256, SUBLANES)

    def fn(i, ga_ref, gb_ref, ya_ref, yb_ref, d_ref, dya_ref, dyb_ref, dga_ref, dgb_ref):
        ga, gb, ya, yb, d = (r[...].astype(F32) for r in (ga_ref, gb_ref, ya_ref, yb_ref, d_ref))
        sa, sb = _sigmoid(ga), _sigmoid(gb)
        dya_ref[...] = (d * sa).astype(BF16)
        dyb_ref[...] = (d * sb).astype(BF16)
        dga_ref[...] = (d * ya * sa * (1.0 - sa)).astype(BF16)
        dgb_ref[...] = (d * yb * sb * (1.0 - sb)).astype(BF16)

    r = _rb(tq, D)
    o = _sds((S, D), BF16)
    return _rows("mix_bwd", fn, S, tq, (proj, proj, ya, yb, dmixin), [_rb(tq, D, cb_a), _rb(tq, D, cb_a + 1), r, r, r],
                 (o, o, o, o), (r, r, r, r))


HALO = 16


def _shift_down(xe, k, tq):
    return pltpu.roll(xe, k, 0)[HALO:HALO + tq]


def _conv_pre(cur_ref, halo_ref, w_ref, i, tq):
    x = cur_ref[...].astype(F32)
    halo = jnp.where(i > 0, halo_ref[...].astype(F32), 0.0)
    xe = jnp.concatenate([halo, x], axis=0)
    xs = [x] + [_shift_down(xe, k, tq) for k in range(1, DN_CONV)]
    w = w_ref[...]
    c = sum(w[DN_CONV - 1 - k:DN_CONV - k, :] * xs[k] for k in range(DN_CONV))
    return c, xs


def _dn_prep(proj, conv_w, W):
    S = proj.shape[0]
    tq = _tile(S, 256, HALO)
    hb = tq // HALO

    def body(cur_ref, halo_ref, w_ref, o_ref):
        j, i = pl.program_id(0), pl.program_id(1)
        c, _ = _conv_pre(cur_ref, halo_ref, w_ref, i, tq)
        y = c * _sigmoid(c)
        scale = jnp.where(j == 0, DN_DK ** -0.5, 1.0)
        for h in range(W // DN_DK):
            sl = slice(h * DN_DK, (h + 1) * DN_DK)
            yh = y[:, sl]
            rs = lax.rsqrt(jnp.sum(yh * yh, axis=-1, keepdims=True) + EPS)
            o_ref[:, sl] = jnp.where(j == 2, yh, yh * rs * scale)

    return _pcall(body, name="dn_prep", grid=(3, S // tq),
                  in_specs=[pl.BlockSpec((tq, W), lambda j, i: (i, j)),
                            pl.BlockSpec((HALO, W), lambda j, i: (jnp.maximum(i * hb - 1, 0), j)),
                            pl.BlockSpec((DN_CONV, W), lambda j, i: (0, j))],
                  out_specs=pl.BlockSpec((tq, W), lambda j, i: (i, j)), out_shape=_sds((S, 3 * W), F32),
                  compiler_params=_cp("arbitrary", "arbitrary"))(proj, proj, conv_w)


def _dn_prep_bwd_a(proj, conv_w, dqkv, W):
    S = proj.shape[0]
    tq = _tile(S, 256, HALO)
    hb = tq // HALO

    def body(cur_ref, halo_ref, w_ref, d_ref, dc_ref, dw_ref):
        j, i = pl.program_id(0), pl.program_id(1)
        c, xs = _conv_pre(cur_ref, halo_ref, w_ref, i, tq)
        sg = _sigmoid(c)
        y = c * sg
        scale = jnp.where(j == 0, DN_DK ** -0.5, 1.0)
        dout = d_ref[0]
        dys = []
        for h in range(W // DN_DK):
            sl = slice(h * DN_DK, (h + 1) * DN_DK)
            yh, dh = y[:, sl], dout[:, sl]
            rs = lax.rsqrt(jnp.sum(yh * yh, axis=-1, keepdims=True) + EPS)
            yn = yh * rs
            dn = scale * rs * (dh - yn * jnp.sum(dh * yn, axis=-1, keepdims=True))
            dys.append(jnp.where(j == 2, dh, dn))
        dy = jnp.concatenate(dys, axis=1)
        dc = dy * (sg * (1.0 + c * (1.0 - sg)))
        dc_ref[...] = dc
        dw = jnp.concatenate([jnp.sum(dc * xs[DN_CONV - 1 - r], axis=0, keepdims=True) for r in range(DN_CONV)], axis=0)
        _acc(i, dw_ref, dw)

    return _pcall(body, name="dn_prep_bwd_a", grid=(3, S // tq),
                  in_specs=[pl.BlockSpec((tq, W), lambda j, i: (i, j)),
                            pl.BlockSpec((HALO, W), lambda j, i: (jnp.maximum(i * hb - 1, 0), j)),
                            pl.BlockSpec((DN_CONV, W), lambda j, i: (0, j)),
                            pl.BlockSpec((1, tq, W), lambda j, i: (j, i, 0))],
                  out_specs=(pl.BlockSpec((tq, W), lambda j, i: (i, j)), pl.BlockSpec((DN_CONV, W), lambda j, i: (0, j))),
                  out_shape=(_sds((S, 3 * W), F32), _sds((DN_CONV, 3 * W), F32)),
                  compiler_params=_cp("arbitrary", "arbitrary"))(proj, proj, conv_w, dqkv)


def _dn_prep_bwd_b(dc, conv_w, W):
    S = dc.shape[0]
    tq = _tile(S, 256, SUBLANES)
    hb = tq // SUBLANES
    nblk = S // tq

    def body(cur_ref, nxt_ref, w_ref, o_ref):
        i = pl.program_id(1)
        d = cur_ref[...]
        nxt = jnp.where(i < nblk - 1, nxt_ref[...], 0.0)
        de = jnp.concatenate([d, nxt], axis=0)
        w = w_ref[...]
        out = w[DN_CONV - 1:DN_CONV, :] * d
        for k in range(1, DN_CONV):
            out = out + w[DN_CONV - 1 - k:DN_CONV - k, :] * pltpu.roll(de, tq + SUBLANES - k, 0)[0:tq]
        o_ref[...] = out.astype(BF16)

    return _pcall(body, name="dn_prep_bwd_b", grid=(3, nblk),
                  in_specs=[pl.BlockSpec((tq, W), lambda j, i: (i, j)),
                            pl.BlockSpec((SUBLANES, W), lambda j, i: (jnp.minimum((i + 1) * hb, S // SUBLANES - 1), j)),
                            pl.BlockSpec((DN_CONV, W), lambda j, i: (0, j))],
                  out_specs=pl.BlockSpec((tq, W), lambda j, i: (i, j)), out_shape=_sds((S, 3 * W), BF16),
                  compiler_params=_cp("arbitrary", "arbitrary"))(dc, dc, conv_w)


def _gate_terms(ba, al, dt):
    u = ba + dt
    sp = jnp.maximum(u, 0.0) + jnp.log(1.0 + jnp.exp(-jnp.abs(u)))
    return _sigmoid(ba), -jnp.exp(al) * sp, u


def _dn_gates(ba, alog_row, dt_row, H):
    S = ba.shape[0]
    tq = _tile(S, 512, SUBLANES)
    W = H * DN_DK

    def fn(i, ba_ref, al_ref, dt_ref, be_ref, g_ref):
        bet, gg, _ = _gate_terms(ba_ref[...], al_ref[...], dt_ref[...])
        for h in range(H):
            sl = slice(h * DN_DK, (h + 1) * DN_DK)
            be_ref[:, sl] = jnp.broadcast_to(bet[:, h:h + 1], (tq, DN_DK))
            g_ref[:, sl] = jnp.broadcast_to(gg[:, H + h:H + h + 1], (tq, DN_DK))

    return _rows("dn_gates", fn, S, tq, (ba, alog_row, dt_row), [_rb(tq, LANES), _full((1, LANES)), _full((1, LANES))],
                 (_sds((S, W), F32), _sds((S, W), F32)), (_rb(tq, W), _rb(tq, W)))


def _dn_gates_bwd(ba, alog_row, dt_row, dbeta_b, dg_b, H):
    S = ba.shape[0]
    tq = _tile(S, 512, SUBLANES)
    W = H * DN_DK

    def fn(i, ba_ref, al_ref, dt_ref, db_ref, dg_ref, o_ref, dal_ref, ddt_ref):
        bet, gg, u = _gate_terms(ba_ref[...], al_ref[...], dt_ref[...])
        lane = lax.broadcasted_iota(jnp.int32, (tq, LANES), 1)
        d = jnp.zeros((tq, LANES), F32)
        for h in range(H):
            d = jnp.where(lane == h, db_ref[:, h * DN_DK:h * DN_DK + 1], d)
            d = jnp.where(lane == H + h, dg_ref[:, h * DN_DK:h * DN_DK + 1], d)
        is_a = (lane >= H) & (lane < 2 * H)
        da = jnp.where(is_a, d * (-jnp.exp(al_ref[...]) * _sigmoid(u)), 0.0)
        dlog = jnp.where(lane < H, d * bet * (1.0 - bet), da)
        o_ref[...] = jnp.concatenate([dlog, jnp.zeros((tq, BA_W - LANES), F32)], axis=1).astype(BF16)
        _acc(i, dal_ref, jnp.sum(jnp.where(is_a, d * gg, 0.0), axis=0, keepdims=True))
        _acc(i, ddt_ref, jnp.sum(da, axis=0, keepdims=True))

    f = _full((1, LANES))
    return _rows("dn_gates_bwd", fn, S, tq, (ba, alog_row, dt_row, dbeta_b, dg_b),
                 [_rb(tq, LANES), f, f, _rb(tq, W), _rb(tq, W)],
                 (_sds((S, BA_W), BF16), _sds((1, LANES), F32), _sds((1, LANES), F32)), (_rb(tq, BA_W), f, f))


def _dn_out(o, proj, ng, W, cb_z):
    S = o.shape[0]
    tq = _tile(S, 256, SUBLANES)

    def fn(i, o_ref, z_ref, g_ref, y_ref):
        for h in range(W // DN_DK):
            sl = slice(h * DN_DK, (h + 1) * DN_DK)
            z = z_ref[:, sl].astype(F32)
            y_ref[:, sl] = (_rms_fwd(o_ref[:, sl], g_ref[...]) * (z * _sigmoid(z))).astype(BF16)

    return _rows("dn_out", fn, S, tq, (o, proj, ng), [_rb(tq, W), _rb(tq, W, cb_z), _full((1, DN_DK))], _sds((S, W), BF16), _rb(tq, W))


def _dn_out_bwd(o, proj, ng, dy, W, cb_z):
    S = o.shape[0]
    tq = _tile(S, 256, SUBLANES)

    def fn(i, o_ref, z_ref, g_ref, d_ref, do_ref, dz_ref, dg_ref):
        g = g_ref[...]
        dg = jnp.zeros((1, DN_DK), F32)
        for h in range(W // DN_DK):
            sl = slice(h * DN_DK, (h + 1) * DN_DK)
            oh, z, d = o_ref[:, sl], z_ref[:, sl].astype(F32), d_ref[:, sl].astype(F32)
            sg = _sigmoid(z)
            dn = d * (z * sg)
            dz_ref[:, sl] = (d * _rms_fwd(oh, g) * (sg * (1.0 + z * (1.0 - sg)))).astype(BF16)
            dx, dgh = _rms_bwd(oh, g, dn)
            do_ref[:, sl] = dx
            dg = dg + dgh
        _acc(i, dg_ref, dg)

    r = _rb(tq, W)
    return _rows("dn_out_bwd", fn, S, tq, (o, proj, ng, dy), [r, _rb(tq, W, cb_z), _full((1, DN_DK)), r],
                 (_sds((S, W), F32), _sds((S, W), BF16), _sds((1, DN_DK), F32)), (r, r, _full((1, DN_DK))))


def _bdot(a, b, mode="nn"):
    return lax.dot_general(a.astype(BF16), b.astype(BF16), (_DIMS[mode], ((), ())), preferred_element_type=F32)


def _rsum(x):
    return jnp.broadcast_to(jnp.sum(x, axis=-1, keepdims=True), x.shape)


def _dot3(a, b, mode="nn"):
    ah, bh = a.astype(BF16), b.astype(BF16)
    al, bl = (a - ah.astype(F32)).astype(BF16), (b - bh.astype(F32)).astype(BF16)
    d = lambda x, y: lax.dot_general(x, y, (_DIMS[mode], ((), ())), preferred_element_type=F32)
    return d(ah, bh) + (d(al, bh) + d(ah, bl))


def _cumsum_rows(x, reverse=False):
    n = x.shape[0]
    row = lax.broadcasted_iota(jnp.int32, x.shape, 0)
    s = 1
    while s < n:
        if reverse:
            x = x + jnp.where(row < n - s, pltpu.roll(x, n - s, 0), 0.0)
        else:
            x = x + jnp.where(row >= s, pltpu.roll(x, s, 0), 0.0)
        s *= 2
    return x


def _each(f, *lists):
    return [f(*a) for a in zip(*lists)]


def _delta_local(qs, ks, vs, bes, grs):
    C = DN_CHUNK
    ri = lax.broadcasted_iota(jnp.int32, (C, C), 0)
    ci = lax.broadcasted_iota(jnp.int32, (C, C), 1)
    causal, strict = ri >= ci, ri > ci
    gcs = [_cumsum_rows(g) for g in grs]
    decays = [jnp.where(causal, jnp.exp(jnp.where(causal, gc[:, :C] - gc.T[:C, :], 0.0)), 0.0) for gc in gcs]
    egs = [jnp.exp(gc) for gc in gcs]
    eks = [jnp.exp(gc[C - 1:C, :] - gc) for gc in gcs]
    gams = [jnp.exp(gc[C - 1:C, :]) for gc in gcs]
    kbs = _each(lambda k, be: k * be, ks, bes)
    kks = _each(lambda kb, k: _bdot(kb, k, "nt"), kbs, ks)
    nls = _each(lambda kk, dc: jnp.where(strict, -kk * dc, 0.0), kks, decays)
    eye = (ri == ci).astype(F32)
    ts = [eye + nl for nl in nls]
    pws = [_dot3(nl, nl) for nl in nls]
    for s in range(4):
        both = _each(lambda t, pw: _dot3(jnp.concatenate([t, pw], axis=0), pw), ts, pws)
        ts = _each(lambda t, b: t + b[:C], ts, both)
        pws = [b[C:] for b in both]
    ts = _each(lambda t, pw: t + _dot3(t, pw), ts, pws)
    vbs = _each(lambda v, be: v * be, vs, bes)
    kbes = _each(lambda kb, eg: kb * eg, kbs, egs)
    uws = _each(lambda t, vb, kbe: _dot3(t, jnp.concatenate([vb, kbe], axis=1)), ts, vbs, kbes)
    us, ws = [uw[:, :DN_DK] for uw in uws], [uw[:, DN_DK:] for uw in uws]
    qks = _each(lambda q, k: _bdot(q, k, "nt"), qs, ks)
    return dict(decay=decays, eg=egs, ek=eks, gam=gams, kb=kbs, kk=kks, t=ts, vb=vbs, kbe=kbes, u=us, w=ws, qk=qks,
                a=_each(lambda qk, dc: qk * dc, qks, decays), qd=_each(lambda q, eg: q * eg, qs, egs),
                kd=_each(lambda k, ek: k * ek, ks, eks), strict=strict)


def _delta_items(refs, CB, HB):
    C, dk = DN_CHUNK, DN_DK
    return [[r[c * C:(c + 1) * C, h * dk:(h + 1) * dk] for h in range(HB) for c in range(CB)] for r in refs]


def _delta_fwd(qkv, beta_b, g_b, H, CB, HB):
    S = qkv.shape[0]
    C, dk = DN_CHUNK, DN_DK
    N = S // C
    R = CB * C
    G = H // HB

    def body(q_ref, k_ref, v_ref, b_ref, g_ref, o_ref, st_ref, s_ref):
        @pl.when(pl.program_id(1) == 0)
        def _():
            s_ref[...] = jnp.zeros((HB, dk, dk), F32)

        L = _delta_local(*_delta_items((q_ref, k_ref, v_ref, b_ref, g_ref), CB, HB))
        ss = [s_ref[h] for h in range(HB)]
        for c in range(CB):
            it = [h * CB + c for h in range(HB)]
            for h in range(HB):
                st_ref[h, c] = ss[h]
            wq = [_bdot(jnp.concatenate([L["w"][i], L["qd"][i]], axis=0), s) for i, s in zip(it, ss)]
            vns = [L["u"][i] - x[:C] for i, x in zip(it, wq)]
            outs = [x[C:] + _bdot(L["a"][i], vn) for i, x, vn in zip(it, wq, vns)]
            ss = [s * L["gam"][i] + _bdot(L["kd"][i], vn, "tn") for i, s, vn in zip(it, ss, vns)]
            for h in range(HB):
                o_ref[c * C:(c + 1) * C, h * dk:(h + 1) * dk] = outs[h]
        for h in range(HB):
            s_ref[h] = ss[h]

    blk = lambda off: pl.BlockSpec((R, HB * dk), lambda h, n: (n, off + h))
    return _pcall(body, name="delta_fwd", grid=(G, N // CB),
                  in_specs=[blk(0), blk(G), blk(2 * G), blk(0), blk(0)],
                  out_specs=(blk(0), pl.BlockSpec((HB, CB, dk, dk), lambda h, n: (h, n, 0, 0))),
                  out_shape=(_sds((S, H * dk), F32), _sds((H, N, dk, dk), F32)),
                  scratch_shapes=[pltpu.VMEM((HB, dk, dk), F32)],
                  compiler_params=_cp("arbitrary", "arbitrary"))(qkv, qkv, qkv, beta_b, g_b)


def _delta_bwd(qkv, beta_b, g_b, states, do, H, CB, HB):
    S = qkv.shape[0]
    C, dk = DN_CHUNK, DN_DK
    N = S // C
    R = CB * C
    NB = N // CB
    G = H // HB

    def body(q_ref, k_ref, v_ref, b_ref, g_ref, st_ref, do_ref, dqkv_ref, db_ref, dg_ref, ds_ref):
        @pl.when(pl.program_id(1) == 0)
        def _():
            ds_ref[...] = jnp.zeros((HB, dk, dk), F32)

        qs, ks, vs, bes, grs, dos = _delta_items((q_ref, k_ref, v_ref, b_ref, g_ref, do_ref), CB, HB)
        L = _delta_local(qs, ks, vs, bes, grs)
        ts, decays, kbs, egs, eks, gams, qds, kds = (L[n] for n in ("t", "decay", "kb", "eg", "ek", "gam", "qd", "kd"))
        s0s = [st_ref[h, c] for h in range(HB) for c in range(CB)]
        vns = _each(lambda u, w, s0: u - _bdot(w, s0), L["u"], L["w"], s0s)
        pre_dvn = _each(lambda a, d: _bdot(a, d, "tn"), L["a"], dos)
        pre_ds = _each(lambda qd, d: _bdot(qd, d, "tn"), qds, dos)
        das = _each(lambda d, vn: _bdot(d, vn, "nt"), dos, vns)
        ds = [ds_ref[h] for h in range(HB)]
        ds1s, dvns = [None] * (HB * CB), [None] * (HB * CB)
        for c in reversed(range(CB)):
            it = [h * CB + c for h in range(HB)]
            new = [pre_dvn[i] + _bdot(kds[i], d) for i, d in zip(it, ds)]
            for i, d, dv in zip(it, ds, new):
                ds1s[i], dvns[i] = d, dv
            ds = [pre_ds[i] + d * gams[i] - _bdot(L["w"][i], dv, "tn") for i, d, dv in zip(it, ds, new)]
        for h in range(HB):
            ds_ref[h] = ds[h]
        dkds = _each(lambda vn, d1: _bdot(vn, d1, "nt"), vns, ds1s)
        dgams = _each(lambda s0, d1: jnp.sum(jnp.sum(s0 * d1, axis=1, keepdims=True), axis=0, keepdims=True), s0s, ds1s)
        ost = _each(lambda d, dv, s0: _bdot(jnp.concatenate([d, dv], axis=0), s0, "nt"), dos, dvns, s0s)
        dqds, dws = [x[:C] for x in ost], [-x[C:] for x in ost]
        dvw = _each(lambda dv, dw: jnp.concatenate([dv, dw], axis=1), dvns, dws)
        tdvw = _each(lambda t, x: _dot3(t, x, "tn"), ts, dvw)
        dvbs, dkbes = [x[:, :dk] for x in tdvw], [x[:, dk:] for x in tdvw]
        dts = _each(lambda x, vb, kbe: _dot3(x, jnp.concatenate([vb, kbe], axis=1), "nt"), dvw, L["vb"], L["kbe"])
        tmp = _each(lambda dt, t: _dot3(dt, t, "nt"), dts, ts)
        dls = _each(lambda t, x: -_dot3(t, x, "tn"), ts, tmp)
        ms = _each(lambda dl, dc: jnp.where(L["strict"], dl * dc, 0.0), dls, decays)
        mas = _each(lambda da, dc: da * dc, das, decays)
        dkbs = _each(lambda m, k, dkbe, eg: _bdot(m, k) + dkbe * eg, ms, ks, dkbes, egs)
        dks = _each(lambda m, kb, ma, q, dkd, ek, dkb, be: _bdot(m, kb, "tn") + _bdot(ma, q, "tn") + dkd * ek + dkb * be,
                    ms, kbs, mas, qs, dkds, eks, dkbs, bes)
        dqs = _each(lambda ma, k, dqd, eg: _bdot(ma, k) + dqd * eg, mas, ks, dqds, egs)
        es = _each(lambda m, kk, ma, qk: m * kk + ma * qk, ms, L["kk"], mas, L["qk"])
        ones = jnp.ones((C, dk), BF16)
        row = lax.broadcasted_iota(jnp.int32, (C, dk), 0)
        for i in range(HB * CB):
            h, c = divmod(i, CB)
            rs, cs = slice(c * C, (c + 1) * C), slice(h * dk, (h + 1) * dk)
            e = es[i]
            e_hi = e.astype(BF16)
            col = _bdot(e_hi, ones, "tn") + _bdot(e - e_hi.astype(F32), ones, "tn")
            t_kd = _rsum(dkds[i] * kds[i])
            dgc = (jnp.broadcast_to(jnp.sum(e, axis=1, keepdims=True), (C, dk)) - col + _rsum(dqds[i] * qds[i]) - t_kd
                   + _rsum(dkbes[i] * L["kbe"][i]))
            dglast = jnp.sum(t_kd[:, 0:1], axis=0, keepdims=True) + dgams[i] * gams[i][:, 0:1]
            dgc = dgc + jnp.where(row == C - 1, dglast, 0.0)
            dqkv_ref[0, rs, cs] = dqs[i]
            dqkv_ref[1, rs, cs] = dks[i]
            dqkv_ref[2, rs, cs] = dvbs[i] * bes[i]
            db_ref[rs, cs] = _rsum(dkbs[i] * ks[i]) + _rsum(dvbs[i] * vs[i])
            dg_ref[rs, cs] = _cumsum_rows(dgc, reverse=True)

    blk = lambda off: pl.BlockSpec((R, HB * dk), lambda h, n: (NB - 1 - n, off + h))
    W = H * dk
    return _pcall(body, name="delta_bwd", grid=(G, NB),
                  in_specs=[blk(0), blk(G), blk(2 * G), blk(0), blk(0),
                            pl.BlockSpec((HB, CB, dk, dk), lambda h, n: (h, NB - 1 - n, 0, 0)), blk(0)],
                  out_specs=(pl.BlockSpec((3, R, HB * dk), lambda h, n: (0, NB - 1 - n, h)), blk(0), blk(0)),
                  out_shape=(_sds((3, S, W), F32), _sds((S, W), F32), _sds((S, W), F32)),
                  scratch_shapes=[pltpu.VMEM((HB, dk, dk), F32)],
                  compiler_params=_cp("arbitrary", "arbitrary"))(qkv, qkv, qkv, beta_b, g_b, states, do)


def _rope_consts():
    lane = np.arange(LANES) % SW_HD
    half = ROT_DIM // 2
    inv = (ROPE_THETA ** (-np.arange(half, dtype=np.float32) * np.float32(2.0 / ROT_DIM))).astype(np.float32)
    freq = np.where(lane < ROT_DIM, inv[lane % half], 0.0).astype(np.float32)
    lo = (lane < half).astype(np.float32)
    hi = ((lane >= half) & (lane < ROT_DIM)).astype(np.float32)
    return jnp.asarray(np.stack([freq, -lo, hi] + [np.zeros(LANES, np.float32)] * 5))


def _rope_tables(pos_col):
    S = pos_col.shape[0]
    tq = _tile(S, 1024, SUBLANES)

    def fn(i, p_ref, c_ref, cos_ref, s1_ref, s2_ref):
        ang = p_ref[...].astype(F32) * c_ref[0:1, :]
        sn = jnp.sin(ang)
        cos_ref[...] = jnp.cos(ang)
        s1_ref[...] = sn * c_ref[1:2, :]
        s2_ref[...] = sn * c_ref[2:3, :]

    o, r = _sds((S, LANES), F32), _rb(tq, LANES)
    return _rows("rope_tables", fn, S, tq, (pos_col, _rope_consts()), [_rb(tq, 1), _full((SUBLANES, LANES))], (o, o, o), (r, r, r))


def _wide(a, w):
    return a if w == LANES else jnp.tile(a, (1, w // LANES))


def _rope(x, cos, s1, s2):
    w, h = x.shape[1], ROT_DIM // 2
    return x * _wide(cos, w) + pltpu.roll(x, w - h, 1) * _wide(s1, w) + pltpu.roll(x, h, 1) * _wide(s2, w)


def _unrope(d, cos, s1, s2):
    w, h = d.shape[1], ROT_DIM // 2
    return d * _wide(cos, w) + pltpu.roll(d * _wide(s1, w), h, 1) + pltpu.roll(d * _wide(s2, w), w - h, 1)


def _swa_setup(n, q_ref, kc_ref, kp_ref, vc_ref, vp_ref, tc, tp):
    B = SW_BLOCK
    qr = _rope(q_ref[...].astype(F32), tc[0][...], tc[1][...], tc[2][...]) * (SW_HD ** -0.5)
    kw = jnp.concatenate([_rope(kp_ref[...].astype(F32), tp[0][...], tp[1][...], tp[2][...]),
                          _rope(kc_ref[...].astype(F32), tc[0][...], tc[1][...], tc[2][...])], axis=0)
    vw = jnp.concatenate([vp_ref[...], vc_ref[...]], axis=0).astype(F32)
    lane = lax.broadcasted_iota(jnp.int32, (2 * B, LANES), 1)
    heads = []
    for hk in range(SW_KV_HEADS):
        kh, vh = kw[:, hk * SW_HD:(hk + 1) * SW_HD], vw[:, hk * SW_HD:(hk + 1) * SW_HD]
        kk, vv = jnp.concatenate([kh, kh], axis=1), jnp.concatenate([vh, vh], axis=1)
        heads.append(tuple(jnp.where(sel, t, 0.0).astype(BF16) for t in (kk, vv) for sel in (lane < SW_HD, lane >= SW_HD)))
    qi = lax.broadcasted_iota(jnp.int32, (B, 2 * B), 0) + B
    ki = lax.broadcasted_iota(jnp.int32, (B, 2 * B), 1)
    off = qi - ki
    ok = (off >= 0) & (off < SW_BLOCK) & ((ki >= B) | (n > 0))
    return qr, heads, jnp.where(ok, 0.0, -1e30), lane


SWA_GROUPS = 2


def _swa_probs(items, qs, heads, bias, sk_ref, G2):
    ss = [_bdot(qs[j], heads[j // G2][half], "nt") + bias for j, half in items]
    sks = [sk_ref[0:1, 2 * j + half:2 * j + half + 1] for j, half in items]
    ms = [jnp.maximum(jnp.max(s, axis=-1, keepdims=True), sk) for s, sk in zip(ss, sks)]
    ps = [jnp.exp(s - m) for s, m in zip(ss, ms)]
    es = [jnp.exp(sk - m) for sk, m in zip(sks, ms)]
    inv = [1.0 / (jnp.sum(p, axis=-1, keepdims=True) + e) for p, e in zip(ps, es)]
    return [p * i for p, i in zip(ps, inv)], [e * i for e, i in zip(es, inv)]


def _swa_specs(W, cb_q, cb_k):
    B = SW_BLOCK
    cur = lambda w, cb: pl.BlockSpec((B, w), lambda n: (n, cb))
    prv = lambda w, cb: pl.BlockSpec((B, w), lambda n: (jnp.maximum(n - 1, 0), cb))
    specs = [cur(W, cb_q), cur(LANES, cb_k), prv(LANES, cb_k), cur(LANES, cb_k + 1), prv(LANES, cb_k + 1)]
    return specs + [cur(LANES, 0)] * 3 + [prv(LANES, 0)] * 3 + [_full((1, LANES))]


def _swa_fwd(proj, tabs, sinks_row, W, cb_q, cb_k):
    S = proj.shape[0]
    G2 = SW_Q_HEADS // SW_KV_HEADS // 2

    def body(q_ref, kc_ref, kp_ref, vc_ref, vp_ref, c0, c1, c2, p0, p1, p2, sk_ref, o_ref):
        n = pl.program_id(0)
        qr, heads, bias, _ = _swa_setup(n, q_ref, kc_ref, kp_ref, vc_ref, vp_ref, (c0, c1, c2), (p0, p1, p2))
        qs = [qr[:, j * LANES:(j + 1) * LANES].astype(BF16) for j in range(W // LANES)]
        for j0 in range(0, W // LANES, SWA_GROUPS):
            items = [(j, half) for j in range(j0, j0 + SWA_GROUPS) for half in range(2)]
            probs, _ = _swa_probs(items, qs, heads, bias, sk_ref, G2)
            pv = [_bdot(p, heads[j // G2][2 + half]) for p, (j, half) in zip(probs, items)]
            for g in range(SWA_GROUPS):
                o_ref[:, (j0 + g) * LANES:(j0 + g + 1) * LANES] = (pv[2 * g] + pv[2 * g + 1]).astype(BF16)

    t = tuple(tabs)
    return _pcall(body, name="swa_fwd", grid=(S // SW_BLOCK,), in_specs=_swa_specs(W, cb_q, cb_k),
                  out_specs=pl.BlockSpec((SW_BLOCK, W), lambda n: (n, 0)), out_shape=_sds((S, W), BF16),
                  compiler_params=_cp("arbitrary"))(proj, proj, proj, proj, proj, *t, *t, sinks_row)


def _swa_bwd(proj, tabs, sinks_row, do, W, cb_q, cb_k):
    S = proj.shape[0]
    B = SW_BLOCK
    G2 = SW_Q_HEADS // SW_KV_HEADS // 2
    SKR = -(-SW_Q_HEADS // SUBLANES) * SUBLANES

    def body(q_ref, kc_ref, kp_ref, vc_ref, vp_ref, c0, c1, c2, p0, p1, p2, sk_ref, do_ref,
             dq_ref, dkc_ref, dkp_ref, dvc_ref, dvp_ref, dsk_ref):
        n = pl.program_id(0)
        qr, heads, bias, lane = _swa_setup(n, q_ref, kc_ref, kp_ref, vc_ref, vp_ref, (c0, c1, c2), (p0, p1, p2))

        @pl.when(n == 0)
        def _():
            dsk_ref[...] = jnp.zeros((SKR, LANES), F32)

        acc_k = [jnp.zeros((2 * B, LANES), F32) for _ in range(SW_KV_HEADS)]
        acc_v = [jnp.zeros((2 * B, LANES), F32) for _ in range(SW_KV_HEADS)]
        qs = [qr[:, j * LANES:(j + 1) * LANES].astype(BF16) for j in range(W // LANES)]
        dos = [do_ref[:, j * LANES:(j + 1) * LANES].astype(BF16) for j in range(W // LANES)]
        dqs = []
        for j0 in range(0, W // LANES, SWA_GROUPS):
            items = [(j, half) for j in range(j0, j0 + SWA_GROUPS) for half in range(2)]
            probs, psinks = _swa_probs(items, qs, heads, bias, sk_ref, G2)
            dps = [_bdot(dos[j], heads[j // G2][2 + half], "nt") for j, half in items]
            deltas = [jnp.sum(p * dp, axis=-1, keepdims=True) for p, dp in zip(probs, dps)]
            dss = [(p * (dp - dl)).astype(BF16) for p, dp, dl in zip(probs, dps, deltas)]
            pbs = [p.astype(BF16) for p in probs]
            dqp = [_bdot(ds, heads[j // G2][half]) for ds, (j, half) in zip(dss, items)]
            dkk = [_bdot(ds, qs[j], "tn") for ds, (j, half) in zip(dss, items)]
            dvv = [_bdot(p, dos[j], "tn") for p, (j, half) in zip(pbs, items)]
            for i, (j, half) in enumerate(items):
                hk, h = j // G2, 2 * j + half
                sel = (lane < SW_HD) if half == 0 else (lane >= SW_HD)
                acc_k[hk] = acc_k[hk] + jnp.where(sel, dkk[i], 0.0)
                acc_v[hk] = acc_v[hk] + jnp.where(sel, dvv[i], 0.0)
                dsk_ref[h:h + 1, :] += jnp.broadcast_to(-jnp.sum(psinks[i] * deltas[i], axis=0, keepdims=True), (1, LANES))
            dqs += [dqp[2 * g] + dqp[2 * g + 1] for g in range(SWA_GROUPS)]
        dq = jnp.concatenate(dqs, axis=1) * (SW_HD ** -0.5)
        dq_ref[...] = _unrope(dq, c0[...], c1[...], c2[...]).astype(BF16)
        fold = lambda a: a[:, :SW_HD] + a[:, SW_HD:]
        dkw = jnp.concatenate([fold(a) for a in acc_k], axis=1)
        dvw = jnp.concatenate([fold(a) for a in acc_v], axis=1)
        dkp_ref[...], dkc_ref[...] = dkw[:B], dkw[B:]
        dvp_ref[...], dvc_ref[...] = dvw[:B], dvw[B:]

    t = tuple(tabs)
    blk = lambda w: pl.BlockSpec((B, w), lambda n: (n, 0))
    o = _sds((S, LANES), F32)
    return _pcall(body, name="swa_bwd", grid=(S // B,), in_specs=_swa_specs(W, cb_q, cb_k) + [blk(W)],
                  out_specs=(blk(W), blk(LANES), blk(LANES), blk(LANES), blk(LANES), _full((SKR, LANES))),
                  out_shape=(_sds((S, W), BF16), o, o, o, o, _sds((SKR, LANES), F32)),
                  compiler_params=_cp("arbitrary"))(proj, proj, proj, proj, proj, *t, *t, sinks_row, do)


def _swa_kv_combine(dkc, dkp, dvc, dvp, tabs):
    S = dkc.shape[0]
    B = SW_BLOCK
    nb = S // B

    def fn(n, kc_ref, kp_ref, vc_ref, vp_ref, c0, c1, c2, o_ref):
        more = n < nb - 1
        dk = kc_ref[...] + jnp.where(more, kp_ref[...], 0.0)
        dv = vc_ref[...] + jnp.where(more, vp_ref[...], 0.0)
        o_ref[...] = jnp.concatenate([_unrope(dk, c0[...], c1[...], c2[...]), dv], axis=1).astype(BF16)

    cur = _rb(B, LANES)
    nxt = pl.BlockSpec((B, LANES), lambda n: (jnp.minimum(n + 1, nb - 1), 0))
    return _rows("swa_kv_combine", fn, S, B, (dkc, dkp, dvc, dvp, *tabs), [cur, nxt, cur, nxt, cur, cur, cur],
                 _sds((S, 2 * LANES), BF16), _rb(B, 2 * LANES))


ANY = pl.BlockSpec(memory_space=pl.ANY)


def _place():
    x, y, c = lax.axis_index("x"), lax.axis_index("y"), lax.axis_index("c")
    return x, y, c, [(1 - x, y), (x, 1 - y), (1 - x, 1 - y)]


def _comm_call(name, body, out_shapes, n_sems, n_local, *ins):
    return _pcall(body, name=name, out_shape=tuple(out_shapes), in_specs=[ANY] * len(ins), out_specs=tuple(ANY for _ in out_shapes),
                  scratch_shapes=[pltpu.SemaphoreType.DMA((n_sems,)), pltpu.SemaphoreType.DMA((n_sems,)),
                                  pltpu.SemaphoreType.DMA((n_local,))])(*ins)


def _remote(src, dst, send, recv, k, to):
    return pltpu.make_async_remote_copy(src_ref=src, dst_ref=dst, send_sem=send.at[k], recv_sem=recv.at[k], device_id=to,
                                        device_id_type=MESH)


def _gather_chips(name, arrs):
    n = len(arrs)
    Lh = arrs[0].shape[0] // 2

    def body(*refs):
        w, o, (send, recv, _) = refs[:n], refs[n:2 * n], refs[2 * n:]
        x, y, c, chips = _place()
        me, sib = 2 * x + y, (x, y, 1 - c)
        own, other = pl.ds(c * Lh, Lh), pl.ds((1 - c) * Lh, Lh)
        idx = [2 * cx + cy for cx, cy in chips]
        first = [[_remote(w[a].at[own], o[a].at[me, own], send, recv, 6 * a + j, (*chips[j], c)) for j in range(3)] for a in range(n)]
        passed = [[_remote(o[a].at[idx[j], own], o[a].at[idx[j], own], send, recv, 6 * a + 3 + j, sib) for j in range(3)] for a in range(n)]
        for cp in [cp for row in first for cp in row]:
            cp.start()
        for j in range(3):
            for a in range(n):
                _remote(w[a].at[own], o[a].at[idx[j], own], send, recv, 6 * a + j, (*chips[j], c)).wait_recv()
                passed[a][j].start()
        for j in range(3):
            for a in range(n):
                _remote(w[a].at[other], o[a].at[idx[j], other], send, recv, 6 * a + 3 + j, sib).wait_recv()
        for cp in [cp for row in first + passed for cp in row]:
            cp.wait_send()

    return _comm_call(name, body, [_sds((4,) + a.shape, a.dtype) for a in arrs], 6 * n, 1, *arrs)


def _pair_swap(name, arrs, whole=False):
    n = len(arrs)
    Lh = arrs[0].shape[0] if whole else arrs[0].shape[0] // 2

    def body(*refs):
        g, o, (send, recv, _) = refs[:n], refs[n:2 * n], refs[2 * n:]
        x, y, c, _ = _place()
        cps = [_remote(g[a] if whole else g[a].at[pl.ds((1 - c) * Lh, Lh)], o[a], send, recv, a, (x, y, 1 - c)) for a in range(n)]
        for cp in cps:
            cp.start()
        for cp in cps:
            cp.wait()

    return _comm_call(name, body, [_sds((Lh,) + a.shape[1:], a.dtype) for a in arrs], n, 1, *arrs)


def _join_halves(mine, theirs):
    c, Lh = lax.axis_index("c"), mine.shape[0]
    out = lax.dynamic_update_slice_in_dim(lax.empty((2 * Lh,) + mine.shape[1:], mine.dtype), mine, c * Lh, 0)
    return lax.dynamic_update_slice_in_dim(out, theirs, (1 - c) * Lh, 0)


def _chip_slice(ref, axis, s):
    if axis is None:
        return ref.at[s]
    q = ref.shape[axis] // 4
    start = s * q if isinstance(s, int) else pl.multiple_of(s * q, q)
    return ref.at[tuple([slice(None)] * axis + [pl.ds(start, q)])]


def _scatter_chips(name, items):
    n = len(items)
    part = lambda a, ax: a.shape[1:] if ax is None else tuple(d // 4 if i == ax else d for i, d in enumerate(a.shape))

    def body(*refs):
        p, o, (send, recv, _) = refs[:n], refs[n:2 * n], refs[2 * n:]
        x, y, c, chips = _place()
        me = 2 * x + y
        idx = [2 * cx + cy for cx, cy in chips]
        cps = [_remote(_chip_slice(p[a], items[a][1], idx[j]), o[a].at[me], send, recv, 3 * a + j, (*chips[j], c))
               for a in range(n) for j in range(3)]
        for cp in cps:
            cp.start()
        for a in range(n):
            for j in range(3):
                _remote(_chip_slice(p[a], items[a][1], me), o[a].at[idx[j]], send, recv, 3 * a + j, (*chips[j], c)).wait_recv()
        for cp in cps:
            cp.wait_send()

    return _comm_call(name, body, [_sds((4,) + part(a, ax), a.dtype) for a, ax in items], 3 * n, 1, *[a for a, _ in items])


def _own_part(a, axis, me):
    if axis is None:
        return lax.dynamic_index_in_dim(a, me, 0, keepdims=False)
    q = a.shape[axis] // 4
    return lax.dynamic_slice_in_dim(a, me * q, q, axis)


def _gather_all(name, b):
    R, C = b.shape
    flips = [(dx, dy, dc) for dx in (0, 1) for dy in (0, 1) for dc in (0, 1)][1:]

    def body(b_ref, o_ref, send, recv, lsem):
        x, y, c, _ = _place()
        me = 4 * x + 2 * y + c
        peers = [(x ^ dx, y ^ dy, c ^ dc) for dx, dy, dc in flips]
        mine = pltpu.make_async_copy(b_ref, o_ref.at[me], lsem.at[0])
        mine.start()
        cps = [_remote(b_ref, o_ref.at[me], send, recv, k, peer) for k, peer in enumerate(peers)]
        for cp in cps:
            cp.start()
        for k, (px, py, pc) in enumerate(peers):
            _remote(b_ref, o_ref.at[4 * px + 2 * py + pc], send, recv, k, (px, py, pc)).wait_recv()
        for cp in cps:
            cp.wait_send()
        mine.wait()

    return _comm_call(name, body, [_sds((8, R, C), b.dtype)], 7, 1, b)[0]


def _block_rows(rows, width):
    return _tile(rows, max(SUBLANES, (1 << 19) // width), SUBLANES)


def _add_half(name, g, got):
    L, A, B = g.shape
    Lh = L // 2
    tq = _block_rows(A, B)

    def body(c_ref, g_ref, r_ref, o_ref):
        o_ref[...] = (g_ref[...] + r_ref[...]).astype(BF16)

    spec = pltpu.PrefetchScalarGridSpec(
        num_scalar_prefetch=1, grid=(Lh, A // tq),
        in_specs=[pl.BlockSpec((1, tq, B), lambda l, i, c_ref: (c_ref[0] * Lh + l, i, 0)),
                  pl.BlockSpec((1, tq, B), lambda l, i, c_ref: (l, i, 0))],
        out_specs=pl.BlockSpec((1, tq, B), lambda l, i, c_ref: (l, i, 0)))
    return _pcall(body, name=name, grid_spec=spec, out_shape=_sds((Lh, A, B), BF16),
                  compiler_params=_cp("arbitrary", "arbitrary"))(lax.axis_index("c").reshape(1).astype(jnp.int32), g, got)


def _sum_slots(name, a):
    n, R, C = a.shape
    tq = _block_rows(R, n * C)

    def fn(i, a_ref, o_ref):
        t = a_ref[0].astype(F32)
        for s in range(1, n):
            t = t + a_ref[s].astype(F32)
        o_ref[...] = t

    return _rows(name, fn, R, tq, (a,), [pl.BlockSpec((n, tq, C), lambda i: (0, i, 0))], _sds((R, C), F32), _rb(tq, C))


def _adamw(name, w, g, m, v):
    R, C = w.shape
    tq = _tile(R, 256, SUBLANES)

    def fn(i, w_ref, g_ref, m_ref, v_ref, d_ref, mo_ref, vo_ref):
        gg = g_ref[...]
        mn = ADAM_B1 * m_ref[...] + (1.0 - ADAM_B1) * gg
        vn = ADAM_B2 * v_ref[...] + (1.0 - ADAM_B2) * (gg * gg)
        mo_ref[...] = mn
        vo_ref[...] = vn
        m_hat = mn / (1.0 - ADAM_B1 ** ADAM_STEP)
        v_hat = vn / (1.0 - ADAM_B2 ** ADAM_STEP)
        d_ref[...] = -ADAM_LR * (m_hat / (jnp.sqrt(v_hat) + ADAM_EPS) + ADAM_WD * w_ref[...])

    r, o = _rb(tq, C), _sds((R, C), F32)
    return _rows(name, fn, R, tq, (w, g, m, v), [r, r, r, r], (o, o, o), (r, r, r))


def _pack(arrs, width, lead=()):
    nl = len(lead)
    flat = jnp.concatenate([a.reshape(lead + (-1,)) for a in arrs], axis=nl)
    n = flat.shape[-1]
    unit = PACK_ROWS * width
    tot = -(-n // unit) * unit
    flat = jnp.pad(flat, [(0, 0)] * nl + [(0, tot - n)])
    return flat.reshape(lead + (tot // width, width))


def _unpack(buf, shapes, lead=()):
    flat = buf.reshape(lead + (-1,))
    out, off = [], 0
    for s in shapes:
        n = int(np.prod(s))
        out.append(flat[..., off:off + n].reshape(lead + tuple(s)))
        off += n
    return out


def _in_groups(W, H):
    o_sq = 4 * W + 2 * H
    o_k = o_sq + W
    o_g = o_k + 2 * KV_W
    return [(0, 4 * W), (o_sq, o_k), (o_g, o_g + 2 * W), (o_k, o_g), (4 * W, o_sq)]


def _relayout_in(shards, W, H):
    c4 = sum(hi - lo for lo, hi in _in_groups(W, H)) // 4
    parts = []
    for lo, hi in _in_groups(W, H):
        for s in range(4):
            a, b = max(lo, s * c4), min(hi, (s + 1) * c4)
            if a < b:
                parts.append(shards[s][:, a - s * c4:b - s * c4])
    parts.append(jnp.zeros((shards.shape[1], BA_W - 2 * H), shards.dtype))
    return jnp.concatenate(parts, axis=1)


def _shard_in(d, W, H):
    groups = _in_groups(W, H)
    starts = [sum(hi - lo for lo, hi in groups[:i]) for i in range(len(groups))]
    stored = sorted(zip(groups, starts))
    c4 = sum(hi - lo for lo, hi in groups) // 4
    out = []
    for s in range(4):
        parts = []
        for (lo, hi), at in stored:
            a, b = max(lo, s * c4), min(hi, (s + 1) * c4)
            if a < b:
                parts.append(d[:, :, at + a - lo:at + b - lo])
        out.append(jnp.concatenate(parts, axis=2))
    return jnp.stack(out)


def _lane_row(vals, at):
    return jnp.pad(vals, (at, LANES - at - vals.shape[0]))[None]


def _layer_fwd(x, lw, tabs, W, H):
    D = x.shape[1]
    cbk = 7 * W // LANES
    h = _pre_norm(x, lw["g1"])
    proj = _mm("mm_in", h, lw["win"], "nn", BF16, tn=768)
    ba = _mm("mm_ba", h, lw["win"][:, 7 * W + 2 * KV_W:], "nn", F32)
    qkv = _dn_prep(proj, lw["conv"], W)
    beta_b, g_b = _dn_gates(ba, lw["alog"], lw["dt"], H)
    o, st = _delta_fwd(qkv, beta_b, g_b, H, DELTA_CB, DELTA_HB)
    oa = _dn_out(o, proj, lw["ng"], W, 3)
    ob = _swa_fwd(proj, tabs, lw["sinks"], W, 4, cbk)
    ya = _mm("mm_up_dn", oa, lw["wup_dn"], "nn", BF16)
    yb = _mm("mm_up_sw", ob, lw["wup_sw"], "nn", BF16)
    mixin = _mix(proj, ya, yb, D, 5)
    mix = _mm("mm_o", mixin, lw["wo"], "nn", F32)
    x1, h2 = _post_mix(x, mix, lw["g2"], lw["g3"])
    f1, act = _mm("mm_ff1", h2, lw["wff1"], "nn", out_dtypes=(BF16, BF16), epi=lambda acc: (acc, jnp.square(jnp.maximum(acc, 0.0))))
    ff = _mm("mm_ff2", act, lw["wff2"], "nn", F32)
    x2 = _post_mlp(x1, ff, lw["g4"])
    saved = dict(x=x, h=h, proj=proj, ba=ba, qkv=qkv, beta_b=beta_b, g_b=g_b, o=o, st=st, oa=oa, ob=ob, ya=ya, yb=yb,
                 mixin=mixin, mix=mix, x1=x1, h2=h2, f1=f1, act=act, ff=ff)
    return x2, saved


def _layer_bwd(dx2, lw, sv, tabs, W, H, l, big):
    D = dx2.shape[1]
    cbk = 7 * W // LANES
    big = dict(big)
    dff, dg4 = _post_mlp_bwd(sv["ff"], lw["g4"], dx2)
    df1 = _mm("mm_ff2_dx", dff, lw["wff2"], "nt", BF16, extras=(sv["f1"],),
              epi=lambda acc, f1: (acc * 2.0 * jnp.maximum(f1.astype(F32), 0.0),))
    big["w_ff2"] = _mm("mm_ff2_dw", sv["act"], dff, "tn", slab=(big["w_ff2"], l))
    dh2 = _mm("mm_ff1_dx", df1, lw["wff1"], "nt", F32)
    big["w_ff1"] = _mm("mm_ff1_dw", sv["h2"], df1, "tn", slab=(big["w_ff1"], l))
    dx1, dmix, dg3, dg2 = _mid_bwd(sv["x1"], lw["g3"], dh2, dx2, sv["mix"], lw["g2"])
    dmixin = _mm("mm_o_dx", dmix, lw["wo"], "nt", BF16)
    big["w_o"] = _mm("mm_o_dw", sv["mixin"], dmix, "tn", slab=(big["w_o"], l))
    dya, dyb, dga, dgb = _mix_bwd(sv["proj"], sv["ya"], sv["yb"], dmixin, D, 5)
    doa = _mm("mm_up_dn_dx", dya, lw["wup_dn"], "nt", BF16)
    big["w_up_dn"] = _mm("mm_up_dn_dw", sv["oa"], dya, "tn", slab=(big["w_up_dn"], l))
    dob = _mm("mm_up_sw_dx", dyb, lw["wup_sw"], "nt", BF16)
    big["w_up_sw"] = _mm("mm_up_sw_dw", sv["ob"], dyb, "tn", slab=(big["w_up_sw"], l))
    do, dz, dng = _dn_out_bwd(sv["o"], sv["proj"], lw["ng"], doa, W, 3)
    dqkvn, dbeta_b, dg_b = _delta_bwd(sv["qkv"], sv["beta_b"], sv["g_b"], sv["st"], do, H, DELTA_CB, DELTA_HB)
    dba, dalog, ddt = _dn_gates_bwd(sv["ba"], lw["alog"], lw["dt"], dbeta_b, dg_b, H)
    dc, dconv = _dn_prep_bwd_a(sv["proj"], lw["conv"], dqkvn, W)
    dqkv = _dn_prep_bwd_b(dc, lw["conv"], W)
    dq_sw, dkc, dkp, dvc, dvp, dsk = _swa_bwd(sv["proj"], tabs, lw["sinks"], dob, W, 4, cbk)
    dkv = _swa_kv_combine(dkc, dkp, dvc, dvp, tabs)
    dproj = jnp.concatenate([dqkv, dz, dq_sw, dga, dgb, dkv, dba], axis=1)
    dh = _mm("mm_in_dx", dproj, lw["win"], "nt", F32, tk=768)
    big["w_in"] = _mm("mm_in_dw", sv["h"], dproj, "tn", tn=768, slab=(big["w_in"], l))
    dx, dg1 = _pre_norm_bwd(sv["x"], lw["g1"], dh, dx1)
    grads = dict(pre_mix_g=dg1[0], dn_conv_w=dconv, dn_a_log=dalog[0, H:2 * H], dn_dt_bias=ddt[0, H:2 * H], dn_norm_g=dng[0],
                 sw_sinks=dsk[:SW_Q_HEADS, 0], post_mix_g=dg2[0], pre_mlp_g=dg3[0], post_mlp_g=dg4[0])
    return dx, grads, big


_WEIGHTS = ["pre_mix_g", "w_in", "dn_conv_w", "dn_a_log", "dn_dt_bias", "dn_norm_g", "sw_sinks", "w_up_dn", "w_up_sw", "w_o",
            "post_mix_g", "pre_mlp_g", "w_ff1", "w_ff2", "post_mlp_g"]
_BIG = {"w_in": 2, "w_up_dn": 1, "w_up_sw": 1, "w_o": 1, "w_ff1": 2, "w_ff2": 1}
_SMALL = [n for n in _WEIGHTS if n not in _BIG]


def _step(P):
    x, target = P["x"][0], P["loss_target"][0]
    S, D = x.shape
    L = P["pre_mix_g"].shape[0]
    H, W = DN_HEADS, DN_HEADS * DN_DK
    assert W == D == SW_Q_HEADS * SW_HD and KV_W == LANES
    me = 2 * lax.axis_index("x") + lax.axis_index("y")

    assert L % 2 == 0
    local = [P[n].astype(BF16) for n in _BIG] + [P["dn_conv_w"]]
    gathered = _gather_chips("weights_gather", local)
    full = {n: lax.dynamic_update_slice_in_dim(g, w[None], me, 0) for n, g, w in zip(list(_BIG) + ["dn_conv_w"], gathered, local)}
    rows = lambda n, l: full[n][:, l].reshape(-1, full[n].shape[-1])
    cols = lambda n, l: jnp.concatenate([full[n][s, l] for s in range(4)], axis=-1)

    tabs = _rope_tables(P["positions"].reshape(S, 1))
    lws = []
    for l in range(L):
        lws.append(dict(
            g1=P["pre_mix_g"][l][None], win=_relayout_in(full["w_in"][:, l], W, H), conv=cols("dn_conv_w", l),
            alog=_lane_row(P["dn_a_log"][l], H), dt=_lane_row(P["dn_dt_bias"][l], H), ng=P["dn_norm_g"][l][None],
            sinks=_lane_row(P["sw_sinks"][l], 0), wup_dn=rows("w_up_dn", l), wup_sw=rows("w_up_sw", l), wo=rows("w_o", l),
            g2=P["post_mix_g"][l][None], g3=P["pre_mlp_g"][l][None], wff1=cols("w_ff1", l), wff2=rows("w_ff2", l),
            g4=P["post_mlp_g"][l][None]))

    saved = []
    for l in range(L):
        x, sv = _layer_fwd(x, lws[l], tabs, W, H)
        saved.append(sv)
    loss_row, dx = _loss_head(x, target)
    layer_grads = [None] * L
    F = 4 * P["w_ff1"].shape[2]
    per_layer = dict(w_in=(D, 7 * W + 2 * KV_W + BA_W), w_up_dn=(W, D), w_up_sw=(W, D), w_o=(D, D), w_ff1=(D, F), w_ff2=(F, D))
    grads = {n: lax.empty((L,) + per_layer[n], F32) for n in _BIG}
    for l in reversed(range(L)):
        dx, layer_grads[l], grads = _layer_bwd(dx, lws[l], saved[l], tabs, W, H, l, grads)
    grads.update({n: jnp.stack([layer_grads[l][n] for l in range(L)]) for n in _SMALL})

    got = _pair_swap("grad_pair_swap", [grads[n] for n in _BIG])
    part = {n: _add_half("grad_pair_add_" + n, grads[n], r) for n, r in zip(_BIG, got)}
    items = [(_shard_in(part[n], W, H), None) if n == "w_in" else (part[n], ax) for n, ax in _BIG.items()]
    slots = _scatter_chips("grad_chip_scatter", items)
    halves = []
    for n, s, (a, ax) in zip(_BIG, slots, items):
        s = lax.dynamic_update_slice_in_dim(s, _own_part(a, ax, me)[None], me, 0)
        flat = s.reshape(4, -1, s.shape[-1])
        halves.append(_sum_slots("grad_chip_sum_" + n, flat).reshape(s.shape[1:]))
    theirs = _pair_swap("grad_pair_share", halves, whole=True)
    gsum = {n: _join_halves(h, t) for n, h, t in zip(_BIG, halves, theirs)}
    small_shapes = [(1,)] + [grads[n].shape for n in _SMALL]
    tot = _sum_slots("small_sum", _gather_all("small_gather", _pack([loss_row[0, :1]] + [grads[n] for n in _SMALL], LANES)))
    small = _unpack(tot, small_shapes)
    loss = small[0][0]
    gsum.update(zip(_SMALL, small[1:]))
    cw = P["dn_conv_w"].shape[2]
    gsum["dn_conv_w"] = lax.dynamic_slice_in_dim(gsum["dn_conv_w"], me * cw, cw, axis=2)

    delta, new_m, new_v = {}, {}, {}
    for n in _BIG:
        s = P[n].shape
        two_d = lambda a: a.reshape(s[0] * s[1], s[2])
        outs = _adamw("adamw_" + n, two_d(P[n]), two_d(gsum[n]), two_d(P["m_" + n]), two_d(P["v_" + n]))
        delta[n], new_m[n], new_v[n] = (o.reshape(s) for o in outs)
    sm_shapes = [P[n].shape for n in _SMALL]
    outs = _adamw("adamw_small", *(_pack([src[pre + n] for n in _SMALL], LANES)
                                   for src, pre in ((P, ""), (gsum, ""), (P, "m_"), (P, "v_"))))
    for d, o in zip((delta, new_m, new_v), outs):
        d.update(zip(_SMALL, _unpack(o, sm_shapes)))

    return (loss, dx[None], *[gsum[n] for n in _WEIGHTS], *[delta[n] for n in _WEIGHTS],
            *[new_m[n] for n in _WEIGHTS], *[new_v[n] for n in _WEIGHTS])


def kernel(x, positions, pre_mix_g, w_in, dn_conv_w, dn_a_log, dn_dt_bias, dn_norm_g, sw_sinks, w_up_dn, w_up_sw, w_o, post_mix_g, pre_mlp_g, w_ff1, w_ff2, post_mlp_g, loss_target, m_pre_mix_g, m_w_in, m_dn_conv_w, m_dn_a_log, m_dn_dt_bias, m_dn_norm_g, m_sw_sinks, m_w_up_dn, m_w_up_sw, m_w_o, m_post_mix_g, m_pre_mlp_g, m_w_ff1, m_w_ff2, m_post_mlp_g, v_pre_mix_g, v_w_in, v_dn_conv_w, v_dn_a_log, v_dn_dt_bias, v_dn_norm_g, v_sw_sinks, v_w_up_dn, v_w_up_sw, v_w_o, v_post_mix_g, v_pre_mlp_g, v_w_ff1, v_w_ff2, v_post_mlp_g):
    vals = (x, positions, pre_mix_g, w_in, dn_conv_w, dn_a_log, dn_dt_bias, dn_norm_g, sw_sinks, w_up_dn, w_up_sw, w_o, post_mix_g, pre_mlp_g, w_ff1, w_ff2, post_mlp_g, loss_target, m_pre_mix_g, m_w_in, m_dn_conv_w, m_dn_a_log, m_dn_dt_bias, m_dn_norm_g, m_sw_sinks, m_w_up_dn, m_w_up_sw, m_w_o, m_post_mix_g, m_pre_mlp_g, m_w_ff1, m_w_ff2, m_post_mlp_g, v_pre_mix_g, v_w_in, v_dn_conv_w, v_dn_a_log, v_dn_dt_bias, v_dn_norm_g, v_sw_sinks, v_w_up_dn, v_w_up_sw, v_w_o, v_post_mix_g, v_pre_mlp_g, v_w_ff1, v_w_ff2, v_post_mlp_g)
    names = ["x", "positions"] + _WEIGHTS + ["loss_target"] + ["m_" + n for n in _WEIGHTS] + ["v_" + n for n in _WEIGHTS]
    return _step(dict(zip(names, vals)))
```

```python
import functools

import numpy as np
import jax
import jax.numpy as jnp
from jax import lax
from jax.experimental import pallas as pl
from jax.experimental.pallas import tpu as pltpu

F32, BF16 = jnp.float32, jnp.bfloat16
MESH = pl.DeviceIdType.MESH

DN_HEADS = 8
DN_DK = 128
DN_CONV = 4
DN_CHUNK = 64
SW_Q_HEADS = 16
SW_KV_HEADS = 2
SW_HD = 64
SW_BLOCK = 128
ROPE_THETA = 500000.0
ROT_DIM = SW_HD // 4
EPS = 1e-6
ADAM_LR, ADAM_B1, ADAM_B2, ADAM_EPS, ADAM_WD, ADAM_STEP = 0.001, 0.9, 0.999, 1e-08, 0.01, 10

LANES = 128
SUBLANES = 8
VMEM_LIMIT = 48 * 1024 * 1024
KV_W = SW_KV_HEADS * SW_HD
BA_W = 256
PACK_ROWS = 512
DELTA_CB = 4
DELTA_HB = 8


def _pcall(body, **kw):
    return pl.pallas_call(body, **kw)


def _cp(*sem):
    return pltpu.CompilerParams(dimension_semantics=sem, vmem_limit_bytes=VMEM_LIMIT)


def _tile(n, pref, unit=LANES):
    if n <= pref:
        return n
    t = (pref // unit) * unit
    while t > unit and n % t:
        t -= unit
    assert n % t == 0, (n, pref)
    return t


def _sds(shape, dtype):
    return jax.ShapeDtypeStruct(tuple(shape), dtype)


_DIMS = {"nn": ((1,), (0,)), "nt": ((1,), (1,)), "tn": ((0,), (0,))}


def _mm(name, a, b, mode, out_dtype=F32, tm=1024, tn=1024, tk=1024, extras=(), epi=None, out_dtypes=None, slab=None):
    if mode == "nn":
        (M, K), (_, N) = a.shape, b.shape
    elif mode == "nt":
        (M, K), (N, _) = a.shape, b.shape
    else:
        (K, M), (_, N) = a.shape, b.shape
    tm, tn, tk = _tile(M, tm), _tile(N, tn), _tile(K, tk)
    nk = K // tk
    a_spec = {"nn": pl.BlockSpec((tm, tk), lambda i, j, k: (i, k)),
              "nt": pl.BlockSpec((tm, tk), lambda i, j, k: (i, k)),
              "tn": pl.BlockSpec((tk, tm), lambda i, j, k: (k, i))}[mode]
    b_spec = {"nn": pl.BlockSpec((tk, tn), lambda i, j, k: (k, j)),
              "nt": pl.BlockSpec((tn, tk), lambda i, j, k: (j, k)),
              "tn": pl.BlockSpec((tk, tn), lambda i, j, k: (k, j))}[mode]
    dims = (_DIMS[mode], ((), ()))
    out_dtypes = tuple(out_dtypes or (out_dtype,))
    ne, no = len(extras), len(out_dtypes)
    o_spec = pl.BlockSpec((tm, tn), lambda i, j, k: (i, j))

    def body(*refs):
        a_ref, b_ref, ex = refs[0], refs[1], refs[2:2 + ne]
        outs = refs[-no:] if nk == 1 else refs[-1 - no:-1]
        part = lax.dot_general(a_ref[...], b_ref[...], dims, preferred_element_type=F32)

        def finish(acc):
            res = epi(acc, *[e[...] for e in ex]) if epi else (acc,)
            for o, r, dt in zip(outs, res, out_dtypes):
                if slab is None:
                    o[...] = r.astype(dt)
                else:
                    o[0] = r.astype(dt)

        if nk == 1:
            finish(part)
            return
        acc_ref, k = refs[-1], pl.program_id(2)

        @pl.when(k == 0)
        def _():
            acc_ref[...] = part

        @pl.when((k > 0) & (k < nk - 1))
        def _():
            acc_ref[...] += part

        @pl.when(k == nk - 1)
        def _():
            finish(acc_ref[...] + part)

    kw = dict(name=name, grid=(M // tm, N // tn, nk), scratch_shapes=[] if nk == 1 else [pltpu.VMEM((tm, tn), F32)],
              compiler_params=_cp("parallel", "parallel", "arbitrary"))
    if slab is not None:
        buf, l = slab
        return _pcall(body, in_specs=[a_spec, b_spec, ANY], out_specs=pl.BlockSpec((1, tm, tn), lambda i, j, k: (l, i, j)),
                      out_shape=_sds(buf.shape, buf.dtype), input_output_aliases={2: 0}, **kw)(a, b, buf)
    out = _pcall(body, in_specs=[a_spec, b_spec] + [o_spec] * ne, out_specs=tuple(o_spec for _ in out_dtypes),
                 out_shape=tuple(_sds((M, N), dt) for dt in out_dtypes), **kw)(a, b, *extras)
    return out if no > 1 else out[0]


def _rows(name, fn, n_rows, tq, ins, in_specs, out_shapes, out_specs):
    def body(*refs):
        fn(pl.program_id(0), *refs)

    return _pcall(body, name=name, grid=(n_rows // tq,), in_specs=in_specs, out_specs=out_specs,
                  out_shape=out_shapes, compiler_params=_cp("arbitrary"))(*ins)


def _rb(tq, w, cb=0):
    return pl.BlockSpec((tq, w), lambda i: (i, cb))


def _full(shape):
    return pl.BlockSpec(tuple(shape), lambda *_: (0,) * len(shape))


def _rms_fwd(x, g):
    r = lax.rsqrt(jnp.mean(x * x, axis=-1, keepdims=True) + EPS)
    return x * r * g


def _rms_bwd(x, g, dy):
    r = lax.rsqrt(jnp.mean(x * x, axis=-1, keepdims=True) + EPS)
    xh = x * r
    t = dy * g
    dx = r * (t - xh * jnp.mean(t * xh, axis=-1, keepdims=True))
    return dx, jnp.sum(dy * xh, axis=0, keepdims=True)


def _acc(i, ref, val):
    @pl.when(i == 0)
    def _():
        ref[...] = val

    @pl.when(i > 0)
    def _():
        ref[...] += val


def _sigmoid(x):
    return 0.5 * jnp.tanh(0.5 * x) + 0.5


def _pre_norm(x, g):
    S, D = x.shape
    tq = _tile(S, 512, SUBLANES)

    def fn(i, x_ref, g_ref, h_ref):
        h_ref[...] = _rms_fwd(x_ref[...], g_ref[...]).astype(BF16)

    return _rows("pre_norm", fn, S, tq, (x, g), [_rb(tq, D), _full((1, D))], _sds((S, D), BF16), _rb(tq, D))


def _post_mix(x, mix, g2, g3):
    S, D = x.shape
    tq = _tile(S, 512, SUBLANES)

    def fn(i, x_ref, m_ref, g2_ref, g3_ref, x1_ref, h2_ref):
        x1 = x_ref[...] + _rms_fwd(m_ref[...], g2_ref[...])
        x1_ref[...] = x1
        h2_ref[...] = _rms_fwd(x1, g3_ref[...]).astype(BF16)

    return _rows("post_mix", fn, S, tq, (x, mix, g2, g3), [_rb(tq, D), _rb(tq, D), _full((1, D)), _full((1, D))],
                 (_sds((S, D), F32), _sds((S, D), BF16)), (_rb(tq, D), _rb(tq, D)))


def _post_mlp(x1, ff, g4):
    S, D = x1.shape
    tq = _tile(S, 512, SUBLANES)

    def fn(i, x_ref, f_ref, g_ref, o_ref):
        o_ref[...] = x_ref[...] + _rms_fwd(f_ref[...], g_ref[...])

    return _rows("post_mlp", fn, S, tq, (x1, ff, g4), [_rb(tq, D), _rb(tq, D), _full((1, D))], _sds((S, D), F32), _rb(tq, D))


def _loss_head(y, target):
    S, D = y.shape
    tq = _tile(S, 512, SUBLANES)

    def fn(i, y_ref, t_ref, l_ref, d_ref):
        e = y_ref[...] - t_ref[...]
        d_ref[...] = e * (1.0 / D)
        part = jnp.sum(jnp.sum(e * e, axis=1, keepdims=True), axis=0, keepdims=True) * (0.5 / D)
        _acc(i, l_ref, jnp.broadcast_to(part, (1, LANES)))

    return _rows("loss_head", fn, S, tq, (y, target), [_rb(tq, D), _rb(tq, D)],
                 (_sds((1, LANES), F32), _sds((S, D), F32)), (_full((1, LANES)), _rb(tq, D)))


def _post_mlp_bwd(ff, g4, dx2):
    S, D = ff.shape
    tq = _tile(S, 512, SUBLANES)

    def fn(i, f_ref, g_ref, d_ref, o_ref, dg_ref):
        dx, dg = _rms_bwd(f_ref[...], g_ref[...], d_ref[...])
        o_ref[...] = dx.astype(BF16)
        _acc(i, dg_ref, dg)

    return _rows("post_mlp_bwd", fn, S, tq, (ff, g4, dx2), [_rb(tq, D), _full((1, D)), _rb(tq, D)],
                 (_sds((S, D), BF16), _sds((1, D), F32)), (_rb(tq, D), _full((1, D))))


def _mid_bwd(x1, g3, dh2, dx2, mix, g2):
    S, D = x1.shape
    tq = _tile(S, 256, SUBLANES)

    def fn(i, x_ref, g3_ref, dh_ref, dx2_ref, m_ref, g2_ref, dx1_ref, dm_ref, dg3_ref, dg2_ref):
        d, dg3 = _rms_bwd(x_ref[...], g3_ref[...], dh_ref[...])
        dx1 = dx2_ref[...] + d
        dx1_ref[...] = dx1
        dm, dg2 = _rms_bwd(m_ref[...], g2_ref[...], dx1)
        dm_ref[...] = dm.astype(BF16)
        _acc(i, dg3_ref, dg3)
        _acc(i, dg2_ref, dg2)

    r, f = _rb(tq, D), _full((1, D))
    return _rows("mid_bwd", fn, S, tq, (x1, g3, dh2, dx2, mix, g2), [r, f, r, r, r, f],
                 (_sds((S, D), F32), _sds((S, D), BF16), _sds((1, D), F32), _sds((1, D), F32)), (r, r, f, f))


def _pre_norm_bwd(x, g1, dh, dx1):
    S, D = x.shape
    tq = _tile(S, 512, SUBLANES)

    def fn(i, x_ref, g_ref, dh_ref, dx1_ref, dx_ref, dg_ref):
        d, dg = _rms_bwd(x_ref[...], g_ref[...], dh_ref[...])
        dx_ref[...] = dx1_ref[...] + d
        _acc(i, dg_ref, dg)

    r, f = _rb(tq, D), _full((1, D))
    return _rows("pre_norm_bwd", fn, S, tq, (x, g1, dh, dx1), [r, f, r, r], (_sds((S, D), F32), _sds((1, D), F32)), (r, f))


def _mix(proj, ya, yb, D, cb_a):
    S = ya.shape[0]
    tq = _tile(S, 256, SUBLANES)

    def fn(i, ga_ref, gb_ref, ya_ref, yb_ref, o_ref):
        ga, gb, ya, yb = (r[...].astype(F32) for r in (ga_ref, gb_ref, ya_ref, yb_ref))
        o_ref[...] = (_sigmoid(ga) * ya + _sigmoid(gb) * yb).astype(BF16)

    return _rows("mix", fn, S, tq, (proj, proj, ya, yb), [_rb(tq, D, cb_a), _rb(tq, D, cb_a + 1), _rb(tq, D), _rb(tq, D)],
                 _sds((S, D), BF16), _rb(tq, D))


def _mix_bwd(proj, ya, yb, dmixin, D, cb_a):
    S = ya.shape[0]
    tq = _tile(S, 256, SUBLANES)

    def fn(i, ga_ref, gb_ref, ya_ref, yb_ref, d_ref, dya_ref, dyb_ref, dga_ref, dgb_ref):
        ga, gb, ya, yb, d = (r[...].astype(F32) for r in (ga_ref, gb_ref, ya_ref, yb_ref, d_ref))
        sa, sb = _sigmoid(ga), _sigmoid(gb)
        dya_ref[...] = (d * sa).astype(BF16)
        dyb_ref[...] = (d * sb).astype(BF16)
        dga_ref[...] = (d * ya * sa * (1.0 - sa)).astype(BF16)
        dgb_ref[...] = (d * yb * sb * (1.0 - sb)).astype(BF16)

    r = _rb(tq, D)
    o = _sds((S, D), BF16)
    return _rows("mix_bwd", fn, S, tq, (proj, proj, ya, yb, dmixin), [_rb(tq, D, cb_a), _rb(tq, D, cb_a + 1), r, r, r],
                 (o, o, o, o), (r, r, r, r))


HALO = 16


def _shift_down(xe, k, tq):
    return pltpu.roll(xe, k, 0)[HALO:HALO + tq]


def _conv_pre(cur_ref, halo_ref, w_ref, i, tq):
    x = cur_ref[...].astype(F32)
    halo = jnp.where(i > 0, halo_ref[...].astype(F32), 0.0)
    xe = jnp.concatenate([halo, x], axis=0)
    xs = [x] + [_shift_down(xe, k, tq) for k in range(1, DN_CONV)]
    w = w_ref[...]
    c = sum(w[DN_CONV - 1 - k:DN_CONV - k, :] * xs[k] for k in range(DN_CONV))
    return c, xs


def _dn_prep(proj, conv_w, W):
    S = proj.shape[0]
    tq = _tile(S, 256, HALO)
    hb = tq // HALO

    def body(cur_ref, halo_ref, w_ref, o_ref):
        j, i = pl.program_id(0), pl.program_id(1)
        c, _ = _conv_pre(cur_ref, halo_ref, w_ref, i, tq)
        y = c * _sigmoid(c)
        scale = jnp.where(j == 0, DN_DK ** -0.5, 1.0)
        for h in range(W // DN_DK):
            sl = slice(h * DN_DK, (h + 1) * DN_DK)
            yh = y[:, sl]
            rs = lax.rsqrt(jnp.sum(yh * yh, axis=-1, keepdims=True) + EPS)
            o_ref[:, sl] = jnp.where(j == 2, yh, yh * rs * scale)

    return _pcall(body, name="dn_prep", grid=(3, S // tq),
                  in_specs=[pl.BlockSpec((tq, W), lambda j, i: (i, j)),
                            pl.BlockSpec((HALO, W), lambda j, i: (jnp.maximum(i * hb - 1, 0), j)),
                            pl.BlockSpec((DN_CONV, W), lambda j, i: (0, j))],
                  out_specs=pl.BlockSpec((tq, W), lambda j, i: (i, j)), out_shape=_sds((S, 3 * W), F32),
                  compiler_params=_cp("arbitrary", "arbitrary"))(proj, proj, conv_w)


def _dn_prep_bwd_a(proj, conv_w, dqkv, W):
    S = proj.shape[0]
    tq = _tile(S, 256, HALO)
    hb = tq // HALO

    def body(cur_ref, halo_ref, w_ref, d_ref, dc_ref, dw_ref):
        j, i = pl.program_id(0), pl.program_id(1)
        c, xs = _conv_pre(cur_ref, halo_ref, w_ref, i, tq)
        sg = _sigmoid(c)
        y = c * sg
        scale = jnp.where(j == 0, DN_DK ** -0.5, 1.0)
        dout = d_ref[0]
        dys = []
        for h in range(W // DN_DK):
            sl = slice(h * DN_DK, (h + 1) * DN_DK)
            yh, dh = y[:, sl], dout[:, sl]
            rs = lax.rsqrt(jnp.sum(yh * yh, axis=-1, keepdims=True) + EPS)
            yn = yh * rs
            dn = scale * rs * (dh - yn * jnp.sum(dh * yn, axis=-1, keepdims=True))
            dys.append(jnp.where(j == 2, dh, dn))
        dy = jnp.concatenate(dys, axis=1)
        dc = dy * (sg * (1.0 + c * (1.0 - sg)))
        dc_ref[...] = dc
        dw = jnp.concatenate([jnp.sum(dc * xs[DN_CONV - 1 - r], axis=0, keepdims=True) for r in range(DN_CONV)], axis=0)
        _acc(i, dw_ref, dw)

    return _pcall(body, name="dn_prep_bwd_a", grid=(3, S // tq),
                  in_specs=[pl.BlockSpec((tq, W), lambda j, i: (i, j)),
                            pl.BlockSpec((HALO, W), lambda j, i: (jnp.maximum(i * hb - 1, 0), j)),
                            pl.BlockSpec((DN_CONV, W), lambda j, i: (0, j)),
                            pl.BlockSpec((1, tq, W), lambda j, i: (j, i, 0))],
                  out_specs=(pl.BlockSpec((tq, W), lambda j, i: (i, j)), pl.BlockSpec((DN_CONV, W), lambda j, i: (0, j))),
                  out_shape=(_sds((S, 3 * W), F32), _sds((DN_CONV, 3 * W), F32)),
                  compiler_params=_cp("arbitrary", "arbitrary"))(proj, proj, conv_w, dqkv)


def _dn_prep_bwd_b(dc, conv_w, W):
    S = dc.shape[0]
    tq = _tile(S, 256, SUBLANES)
    hb = tq // SUBLANES
    nblk = S // tq

    def body(cur_ref, nxt_ref, w_ref, o_ref):
        i = pl.program_id(1)
        d = cur_ref[...]
        nxt = jnp.where(i < nblk - 1, nxt_ref[...], 0.0)
        de = jnp.concatenate([d, nxt], axis=0)
        w = w_ref[...]
        out = w[DN_CONV - 1:DN_CONV, :] * d
        for k in range(1, DN_CONV):
            out = out + w[DN_CONV - 1 - k:DN_CONV - k, :] * pltpu.roll(de, tq + SUBLANES - k, 0)[0:tq]
        o_ref[...] = out.astype(BF16)

    return _pcall(body, name="dn_prep_bwd_b", grid=(3, nblk),
                  in_specs=[pl.BlockSpec((tq, W), lambda j, i: (i, j)),
                            pl.BlockSpec((SUBLANES, W), lambda j, i: (jnp.minimum((i + 1) * hb, S // SUBLANES - 1), j)),
                            pl.BlockSpec((DN_CONV, W), lambda j, i: (0, j))],
                  out_specs=pl.BlockSpec((tq, W), lambda j, i: (i, j)), out_shape=_sds((S, 3 * W), BF16),
                  compiler_params=_cp("arbitrary", "arbitrary"))(dc, dc, conv_w)


def _gate_terms(ba, al, dt):
    u = ba + dt
    sp = jnp.maximum(u, 0.0) + jnp.log(1.0 + jnp.exp(-jnp.abs(u)))
    return _sigmoid(ba), -jnp.exp(al) * sp, u


def _dn_gates(ba, alog_row, dt_row, H):
    S = ba.shape[0]
    tq = _tile(S, 512, SUBLANES)
    W = H * DN_DK

    def fn(i, ba_ref, al_ref, dt_ref, be_ref, g_ref):
        bet, gg, _ = _gate_terms(ba_ref[...], al_ref[...], dt_ref[...])
        for h in range(H):
            sl = slice(h * DN_DK, (h + 1) * DN_DK)
            be_ref[:, sl] = jnp.broadcast_to(bet[:, h:h + 1], (tq, DN_DK))
            g_ref[:, sl] = jnp.broadcast_to(gg[:, H + h:H + h + 1], (tq, DN_DK))

    return _rows("dn_gates", fn, S, tq, (ba, alog_row, dt_row), [_rb(tq, LANES), _full((1, LANES)), _full((1, LANES))],
                 (_sds((S, W), F32), _sds((S, W), F32)), (_rb(tq, W), _rb(tq, W)))


def _dn_gates_bwd(ba, alog_row, dt_row, dbeta_b, dg_b, H):
    S = ba.shape[0]
    tq = _tile(S, 512, SUBLANES)
    W = H * DN_DK

    def fn(i, ba_ref, al_ref, dt_ref, db_ref, dg_ref, o_ref, dal_ref, ddt_ref):
        bet, gg, u = _gate_terms(ba_ref[...], al_ref[...], dt_ref[...])
        lane = lax.broadcasted_iota(jnp.int32, (tq, LANES), 1)
        d = jnp.zeros((tq, LANES), F32)
        for h in range(H):
            d = jnp.where(lane == h, db_ref[:, h * DN_DK:h * DN_DK + 1], d)
            d = jnp.where(lane == H + h, dg_ref[:, h * DN_DK:h * DN_DK + 1], d)
        is_a = (lane >= H) & (lane < 2 * H)
        da = jnp.where(is_a, d * (-jnp.exp(al_ref[...]) * _sigmoid(u)), 0.0)
        dlog = jnp.where(lane < H, d * bet * (1.0 - bet), da)
        o_ref[...] = jnp.concatenate([dlog, jnp.zeros((tq, BA_W - LANES), F32)], axis=1).astype(BF16)
        _acc(i, dal_ref, jnp.sum(jnp.where(is_a, d * gg, 0.0), axis=0, keepdims=True))
        _acc(i, ddt_ref, jnp.sum(da, axis=0, keepdims=True))

    f = _full((1, LANES))
    return _rows("dn_gates_bwd", fn, S, tq, (ba, alog_row, dt_row, dbeta_b, dg_b),
                 [_rb(tq, LANES), f, f, _rb(tq, W), _rb(tq, W)],
                 (_sds((S, BA_W), BF16), _sds((1, LANES), F32), _sds((1, LANES), F32)), (_rb(tq, BA_W), f, f))


def _dn_out(o, proj, ng, W, cb_z):
    S = o.shape[0]
    tq = _tile(S, 256, SUBLANES)

    def fn(i, o_ref, z_ref, g_ref, y_ref):
        for h in range(W // DN_DK):
            sl = slice(h * DN_DK, (h + 1) * DN_DK)
            z = z_ref[:, sl].astype(F32)
            y_ref[:, sl] = (_rms_fwd(o_ref[:, sl], g_ref[...]) * (z * _sigmoid(z))).astype(BF16)

    return _rows("dn_out", fn, S, tq, (o, proj, ng), [_rb(tq, W), _rb(tq, W, cb_z), _full((1, DN_DK))], _sds((S, W), BF16), _rb(tq, W))


def _dn_out_bwd(o, proj, ng, dy, W, cb_z):
    S = o.shape[0]
    tq = _tile(S, 256, SUBLANES)

    def fn(i, o_ref, z_ref, g_ref, d_ref, do_ref, dz_ref, dg_ref):
        g = g_ref[...]
        dg = jnp.zeros((1, DN_DK), F32)
        for h in range(W // DN_DK):
            sl = slice(h * DN_DK, (h + 1) * DN_DK)
            oh, z, d = o_ref[:, sl], z_ref[:, sl].astype(F32), d_ref[:, sl].astype(F32)
            sg = _sigmoid(z)
            dn = d * (z * sg)
            dz_ref[:, sl] = (d * _rms_fwd(oh, g) * (sg * (1.0 + z * (1.0 - sg)))).astype(BF16)
            dx, dgh = _rms_bwd(oh, g, dn)
            do_ref[:, sl] = dx
            dg = dg + dgh
        _acc(i, dg_ref, dg)

    r = _rb(tq, W)
    return _rows("dn_out_bwd", fn, S, tq, (o, proj, ng, dy), [r, _rb(tq, W, cb_z), _full((1, DN_DK)), r],
                 (_sds((S, W), F32), _sds((S, W), BF16), _sds((1, DN_DK), F32)), (r, r, _full((1, DN_DK))))


def _bdot(a, b, mode="nn"):
    return lax.dot_general(a.astype(BF16), b.astype(BF16), (_DIMS[mode], ((), ())), preferred_element_type=F32)


def _rsum(x):
    return jnp.broadcast_to(jnp.sum(x, axis=-1, keepdims=True), x.shape)


def _dot3(a, b, mode="nn"):
    ah, bh = a.astype(BF16), b.astype(BF16)
    al, bl = (a - ah.astype(F32)).astype(BF16), (b - bh.astype(F32)).astype(BF16)
    d = lambda x, y: lax.dot_general(x, y, (_DIMS[mode], ((), ())), preferred_element_type=F32)
    return d(ah, bh) + (d(al, bh) + d(ah, bl))


def _cumsum_rows(x, reverse=False):
    n = x.shape[0]
    row = lax.broadcasted_iota(jnp.int32, x.shape, 0)
    s = 1
    while s < n:
        if reverse:
            x = x + jnp.where(row < n - s, pltpu.roll(x, n - s, 0), 0.0)
        else:
            x = x + jnp.where(row >= s, pltpu.roll(x, s, 0), 0.0)
        s *= 2
    return x


def _each(f, *lists):
    return [f(*a) for a in zip(*lists)]


def _delta_local(qs, ks, vs, bes, grs):
    C = DN_CHUNK
    ri = lax.broadcasted_iota(jnp.int32, (C, C), 0)
    ci = lax.broadcasted_iota(jnp.int32, (C, C), 1)
    causal, strict = ri >= ci, ri > ci
    gcs = [_cumsum_rows(g) for g in grs]
    decays = [jnp.where(causal, jnp.exp(jnp.where(causal, gc[:, :C] - gc.T[:C, :], 0.0)), 0.0) for gc in gcs]
    egs = [jnp.exp(gc) for gc in gcs]
    eks = [jnp.exp(gc[C - 1:C, :] - gc) for gc in gcs]
    gams = [jnp.exp(gc[C - 1:C, :]) for gc in gcs]
    kbs = _each(lambda k, be: k * be, ks, bes)
    kks = _each(lambda kb, k: _bdot(kb, k, "nt"), kbs, ks)
    nls = _each(lambda kk, dc: jnp.where(strict, -kk * dc, 0.0), kks, decays)
    eye = (ri == ci).astype(F32)
    ts = [eye + nl for nl in nls]
    pws = [_dot3(nl, nl) for nl in nls]
    for s in range(4):
        both = _each(lambda t, pw: _dot3(jnp.concatenate([t, pw], axis=0), pw), ts, pws)
        ts = _each(lambda t, b: t + b[:C], ts, both)
        pws = [b[C:] for b in both]
    ts = _each(lambda t, pw: t + _dot3(t, pw), ts, pws)
    vbs = _each(lambda v, be: v * be, vs, bes)
    kbes = _each(lambda kb, eg: kb * eg, kbs, egs)
    uws = _each(lambda t, vb, kbe: _dot3(t, jnp.concatenate([vb, kbe], axis=1)), ts, vbs, kbes)
    us, ws = [uw[:, :DN_DK] for uw in uws], [uw[:, DN_DK:] for uw in uws]
    qks = _each(lambda q, k: _bdot(q, k, "nt"), qs, ks)
    return dict(decay=decays, eg=egs, ek=eks, gam=gams, kb=kbs, kk=kks, t=ts, vb=vbs, kbe=kbes, u=us, w=ws, qk=qks,
                a=_each(lambda qk, dc: qk * dc, qks, decays), qd=_each(lambda q, eg: q * eg, qs, egs),
                kd=_each(lambda k, ek: k * ek, ks, eks), strict=strict)


def _delta_items(refs, CB, HB):
    C, dk = DN_CHUNK, DN_DK
    return [[r[c * C:(c + 1) * C, h * dk:(h + 1) * dk] for h in range(HB) for c in range(CB)] for r in refs]


def _delta_fwd(qkv, beta_b, g_b, H, CB, HB):
    S = qkv.shape[0]
    C, dk = DN_CHUNK, DN_DK
    N = S // C
    R = CB * C
    G = H // HB

    def body(q_ref, k_ref, v_ref, b_ref, g_ref, o_ref, st_ref, s_ref):
        @pl.when(pl.program_id(1) == 0)
        def _():
            s_ref[...] = jnp.zeros((HB, dk, dk), F32)

        L = _delta_local(*_delta_items((q_ref, k_ref, v_ref, b_ref, g_ref), CB, HB))
        ss = [s_ref[h] for h in range(HB)]
        for c in range(CB):
            it = [h * CB + c for h in range(HB)]
            for h in range(HB):
                st_ref[h, c] = ss[h]
            wq = [_bdot(jnp.concatenate([L["w"][i], L["qd"][i]], axis=0), s) for i, s in zip(it, ss)]
            vns = [L["u"][i] - x[:C] for i, x in zip(it, wq)]
            outs = [x[C:] + _bdot(L["a"][i], vn) for i, x, vn in zip(it, wq, vns)]
            ss = [s * L["gam"][i] + _bdot(L["kd"][i], vn, "tn") for i, s, vn in zip(it, ss, vns)]
            for h in range(HB):
                o_ref[c * C:(c + 1) * C, h * dk:(h + 1) * dk] = outs[h]
        for h in range(HB):
            s_ref[h] = ss[h]

    blk = lambda off: pl.BlockSpec((R, HB * dk), lambda h, n: (n, off + h))
    return _pcall(body, name="delta_fwd", grid=(G, N // CB),
                  in_specs=[blk(0), blk(G), blk(2 * G), blk(0), blk(0)],
                  out_specs=(blk(0), pl.BlockSpec((HB, CB, dk, dk), lambda h, n: (h, n, 0, 0))),
                  out_shape=(_sds((S, H * dk), F32), _sds((H, N, dk, dk), F32)),
                  scratch_shapes=[pltpu.VMEM((HB, dk, dk), F32)],
                  compiler_params=_cp("arbitrary", "arbitrary"))(qkv, qkv, qkv, beta_b, g_b)


def _delta_bwd(qkv, beta_b, g_b, states, do, H, CB, HB):
    S = qkv.shape[0]
    C, dk = DN_CHUNK, DN_DK
    N = S // C
    R = CB * C
    NB = N // CB
    G = H // HB

    def body(q_ref, k_ref, v_ref, b_ref, g_ref, st_ref, do_ref, dqkv_ref, db_ref, dg_ref, ds_ref):
        @pl.when(pl.program_id(1) == 0)
        def _():
            ds_ref[...] = jnp.zeros((HB, dk, dk), F32)

        qs, ks, vs, bes, grs, dos = _delta_items((q_ref, k_ref, v_ref, b_ref, g_ref, do_ref), CB, HB)
        L = _delta_local(qs, ks, vs, bes, grs)
        ts, decays, kbs, egs, eks, gams, qds, kds = (L[n] for n in ("t", "decay", "kb", "eg", "ek", "gam", "qd", "kd"))
        s0s = [st_ref[h, c] for h in range(HB) for c in range(CB)]
        vns = _each(lambda u, w, s0: u - _bdot(w, s0), L["u"], L["w"], s0s)
        pre_dvn = _each(lambda a, d: _bdot(a, d, "tn"), L["a"], dos)
        pre_ds = _each(lambda qd, d: _bdot(qd, d, "tn"), qds, dos)
        das = _each(lambda d, vn: _bdot(d, vn, "nt"), dos, vns)
        ds = [ds_ref[h] for h in range(HB)]
        ds1s, dvns = [None] * (HB * CB), [None] * (HB * CB)
        for c in reversed(range(CB)):
            it = [h * CB + c for h in range(HB)]
            new = [pre_dvn[i] + _bdot(kds[i], d) for i, d in zip(it, ds)]
            for i, d, dv in zip(it, ds, new):
                ds1s[i], dvns[i] = d, dv
            ds = [pre_ds[i] + d * gams[i] - _bdot(L["w"][i], dv, "tn") for i, d, dv in zip(it, ds, new)]
        for h in range(HB):
            ds_ref[h] = ds[h]
        dkds = _each(lambda vn, d1: _bdot(vn, d1, "nt"), vns, ds1s)
        dgams = _each(lambda s0, d1: jnp.sum(jnp.sum(s0 * d1, axis=1, keepdims=True), axis=0, keepdims=True), s0s, ds1s)
        ost = _each(lambda d, dv, s0: _bdot(jnp.concatenate([d, dv], axis=0), s0, "nt"), dos, dvns, s0s)
        dqds, dws = [x[:C] for x in ost], [-x[C:] for x in ost]
        dvw = _each(lambda dv, dw: jnp.concatenate([dv, dw], axis=1), dvns, dws)
        tdvw = _each(lambda t, x: _dot3(t, x, "tn"), ts, dvw)
        dvbs, dkbes = [x[:, :dk] for x in tdvw], [x[:, dk:] for x in tdvw]
        dts = _each(lambda x, vb, kbe: _dot3(x, jnp.concatenate([vb, kbe], axis=1), "nt"), dvw, L["vb"], L["kbe"])
        tmp = _each(lambda dt, t: _dot3(dt, t, "nt"), dts, ts)
        dls = _each(lambda t, x: -_dot3(t, x, "tn"), ts, tmp)
        ms = _each(lambda dl, dc: jnp.where(L["strict"], dl * dc, 0.0), dls, decays)
        mas = _each(lambda da, dc: da * dc, das, decays)
        dkbs = _each(lambda m, k, dkbe, eg: _bdot(m, k) + dkbe * eg, ms, ks, dkbes, egs)
        dks = _each(lambda m, kb, ma, q, dkd, ek, dkb, be: _bdot(m, kb, "tn") + _bdot(ma, q, "tn") + dkd * ek + dkb * be,
                    ms, kbs, mas, qs, dkds, eks, dkbs, bes)
        dqs = _each(lambda ma, k, dqd, eg: _bdot(ma, k) + dqd * eg, mas, ks, dqds, egs)
        es = _each(lambda m, kk, ma, qk: m * kk + ma * qk, ms, L["kk"], mas, L["qk"])
        ones = jnp.ones((C, dk), BF16)
        row = lax.broadcasted_iota(jnp.int32, (C, dk), 0)
        for i in range(HB * CB):
            h, c = divmod(i, CB)
            rs, cs = slice(c * C, (c + 1) * C), slice(h * dk, (h + 1) * dk)
            e = es[i]
            e_hi = e.astype(BF16)
            col = _bdot(e_hi, ones, "tn") + _bdot(e - e_hi.astype(F32), ones, "tn")
            t_kd = _rsum(dkds[i] * kds[i])
            dgc = (jnp.broadcast_to(jnp.sum(e, axis=1, keepdims=True), (C, dk)) - col + _rsum(dqds[i] * qds[i]) - t_kd
                   + _rsum(dkbes[i] * L["kbe"][i]))
            dglast = jnp.sum(t_kd[:, 0:1], axis=0, keepdims=True) + dgams[i] * gams[i][:, 0:1]
            dgc = dgc + jnp.where(row == C - 1, dglast, 0.0)
            dqkv_ref[0, rs, cs] = dqs[i]
            dqkv_ref[1, rs, cs] = dks[i]
            dqkv_ref[2, rs, cs] = dvbs[i] * bes[i]
            db_ref[rs, cs] = _rsum(dkbs[i] * ks[i]) + _rsum(dvbs[i] * vs[i])
            dg_ref[rs, cs] = _cumsum_rows(dgc, reverse=True)

    blk = lambda off: pl.BlockSpec((R, HB * dk), lambda h, n: (NB - 1 - n, off + h))
    W = H * dk
    return _pcall(body, name="delta_bwd", grid=(G, NB),
                  in_specs=[blk(0), blk(G), blk(2 * G), blk(0), blk(0),
                            pl.BlockSpec((HB, CB, dk, dk), lambda h, n: (h, NB - 1 - n, 0, 0)), blk(0)],
                  out_specs=(pl.BlockSpec((3, R, HB * dk), lambda h, n: (0, NB - 1 - n, h)), blk(0), blk(0)),
                  out_shape=(_sds((3, S, W), F32), _sds((S, W), F32), _sds((S, W), F32)),
                  scratch_shapes=[pltpu.VMEM((HB, dk, dk), F32)],
                  compiler_params=_cp("arbitrary", "arbitrary"))(qkv, qkv, qkv, beta_b, g_b, states, do)


def _rope_consts():
    lane = np.arange(LANES) % SW_HD
    half = ROT_DIM // 2
    inv = (ROPE_THETA ** (-np.arange(half, dtype=np.float32) * np.float32(2.0 / ROT_DIM))).astype(np.float32)
    freq = np.where(lane < ROT_DIM, inv[lane % half], 0.0).astype(np.float32)
    lo = (lane < half).astype(np.float32)
    hi = ((lane >= half) & (lane < ROT_DIM)).astype(np.float32)
    return jnp.asarray(np.stack([freq, -lo, hi] + [np.zeros(LANES, np.float32)] * 5))


def _rope_tables(pos_col):
    S = pos_col.shape[0]
    tq = _tile(S, 1024, SUBLANES)

    def fn(i, p_ref, c_ref, cos_ref, s1_ref, s2_ref):
        ang = p_ref[...].astype(F32) * c_ref[0:1, :]
        sn = jnp.sin(ang)
        cos_ref[...] = jnp.cos(ang)
        s1_ref[...] = sn * c_ref[1:2, :]
        s2_ref[...] = sn * c_ref[2:3, :]

    o, r = _sds((S, LANES), F32), _rb(tq, LANES)
    return _rows("rope_tables", fn, S, tq, (pos_col, _rope_consts()), [_rb(tq, 1), _full((SUBLANES, LANES))], (o, o, o), (r, r, r))


def _wide(a, w):
    return a if w == LANES else jnp.tile(a, (1, w // LANES))


def _rope(x, cos, s1, s2):
    w, h = x.shape[1], ROT_DIM // 2
    return x * _wide(cos, w) + pltpu.roll(x, w - h, 1) * _wide(s1, w) + pltpu.roll(x, h, 1) * _wide(s2, w)


def _unrope(d, cos, s1, s2):
    w, h = d.shape[1], ROT_DIM // 2
    return d * _wide(cos, w) + pltpu.roll(d * _wide(s1, w), h, 1) + pltpu.roll(d * _wide(s2, w), w - h, 1)


def _swa_setup(n, q_ref, kc_ref, kp_ref, vc_ref, vp_ref, tc, tp):
    B = SW_BLOCK
    qr = _rope(q_ref[...].astype(F32), tc[0][...], tc[1][...], tc[2][...]) * (SW_HD ** -0.5)
    kw = jnp.concatenate([_rope(kp_ref[...].astype(F32), tp[0][...], tp[1][...], tp[2][...]),
                          _rope(kc_ref[...].astype(F32), tc[0][...], tc[1][...], tc[2][...])], axis=0)
    vw = jnp.concatenate([vp_ref[...], vc_ref[...]], axis=0).astype(F32)
    lane = lax.broadcasted_iota(jnp.int32, (2 * B, LANES), 1)
    heads = []
    for hk in range(SW_KV_HEADS):
        kh, vh = kw[:, hk * SW_HD:(hk + 1) * SW_HD], vw[:, hk * SW_HD:(hk + 1) * SW_HD]
        kk, vv = jnp.concatenate([kh, kh], axis=1), jnp.concatenate([vh, vh], axis=1)
        heads.append(tuple(jnp.where(sel, t, 0.0).astype(BF16) for t in (kk, vv) for sel in (lane < SW_HD, lane >= SW_HD)))
    qi = lax.broadcasted_iota(jnp.int32, (B, 2 * B), 0) + B
    ki = lax.broadcasted_iota(jnp.int32, (B, 2 * B), 1)
    off = qi - ki
    ok = (off >= 0) & (off < SW_BLOCK) & ((ki >= B) | (n > 0))
    return qr, heads, jnp.where(ok, 0.0, -1e30), lane


SWA_GROUPS = 4
SWA_GROUPS_BWD = 2


def _swa_probs(items, qs, heads, bias, sk_ref, G2):
    ss = [_bdot(qs[j], heads[j // G2][half], "nt") + bias for j, half in items]
    sks = [sk_ref[0:1, 2 * j + half:2 * j + half + 1] for j, half in items]
    ms = [jnp.maximum(jnp.max(s, axis=-1, keepdims=True), sk) for s, sk in zip(ss, sks)]
    ps = [jnp.exp(s - m) for s, m in zip(ss, ms)]
    es = [jnp.exp(sk - m) for sk, m in zip(sks, ms)]
    inv = [1.0 / (jnp.sum(p, axis=-1, keepdims=True) + e) for p, e in zip(ps, es)]
    return [p * i for p, i in zip(ps, inv)], [e * i for e, i in zip(es, inv)]


def _swa_specs(W, cb_q, cb_k):
    B = SW_BLOCK
    assert (W // LANES) % SWA_GROUPS == 0 and (W // LANES) % SWA_GROUPS_BWD == 0
    cur = lambda w, cb: pl.BlockSpec((B, w), lambda n: (n, cb))
    prv = lambda w, cb: pl.BlockSpec((B, w), lambda n: (jnp.maximum(n - 1, 0), cb))
    specs = [cur(W, cb_q), cur(LANES, cb_k), prv(LANES, cb_k), cur(LANES, cb_k + 1), prv(LANES, cb_k + 1)]
    return specs + [cur(LANES, 0)] * 3 + [prv(LANES, 0)] * 3 + [_full((1, LANES))]


def _swa_fwd(proj, tabs, sinks_row, W, cb_q, cb_k):
    S = proj.shape[0]
    G2 = SW_Q_HEADS // SW_KV_HEADS // 2

    def body(q_ref, kc_ref, kp_ref, vc_ref, vp_ref, c0, c1, c2, p0, p1, p2, sk_ref, o_ref):
        n = pl.program_id(0)
        qr, heads, bias, _ = _swa_setup(n, q_ref, kc_ref, kp_ref, vc_ref, vp_ref, (c0, c1, c2), (p0, p1, p2))
        qs = [qr[:, j * LANES:(j + 1) * LANES].astype(BF16) for j in range(W // LANES)]
        for j0 in range(0, W // LANES, SWA_GROUPS):
            items = [(j, half) for j in range(j0, j0 + SWA_GROUPS) for half in range(2)]
            probs, _ = _swa_probs(items, qs, heads, bias, sk_ref, G2)
            pv = [_bdot(p, heads[j // G2][2 + half]) for p, (j, half) in zip(probs, items)]
            for g in range(SWA_GROUPS):
                o_ref[:, (j0 + g) * LANES:(j0 + g + 1) * LANES] = (pv[2 * g] + pv[2 * g + 1]).astype(BF16)

    t = tuple(tabs)
    return _pcall(body, name="swa_fwd", grid=(S // SW_BLOCK,), in_specs=_swa_specs(W, cb_q, cb_k),
                  out_specs=pl.BlockSpec((SW_BLOCK, W), lambda n: (n, 0)), out_shape=_sds((S, W), BF16),
                  compiler_params=_cp("arbitrary"))(proj, proj, proj, proj, proj, *t, *t, sinks_row)


def _swa_bwd(proj, tabs, sinks_row, do, W, cb_q, cb_k):
    S = proj.shape[0]
    B = SW_BLOCK
    G2 = SW_Q_HEADS // SW_KV_HEADS // 2
    SKR = -(-SW_Q_HEADS // SUBLANES) * SUBLANES

    def body(q_ref, kc_ref, kp_ref, vc_ref, vp_ref, c0, c1, c2, p0, p1, p2, sk_ref, do_ref,
             dq_ref, dkc_ref, dkp_ref, dvc_ref, dvp_ref, dsk_ref):
        n = pl.program_id(0)
        qr, heads, bias, lane = _swa_setup(n, q_ref, kc_ref, kp_ref, vc_ref, vp_ref, (c0, c1, c2), (p0, p1, p2))

        @pl.when(n == 0)
        def _():
            dsk_ref[...] = jnp.zeros((SKR, LANES), F32)

        acc_k = [jnp.zeros((2 * B, LANES), F32) for _ in range(SW_KV_HEADS)]
        acc_v = [jnp.zeros((2 * B, LANES), F32) for _ in range(SW_KV_HEADS)]
        qs = [qr[:, j * LANES:(j + 1) * LANES].astype(BF16) for j in range(W // LANES)]
        dos = [do_ref[:, j * LANES:(j + 1) * LANES].astype(BF16) for j in range(W // LANES)]
        dqs = []
        for j0 in range(0, W // LANES, SWA_GROUPS_BWD):
            items = [(j, half) for j in range(j0, j0 + SWA_GROUPS_BWD) for half in range(2)]
            probs, psinks = _swa_probs(items, qs, heads, bias, sk_ref, G2)
            dps = [_bdot(dos[j], heads[j // G2][2 + half], "nt") for j, half in items]
            deltas = [jnp.sum(p * dp, axis=-1, keepdims=True) for p, dp in zip(probs, dps)]
            dss = [(p * (dp - dl)).astype(BF16) for p, dp, dl in zip(probs, dps, deltas)]
            pbs = [p.astype(BF16) for p in probs]
            dqp = [_bdot(ds, heads[j // G2][half]) for ds, (j, half) in zip(dss, items)]
            dkk = [_bdot(ds, qs[j], "tn") for ds, (j, half) in zip(dss, items)]
            dvv = [_bdot(p, dos[j], "tn") for p, (j, half) in zip(pbs, items)]
            for i, (j, half) in enumerate(items):
                hk, h = j // G2, 2 * j + half
                sel = (lane < SW_HD) if half == 0 else (lane >= SW_HD)
                acc_k[hk] = acc_k[hk] + jnp.where(sel, dkk[i], 0.0)
                acc_v[hk] = acc_v[hk] + jnp.where(sel, dvv[i], 0.0)
                dsk_ref[h:h + 1, :] += jnp.broadcast_to(-jnp.sum(psinks[i] * deltas[i], axis=0, keepdims=True), (1, LANES))
            dqs += [dqp[2 * g] + dqp[2 * g + 1] for g in range(SWA_GROUPS_BWD)]
        dq = jnp.concatenate(dqs, axis=1) * (SW_HD ** -0.5)
        dq_ref[...] = _unrope(dq, c0[...], c1[...], c2[...]).astype(BF16)
        fold = lambda a: a[:, :SW_HD] + a[:, SW_HD:]
        dkw = jnp.concatenate([fold(a) for a in acc_k], axis=1)
        dvw = jnp.concatenate([fold(a) for a in acc_v], axis=1)
        dkp_ref[...], dkc_ref[...] = dkw[:B], dkw[B:]
        dvp_ref[...], dvc_ref[...] = dvw[:B], dvw[B:]

    t = tuple(tabs)
    blk = lambda w: pl.BlockSpec((B, w), lambda n: (n, 0))
    o = _sds((S, LANES), F32)
    return _pcall(body, name="swa_bwd", grid=(S // B,), in_specs=_swa_specs(W, cb_q, cb_k) + [blk(W)],
                  out_specs=(blk(W), blk(LANES), blk(LANES), blk(LANES), blk(LANES), _full((SKR, LANES))),
                  out_shape=(_sds((S, W), BF16), o, o, o, o, _sds((SKR, LANES), F32)),
                  compiler_params=_cp("arbitrary"))(proj, proj, proj, proj, proj, *t, *t, sinks_row, do)


def _swa_kv_combine(dkc, dkp, dvc, dvp, tabs):
    S = dkc.shape[0]
    B = SW_BLOCK
    nb = S // B

    def fn(n, kc_ref, kp_ref, vc_ref, vp_ref, c0, c1, c2, o_ref):
        more = n < nb - 1
        dk = kc_ref[...] + jnp.where(more, kp_ref[...], 0.0)
        dv = vc_ref[...] + jnp.where(more, vp_ref[...], 0.0)
        o_ref[...] = jnp.concatenate([_unrope(dk, c0[...], c1[...], c2[...]), dv], axis=1).astype(BF16)

    cur = _rb(B, LANES)
    nxt = pl.BlockSpec((B, LANES), lambda n: (jnp.minimum(n + 1, nb - 1), 0))
    return _rows("swa_kv_combine", fn, S, B, (dkc, dkp, dvc, dvp, *tabs), [cur, nxt, cur, nxt, cur, cur, cur],
                 _sds((S, 2 * LANES), BF16), _rb(B, 2 * LANES))


ANY = pl.BlockSpec(memory_space=pl.ANY)


def _place():
    x, y, c = lax.axis_index("x"), lax.axis_index("y"), lax.axis_index("c")
    return x, y, c, [(1 - x, y), (x, 1 - y), (1 - x, 1 - y)]


def _comm_call(name, body, out_shapes, n_sems, n_local, *ins):
    return _pcall(body, name=name, out_shape=tuple(out_shapes), in_specs=[ANY] * len(ins), out_specs=tuple(ANY for _ in out_shapes),
                  scratch_shapes=[pltpu.SemaphoreType.DMA((n_sems,)), pltpu.SemaphoreType.DMA((n_sems,)),
                                  pltpu.SemaphoreType.DMA((n_local,))])(*ins)


def _remote(src, dst, send, recv, k, to):
    return pltpu.make_async_remote_copy(src_ref=src, dst_ref=dst, send_sem=send.at[k], recv_sem=recv.at[k], device_id=to,
                                        device_id_type=MESH)


def _gather_chips(name, arrs):
    n = len(arrs)
    Lh = arrs[0].shape[0] // 2

    def body(*refs):
        w, o, (send, recv, _) = refs[:n], refs[n:2 * n], refs[2 * n:]
        x, y, c, chips = _place()
        me, sib = 2 * x + y, (x, y, 1 - c)
        own, other = pl.ds(c * Lh, Lh), pl.ds((1 - c) * Lh, Lh)
        idx = [2 * cx + cy for cx, cy in chips]
        first = [[_remote(w[a].at[own], o[a].at[me, own], send, recv, 6 * a + j, (*chips[j], c)) for j in range(3)] for a in range(n)]
        passed = [[_remote(o[a].at[idx[j], own], o[a].at[idx[j], own], send, recv, 6 * a + 3 + j, sib) for j in range(3)] for a in range(n)]
        for cp in [cp for row in first for cp in row]:
            cp.start()
        for j in range(3):
            for a in range(n):
                _remote(w[a].at[own], o[a].at[idx[j], own], send, recv, 6 * a + j, (*chips[j], c)).wait_recv()
                passed[a][j].start()
        for j in range(3):
            for a in range(n):
                _remote(w[a].at[other], o[a].at[idx[j], other], send, recv, 6 * a + 3 + j, sib).wait_recv()
        for cp in [cp for row in first + passed for cp in row]:
            cp.wait_send()

    return _comm_call(name, body, [_sds((4,) + a.shape, a.dtype) for a in arrs], 6 * n, 1, *arrs)


def _pair_swap(name, arrs, whole=False):
    n = len(arrs)
    Lh = arrs[0].shape[0] if whole else arrs[0].shape[0] // 2

    def body(*refs):
        g, o, (send, recv, _) = refs[:n], refs[n:2 * n], refs[2 * n:]
        x, y, c, _ = _place()
        cps = [_remote(g[a] if whole else g[a].at[pl.ds((1 - c) * Lh, Lh)], o[a], send, recv, a, (x, y, 1 - c)) for a in range(n)]
        for cp in cps:
            cp.start()
        for cp in cps:
            cp.wait()

    return _comm_call(name, body, [_sds((Lh,) + a.shape[1:], a.dtype) for a in arrs], n, 1, *arrs)


def _chip_slice(ref, axis, s):
    if axis is None:
        return ref.at[s]
    q = ref.shape[axis] // 4
    start = s * q if isinstance(s, int) else pl.multiple_of(s * q, q)
    return ref.at[tuple([slice(None)] * axis + [pl.ds(start, q)])]


def _scatter_chips(name, items):
    n = len(items)
    part = lambda a, ax: a.shape[1:] if ax is None else tuple(d // 4 if i == ax else d for i, d in enumerate(a.shape))

    def body(*refs):
        p, o, (send, recv, _) = refs[:n], refs[n:2 * n], refs[2 * n:]
        x, y, c, chips = _place()
        me = 2 * x + y
        idx = [2 * cx + cy for cx, cy in chips]
        cps = [_remote(_chip_slice(p[a], items[a][1], idx[j]), o[a].at[me], send, recv, 3 * a + j, (*chips[j], c))
               for a in range(n) for j in range(3)]
        for cp in cps:
            cp.start()
        for a in range(n):
            for j in range(3):
                _remote(_chip_slice(p[a], items[a][1], me), o[a].at[idx[j]], send, recv, 3 * a + j, (*chips[j], c)).wait_recv()
        for cp in cps:
            cp.wait_send()

    return _comm_call(name, body, [_sds((4,) + part(a, ax), a.dtype) for a, ax in items], 3 * n, 1, *[a for a, _ in items])


def _own_part(a, axis, me):
    if axis is None:
        return lax.dynamic_index_in_dim(a, me, 0, keepdims=False)
    q = a.shape[axis] // 4
    return lax.dynamic_slice_in_dim(a, me * q, q, axis)


def _gather_all(name, b):
    R, C = b.shape
    flips = [(dx, dy, dc) for dx in (0, 1) for dy in (0, 1) for dc in (0, 1)][1:]

    def body(b_ref, o_ref, send, recv, lsem):
        x, y, c, _ = _place()
        me = 4 * x + 2 * y + c
        peers = [(x ^ dx, y ^ dy, c ^ dc) for dx, dy, dc in flips]
        mine = pltpu.make_async_copy(b_ref, o_ref.at[me], lsem.at[0])
        mine.start()
        cps = [_remote(b_ref, o_ref.at[me], send, recv, k, peer) for k, peer in enumerate(peers)]
        for cp in cps:
            cp.start()
        for k, (px, py, pc) in enumerate(peers):
            _remote(b_ref, o_ref.at[4 * px + 2 * py + pc], send, recv, k, (px, py, pc)).wait_recv()
        for cp in cps:
            cp.wait_send()
        mine.wait()

    return _comm_call(name, body, [_sds((8, R, C), b.dtype)], 7, 1, b)[0]


def _block_rows(rows, width):
    return _tile(rows, max(SUBLANES, (1 << 19) // width), SUBLANES)


def _add_half(name, g, got):
    L, A, B = g.shape
    Lh = L // 2
    tq = _block_rows(A, B)

    def body(c_ref, g_ref, r_ref, o_ref):
        o_ref[...] = (g_ref[...] + r_ref[...]).astype(BF16)

    spec = pltpu.PrefetchScalarGridSpec(
        num_scalar_prefetch=1, grid=(Lh, A // tq),
        in_specs=[pl.BlockSpec((1, tq, B), lambda l, i, c_ref: (c_ref[0] * Lh + l, i, 0)),
                  pl.BlockSpec((1, tq, B), lambda l, i, c_ref: (l, i, 0))],
        out_specs=pl.BlockSpec((1, tq, B), lambda l, i, c_ref: (l, i, 0)))
    return _pcall(body, name=name, grid_spec=spec, out_shape=_sds((Lh, A, B), BF16),
                  compiler_params=_cp("arbitrary", "arbitrary"))(lax.axis_index("c").reshape(1).astype(jnp.int32), g, got)


def _sum_slots(name, a):
    n, R, C = a.shape
    tq = _block_rows(R, n * C)

    def fn(i, a_ref, o_ref):
        t = a_ref[0].astype(F32)
        for s in range(1, n):
            t = t + a_ref[s].astype(F32)
        o_ref[...] = t

    return _rows(name, fn, R, tq, (a,), [pl.BlockSpec((n, tq, C), lambda i: (0, i, 0))], _sds((R, C), F32), _rb(tq, C))


def _adam_update(w, g, m, v):
    mn = ADAM_B1 * m + (1.0 - ADAM_B1) * g
    vn = ADAM_B2 * v + (1.0 - ADAM_B2) * (g * g)
    m_hat = mn / (1.0 - ADAM_B1 ** ADAM_STEP)
    v_hat = vn / (1.0 - ADAM_B2 ** ADAM_STEP)
    return -ADAM_LR * (m_hat / (jnp.sqrt(v_hat) + ADAM_EPS) + ADAM_WD * w), mn, vn


def _adamw(name, w, g, m, v):
    R, C = w.shape
    tq = _tile(R, 256, SUBLANES)

    def fn(i, w_ref, g_ref, m_ref, v_ref, d_ref, mo_ref, vo_ref):
        d_ref[...], mo_ref[...], vo_ref[...] = _adam_update(w_ref[...], g_ref[...], m_ref[...], v_ref[...])

    r, o = _rb(tq, C), _sds((R, C), F32)
    return _rows(name, fn, R, tq, (w, g, m, v), [r, r, r, r], (o, o, o), (r, r, r))


def _adamw_halves(name, w, mine, theirs, m, v):
    L, A, B = w.shape
    Lh = L // 2
    tq = _tile(A, 256, SUBLANES)

    def body(c_ref, w_ref, a_ref, b_ref, m_ref, v_ref, g_ref, d_ref, mo_ref, vo_ref):
        is_mine = pl.program_id(0) // Lh == c_ref[0]
        g = jnp.where(is_mine, a_ref[...], b_ref[...])
        g_ref[...] = g
        d_ref[...], mo_ref[...], vo_ref[...] = _adam_update(w_ref[...], g, m_ref[...], v_ref[...])

    full = pl.BlockSpec((1, tq, B), lambda l, i, c_ref: (l, i, 0))
    half = pl.BlockSpec((1, tq, B), lambda l, i, c_ref: (l % Lh, i, 0))
    o = _sds((L, A, B), F32)
    spec = pltpu.PrefetchScalarGridSpec(num_scalar_prefetch=1, grid=(L, A // tq), in_specs=[full, half, half, full, full],
                                        out_specs=(full, full, full, full))
    return _pcall(body, name=name, grid_spec=spec, out_shape=(o, o, o, o),
                  compiler_params=_cp("arbitrary", "arbitrary"))(lax.axis_index("c").reshape(1).astype(jnp.int32), w, mine, theirs, m, v)


def _pack(arrs, width, lead=()):
    nl = len(lead)
    flat = jnp.concatenate([a.reshape(lead + (-1,)) for a in arrs], axis=nl)
    n = flat.shape[-1]
    unit = PACK_ROWS * width
    tot = -(-n // unit) * unit
    flat = jnp.pad(flat, [(0, 0)] * nl + [(0, tot - n)])
    return flat.reshape(lead + (tot // width, width))


def _unpack(buf, shapes, lead=()):
    flat = buf.reshape(lead + (-1,))
    out, off = [], 0
    for s in shapes:
        n = int(np.prod(s))
        out.append(flat[..., off:off + n].reshape(lead + tuple(s)))
        off += n
    return out


def _in_groups(W, H):
    o_sq = 4 * W + 2 * H
    o_k = o_sq + W
    o_g = o_k + 2 * KV_W
    return [(0, 4 * W), (o_sq, o_k), (o_g, o_g + 2 * W), (o_k, o_g), (4 * W, o_sq)]


def _relayout_in(shards, W, H):
    c4 = sum(hi - lo for lo, hi in _in_groups(W, H)) // 4
    parts = []
    for lo, hi in _in_groups(W, H):
        for s in range(4):
            a, b = max(lo, s * c4), min(hi, (s + 1) * c4)
            if a < b:
                parts.append(shards[s][:, a - s * c4:b - s * c4])
    parts.append(jnp.zeros((shards.shape[1], BA_W - 2 * H), shards.dtype))
    return jnp.concatenate(parts, axis=1)


def _shard_in(d, W, H):
    groups = _in_groups(W, H)
    starts = [sum(hi - lo for lo, hi in groups[:i]) for i in range(len(groups))]
    stored = sorted(zip(groups, starts))
    c4 = sum(hi - lo for lo, hi in groups) // 4
    out = []
    for s in range(4):
        parts = []
        for (lo, hi), at in stored:
            a, b = max(lo, s * c4), min(hi, (s + 1) * c4)
            if a < b:
                parts.append(d[:, :, at + a - lo:at + b - lo])
        out.append(jnp.concatenate(parts, axis=2))
    return jnp.stack(out)


def _lane_row(vals, at):
    return jnp.pad(vals, (at, LANES - at - vals.shape[0]))[None]


def _layer_fwd(x, lw, tabs, W, H):
    D = x.shape[1]
    cbk = 7 * W // LANES
    h = _pre_norm(x, lw["g1"])
    proj = _mm("mm_in", h, lw["win"], "nn", BF16, tn=768)
    ba = _mm("mm_ba", h, lw["win"][:, 7 * W + 2 * KV_W:], "nn", F32)
    qkv = _dn_prep(proj, lw["conv"], W)
    beta_b, g_b = _dn_gates(ba, lw["alog"], lw["dt"], H)
    o, st = _delta_fwd(qkv, beta_b, g_b, H, DELTA_CB, DELTA_HB)
    oa = _dn_out(o, proj, lw["ng"], W, 3)
    ob = _swa_fwd(proj, tabs, lw["sinks"], W, 4, cbk)
    ya = _mm("mm_up_dn", oa, lw["wup_dn"], "nn", BF16)
    yb = _mm("mm_up_sw", ob, lw["wup_sw"], "nn", BF16)
    mixin = _mix(proj, ya, yb, D, 5)
    mix = _mm("mm_o", mixin, lw["wo"], "nn", F32)
    x1, h2 = _post_mix(x, mix, lw["g2"], lw["g3"])
    f1, act = _mm("mm_ff1", h2, lw["wff1"], "nn", out_dtypes=(BF16, BF16), epi=lambda acc: (acc, jnp.square(jnp.maximum(acc, 0.0))))
    ff = _mm("mm_ff2", act, lw["wff2"], "nn", F32)
    x2 = _post_mlp(x1, ff, lw["g4"])
    saved = dict(x=x, h=h, proj=proj, ba=ba, qkv=qkv, beta_b=beta_b, g_b=g_b, o=o, st=st, oa=oa, ob=ob, ya=ya, yb=yb,
                 mixin=mixin, mix=mix, x1=x1, h2=h2, f1=f1, act=act, ff=ff)
    return x2, saved


def _layer_bwd(dx2, lw, sv, tabs, W, H, l, big):
    D = dx2.shape[1]
    cbk = 7 * W // LANES
    big = dict(big)
    dff, dg4 = _post_mlp_bwd(sv["ff"], lw["g4"], dx2)
    df1 = _mm("mm_ff2_dx", dff, lw["wff2"], "nt", BF16, extras=(sv["f1"],),
              epi=lambda acc, f1: (acc * 2.0 * jnp.maximum(f1.astype(F32), 0.0),))
    big["w_ff2"] = _mm("mm_ff2_dw", sv["act"], dff, "tn", slab=(big["w_ff2"], l))
    dh2 = _mm("mm_ff1_dx", df1, lw["wff1"], "nt", F32)
    big["w_ff1"] = _mm("mm_ff1_dw", sv["h2"], df1, "tn", slab=(big["w_ff1"], l))
    dx1, dmix, dg3, dg2 = _mid_bwd(sv["x1"], lw["g3"], dh2, dx2, sv["mix"], lw["g2"])
    dmixin = _mm("mm_o_dx", dmix, lw["wo"], "nt", BF16)
    big["w_o"] = _mm("mm_o_dw", sv["mixin"], dmix, "tn", slab=(big["w_o"], l))
    dya, dyb, dga, dgb = _mix_bwd(sv["proj"], sv["ya"], sv["yb"], dmixin, D, 5)
    doa = _mm("mm_up_dn_dx", dya, lw["wup_dn"], "nt", BF16)
    big["w_up_dn"] = _mm("mm_up_dn_dw", sv["oa"], dya, "tn", slab=(big["w_up_dn"], l))
    dob = _mm("mm_up_sw_dx", dyb, lw["wup_sw"], "nt", BF16)
    big["w_up_sw"] = _mm("mm_up_sw_dw", sv["ob"], dyb, "tn", slab=(big["w_up_sw"], l))
    do, dz, dng = _dn_out_bwd(sv["o"], sv["proj"], lw["ng"], doa, W, 3)
    dqkvn, dbeta_b, dg_b = _delta_bwd(sv["qkv"], sv["beta_b"], sv["g_b"], sv["st"], do, H, DELTA_CB, DELTA_HB)
    dba, dalog, ddt = _dn_gates_bwd(sv["ba"], lw["alog"], lw["dt"], dbeta_b, dg_b, H)
    dc, dconv = _dn_prep_bwd_a(sv["proj"], lw["conv"], dqkvn, W)
    dqkv = _dn_prep_bwd_b(dc, lw["conv"], W)
    dq_sw, dkc, dkp, dvc, dvp, dsk = _swa_bwd(sv["proj"], tabs, lw["sinks"], dob, W, 4, cbk)
    dkv = _swa_kv_combine(dkc, dkp, dvc, dvp, tabs)
    dproj = jnp.concatenate([dqkv, dz, dq_sw, dga, dgb, dkv, dba], axis=1)
    dh = _mm("mm_in_dx", dproj, lw["win"], "nt", F32, tk=768)
    big["w_in"] = _mm("mm_in_dw", sv["h"], dproj, "tn", tn=768, slab=(big["w_in"], l))
    dx, dg1 = _pre_norm_bwd(sv["x"], lw["g1"], dh, dx1)
    grads = dict(pre_mix_g=dg1[0], dn_conv_w=dconv, dn_a_log=dalog[0, H:2 * H], dn_dt_bias=ddt[0, H:2 * H], dn_norm_g=dng[0],
                 sw_sinks=dsk[:SW_Q_HEADS, 0], post_mix_g=dg2[0], pre_mlp_g=dg3[0], post_mlp_g=dg4[0])
    return dx, grads, big


_WEIGHTS = ["pre_mix_g", "w_in", "dn_conv_w", "dn_a_log", "dn_dt_bias", "dn_norm_g", "sw_sinks", "w_up_dn", "w_up_sw", "w_o",
            "post_mix_g", "pre_mlp_g", "w_ff1", "w_ff2", "post_mlp_g"]
_BIG = {"w_in": 2, "w_up_dn": 1, "w_up_sw": 1, "w_o": 1, "w_ff1": 2, "w_ff2": 1}
_SMALL = [n for n in _WEIGHTS if n not in _BIG]


def _step(P):
    x, target = P["x"][0], P["loss_target"][0]
    S, D = x.shape
    L = P["pre_mix_g"].shape[0]
    H, W = DN_HEADS, DN_HEADS * DN_DK
    assert W == D == SW_Q_HEADS * SW_HD and KV_W == LANES
    me = 2 * lax.axis_index("x") + lax.axis_index("y")

    assert L % 2 == 0
    local = [P[n].astype(BF16) for n in _BIG] + [P["dn_conv_w"]]
    gathered = _gather_chips("weights_gather", local)
    full = {n: lax.dynamic_update_slice_in_dim(g, w[None], me, 0) for n, g, w in zip(list(_BIG) + ["dn_conv_w"], gathered, local)}
    rows = lambda n, l: full[n][:, l].reshape(-1, full[n].shape[-1])
    cols = lambda n, l: jnp.concatenate([full[n][s, l] for s in range(4)], axis=-1)

    tabs = _rope_tables(P["positions"].reshape(S, 1))
    lws = []
    for l in range(L):
        lws.append(dict(
            g1=P["pre_mix_g"][l][None], win=_relayout_in(full["w_in"][:, l], W, H), conv=cols("dn_conv_w", l),
            alog=_lane_row(P["dn_a_log"][l], H), dt=_lane_row(P["dn_dt_bias"][l], H), ng=P["dn_norm_g"][l][None],
            sinks=_lane_row(P["sw_sinks"][l], 0), wup_dn=rows("w_up_dn", l), wup_sw=rows("w_up_sw", l), wo=rows("w_o", l),
            g2=P["post_mix_g"][l][None], g3=P["pre_mlp_g"][l][None], wff1=cols("w_ff1", l), wff2=rows("w_ff2", l),
            g4=P["post_mlp_g"][l][None]))

    saved = []
    for l in range(L):
        x, sv = _layer_fwd(x, lws[l], tabs, W, H)
        saved.append(sv)
    loss_row, dx = _loss_head(x, target)
    layer_grads = [None] * L
    F = 4 * P["w_ff1"].shape[2]
    per_layer = dict(w_in=(D, 7 * W + 2 * KV_W + BA_W), w_up_dn=(W, D), w_up_sw=(W, D), w_o=(D, D), w_ff1=(D, F), w_ff2=(F, D))
    grads = {n: lax.empty((L,) + per_layer[n], F32) for n in _BIG}
    for l in reversed(range(L)):
        dx, layer_grads[l], grads = _layer_bwd(dx, lws[l], saved[l], tabs, W, H, l, grads)
    grads.update({n: jnp.stack([layer_grads[l][n] for l in range(L)]) for n in _SMALL})

    got = _pair_swap("grad_pair_swap", [grads[n] for n in _BIG])
    part = {n: _add_half("grad_pair_add_" + n, grads[n], r) for n, r in zip(_BIG, got)}
    items = [(_shard_in(part[n], W, H), None) if n == "w_in" else (part[n], ax) for n, ax in _BIG.items()]
    slots = _scatter_chips("grad_chip_scatter", items)
    halves = []
    for n, s, (a, ax) in zip(_BIG, slots, items):
        s = lax.dynamic_update_slice_in_dim(s, _own_part(a, ax, me)[None], me, 0)
        flat = s.reshape(4, -1, s.shape[-1])
        halves.append(_sum_slots("grad_chip_sum_" + n, flat).reshape(s.shape[1:]))
    theirs = _pair_swap("grad_pair_share", halves, whole=True)
    gsum, delta, new_m, new_v = {}, {}, {}, {}
    for n, mine, their in zip(_BIG, halves, theirs):
        gsum[n], delta[n], new_m[n], new_v[n] = _adamw_halves("adamw_" + n, P[n], mine, their, P["m_" + n], P["v_" + n])
    small_shapes = [(1,)] + [grads[n].shape for n in _SMALL]
    tot = _sum_slots("small_sum", _gather_all("small_gather", _pack([loss_row[0, :1]] + [grads[n] for n in _SMALL], LANES)))
    small = _unpack(tot, small_shapes)
    loss = small[0][0]
    gsum.update(zip(_SMALL, small[1:]))
    cw = P["dn_conv_w"].shape[2]
    gsum["dn_conv_w"] = lax.dynamic_slice_in_dim(gsum["dn_conv_w"], me * cw, cw, axis=2)

    sm_shapes = [P[n].shape for n in _SMALL]
    outs = _adamw("adamw_small", *(_pack([src[pre + n] for n in _SMALL], LANES)
                                   for src, pre in ((P, ""), (gsum, ""), (P, "m_"), (P, "v_"))))
    for d, o in zip((delta, new_m, new_v), outs):
        d.update(zip(_SMALL, _unpack(o, sm_shapes)))

    return (loss, dx[None], *[gsum[n] for n in _WEIGHTS], *[delta[n] for n in _WEIGHTS],
            *[new_m[n] for n in _WEIGHTS], *[new_v[n] for n in _WEIGHTS])


def kernel(x, positions, pre_mix_g, w_in, dn_conv_w, dn_a_log, dn_dt_bias, dn_norm_g, sw_sinks, w_up_dn, w_up_sw, w_o, post_mix_g, pre_mlp_g, w_ff1, w_ff2, post_mlp_g, loss_target, m_pre_mix_g, m_w_in, m_dn_conv_w, m_dn_a_log, m_dn_dt_bias, m_dn_norm_g, m_sw_sinks, m_w_up_dn, m_w_up_sw, m_w_o, m_post_mix_g, m_pre_mlp_g, m_w_ff1, m_w_ff2, m_post_mlp_g, v_pre_mix_g, v_w_in, v_dn_conv_w, v_dn_a_log, v_dn_dt_bias, v_dn_norm_g, v_sw_sinks, v_w_up_dn, v_w_up_sw, v_w_o, v_post_mix_g, v_pre_mlp_g, v_w_ff1, v_w_ff2, v_post_mlp_g):
    vals = (x, positions, pre_mix_g, w_in, dn_conv_w, dn_a_log, dn_dt_bias, dn_norm_g, sw_sinks, w_up_dn, w_up_sw, w_o, post_mix_g, pre_mlp_g, w_ff1, w_ff2, post_mlp_g, loss_target, m_pre_mix_g, m_w_in, m_dn_conv_w, m_dn_a_log, m_dn_dt_bias, m_dn_norm_g, m_sw_sinks, m_w_up_dn, m_w_up_sw, m_w_o, m_post_mix_g, m_pre_mlp_g, m_w_ff1, m_w_ff2, m_post_mlp_g, v_pre_mix_g, v_w_in, v_dn_conv_w, v_dn_a_log, v_dn_dt_bias, v_dn_norm_g, v_sw_sinks, v_w_up_dn, v_w_up_sw, v_w_o, v_post_mix_g, v_pre_mlp_g, v_w_ff1, v_w_ff2, v_post_mlp_g)
    names = ["x", "positions"] + _WEIGHTS + ["loss_target"] + ["m_" + n for n in _WEIGHTS] + ["v_" + n for n in _WEIGHTS]
    return _step(dict(zip(names, vals)))
```

```python
import functools

import numpy as np
import jax
import jax.numpy as jnp
from jax import lax
from jax.experimental import pallas as pl
from jax.experimental.pallas import tpu as pltpu

F32, BF16 = jnp.float32, jnp.bfloat16
MESH = pl.DeviceIdType.MESH

DN_HEADS = 8
DN_DK = 128
DN_CONV = 4
DN_CHUNK = 64
SW_Q_HEADS = 16
SW_KV_HEADS = 2
SW_HD = 64
SW_BLOCK = 128
ROPE_THETA = 500000.0
ROT_DIM = SW_HD // 4
EPS = 1e-6
ADAM_LR, ADAM_B1, ADAM_B2, ADAM_EPS, ADAM_WD, ADAM_STEP = 0.001, 0.9, 0.999, 1e-08, 0.01, 10

LANES = 128
SUBLANES = 8
VMEM_LIMIT = 48 * 1024 * 1024
KV_W = SW_KV_HEADS * SW_HD
BA_W = 256
PACK_ROWS = 512
DELTA_CB = 4
DELTA_HB = 8


def _pcall(body, **kw):
    return pl.pallas_call(body, **kw)


def _cp(*sem):
    return pltpu.CompilerParams(dimension_semantics=sem, vmem_limit_bytes=VMEM_LIMIT)


def _tile(n, pref, unit=LANES):
    if n <= pref:
        return n
    t = (pref // unit) * unit
    while t > unit and n % t:
        t -= unit
    assert n % t == 0, (n, pref)
    return t


def _sds(shape, dtype):
    return jax.ShapeDtypeStruct(tuple(shape), dtype)


_DIMS = {"nn": ((1,), (0,)), "nt": ((1,), (1,)), "tn": ((0,), (0,))}


def _mm(name, a, b, mode, out_dtype=F32, tm=1024, tn=1024, tk=1024, extras=(), epi=None, out_dtypes=None, slab=None):
    if mode == "nn":
        (M, K), (_, N) = a.shape, b.shape
    elif mode == "nt":
        (M, K), (N, _) = a.shape, b.shape
    else:
        (K, M), (_, N) = a.shape, b.shape
    tm, tn, tk = _tile(M, tm), _tile(N, tn), _tile(K, tk)
    nk = K // tk
    a_spec = {"nn": pl.BlockSpec((tm, tk), lambda i, j, k: (i, k)),
              "nt": pl.BlockSpec((tm, tk), lambda i, j, k: (i, k)),
              "tn": pl.BlockSpec((tk, tm), lambda i, j, k: (k, i))}[mode]
    b_spec = {"nn": pl.BlockSpec((tk, tn), lambda i, j, k: (k, j)),
              "nt": pl.BlockSpec((tn, tk), lambda i, j, k: (j, k)),
              "tn": pl.BlockSpec((tk, tn), lambda i, j, k: (k, j))}[mode]
    dims = (_DIMS[mode], ((), ()))
    out_dtypes = tuple(out_dtypes or (out_dtype,))
    ne, no = len(extras), len(out_dtypes)
    o_spec = pl.BlockSpec((tm, tn), lambda i, j, k: (i, j))

    def body(*refs):
        a_ref, b_ref, ex = refs[0], refs[1], refs[2:2 + ne]
        outs = refs[-no:] if nk == 1 else refs[-1 - no:-1]
        part = lax.dot_general(a_ref[...], b_ref[...], dims, preferred_element_type=F32)

        def finish(acc):
            res = epi(acc, *[e[...] for e in ex]) if epi else (acc,)
            for o, r, dt in zip(outs, res, out_dtypes):
                if slab is None:
                    o[...] = r.astype(dt)
                else:
                    o[0] = r.astype(dt)

        if nk == 1:
            finish(part)
            return
        acc_ref, k = refs[-1], pl.program_id(2)

        @pl.when(k == 0)
        def _():
            acc_ref[...] = part

        @pl.when((k > 0) & (k < nk - 1))
        def _():
            acc_ref[...] += part

        @pl.when(k == nk - 1)
        def _():
            finish(acc_ref[...] + part)

    kw = dict(name=name, grid=(M // tm, N // tn, nk), scratch_shapes=[] if nk == 1 else [pltpu.VMEM((tm, tn), F32)],
              compiler_params=_cp("parallel", "parallel", "arbitrary"))
    if slab is not None:
        buf, l = slab
        return _pcall(body, in_specs=[a_spec, b_spec, ANY], out_specs=pl.BlockSpec((1, tm, tn), lambda i, j, k: (l, i, j)),
                      out_shape=_sds(buf.shape, buf.dtype), input_output_aliases={2: 0}, **kw)(a, b, buf)
    out = _pcall(body, in_specs=[a_spec, b_spec] + [o_spec] * ne, out_specs=tuple(o_spec for _ in out_dtypes),
                 out_shape=tuple(_sds((M, N), dt) for dt in out_dtypes), **kw)(a, b, *extras)
    return out if no > 1 else out[0]


def _rows(name, fn, n_rows, tq, ins, in_specs, out_shapes, out_specs):
    def body(*refs):
        fn(pl.program_id(0), *refs)

    return _pcall(body, name=name, grid=(n_rows // tq,), in_specs=in_specs, out_specs=out_specs,
                  out_shape=out_shapes, compiler_params=_cp("arbitrary"))(*ins)


def _rb(tq, w, cb=0):
    return pl.BlockSpec((tq, w), lambda i: (i, cb))


def _full(shape):
    return pl.BlockSpec(tuple(shape), lambda *_: (0,) * len(shape))


def _rms_fwd(x, g):
    r = lax.rsqrt(jnp.mean(x * x, axis=-1, keepdims=True) + EPS)
    return x * r * g


def _rms_bwd(x, g, dy):
    r = lax.rsqrt(jnp.mean(x * x, axis=-1, keepdims=True) + EPS)
    xh = x * r
    t = dy * g
    dx = r * (t - xh * jnp.mean(t * xh, axis=-1, keepdims=True))
    return dx, jnp.sum(dy * xh, axis=0, keepdims=True)


def _acc(i, ref, val):
    @pl.when(i == 0)
    def _():
        ref[...] = val

    @pl.when(i > 0)
    def _():
        ref[...] += val


def _sigmoid(x):
    return 0.5 * jnp.tanh(0.5 * x) + 0.5


def _pre_norm(x, g):
    S, D = x.shape
    tq = _tile(S, 512, SUBLANES)

    def fn(i, x_ref, g_ref, h_ref):
        h_ref[...] = _rms_fwd(x_ref[...], g_ref[...]).astype(BF16)

    return _rows("pre_norm", fn, S, tq, (x, g), [_rb(tq, D), _full((1, D))], _sds((S, D), BF16), _rb(tq, D))


def _post_mix(x, mix, g2, g3):
    S, D = x.shape
    tq = _tile(S, 512, SUBLANES)

    def fn(i, x_ref, m_ref, g2_ref, g3_ref, x1_ref, h2_ref):
        x1 = x_ref[...] + _rms_fwd(m_ref[...], g2_ref[...])
        x1_ref[...] = x1
        h2_ref[...] = _rms_fwd(x1, g3_ref[...]).astype(BF16)

    return _rows("post_mix", fn, S, tq, (x, mix, g2, g3), [_rb(tq, D), _rb(tq, D), _full((1, D)), _full((1, D))],
                 (_sds((S, D), F32), _sds((S, D), BF16)), (_rb(tq, D), _rb(tq, D)))


def _post_mlp(x1, ff, g4):
    S, D = x1.shape
    tq = _tile(S, 512, SUBLANES)

    def fn(i, x_ref, f_ref, g_ref, o_ref):
        o_ref[...] = x_ref[...] + _rms_fwd(f_ref[...], g_ref[...])

    return _rows("post_mlp", fn, S, tq, (x1, ff, g4), [_rb(tq, D), _rb(tq, D), _full((1, D))], _sds((S, D), F32), _rb(tq, D))


def _loss_head(y, target):
    S, D = y.shape
    tq = _tile(S, 512, SUBLANES)

    def fn(i, y_ref, t_ref, l_ref, d_ref):
        e = y_ref[...] - t_ref[...]
        d_ref[...] = e * (1.0 / D)
        part = jnp.sum(jnp.sum(e * e, axis=1, keepdims=True), axis=0, keepdims=True) * (0.5 / D)
        _acc(i, l_ref, jnp.broadcast_to(part, (1, LANES)))

    return _rows("loss_head", fn, S, tq, (y, target), [_rb(tq, D), _rb(tq, D)],
                 (_sds((1, LANES), F32), _sds((S, D), F32)), (_full((1, LANES)), _rb(tq, D)))


def _post_mlp_bwd(ff, g4, dx2):
    S, D = ff.shape
    tq = _tile(S, 512, SUBLANES)

    def fn(i, f_ref, g_ref, d_ref, o_ref, dg_ref):
        dx, dg = _rms_bwd(f_ref[...], g_ref[...], d_ref[...])
        o_ref[...] = dx.astype(BF16)
        _acc(i, dg_ref, dg)

    return _rows("post_mlp_bwd", fn, S, tq, (ff, g4, dx2), [_rb(tq, D), _full((1, D)), _rb(tq, D)],
                 (_sds((S, D), BF16), _sds((1, D), F32)), (_rb(tq, D), _full((1, D))))


def _mid_bwd(x1, g3, dh2, dx2, mix, g2):
    S, D = x1.shape
    tq = _tile(S, 256, SUBLANES)

    def fn(i, x_ref, g3_ref, dh_ref, dx2_ref, m_ref, g2_ref, dx1_ref, dm_ref, dg3_ref, dg2_ref):
        d, dg3 = _rms_bwd(x_ref[...], g3_ref[...], dh_ref[...])
        dx1 = dx2_ref[...] + d
        dx1_ref[...] = dx1
        dm, dg2 = _rms_bwd(m_ref[...], g2_ref[...], dx1)
        dm_ref[...] = dm.astype(BF16)
        _acc(i, dg3_ref, dg3)
        _acc(i, dg2_ref, dg2)

    r, f = _rb(tq, D), _full((1, D))
    return _rows("mid_bwd", fn, S, tq, (x1, g3, dh2, dx2, mix, g2), [r, f, r, r, r, f],
                 (_sds((S, D), F32), _sds((S, D), BF16), _sds((1, D), F32), _sds((1, D), F32)), (r, r, f, f))


def _pre_norm_bwd(x, g1, dh, dx1):
    S, D = x.shape
    tq = _tile(S, 512, SUBLANES)

    def fn(i, x_ref, g_ref, dh_ref, dx1_ref, dx_ref, dg_ref):
        d, dg = _rms_bwd(x_ref[...], g_ref[...], dh_ref[...])
        dx_ref[...] = dx1_ref[...] + d
        _acc(i, dg_ref, dg)

    r, f = _rb(tq, D), _full((1, D))
    return _rows("pre_norm_bwd", fn, S, tq, (x, g1, dh, dx1), [r, f, r, r], (_sds((S, D), F32), _sds((1, D), F32)), (r, f))


def _mix(proj, ya, yb, D, cb_a):
    S = ya.shape[0]
    tq = _tile(S, 256, SUBLANES)

    def fn(i, ga_ref, gb_ref, ya_ref, yb_ref, o_ref):
        ga, gb, ya, yb = (r[...].astype(F32) for r in (ga_ref, gb_ref, ya_ref, yb_ref))
        o_ref[...] = (_sigmoid(ga) * ya + _sigmoid(gb) * yb).astype(BF16)

    return _rows("mix", fn, S, tq, (proj, proj, ya, yb), [_rb(tq, D, cb_a), _rb(tq, D, cb_a + 1), _rb(tq, D), _rb(tq, D)],
                 _sds((S, D), BF16), _rb(tq, D))


def _mix_bwd(proj, ya, yb, dmixin, D, cb_a):
    S = ya.shape[0]
    tq = _tile(S, 256, SUBLANES)

    def fn(i, ga_ref, gb_ref, ya_ref, yb_ref, d_ref, dya_ref, dyb_ref, dga_ref, dgb_ref):
        ga, gb, ya, yb, d = (r[...].astype(F32) for r in (ga_ref, gb_ref, ya_ref, yb_ref, d_ref))
        sa, sb = _sigmoid(ga), _sigmoid(gb)
        dya_ref[...] = (d * sa).astype(BF16)
        dyb_ref[...] = (d * sb).astype(BF16)
        dga_ref[...] = (d * ya * sa * (1.0 - sa)).astype(BF16)
        dgb_ref[...] = (d * yb * sb * (1.0 - sb)).astype(BF16)

    r = _rb(tq, D)
    o = _sds((S, D), BF16)
    return _rows("mix_bwd", fn, S, tq, (proj, proj, ya, yb, dmixin), [_rb(tq, D, cb_a), _rb(tq, D, cb_a + 1), r, r, r],
                 (o, o, o, o), (r, r, r, r))


HALO = 16


def _shift_down(xe, k, tq):
    return pltpu.roll(xe, k, 0)[HALO:HALO + tq]


def _conv_pre(cur_ref, halo_ref, w_ref, i, tq):
    x = cur_ref[...].astype(F32)
    halo = jnp.where(i > 0, halo_ref[...].astype(F32), 0.0)
    xe = jnp.concatenate([halo, x], axis=0)
    xs = [x] + [_shift_down(xe, k, tq) for k in range(1, DN_CONV)]
    w = w_ref[...]
    c = sum(w[DN_CONV - 1 - k:DN_CONV - k, :] * xs[k] for k in range(DN_CONV))
    return c, xs


def _dn_prep(proj, conv_w, W):
    S = proj.shape[0]
    tq = _tile(S, 256, HALO)
    hb = tq // HALO

    def body(cur_ref, halo_ref, w_ref, o_ref):
        j, i = pl.program_id(0), pl.program_id(1)
        c, _ = _conv_pre(cur_ref, halo_ref, w_ref, i, tq)
        y = c * _sigmoid(c)
        scale = jnp.where(j == 0, DN_DK ** -0.5, 1.0)
        for h in range(W // DN_DK):
            sl = slice(h * DN_DK, (h + 1) * DN_DK)
            yh = y[:, sl]
            rs = lax.rsqrt(jnp.sum(yh * yh, axis=-1, keepdims=True) + EPS)
            o_ref[:, sl] = jnp.where(j == 2, yh, yh * rs * scale)

    return _pcall(body, name="dn_prep", grid=(3, S // tq),
                  in_specs=[pl.BlockSpec((tq, W), lambda j, i: (i, j)),
                            pl.BlockSpec((HALO, W), lambda j, i: (jnp.maximum(i * hb - 1, 0), j)),
                            pl.BlockSpec((DN_CONV, W), lambda j, i: (0, j))],
                  out_specs=pl.BlockSpec((tq, W), lambda j, i: (i, j)), out_shape=_sds((S, 3 * W), F32),
                  compiler_params=_cp("arbitrary", "arbitrary"))(proj, proj, conv_w)


def _dn_prep_bwd_a(proj, conv_w, dqkv, W):
    S = proj.shape[0]
    tq = _tile(S, 256, HALO)
    hb = tq // HALO

    def body(cur_ref, halo_ref, w_ref, d_ref, dc_ref, dw_ref):
        j, i = pl.program_id(0), pl.program_id(1)
        c, xs = _conv_pre(cur_ref, halo_ref, w_ref, i, tq)
        sg = _sigmoid(c)
        y = c * sg
        scale = jnp.where(j == 0, DN_DK ** -0.5, 1.0)
        dout = d_ref[0]
        dys = []
        for h in range(W // DN_DK):
            sl = slice(h * DN_DK, (h + 1) * DN_DK)
            yh, dh = y[:, sl], dout[:, sl]
            rs = lax.rsqrt(jnp.sum(yh * yh, axis=-1, keepdims=True) + EPS)
            yn = yh * rs
            dn = scale * rs * (dh - yn * jnp.sum(dh * yn, axis=-1, keepdims=True))
            dys.append(jnp.where(j == 2, dh, dn))
        dy = jnp.concatenate(dys, axis=1)
        dc = dy * (sg * (1.0 + c * (1.0 - sg)))
        dc_ref[...] = dc
        dw = jnp.concatenate([jnp.sum(dc * xs[DN_CONV - 1 - r], axis=0, keepdims=True) for r in range(DN_CONV)], axis=0)
        _acc(i, dw_ref, dw)

    return _pcall(body, name="dn_prep_bwd_a", grid=(3, S // tq),
                  in_specs=[pl.BlockSpec((tq, W), lambda j, i: (i, j)),
                            pl.BlockSpec((HALO, W), lambda j, i: (jnp.maximum(i * hb - 1, 0), j)),
                            pl.BlockSpec((DN_CONV, W), lambda j, i: (0, j)),
                            pl.BlockSpec((1, tq, W), lambda j, i: (j, i, 0))],
                  out_specs=(pl.BlockSpec((tq, W), lambda j, i: (i, j)), pl.BlockSpec((DN_CONV, W), lambda j, i: (0, j))),
                  out_shape=(_sds((S, 3 * W), F32), _sds((DN_CONV, 3 * W), F32)),
                  compiler_params=_cp("arbitrary", "arbitrary"))(proj, proj, conv_w, dqkv)


def _dn_prep_bwd_b(dc, conv_w, W):
    S = dc.shape[0]
    tq = _tile(S, 256, SUBLANES)
    hb = tq // SUBLANES
    nblk = S // tq

    def body(cur_ref, nxt_ref, w_ref, o_ref):
        i = pl.program_id(1)
        d = cur_ref[...]
        nxt = jnp.where(i < nblk - 1, nxt_ref[...], 0.0)
        de = jnp.concatenate([d, nxt], axis=0)
        w = w_ref[...]
        out = w[DN_CONV - 1:DN_CONV, :] * d
        for k in range(1, DN_CONV):
            out = out + w[DN_CONV - 1 - k:DN_CONV - k, :] * pltpu.roll(de, tq + SUBLANES - k, 0)[0:tq]
        o_ref[...] = out.astype(BF16)

    return _pcall(body, name="dn_prep_bwd_b", grid=(3, nblk),
                  in_specs=[pl.BlockSpec((tq, W), lambda j, i: (i, j)),
                            pl.BlockSpec((SUBLANES, W), lambda j, i: (jnp.minimum((i + 1) * hb, S // SUBLANES - 1), j)),
                            pl.BlockSpec((DN_CONV, W), lambda j, i: (0, j))],
                  out_specs=pl.BlockSpec((tq, W), lambda j, i: (i, j)), out_shape=_sds((S, 3 * W), BF16),
                  compiler_params=_cp("arbitrary", "arbitrary"))(dc, dc, conv_w)


def _gate_terms(ba, al, dt):
    u = ba + dt
    sp = jnp.maximum(u, 0.0) + jnp.log(1.0 + jnp.exp(-jnp.abs(u)))
    return _sigmoid(ba), -jnp.exp(al) * sp, u


def _dn_gates(ba, alog_row, dt_row, H):
    S = ba.shape[0]
    tq = _tile(S, 512, SUBLANES)
    W = H * DN_DK

    def fn(i, ba_ref, al_ref, dt_ref, be_ref, g_ref):
        bet, gg, _ = _gate_terms(ba_ref[...], al_ref[...], dt_ref[...])
        for h in range(H):
            sl = slice(h * DN_DK, (h + 1) * DN_DK)
            be_ref[:, sl] = jnp.broadcast_to(bet[:, h:h + 1], (tq, DN_DK))
            g_ref[:, sl] = jnp.broadcast_to(gg[:, H + h:H + h + 1], (tq, DN_DK))

    return _rows("dn_gates", fn, S, tq, (ba, alog_row, dt_row), [_rb(tq, LANES), _full((1, LANES)), _full((1, LANES))],
                 (_sds((S, W), F32), _sds((S, W), F32)), (_rb(tq, W), _rb(tq, W)))


def _dn_gates_bwd(ba, alog_row, dt_row, dbeta_b, dg_b, H):
    S = ba.shape[0]
    tq = _tile(S, 512, SUBLANES)
    W = H * DN_DK

    def fn(i, ba_ref, al_ref, dt_ref, db_ref, dg_ref, o_ref, dal_ref, ddt_ref):
        bet, gg, u = _gate_terms(ba_ref[...], al_ref[...], dt_ref[...])
        lane = lax.broadcasted_iota(jnp.int32, (tq, LANES), 1)
        d = jnp.zeros((tq, LANES), F32)
        for h in range(H):
            d = jnp.where(lane == h, db_ref[:, h * DN_DK:h * DN_DK + 1], d)
            d = jnp.where(lane == H + h, dg_ref[:, h * DN_DK:h * DN_DK + 1], d)
        is_a = (lane >= H) & (lane < 2 * H)
        da = jnp.where(is_a, d * (-jnp.exp(al_ref[...]) * _sigmoid(u)), 0.0)
        dlog = jnp.where(lane < H, d * bet * (1.0 - bet), da)
        o_ref[...] = jnp.concatenate([dlog, jnp.zeros((tq, BA_W - LANES), F32)], axis=1).astype(BF16)
        _acc(i, dal_ref, jnp.sum(jnp.where(is_a, d * gg, 0.0), axis=0, keepdims=True))
        _acc(i, ddt_ref, jnp.sum(da, axis=0, keepdims=True))

    f = _full((1, LANES))
    return _rows("dn_gates_bwd", fn, S, tq, (ba, alog_row, dt_row, dbeta_b, dg_b),
                 [_rb(tq, LANES), f, f, _rb(tq, W), _rb(tq, W)],
                 (_sds((S, BA_W), BF16), _sds((1, LANES), F32), _sds((1, LANES), F32)), (_rb(tq, BA_W), f, f))


def _dn_out(o, proj, ng, W, cb_z):
    S = o.shape[0]
    tq = _tile(S, 256, SUBLANES)

    def fn(i, o_ref, z_ref, g_ref, y_ref):
        for h in range(W // DN_DK):
            sl = slice(h * DN_DK, (h + 1) * DN_DK)
            z = z_ref[:, sl].astype(F32)
            y_ref[:, sl] = (_rms_fwd(o_ref[:, sl], g_ref[...]) * (z * _sigmoid(z))).astype(BF16)

    return _rows("dn_out", fn, S, tq, (o, proj, ng), [_rb(tq, W), _rb(tq, W, cb_z), _full((1, DN_DK))], _sds((S, W), BF16), _rb(tq, W))


def _dn_out_bwd(o, proj, ng, dy, W, cb_z):
    S = o.shape[0]
    tq = _tile(S, 256, SUBLANES)

    def fn(i, o_ref, z_ref, g_ref, d_ref, do_ref, dz_ref, dg_ref):
        g = g_ref[...]
        dg = jnp.zeros((1, DN_DK), F32)
        for h in range(W // DN_DK):
            sl = slice(h * DN_DK, (h + 1) * DN_DK)
            oh, z, d = o_ref[:, sl], z_ref[:, sl].astype(F32), d_ref[:, sl].astype(F32)
            sg = _sigmoid(z)
            dn = d * (z * sg)
            dz_ref[:, sl] = (d * _rms_fwd(oh, g) * (sg * (1.0 + z * (1.0 - sg)))).astype(BF16)
            dx, dgh = _rms_bwd(oh, g, dn)
            do_ref[:, sl] = dx
            dg = dg + dgh
        _acc(i, dg_ref, dg)

    r = _rb(tq, W)
    return _rows("dn_out_bwd", fn, S, tq, (o, proj, ng, dy), [r, _rb(tq, W, cb_z), _full((1, DN_DK)), r],
                 (_sds((S, W), F32), _sds((S, W), BF16), _sds((1, DN_DK), F32)), (r, r, _full((1, DN_DK))))


def _bdot(a, b, mode="nn"):
    return lax.dot_general(a.astype(BF16), b.astype(BF16), (_DIMS[mode], ((), ())), preferred_element_type=F32)


def _rsum(x):
    return jnp.broadcast_to(jnp.sum(x, axis=-1, keepdims=True), x.shape)


def _dot3(a, b, mode="nn"):
    ah, bh = a.astype(BF16), b.astype(BF16)
    al, bl = (a - ah.astype(F32)).astype(BF16), (b - bh.astype(F32)).astype(BF16)
    d = lambda x, y: lax.dot_general(x, y, (_DIMS[mode], ((), ())), preferred_element_type=F32)
    return d(ah, bh) + (d(al, bh) + d(ah, bl))


def _cumsum_rows(x, reverse=False):
    n = x.shape[0]
    row = lax.broadcasted_iota(jnp.int32, x.shape, 0)
    s = 1
    while s < n:
        if reverse:
            x = x + jnp.where(row < n - s, pltpu.roll(x, n - s, 0), 0.0)
        else:
            x = x + jnp.where(row >= s, pltpu.roll(x, s, 0), 0.0)
        s *= 2
    return x


def _each(f, *lists):
    return [f(*a) for a in zip(*lists)]


def _delta_local(qs, ks, vs, bes, grs):
    C = DN_CHUNK
    ri = lax.broadcasted_iota(jnp.int32, (C, C), 0)
    ci = lax.broadcasted_iota(jnp.int32, (C, C), 1)
    causal, strict = ri >= ci, ri > ci
    gcs = [_cumsum_rows(g) for g in grs]
    decays = [jnp.where(causal, jnp.exp(jnp.where(causal, gc[:, :C] - gc.T[:C, :], 0.0)), 0.0) for gc in gcs]
    egs = [jnp.exp(gc) for gc in gcs]
    eks = [jnp.exp(gc[C - 1:C, :] - gc) for gc in gcs]
    gams = [jnp.exp(gc[C - 1:C, :]) for gc in gcs]
    kbs = _each(lambda k, be: k * be, ks, bes)
    kks = _each(lambda kb, k: _bdot(kb, k, "nt"), kbs, ks)
    nls = _each(lambda kk, dc: jnp.where(strict, -kk * dc, 0.0), kks, decays)
    eye = (ri == ci).astype(F32)
    ts = [eye + nl for nl in nls]
    pws = [_dot3(nl, nl) for nl in nls]
    for s in range(4):
        both = _each(lambda t, pw: _dot3(jnp.concatenate([t, pw], axis=0), pw), ts, pws)
        ts = _each(lambda t, b: t + b[:C], ts, both)
        pws = [b[C:] for b in both]
    ts = _each(lambda t, pw: t + _dot3(t, pw), ts, pws)
    vbs = _each(lambda v, be: v * be, vs, bes)
    kbes = _each(lambda kb, eg: kb * eg, kbs, egs)
    uws = _each(lambda t, vb, kbe: _dot3(t, jnp.concatenate([vb, kbe], axis=1)), ts, vbs, kbes)
    us, ws = [uw[:, :DN_DK] for uw in uws], [uw[:, DN_DK:] for uw in uws]
    qks = _each(lambda q, k: _bdot(q, k, "nt"), qs, ks)
    return dict(decay=decays, eg=egs, ek=eks, gam=gams, kb=kbs, kk=kks, t=ts, vb=vbs, kbe=kbes, u=us, w=ws, qk=qks,
                a=_each(lambda qk, dc: qk * dc, qks, decays), qd=_each(lambda q, eg: q * eg, qs, egs),
                kd=_each(lambda k, ek: k * ek, ks, eks), strict=strict)


def _delta_items(refs, CB, HB):
    C, dk = DN_CHUNK, DN_DK
    return [[r[c * C:(c + 1) * C, h * dk:(h + 1) * dk] for h in range(HB) for c in range(CB)] for r in refs]


def _delta_fwd(qkv, beta_b, g_b, H, CB, HB):
    S = qkv.shape[0]
    C, dk = DN_CHUNK, DN_DK
    N = S // C
    R = CB * C
    G = H // HB

    def body(q_ref, k_ref, v_ref, b_ref, g_ref, o_ref, st_ref, s_ref):
        @pl.when(pl.program_id(1) == 0)
        def _():
            s_ref[...] = jnp.zeros((HB, dk, dk), F32)

        L = _delta_local(*_delta_items((q_ref, k_ref, v_ref, b_ref, g_ref), CB, HB))
        ss = [s_ref[h] for h in range(HB)]
        for c in range(CB):
            it = [h * CB + c for h in range(HB)]
            for h in range(HB):
                st_ref[h, c] = ss[h]
            wq = [_bdot(jnp.concatenate([L["w"][i], L["qd"][i]], axis=0), s) for i, s in zip(it, ss)]
            vns = [L["u"][i] - x[:C] for i, x in zip(it, wq)]
            outs = [x[C:] + _bdot(L["a"][i], vn) for i, x, vn in zip(it, wq, vns)]
            ss = [s * L["gam"][i] + _bdot(L["kd"][i], vn, "tn") for i, s, vn in zip(it, ss, vns)]
            for h in range(HB):
                o_ref[c * C:(c + 1) * C, h * dk:(h + 1) * dk] = outs[h]
        for h in range(HB):
            s_ref[h] = ss[h]

    blk = lambda off: pl.BlockSpec((R, HB * dk), lambda h, n: (n, off + h))
    return _pcall(body, name="delta_fwd", grid=(G, N // CB),
                  in_specs=[blk(0), blk(G), blk(2 * G), blk(0), blk(0)],
                  out_specs=(blk(0), pl.BlockSpec((HB, CB, dk, dk), lambda h, n: (h, n, 0, 0))),
                  out_shape=(_sds((S, H * dk), F32), _sds((H, N, dk, dk), F32)),
                  scratch_shapes=[pltpu.VMEM((HB, dk, dk), F32)],
                  compiler_params=_cp("arbitrary", "arbitrary"))(qkv, qkv, qkv, beta_b, g_b)


def _delta_bwd(qkv, beta_b, g_b, states, do, H, CB, HB):
    S = qkv.shape[0]
    C, dk = DN_CHUNK, DN_DK
    N = S // C
    R = CB * C
    NB = N // CB
    G = H // HB

    def body(q_ref, k_ref, v_ref, b_ref, g_ref, st_ref, do_ref, dqkv_ref, db_ref, dg_ref, ds_ref):
        @pl.when(pl.program_id(1) == 0)
        def _():
            ds_ref[...] = jnp.zeros((HB, dk, dk), F32)

        qs, ks, vs, bes, grs, dos = _delta_items((q_ref, k_ref, v_ref, b_ref, g_ref, do_ref), CB, HB)
        L = _delta_local(qs, ks, vs, bes, grs)
        ts, decays, kbs, egs, eks, gams, qds, kds = (L[n] for n in ("t", "decay", "kb", "eg", "ek", "gam", "qd", "kd"))
        s0s = [st_ref[h, c] for h in range(HB) for c in range(CB)]
        vns = _each(lambda u, w, s0: u - _bdot(w, s0), L["u"], L["w"], s0s)
        pre_dvn = _each(lambda a, d: _bdot(a, d, "tn"), L["a"], dos)
        pre_ds = _each(lambda qd, d: _bdot(qd, d, "tn"), qds, dos)
        das = _each(lambda d, vn: _bdot(d, vn, "nt"), dos, vns)
        ds = [ds_ref[h] for h in range(HB)]
        ds1s, dvns = [None] * (HB * CB), [None] * (HB * CB)
        for c in reversed(range(CB)):
            it = [h * CB + c for h in range(HB)]
            new = [pre_dvn[i] + _bdot(kds[i], d) for i, d in zip(it, ds)]
            for i, d, dv in zip(it, ds, new):
                ds1s[i], dvns[i] = d, dv
            ds = [pre_ds[i] + d * gams[i] - _bdot(L["w"][i], dv, "tn") for i, d, dv in zip(it, ds, new)]
        for h in range(HB):
            ds_ref[h] = ds[h]
        dkds = _each(lambda vn, d1: _bdot(vn, d1, "nt"), vns, ds1s)
        dgams = _each(lambda s0, d1: jnp.sum(jnp.sum(s0 * d1, axis=1, keepdims=True), axis=0, keepdims=True), s0s, ds1s)
        ost = _each(lambda d, dv, s0: _bdot(jnp.concatenate([d, dv], axis=0), s0, "nt"), dos, dvns, s0s)
        dqds, dws = [x[:C] for x in ost], [-x[C:] for x in ost]
        dvw = _each(lambda dv, dw: jnp.concatenate([dv, dw], axis=1), dvns, dws)
        tdvw = _each(lambda t, x: _dot3(t, x, "tn"), ts, dvw)
        dvbs, dkbes = [x[:, :dk] for x in tdvw], [x[:, dk:] for x in tdvw]
        dts = _each(lambda x, vb, kbe: _dot3(x, jnp.concatenate([vb, kbe], axis=1), "nt"), dvw, L["vb"], L["kbe"])
        tmp = _each(lambda dt, t: _dot3(dt, t, "nt"), dts, ts)
        dls = _each(lambda t, x: -_dot3(t, x, "tn"), ts, tmp)
        ms = _each(lambda dl, dc: jnp.where(L["strict"], dl * dc, 0.0), dls, decays)
        mas = _each(lambda da, dc: da * dc, das, decays)
        dkbs = _each(lambda m, k, dkbe, eg: _bdot(m, k) + dkbe * eg, ms, ks, dkbes, egs)
        dks = _each(lambda m, kb, ma, q, dkd, ek, dkb, be: _bdot(m, kb, "tn") + _bdot(ma, q, "tn") + dkd * ek + dkb * be,
                    ms, kbs, mas, qs, dkds, eks, dkbs, bes)
        dqs = _each(lambda ma, k, dqd, eg: _bdot(ma, k) + dqd * eg, mas, ks, dqds, egs)
        es = _each(lambda m, kk, ma, qk: m * kk + ma * qk, ms, L["kk"], mas, L["qk"])
        ones = jnp.ones((C, dk), BF16)
        row = lax.broadcasted_iota(jnp.int32, (C, dk), 0)
        for i in range(HB * CB):
            h, c = divmod(i, CB)
            rs, cs = slice(c * C, (c + 1) * C), slice(h * dk, (h + 1) * dk)
            e = es[i]
            e_hi = e.astype(BF16)
            col = _bdot(e_hi, ones, "tn") + _bdot(e - e_hi.astype(F32), ones, "tn")
            t_kd = _rsum(dkds[i] * kds[i])
            dgc = (jnp.broadcast_to(jnp.sum(e, axis=1, keepdims=True), (C, dk)) - col + _rsum(dqds[i] * qds[i]) - t_kd
                   + _rsum(dkbes[i] * L["kbe"][i]))
            dglast = jnp.sum(t_kd[:, 0:1], axis=0, keepdims=True) + dgams[i] * gams[i][:, 0:1]
            dgc = dgc + jnp.where(row == C - 1, dglast, 0.0)
            dqkv_ref[0, rs, cs] = dqs[i]
            dqkv_ref[1, rs, cs] = dks[i]
            dqkv_ref[2, rs, cs] = dvbs[i] * bes[i]
            db_ref[rs, cs] = _rsum(dkbs[i] * ks[i]) + _rsum(dvbs[i] * vs[i])
            dg_ref[rs, cs] = _cumsum_rows(dgc, reverse=True)

    blk = lambda off: pl.BlockSpec((R, HB * dk), lambda h, n: (NB - 1 - n, off + h))
    W = H * dk
    return _pcall(body, name="delta_bwd", grid=(G, NB),
                  in_specs=[blk(0), blk(G), blk(2 * G), blk(0), blk(0),
                            pl.BlockSpec((HB, CB, dk, dk), lambda h, n: (h, NB - 1 - n, 0, 0)), blk(0)],
                  out_specs=(pl.BlockSpec((3, R, HB * dk), lambda h, n: (0, NB - 1 - n, h)), blk(0), blk(0)),
                  out_shape=(_sds((3, S, W), F32), _sds((S, W), F32), _sds((S, W), F32)),
                  scratch_shapes=[pltpu.VMEM((HB, dk, dk), F32)],
                  compiler_params=_cp("arbitrary", "arbitrary"))(qkv, qkv, qkv, beta_b, g_b, states, do)


def _rope_consts():
    lane = np.arange(LANES) % SW_HD
    half = ROT_DIM // 2
    inv = (ROPE_THETA ** (-np.arange(half, dtype=np.float32) * np.float32(2.0 / ROT_DIM))).astype(np.float32)
    freq = np.where(lane < ROT_DIM, inv[lane % half], 0.0).astype(np.float32)
    lo = (lane < half).astype(np.float32)
    hi = ((lane >= half) & (lane < ROT_DIM)).astype(np.float32)
    return jnp.asarray(np.stack([freq, -lo, hi] + [np.zeros(LANES, np.float32)] * 5))


def _rope_tables(pos_col):
    S = pos_col.shape[0]
    tq = _tile(S, 1024, SUBLANES)

    def fn(i, p_ref, c_ref, cos_ref, s1_ref, s2_ref):
        ang = p_ref[...].astype(F32) * c_ref[0:1, :]
        sn = jnp.sin(ang)
        cos_ref[...] = jnp.cos(ang)
        s1_ref[...] = sn * c_ref[1:2, :]
        s2_ref[...] = sn * c_ref[2:3, :]

    o, r = _sds((S, LANES), F32), _rb(tq, LANES)
    return _rows("rope_tables", fn, S, tq, (pos_col, _rope_consts()), [_rb(tq, 1), _full((SUBLANES, LANES))], (o, o, o), (r, r, r))


def _wide(a, w):
    return a if w == LANES else jnp.tile(a, (1, w // LANES))


def _rope(x, cos, s1, s2):
    w, h = x.shape[1], ROT_DIM // 2
    return x * _wide(cos, w) + pltpu.roll(x, w - h, 1) * _wide(s1, w) + pltpu.roll(x, h, 1) * _wide(s2, w)


def _unrope(d, cos, s1, s2):
    w, h = d.shape[1], ROT_DIM // 2
    return d * _wide(cos, w) + pltpu.roll(d * _wide(s1, w), h, 1) + pltpu.roll(d * _wide(s2, w), w - h, 1)


def _swa_setup(n, q_ref, kc_ref, kp_ref, vc_ref, vp_ref, tc, tp):
    B = SW_BLOCK
    qr = _rope(q_ref[...].astype(F32), tc[0][...], tc[1][...], tc[2][...]) * (SW_HD ** -0.5)
    kw = jnp.concatenate([_rope(kp_ref[...].astype(F32), tp[0][...], tp[1][...], tp[2][...]),
                          _rope(kc_ref[...].astype(F32), tc[0][...], tc[1][...], tc[2][...])], axis=0)
    vw = jnp.concatenate([vp_ref[...], vc_ref[...]], axis=0).astype(F32)
    lane = lax.broadcasted_iota(jnp.int32, (2 * B, LANES), 1)
    heads = []
    for hk in range(SW_KV_HEADS):
        kh, vh = kw[:, hk * SW_HD:(hk + 1) * SW_HD], vw[:, hk * SW_HD:(hk + 1) * SW_HD]
        kk, vv = jnp.concatenate([kh, kh], axis=1), jnp.concatenate([vh, vh], axis=1)
        heads.append(tuple(jnp.where(sel, t, 0.0).astype(BF16) for t in (kk, vv) for sel in (lane < SW_HD, lane >= SW_HD)))
    qi = lax.broadcasted_iota(jnp.int32, (B, 2 * B), 0) + B
    ki = lax.broadcasted_iota(jnp.int32, (B, 2 * B), 1)
    off = qi - ki
    ok = (off >= 0) & (off < SW_BLOCK) & ((ki >= B) | (n > 0))
    return qr, heads, jnp.where(ok, 0.0, -1e30), lane


SWA_GROUPS = 4
SWA_GROUPS_BWD = 2


def _swa_probs(items, qs, heads, bias, sk_ref, G2):
    ss = [_bdot(qs[j], heads[j // G2][half], "nt") + bias for j, half in items]
    sks = [sk_ref[0:1, 2 * j + half:2 * j + half + 1] for j, half in items]
    ms = [jnp.maximum(jnp.max(s, axis=-1, keepdims=True), sk) for s, sk in zip(ss, sks)]
    ps = [jnp.exp(s - m) for s, m in zip(ss, ms)]
    es = [jnp.exp(sk - m) for sk, m in zip(sks, ms)]
    inv = [1.0 / (jnp.sum(p, axis=-1, keepdims=True) + e) for p, e in zip(ps, es)]
    return [p * i for p, i in zip(ps, inv)], [e * i for e, i in zip(es, inv)]


def _swa_specs(W, cb_q, cb_k):
    B = SW_BLOCK
    assert (W // LANES) % SWA_GROUPS == 0 and (W // LANES) % SWA_GROUPS_BWD == 0
    cur = lambda w, cb: pl.BlockSpec((B, w), lambda n: (n, cb))
    prv = lambda w, cb: pl.BlockSpec((B, w), lambda n: (jnp.maximum(n - 1, 0), cb))
    specs = [cur(W, cb_q), cur(LANES, cb_k), prv(LANES, cb_k), cur(LANES, cb_k + 1), prv(LANES, cb_k + 1)]
    return specs + [cur(LANES, 0)] * 3 + [prv(LANES, 0)] * 3 + [_full((1, LANES))]


def _swa_fwd(proj, tabs, sinks_row, W, cb_q, cb_k):
    S = proj.shape[0]
    G2 = SW_Q_HEADS // SW_KV_HEADS // 2

    def body(q_ref, kc_ref, kp_ref, vc_ref, vp_ref, c0, c1, c2, p0, p1, p2, sk_ref, o_ref):
        n = pl.program_id(0)
        qr, heads, bias, _ = _swa_setup(n, q_ref, kc_ref, kp_ref, vc_ref, vp_ref, (c0, c1, c2), (p0, p1, p2))
        qs = [qr[:, j * LANES:(j + 1) * LANES].astype(BF16) for j in range(W // LANES)]
        for j0 in range(0, W // LANES, SWA_GROUPS):
            items = [(j, half) for j in range(j0, j0 + SWA_GROUPS) for half in range(2)]
            probs, _ = _swa_probs(items, qs, heads, bias, sk_ref, G2)
            pv = [_bdot(p, heads[j // G2][2 + half]) for p, (j, half) in zip(probs, items)]
            for g in range(SWA_GROUPS):
                o_ref[:, (j0 + g) * LANES:(j0 + g + 1) * LANES] = (pv[2 * g] + pv[2 * g + 1]).astype(BF16)

    t = tuple(tabs)
    return _pcall(body, name="swa_fwd", grid=(S // SW_BLOCK,), in_specs=_swa_specs(W, cb_q, cb_k),
                  out_specs=pl.BlockSpec((SW_BLOCK, W), lambda n: (n, 0)), out_shape=_sds((S, W), BF16),
                  compiler_params=_cp("arbitrary"))(proj, proj, proj, proj, proj, *t, *t, sinks_row)


def _swa_bwd(proj, tabs, sinks_row, do, W, cb_q, cb_k):
    S = proj.shape[0]
    B = SW_BLOCK
    G2 = SW_Q_HEADS // SW_KV_HEADS // 2
    SKR = -(-SW_Q_HEADS // SUBLANES) * SUBLANES

    def body(q_ref, kc_ref, kp_ref, vc_ref, vp_ref, c0, c1, c2, p0, p1, p2, sk_ref, do_ref,
             dq_ref, dkc_ref, dkp_ref, dvc_ref, dvp_ref, dsk_ref):
        n = pl.program_id(0)
        qr, heads, bias, lane = _swa_setup(n, q_ref, kc_ref, kp_ref, vc_ref, vp_ref, (c0, c1, c2), (p0, p1, p2))

        @pl.when(n == 0)
        def _():
            dsk_ref[...] = jnp.zeros((SKR, LANES), F32)

        acc_k = [jnp.zeros((2 * B, LANES), F32) for _ in range(SW_KV_HEADS)]
        acc_v = [jnp.zeros((2 * B, LANES), F32) for _ in range(SW_KV_HEADS)]
        qs = [qr[:, j * LANES:(j + 1) * LANES].astype(BF16) for j in range(W // LANES)]
        dos = [do_ref[:, j * LANES:(j + 1) * LANES].astype(BF16) for j in range(W // LANES)]
        dqs = []
        for j0 in range(0, W // LANES, SWA_GROUPS_BWD):
            items = [(j, half) for j in range(j0, j0 + SWA_GROUPS_BWD) for half in range(2)]
            probs, psinks = _swa_probs(items, qs, heads, bias, sk_ref, G2)
            dps = [_bdot(dos[j], heads[j // G2][2 + half], "nt") for j, half in items]
            deltas = [jnp.sum(p * dp, axis=-1, keepdims=True) for p, dp in zip(probs, dps)]
            dss = [(p * (dp - dl)).astype(BF16) for p, dp, dl in zip(probs, dps, deltas)]
            pbs = [p.astype(BF16) for p in probs]
            dqp = [_bdot(ds, heads[j // G2][half]) for ds, (j, half) in zip(dss, items)]
            dkk = [_bdot(ds, qs[j], "tn") for ds, (j, half) in zip(dss, items)]
            dvv = [_bdot(p, dos[j], "tn") for p, (j, half) in zip(pbs, items)]
            for i, (j, half) in enumerate(items):
                hk, h = j // G2, 2 * j + half
                sel = (lane < SW_HD) if half == 0 else (lane >= SW_HD)
                acc_k[hk] = acc_k[hk] + jnp.where(sel, dkk[i], 0.0)
                acc_v[hk] = acc_v[hk] + jnp.where(sel, dvv[i], 0.0)
                dsk_ref[h:h + 1, :] += jnp.broadcast_to(-jnp.sum(psinks[i] * deltas[i], axis=0, keepdims=True), (1, LANES))
            dqs += [dqp[2 * g] + dqp[2 * g + 1] for g in range(SWA_GROUPS_BWD)]
        dq = jnp.concatenate(dqs, axis=1) * (SW_HD ** -0.5)
        dq_ref[...] = _unrope(dq, c0[...], c1[...], c2[...]).astype(BF16)
        fold = lambda a: a[:, :SW_HD] + a[:, SW_HD:]
        dkw = jnp.concatenate([fold(a) for a in acc_k], axis=1)
        dvw = jnp.concatenate([fold(a) for a in acc_v], axis=1)
        dkp_ref[...], dkc_ref[...] = dkw[:B], dkw[B:]
        dvp_ref[...], dvc_ref[...] = dvw[:B], dvw[B:]

    t = tuple(tabs)
    blk = lambda w: pl.BlockSpec((B, w), lambda n: (n, 0))
    o = _sds((S, LANES), F32)
    return _pcall(body, name="swa_bwd", grid=(S // B,), in_specs=_swa_specs(W, cb_q, cb_k) + [blk(W)],
                  out_specs=(blk(W), blk(LANES), blk(LANES), blk(LANES), blk(LANES), _full((SKR, LANES))),
                  out_shape=(_sds((S, W), BF16), o, o, o, o, _sds((SKR, LANES), F32)),
                  compiler_params=_cp("arbitrary"))(proj, proj, proj, proj, proj, *t, *t, sinks_row, do)


def _swa_kv_combine(dkc, dkp, dvc, dvp, tabs):
    S = dkc.shape[0]
    B = SW_BLOCK
    nb = S // B

    def fn(n, kc_ref, kp_ref, vc_ref, vp_ref, c0, c1, c2, o_ref):
        more = n < nb - 1
        dk = kc_ref[...] + jnp.where(more, kp_ref[...], 0.0)
        dv = vc_ref[...] + jnp.where(more, vp_ref[...], 0.0)
        o_ref[...] = jnp.concatenate([_unrope(dk, c0[...], c1[...], c2[...]), dv], axis=1).astype(BF16)

    cur = _rb(B, LANES)
    nxt = pl.BlockSpec((B, LANES), lambda n: (jnp.minimum(n + 1, nb - 1), 0))
    return _rows("swa_kv_combine", fn, S, B, (dkc, dkp, dvc, dvp, *tabs), [cur, nxt, cur, nxt, cur, cur, cur],
                 _sds((S, 2 * LANES), BF16), _rb(B, 2 * LANES))


ANY = pl.BlockSpec(memory_space=pl.ANY)


def _place():
    x, y, c = lax.axis_index("x"), lax.axis_index("y"), lax.axis_index("c")
    return x, y, c, [(1 - x, y), (x, 1 - y), (1 - x, 1 - y)]


def _comm_call(name, body, out_shapes, n_sems, n_local, *ins):
    return _pcall(body, name=name, out_shape=tuple(out_shapes), in_specs=[ANY] * len(ins), out_specs=tuple(ANY for _ in out_shapes),
                  scratch_shapes=[pltpu.SemaphoreType.DMA((n_sems,)), pltpu.SemaphoreType.DMA((n_sems,)),
                                  pltpu.SemaphoreType.DMA((n_local,))])(*ins)


def _remote(src, dst, send, recv, k, to):
    return pltpu.make_async_remote_copy(src_ref=src, dst_ref=dst, send_sem=send.at[k], recv_sem=recv.at[k], device_id=to,
                                        device_id_type=MESH)


def _gather_chips(name, arrs):
    n = len(arrs)
    Lh = arrs[0].shape[0] // 2

    def body(*refs):
        w, o, (send, recv, _) = refs[:n], refs[n:2 * n], refs[2 * n:]
        x, y, c, chips = _place()
        me, sib = 2 * x + y, (x, y, 1 - c)
        own, other = pl.ds(c * Lh, Lh), pl.ds((1 - c) * Lh, Lh)
        idx = [2 * cx + cy for cx, cy in chips]
        first = [[_remote(w[a].at[own], o[a].at[me, own], send, recv, 6 * a + j, (*chips[j], c)) for j in range(3)] for a in range(n)]
        passed = [[_remote(o[a].at[idx[j], own], o[a].at[idx[j], own], send, recv, 6 * a + 3 + j, sib) for j in range(3)] for a in range(n)]
        for cp in [cp for row in first for cp in row]:
            cp.start()
        for j in range(3):
            for a in range(n):
                _remote(w[a].at[own], o[a].at[idx[j], own], send, recv, 6 * a + j, (*chips[j], c)).wait_recv()
                passed[a][j].start()
        for j in range(3):
            for a in range(n):
                _remote(w[a].at[other], o[a].at[idx[j], other], send, recv, 6 * a + 3 + j, sib).wait_recv()
        for cp in [cp for row in first + passed for cp in row]:
            cp.wait_send()

    return _comm_call(name, body, [_sds((4,) + a.shape, a.dtype) for a in arrs], 6 * n, 1, *arrs)


def _pair_swap(name, arrs, whole=False):
    n = len(arrs)
    Lh = arrs[0].shape[0] if whole else arrs[0].shape[0] // 2

    def body(*refs):
        g, o, (send, recv, _) = refs[:n], refs[n:2 * n], refs[2 * n:]
        x, y, c, _ = _place()
        cps = [_remote(g[a] if whole else g[a].at[pl.ds((1 - c) * Lh, Lh)], o[a], send, recv, a, (x, y, 1 - c)) for a in range(n)]
        for cp in cps:
            cp.start()
        for cp in cps:
            cp.wait()

    return _comm_call(name, body, [_sds((Lh,) + a.shape[1:], a.dtype) for a in arrs], n, 1, *arrs)


def _chip_slice(ref, axis, s):
    if axis is None:
        return ref.at[s]
    q = ref.shape[axis] // 4
    start = s * q if isinstance(s, int) else pl.multiple_of(s * q, q)
    return ref.at[tuple([slice(None)] * axis + [pl.ds(start, q)])]


def _scatter_chips(name, items):
    n = len(items)
    part = lambda a, ax: a.shape[1:] if ax is None else tuple(d // 4 if i == ax else d for i, d in enumerate(a.shape))

    def body(*refs):
        p, o, (send, recv, _) = refs[:n], refs[n:2 * n], refs[2 * n:]
        x, y, c, chips = _place()
        me = 2 * x + y
        idx = [2 * cx + cy for cx, cy in chips]
        cps = [_remote(_chip_slice(p[a], items[a][1], idx[j]), o[a].at[me], send, recv, 3 * a + j, (*chips[j], c))
               for a in range(n) for j in range(3)]
        for cp in cps:
            cp.start()
        for a in range(n):
            for j in range(3):
                _remote(_chip_slice(p[a], items[a][1], me), o[a].at[idx[j]], send, recv, 3 * a + j, (*chips[j], c)).wait_recv()
        for cp in cps:
            cp.wait_send()

    return _comm_call(name, body, [_sds((4,) + part(a, ax), a.dtype) for a, ax in items], 3 * n, 1, *[a for a, _ in items])


def _own_part(a, axis, me):
    if axis is None:
        return lax.dynamic_index_in_dim(a, me, 0, keepdims=False)
    q = a.shape[axis] // 4
    return lax.dynamic_slice_in_dim(a, me * q, q, axis)


HBM = pl.BlockSpec(memory_space=pltpu.HBM)
SEM = pl.BlockSpec(memory_space=pltpu.SEMAPHORE)
EFFECT = pltpu.SideEffectType.DATAFLOW_SIDE_EFFECTING


def _chip_copies(w, land, send, recv):
    x, y, c, chips = _place()
    here = [pltpu.make_async_remote_copy(src_ref=w[a], dst_ref=land[a].at[2 * x + y], send_sem=send[3 * a + j], recv_sem=recv[3 * a + j],
                                         device_id=(*chips[j], c), device_id_type=MESH) for a in range(len(w)) for j in range(3)]
    lands = [pltpu.make_async_remote_copy(src_ref=w[a], dst_ref=land[a].at[2 * chips[j][0] + chips[j][1]], send_sem=send[3 * a + j],
                                          recv_sem=recv[3 * a + j], device_id=(*chips[j], c), device_id_type=MESH)
             for a in range(len(w)) for j in range(3)]
    return here, lands


def _gather_chips_start(name, arrs):
    n = len(arrs)
    lands = [lax.empty((4,) + a.shape, a.dtype) for a in arrs]

    def body(*refs):
        w, land, outs = refs[:n], refs[n:2 * n], refs[2 * n:]
        for cp in _chip_copies(w, land, outs[:3 * n], outs[3 * n:6 * n])[0]:
            cp.start()
        outs[-1][...] = jnp.zeros((SUBLANES, LANES), F32)

    thru = [pltpu.HBM(a.shape, a.dtype) for a in arrs + lands]
    outs = _pcall(body, name=name, out_shape=tuple([pltpu.SemaphoreType.DMA(())] * (6 * n) + thru + [_sds((SUBLANES, LANES), F32)]),
                  in_specs=[HBM] * (2 * n), out_specs=tuple([SEM] * (6 * n) + [HBM] * (2 * n) + [pl.BlockSpec(memory_space=pltpu.VMEM)]),
                  input_output_aliases={i: 6 * n + i for i in range(2 * n)},
                  compiler_params=pltpu.CompilerParams(has_side_effects=EFFECT))(
        *[pltpu.with_memory_space_constraint(a, pltpu.HBM) for a in arrs + lands])
    return outs[:6 * n], outs[6 * n:7 * n], outs[7 * n:8 * n], outs[-1]


def _gather_chips_wait(name, sems, arrs, lands, after):
    n = len(arrs)

    def body(*refs):
        w, land, sem = refs[:n], refs[n:2 * n], refs[2 * n:8 * n]
        for cp in _chip_copies(w, land, sem[:3 * n], sem[3 * n:])[1]:
            cp.wait_send()
            cp.wait_recv()

    thru = tuple(pltpu.HBM(a.shape, a.dtype) for a in list(arrs) + list(lands))
    outs = _pcall(body, name=name, out_shape=thru, in_specs=[HBM] * (2 * n) + [SEM] * (6 * n) + [ANY], out_specs=tuple([HBM] * (2 * n)),
                  input_output_aliases={i: i for i in range(2 * n)},
                  compiler_params=pltpu.CompilerParams(has_side_effects=EFFECT))(*arrs, *lands, *sems, after)
    return outs[n:]


def _gather_all(name, b):
    R, C = b.shape
    flips = [(dx, dy, dc) for dx in (0, 1) for dy in (0, 1) for dc in (0, 1)][1:]

    def body(b_ref, o_ref, send, recv, lsem):
        x, y, c, _ = _place()
        me = 4 * x + 2 * y + c
        peers = [(x ^ dx, y ^ dy, c ^ dc) for dx, dy, dc in flips]
        mine = pltpu.make_async_copy(b_ref, o_ref.at[me], lsem.at[0])
        mine.start()
        cps = [_remote(b_ref, o_ref.at[me], send, recv, k, peer) for k, peer in enumerate(peers)]
        for cp in cps:
            cp.start()
        for k, (px, py, pc) in enumerate(peers):
            _remote(b_ref, o_ref.at[4 * px + 2 * py + pc], send, recv, k, (px, py, pc)).wait_recv()
        for cp in cps:
            cp.wait_send()
        mine.wait()

    return _comm_call(name, body, [_sds((8, R, C), b.dtype)], 7, 1, b)[0]


def _block_rows(rows, width):
    return _tile(rows, max(SUBLANES, (1 << 19) // width), SUBLANES)


def _add_half(name, g, got):
    L, A, B = g.shape
    Lh = L // 2
    tq = _block_rows(A, B)

    def body(c_ref, g_ref, r_ref, o_ref):
        o_ref[...] = (g_ref[...] + r_ref[...]).astype(BF16)

    spec = pltpu.PrefetchScalarGridSpec(
        num_scalar_prefetch=1, grid=(Lh, A // tq),
        in_specs=[pl.BlockSpec((1, tq, B), lambda l, i, c_ref: (c_ref[0] * Lh + l, i, 0)),
                  pl.BlockSpec((1, tq, B), lambda l, i, c_ref: (l, i, 0))],
        out_specs=pl.BlockSpec((1, tq, B), lambda l, i, c_ref: (l, i, 0)))
    return _pcall(body, name=name, grid_spec=spec, out_shape=_sds((Lh, A, B), BF16),
                  compiler_params=_cp("arbitrary", "arbitrary"))(lax.axis_index("c").reshape(1).astype(jnp.int32), g, got)


def _sum_slots(name, a):
    n, R, C = a.shape
    tq = _block_rows(R, n * C)

    def fn(i, a_ref, o_ref):
        t = a_ref[0].astype(F32)
        for s in range(1, n):
            t = t + a_ref[s].astype(F32)
        o_ref[...] = t

    return _rows(name, fn, R, tq, (a,), [pl.BlockSpec((n, tq, C), lambda i: (0, i, 0))], _sds((R, C), F32), _rb(tq, C))


def _adam_update(w, g, m, v):
    mn = ADAM_B1 * m + (1.0 - ADAM_B1) * g
    vn = ADAM_B2 * v + (1.0 - ADAM_B2) * (g * g)
    m_hat = mn / (1.0 - ADAM_B1 ** ADAM_STEP)
    v_hat = vn / (1.0 - ADAM_B2 ** ADAM_STEP)
    return -ADAM_LR * (m_hat / (jnp.sqrt(v_hat) + ADAM_EPS) + ADAM_WD * w), mn, vn


def _adamw(name, w, g, m, v):
    R, C = w.shape
    tq = _tile(R, 256, SUBLANES)

    def fn(i, w_ref, g_ref, m_ref, v_ref, d_ref, mo_ref, vo_ref):
        d_ref[...], mo_ref[...], vo_ref[...] = _adam_update(w_ref[...], g_ref[...], m_ref[...], v_ref[...])

    r, o = _rb(tq, C), _sds((R, C), F32)
    return _rows(name, fn, R, tq, (w, g, m, v), [r, r, r, r], (o, o, o), (r, r, r))


def _adamw_halves(name, w, mine, theirs, m, v):
    L, A, B = w.shape
    Lh = L // 2
    tq = _tile(A, 256, SUBLANES)

    def body(c_ref, w_ref, a_ref, b_ref, m_ref, v_ref, g_ref, d_ref, mo_ref, vo_ref):
        is_mine = pl.program_id(0) // Lh == c_ref[0]
        g = jnp.where(is_mine, a_ref[...], b_ref[...])
        g_ref[...] = g
        d_ref[...], mo_ref[...], vo_ref[...] = _adam_update(w_ref[...], g, m_ref[...], v_ref[...])

    full = pl.BlockSpec((1, tq, B), lambda l, i, c_ref: (l, i, 0))
    half = pl.BlockSpec((1, tq, B), lambda l, i, c_ref: (l % Lh, i, 0))
    o = _sds((L, A, B), F32)
    spec = pltpu.PrefetchScalarGridSpec(num_scalar_prefetch=1, grid=(L, A // tq), in_specs=[full, half, half, full, full],
                                        out_specs=(full, full, full, full))
    return _pcall(body, name=name, grid_spec=spec, out_shape=(o, o, o, o),
                  compiler_params=_cp("arbitrary", "arbitrary"))(lax.axis_index("c").reshape(1).astype(jnp.int32), w, mine, theirs, m, v)


def _pack(arrs, width, lead=()):
    nl = len(lead)
    flat = jnp.concatenate([a.reshape(lead + (-1,)) for a in arrs], axis=nl)
    n = flat.shape[-1]
    unit = PACK_ROWS * width
    tot = -(-n // unit) * unit
    flat = jnp.pad(flat, [(0, 0)] * nl + [(0, tot - n)])
    return flat.reshape(lead + (tot // width, width))


def _unpack(buf, shapes, lead=()):
    flat = buf.reshape(lead + (-1,))
    out, off = [], 0
    for s in shapes:
        n = int(np.prod(s))
        out.append(flat[..., off:off + n].reshape(lead + tuple(s)))
        off += n
    return out


def _in_groups(W, H):
    o_sq = 4 * W + 2 * H
    o_k = o_sq + W
    o_g = o_k + 2 * KV_W
    return [(0, 4 * W), (o_sq, o_k), (o_g, o_g + 2 * W), (o_k, o_g), (4 * W, o_sq)]


def _relayout_in(shards, W, H):
    c4 = sum(hi - lo for lo, hi in _in_groups(W, H)) // 4
    parts = []
    for lo, hi in _in_groups(W, H):
        for s in range(4):
            a, b = max(lo, s * c4), min(hi, (s + 1) * c4)
            if a < b:
                parts.append(shards[s][:, a - s * c4:b - s * c4])
    parts.append(jnp.zeros((shards.shape[1], BA_W - 2 * H), shards.dtype))
    return jnp.concatenate(parts, axis=1)


def _shard_in(d, W, H):
    groups = _in_groups(W, H)
    starts = [sum(hi - lo for lo, hi in groups[:i]) for i in range(len(groups))]
    stored = sorted(zip(groups, starts))
    c4 = sum(hi - lo for lo, hi in groups) // 4
    out = []
    for s in range(4):
        parts = []
        for (lo, hi), at in stored:
            a, b = max(lo, s * c4), min(hi, (s + 1) * c4)
            if a < b:
                parts.append(d[:, :, at + a - lo:at + b - lo])
        out.append(jnp.concatenate(parts, axis=2))
    return jnp.stack(out)


def _lane_row(vals, at):
    return jnp.pad(vals, (at, LANES - at - vals.shape[0]))[None]


def _layer_fwd(x, lw, tabs, W, H):
    D = x.shape[1]
    cbk = 7 * W // LANES
    h = _pre_norm(x, lw["g1"])
    proj = _mm("mm_in", h, lw["win"], "nn", BF16, tn=768)
    ba = _mm("mm_ba", h, lw["win"][:, 7 * W + 2 * KV_W:], "nn", F32)
    qkv = _dn_prep(proj, lw["conv"], W)
    beta_b, g_b = _dn_gates(ba, lw["alog"], lw["dt"], H)
    o, st = _delta_fwd(qkv, beta_b, g_b, H, DELTA_CB, DELTA_HB)
    oa = _dn_out(o, proj, lw["ng"], W, 3)
    ob = _swa_fwd(proj, tabs, lw["sinks"], W, 4, cbk)
    ya = _mm("mm_up_dn", oa, lw["wup_dn"], "nn", BF16)
    yb = _mm("mm_up_sw", ob, lw["wup_sw"], "nn", BF16)
    mixin = _mix(proj, ya, yb, D, 5)
    mix = _mm("mm_o", mixin, lw["wo"], "nn", F32)
    x1, h2 = _post_mix(x, mix, lw["g2"], lw["g3"])
    f1, act = _mm("mm_ff1", h2, lw["wff1"], "nn", out_dtypes=(BF16, BF16), epi=lambda acc: (acc, jnp.square(jnp.maximum(acc, 0.0))))
    ff = _mm("mm_ff2", act, lw["wff2"], "nn", F32)
    x2 = _post_mlp(x1, ff, lw["g4"])
    saved = dict(x=x, h=h, proj=proj, ba=ba, qkv=qkv, beta_b=beta_b, g_b=g_b, o=o, st=st, oa=oa, ob=ob, ya=ya, yb=yb,
                 mixin=mixin, mix=mix, x1=x1, h2=h2, f1=f1, act=act, ff=ff)
    return x2, saved


def _layer_bwd(dx2, lw, sv, tabs, W, H, l, big):
    D = dx2.shape[1]
    cbk = 7 * W // LANES
    big = dict(big)
    dff, dg4 = _post_mlp_bwd(sv["ff"], lw["g4"], dx2)
    df1 = _mm("mm_ff2_dx", dff, lw["wff2"], "nt", BF16, extras=(sv["f1"],),
              epi=lambda acc, f1: (acc * 2.0 * jnp.maximum(f1.astype(F32), 0.0),))
    big["w_ff2"] = _mm("mm_ff2_dw", sv["act"], dff, "tn", slab=(big["w_ff2"], l))
    dh2 = _mm("mm_ff1_dx", df1, lw["wff1"], "nt", F32)
    big["w_ff1"] = _mm("mm_ff1_dw", sv["h2"], df1, "tn", slab=(big["w_ff1"], l))
    dx1, dmix, dg3, dg2 = _mid_bwd(sv["x1"], lw["g3"], dh2, dx2, sv["mix"], lw["g2"])
    dmixin = _mm("mm_o_dx", dmix, lw["wo"], "nt", BF16)
    big["w_o"] = _mm("mm_o_dw", sv["mixin"], dmix, "tn", slab=(big["w_o"], l))
    dya, dyb, dga, dgb = _mix_bwd(sv["proj"], sv["ya"], sv["yb"], dmixin, D, 5)
    doa = _mm("mm_up_dn_dx", dya, lw["wup_dn"], "nt", BF16)
    big["w_up_dn"] = _mm("mm_up_dn_dw", sv["oa"], dya, "tn", slab=(big["w_up_dn"], l))
    dob = _mm("mm_up_sw_dx", dyb, lw["wup_sw"], "nt", BF16)
    big["w_up_sw"] = _mm("mm_up_sw_dw", sv["ob"], dyb, "tn", slab=(big["w_up_sw"], l))
    do, dz, dng = _dn_out_bwd(sv["o"], sv["proj"], lw["ng"], doa, W, 3)
    dqkvn, dbeta_b, dg_b = _delta_bwd(sv["qkv"], sv["beta_b"], sv["g_b"], sv["st"], do, H, DELTA_CB, DELTA_HB)
    dba, dalog, ddt = _dn_gates_bwd(sv["ba"], lw["alog"], lw["dt"], dbeta_b, dg_b, H)
    dc, dconv = _dn_prep_bwd_a(sv["proj"], lw["conv"], dqkvn, W)
    dqkv = _dn_prep_bwd_b(dc, lw["conv"], W)
    dq_sw, dkc, dkp, dvc, dvp, dsk = _swa_bwd(sv["proj"], tabs, lw["sinks"], dob, W, 4, cbk)
    dkv = _swa_kv_combine(dkc, dkp, dvc, dvp, tabs)
    dproj = jnp.concatenate([dqkv, dz, dq_sw, dga, dgb, dkv, dba], axis=1)
    dh = _mm("mm_in_dx", dproj, lw["win"], "nt", F32, tk=768)
    big["w_in"] = _mm("mm_in_dw", sv["h"], dproj, "tn", tn=768, slab=(big["w_in"], l))
    dx, dg1 = _pre_norm_bwd(sv["x"], lw["g1"], dh, dx1)
    grads = dict(pre_mix_g=dg1[0], dn_conv_w=dconv, dn_a_log=dalog[0, H:2 * H], dn_dt_bias=ddt[0, H:2 * H], dn_norm_g=dng[0],
                 sw_sinks=dsk[:SW_Q_HEADS, 0], post_mix_g=dg2[0], pre_mlp_g=dg3[0], post_mlp_g=dg4[0])
    return dx, grads, big


_WEIGHTS = ["pre_mix_g", "w_in", "dn_conv_w", "dn_a_log", "dn_dt_bias", "dn_norm_g", "sw_sinks", "w_up_dn", "w_up_sw", "w_o",
            "post_mix_g", "pre_mlp_g", "w_ff1", "w_ff2", "post_mlp_g"]
_BIG = {"w_in": 2, "w_up_dn": 1, "w_up_sw": 1, "w_o": 1, "w_ff1": 2, "w_ff2": 1}
_SMALL = [n for n in _WEIGHTS if n not in _BIG]


def _step(P):
    x, target = P["x"][0], P["loss_target"][0]
    S, D = x.shape
    L = P["pre_mix_g"].shape[0]
    H, W = DN_HEADS, DN_HEADS * DN_DK
    assert W == D == SW_Q_HEADS * SW_HD and KV_W == LANES
    me = 2 * lax.axis_index("x") + lax.axis_index("y")

    assert L % 4 == 0
    names = list(_BIG) + ["dn_conv_w"]
    local = [P[n].astype(BF16) for n in _BIG] + [P["dn_conv_w"]]
    lo, hi = [a[:L // 2] for a in local], [a[L // 2:] for a in local]
    own_slot = lambda gathered, mine: [lax.dynamic_update_slice_in_dim(g, w[None], me, 0) for g, w in zip(gathered, mine)]
    sems, hi_thru, lands, token = _gather_chips_start("weights_late_start", hi)
    full = dict(zip(names, own_slot(_gather_chips("weights_gather", lo), lo)))

    def layer_weights(full, l, k):
        rows = lambda n: full[n][:, k].reshape(-1, full[n].shape[-1])
        cols = lambda n: jnp.concatenate([full[n][s, k] for s in range(4)], axis=-1)
        return dict(
            g1=P["pre_mix_g"][l][None], win=_relayout_in(full["w_in"][:, k], W, H), conv=cols("dn_conv_w"),
            alog=_lane_row(P["dn_a_log"][l], H), dt=_lane_row(P["dn_dt_bias"][l], H), ng=P["dn_norm_g"][l][None],
            sinks=_lane_row(P["sw_sinks"][l], 0), wup_dn=rows("w_up_dn"), wup_sw=rows("w_up_sw"), wo=rows("w_o"),
            g2=P["post_mix_g"][l][None], g3=P["pre_mlp_g"][l][None], wff1=cols("w_ff1"), wff2=rows("w_ff2"),
            g4=P["post_mlp_g"][l][None])

    tabs = _rope_tables(P["positions"].reshape(S, 1))
    lws = [layer_weights(full, l, l) for l in range(L // 2)]
    lws[0]["g1"] = lws[0]["g1"] + token[0, 0]

    saved = []
    for l in range(L):
        if l == L // 2:
            late = dict(zip(names, own_slot(_gather_chips_wait("weights_late_wait", sems, hi_thru, lands, x), hi)))
            lws += [layer_weights(late, k + L // 2, k) for k in range(L // 2)]
        x, sv = _layer_fwd(x, lws[l], tabs, W, H)
        saved.append(sv)
    loss_row, dx = _loss_head(x, target)
    layer_grads = [None] * L
    F = 4 * P["w_ff1"].shape[2]
    per_layer = dict(w_in=(D, 7 * W + 2 * KV_W + BA_W), w_up_dn=(W, D), w_up_sw=(W, D), w_o=(D, D), w_ff1=(D, F), w_ff2=(F, D))
    grads = {n: lax.empty((L,) + per_layer[n], F32) for n in _BIG}
    for l in reversed(range(L)):
        dx, layer_grads[l], grads = _layer_bwd(dx, lws[l], saved[l], tabs, W, H, l, grads)
    grads.update({n: jnp.stack([layer_grads[l][n] for l in range(L)]) for n in _SMALL})

    got = _pair_swap("grad_pair_swap", [grads[n] for n in _BIG])
    part = {n: _add_half("grad_pair_add_" + n, grads[n], r) for n, r in zip(_BIG, got)}
    items = [(_shard_in(part[n], W, H), None) if n == "w_in" else (part[n], ax) for n, ax in _BIG.items()]
    slots = _scatter_chips("grad_chip_scatter", items)
    halves = []
    for n, s, (a, ax) in zip(_BIG, slots, items):
        s = lax.dynamic_update_slice_in_dim(s, _own_part(a, ax, me)[None], me, 0)
        flat = s.reshape(4, -1, s.shape[-1])
        halves.append(_sum_slots("grad_chip_sum_" + n, flat).reshape(s.shape[1:]))
    theirs = _pair_swap("grad_pair_share", halves, whole=True)
    gsum, delta, new_m, new_v = {}, {}, {}, {}
    for n, mine, their in zip(_BIG, halves, theirs):
        gsum[n], delta[n], new_m[n], new_v[n] = _adamw_halves("adamw_" + n, P[n], mine, their, P["m_" + n], P["v_" + n])
    small_shapes = [(1,)] + [grads[n].shape for n in _SMALL]
    tot = _sum_slots("small_sum", _gather_all("small_gather", _pack([loss_row[0, :1]] + [grads[n] for n in _SMALL], LANES)))
    small = _unpack(tot, small_shapes)
    loss = small[0][0]
    gsum.update(zip(_SMALL, small[1:]))
    cw = P["dn_conv_w"].shape[2]
    gsum["dn_conv_w"] = lax.dynamic_slice_in_dim(gsum["dn_conv_w"], me * cw, cw, axis=2)

    sm_shapes = [P[n].shape for n in _SMALL]
    outs = _adamw("adamw_small", *(_pack([src[pre + n] for n in _SMALL], LANES)
                                   for src, pre in ((P, ""), (gsum, ""), (P, "m_"), (P, "v_"))))
    for d, o in zip((delta, new_m, new_v), outs):
        d.update(zip(_SMALL, _unpack(o, sm_shapes)))

    return (loss, dx[None], *[gsum[n] for n in _WEIGHTS], *[delta[n] for n in _WEIGHTS],
            *[new_m[n] for n in _WEIGHTS], *[new_v[n] for n in _WEIGHTS])


def kernel(x, positions, pre_mix_g, w_in, dn_conv_w, dn_a_log, dn_dt_bias, dn_norm_g, sw_sinks, w_up_dn, w_up_sw, w_o, post_mix_g, pre_mlp_g, w_ff1, w_ff2, post_mlp_g, loss_target, m_pre_mix_g, m_w_in, m_dn_conv_w, m_dn_a_log, m_dn_dt_bias, m_dn_norm_g, m_sw_sinks, m_w_up_dn, m_w_up_sw, m_w_o, m_post_mix_g, m_pre_mlp_g, m_w_ff1, m_w_ff2, m_post_mlp_g, v_pre_mix_g, v_w_in, v_dn_conv_w, v_dn_a_log, v_dn_dt_bias, v_dn_norm_g, v_sw_sinks, v_w_up_dn, v_w_up_sw, v_w_o, v_post_mix_g, v_pre_mlp_g, v_w_ff1, v_w_ff2, v_post_mlp_g):
    vals = (x, positions, pre_mix_g, w_in, dn_conv_w, dn_a_log, dn_dt_bias, dn_norm_g, sw_sinks, w_up_dn, w_up_sw, w_o, post_mix_g, pre_mlp_g, w_ff1, w_ff2, post_mlp_g, loss_target, m_pre_mix_g, m_w_in, m_dn_conv_w, m_dn_a_log, m_dn_dt_bias, m_dn_norm_g, m_sw_sinks, m_w_up_dn, m_w_up_sw, m_w_o, m_post_mix_g, m_pre_mlp_g, m_w_ff1, m_w_ff2, m_post_mlp_g, v_pre_mix_g, v_w_in, v_dn_conv_w, v_dn_a_log, v_dn_dt_bias, v_dn_norm_g, v_sw_sinks, v_w_up_dn, v_w_up_sw, v_w_o, v_post_mix_g, v_pre_mlp_g, v_w_ff1, v_w_ff2, v_post_mlp_g)
    names = ["x", "positions"] + _WEIGHTS + ["loss_target"] + ["m_" + n for n in _WEIGHTS] + ["v_" + n for n in _WEIGHTS]
    return _step(dict(zip(names, vals)))
```

```python
import functools

import numpy as np
import jax
import jax.numpy as jnp
from jax import lax
from jax.experimental import pallas as pl
from jax.experimental.pallas import tpu as pltpu

F32, BF16 = jnp.float32, jnp.bfloat16
MESH = pl.DeviceIdType.MESH

DN_HEADS = 8
DN_DK = 128
DN_CONV = 4
DN_CHUNK = 64
SW_Q_HEADS = 16
SW_KV_HEADS = 2
SW_HD = 64
SW_BLOCK = 128
ROPE_THETA = 500000.0
ROT_DIM = SW_HD // 4
EPS = 1e-6
ADAM_LR, ADAM_B1, ADAM_B2, ADAM_EPS, ADAM_WD, ADAM_STEP = 0.001, 0.9, 0.999, 1e-08, 0.01, 10

LANES = 128
SUBLANES = 8
VMEM_LIMIT = 48 * 1024 * 1024
KV_W = SW_KV_HEADS * SW_HD
BA_W = 256
PACK_ROWS = 512
DELTA_CB = 4
DELTA_HB = 8


def _pcall(body, **kw):
    return pl.pallas_call(body, **kw)


def _cp(*sem):
    return pltpu.CompilerParams(dimension_semantics=sem, vmem_limit_bytes=VMEM_LIMIT)


def _tile(n, pref, unit=LANES):
    if n <= pref:
        return n
    t = (pref // unit) * unit
    while t > unit and n % t:
        t -= unit
    assert n % t == 0, (n, pref)
    return t


def _sds(shape, dtype):
    return jax.ShapeDtypeStruct(tuple(shape), dtype)


_DIMS = {"nn": ((1,), (0,)), "nt": ((1,), (1,)), "tn": ((0,), (0,))}


def _mm(name, a, b, mode, out_dtype=F32, tm=1024, tn=1024, tk=1024, extras=(), epi=None, out_dtypes=None, slab=None):
    if mode == "nn":
        (M, K), (_, N) = a.shape, b.shape
    elif mode == "nt":
        (M, K), (N, _) = a.shape, b.shape
    else:
        (K, M), (_, N) = a.shape, b.shape
    tm, tn, tk = _tile(M, tm), _tile(N, tn), _tile(K, tk)
    nk = K // tk
    a_spec = {"nn": pl.BlockSpec((tm, tk), lambda i, j, k: (i, k)),
              "nt": pl.BlockSpec((tm, tk), lambda i, j, k: (i, k)),
              "tn": pl.BlockSpec((tk, tm), lambda i, j, k: (k, i))}[mode]
    b_spec = {"nn": pl.BlockSpec((tk, tn), lambda i, j, k: (k, j)),
              "nt": pl.BlockSpec((tn, tk), lambda i, j, k: (j, k)),
              "tn": pl.BlockSpec((tk, tn), lambda i, j, k: (k, j))}[mode]
    dims = (_DIMS[mode], ((), ()))
    out_dtypes = tuple(out_dtypes or (out_dtype,))
    ne, no = len(extras), len(out_dtypes)
    o_spec = pl.BlockSpec((tm, tn), lambda i, j, k: (i, j))

    def body(*refs):
        a_ref, b_ref, ex = refs[0], refs[1], refs[2:2 + ne]
        outs = refs[-no:] if nk == 1 else refs[-1 - no:-1]
        part = lax.dot_general(a_ref[...], b_ref[...], dims, preferred_element_type=F32)

        def finish(acc):
            res = epi(acc, *[e[...] for e in ex]) if epi else (acc,)
            for o, r, dt in zip(outs, res, out_dtypes):
                if slab is None:
                    o[...] = r.astype(dt)
                else:
                    o[0] = r.astype(dt)

        if nk == 1:
            finish(part)
            return
        acc_ref, k = refs[-1], pl.program_id(2)

        @pl.when(k == 0)
        def _():
            acc_ref[...] = part

        @pl.when((k > 0) & (k < nk - 1))
        def _():
            acc_ref[...] += part

        @pl.when(k == nk - 1)
        def _():
            finish(acc_ref[...] + part)

    kw = dict(name=name, grid=(M // tm, N // tn, nk), scratch_shapes=[] if nk == 1 else [pltpu.VMEM((tm, tn), F32)],
              compiler_params=_cp("parallel", "parallel", "arbitrary"))
    if slab is not None:
        buf, l = slab
        return _pcall(body, in_specs=[a_spec, b_spec, ANY], out_specs=pl.BlockSpec((1, tm, tn), lambda i, j, k: (l, i, j)),
                      out_shape=_sds(buf.shape, buf.dtype), input_output_aliases={2: 0}, **kw)(a, b, buf)
    out = _pcall(body, in_specs=[a_spec, b_spec] + [o_spec] * ne, out_specs=tuple(o_spec for _ in out_dtypes),
                 out_shape=tuple(_sds((M, N), dt) for dt in out_dtypes), **kw)(a, b, *extras)
    return out if no > 1 else out[0]


def _rows(name, fn, n_rows, tq, ins, in_specs, out_shapes, out_specs):
    def body(*refs):
        fn(pl.program_id(0), *refs)

    return _pcall(body, name=name, grid=(n_rows // tq,), in_specs=in_specs, out_specs=out_specs,
                  out_shape=out_shapes, compiler_params=_cp("arbitrary"))(*ins)


def _rb(tq, w, cb=0):
    return pl.BlockSpec((tq, w), lambda i: (i, cb))


def _full(shape):
    return pl.BlockSpec(tuple(shape), lambda *_: (0,) * len(shape))


def _rms_fwd(x, g):
    r = lax.rsqrt(jnp.mean(x * x, axis=-1, keepdims=True) + EPS)
    return x * r * g


def _rms_bwd(x, g, dy):
    r = lax.rsqrt(jnp.mean(x * x, axis=-1, keepdims=True) + EPS)
    xh = x * r
    t = dy * g
    dx = r * (t - xh * jnp.mean(t * xh, axis=-1, keepdims=True))
    return dx, jnp.sum(dy * xh, axis=0, keepdims=True)


def _acc(i, ref, val):
    @pl.when(i == 0)
    def _():
        ref[...] = val

    @pl.when(i > 0)
    def _():
        ref[...] += val


def _sigmoid(x):
    return 0.5 * jnp.tanh(0.5 * x) + 0.5


def _pre_norm(x, g):
    S, D = x.shape
    tq = _tile(S, 512, SUBLANES)

    def fn(i, x_ref, g_ref, h_ref):
        h_ref[...] = _rms_fwd(x_ref[...], g_ref[...]).astype(BF16)

    return _rows("pre_norm", fn, S, tq, (x, g), [_rb(tq, D), _full((1, D))], _sds((S, D), BF16), _rb(tq, D))


def _post_mix(x, mix, g2, g3):
    S, D = x.shape
    tq = _tile(S, 512, SUBLANES)

    def fn(i, x_ref, m_ref, g2_ref, g3_ref, x1_ref, h2_ref):
        x1 = x_ref[...] + _rms_fwd(m_ref[...], g2_ref[...])
        x1_ref[...] = x1
        h2_ref[...] = _rms_fwd(x1, g3_ref[...]).astype(BF16)

    return _rows("post_mix", fn, S, tq, (x, mix, g2, g3), [_rb(tq, D), _rb(tq, D), _full((1, D)), _full((1, D))],
                 (_sds((S, D), F32), _sds((S, D), BF16)), (_rb(tq, D), _rb(tq, D)))


def _post_mlp(x1, ff, g4):
    S, D = x1.shape
    tq = _tile(S, 512, SUBLANES)

    def fn(i, x_ref, f_ref, g_ref, o_ref):
        o_ref[...] = x_ref[...] + _rms_fwd(f_ref[...], g_ref[...])

    return _rows("post_mlp", fn, S, tq, (x1, ff, g4), [_rb(tq, D), _rb(tq, D), _full((1, D))], _sds((S, D), F32), _rb(tq, D))


def _loss_head(y, target):
    S, D = y.shape
    tq = _tile(S, 512, SUBLANES)

    def fn(i, y_ref, t_ref, l_ref, d_ref):
        e = y_ref[...] - t_ref[...]
        d_ref[...] = e * (1.0 / D)
        part = jnp.sum(jnp.sum(e * e, axis=1, keepdims=True), axis=0, keepdims=True) * (0.5 / D)
        _acc(i, l_ref, jnp.broadcast_to(part, (1, LANES)))

    return _rows("loss_head", fn, S, tq, (y, target), [_rb(tq, D), _rb(tq, D)],
                 (_sds((1, LANES), F32), _sds((S, D), F32)), (_full((1, LANES)), _rb(tq, D)))


def _post_mlp_bwd(ff, g4, dx2):
    S, D = ff.shape
    tq = _tile(S, 512, SUBLANES)

    def fn(i, f_ref, g_ref, d_ref, o_ref, dg_ref):
        dx, dg = _rms_bwd(f_ref[...], g_ref[...], d_ref[...])
        o_ref[...] = dx.astype(BF16)
        _acc(i, dg_ref, dg)

    return _rows("post_mlp_bwd", fn, S, tq, (ff, g4, dx2), [_rb(tq, D), _full((1, D)), _rb(tq, D)],
                 (_sds((S, D), BF16), _sds((1, D), F32)), (_rb(tq, D), _full((1, D))))


def _mid_bwd(x1, g3, dh2, dx2, mix, g2):
    S, D = x1.shape
    tq = _tile(S, 256, SUBLANES)

    def fn(i, x_ref, g3_ref, dh_ref, dx2_ref, m_ref, g2_ref, dx1_ref, dm_ref, dg3_ref, dg2_ref):
        d, dg3 = _rms_bwd(x_ref[...], g3_ref[...], dh_ref[...])
        dx1 = dx2_ref[...] + d
        dx1_ref[...] = dx1
        dm, dg2 = _rms_bwd(m_ref[...], g2_ref[...], dx1)
        dm_ref[...] = dm.astype(BF16)
        _acc(i, dg3_ref, dg3)
        _acc(i, dg2_ref, dg2)

    r, f = _rb(tq, D), _full((1, D))
    return _rows("mid_bwd", fn, S, tq, (x1, g3, dh2, dx2, mix, g2), [r, f, r, r, r, f],
                 (_sds((S, D), F32), _sds((S, D), BF16), _sds((1, D), F32), _sds((1, D), F32)), (r, r, f, f))


def _pre_norm_bwd(x, g1, dh, dx1):
    S, D = x.shape
    tq = _tile(S, 512, SUBLANES)

    def fn(i, x_ref, g_ref, dh_ref, dx1_ref, dx_ref, dg_ref):
        d, dg = _rms_bwd(x_ref[...], g_ref[...], dh_ref[...])
        dx_ref[...] = dx1_ref[...] + d
        _acc(i, dg_ref, dg)

    r, f = _rb(tq, D), _full((1, D))
    return _rows("pre_norm_bwd", fn, S, tq, (x, g1, dh, dx1), [r, f, r, r], (_sds((S, D), F32), _sds((1, D), F32)), (r, f))


def _mix(proj, ya, yb, D, cb_a):
    S = ya.shape[0]
    tq = _tile(S, 256, SUBLANES)

    def fn(i, ga_ref, gb_ref, ya_ref, yb_ref, o_ref):
        ga, gb, ya, yb = (r[...].astype(F32) for r in (ga_ref, gb_ref, ya_ref, yb_ref))
        o_ref[...] = (_sigmoid(ga) * ya + _sigmoid(gb) * yb).astype(BF16)

    return _rows("mix", fn, S, tq, (proj, proj, ya, yb), [_rb(tq, D, cb_a), _rb(tq, D, cb_a + 1), _rb(tq, D), _rb(tq, D)],
                 _sds((S, D), BF16), _rb(tq, D))


def _mix_bwd(proj, ya, yb, dmixin, D, cb_a):
    S = ya.shape[0]
    tq = _tile(S, 256, SUBLANES)

    def fn(i, ga_ref, gb_ref, ya_ref, yb_ref, d_ref, dya_ref, dyb_ref, dga_ref, dgb_ref):
        ga, gb, ya, yb, d = (r[...].astype(F32) for r in (ga_ref, gb_ref, ya_ref, yb_ref, d_ref))
        sa, sb = _sigmoid(ga), _sigmoid(gb)
        dya_ref[...] = (d * sa).astype(BF16)
        dyb_ref[...] = (d * sb).astype(BF16)
        dga_ref[...] = (d * ya * sa * (1.0 - sa)).astype(BF16)
        dgb_ref[...] = (d * yb * sb * (1.0 - sb)).astype(BF16)

    r = _rb(tq, D)
    o = _sds((S, D), BF16)
    return _rows("mix_bwd", fn, S, tq, (proj, proj, ya, yb, dmixin), [_rb(tq, D, cb_a), _rb(tq, D, cb_a + 1), r, r, r],
                 (o, o, o, o), (r, r, r, r))


HALO = 16


def _shift_down(xe, k, tq):
    return pltpu.roll(xe, k, 0)[HALO:HALO + tq]


def _conv_pre(cur_ref, halo_ref, w_ref, i, tq):
    x = cur_ref[...].astype(F32)
    halo = jnp.where(i > 0, halo_ref[...].astype(F32), 0.0)
    xe = jnp.concatenate([halo, x], axis=0)
    xs = [x] + [_shift_down(xe, k, tq) for k in range(1, DN_CONV)]
    w = w_ref[...]
    c = sum(w[DN_CONV - 1 - k:DN_CONV - k, :] * xs[k] for k in range(DN_CONV))
    return c, xs


def _dn_prep(proj, conv_w, W):
    S = proj.shape[0]
    tq = _tile(S, 256, HALO)
    hb = tq // HALO

    def body(cur_ref, halo_ref, w_ref, o_ref):
        j, i = pl.program_id(0), pl.program_id(1)
        c, _ = _conv_pre(cur_ref, halo_ref, w_ref, i, tq)
        y = c * _sigmoid(c)
        scale = jnp.where(j == 0, DN_DK ** -0.5, 1.0)
        for h in range(W // DN_DK):
            sl = slice(h * DN_DK, (h + 1) * DN_DK)
            yh = y[:, sl]
            rs = lax.rsqrt(jnp.sum(yh * yh, axis=-1, keepdims=True) + EPS)
            o_ref[:, sl] = jnp.where(j == 2, yh, yh * rs * scale)

    return _pcall(body, name="dn_prep", grid=(3, S // tq),
                  in_specs=[pl.BlockSpec((tq, W), lambda j, i: (i, j)),
                            pl.BlockSpec((HALO, W), lambda j, i: (jnp.maximum(i * hb - 1, 0), j)),
                            pl.BlockSpec((DN_CONV, W), lambda j, i: (0, j))],
                  out_specs=pl.BlockSpec((tq, W), lambda j, i: (i, j)), out_shape=_sds((S, 3 * W), F32),
                  compiler_params=_cp("arbitrary", "arbitrary"))(proj, proj, conv_w)


def _dn_prep_bwd_a(proj, conv_w, dqkv, W):
    S = proj.shape[0]
    tq = _tile(S, 256, HALO)
    hb = tq // HALO

    def body(cur_ref, halo_ref, w_ref, d_ref, dc_ref, dw_ref):
        j, i = pl.program_id(0), pl.program_id(1)
        c, xs = _conv_pre(cur_ref, halo_ref, w_ref, i, tq)
        sg = _sigmoid(c)
        y = c * sg
        scale = jnp.where(j == 0, DN_DK ** -0.5, 1.0)
        dout = d_ref[0]
        dys = []
        for h in range(W // DN_DK):
            sl = slice(h * DN_DK, (h + 1) * DN_DK)
            yh, dh = y[:, sl], dout[:, sl]
            rs = lax.rsqrt(jnp.sum(yh * yh, axis=-1, keepdims=True) + EPS)
            yn = yh * rs
            dn = scale * rs * (dh - yn * jnp.sum(dh * yn, axis=-1, keepdims=True))
            dys.append(jnp.where(j == 2, dh, dn))
        dy = jnp.concatenate(dys, axis=1)
        dc = dy * (sg * (1.0 + c * (1.0 - sg)))
        dc_ref[...] = dc
        dw = jnp.concatenate([jnp.sum(dc * xs[DN_CONV - 1 - r], axis=0, keepdims=True) for r in range(DN_CONV)], axis=0)
        _acc(i, dw_ref, dw)

    return _pcall(body, name="dn_prep_bwd_a", grid=(3, S // tq),
                  in_specs=[pl.BlockSpec((tq, W), lambda j, i: (i, j)),
                            pl.BlockSpec((HALO, W), lambda j, i: (jnp.maximum(i * hb - 1, 0), j)),
                            pl.BlockSpec((DN_CONV, W), lambda j, i: (0, j)),
                            pl.BlockSpec((1, tq, W), lambda j, i: (j, i, 0))],
                  out_specs=(pl.BlockSpec((tq, W), lambda j, i: (i, j)), pl.BlockSpec((DN_CONV, W), lambda j, i: (0, j))),
                  out_shape=(_sds((S, 3 * W), F32), _sds((DN_CONV, 3 * W), F32)),
                  compiler_params=_cp("arbitrary", "arbitrary"))(proj, proj, conv_w, dqkv)


def _dn_prep_bwd_b(dc, conv_w, W):
    S = dc.shape[0]
    tq = _tile(S, 256, SUBLANES)
    hb = tq // SUBLANES
    nblk = S // tq

    def body(cur_ref, nxt_ref, w_ref, o_ref):
        i = pl.program_id(1)
        d = cur_ref[...]
        nxt = jnp.where(i < nblk - 1, nxt_ref[...], 0.0)
        de = jnp.concatenate([d, nxt], axis=0)
        w = w_ref[...]
        out = w[DN_CONV - 1:DN_CONV, :] * d
        for k in range(1, DN_CONV):
            out = out + w[DN_CONV - 1 - k:DN_CONV - k, :] * pltpu.roll(de, tq + SUBLANES - k, 0)[0:tq]
        o_ref[...] = out.astype(BF16)

    return _pcall(body, name="dn_prep_bwd_b", grid=(3, nblk),
                  in_specs=[pl.BlockSpec((tq, W), lambda j, i: (i, j)),
                            pl.BlockSpec((SUBLANES, W), lambda j, i: (jnp.minimum((i + 1) * hb, S // SUBLANES - 1), j)),
                            pl.BlockSpec((DN_CONV, W), lambda j, i: (0, j))],
                  out_specs=pl.BlockSpec((tq, W), lambda j, i: (i, j)), out_shape=_sds((S, 3 * W), BF16),
                  compiler_params=_cp("arbitrary", "arbitrary"))(dc, dc, conv_w)


def _gate_terms(ba, al, dt):
    u = ba + dt
    sp = jnp.maximum(u, 0.0) + jnp.log(1.0 + jnp.exp(-jnp.abs(u)))
    return _sigmoid(ba), -jnp.exp(al) * sp, u


def _dn_gates(ba, alog_row, dt_row, H):
    S = ba.shape[0]
    tq = _tile(S, 512, SUBLANES)
    W = H * DN_DK

    def fn(i, ba_ref, al_ref, dt_ref, be_ref, g_ref):
        bet, gg, _ = _gate_terms(ba_ref[...], al_ref[...], dt_ref[...])
        for h in range(H):
            sl = slice(h * DN_DK, (h + 1) * DN_DK)
            be_ref[:, sl] = jnp.broadcast_to(bet[:, h:h + 1], (tq, DN_DK))
            g_ref[:, sl] = jnp.broadcast_to(gg[:, H + h:H + h + 1], (tq, DN_DK))

    return _rows("dn_gates", fn, S, tq, (ba, alog_row, dt_row), [_rb(tq, LANES), _full((1, LANES)), _full((1, LANES))],
                 (_sds((S, W), F32), _sds((S, W), F32)), (_rb(tq, W), _rb(tq, W)))


def _dn_gates_bwd(ba, alog_row, dt_row, dbeta_b, dg_b, H):
    S = ba.shape[0]
    tq = _tile(S, 512, SUBLANES)
    W = H * DN_DK

    def fn(i, ba_ref, al_ref, dt_ref, db_ref, dg_ref, o_ref, dal_ref, ddt_ref):
        bet, gg, u = _gate_terms(ba_ref[...], al_ref[...], dt_ref[...])
        lane = lax.broadcasted_iota(jnp.int32, (tq, LANES), 1)
        d = jnp.zeros((tq, LANES), F32)
        for h in range(H):
            d = jnp.where(lane == h, db_ref[:, h * DN_DK:h * DN_DK + 1], d)
            d = jnp.where(lane == H + h, dg_ref[:, h * DN_DK:h * DN_DK + 1], d)
        is_a = (lane >= H) & (lane < 2 * H)
        da = jnp.where(is_a, d * (-jnp.exp(al_ref[...]) * _sigmoid(u)), 0.0)
        dlog = jnp.where(lane < H, d * bet * (1.0 - bet), da)
        o_ref[...] = jnp.concatenate([dlog, jnp.zeros((tq, BA_W - LANES), F32)], axis=1).astype(BF16)
        _acc(i, dal_ref, jnp.sum(jnp.where(is_a, d * gg, 0.0), axis=0, keepdims=True))
        _acc(i, ddt_ref, jnp.sum(da, axis=0, keepdims=True))

    f = _full((1, LANES))
    return _rows("dn_gates_bwd", fn, S, tq, (ba, alog_row, dt_row, dbeta_b, dg_b),
                 [_rb(tq, LANES), f, f, _rb(tq, W), _rb(tq, W)],
                 (_sds((S, BA_W), BF16), _sds((1, LANES), F32), _sds((1, LANES), F32)), (_rb(tq, BA_W), f, f))


def _dn_out(o, proj, ng, W, cb_z):
    S = o.shape[0]
    tq = _tile(S, 256, SUBLANES)

    def fn(i, o_ref, z_ref, g_ref, y_ref):
        for h in range(W // DN_DK):
            sl = slice(h * DN_DK, (h + 1) * DN_DK)
            z = z_ref[:, sl].astype(F32)
            y_ref[:, sl] = (_rms_fwd(o_ref[:, sl], g_ref[...]) * (z * _sigmoid(z))).astype(BF16)

    return _rows("dn_out", fn, S, tq, (o, proj, ng), [_rb(tq, W), _rb(tq, W, cb_z), _full((1, DN_DK))], _sds((S, W), BF16), _rb(tq, W))


def _dn_out_bwd(o, proj, ng, dy, W, cb_z):
    S = o.shape[0]
    tq = _tile(S, 256, SUBLANES)

    def fn(i, o_ref, z_ref, g_ref, d_ref, do_ref, dz_ref, dg_ref):
        g = g_ref[...]
        dg = jnp.zeros((1, DN_DK), F32)
        for h in range(W // DN_DK):
            sl = slice(h * DN_DK, (h + 1) * DN_DK)
            oh, z, d = o_ref[:, sl], z_ref[:, sl].astype(F32), d_ref[:, sl].astype(F32)
            sg = _sigmoid(z)
            dn = d * (z * sg)
            dz_ref[:, sl] = (d * _rms_fwd(oh, g) * (sg * (1.0 + z * (1.0 - sg)))).astype(BF16)
            dx, dgh = _rms_bwd(oh, g, dn)
            do_ref[:, sl] = dx
            dg = dg + dgh
        _acc(i, dg_ref, dg)

    r = _rb(tq, W)
    return _rows("dn_out_bwd", fn, S, tq, (o, proj, ng, dy), [r, _rb(tq, W, cb_z), _full((1, DN_DK)), r],
                 (_sds((S, W), F32), _sds((S, W), BF16), _sds((1, DN_DK), F32)), (r, r, _full((1, DN_DK))))


def _bdot(a, b, mode="nn"):
    return lax.dot_general(a.astype(BF16), b.astype(BF16), (_DIMS[mode], ((), ())), preferred_element_type=F32)


def _rsum(x):
    return jnp.broadcast_to(jnp.sum(x, axis=-1, keepdims=True), x.shape)


def _dot3(a, b, mode="nn"):
    ah, bh = a.astype(BF16), b.astype(BF16)
    al, bl = (a - ah.astype(F32)).astype(BF16), (b - bh.astype(F32)).astype(BF16)
    d = lambda x, y: lax.dot_general(x, y, (_DIMS[mode], ((), ())), preferred_element_type=F32)
    return d(ah, bh) + (d(al, bh) + d(ah, bl))


def _cumsum_rows(x, reverse=False):
    n = x.shape[0]
    row = lax.broadcasted_iota(jnp.int32, x.shape, 0)
    s = 1
    while s < n:
        if reverse:
            x = x + jnp.where(row < n - s, pltpu.roll(x, n - s, 0), 0.0)
        else:
            x = x + jnp.where(row >= s, pltpu.roll(x, s, 0), 0.0)
        s *= 2
    return x


def _each(f, *lists):
    return [f(*a) for a in zip(*lists)]


def _delta_local(qs, ks, vs, bes, grs):
    C = DN_CHUNK
    ri = lax.broadcasted_iota(jnp.int32, (C, C), 0)
    ci = lax.broadcasted_iota(jnp.int32, (C, C), 1)
    causal, strict = ri >= ci, ri > ci
    gcs = [_cumsum_rows(g) for g in grs]
    decays = [jnp.where(causal, jnp.exp(jnp.where(causal, gc[:, :C] - gc.T[:C, :], 0.0)), 0.0) for gc in gcs]
    egs = [jnp.exp(gc) for gc in gcs]
    eks = [jnp.exp(gc[C - 1:C, :] - gc) for gc in gcs]
    gams = [jnp.exp(gc[C - 1:C, :]) for gc in gcs]
    kbs = _each(lambda k, be: k * be, ks, bes)
    kks = _each(lambda kb, k: _bdot(kb, k, "nt"), kbs, ks)
    nls = _each(lambda kk, dc: jnp.where(strict, -kk * dc, 0.0), kks, decays)
    eye = (ri == ci).astype(F32)
    ts = [eye + nl for nl in nls]
    pws = [_dot3(nl, nl) for nl in nls]
    for s in range(4):
        both = _each(lambda t, pw: _dot3(jnp.concatenate([t, pw], axis=0), pw), ts, pws)
        ts = _each(lambda t, b: t + b[:C], ts, both)
        pws = [b[C:] for b in both]
    ts = _each(lambda t, pw: t + _dot3(t, pw), ts, pws)
    vbs = _each(lambda v, be: v * be, vs, bes)
    kbes = _each(lambda kb, eg: kb * eg, kbs, egs)
    uws = _each(lambda t, vb, kbe: _dot3(t, jnp.concatenate([vb, kbe], axis=1)), ts, vbs, kbes)
    us, ws = [uw[:, :DN_DK] for uw in uws], [uw[:, DN_DK:] for uw in uws]
    qks = _each(lambda q, k: _bdot(q, k, "nt"), qs, ks)
    return dict(decay=decays, eg=egs, ek=eks, gam=gams, kb=kbs, kk=kks, t=ts, vb=vbs, kbe=kbes, u=us, w=ws, qk=qks,
                a=_each(lambda qk, dc: qk * dc, qks, decays), qd=_each(lambda q, eg: q * eg, qs, egs),
                kd=_each(lambda k, ek: k * ek, ks, eks), strict=strict)


def _delta_items(refs, CB, HB):
    C, dk = DN_CHUNK, DN_DK
    return [[r[c * C:(c + 1) * C, h * dk:(h + 1) * dk] for h in range(HB) for c in range(CB)] for r in refs]


def _delta_fwd(qkv, beta_b, g_b, H, CB, HB):
    S = qkv.shape[0]
    C, dk = DN_CHUNK, DN_DK
    N = S // C
    R = CB * C
    G = H // HB

    def body(q_ref, k_ref, v_ref, b_ref, g_ref, o_ref, st_ref, s_ref):
        @pl.when(pl.program_id(1) == 0)
        def _():
            s_ref[...] = jnp.zeros((HB, dk, dk), F32)

        L = _delta_local(*_delta_items((q_ref, k_ref, v_ref, b_ref, g_ref), CB, HB))
        ss = [s_ref[h] for h in range(HB)]
        for c in range(CB):
            it = [h * CB + c for h in range(HB)]
            for h in range(HB):
                st_ref[h, c] = ss[h]
            wq = [_bdot(jnp.concatenate([L["w"][i], L["qd"][i]], axis=0), s) for i, s in zip(it, ss)]
            vns = [L["u"][i] - x[:C] for i, x in zip(it, wq)]
            outs = [x[C:] + _bdot(L["a"][i], vn) for i, x, vn in zip(it, wq, vns)]
            ss = [s * L["gam"][i] + _bdot(L["kd"][i], vn, "tn") for i, s, vn in zip(it, ss, vns)]
            for h in range(HB):
                o_ref[c * C:(c + 1) * C, h * dk:(h + 1) * dk] = outs[h]
        for h in range(HB):
            s_ref[h] = ss[h]

    blk = lambda off: pl.BlockSpec((R, HB * dk), lambda h, n: (n, off + h))
    return _pcall(body, name="delta_fwd", grid=(G, N // CB),
                  in_specs=[blk(0), blk(G), blk(2 * G), blk(0), blk(0)],
                  out_specs=(blk(0), pl.BlockSpec((HB, CB, dk, dk), lambda h, n: (h, n, 0, 0))),
                  out_shape=(_sds((S, H * dk), F32), _sds((H, N, dk, dk), F32)),
                  scratch_shapes=[pltpu.VMEM((HB, dk, dk), F32)],
                  compiler_params=_cp("arbitrary", "arbitrary"))(qkv, qkv, qkv, beta_b, g_b)


def _delta_bwd(qkv, beta_b, g_b, states, do, H, CB, HB):
    S = qkv.shape[0]
    C, dk = DN_CHUNK, DN_DK
    N = S // C
    R = CB * C
    NB = N // CB
    G = H // HB

    def body(q_ref, k_ref, v_ref, b_ref, g_ref, st_ref, do_ref, dqkv_ref, db_ref, dg_ref, ds_ref):
        @pl.when(pl.program_id(1) == 0)
        def _():
            ds_ref[...] = jnp.zeros((HB, dk, dk), F32)

        qs, ks, vs, bes, grs, dos = _delta_items((q_ref, k_ref, v_ref, b_ref, g_ref, do_ref), CB, HB)
        L = _delta_local(qs, ks, vs, bes, grs)
        ts, decays, kbs, egs, eks, gams, qds, kds = (L[n] for n in ("t", "decay", "kb", "eg", "ek", "gam", "qd", "kd"))
        s0s = [st_ref[h, c] for h in range(HB) for c in range(CB)]
        vns = _each(lambda u, w, s0: u - _bdot(w, s0), L["u"], L["w"], s0s)
        pre_dvn = _each(lambda a, d: _bdot(a, d, "tn"), L["a"], dos)
        pre_ds = _each(lambda qd, d: _bdot(qd, d, "tn"), qds, dos)
        das = _each(lambda d, vn: _bdot(d, vn, "nt"), dos, vns)
        ds = [ds_ref[h] for h in range(HB)]
        ds1s, dvns = [None] * (HB * CB), [None] * (HB * CB)
        for c in reversed(range(CB)):
            it = [h * CB + c for h in range(HB)]
            new = [pre_dvn[i] + _bdot(kds[i], d) for i, d in zip(it, ds)]
            for i, d, dv in zip(it, ds, new):
                ds1s[i], dvns[i] = d, dv
            ds = [pre_ds[i] + d * gams[i] - _bdot(L["w"][i], dv, "tn") for i, d, dv in zip(it, ds, new)]
        for h in range(HB):
            ds_ref[h] = ds[h]
        dkds = _each(lambda vn, d1: _bdot(vn, d1, "nt"), vns, ds1s)
        dgams = _each(lambda s0, d1: jnp.sum(jnp.sum(s0 * d1, axis=1, keepdims=True), axis=0, keepdims=True), s0s, ds1s)
        ost = _each(lambda d, dv, s0: _bdot(jnp.concatenate([d, dv], axis=0), s0, "nt"), dos, dvns, s0s)
        dqds, dws = [x[:C] for x in ost], [-x[C:] for x in ost]
        dvw = _each(lambda dv, dw: jnp.concatenate([dv, dw], axis=1), dvns, dws)
        tdvw = _each(lambda t, x: _dot3(t, x, "tn"), ts, dvw)
        dvbs, dkbes = [x[:, :dk] for x in tdvw], [x[:, dk:] for x in tdvw]
        dts = _each(lambda x, vb, kbe: _dot3(x, jnp.concatenate([vb, kbe], axis=1), "nt"), dvw, L["vb"], L["kbe"])
        tmp = _each(lambda dt, t: _dot3(dt, t, "nt"), dts, ts)
        dls = _each(lambda t, x: -_dot3(t, x, "tn"), ts, tmp)
        ms = _each(lambda dl, dc: jnp.where(L["strict"], dl * dc, 0.0), dls, decays)
        mas = _each(lambda da, dc: da * dc, das, decays)
        dkbs = _each(lambda m, k, dkbe, eg: _bdot(m, k) + dkbe * eg, ms, ks, dkbes, egs)
        dks = _each(lambda m, kb, ma, q, dkd, ek, dkb, be: _bdot(m, kb, "tn") + _bdot(ma, q, "tn") + dkd * ek + dkb * be,
                    ms, kbs, mas, qs, dkds, eks, dkbs, bes)
        dqs = _each(lambda ma, k, dqd, eg: _bdot(ma, k) + dqd * eg, mas, ks, dqds, egs)
        es = _each(lambda m, kk, ma, qk: m * kk + ma * qk, ms, L["kk"], mas, L["qk"])
        ones = jnp.ones((C, dk), BF16)
        row = lax.broadcasted_iota(jnp.int32, (C, dk), 0)
        for i in range(HB * CB):
            h, c = divmod(i, CB)
            rs, cs = slice(c * C, (c + 1) * C), slice(h * dk, (h + 1) * dk)
            e = es[i]
            e_hi = e.astype(BF16)
            col = _bdot(e_hi, ones, "tn") + _bdot(e - e_hi.astype(F32), ones, "tn")
            t_kd = _rsum(dkds[i] * kds[i])
            dgc = (jnp.broadcast_to(jnp.sum(e, axis=1, keepdims=True), (C, dk)) - col + _rsum(dqds[i] * qds[i]) - t_kd
                   + _rsum(dkbes[i] * L["kbe"][i]))
            dglast = jnp.sum(t_kd[:, 0:1], axis=0, keepdims=True) + dgams[i] * gams[i][:, 0:1]
            dgc = dgc + jnp.where(row == C - 1, dglast, 0.0)
            dqkv_ref[0, rs, cs] = dqs[i]
            dqkv_ref[1, rs, cs] = dks[i]
            dqkv_ref[2, rs, cs] = dvbs[i] * bes[i]
            db_ref[rs, cs] = _rsum(dkbs[i] * ks[i]) + _rsum(dvbs[i] * vs[i])
            dg_ref[rs, cs] = _cumsum_rows(dgc, reverse=True)

    blk = lambda off: pl.BlockSpec((R, HB * dk), lambda h, n: (NB - 1 - n, off + h))
    W = H * dk
    return _pcall(body, name="delta_bwd", grid=(G, NB),
                  in_specs=[blk(0), blk(G), blk(2 * G), blk(0), blk(0),
                            pl.BlockSpec((HB, CB, dk, dk), lambda h, n: (h, NB - 1 - n, 0, 0)), blk(0)],
                  out_specs=(pl.BlockSpec((3, R, HB * dk), lambda h, n: (0, NB - 1 - n, h)), blk(0), blk(0)),
                  out_shape=(_sds((3, S, W), F32), _sds((S, W), F32), _sds((S, W), F32)),
                  scratch_shapes=[pltpu.VMEM((HB, dk, dk), F32)],
                  compiler_params=_cp("arbitrary", "arbitrary"))(qkv, qkv, qkv, beta_b, g_b, states, do)


def _rope_consts():
    lane = np.arange(LANES) % SW_HD
    half = ROT_DIM // 2
    inv = (ROPE_THETA ** (-np.arange(half, dtype=np.float32) * np.float32(2.0 / ROT_DIM))).astype(np.float32)
    freq = np.where(lane < ROT_DIM, inv[lane % half], 0.0).astype(np.float32)
    lo = (lane < half).astype(np.float32)
    hi = ((lane >= half) & (lane < ROT_DIM)).astype(np.float32)
    return jnp.asarray(np.stack([freq, -lo, hi] + [np.zeros(LANES, np.float32)] * 5))


def _rope_tables(pos_col):
    S = pos_col.shape[0]
    tq = _tile(S, 1024, SUBLANES)

    def fn(i, p_ref, c_ref, cos_ref, s1_ref, s2_ref):
        ang = p_ref[...].astype(F32) * c_ref[0:1, :]
        sn = jnp.sin(ang)
        cos_ref[...] = jnp.cos(ang)
        s1_ref[...] = sn * c_ref[1:2, :]
        s2_ref[...] = sn * c_ref[2:3, :]

    o, r = _sds((S, LANES), F32), _rb(tq, LANES)
    return _rows("rope_tables", fn, S, tq, (pos_col, _rope_consts()), [_rb(tq, 1), _full((SUBLANES, LANES))], (o, o, o), (r, r, r))


def _wide(a, w):
    return a if w == LANES else jnp.tile(a, (1, w // LANES))


def _rope(x, cos, s1, s2):
    w, h = x.shape[1], ROT_DIM // 2
    return x * _wide(cos, w) + pltpu.roll(x, w - h, 1) * _wide(s1, w) + pltpu.roll(x, h, 1) * _wide(s2, w)


def _unrope(d, cos, s1, s2):
    w, h = d.shape[1], ROT_DIM // 2
    return d * _wide(cos, w) + pltpu.roll(d * _wide(s1, w), h, 1) + pltpu.roll(d * _wide(s2, w), w - h, 1)


def _swa_setup(n, q_ref, kc_ref, kp_ref, vc_ref, vp_ref, tc, tp):
    B = SW_BLOCK
    qr = _rope(q_ref[...].astype(F32), tc[0][...], tc[1][...], tc[2][...]) * (SW_HD ** -0.5)
    kw = jnp.concatenate([_rope(kp_ref[...].astype(F32), tp[0][...], tp[1][...], tp[2][...]),
                          _rope(kc_ref[...].astype(F32), tc[0][...], tc[1][...], tc[2][...])], axis=0)
    vw = jnp.concatenate([vp_ref[...], vc_ref[...]], axis=0).astype(F32)
    lane = lax.broadcasted_iota(jnp.int32, (2 * B, LANES), 1)
    heads = []
    for hk in range(SW_KV_HEADS):
        kh, vh = kw[:, hk * SW_HD:(hk + 1) * SW_HD], vw[:, hk * SW_HD:(hk + 1) * SW_HD]
        kk, vv = jnp.concatenate([kh, kh], axis=1), jnp.concatenate([vh, vh], axis=1)
        heads.append(tuple(jnp.where(sel, t, 0.0).astype(BF16) for t in (kk, vv) for sel in (lane < SW_HD, lane >= SW_HD)))
    qi = lax.broadcasted_iota(jnp.int32, (B, 2 * B), 0) + B
    ki = lax.broadcasted_iota(jnp.int32, (B, 2 * B), 1)
    off = qi - ki
    ok = (off >= 0) & (off < SW_BLOCK) & ((ki >= B) | (n > 0))
    return qr, heads, jnp.where(ok, 0.0, -1e30), lane


SWA_GROUPS = 4
SWA_GROUPS_BWD = 2


def _swa_probs(items, qs, heads, bias, sk_ref, G2):
    ss = [_bdot(qs[j], heads[j // G2][half], "nt") + bias for j, half in items]
    sks = [sk_ref[0:1, 2 * j + half:2 * j + half + 1] for j, half in items]
    ms = [jnp.maximum(jnp.max(s, axis=-1, keepdims=True), sk) for s, sk in zip(ss, sks)]
    ps = [jnp.exp(s - m) for s, m in zip(ss, ms)]
    es = [jnp.exp(sk - m) for sk, m in zip(sks, ms)]
    inv = [1.0 / (jnp.sum(p, axis=-1, keepdims=True) + e) for p, e in zip(ps, es)]
    return [p * i for p, i in zip(ps, inv)], [e * i for e, i in zip(es, inv)]


def _swa_specs(W, cb_q, cb_k):
    B = SW_BLOCK
    assert (W // LANES) % SWA_GROUPS == 0 and (W // LANES) % SWA_GROUPS_BWD == 0
    cur = lambda w, cb: pl.BlockSpec((B, w), lambda n: (n, cb))
    prv = lambda w, cb: pl.BlockSpec((B, w), lambda n: (jnp.maximum(n - 1, 0), cb))
    specs = [cur(W, cb_q), cur(LANES, cb_k), prv(LANES, cb_k), cur(LANES, cb_k + 1), prv(LANES, cb_k + 1)]
    return specs + [cur(LANES, 0)] * 3 + [prv(LANES, 0)] * 3 + [_full((1, LANES))]


def _swa_fwd(proj, tabs, sinks_row, W, cb_q, cb_k):
    S = proj.shape[0]
    G2 = SW_Q_HEADS // SW_KV_HEADS // 2

    def body(q_ref, kc_ref, kp_ref, vc_ref, vp_ref, c0, c1, c2, p0, p1, p2, sk_ref, o_ref):
        n = pl.program_id(0)
        qr, heads, bias, _ = _swa_setup(n, q_ref, kc_ref, kp_ref, vc_ref, vp_ref, (c0, c1, c2), (p0, p1, p2))
        qs = [qr[:, j * LANES:(j + 1) * LANES].astype(BF16) for j in range(W // LANES)]
        for j0 in range(0, W // LANES, SWA_GROUPS):
            items = [(j, half) for j in range(j0, j0 + SWA_GROUPS) for half in range(2)]
            probs, _ = _swa_probs(items, qs, heads, bias, sk_ref, G2)
            pv = [_bdot(p, heads[j // G2][2 + half]) for p, (j, half) in zip(probs, items)]
            for g in range(SWA_GROUPS):
                o_ref[:, (j0 + g) * LANES:(j0 + g + 1) * LANES] = (pv[2 * g] + pv[2 * g + 1]).astype(BF16)

    t = tuple(tabs)
    return _pcall(body, name="swa_fwd", grid=(S // SW_BLOCK,), in_specs=_swa_specs(W, cb_q, cb_k),
                  out_specs=pl.BlockSpec((SW_BLOCK, W), lambda n: (n, 0)), out_shape=_sds((S, W), BF16),
                  compiler_params=_cp("arbitrary"))(proj, proj, proj, proj, proj, *t, *t, sinks_row)


def _swa_bwd(proj, tabs, sinks_row, do, W, cb_q, cb_k):
    S = proj.shape[0]
    B = SW_BLOCK
    G2 = SW_Q_HEADS // SW_KV_HEADS // 2
    SKR = -(-SW_Q_HEADS // SUBLANES) * SUBLANES

    def body(q_ref, kc_ref, kp_ref, vc_ref, vp_ref, c0, c1, c2, p0, p1, p2, sk_ref, do_ref,
             dq_ref, dkc_ref, dkp_ref, dvc_ref, dvp_ref, dsk_ref):
        n = pl.program_id(0)
        qr, heads, bias, lane = _swa_setup(n, q_ref, kc_ref, kp_ref, vc_ref, vp_ref, (c0, c1, c2), (p0, p1, p2))

        @pl.when(n == 0)
        def _():
            dsk_ref[...] = jnp.zeros((SKR, LANES), F32)

        acc_k = [jnp.zeros((2 * B, LANES), F32) for _ in range(SW_KV_HEADS)]
        acc_v = [jnp.zeros((2 * B, LANES), F32) for _ in range(SW_KV_HEADS)]
        qs = [qr[:, j * LANES:(j + 1) * LANES].astype(BF16) for j in range(W // LANES)]
        dos = [do_ref[:, j * LANES:(j + 1) * LANES].astype(BF16) for j in range(W // LANES)]
        dqs = []
        for j0 in range(0, W // LANES, SWA_GROUPS_BWD):
            items = [(j, half) for j in range(j0, j0 + SWA_GROUPS_BWD) for half in range(2)]
            probs, psinks = _swa_probs(items, qs, heads, bias, sk_ref, G2)
            dps = [_bdot(dos[j], heads[j // G2][2 + half], "nt") for j, half in items]
            deltas = [jnp.sum(p * dp, axis=-1, keepdims=True) for p, dp in zip(probs, dps)]
            dss = [(p * (dp - dl)).astype(BF16) for p, dp, dl in zip(probs, dps, deltas)]
            pbs = [p.astype(BF16) for p in probs]
            dqp = [_bdot(ds, heads[j // G2][half]) for ds, (j, half) in zip(dss, items)]
            dkk = [_bdot(ds, qs[j], "tn") for ds, (j, half) in zip(dss, items)]
            dvv = [_bdot(p, dos[j], "tn") for p, (j, half) in zip(pbs, items)]
            for i, (j, half) in enumerate(items):
                hk, h = j // G2, 2 * j + half
                sel = (lane < SW_HD) if half == 0 else (lane >= SW_HD)
                acc_k[hk] = acc_k[hk] + jnp.where(sel, dkk[i], 0.0)
                acc_v[hk] = acc_v[hk] + jnp.where(sel, dvv[i], 0.0)
                dsk_ref[h:h + 1, :] += jnp.broadcast_to(-jnp.sum(psinks[i] * deltas[i], axis=0, keepdims=True), (1, LANES))
            dqs += [dqp[2 * g] + dqp[2 * g + 1] for g in range(SWA_GROUPS_BWD)]
        dq = jnp.concatenate(dqs, axis=1) * (SW_HD ** -0.5)
        dq_ref[...] = _unrope(dq, c0[...], c1[...], c2[...]).astype(BF16)
        fold = lambda a: a[:, :SW_HD] + a[:, SW_HD:]
        dkw = jnp.concatenate([fold(a) for a in acc_k], axis=1)
        dvw = jnp.concatenate([fold(a) for a in acc_v], axis=1)
        dkp_ref[...], dkc_ref[...] = dkw[:B], dkw[B:]
        dvp_ref[...], dvc_ref[...] = dvw[:B], dvw[B:]

    t = tuple(tabs)
    blk = lambda w: pl.BlockSpec((B, w), lambda n: (n, 0))
    o = _sds((S, LANES), F32)
    return _pcall(body, name="swa_bwd", grid=(S // B,), in_specs=_swa_specs(W, cb_q, cb_k) + [blk(W)],
                  out_specs=(blk(W), blk(LANES), blk(LANES), blk(LANES), blk(LANES), _full((SKR, LANES))),
                  out_shape=(_sds((S, W), BF16), o, o, o, o, _sds((SKR, LANES), F32)),
                  compiler_params=_cp("arbitrary"))(proj, proj, proj, proj, proj, *t, *t, sinks_row, do)


def _swa_kv_combine(dkc, dkp, dvc, dvp, tabs):
    S = dkc.shape[0]
    B = SW_BLOCK
    nb = S // B

    def fn(n, kc_ref, kp_ref, vc_ref, vp_ref, c0, c1, c2, o_ref):
        more = n < nb - 1
        dk = kc_ref[...] + jnp.where(more, kp_ref[...], 0.0)
        dv = vc_ref[...] + jnp.where(more, vp_ref[...], 0.0)
        o_ref[...] = jnp.concatenate([_unrope(dk, c0[...], c1[...], c2[...]), dv], axis=1).astype(BF16)

    cur = _rb(B, LANES)
    nxt = pl.BlockSpec((B, LANES), lambda n: (jnp.minimum(n + 1, nb - 1), 0))
    return _rows("swa_kv_combine", fn, S, B, (dkc, dkp, dvc, dvp, *tabs), [cur, nxt, cur, nxt, cur, cur, cur],
                 _sds((S, 2 * LANES), BF16), _rb(B, 2 * LANES))


ANY = pl.BlockSpec(memory_space=pl.ANY)


def _place():
    x, y, c = lax.axis_index("x"), lax.axis_index("y"), lax.axis_index("c")
    return x, y, c, [(1 - x, y), (x, 1 - y), (1 - x, 1 - y)]


def _comm_call(name, body, out_shapes, n_sems, n_local, *ins):
    return _pcall(body, name=name, out_shape=tuple(out_shapes), in_specs=[ANY] * len(ins), out_specs=tuple(ANY for _ in out_shapes),
                  scratch_shapes=[pltpu.SemaphoreType.DMA((n_sems,)), pltpu.SemaphoreType.DMA((n_sems,)),
                                  pltpu.SemaphoreType.DMA((n_local,))])(*ins)


def _remote(src, dst, send, recv, k, to):
    return pltpu.make_async_remote_copy(src_ref=src, dst_ref=dst, send_sem=send.at[k], recv_sem=recv.at[k], device_id=to,
                                        device_id_type=MESH)


def _gather_chips(name, arrs):
    n = len(arrs)
    Lh = arrs[0].shape[0] // 2

    def body(*refs):
        w, o, (send, recv, _) = refs[:n], refs[n:2 * n], refs[2 * n:]
        x, y, c, chips = _place()
        me, sib = 2 * x + y, (x, y, 1 - c)
        own, other = pl.ds(c * Lh, Lh), pl.ds((1 - c) * Lh, Lh)
        idx = [2 * cx + cy for cx, cy in chips]
        first = [[_remote(w[a].at[own], o[a].at[me, own], send, recv, 6 * a + j, (*chips[j], c)) for j in range(3)] for a in range(n)]
        passed = [[_remote(o[a].at[idx[j], own], o[a].at[idx[j], own], send, recv, 6 * a + 3 + j, sib) for j in range(3)] for a in range(n)]
        for cp in [cp for row in first for cp in row]:
            cp.start()
        for j in range(3):
            for a in range(n):
                _remote(w[a].at[own], o[a].at[idx[j], own], send, recv, 6 * a + j, (*chips[j], c)).wait_recv()
                passed[a][j].start()
        for j in range(3):
            for a in range(n):
                _remote(w[a].at[other], o[a].at[idx[j], other], send, recv, 6 * a + 3 + j, sib).wait_recv()
        for cp in [cp for row in first + passed for cp in row]:
            cp.wait_send()

    return _comm_call(name, body, [_sds((4,) + a.shape, a.dtype) for a in arrs], 6 * n, 1, *arrs)


def _pair_swap(name, arrs, whole=False):
    n = len(arrs)
    Lh = arrs[0].shape[0] if whole else arrs[0].shape[0] // 2

    def body(*refs):
        g, o, (send, recv, _) = refs[:n], refs[n:2 * n], refs[2 * n:]
        x, y, c, _ = _place()
        cps = [_remote(g[a] if whole else g[a].at[pl.ds((1 - c) * Lh, Lh)], o[a], send, recv, a, (x, y, 1 - c)) for a in range(n)]
        for cp in cps:
            cp.start()
        for cp in cps:
            cp.wait()

    return _comm_call(name, body, [_sds((Lh,) + a.shape[1:], a.dtype) for a in arrs], n, 1, *arrs)


def _chip_slice(ref, axis, s):
    if axis is None:
        return ref.at[s]
    q = ref.shape[axis] // 4
    start = s * q if isinstance(s, int) else pl.multiple_of(s * q, q)
    return ref.at[tuple([slice(None)] * axis + [pl.ds(start, q)])]


def _scatter_chips(name, items):
    n = len(items)
    part = lambda a, ax: a.shape[1:] if ax is None else tuple(d // 4 if i == ax else d for i, d in enumerate(a.shape))

    def body(*refs):
        p, o, (send, recv, _) = refs[:n], refs[n:2 * n], refs[2 * n:]
        x, y, c, chips = _place()
        me = 2 * x + y
        idx = [2 * cx + cy for cx, cy in chips]
        cps = [_remote(_chip_slice(p[a], items[a][1], idx[j]), o[a].at[me], send, recv, 3 * a + j, (*chips[j], c))
               for a in range(n) for j in range(3)]
        for cp in cps:
            cp.start()
        for a in range(n):
            for j in range(3):
                _remote(_chip_slice(p[a], items[a][1], me), o[a].at[idx[j]], send, recv, 3 * a + j, (*chips[j], c)).wait_recv()
        for cp in cps:
            cp.wait_send()

    return _comm_call(name, body, [_sds((4,) + part(a, ax), a.dtype) for a, ax in items], 3 * n, 1, *[a for a, _ in items])


def _own_part(a, axis, me):
    if axis is None:
        return lax.dynamic_index_in_dim(a, me, 0, keepdims=False)
    q = a.shape[axis] // 4
    return lax.dynamic_slice_in_dim(a, me * q, q, axis)


HBM = pl.BlockSpec(memory_space=pltpu.HBM)
SEM = pl.BlockSpec(memory_space=pltpu.SEMAPHORE)
EFFECT = pltpu.SideEffectType.DATAFLOW_SIDE_EFFECTING


def _split_start(name, arrs, land_shapes, plan, nc, after=None):
    n = len(arrs)
    lands = [lax.empty(s.shape, s.dtype) for s in land_shapes]
    ins = list(arrs) + lands + ([] if after is None else [after])

    def body(*refs):
        outs = refs[len(ins):]
        for k, (src, dst, _, peer) in enumerate(plan(refs[:n], refs[n:n + len(lands)])):
            pltpu.make_async_remote_copy(src_ref=src, dst_ref=dst, send_sem=outs[k], recv_sem=outs[nc + k], device_id=peer,
                                         device_id_type=MESH).start()
        outs[-1][...] = jnp.zeros((SUBLANES, LANES), F32)

    nt = n + len(lands)
    thru = [pltpu.HBM(a.shape, a.dtype) for a in list(arrs) + lands]
    outs = _pcall(body, name=name, out_shape=tuple([pltpu.SemaphoreType.DMA(())] * (2 * nc) + thru + [_sds((SUBLANES, LANES), F32)]),
                  in_specs=[HBM] * nt + [ANY] * (len(ins) - nt),
                  out_specs=tuple([SEM] * (2 * nc) + [HBM] * nt + [pl.BlockSpec(memory_space=pltpu.VMEM)]),
                  input_output_aliases={i: 2 * nc + i for i in range(nt)},
                  compiler_params=pltpu.CompilerParams(has_side_effects=EFFECT))(
        *[pltpu.with_memory_space_constraint(a, pltpu.HBM) for a in ins[:nt]], *ins[nt:])
    return dict(sems=outs[:2 * nc], arrs=outs[2 * nc:2 * nc + n], lands=outs[2 * nc + n:2 * nc + nt], token=outs[-1], plan=plan, nc=nc)


def _split_wait(name, handle, after):
    arrs, lands, sems, nc = list(handle["arrs"]), list(handle["lands"]), list(handle["sems"]), handle["nc"]
    n, nt = len(arrs), len(arrs) + len(lands)

    def body(*refs):
        sem = refs[nt:nt + 2 * nc]
        for k, (src, _, landing, peer) in enumerate(handle["plan"](refs[:n], refs[n:nt])):
            cp = pltpu.make_async_remote_copy(src_ref=src, dst_ref=landing, send_sem=sem[k], recv_sem=sem[nc + k], device_id=peer,
                                              device_id_type=MESH)
            cp.wait_send()
            cp.wait_recv()

    thru = tuple(pltpu.HBM(a.shape, a.dtype) for a in arrs + lands)
    outs = _pcall(body, name=name, out_shape=thru, in_specs=[HBM] * nt + [SEM] * (2 * nc) + [ANY], out_specs=tuple([HBM] * nt),
                  input_output_aliases={i: i for i in range(nt)},
                  compiler_params=pltpu.CompilerParams(has_side_effects=EFFECT))(*arrs, *lands, *sems, after)
    return list(outs[n:])


WHOLE = "whole"


def _plan_chips(axes):
    def plan(src, land):
        x, y, c, chips = _place()
        idx = [2 * cx + cy for cx, cy in chips]
        part = lambda a, s: src[a] if axes[a] is WHOLE else _chip_slice(src[a], axes[a], s)
        return [(part(a, idx[j]), land[a].at[2 * x + y], land[a].at[idx[j]], (*chips[j], c))
                for a in range(len(land)) for j in range(3)]
    return plan


def _plan_sibling(half):
    def plan(src, land):
        x, y, c, _ = _place()
        lh = lambda a: src[a].shape[0] // 2
        return [(src[a].at[pl.ds((1 - c) * lh(a), lh(a))] if half else src[a], land[a], land[a], (x, y, 1 - c))
                for a in range(len(land))]
    return plan


def _chips_start(name, arrs, axes, after=None):
    part = lambda a, ax: a.shape if ax is WHOLE else a.shape[1:] if ax is None else tuple(d // 4 if i == ax else d for i, d in enumerate(a.shape))
    return _split_start(name, arrs, [_sds((4,) + part(a, ax), a.dtype) for a, ax in zip(arrs, axes)], _plan_chips(axes), 3 * len(arrs), after)


def _sibling_start(name, arrs, half, after=None):
    shp = lambda a: (a.shape[0] // 2,) + a.shape[1:] if half else a.shape
    return _split_start(name, arrs, [_sds(shp(a), a.dtype) for a in arrs], _plan_sibling(half), len(arrs), after)


def _gather_all(name, b):
    R, C = b.shape
    flips = [(dx, dy, dc) for dx in (0, 1) for dy in (0, 1) for dc in (0, 1)][1:]

    def body(b_ref, o_ref, send, recv, lsem):
        x, y, c, _ = _place()
        me = 4 * x + 2 * y + c
        peers = [(x ^ dx, y ^ dy, c ^ dc) for dx, dy, dc in flips]
        mine = pltpu.make_async_copy(b_ref, o_ref.at[me], lsem.at[0])
        mine.start()
        cps = [_remote(b_ref, o_ref.at[me], send, recv, k, peer) for k, peer in enumerate(peers)]
        for cp in cps:
            cp.start()
        for k, (px, py, pc) in enumerate(peers):
            _remote(b_ref, o_ref.at[4 * px + 2 * py + pc], send, recv, k, (px, py, pc)).wait_recv()
        for cp in cps:
            cp.wait_send()
        mine.wait()

    return _comm_call(name, body, [_sds((8, R, C), b.dtype)], 7, 1, b)[0]


def _block_rows(rows, width):
    return _tile(rows, max(SUBLANES, (1 << 19) // width), SUBLANES)


def _add_half(name, g, got):
    L, A, B = g.shape
    Lh = L // 2
    tq = _block_rows(A, B)

    def body(c_ref, g_ref, r_ref, o_ref):
        o_ref[...] = (g_ref[...] + r_ref[...]).astype(BF16)

    spec = pltpu.PrefetchScalarGridSpec(
        num_scalar_prefetch=1, grid=(Lh, A // tq),
        in_specs=[pl.BlockSpec((1, tq, B), lambda l, i, c_ref: (c_ref[0] * Lh + l, i, 0)),
                  pl.BlockSpec((1, tq, B), lambda l, i, c_ref: (l, i, 0))],
        out_specs=pl.BlockSpec((1, tq, B), lambda l, i, c_ref: (l, i, 0)))
    return _pcall(body, name=name, grid_spec=spec, out_shape=_sds((Lh, A, B), BF16),
                  compiler_params=_cp("arbitrary", "arbitrary"))(lax.axis_index("c").reshape(1).astype(jnp.int32), g, got)


def _sum_slots(name, a):
    n, R, C = a.shape
    tq = _block_rows(R, n * C)

    def fn(i, a_ref, o_ref):
        t = a_ref[0].astype(F32)
        for s in range(1, n):
            t = t + a_ref[s].astype(F32)
        o_ref[...] = t

    return _rows(name, fn, R, tq, (a,), [pl.BlockSpec((n, tq, C), lambda i: (0, i, 0))], _sds((R, C), F32), _rb(tq, C))


def _adam_update(w, g, m, v):
    mn = ADAM_B1 * m + (1.0 - ADAM_B1) * g
    vn = ADAM_B2 * v + (1.0 - ADAM_B2) * (g * g)
    m_hat = mn / (1.0 - ADAM_B1 ** ADAM_STEP)
    v_hat = vn / (1.0 - ADAM_B2 ** ADAM_STEP)
    return -ADAM_LR * (m_hat / (jnp.sqrt(v_hat) + ADAM_EPS) + ADAM_WD * w), mn, vn


def _adamw(name, w, g, m, v):
    R, C = w.shape
    tq = _tile(R, 256, SUBLANES)

    def fn(i, w_ref, g_ref, m_ref, v_ref, d_ref, mo_ref, vo_ref):
        d_ref[...], mo_ref[...], vo_ref[...] = _adam_update(w_ref[...], g_ref[...], m_ref[...], v_ref[...])

    r, o = _rb(tq, C), _sds((R, C), F32)
    return _rows(name, fn, R, tq, (w, g, m, v), [r, r, r, r], (o, o, o), (r, r, r))


def _adamw_halves(name, w, mine, theirs, m, v, l0, prev=None):
    L, A, B = w.shape
    Lh = mine.shape[0]
    tq = _tile(A, 256, SUBLANES)

    def body(c_ref, w_ref, a_ref, b_ref, m_ref, v_ref, *refs):
        g_ref, d_ref, mo_ref, vo_ref = refs[-4:]
        is_mine = pl.program_id(0) // Lh == c_ref[0]
        g = jnp.where(is_mine, a_ref[...], b_ref[...])
        g_ref[...] = g
        d_ref[...], mo_ref[...], vo_ref[...] = _adam_update(w_ref[...], g, m_ref[...], v_ref[...])

    full = pl.BlockSpec((1, tq, B), lambda l, i, c_ref: (l0 + l, i, 0))
    half = pl.BlockSpec((1, tq, B), lambda l, i, c_ref: (l % Lh, i, 0))
    o = _sds((L, A, B), F32)
    prev = list(prev or ())
    spec = pltpu.PrefetchScalarGridSpec(num_scalar_prefetch=1, grid=(2 * Lh, A // tq), in_specs=[full, half, half, full, full] + [ANY] * len(prev),
                                        out_specs=(full, full, full, full))
    return _pcall(body, name=name, grid_spec=spec, out_shape=(o, o, o, o), input_output_aliases={6 + i: i for i in range(len(prev))},
                  compiler_params=_cp("arbitrary", "arbitrary"))(lax.axis_index("c").reshape(1).astype(jnp.int32), w, mine, theirs, m, v, *prev)


def _pack(arrs, width, lead=()):
    nl = len(lead)
    flat = jnp.concatenate([a.reshape(lead + (-1,)) for a in arrs], axis=nl)
    n = flat.shape[-1]
    unit = PACK_ROWS * width
    tot = -(-n // unit) * unit
    flat = jnp.pad(flat, [(0, 0)] * nl + [(0, tot - n)])
    return flat.reshape(lead + (tot // width, width))


def _unpack(buf, shapes, lead=()):
    flat = buf.reshape(lead + (-1,))
    out, off = [], 0
    for s in shapes:
        n = int(np.prod(s))
        out.append(flat[..., off:off + n].reshape(lead + tuple(s)))
        off += n
    return out


def _in_groups(W, H):
    o_sq = 4 * W + 2 * H
    o_k = o_sq + W
    o_g = o_k + 2 * KV_W
    return [(0, 4 * W), (o_sq, o_k), (o_g, o_g + 2 * W), (o_k, o_g), (4 * W, o_sq)]


def _relayout_in(shards, W, H):
    c4 = sum(hi - lo for lo, hi in _in_groups(W, H)) // 4
    parts = []
    for lo, hi in _in_groups(W, H):
        for s in range(4):
            a, b = max(lo, s * c4), min(hi, (s + 1) * c4)
            if a < b:
                parts.append(shards[s][:, a - s * c4:b - s * c4])
    parts.append(jnp.zeros((shards.shape[1], BA_W - 2 * H), shards.dtype))
    return jnp.concatenate(parts, axis=1)


def _shard_in(d, W, H):
    groups = _in_groups(W, H)
    starts = [sum(hi - lo for lo, hi in groups[:i]) for i in range(len(groups))]
    stored = sorted(zip(groups, starts))
    c4 = sum(hi - lo for lo, hi in groups) // 4
    out = []
    for s in range(4):
        parts = []
        for (lo, hi), at in stored:
            a, b = max(lo, s * c4), min(hi, (s + 1) * c4)
            if a < b:
                parts.append(d[:, :, at + a - lo:at + b - lo])
        out.append(jnp.concatenate(parts, axis=2))
    return jnp.stack(out)


def _lane_row(vals, at):
    return jnp.pad(vals, (at, LANES - at - vals.shape[0]))[None]


def _layer_fwd(x, lw, tabs, W, H):
    D = x.shape[1]
    cbk = 7 * W // LANES
    h = _pre_norm(x, lw["g1"])
    proj = _mm("mm_in", h, lw["win"], "nn", BF16, tn=768)
    ba = _mm("mm_ba", h, lw["win"][:, 7 * W + 2 * KV_W:], "nn", F32)
    qkv = _dn_prep(proj, lw["conv"], W)
    beta_b, g_b = _dn_gates(ba, lw["alog"], lw["dt"], H)
    o, st = _delta_fwd(qkv, beta_b, g_b, H, DELTA_CB, DELTA_HB)
    oa = _dn_out(o, proj, lw["ng"], W, 3)
    ob = _swa_fwd(proj, tabs, lw["sinks"], W, 4, cbk)
    ya = _mm("mm_up_dn", oa, lw["wup_dn"], "nn", BF16)
    yb = _mm("mm_up_sw", ob, lw["wup_sw"], "nn", BF16)
    mixin = _mix(proj, ya, yb, D, 5)
    mix = _mm("mm_o", mixin, lw["wo"], "nn", F32)
    x1, h2 = _post_mix(x, mix, lw["g2"], lw["g3"])
    f1, act = _mm("mm_ff1", h2, lw["wff1"], "nn", out_dtypes=(BF16, BF16), epi=lambda acc: (acc, jnp.square(jnp.maximum(acc, 0.0))))
    ff = _mm("mm_ff2", act, lw["wff2"], "nn", F32)
    x2 = _post_mlp(x1, ff, lw["g4"])
    saved = dict(x=x, h=h, proj=proj, ba=ba, qkv=qkv, beta_b=beta_b, g_b=g_b, o=o, st=st, oa=oa, ob=ob, ya=ya, yb=yb,
                 mixin=mixin, mix=mix, x1=x1, h2=h2, f1=f1, act=act, ff=ff)
    return x2, saved


def _layer_bwd(dx2, lw, sv, tabs, W, H, l, big):
    D = dx2.shape[1]
    cbk = 7 * W // LANES
    big = dict(big)
    dff, dg4 = _post_mlp_bwd(sv["ff"], lw["g4"], dx2)
    df1 = _mm("mm_ff2_dx", dff, lw["wff2"], "nt", BF16, extras=(sv["f1"],),
              epi=lambda acc, f1: (acc * 2.0 * jnp.maximum(f1.astype(F32), 0.0),))
    big["w_ff2"] = _mm("mm_ff2_dw", sv["act"], dff, "tn", slab=(big["w_ff2"], l))
    dh2 = _mm("mm_ff1_dx", df1, lw["wff1"], "nt", F32)
    big["w_ff1"] = _mm("mm_ff1_dw", sv["h2"], df1, "tn", slab=(big["w_ff1"], l))
    dx1, dmix, dg3, dg2 = _mid_bwd(sv["x1"], lw["g3"], dh2, dx2, sv["mix"], lw["g2"])
    dmixin = _mm("mm_o_dx", dmix, lw["wo"], "nt", BF16)
    big["w_o"] = _mm("mm_o_dw", sv["mixin"], dmix, "tn", slab=(big["w_o"], l))
    dya, dyb, dga, dgb = _mix_bwd(sv["proj"], sv["ya"], sv["yb"], dmixin, D, 5)
    doa = _mm("mm_up_dn_dx", dya, lw["wup_dn"], "nt", BF16)
    big["w_up_dn"] = _mm("mm_up_dn_dw", sv["oa"], dya, "tn", slab=(big["w_up_dn"], l))
    dob = _mm("mm_up_sw_dx", dyb, lw["wup_sw"], "nt", BF16)
    big["w_up_sw"] = _mm("mm_up_sw_dw", sv["ob"], dyb, "tn", slab=(big["w_up_sw"], l))
    do, dz, dng = _dn_out_bwd(sv["o"], sv["proj"], lw["ng"], doa, W, 3)
    dqkvn, dbeta_b, dg_b = _delta_bwd(sv["qkv"], sv["beta_b"], sv["g_b"], sv["st"], do, H, DELTA_CB, DELTA_HB)
    dba, dalog, ddt = _dn_gates_bwd(sv["ba"], lw["alog"], lw["dt"], dbeta_b, dg_b, H)
    dc, dconv = _dn_prep_bwd_a(sv["proj"], lw["conv"], dqkvn, W)
    dqkv = _dn_prep_bwd_b(dc, lw["conv"], W)
    dq_sw, dkc, dkp, dvc, dvp, dsk = _swa_bwd(sv["proj"], tabs, lw["sinks"], dob, W, 4, cbk)
    dkv = _swa_kv_combine(dkc, dkp, dvc, dvp, tabs)
    dproj = jnp.concatenate([dqkv, dz, dq_sw, dga, dgb, dkv, dba], axis=1)
    dh = _mm("mm_in_dx", dproj, lw["win"], "nt", F32, tk=768)
    big["w_in"] = _mm("mm_in_dw", sv["h"], dproj, "tn", tn=768, slab=(big["w_in"], l))
    dx, dg1 = _pre_norm_bwd(sv["x"], lw["g1"], dh, dx1)
    grads = dict(pre_mix_g=dg1[0], dn_conv_w=dconv, dn_a_log=dalog[0, H:2 * H], dn_dt_bias=ddt[0, H:2 * H], dn_norm_g=dng[0],
                 sw_sinks=dsk[:SW_Q_HEADS, 0], post_mix_g=dg2[0], pre_mlp_g=dg3[0], post_mlp_g=dg4[0])
    return dx, grads, big


_WEIGHTS = ["pre_mix_g", "w_in", "dn_conv_w", "dn_a_log", "dn_dt_bias", "dn_norm_g", "sw_sinks", "w_up_dn", "w_up_sw", "w_o",
            "post_mix_g", "pre_mlp_g", "w_ff1", "w_ff2", "post_mlp_g"]
_BIG = {"w_in": 2, "w_up_dn": 1, "w_up_sw": 1, "w_o": 1, "w_ff1": 2, "w_ff2": 1}
_SMALL = [n for n in _WEIGHTS if n not in _BIG]


def _step(P):
    x, target = P["x"][0], P["loss_target"][0]
    S, D = x.shape
    L = P["pre_mix_g"].shape[0]
    H, W = DN_HEADS, DN_HEADS * DN_DK
    assert W == D == SW_Q_HEADS * SW_HD and KV_W == LANES
    me = 2 * lax.axis_index("x") + lax.axis_index("y")

    assert L % 4 == 0
    names = list(_BIG) + ["dn_conv_w"]
    local = [P[n].astype(BF16) for n in _BIG] + [P["dn_conv_w"]]
    first, rest = [a[:1] for a in local], [a[1:] for a in local]
    own_slot = lambda gathered, mine: [lax.dynamic_update_slice_in_dim(g, w[None], me, 0) for g, w in zip(gathered, mine)]
    whole = [WHOLE] * len(local)
    h_first = _chips_start("weights_first_start", first, whole)
    full = dict(zip(names, own_slot(_split_wait("weights_first_wait", h_first, x), first)))
    h_rest = _chips_start("weights_rest_start", rest, whole, after=full["w_in"])

    def layer_weights(full, l, k):
        rows = lambda n: full[n][:, k].reshape(-1, full[n].shape[-1])
        cols = lambda n: jnp.concatenate([full[n][s, k] for s in range(4)], axis=-1)
        return dict(
            g1=P["pre_mix_g"][l][None], win=_relayout_in(full["w_in"][:, k], W, H), conv=cols("dn_conv_w"),
            alog=_lane_row(P["dn_a_log"][l], H), dt=_lane_row(P["dn_dt_bias"][l], H), ng=P["dn_norm_g"][l][None],
            sinks=_lane_row(P["sw_sinks"][l], 0), wup_dn=rows("w_up_dn"), wup_sw=rows("w_up_sw"), wo=rows("w_o"),
            g2=P["post_mix_g"][l][None], g3=P["pre_mlp_g"][l][None], wff1=cols("w_ff1"), wff2=rows("w_ff2"),
            g4=P["post_mlp_g"][l][None])

    tabs = _rope_tables(P["positions"].reshape(S, 1))
    lws = [layer_weights(full, 0, 0)]
    lws[0]["g1"] = lws[0]["g1"] + h_rest["token"][0, 0]

    saved = []
    for l in range(L):
        if l == 1:
            late = dict(zip(names, own_slot(_split_wait("weights_rest_wait", h_rest, x), rest)))
            lws += [layer_weights(late, k + 1, k) for k in range(L - 1)]
        x, sv = _layer_fwd(x, lws[l], tabs, W, H)
        saved.append(sv)
    loss_row, dx = _loss_head(x, target)

    Lb = L // 2
    layer_grads = [None] * L
    F = 4 * P["w_ff1"].shape[2]
    per_layer = dict(w_in=(D, 7 * W + 2 * KV_W + BA_W), w_up_dn=(W, D), w_up_sw=(W, D), w_o=(D, D), w_ff1=(D, F), w_ff2=(F, D))
    batch = [{n: lax.empty((Lb,) + per_layer[n], F32) for n in _BIG} for _ in range(2)]
    axes = [None if n == "w_in" else ax for n, ax in _BIG.items()]

    def pair_sums(tag, g, got):
        part = {n: _add_half("grad_pair_add_%s_%s" % (tag, n), g[n], r) for n, r in zip(_BIG, got)}
        return [_shard_in(part[n], W, H) if n == "w_in" else part[n] for n in _BIG]

    def chip_sums(tag, parts, slots):
        halves = []
        for n, s, a, ax in zip(_BIG, slots, parts, axes):
            s = lax.dynamic_update_slice_in_dim(s, _own_part(a, ax, me)[None], me, 0)
            halves.append(_sum_slots("grad_chip_sum_%s_%s" % (tag, n), s.reshape(4, -1, s.shape[-1])).reshape(s.shape[1:]))
        return halves

    def shared(tag, halves):
        h = _sibling_start("grad_share_start_" + tag, halves, False)
        return _split_wait("grad_share_wait_" + tag, h, halves[0])

    for l in reversed(range(L)):
        dx, layer_grads[l], batch[l // Lb] = _layer_bwd(dx, lws[l], saved[l], tabs, W, H, l % Lb, batch[l // Lb])
        if l == Lb:
            h_swap = _sibling_start("grad_swap_start_hi", [batch[1][n] for n in _BIG], True)
            lws[l - 1]["g4"] = lws[l - 1]["g4"] + h_swap["token"][0, 0]
        if l == Lb - 1:
            parts_hi = pair_sums("hi", batch[1], _split_wait("grad_swap_wait_hi", h_swap, dx))
            h_scat_hi = _chips_start("grad_scatter_start_hi", parts_hi, axes)
            if l > 0:
                lws[l - 1]["g4"] = lws[l - 1]["g4"] + h_scat_hi["token"][0, 0]
    h_swap = _sibling_start("grad_swap_start_lo", [batch[0][n] for n in _BIG], True)
    parts_lo = pair_sums("lo", batch[0], _split_wait("grad_swap_wait_lo", h_swap, dx))
    h_scat_lo = _chips_start("grad_scatter_start_lo", parts_lo, axes)
    halves = chip_sums("hi", parts_hi, _split_wait("grad_scatter_wait_hi", h_scat_hi, h_scat_lo["token"]))
    upper = {n: _adamw_halves("adamw_hi_" + n, P[n], mine, their, P["m_" + n], P["v_" + n], Lb)
             for n, mine, their in zip(_BIG, halves, shared("hi", halves))}
    halves = chip_sums("lo", parts_lo, _split_wait("grad_scatter_wait_lo", h_scat_lo, upper["w_in"][0]))
    gsum, delta, new_m, new_v = {}, {}, {}, {}
    for n, mine, their in zip(_BIG, halves, shared("lo", halves)):
        gsum[n], delta[n], new_m[n], new_v[n] = _adamw_halves("adamw_lo_" + n, P[n], mine, their, P["m_" + n], P["v_" + n], 0, upper[n])
    grads = {n: jnp.stack([layer_grads[l][n] for l in range(L)]) for n in _SMALL}
    small_shapes = [(1,)] + [grads[n].shape for n in _SMALL]
    tot = _sum_slots("small_sum", _gather_all("small_gather", _pack([loss_row[0, :1]] + [grads[n] for n in _SMALL], LANES)))
    small = _unpack(tot, small_shapes)
    loss = small[0][0]
    gsum.update(zip(_SMALL, small[1:]))
    cw = P["dn_conv_w"].shape[2]
    gsum["dn_conv_w"] = lax.dynamic_slice_in_dim(gsum["dn_conv_w"], me * cw, cw, axis=2)

    sm_shapes = [P[n].shape for n in _SMALL]
    outs = _adamw("adamw_small", *(_pack([src[pre + n] for n in _SMALL], LANES)
                                   for src, pre in ((P, ""), (gsum, ""), (P, "m_"), (P, "v_"))))
    for d, o in zip((delta, new_m, new_v), outs):
        d.update(zip(_SMALL, _unpack(o, sm_shapes)))

    return (loss, dx[None], *[gsum[n] for n in _WEIGHTS], *[delta[n] for n in _WEIGHTS],
            *[new_m[n] for n in _WEIGHTS], *[new_v[n] for n in _WEIGHTS])


def kernel(x, positions, pre_mix_g, w_in, dn_conv_w, dn_a_log, dn_dt_bias, dn_norm_g, sw_sinks, w_up_dn, w_up_sw, w_o, post_mix_g, pre_mlp_g, w_ff1, w_ff2, post_mlp_g, loss_target, m_pre_mix_g, m_w_in, m_dn_conv_w, m_dn_a_log, m_dn_dt_bias, m_dn_norm_g, m_sw_sinks, m_w_up_dn, m_w_up_sw, m_w_o, m_post_mix_g, m_pre_mlp_g, m_w_ff1, m_w_ff2, m_post_mlp_g, v_pre_mix_g, v_w_in, v_dn_conv_w, v_dn_a_log, v_dn_dt_bias, v_dn_norm_g, v_sw_sinks, v_w_up_dn, v_w_up_sw, v_w_o, v_post_mix_g, v_pre_mlp_g, v_w_ff1, v_w_ff2, v_post_mlp_g):
    vals = (x, positions, pre_mix_g, w_in, dn_conv_w, dn_a_log, dn_dt_bias, dn_norm_g, sw_sinks, w_up_dn, w_up_sw, w_o, post_mix_g, pre_mlp_g, w_ff1, w_ff2, post_mlp_g, loss_target, m_pre_mix_g, m_w_in, m_dn_conv_w, m_dn_a_log, m_dn_dt_bias, m_dn_norm_g, m_sw_sinks, m_w_up_dn, m_w_up_sw, m_w_o, m_post_mix_g, m_pre_mlp_g, m_w_ff1, m_w_ff2, m_post_mlp_g, v_pre_mix_g, v_w_in, v_dn_conv_w, v_dn_a_log, v_dn_dt_bias, v_dn_norm_g, v_sw_sinks, v_w_up_dn, v_w_up_sw, v_w_o, v_post_mix_g, v_pre_mlp_g, v_w_ff1, v_w_ff2, v_post_mlp_g)
    names = ["x", "positions"] + _WEIGHTS + ["loss_target"] + ["m_" + n for n in _WEIGHTS] + ["v_" + n for n in _WEIGHTS]
    return _step(dict(zip(names, vals)))
```

```python
import functools

import numpy as np
import jax
import jax.numpy as jnp
from jax import lax
from jax.experimental import pallas as pl
from jax.experimental.pallas import tpu as pltpu

F32, BF16 = jnp.float32, jnp.bfloat16
MESH = pl.DeviceIdType.MESH

DN_HEADS = 8
DN_DK = 128
DN_CONV = 4
DN_CHUNK = 64
SW_Q_HEADS = 16
SW_KV_HEADS = 2
SW_HD = 64
SW_BLOCK = 128
ROPE_THETA = 500000.0
ROT_DIM = SW_HD // 4
EPS = 1e-6
ADAM_LR, ADAM_B1, ADAM_B2, ADAM_EPS, ADAM_WD, ADAM_STEP = 0.001, 0.9, 0.999, 1e-08, 0.01, 10

LANES = 128
SUBLANES = 8
VMEM_LIMIT = 48 * 1024 * 1024
KV_W = SW_KV_HEADS * SW_HD
BA_W = 256
PACK_ROWS = 512
DELTA_CB = 4
DELTA_HB = 8


def _pcall(body, **kw):
    return pl.pallas_call(body, **kw)


def _cp(*sem):
    return pltpu.CompilerParams(dimension_semantics=sem, vmem_limit_bytes=VMEM_LIMIT)


def _tile(n, pref, unit=LANES):
    if n <= pref:
        return n
    t = (pref // unit) * unit
    while t > unit and n % t:
        t -= unit
    assert n % t == 0, (n, pref)
    return t


def _sds(shape, dtype):
    return jax.ShapeDtypeStruct(tuple(shape), dtype)


_DIMS = {"nn": ((1,), (0,)), "nt": ((1,), (1,)), "tn": ((0,), (0,))}


def _mm(name, a, b, mode, out_dtype=F32, tm=1024, tn=1024, tk=1024, extras=(), epi=None, out_dtypes=None, slab=None):
    if mode == "nn":
        (M, K), (_, N) = a.shape, b.shape
    elif mode == "nt":
        (M, K), (N, _) = a.shape, b.shape
    else:
        (K, M), (_, N) = a.shape, b.shape
    tm, tn, tk = _tile(M, tm), _tile(N, tn), _tile(K, tk)
    nk = K // tk
    a_spec = {"nn": pl.BlockSpec((tm, tk), lambda i, j, k: (i, k)),
              "nt": pl.BlockSpec((tm, tk), lambda i, j, k: (i, k)),
              "tn": pl.BlockSpec((tk, tm), lambda i, j, k: (k, i))}[mode]
    b_spec = {"nn": pl.BlockSpec((tk, tn), lambda i, j, k: (k, j)),
              "nt": pl.BlockSpec((tn, tk), lambda i, j, k: (j, k)),
              "tn": pl.BlockSpec((tk, tn), lambda i, j, k: (k, j))}[mode]
    dims = (_DIMS[mode], ((), ()))
    out_dtypes = tuple(out_dtypes or (out_dtype,))
    ne, no = len(extras), len(out_dtypes)
    o_spec = pl.BlockSpec((tm, tn), lambda i, j, k: (i, j))

    def body(*refs):
        a_ref, b_ref, ex = refs[0], refs[1], refs[2:2 + ne]
        outs = refs[-no:] if nk == 1 else refs[-1 - no:-1]
        part = lax.dot_general(a_ref[...], b_ref[...], dims, preferred_element_type=F32)

        def finish(acc):
            res = epi(acc, *[e[...] for e in ex]) if epi else (acc,)
            for o, r, dt in zip(outs, res, out_dtypes):
                if slab is None:
                    o[...] = r.astype(dt)
                else:
                    o[0] = r.astype(dt)

        if nk == 1:
            finish(part)
            return
        acc_ref, k = refs[-1], pl.program_id(2)

        @pl.when(k == 0)
        def _():
            acc_ref[...] = part

        @pl.when((k > 0) & (k < nk - 1))
        def _():
            acc_ref[...] += part

        @pl.when(k == nk - 1)
        def _():
            finish(acc_ref[...] + part)

    kw = dict(name=name, grid=(M // tm, N // tn, nk), scratch_shapes=[] if nk == 1 else [pltpu.VMEM((tm, tn), F32)],
              compiler_params=_cp("parallel", "parallel", "arbitrary"))
    if slab is not None:
        buf, l = slab
        return _pcall(body, in_specs=[a_spec, b_spec, ANY], out_specs=pl.BlockSpec((1, tm, tn), lambda i, j, k: (l, i, j)),
                      out_shape=_sds(buf.shape, buf.dtype), input_output_aliases={2: 0}, **kw)(a, b, buf)
    out = _pcall(body, in_specs=[a_spec, b_spec] + [o_spec] * ne, out_specs=tuple(o_spec for _ in out_dtypes),
                 out_shape=tuple(_sds((M, N), dt) for dt in out_dtypes), **kw)(a, b, *extras)
    return out if no > 1 else out[0]


def _rows(name, fn, n_rows, tq, ins, in_specs, out_shapes, out_specs):
    def body(*refs):
        fn(pl.program_id(0), *refs)

    return _pcall(body, name=name, grid=(n_rows // tq,), in_specs=in_specs, out_specs=out_specs,
                  out_shape=out_shapes, compiler_params=_cp("arbitrary"))(*ins)


def _rb(tq, w, cb=0):
    return pl.BlockSpec((tq, w), lambda i: (i, cb))


def _full(shape):
    return pl.BlockSpec(tuple(shape), lambda *_: (0,) * len(shape))


def _rms_fwd(x, g):
    r = lax.rsqrt(jnp.mean(x * x, axis=-1, keepdims=True) + EPS)
    return x * r * g


def _rms_bwd(x, g, dy):
    r = lax.rsqrt(jnp.mean(x * x, axis=-1, keepdims=True) + EPS)
    xh = x * r
    t = dy * g
    dx = r * (t - xh * jnp.mean(t * xh, axis=-1, keepdims=True))
    return dx, jnp.sum(dy * xh, axis=0, keepdims=True)


def _acc(i, ref, val):
    @pl.when(i == 0)
    def _():
        ref[...] = val

    @pl.when(i > 0)
    def _():
        ref[...] += val


def _sigmoid(x):
    return 0.5 * jnp.tanh(0.5 * x) + 0.5


def _pre_norm(x, g):
    S, D = x.shape
    tq = _tile(S, 512, SUBLANES)

    def fn(i, x_ref, g_ref, h_ref):
        h_ref[...] = _rms_fwd(x_ref[...], g_ref[...]).astype(BF16)

    return _rows("pre_norm", fn, S, tq, (x, g), [_rb(tq, D), _full((1, D))], _sds((S, D), BF16), _rb(tq, D))


def _post_mix(x, mix, g2, g3):
    S, D = x.shape
    tq = _tile(S, 512, SUBLANES)

    def fn(i, x_ref, m_ref, g2_ref, g3_ref, x1_ref, h2_ref):
        x1 = x_ref[...] + _rms_fwd(m_ref[...], g2_ref[...])
        x1_ref[...] = x1
        h2_ref[...] = _rms_fwd(x1, g3_ref[...]).astype(BF16)

    return _rows("post_mix", fn, S, tq, (x, mix, g2, g3), [_rb(tq, D), _rb(tq, D), _full((1, D)), _full((1, D))],
                 (_sds((S, D), F32), _sds((S, D), BF16)), (_rb(tq, D), _rb(tq, D)))


def _post_mlp(x1, ff, g4):
    S, D = x1.shape
    tq = _tile(S, 512, SUBLANES)

    def fn(i, x_ref, f_ref, g_ref, o_ref):
        o_ref[...] = x_ref[...] + _rms_fwd(f_ref[...], g_ref[...])

    return _rows("post_mlp", fn, S, tq, (x1, ff, g4), [_rb(tq, D), _rb(tq, D), _full((1, D))], _sds((S, D), F32), _rb(tq, D))


def _loss_head(y, target):
    S, D = y.shape
    tq = _tile(S, 512, SUBLANES)

    def fn(i, y_ref, t_ref, l_ref, d_ref):
        e = y_ref[...] - t_ref[...]
        d_ref[...] = e * (1.0 / D)
        part = jnp.sum(jnp.sum(e * e, axis=1, keepdims=True), axis=0, keepdims=True) * (0.5 / D)
        _acc(i, l_ref, jnp.broadcast_to(part, (1, LANES)))

    return _rows("loss_head", fn, S, tq, (y, target), [_rb(tq, D), _rb(tq, D)],
                 (_sds((1, LANES), F32), _sds((S, D), F32)), (_full((1, LANES)), _rb(tq, D)))


def _post_mlp_bwd(ff, g4, dx2):
    S, D = ff.shape
    tq = _tile(S, 512, SUBLANES)

    def fn(i, f_ref, g_ref, d_ref, o_ref, dg_ref):
        dx, dg = _rms_bwd(f_ref[...], g_ref[...], d_ref[...])
        o_ref[...] = dx.astype(BF16)
        _acc(i, dg_ref, dg)

    return _rows("post_mlp_bwd", fn, S, tq, (ff, g4, dx2), [_rb(tq, D), _full((1, D)), _rb(tq, D)],
                 (_sds((S, D), BF16), _sds((1, D), F32)), (_rb(tq, D), _full((1, D))))


def _mid_bwd(x1, g3, dh2, dx2, mix, g2):
    S, D = x1.shape
    tq = _tile(S, 256, SUBLANES)

    def fn(i, x_ref, g3_ref, dh_ref, dx2_ref, m_ref, g2_ref, dx1_ref, dm_ref, dg3_ref, dg2_ref):
        d, dg3 = _rms_bwd(x_ref[...], g3_ref[...], dh_ref[...])
        dx1 = dx2_ref[...] + d
        dx1_ref[...] = dx1
        dm, dg2 = _rms_bwd(m_ref[...], g2_ref[...], dx1)
        dm_ref[...] = dm.astype(BF16)
        _acc(i, dg3_ref, dg3)
        _acc(i, dg2_ref, dg2)

    r, f = _rb(tq, D), _full((1, D))
    return _rows("mid_bwd", fn, S, tq, (x1, g3, dh2, dx2, mix, g2), [r, f, r, r, r, f],
                 (_sds((S, D), F32), _sds((S, D), BF16), _sds((1, D), F32), _sds((1, D), F32)), (r, r, f, f))


def _pre_norm_bwd(x, g1, dh, dx1):
    S, D = x.shape
    tq = _tile(S, 512, SUBLANES)

    def fn(i, x_ref, g_ref, dh_ref, dx1_ref, dx_ref, dg_ref):
        d, dg = _rms_bwd(x_ref[...], g_ref[...], dh_ref[...])
        dx_ref[...] = dx1_ref[...] + d
        _acc(i, dg_ref, dg)

    r, f = _rb(tq, D), _full((1, D))
    return _rows("pre_norm_bwd", fn, S, tq, (x, g1, dh, dx1), [r, f, r, r], (_sds((S, D), F32), _sds((1, D), F32)), (r, f))


def _mix(proj, ya, yb, D, cb_a):
    S = ya.shape[0]
    tq = _tile(S, 256, SUBLANES)

    def fn(i, ga_ref, gb_ref, ya_ref, yb_ref, o_ref):
        ga, gb, ya, yb = (r[...].astype(F32) for r in (ga_ref, gb_ref, ya_ref, yb_ref))
        o_ref[...] = (_sigmoid(ga) * ya + _sigmoid(gb) * yb).astype(BF16)

    return _rows("mix", fn, S, tq, (proj, proj, ya, yb), [_rb(tq, D, cb_a), _rb(tq, D, cb_a + 1), _rb(tq, D), _rb(tq, D)],
                 _sds((S, D), BF16), _rb(tq, D))


def _mix_bwd(proj, ya, yb, dmixin, D, cb_a):
    S = ya.shape[0]
    tq = _tile(S, 256, SUBLANES)

    def fn(i, ga_ref, gb_ref, ya_ref, yb_ref, d_ref, dya_ref, dyb_ref, dga_ref, dgb_ref):
        ga, gb, ya, yb, d = (r[...].astype(F32) for r in (ga_ref, gb_ref, ya_ref, yb_ref, d_ref))
        sa, sb = _sigmoid(ga), _sigmoid(gb)
        dya_ref[...] = (d * sa).astype(BF16)
        dyb_ref[...] = (d * sb).astype(BF16)
        dga_ref[...] = (d * ya * sa * (1.0 - sa)).astype(BF16)
        dgb_ref[...] = (d * yb * sb * (1.0 - sb)).astype(BF16)

    r = _rb(tq, D)
    o = _sds((S, D), BF16)
    return _rows("mix_bwd", fn, S, tq, (proj, proj, ya, yb, dmixin), [_rb(tq, D, cb_a), _rb(tq, D, cb_a + 1), r, r, r],
                 (o, o, o, o), (r, r, r, r))


HALO = 16


def _shift_down(xe, k, tq):
    return pltpu.roll(xe, k, 0)[HALO:HALO + tq]


def _conv_pre(cur_ref, halo_ref, w_ref, i, tq):
    x = cur_ref[...].astype(F32)
    halo = jnp.where(i > 0, halo_ref[...].astype(F32), 0.0)
    xe = jnp.concatenate([halo, x], axis=0)
    xs = [x] + [_shift_down(xe, k, tq) for k in range(1, DN_CONV)]
    w = w_ref[...]
    c = sum(w[DN_CONV - 1 - k:DN_CONV - k, :] * xs[k] for k in range(DN_CONV))
    return c, xs


def _dn_prep(proj, conv_w, W):
    S = proj.shape[0]
    tq = _tile(S, 256, HALO)
    hb = tq // HALO

    def body(cur_ref, halo_ref, w_ref, o_ref):
        j, i = pl.program_id(0), pl.program_id(1)
        c, _ = _conv_pre(cur_ref, halo_ref, w_ref, i, tq)
        y = c * _sigmoid(c)
        scale = jnp.where(j == 0, DN_DK ** -0.5, 1.0)
        for h in range(W // DN_DK):
            sl = slice(h * DN_DK, (h + 1) * DN_DK)
            yh = y[:, sl]
            rs = lax.rsqrt(jnp.sum(yh * yh, axis=-1, keepdims=True) + EPS)
            o_ref[:, sl] = jnp.where(j == 2, yh, yh * rs * scale)

    return _pcall(body, name="dn_prep", grid=(3, S // tq),
                  in_specs=[pl.BlockSpec((tq, W), lambda j, i: (i, j)),
                            pl.BlockSpec((HALO, W), lambda j, i: (jnp.maximum(i * hb - 1, 0), j)),
                            pl.BlockSpec((DN_CONV, W), lambda j, i: (0, j))],
                  out_specs=pl.BlockSpec((tq, W), lambda j, i: (i, j)), out_shape=_sds((S, 3 * W), F32),
                  compiler_params=_cp("arbitrary", "arbitrary"))(proj, proj, conv_w)


def _dn_prep_bwd_a(proj, conv_w, dqkv, W):
    S = proj.shape[0]
    tq = _tile(S, 256, HALO)
    hb = tq // HALO

    def body(cur_ref, halo_ref, w_ref, d_ref, dc_ref, dw_ref):
        j, i = pl.program_id(0), pl.program_id(1)
        c, xs = _conv_pre(cur_ref, halo_ref, w_ref, i, tq)
        sg = _sigmoid(c)
        y = c * sg
        scale = jnp.where(j == 0, DN_DK ** -0.5, 1.0)
        dout = d_ref[0]
        dys = []
        for h in range(W // DN_DK):
            sl = slice(h * DN_DK, (h + 1) * DN_DK)
            yh, dh = y[:, sl], dout[:, sl]
            rs = lax.rsqrt(jnp.sum(yh * yh, axis=-1, keepdims=True) + EPS)
            yn = yh * rs
            dn = scale * rs * (dh - yn * jnp.sum(dh * yn, axis=-1, keepdims=True))
            dys.append(jnp.where(j == 2, dh, dn))
        dy = jnp.concatenate(dys, axis=1)
        dc = dy * (sg * (1.0 + c * (1.0 - sg)))
        dc_ref[...] = dc
        dw = jnp.concatenate([jnp.sum(dc * xs[DN_CONV - 1 - r], axis=0, keepdims=True) for r in range(DN_CONV)], axis=0)
        _acc(i, dw_ref, dw)

    return _pcall(body, name="dn_prep_bwd_a", grid=(3, S // tq),
                  in_specs=[pl.BlockSpec((tq, W), lambda j, i: (i, j)),
                            pl.BlockSpec((HALO, W), lambda j, i: (jnp.maximum(i * hb - 1, 0), j)),
                            pl.BlockSpec((DN_CONV, W), lambda j, i: (0, j)),
                            pl.BlockSpec((1, tq, W), lambda j, i: (j, i, 0))],
                  out_specs=(pl.BlockSpec((tq, W), lambda j, i: (i, j)), pl.BlockSpec((DN_CONV, W), lambda j, i: (0, j))),
                  out_shape=(_sds((S, 3 * W), F32), _sds((DN_CONV, 3 * W), F32)),
                  compiler_params=_cp("arbitrary", "arbitrary"))(proj, proj, conv_w, dqkv)


def _dn_prep_bwd_b(dc, conv_w, W):
    S = dc.shape[0]
    tq = _tile(S, 256, SUBLANES)
    hb = tq // SUBLANES
    nblk = S // tq

    def body(cur_ref, nxt_ref, w_ref, o_ref):
        i = pl.program_id(1)
        d = cur_ref[...]
        nxt = jnp.where(i < nblk - 1, nxt_ref[...], 0.0)
        de = jnp.concatenate([d, nxt], axis=0)
        w = w_ref[...]
        out = w[DN_CONV - 1:DN_CONV, :] * d
        for k in range(1, DN_CONV):
            out = out + w[DN_CONV - 1 - k:DN_CONV - k, :] * pltpu.roll(de, tq + SUBLANES - k, 0)[0:tq]
        o_ref[...] = out.astype(BF16)

    return _pcall(body, name="dn_prep_bwd_b", grid=(3, nblk),
                  in_specs=[pl.BlockSpec((tq, W), lambda j, i: (i, j)),
                            pl.BlockSpec((SUBLANES, W), lambda j, i: (jnp.minimum((i + 1) * hb, S // SUBLANES - 1), j)),
                            pl.BlockSpec((DN_CONV, W), lambda j, i: (0, j))],
                  out_specs=pl.BlockSpec((tq, W), lambda j, i: (i, j)), out_shape=_sds((S, 3 * W), BF16),
                  compiler_params=_cp("arbitrary", "arbitrary"))(dc, dc, conv_w)


def _gate_terms(ba, al, dt):
    u = ba + dt
    sp = jnp.maximum(u, 0.0) + jnp.log(1.0 + jnp.exp(-jnp.abs(u)))
    return _sigmoid(ba), -jnp.exp(al) * sp, u


def _dn_gates(ba, alog_row, dt_row, H):
    S = ba.shape[0]
    tq = _tile(S, 512, SUBLANES)
    W = H * DN_DK

    def fn(i, ba_ref, al_ref, dt_ref, be_ref, g_ref):
        bet, gg, _ = _gate_terms(ba_ref[...], al_ref[...], dt_ref[...])
        for h in range(H):
            sl = slice(h * DN_DK, (h + 1) * DN_DK)
            be_ref[:, sl] = jnp.broadcast_to(bet[:, h:h + 1], (tq, DN_DK))
            g_ref[:, sl] = jnp.broadcast_to(gg[:, H + h:H + h + 1], (tq, DN_DK))

    return _rows("dn_gates", fn, S, tq, (ba, alog_row, dt_row), [_rb(tq, LANES), _full((1, LANES)), _full((1, LANES))],
                 (_sds((S, W), F32), _sds((S, W), F32)), (_rb(tq, W), _rb(tq, W)))


def _dn_gates_bwd(ba, alog_row, dt_row, dbeta_b, dg_b, H):
    S = ba.shape[0]
    tq = _tile(S, 512, SUBLANES)
    W = H * DN_DK

    def fn(i, ba_ref, al_ref, dt_ref, db_ref, dg_ref, o_ref, dal_ref, ddt_ref):
        bet, gg, u = _gate_terms(ba_ref[...], al_ref[...], dt_ref[...])
        lane = lax.broadcasted_iota(jnp.int32, (tq, LANES), 1)
        d = jnp.zeros((tq, LANES), F32)
        for h in range(H):
            d = jnp.where(lane == h, db_ref[:, h * DN_DK:h * DN_DK + 1], d)
            d = jnp.where(lane == H + h, dg_ref[:, h * DN_DK:h * DN_DK + 1], d)
        is_a = (lane >= H) & (lane < 2 * H)
        da = jnp.where(is_a, d * (-jnp.exp(al_ref[...]) * _sigmoid(u)), 0.0)
        dlog = jnp.where(lane < H, d * bet * (1.0 - bet), da)
        o_ref[...] = jnp.concatenate([dlog, jnp.zeros((tq, BA_W - LANES), F32)], axis=1).astype(BF16)
        _acc(i, dal_ref, jnp.sum(jnp.where(is_a, d * gg, 0.0), axis=0, keepdims=True))
        _acc(i, ddt_ref, jnp.sum(da, axis=0, keepdims=True))

    f = _full((1, LANES))
    return _rows("dn_gates_bwd", fn, S, tq, (ba, alog_row, dt_row, dbeta_b, dg_b),
                 [_rb(tq, LANES), f, f, _rb(tq, W), _rb(tq, W)],
                 (_sds((S, BA_W), BF16), _sds((1, LANES), F32), _sds((1, LANES), F32)), (_rb(tq, BA_W), f, f))


def _dn_out(o, proj, ng, W, cb_z):
    S = o.shape[0]
    tq = _tile(S, 256, SUBLANES)

    def fn(i, o_ref, z_ref, g_ref, y_ref):
        for h in range(W // DN_DK):
            sl = slice(h * DN_DK, (h + 1) * DN_DK)
            z = z_ref[:, sl].astype(F32)
            y_ref[:, sl] = (_rms_fwd(o_ref[:, sl], g_ref[...]) * (z * _sigmoid(z))).astype(BF16)

    return _rows("dn_out", fn, S, tq, (o, proj, ng), [_rb(tq, W), _rb(tq, W, cb_z), _full((1, DN_DK))], _sds((S, W), BF16), _rb(tq, W))


def _dn_out_bwd(o, proj, ng, dy, W, cb_z):
    S = o.shape[0]
    tq = _tile(S, 256, SUBLANES)

    def fn(i, o_ref, z_ref, g_ref, d_ref, do_ref, dz_ref, dg_ref):
        g = g_ref[...]
        dg = jnp.zeros((1, DN_DK), F32)
        for h in range(W // DN_DK):
            sl = slice(h * DN_DK, (h + 1) * DN_DK)
            oh, z, d = o_ref[:, sl], z_ref[:, sl].astype(F32), d_ref[:, sl].astype(F32)
            sg = _sigmoid(z)
            dn = d * (z * sg)
            dz_ref[:, sl] = (d * _rms_fwd(oh, g) * (sg * (1.0 + z * (1.0 - sg)))).astype(BF16)
            dx, dgh = _rms_bwd(oh, g, dn)
            do_ref[:, sl] = dx
            dg = dg + dgh
        _acc(i, dg_ref, dg)

    r = _rb(tq, W)
    return _rows("dn_out_bwd", fn, S, tq, (o, proj, ng, dy), [r, _rb(tq, W, cb_z), _full((1, DN_DK)), r],
                 (_sds((S, W), F32), _sds((S, W), BF16), _sds((1, DN_DK), F32)), (r, r, _full((1, DN_DK))))


def _bdot(a, b, mode="nn"):
    return lax.dot_general(a.astype(BF16), b.astype(BF16), (_DIMS[mode], ((), ())), preferred_element_type=F32)


def _rsum(x):
    return jnp.broadcast_to(jnp.sum(x, axis=-1, keepdims=True), x.shape)


def _dot3(a, b, mode="nn"):
    ah, bh = a.astype(BF16), b.astype(BF16)
    al, bl = (a - ah.astype(F32)).astype(BF16), (b - bh.astype(F32)).astype(BF16)
    d = lambda x, y: lax.dot_general(x, y, (_DIMS[mode], ((), ())), preferred_element_type=F32)
    return d(ah, bh) + (d(al, bh) + d(ah, bl))


def _cumsum_rows(x, reverse=False):
    n = x.shape[0]
    row = lax.broadcasted_iota(jnp.int32, x.shape, 0)
    s = 1
    while s < n:
        if reverse:
            x = x + jnp.where(row < n - s, pltpu.roll(x, n - s, 0), 0.0)
        else:
            x = x + jnp.where(row >= s, pltpu.roll(x, s, 0), 0.0)
        s *= 2
    return x


def _each(f, *lists):
    return [f(*a) for a in zip(*lists)]


def _delta_local(qs, ks, vs, bes, grs):
    C = DN_CHUNK
    ri = lax.broadcasted_iota(jnp.int32, (C, C), 0)
    ci = lax.broadcasted_iota(jnp.int32, (C, C), 1)
    causal, strict = ri >= ci, ri > ci
    gcs = [_cumsum_rows(g) for g in grs]
    decays = [jnp.where(causal, jnp.exp(jnp.where(causal, gc[:, :C] - gc.T[:C, :], 0.0)), 0.0) for gc in gcs]
    egs = [jnp.exp(gc) for gc in gcs]
    eks = [jnp.exp(gc[C - 1:C, :] - gc) for gc in gcs]
    gams = [jnp.exp(gc[C - 1:C, :]) for gc in gcs]
    kbs = _each(lambda k, be: k * be, ks, bes)
    kks = _each(lambda kb, k: _bdot(kb, k, "nt"), kbs, ks)
    nls = _each(lambda kk, dc: jnp.where(strict, -kk * dc, 0.0), kks, decays)
    eye = (ri == ci).astype(F32)
    ts = [eye + nl for nl in nls]
    pws = [_dot3(nl, nl) for nl in nls]
    for s in range(4):
        both = _each(lambda t, pw: _dot3(jnp.concatenate([t, pw], axis=0), pw), ts, pws)
        ts = _each(lambda t, b: t + b[:C], ts, both)
        pws = [b[C:] for b in both]
    ts = _each(lambda t, pw: t + _dot3(t, pw), ts, pws)
    vbs = _each(lambda v, be: v * be, vs, bes)
    kbes = _each(lambda kb, eg: kb * eg, kbs, egs)
    uws = _each(lambda t, vb, kbe: _dot3(t, jnp.concatenate([vb, kbe], axis=1)), ts, vbs, kbes)
    us, ws = [uw[:, :DN_DK] for uw in uws], [uw[:, DN_DK:] for uw in uws]
    qks = _each(lambda q, k: _bdot(q, k, "nt"), qs, ks)
    return dict(decay=decays, eg=egs, ek=eks, gam=gams, kb=kbs, kk=kks, t=ts, vb=vbs, kbe=kbes, u=us, w=ws, qk=qks,
                a=_each(lambda qk, dc: qk * dc, qks, decays), qd=_each(lambda q, eg: q * eg, qs, egs),
                kd=_each(lambda k, ek: k * ek, ks, eks), strict=strict)


def _delta_items(refs, CB, HB):
    C, dk = DN_CHUNK, DN_DK
    return [[r[c * C:(c + 1) * C, h * dk:(h + 1) * dk] for h in range(HB) for c in range(CB)] for r in refs]


def _delta_fwd(qkv, beta_b, g_b, H, CB, HB):
    S = qkv.shape[0]
    C, dk = DN_CHUNK, DN_DK
    N = S // C
    R = CB * C
    G = H // HB

    def body(q_ref, k_ref, v_ref, b_ref, g_ref, o_ref, st_ref, s_ref):
        @pl.when(pl.program_id(1) == 0)
        def _():
            s_ref[...] = jnp.zeros((HB, dk, dk), F32)

        L = _delta_local(*_delta_items((q_ref, k_ref, v_ref, b_ref, g_ref), CB, HB))
        ss = [s_ref[h] for h in range(HB)]
        for c in range(CB):
            it = [h * CB + c for h in range(HB)]
            for h in range(HB):
                st_ref[h, c] = ss[h]
            wq = [_bdot(jnp.concatenate([L["w"][i], L["qd"][i]], axis=0), s) for i, s in zip(it, ss)]
            vns = [L["u"][i] - x[:C] for i, x in zip(it, wq)]
            outs = [x[C:] + _bdot(L["a"][i], vn) for i, x, vn in zip(it, wq, vns)]
            ss = [s * L["gam"][i] + _bdot(L["kd"][i], vn, "tn") for i, s, vn in zip(it, ss, vns)]
            for h in range(HB):
                o_ref[c * C:(c + 1) * C, h * dk:(h + 1) * dk] = outs[h]
        for h in range(HB):
            s_ref[h] = ss[h]

    blk = lambda off: pl.BlockSpec((R, HB * dk), lambda h, n: (n, off + h))
    return _pcall(body, name="delta_fwd", grid=(G, N // CB),
                  in_specs=[blk(0), blk(G), blk(2 * G), blk(0), blk(0)],
                  out_specs=(blk(0), pl.BlockSpec((HB, CB, dk, dk), lambda h, n: (h, n, 0, 0))),
                  out_shape=(_sds((S, H * dk), F32), _sds((H, N, dk, dk), F32)),
                  scratch_shapes=[pltpu.VMEM((HB, dk, dk), F32)],
                  compiler_params=_cp("arbitrary", "arbitrary"))(qkv, qkv, qkv, beta_b, g_b)


def _delta_bwd(qkv, beta_b, g_b, states, do, H, CB, HB):
    S = qkv.shape[0]
    C, dk = DN_CHUNK, DN_DK
    N = S // C
    R = CB * C
    NB = N // CB
    G = H // HB

    def body(q_ref, k_ref, v_ref, b_ref, g_ref, st_ref, do_ref, dqkv_ref, db_ref, dg_ref, ds_ref):
        @pl.when(pl.program_id(1) == 0)
        def _():
            ds_ref[...] = jnp.zeros((HB, dk, dk), F32)

        qs, ks, vs, bes, grs, dos = _delta_items((q_ref, k_ref, v_ref, b_ref, g_ref, do_ref), CB, HB)
        L = _delta_local(qs, ks, vs, bes, grs)
        ts, decays, kbs, egs, eks, gams, qds, kds = (L[n] for n in ("t", "decay", "kb", "eg", "ek", "gam", "qd", "kd"))
        s0s = [st_ref[h, c] for h in range(HB) for c in range(CB)]
        vns = _each(lambda u, w, s0: u - _bdot(w, s0), L["u"], L["w"], s0s)
        pre_dvn = _each(lambda a, d: _bdot(a, d, "tn"), L["a"], dos)
        pre_ds = _each(lambda qd, d: _bdot(qd, d, "tn"), qds, dos)
        das = _each(lambda d, vn: _bdot(d, vn, "nt"), dos, vns)
        ds = [ds_ref[h] for h in range(HB)]
        ds1s, dvns = [None] * (HB * CB), [None] * (HB * CB)
        for c in reversed(range(CB)):
            it = [h * CB + c for h in range(HB)]
            new = [pre_dvn[i] + _bdot(kds[i], d) for i, d in zip(it, ds)]
            for i, d, dv in zip(it, ds, new):
                ds1s[i], dvns[i] = d, dv
            ds = [pre_ds[i] + d * gams[i] - _bdot(L["w"][i], dv, "tn") for i, d, dv in zip(it, ds, new)]
        for h in range(HB):
            ds_ref[h] = ds[h]
        dkds = _each(lambda vn, d1: _bdot(vn, d1, "nt"), vns, ds1s)
        dgams = _each(lambda s0, d1: jnp.sum(jnp.sum(s0 * d1, axis=1, keepdims=True), axis=0, keepdims=True), s0s, ds1s)
        ost = _each(lambda d, dv, s0: _bdot(jnp.concatenate([d, dv], axis=0), s0, "nt"), dos, dvns, s0s)
        dqds, dws = [x[:C] for x in ost], [-x[C:] for x in ost]
        dvw = _each(lambda dv, dw: jnp.concatenate([dv, dw], axis=1), dvns, dws)
        tdvw = _each(lambda t, x: _dot3(t, x, "tn"), ts, dvw)
        dvbs, dkbes = [x[:, :dk] for x in tdvw], [x[:, dk:] for x in tdvw]
        dts = _each(lambda x, vb, kbe: _dot3(x, jnp.concatenate([vb, kbe], axis=1), "nt"), dvw, L["vb"], L["kbe"])
        tmp = _each(lambda dt, t: _dot3(dt, t, "nt"), dts, ts)
        dls = _each(lambda t, x: -_dot3(t, x, "tn"), ts, tmp)
        ms = _each(lambda dl, dc: jnp.where(L["strict"], dl * dc, 0.0), dls, decays)
        mas = _each(lambda da, dc: da * dc, das, decays)
        dkbs = _each(lambda m, k, dkbe, eg: _bdot(m, k) + dkbe * eg, ms, ks, dkbes, egs)
        dks = _each(lambda m, kb, ma, q, dkd, ek, dkb, be: _bdot(m, kb, "tn") + _bdot(ma, q, "tn") + dkd * ek + dkb * be,
                    ms, kbs, mas, qs, dkds, eks, dkbs, bes)
        dqs = _each(lambda ma, k, dqd, eg: _bdot(ma, k) + dqd * eg, mas, ks, dqds, egs)
        es = _each(lambda m, kk, ma, qk: m * kk + ma * qk, ms, L["kk"], mas, L["qk"])
        ones = jnp.ones((C, dk), BF16)
        row = lax.broadcasted_iota(jnp.int32, (C, dk), 0)
        for i in range(HB * CB):
            h, c = divmod(i, CB)
            rs, cs = slice(c * C, (c + 1) * C), slice(h * dk, (h + 1) * dk)
            e = es[i]
            e_hi = e.astype(BF16)
            col = _bdot(e_hi, ones, "tn") + _bdot(e - e_hi.astype(F32), ones, "tn")
            t_kd = _rsum(dkds[i] * kds[i])
            dgc = (jnp.broadcast_to(jnp.sum(e, axis=1, keepdims=True), (C, dk)) - col + _rsum(dqds[i] * qds[i]) - t_kd
                   + _rsum(dkbes[i] * L["kbe"][i]))
            dglast = jnp.sum(t_kd[:, 0:1], axis=0, keepdims=True) + dgams[i] * gams[i][:, 0:1]
            dgc = dgc + jnp.where(row == C - 1, dglast, 0.0)
            dqkv_ref[0, rs, cs] = dqs[i]
            dqkv_ref[1, rs, cs] = dks[i]
            dqkv_ref[2, rs, cs] = dvbs[i] * bes[i]
            db_ref[rs, cs] = _rsum(dkbs[i] * ks[i]) + _rsum(dvbs[i] * vs[i])
            dg_ref[rs, cs] = _cumsum_rows(dgc, reverse=True)

    blk = lambda off: pl.BlockSpec((R, HB * dk), lambda h, n: (NB - 1 - n, off + h))
    W = H * dk
    return _pcall(body, name="delta_bwd", grid=(G, NB),
                  in_specs=[blk(0), blk(G), blk(2 * G), blk(0), blk(0),
                            pl.BlockSpec((HB, CB, dk, dk), lambda h, n: (h, NB - 1 - n, 0, 0)), blk(0)],
                  out_specs=(pl.BlockSpec((3, R, HB * dk), lambda h, n: (0, NB - 1 - n, h)), blk(0), blk(0)),
                  out_shape=(_sds((3, S, W), F32), _sds((S, W), F32), _sds((S, W), F32)),
                  scratch_shapes=[pltpu.VMEM((HB, dk, dk), F32)],
                  compiler_params=_cp("arbitrary", "arbitrary"))(qkv, qkv, qkv, beta_b, g_b, states, do)


def _rope_consts():
    lane = np.arange(LANES) % SW_HD
    half = ROT_DIM // 2
    inv = (ROPE_THETA ** (-np.arange(half, dtype=np.float32) * np.float32(2.0 / ROT_DIM))).astype(np.float32)
    freq = np.where(lane < ROT_DIM, inv[lane % half], 0.0).astype(np.float32)
    lo = (lane < half).astype(np.float32)
    hi = ((lane >= half) & (lane < ROT_DIM)).astype(np.float32)
    return jnp.asarray(np.stack([freq, -lo, hi] + [np.zeros(LANES, np.float32)] * 5))


def _rope_tables(pos_col):
    S = pos_col.shape[0]
    tq = _tile(S, 1024, SUBLANES)

    def fn(i, p_ref, c_ref, cos_ref, s1_ref, s2_ref):
        ang = p_ref[...].astype(F32) * c_ref[0:1, :]
        sn = jnp.sin(ang)
        cos_ref[...] = jnp.cos(ang)
        s1_ref[...] = sn * c_ref[1:2, :]
        s2_ref[...] = sn * c_ref[2:3, :]

    o, r = _sds((S, LANES), F32), _rb(tq, LANES)
    return _rows("rope_tables", fn, S, tq, (pos_col, _rope_consts()), [_rb(tq, 1), _full((SUBLANES, LANES))], (o, o, o), (r, r, r))


def _wide(a, w):
    return a if w == LANES else jnp.tile(a, (1, w // LANES))


def _rope(x, cos, s1, s2):
    w, h = x.shape[1], ROT_DIM // 2
    return x * _wide(cos, w) + pltpu.roll(x, w - h, 1) * _wide(s1, w) + pltpu.roll(x, h, 1) * _wide(s2, w)


def _unrope(d, cos, s1, s2):
    w, h = d.shape[1], ROT_DIM // 2
    return d * _wide(cos, w) + pltpu.roll(d * _wide(s1, w), h, 1) + pltpu.roll(d * _wide(s2, w), w - h, 1)


def _swa_setup(n, q_ref, kc_ref, kp_ref, vc_ref, vp_ref, tc, tp):
    B = SW_BLOCK
    qr = _rope(q_ref[...].astype(F32), tc[0][...], tc[1][...], tc[2][...]) * (SW_HD ** -0.5)
    kw = jnp.concatenate([_rope(kp_ref[...].astype(F32), tp[0][...], tp[1][...], tp[2][...]),
                          _rope(kc_ref[...].astype(F32), tc[0][...], tc[1][...], tc[2][...])], axis=0)
    vw = jnp.concatenate([vp_ref[...], vc_ref[...]], axis=0).astype(F32)
    lane = lax.broadcasted_iota(jnp.int32, (2 * B, LANES), 1)
    heads = []
    for hk in range(SW_KV_HEADS):
        kh, vh = kw[:, hk * SW_HD:(hk + 1) * SW_HD], vw[:, hk * SW_HD:(hk + 1) * SW_HD]
        kk, vv = jnp.concatenate([kh, kh], axis=1), jnp.concatenate([vh, vh], axis=1)
        heads.append(tuple(jnp.where(sel, t, 0.0).astype(BF16) for t in (kk, vv) for sel in (lane < SW_HD, lane >= SW_HD)))
    qi = lax.broadcasted_iota(jnp.int32, (B, 2 * B), 0) + B
    ki = lax.broadcasted_iota(jnp.int32, (B, 2 * B), 1)
    off = qi - ki
    ok = (off >= 0) & (off < SW_BLOCK) & ((ki >= B) | (n > 0))
    return qr, heads, jnp.where(ok, 0.0, -1e30), lane


SWA_GROUPS = 4
SWA_GROUPS_BWD = 2


def _swa_probs(items, qs, heads, bias, sk_ref, G2):
    ss = [_bdot(qs[j], heads[j // G2][half], "nt") + bias for j, half in items]
    sks = [sk_ref[0:1, 2 * j + half:2 * j + half + 1] for j, half in items]
    ms = [jnp.maximum(jnp.max(s, axis=-1, keepdims=True), sk) for s, sk in zip(ss, sks)]
    ps = [jnp.exp(s - m) for s, m in zip(ss, ms)]
    es = [jnp.exp(sk - m) for sk, m in zip(sks, ms)]
    inv = [1.0 / (jnp.sum(p, axis=-1, keepdims=True) + e) for p, e in zip(ps, es)]
    return [p * i for p, i in zip(ps, inv)], [e * i for e, i in zip(es, inv)]


def _swa_specs(W, cb_q, cb_k):
    B = SW_BLOCK
    assert (W // LANES) % SWA_GROUPS == 0 and (W // LANES) % SWA_GROUPS_BWD == 0
    cur = lambda w, cb: pl.BlockSpec((B, w), lambda n: (n, cb))
    prv = lambda w, cb: pl.BlockSpec((B, w), lambda n: (jnp.maximum(n - 1, 0), cb))
    specs = [cur(W, cb_q), cur(LANES, cb_k), prv(LANES, cb_k), cur(LANES, cb_k + 1), prv(LANES, cb_k + 1)]
    return specs + [cur(LANES, 0)] * 3 + [prv(LANES, 0)] * 3 + [_full((1, LANES))]


def _swa_fwd(proj, tabs, sinks_row, W, cb_q, cb_k):
    S = proj.shape[0]
    G2 = SW_Q_HEADS // SW_KV_HEADS // 2

    def body(q_ref, kc_ref, kp_ref, vc_ref, vp_ref, c0, c1, c2, p0, p1, p2, sk_ref, o_ref):
        n = pl.program_id(0)
        qr, heads, bias, _ = _swa_setup(n, q_ref, kc_ref, kp_ref, vc_ref, vp_ref, (c0, c1, c2), (p0, p1, p2))
        qs = [qr[:, j * LANES:(j + 1) * LANES].astype(BF16) for j in range(W // LANES)]
        for j0 in range(0, W // LANES, SWA_GROUPS):
            items = [(j, half) for j in range(j0, j0 + SWA_GROUPS) for half in range(2)]
            probs, _ = _swa_probs(items, qs, heads, bias, sk_ref, G2)
            pv = [_bdot(p, heads[j // G2][2 + half]) for p, (j, half) in zip(probs, items)]
            for g in range(SWA_GROUPS):
                o_ref[:, (j0 + g) * LANES:(j0 + g + 1) * LANES] = (pv[2 * g] + pv[2 * g + 1]).astype(BF16)

    t = tuple(tabs)
    return _pcall(body, name="swa_fwd", grid=(S // SW_BLOCK,), in_specs=_swa_specs(W, cb_q, cb_k),
                  out_specs=pl.BlockSpec((SW_BLOCK, W), lambda n: (n, 0)), out_shape=_sds((S, W), BF16),
                  compiler_params=_cp("arbitrary"))(proj, proj, proj, proj, proj, *t, *t, sinks_row)


def _swa_bwd(proj, tabs, sinks_row, do, W, cb_q, cb_k):
    S = proj.shape[0]
    B = SW_BLOCK
    G2 = SW_Q_HEADS // SW_KV_HEADS // 2
    SKR = -(-SW_Q_HEADS // SUBLANES) * SUBLANES

    def body(q_ref, kc_ref, kp_ref, vc_ref, vp_ref, c0, c1, c2, p0, p1, p2, sk_ref, do_ref,
             dq_ref, dkc_ref, dkp_ref, dvc_ref, dvp_ref, dsk_ref):
        n = pl.program_id(0)
        qr, heads, bias, lane = _swa_setup(n, q_ref, kc_ref, kp_ref, vc_ref, vp_ref, (c0, c1, c2), (p0, p1, p2))

        @pl.when(n == 0)
        def _():
            dsk_ref[...] = jnp.zeros((SKR, LANES), F32)

        acc_k = [jnp.zeros((2 * B, LANES), F32) for _ in range(SW_KV_HEADS)]
        acc_v = [jnp.zeros((2 * B, LANES), F32) for _ in range(SW_KV_HEADS)]
        qs = [qr[:, j * LANES:(j + 1) * LANES].astype(BF16) for j in range(W // LANES)]
        dos = [do_ref[:, j * LANES:(j + 1) * LANES].astype(BF16) for j in range(W // LANES)]
        dqs = []
        for j0 in range(0, W // LANES, SWA_GROUPS_BWD):
            items = [(j, half) for j in range(j0, j0 + SWA_GROUPS_BWD) for half in range(2)]
            probs, psinks = _swa_probs(items, qs, heads, bias, sk_ref, G2)
            dps = [_bdot(dos[j], heads[j // G2][2 + half], "nt") for j, half in items]
            deltas = [jnp.sum(p * dp, axis=-1, keepdims=True) for p, dp in zip(probs, dps)]
            dss = [(p * (dp - dl)).astype(BF16) for p, dp, dl in zip(probs, dps, deltas)]
            pbs = [p.astype(BF16) for p in probs]
            dqp = [_bdot(ds, heads[j // G2][half]) for ds, (j, half) in zip(dss, items)]
            dkk = [_bdot(ds, qs[j], "tn") for ds, (j, half) in zip(dss, items)]
            dvv = [_bdot(p, dos[j], "tn") for p, (j, half) in zip(pbs, items)]
            for i, (j, half) in enumerate(items):
                hk, h = j // G2, 2 * j + half
                sel = (lane < SW_HD) if half == 0 else (lane >= SW_HD)
                acc_k[hk] = acc_k[hk] + jnp.where(sel, dkk[i], 0.0)
                acc_v[hk] = acc_v[hk] + jnp.where(sel, dvv[i], 0.0)
                dsk_ref[h:h + 1, :] += jnp.broadcast_to(-jnp.sum(psinks[i] * deltas[i], axis=0, keepdims=True), (1, LANES))
            dqs += [dqp[2 * g] + dqp[2 * g + 1] for g in range(SWA_GROUPS_BWD)]
        dq = jnp.concatenate(dqs, axis=1) * (SW_HD ** -0.5)
        dq_ref[...] = _unrope(dq, c0[...], c1[...], c2[...]).astype(BF16)
        fold = lambda a: a[:, :SW_HD] + a[:, SW_HD:]
        dkw = jnp.concatenate([fold(a) for a in acc_k], axis=1)
        dvw = jnp.concatenate([fold(a) for a in acc_v], axis=1)
        dkp_ref[...], dkc_ref[...] = dkw[:B], dkw[B:]
        dvp_ref[...], dvc_ref[...] = dvw[:B], dvw[B:]

    t = tuple(tabs)
    blk = lambda w: pl.BlockSpec((B, w), lambda n: (n, 0))
    o = _sds((S, LANES), F32)
    return _pcall(body, name="swa_bwd", grid=(S // B,), in_specs=_swa_specs(W, cb_q, cb_k) + [blk(W)],
                  out_specs=(blk(W), blk(LANES), blk(LANES), blk(LANES), blk(LANES), _full((SKR, LANES))),
                  out_shape=(_sds((S, W), BF16), o, o, o, o, _sds((SKR, LANES), F32)),
                  compiler_params=_cp("arbitrary"))(proj, proj, proj, proj, proj, *t, *t, sinks_row, do)


def _swa_kv_combine(dkc, dkp, dvc, dvp, tabs):
    S = dkc.shape[0]
    B = SW_BLOCK
    nb = S // B

    def fn(n, kc_ref, kp_ref, vc_ref, vp_ref, c0, c1, c2, o_ref):
        more = n < nb - 1
        dk = kc_ref[...] + jnp.where(more, kp_ref[...], 0.0)
        dv = vc_ref[...] + jnp.where(more, vp_ref[...], 0.0)
        o_ref[...] = jnp.concatenate([_unrope(dk, c0[...], c1[...], c2[...]), dv], axis=1).astype(BF16)

    cur = _rb(B, LANES)
    nxt = pl.BlockSpec((B, LANES), lambda n: (jnp.minimum(n + 1, nb - 1), 0))
    return _rows("swa_kv_combine", fn, S, B, (dkc, dkp, dvc, dvp, *tabs), [cur, nxt, cur, nxt, cur, cur, cur],
                 _sds((S, 2 * LANES), BF16), _rb(B, 2 * LANES))


ANY = pl.BlockSpec(memory_space=pl.ANY)


def _place():
    x, y, c = lax.axis_index("x"), lax.axis_index("y"), lax.axis_index("c")
    return x, y, c, [(1 - x, y), (x, 1 - y), (1 - x, 1 - y)]


def _comm_call(name, body, out_shapes, n_sems, n_local, *ins):
    return _pcall(body, name=name, out_shape=tuple(out_shapes), in_specs=[ANY] * len(ins), out_specs=tuple(ANY for _ in out_shapes),
                  scratch_shapes=[pltpu.SemaphoreType.DMA((n_sems,)), pltpu.SemaphoreType.DMA((n_sems,)),
                                  pltpu.SemaphoreType.DMA((n_local,))])(*ins)


def _remote(src, dst, send, recv, k, to):
    return pltpu.make_async_remote_copy(src_ref=src, dst_ref=dst, send_sem=send.at[k], recv_sem=recv.at[k], device_id=to,
                                        device_id_type=MESH)


def _gather_chips(name, arrs):
    n = len(arrs)
    Lh = arrs[0].shape[0] // 2

    def body(*refs):
        w, o, (send, recv, _) = refs[:n], refs[n:2 * n], refs[2 * n:]
        x, y, c, chips = _place()
        me, sib = 2 * x + y, (x, y, 1 - c)
        own, other = pl.ds(c * Lh, Lh), pl.ds((1 - c) * Lh, Lh)
        idx = [2 * cx + cy for cx, cy in chips]
        first = [[_remote(w[a].at[own], o[a].at[me, own], send, recv, 6 * a + j, (*chips[j], c)) for j in range(3)] for a in range(n)]
        passed = [[_remote(o[a].at[idx[j], own], o[a].at[idx[j], own], send, recv, 6 * a + 3 + j, sib) for j in range(3)] for a in range(n)]
        for cp in [cp for row in first for cp in row]:
            cp.start()
        for j in range(3):
            for a in range(n):
                _remote(w[a].at[own], o[a].at[idx[j], own], send, recv, 6 * a + j, (*chips[j], c)).wait_recv()
                passed[a][j].start()
        for j in range(3):
            for a in range(n):
                _remote(w[a].at[other], o[a].at[idx[j], other], send, recv, 6 * a + 3 + j, sib).wait_recv()
        for cp in [cp for row in first + passed for cp in row]:
            cp.wait_send()

    return _comm_call(name, body, [_sds((4,) + a.shape, a.dtype) for a in arrs], 6 * n, 1, *arrs)


def _pair_swap(name, arrs, whole=False):
    n = len(arrs)
    Lh = arrs[0].shape[0] if whole else arrs[0].shape[0] // 2

    def body(*refs):
        g, o, (send, recv, _) = refs[:n], refs[n:2 * n], refs[2 * n:]
        x, y, c, _ = _place()
        cps = [_remote(g[a] if whole else g[a].at[pl.ds((1 - c) * Lh, Lh)], o[a], send, recv, a, (x, y, 1 - c)) for a in range(n)]
        for cp in cps:
            cp.start()
        for cp in cps:
            cp.wait()

    return _comm_call(name, body, [_sds((Lh,) + a.shape[1:], a.dtype) for a in arrs], n, 1, *arrs)


def _chip_slice(ref, axis, s):
    if axis is None:
        return ref.at[s]
    q = ref.shape[axis] // 4
    start = s * q if isinstance(s, int) else pl.multiple_of(s * q, q)
    return ref.at[tuple([slice(None)] * axis + [pl.ds(start, q)])]


def _scatter_chips(name, items):
    n = len(items)
    part = lambda a, ax: a.shape[1:] if ax is None else tuple(d // 4 if i == ax else d for i, d in enumerate(a.shape))

    def body(*refs):
        p, o, (send, recv, _) = refs[:n], refs[n:2 * n], refs[2 * n:]
        x, y, c, chips = _place()
        me = 2 * x + y
        idx = [2 * cx + cy for cx, cy in chips]
        cps = [_remote(_chip_slice(p[a], items[a][1], idx[j]), o[a].at[me], send, recv, 3 * a + j, (*chips[j], c))
               for a in range(n) for j in range(3)]
        for cp in cps:
            cp.start()
        for a in range(n):
            for j in range(3):
                _remote(_chip_slice(p[a], items[a][1], me), o[a].at[idx[j]], send, recv, 3 * a + j, (*chips[j], c)).wait_recv()
        for cp in cps:
            cp.wait_send()

    return _comm_call(name, body, [_sds((4,) + part(a, ax), a.dtype) for a, ax in items], 3 * n, 1, *[a for a, _ in items])


def _own_part(a, axis, me):
    if axis is None:
        return lax.dynamic_index_in_dim(a, me, 0, keepdims=False)
    q = a.shape[axis] // 4
    return lax.dynamic_slice_in_dim(a, me * q, q, axis)


HBM = pl.BlockSpec(memory_space=pltpu.HBM)
SEM = pl.BlockSpec(memory_space=pltpu.SEMAPHORE)
EFFECT = pltpu.SideEffectType.DATAFLOW_SIDE_EFFECTING


def _split_start(name, arrs, land_shapes, plan, nc, after=None):
    n = len(arrs)
    lands = [lax.empty(s.shape, s.dtype) for s in land_shapes]
    ins = list(arrs) + lands + ([] if after is None else [after])

    def body(*refs):
        outs = refs[len(ins):]
        for k, (src, dst, _, peer) in enumerate(plan(refs[:n], refs[n:n + len(lands)])):
            pltpu.make_async_remote_copy(src_ref=src, dst_ref=dst, send_sem=outs[k], recv_sem=outs[nc + k], device_id=peer,
                                         device_id_type=MESH).start()
        outs[-1][...] = jnp.zeros((SUBLANES, LANES), F32)

    nt = n + len(lands)
    thru = [pltpu.HBM(a.shape, a.dtype) for a in list(arrs) + lands]
    outs = _pcall(body, name=name, out_shape=tuple([pltpu.SemaphoreType.DMA(())] * (2 * nc) + thru + [_sds((SUBLANES, LANES), F32)]),
                  in_specs=[HBM] * nt + [ANY] * (len(ins) - nt),
                  out_specs=tuple([SEM] * (2 * nc) + [HBM] * nt + [pl.BlockSpec(memory_space=pltpu.VMEM)]),
                  input_output_aliases={i: 2 * nc + i for i in range(nt)},
                  compiler_params=pltpu.CompilerParams(has_side_effects=EFFECT))(
        *[pltpu.with_memory_space_constraint(a, pltpu.HBM) for a in ins[:nt]], *ins[nt:])
    return dict(sems=outs[:2 * nc], arrs=outs[2 * nc:2 * nc + n], lands=outs[2 * nc + n:2 * nc + nt], token=outs[-1], plan=plan, nc=nc)


def _split_wait(name, handle, after):
    arrs, lands, sems, nc = list(handle["arrs"]), list(handle["lands"]), list(handle["sems"]), handle["nc"]
    n, nt = len(arrs), len(arrs) + len(lands)

    def body(*refs):
        sem = refs[nt:nt + 2 * nc]
        for k, (src, _, landing, peer) in enumerate(handle["plan"](refs[:n], refs[n:nt])):
            cp = pltpu.make_async_remote_copy(src_ref=src, dst_ref=landing, send_sem=sem[k], recv_sem=sem[nc + k], device_id=peer,
                                              device_id_type=MESH)
            cp.wait_send()
            cp.wait_recv()

    thru = tuple(pltpu.HBM(a.shape, a.dtype) for a in arrs + lands)
    outs = _pcall(body, name=name, out_shape=thru, in_specs=[HBM] * nt + [SEM] * (2 * nc) + [ANY], out_specs=tuple([HBM] * nt),
                  input_output_aliases={i: i for i in range(nt)},
                  compiler_params=pltpu.CompilerParams(has_side_effects=EFFECT))(*arrs, *lands, *sems, after)
    return list(outs[:n]), list(outs[n:])


WHOLE = "whole"


def _plan_chips(axes):
    def plan(src, land):
        x, y, c, chips = _place()
        idx = [2 * cx + cy for cx, cy in chips]
        part = lambda a, s: src[a] if axes[a] is WHOLE else _chip_slice(src[a], axes[a], s)
        return [(part(a, idx[j]), land[a].at[2 * x + y], land[a].at[idx[j]], (*chips[j], c))
                for a in range(len(land)) for j in range(3)]
    return plan


def _plan_sibling(half):
    def plan(src, land):
        x, y, c, _ = _place()
        lh = lambda a: src[a].shape[0] // 2
        return [(src[a].at[pl.ds((1 - c) * lh(a), lh(a))] if half else src[a], land[a], land[a], (x, y, 1 - c))
                for a in range(len(land))]
    return plan


def _chips_start(name, arrs, axes, after=None):
    part = lambda a, ax: a.shape if ax is WHOLE else a.shape[1:] if ax is None else tuple(d // 4 if i == ax else d for i, d in enumerate(a.shape))
    return _split_start(name, arrs, [_sds((4,) + part(a, ax), a.dtype) for a, ax in zip(arrs, axes)], _plan_chips(axes), 3 * len(arrs), after)


def _sibling_start(name, arrs, half, after=None):
    shp = lambda a: (a.shape[0] // 2,) + a.shape[1:] if half else a.shape
    return _split_start(name, arrs, [_sds(shp(a), a.dtype) for a in arrs], _plan_sibling(half), len(arrs), after)


def _gather_all(name, b):
    R, C = b.shape
    flips = [(dx, dy, dc) for dx in (0, 1) for dy in (0, 1) for dc in (0, 1)][1:]

    def body(b_ref, o_ref, send, recv, lsem):
        x, y, c, _ = _place()
        me = 4 * x + 2 * y + c
        peers = [(x ^ dx, y ^ dy, c ^ dc) for dx, dy, dc in flips]
        mine = pltpu.make_async_copy(b_ref, o_ref.at[me], lsem.at[0])
        mine.start()
        cps = [_remote(b_ref, o_ref.at[me], send, recv, k, peer) for k, peer in enumerate(peers)]
        for cp in cps:
            cp.start()
        for k, (px, py, pc) in enumerate(peers):
            _remote(b_ref, o_ref.at[4 * px + 2 * py + pc], send, recv, k, (px, py, pc)).wait_recv()
        for cp in cps:
            cp.wait_send()
        mine.wait()

    return _comm_call(name, body, [_sds((8, R, C), b.dtype)], 7, 1, b)[0]


def _block_rows(rows, width):
    return _tile(rows, max(SUBLANES, (1 << 19) // width), SUBLANES)


def _add_half(name, g, got):
    L, A, B = g.shape
    Lh = L // 2
    tq = _block_rows(A, B)

    def body(c_ref, g_ref, r_ref, o_ref):
        o_ref[...] = (g_ref[...] + r_ref[...]).astype(BF16)

    spec = pltpu.PrefetchScalarGridSpec(
        num_scalar_prefetch=1, grid=(Lh, A // tq),
        in_specs=[pl.BlockSpec((1, tq, B), lambda l, i, c_ref: (c_ref[0] * Lh + l, i, 0)),
                  pl.BlockSpec((1, tq, B), lambda l, i, c_ref: (l, i, 0))],
        out_specs=pl.BlockSpec((1, tq, B), lambda l, i, c_ref: (l, i, 0)))
    return _pcall(body, name=name, grid_spec=spec, out_shape=_sds((Lh, A, B), BF16),
                  compiler_params=_cp("arbitrary", "arbitrary"))(lax.axis_index("c").reshape(1).astype(jnp.int32), g, got)


def _sum_slots(name, a):
    n, R, C = a.shape
    tq = _block_rows(R, n * C)

    def fn(i, a_ref, o_ref):
        t = a_ref[0].astype(F32)
        for s in range(1, n):
            t = t + a_ref[s].astype(F32)
        o_ref[...] = t

    return _rows(name, fn, R, tq, (a,), [pl.BlockSpec((n, tq, C), lambda i: (0, i, 0))], _sds((R, C), F32), _rb(tq, C))


def _adam_update(w, g, m, v):
    mn = ADAM_B1 * m + (1.0 - ADAM_B1) * g
    vn = ADAM_B2 * v + (1.0 - ADAM_B2) * (g * g)
    m_hat = mn / (1.0 - ADAM_B1 ** ADAM_STEP)
    v_hat = vn / (1.0 - ADAM_B2 ** ADAM_STEP)
    return -ADAM_LR * (m_hat / (jnp.sqrt(v_hat) + ADAM_EPS) + ADAM_WD * w), mn, vn


def _adamw(name, w, g, m, v):
    R, C = w.shape
    tq = _tile(R, 256, SUBLANES)

    def fn(i, w_ref, g_ref, m_ref, v_ref, d_ref, mo_ref, vo_ref):
        d_ref[...], mo_ref[...], vo_ref[...] = _adam_update(w_ref[...], g_ref[...], m_ref[...], v_ref[...])

    r, o = _rb(tq, C), _sds((R, C), F32)
    return _rows(name, fn, R, tq, (w, g, m, v), [r, r, r, r], (o, o, o), (r, r, r))


def _adamw_halves(name, w, mine, theirs, m, v, l0, prev=None):
    L, A, B = w.shape
    Lh = mine.shape[0]
    tq = _tile(A, 256, SUBLANES)

    def body(c_ref, w_ref, a_ref, b_ref, m_ref, v_ref, *refs):
        g_ref, d_ref, mo_ref, vo_ref = refs[-4:]
        is_mine = pl.program_id(0) // Lh == c_ref[0]
        g = jnp.where(is_mine, a_ref[...], b_ref[...])
        g_ref[...] = g
        d_ref[...], mo_ref[...], vo_ref[...] = _adam_update(w_ref[...], g, m_ref[...], v_ref[...])

    full = pl.BlockSpec((1, tq, B), lambda l, i, c_ref: (l0 + l, i, 0))
    half = pl.BlockSpec((1, tq, B), lambda l, i, c_ref: (l % Lh, i, 0))
    o = _sds((L, A, B), F32)
    prev = list(prev or ())
    spec = pltpu.PrefetchScalarGridSpec(num_scalar_prefetch=1, grid=(2 * Lh, A // tq), in_specs=[full, half, half, full, full] + [ANY] * len(prev),
                                        out_specs=(full, full, full, full))
    return _pcall(body, name=name, grid_spec=spec, out_shape=(o, o, o, o), input_output_aliases={6 + i: i for i in range(len(prev))},
                  compiler_params=_cp("arbitrary", "arbitrary"))(lax.axis_index("c").reshape(1).astype(jnp.int32), w, mine, theirs, m, v, *prev)


def _pack(arrs, width, lead=()):
    nl = len(lead)
    flat = jnp.concatenate([a.reshape(lead + (-1,)) for a in arrs], axis=nl)
    n = flat.shape[-1]
    unit = PACK_ROWS * width
    tot = -(-n // unit) * unit
    flat = jnp.pad(flat, [(0, 0)] * nl + [(0, tot - n)])
    return flat.reshape(lead + (tot // width, width))


def _unpack(buf, shapes, lead=()):
    flat = buf.reshape(lead + (-1,))
    out, off = [], 0
    for s in shapes:
        n = int(np.prod(s))
        out.append(flat[..., off:off + n].reshape(lead + tuple(s)))
        off += n
    return out


def _in_groups(W, H):
    o_sq = 4 * W + 2 * H
    o_k = o_sq + W
    o_g = o_k + 2 * KV_W
    return [(0, 4 * W), (o_sq, o_k), (o_g, o_g + 2 * W), (o_k, o_g), (4 * W, o_sq)]


def _relayout_in(shards, W, H):
    c4 = sum(hi - lo for lo, hi in _in_groups(W, H)) // 4
    parts = []
    for lo, hi in _in_groups(W, H):
        for s in range(4):
            a, b = max(lo, s * c4), min(hi, (s + 1) * c4)
            if a < b:
                parts.append(shards[s][:, a - s * c4:b - s * c4])
    parts.append(jnp.zeros((shards.shape[1], BA_W - 2 * H), shards.dtype))
    return jnp.concatenate(parts, axis=1)


def _shard_in(d, W, H):
    groups = _in_groups(W, H)
    starts = [sum(hi - lo for lo, hi in groups[:i]) for i in range(len(groups))]
    stored = sorted(zip(groups, starts))
    c4 = sum(hi - lo for lo, hi in groups) // 4
    out = []
    for s in range(4):
        parts = []
        for (lo, hi), at in stored:
            a, b = max(lo, s * c4), min(hi, (s + 1) * c4)
            if a < b:
                parts.append(d[:, :, at + a - lo:at + b - lo])
        out.append(jnp.concatenate(parts, axis=2))
    return jnp.stack(out)


def _lane_row(vals, at):
    return jnp.pad(vals, (at, LANES - at - vals.shape[0]))[None]


def _layer_fwd(x, lw, tabs, W, H, more=None):
    D = x.shape[1]
    cbk = 7 * W // LANES
    h = _pre_norm(x, lw["g1"])
    proj = _mm("mm_in", h, lw["win"], "nn", BF16, tn=768)
    ba = _mm("mm_ba", h, lw["win"][:, 7 * W + 2 * KV_W:], "nn", F32)
    qkv = _dn_prep(proj, lw["conv"], W)
    beta_b, g_b = _dn_gates(ba, lw["alog"], lw["dt"], H)
    o, st = _delta_fwd(qkv, beta_b, g_b, H, DELTA_CB, DELTA_HB)
    oa = _dn_out(o, proj, lw["ng"], W, 3)
    ob = _swa_fwd(proj, tabs, lw["sinks"], W, 4, cbk)
    if more is not None:
        lw.update(more(ob))
    ya = _mm("mm_up_dn", oa, lw["wup_dn"], "nn", BF16)
    yb = _mm("mm_up_sw", ob, lw["wup_sw"], "nn", BF16)
    mixin = _mix(proj, ya, yb, D, 5)
    mix = _mm("mm_o", mixin, lw["wo"], "nn", F32)
    x1, h2 = _post_mix(x, mix, lw["g2"], lw["g3"])
    f1, act = _mm("mm_ff1", h2, lw["wff1"], "nn", out_dtypes=(BF16, BF16), epi=lambda acc: (acc, jnp.square(jnp.maximum(acc, 0.0))))
    ff = _mm("mm_ff2", act, lw["wff2"], "nn", F32)
    x2 = _post_mlp(x1, ff, lw["g4"])
    saved = dict(x=x, h=h, proj=proj, ba=ba, qkv=qkv, beta_b=beta_b, g_b=g_b, o=o, st=st, oa=oa, ob=ob, ya=ya, yb=yb,
                 mixin=mixin, mix=mix, x1=x1, h2=h2, f1=f1, act=act, ff=ff)
    return x2, saved


def _layer_bwd(dx2, lw, sv, tabs, W, H, l, big):
    D = dx2.shape[1]
    cbk = 7 * W // LANES
    big = dict(big)
    dff, dg4 = _post_mlp_bwd(sv["ff"], lw["g4"], dx2)
    df1 = _mm("mm_ff2_dx", dff, lw["wff2"], "nt", BF16, extras=(sv["f1"],),
              epi=lambda acc, f1: (acc * 2.0 * jnp.maximum(f1.astype(F32), 0.0),))
    big["w_ff2"] = _mm("mm_ff2_dw", sv["act"], dff, "tn", slab=(big["w_ff2"], l))
    dh2 = _mm("mm_ff1_dx", df1, lw["wff1"], "nt", F32)
    big["w_ff1"] = _mm("mm_ff1_dw", sv["h2"], df1, "tn", slab=(big["w_ff1"], l))
    dx1, dmix, dg3, dg2 = _mid_bwd(sv["x1"], lw["g3"], dh2, dx2, sv["mix"], lw["g2"])
    dmixin = _mm("mm_o_dx", dmix, lw["wo"], "nt", BF16)
    big["w_o"] = _mm("mm_o_dw", sv["mixin"], dmix, "tn", slab=(big["w_o"], l))
    dya, dyb, dga, dgb = _mix_bwd(sv["proj"], sv["ya"], sv["yb"], dmixin, D, 5)
    doa = _mm("mm_up_dn_dx", dya, lw["wup_dn"], "nt", BF16)
    big["w_up_dn"] = _mm("mm_up_dn_dw", sv["oa"], dya, "tn", slab=(big["w_up_dn"], l))
    dob = _mm("mm_up_sw_dx", dyb, lw["wup_sw"], "nt", BF16)
    big["w_up_sw"] = _mm("mm_up_sw_dw", sv["ob"], dyb, "tn", slab=(big["w_up_sw"], l))
    do, dz, dng = _dn_out_bwd(sv["o"], sv["proj"], lw["ng"], doa, W, 3)
    dqkvn, dbeta_b, dg_b = _delta_bwd(sv["qkv"], sv["beta_b"], sv["g_b"], sv["st"], do, H, DELTA_CB, DELTA_HB)
    dba, dalog, ddt = _dn_gates_bwd(sv["ba"], lw["alog"], lw["dt"], dbeta_b, dg_b, H)
    dc, dconv = _dn_prep_bwd_a(sv["proj"], lw["conv"], dqkvn, W)
    dqkv = _dn_prep_bwd_b(dc, lw["conv"], W)
    dq_sw, dkc, dkp, dvc, dvp, dsk = _swa_bwd(sv["proj"], tabs, lw["sinks"], dob, W, 4, cbk)
    dkv = _swa_kv_combine(dkc, dkp, dvc, dvp, tabs)
    dproj = jnp.concatenate([dqkv, dz, dq_sw, dga, dgb, dkv, dba], axis=1)
    dh = _mm("mm_in_dx", dproj, lw["win"], "nt", F32, tk=768)
    big["w_in"] = _mm("mm_in_dw", sv["h"], dproj, "tn", tn=768, slab=(big["w_in"], l))
    dx, dg1 = _pre_norm_bwd(sv["x"], lw["g1"], dh, dx1)
    grads = dict(pre_mix_g=dg1[0], dn_conv_w=dconv, dn_a_log=dalog[0, H:2 * H], dn_dt_bias=ddt[0, H:2 * H], dn_norm_g=dng[0],
                 sw_sinks=dsk[:SW_Q_HEADS, 0], post_mix_g=dg2[0], pre_mlp_g=dg3[0], post_mlp_g=dg4[0])
    return dx, grads, big


_WEIGHTS = ["pre_mix_g", "w_in", "dn_conv_w", "dn_a_log", "dn_dt_bias", "dn_norm_g", "sw_sinks", "w_up_dn", "w_up_sw", "w_o",
            "post_mix_g", "pre_mlp_g", "w_ff1", "w_ff2", "post_mlp_g"]
_BIG = {"w_in": 2, "w_up_dn": 1, "w_up_sw": 1, "w_o": 1, "w_ff1": 2, "w_ff2": 1}
_SMALL = [n for n in _WEIGHTS if n not in _BIG]


def _step(P):
    x, target = P["x"][0], P["loss_target"][0]
    S, D = x.shape
    L = P["pre_mix_g"].shape[0]
    H, W = DN_HEADS, DN_HEADS * DN_DK
    assert W == D == SW_Q_HEADS * SW_HD and KV_W == LANES
    me = 2 * lax.axis_index("x") + lax.axis_index("y")

    assert L % 4 == 0
    names = list(_BIG) + ["dn_conv_w"]
    local = [P[n].astype(BF16) for n in _BIG] + [P["dn_conv_w"]]
    early_names = ("w_in", "dn_conv_w")
    own_slot = lambda gathered, mine: [lax.dynamic_update_slice_in_dim(g, w[None], me, 0) for g, w in zip(gathered, mine)]
    h_first = _chips_start("weights_first_start", [a[:1] for n, a in zip(names, local) if n in early_names], [WHOLE] * 2)
    early = dict(zip(early_names, own_slot(*reversed(_split_wait("weights_first_wait", h_first, x)))))
    rest = [a[1:] if n in early_names else a for n, a in zip(names, local)]
    h_rest = _chips_start("weights_rest_start", rest, [WHOLE] * len(rest), after=early["w_in"])
    late = {}

    def head(full, l, k):
        return dict(
            g1=P["pre_mix_g"][l][None], win=_relayout_in(full["w_in"][:, k], W, H),
            conv=jnp.concatenate([full["dn_conv_w"][s, k] for s in range(4)], axis=-1),
            alog=_lane_row(P["dn_a_log"][l], H), dt=_lane_row(P["dn_dt_bias"][l], H), ng=P["dn_norm_g"][l][None],
            sinks=_lane_row(P["sw_sinks"][l], 0), g2=P["post_mix_g"][l][None], g3=P["pre_mlp_g"][l][None], g4=P["post_mlp_g"][l][None])

    def tail(l):
        rows = lambda n: late[n][:, l].reshape(-1, late[n].shape[-1])
        return dict(wup_dn=rows("w_up_dn"), wup_sw=rows("w_up_sw"), wo=rows("w_o"), wff2=rows("w_ff2"),
                    wff1=jnp.concatenate([late["w_ff1"][s, l] for s in range(4)], axis=-1))

    tabs = _rope_tables(P["positions"].reshape(S, 1))
    lws = [head(early, 0, 0)]
    lws[0]["g1"] = lws[0]["g1"] + h_rest["token"][0, 0]

    def rest_arrived(after):
        mine, landed = _split_wait("weights_rest_wait", h_rest, after)
        late.update(zip(names, own_slot(landed, mine)))
        lws.extend({**head(late, l, l - 1), **tail(l)} for l in range(1, L))
        return tail(0)

    saved = []
    for l in range(L):
        x, sv = _layer_fwd(x, lws[l], tabs, W, H, rest_arrived if l == 0 else None)
        saved.append(sv)
    loss_row, dx = _loss_head(x, target)

    Lb = L // 2
    layer_grads = [None] * L
    F = 4 * P["w_ff1"].shape[2]
    per_layer = dict(w_in=(D, 7 * W + 2 * KV_W + BA_W), w_up_dn=(W, D), w_up_sw=(W, D), w_o=(D, D), w_ff1=(D, F), w_ff2=(F, D))
    batch = [{n: lax.empty((Lb,) + per_layer[n], F32) for n in _BIG} for _ in range(2)]
    axes = [None if n == "w_in" else ax for n, ax in _BIG.items()]

    def pair_sums(tag, h_swap, after):
        g, got = _split_wait("grad_swap_wait_" + tag, h_swap, after)
        part = {n: _add_half("grad_pair_add_%s_%s" % (tag, n), a, r) for n, a, r in zip(_BIG, g, got)}
        return [_shard_in(part[n], W, H) if n == "w_in" else part[n] for n in _BIG]

    def chip_sums(tag, h_scat, after):
        parts, slots = _split_wait("grad_scatter_wait_" + tag, h_scat, after)
        halves = []
        for n, s, a, ax in zip(_BIG, slots, parts, axes):
            s = lax.dynamic_update_slice_in_dim(s, _own_part(a, ax, me)[None], me, 0)
            halves.append(_sum_slots("grad_chip_sum_%s_%s" % (tag, n), s.reshape(4, -1, s.shape[-1])).reshape(s.shape[1:]))
        h = _sibling_start("grad_share_start_" + tag, halves, False)
        return _split_wait("grad_share_wait_" + tag, h, halves[0])

    for l in reversed(range(L)):
        dx, layer_grads[l], batch[l // Lb] = _layer_bwd(dx, lws[l], saved[l], tabs, W, H, l % Lb, batch[l // Lb])
        if l == Lb:
            h_swap = _sibling_start("grad_swap_start_hi", [batch[1][n] for n in _BIG], True)
            lws[l - 1]["g4"] = lws[l - 1]["g4"] + h_swap["token"][0, 0]
        if l == Lb - 1:
            h_scat_hi = _chips_start("grad_scatter_start_hi", pair_sums("hi", h_swap, dx), axes)
            if l > 0:
                lws[l - 1]["g4"] = lws[l - 1]["g4"] + h_scat_hi["token"][0, 0]
    h_swap = _sibling_start("grad_swap_start_lo", [batch[0][n] for n in _BIG], True)
    h_scat_lo = _chips_start("grad_scatter_start_lo", pair_sums("lo", h_swap, dx), axes)
    upper = {n: _adamw_halves("adamw_hi_" + n, P[n], mine, their, P["m_" + n], P["v_" + n], Lb)
             for n, mine, their in zip(_BIG, *chip_sums("hi", h_scat_hi, h_scat_lo["token"]))}
    gsum, delta, new_m, new_v = {}, {}, {}, {}
    for n, mine, their in zip(_BIG, *chip_sums("lo", h_scat_lo, upper["w_in"][0])):
        gsum[n], delta[n], new_m[n], new_v[n] = _adamw_halves("adamw_lo_" + n, P[n], mine, their, P["m_" + n], P["v_" + n], 0, upper[n])
    grads = {n: jnp.stack([layer_grads[l][n] for l in range(L)]) for n in _SMALL}
    small_shapes = [(1,)] + [grads[n].shape for n in _SMALL]
    tot = _sum_slots("small_sum", _gather_all("small_gather", _pack([loss_row[0, :1]] + [grads[n] for n in _SMALL], LANES)))
    small = _unpack(tot, small_shapes)
    loss = small[0][0]
    gsum.update(zip(_SMALL, small[1:]))
    cw = P["dn_conv_w"].shape[2]
    gsum["dn_conv_w"] = lax.dynamic_slice_in_dim(gsum["dn_conv_w"], me * cw, cw, axis=2)

    sm_shapes = [P[n].shape for n in _SMALL]
    outs = _adamw("adamw_small", *(_pack([src[pre + n] for n in _SMALL], LANES)
                                   for src, pre in ((P, ""), (gsum, ""), (P, "m_"), (P, "v_"))))
    for d, o in zip((delta, new_m, new_v), outs):
        d.update(zip(_SMALL, _unpack(o, sm_shapes)))

    return (loss, dx[None], *[gsum[n] for n in _WEIGHTS], *[delta[n] for n in _WEIGHTS],
            *[new_m[n] for n in _WEIGHTS], *[new_v[n] for n in _WEIGHTS])


def kernel(x, positions, pre_mix_g, w_in, dn_conv_w, dn_a_log, dn_dt_bias, dn_norm_g, sw_sinks, w_up_dn, w_up_sw, w_o, post_mix_g, pre_mlp_g, w_ff1, w_ff2, post_mlp_g, loss_target, m_pre_mix_g, m_w_in, m_dn_conv_w, m_dn_a_log, m_dn_dt_bias, m_dn_norm_g, m_sw_sinks, m_w_up_dn, m_w_up_sw, m_w_o, m_post_mix_g, m_pre_mlp_g, m_w_ff1, m_w_ff2, m_post_mlp_g, v_pre_mix_g, v_w_in, v_dn_conv_w, v_dn_a_log, v_dn_dt_bias, v_dn_norm_g, v_sw_sinks, v_w_up_dn, v_w_up_sw, v_w_o, v_post_mix_g, v_pre_mlp_g, v_w_ff1, v_w_ff2, v_post_mlp_g):
    vals = (x, positions, pre_mix_g, w_in, dn_conv_w, dn_a_log, dn_dt_bias, dn_norm_g, sw_sinks, w_up_dn, w_up_sw, w_o, post_mix_g, pre_mlp_g, w_ff1, w_ff2, post_mlp_g, loss_target, m_pre_mix_g, m_w_in, m_dn_conv_w, m_dn_a_log, m_dn_dt_bias, m_dn_norm_g, m_sw_sinks, m_w_up_dn, m_w_up_sw, m_w_o, m_post_mix_g, m_pre_mlp_g, m_w_ff1, m_w_ff2, m_post_mlp_g, v_pre_mix_g, v_w_in, v_dn_conv_w, v_dn_a_log, v_dn_dt_bias, v_dn_norm_g, v_sw_sinks, v_w_up_dn, v_w_up_sw, v_w_o, v_post_mix_g, v_pre_mlp_g, v_w_ff1, v_w_ff2, v_post_mlp_g)
    names = ["x", "positions"] + _WEIGHTS + ["loss_target"] + ["m_" + n for n in _WEIGHTS] + ["v_" + n for n in _WEIGHTS]
    return _step(dict(zip(names, vals)))
```

```python
import functools

import numpy as np
import jax
import jax.numpy as jnp
from jax import lax
from jax.experimental import pallas as pl
from jax.experimental.pallas import tpu as pltpu

F32, BF16 = jnp.float32, jnp.bfloat16
MESH = pl.DeviceIdType.MESH

DN_HEADS = 8
DN_DK = 128
DN_CONV = 4
DN_CHUNK = 64
SW_Q_HEADS = 16
SW_KV_HEADS = 2
SW_HD = 64
SW_BLOCK = 128
ROPE_THETA = 500000.0
ROT_DIM = SW_HD // 4
EPS = 1e-6
ADAM_LR, ADAM_B1, ADAM_B2, ADAM_EPS, ADAM_WD, ADAM_STEP = 0.001, 0.9, 0.999, 1e-08, 0.01, 10

LANES = 128
SUBLANES = 8
VMEM_LIMIT = 48 * 1024 * 1024
KV_W = SW_KV_HEADS * SW_HD
BA_W = 256
PACK_ROWS = 512
DELTA_CB = 4
DELTA_HB = 8


def _pcall(body, **kw):
    return pl.pallas_call(body, **kw)


def _cp(*sem):
    return pltpu.CompilerParams(dimension_semantics=sem, vmem_limit_bytes=VMEM_LIMIT)


def _tile(n, pref, unit=LANES):
    if n <= pref:
        return n
    t = (pref // unit) * unit
    while t > unit and n % t:
        t -= unit
    assert n % t == 0, (n, pref)
    return t


def _sds(shape, dtype):
    return jax.ShapeDtypeStruct(tuple(shape), dtype)


_DIMS = {"nn": ((1,), (0,)), "nt": ((1,), (1,)), "tn": ((0,), (0,))}


def _mm(name, a, b, mode, out_dtype=F32, tm=1024, tn=1024, tk=1024, extras=(), epi=None, out_dtypes=None, slab=None):
    if mode == "nn":
        (M, K), (_, N) = a.shape, b.shape
    elif mode == "nt":
        (M, K), (N, _) = a.shape, b.shape
    else:
        (K, M), (_, N) = a.shape, b.shape
    tm, tn, tk = _tile(M, tm), _tile(N, tn), _tile(K, tk)
    nk = K // tk
    a_spec = {"nn": pl.BlockSpec((tm, tk), lambda i, j, k: (i, k)),
              "nt": pl.BlockSpec((tm, tk), lambda i, j, k: (i, k)),
              "tn": pl.BlockSpec((tk, tm), lambda i, j, k: (k, i))}[mode]
    b_spec = {"nn": pl.BlockSpec((tk, tn), lambda i, j, k: (k, j)),
              "nt": pl.BlockSpec((tn, tk), lambda i, j, k: (j, k)),
              "tn": pl.BlockSpec((tk, tn), lambda i, j, k: (k, j))}[mode]
    dims = (_DIMS[mode], ((), ()))
    out_dtypes = tuple(out_dtypes or (out_dtype,))
    ne, no = len(extras), len(out_dtypes)
    o_spec = pl.BlockSpec((tm, tn), lambda i, j, k: (i, j))

    def body(*refs):
        a_ref, b_ref, ex = refs[0], refs[1], refs[2:2 + ne]
        outs = refs[-no:] if nk == 1 else refs[-1 - no:-1]
        part = lax.dot_general(a_ref[...], b_ref[...], dims, preferred_element_type=F32)

        def finish(acc):
            res = epi(acc, *[e[...] for e in ex]) if epi else (acc,)
            for o, r, dt in zip(outs, res, out_dtypes):
                if slab is None:
                    o[...] = r.astype(dt)
                else:
                    o[0] = r.astype(dt)

        if nk == 1:
            finish(part)
            return
        acc_ref, k = refs[-1], pl.program_id(2)

        @pl.when(k == 0)
        def _():
            acc_ref[...] = part

        @pl.when((k > 0) & (k < nk - 1))
        def _():
            acc_ref[...] += part

        @pl.when(k == nk - 1)
        def _():
            finish(acc_ref[...] + part)

    kw = dict(name=name, grid=(M // tm, N // tn, nk), scratch_shapes=[] if nk == 1 else [pltpu.VMEM((tm, tn), F32)],
              compiler_params=_cp("parallel", "parallel", "arbitrary"))
    if slab is not None:
        buf, l = slab
        return _pcall(body, in_specs=[a_spec, b_spec, ANY], out_specs=pl.BlockSpec((1, tm, tn), lambda i, j, k: (l, i, j)),
                      out_shape=_sds(buf.shape, buf.dtype), input_output_aliases={2: 0}, **kw)(a, b, buf)
    out = _pcall(body, in_specs=[a_spec, b_spec] + [o_spec] * ne, out_specs=tuple(o_spec for _ in out_dtypes),
                 out_shape=tuple(_sds((M, N), dt) for dt in out_dtypes), **kw)(a, b, *extras)
    return out if no > 1 else out[0]


def _rows(name, fn, n_rows, tq, ins, in_specs, out_shapes, out_specs):
    def body(*refs):
        fn(pl.program_id(0), *refs)

    return _pcall(body, name=name, grid=(n_rows // tq,), in_specs=in_specs, out_specs=out_specs,
                  out_shape=out_shapes, compiler_params=_cp("arbitrary"))(*ins)


def _rb(tq, w, cb=0):
    return pl.BlockSpec((tq, w), lambda i: (i, cb))


def _full(shape):
    return pl.BlockSpec(tuple(shape), lambda *_: (0,) * len(shape))


def _rms_fwd(x, g):
    r = lax.rsqrt(jnp.mean(x * x, axis=-1, keepdims=True) + EPS)
    return x * r * g


def _rms_bwd(x, g, dy):
    r = lax.rsqrt(jnp.mean(x * x, axis=-1, keepdims=True) + EPS)
    xh = x * r
    t = dy * g
    dx = r * (t - xh * jnp.mean(t * xh, axis=-1, keepdims=True))
    return dx, jnp.sum(dy * xh, axis=0, keepdims=True)


def _acc(i, ref, val):
    @pl.when(i == 0)
    def _():
        ref[...] = val

    @pl.when(i > 0)
    def _():
        ref[...] += val


def _sigmoid(x):
    return 0.5 * jnp.tanh(0.5 * x) + 0.5


def _pre_norm(x, g):
    S, D = x.shape
    tq = _tile(S, 512, SUBLANES)

    def fn(i, x_ref, g_ref, h_ref):
        h_ref[...] = _rms_fwd(x_ref[...], g_ref[...]).astype(BF16)

    return _rows("pre_norm", fn, S, tq, (x, g), [_rb(tq, D), _full((1, D))], _sds((S, D), BF16), _rb(tq, D))


def _post_mix(x, mix, g2, g3):
    S, D = x.shape
    tq = _tile(S, 512, SUBLANES)

    def fn(i, x_ref, m_ref, g2_ref, g3_ref, x1_ref, h2_ref):
        x1 = x_ref[...] + _rms_fwd(m_ref[...], g2_ref[...])
        x1_ref[...] = x1
        h2_ref[...] = _rms_fwd(x1, g3_ref[...]).astype(BF16)

    return _rows("post_mix", fn, S, tq, (x, mix, g2, g3), [_rb(tq, D), _rb(tq, D), _full((1, D)), _full((1, D))],
                 (_sds((S, D), F32), _sds((S, D), BF16)), (_rb(tq, D), _rb(tq, D)))


def _post_mlp(x1, ff, g4):
    S, D = x1.shape
    tq = _tile(S, 512, SUBLANES)

    def fn(i, x_ref, f_ref, g_ref, o_ref):
        o_ref[...] = x_ref[...] + _rms_fwd(f_ref[...], g_ref[...])

    return _rows("post_mlp", fn, S, tq, (x1, ff, g4), [_rb(tq, D), _rb(tq, D), _full((1, D))], _sds((S, D), F32), _rb(tq, D))


def _loss_head(y, target):
    S, D = y.shape
    tq = _tile(S, 512, SUBLANES)

    def fn(i, y_ref, t_ref, l_ref, d_ref):
        e = y_ref[...] - t_ref[...]
        d_ref[...] = e * (1.0 / D)
        part = jnp.sum(jnp.sum(e * e, axis=1, keepdims=True), axis=0, keepdims=True) * (0.5 / D)
        _acc(i, l_ref, jnp.broadcast_to(part, (1, LANES)))

    return _rows("loss_head", fn, S, tq, (y, target), [_rb(tq, D), _rb(tq, D)],
                 (_sds((1, LANES), F32), _sds((S, D), F32)), (_full((1, LANES)), _rb(tq, D)))


def _post_mlp_bwd(ff, g4, dx2):
    S, D = ff.shape
    tq = _tile(S, 512, SUBLANES)

    def fn(i, f_ref, g_ref, d_ref, o_ref, dg_ref):
        dx, dg = _rms_bwd(f_ref[...], g_ref[...], d_ref[...])
        o_ref[...] = dx.astype(BF16)
        _acc(i, dg_ref, dg)

    return _rows("post_mlp_bwd", fn, S, tq, (ff, g4, dx2), [_rb(tq, D), _full((1, D)), _rb(tq, D)],
                 (_sds((S, D), BF16), _sds((1, D), F32)), (_rb(tq, D), _full((1, D))))


def _mid_bwd(x1, g3, dh2, dx2, mix, g2):
    S, D = x1.shape
    tq = _tile(S, 256, SUBLANES)

    def fn(i, x_ref, g3_ref, dh_ref, dx2_ref, m_ref, g2_ref, dx1_ref, dm_ref, dg3_ref, dg2_ref):
        d, dg3 = _rms_bwd(x_ref[...], g3_ref[...], dh_ref[...])
        dx1 = dx2_ref[...] + d
        dx1_ref[...] = dx1
        dm, dg2 = _rms_bwd(m_ref[...], g2_ref[...], dx1)
        dm_ref[...] = dm.astype(BF16)
        _acc(i, dg3_ref, dg3)
        _acc(i, dg2_ref, dg2)

    r, f = _rb(tq, D), _full((1, D))
    return _rows("mid_bwd", fn, S, tq, (x1, g3, dh2, dx2, mix, g2), [r, f, r, r, r, f],
                 (_sds((S, D), F32), _sds((S, D), BF16), _sds((1, D), F32), _sds((1, D), F32)), (r, r, f, f))


def _pre_norm_bwd(x, g1, dh, dx1):
    S, D = x.shape
    tq = _tile(S, 512, SUBLANES)

    def fn(i, x_ref, g_ref, dh_ref, dx1_ref, dx_ref, dg_ref):
        d, dg = _rms_bwd(x_ref[...], g_ref[...], dh_ref[...])
        dx_ref[...] = dx1_ref[...] + d
        _acc(i, dg_ref, dg)

    r, f = _rb(tq, D), _full((1, D))
    return _rows("pre_norm_bwd", fn, S, tq, (x, g1, dh, dx1), [r, f, r, r], (_sds((S, D), F32), _sds((1, D), F32)), (r, f))


def _mix(proj, ya, yb, D, cb_a):
    S = ya.shape[0]
    tq = _tile(S, 256, SUBLANES)

    def fn(i, ga_ref, gb_ref, ya_ref, yb_ref, o_ref):
        ga, gb, ya, yb = (r[...].astype(F32) for r in (ga_ref, gb_ref, ya_ref, yb_ref))
        o_ref[...] = (_sigmoid(ga) * ya + _sigmoid(gb) * yb).astype(BF16)

    return _rows("mix", fn, S, tq, (proj, proj, ya, yb), [_rb(tq, D, cb_a), _rb(tq, D, cb_a + 1), _rb(tq, D), _rb(tq, D)],
                 _sds((S, D), BF16), _rb(tq, D))


def _mix_bwd(proj, ya, yb, dmixin, D, cb_a):
    S = ya.shape[0]
    tq = _tile(S, 256, SUBLANES)

    def fn(i, ga_ref, gb_ref, ya_ref, yb_ref, d_ref, dya_ref, dyb_ref, dga_ref, dgb_ref):
        ga, gb, ya, yb, d = (r[...].astype(F32) for r in (ga_ref, gb_ref, ya_ref, yb_ref, d_ref))
        sa, sb = _sigmoid(ga), _sigmoid(gb)
        dya_ref[...] = (d * sa).astype(BF16)
        dyb_ref[...] = (d * sb).astype(BF16)
        dga_ref[...] = (d * ya * sa * (1.0 - sa)).astype(BF16)
        dgb_ref[...] = (d * yb * sb * (1.0 - sb)).astype(BF16)

    r = _rb(tq, D)
    o = _sds((S, D), BF16)
    return _rows("mix_bwd", fn, S, tq, (proj, proj, ya, yb, dmixin), [_rb(tq, D, cb_a), _rb(tq, D, cb_a + 1), r, r, r],
                 (o, o, o, o), (r, r, r, r))


HALO = 16


def _shift_down(xe, k, tq):
    return pltpu.roll(xe, k, 0)[HALO:HALO + tq]


def _conv_pre(cur_ref, halo_ref, w_ref, i, tq):
    x = cur_ref[...].astype(F32)
    halo = jnp.where(i > 0, halo_ref[...].astype(F32), 0.0)
    xe = jnp.concatenate([halo, x], axis=0)
    xs = [x] + [_shift_down(xe, k, tq) for k in range(1, DN_CONV)]
    w = w_ref[...]
    c = sum(w[DN_CONV - 1 - k:DN_CONV - k, :] * xs[k] for k in range(DN_CONV))
    return c, xs


def _dn_prep(proj, conv_w, W):
    S = proj.shape[0]
    tq = _tile(S, 256, HALO)
    hb = tq // HALO

    def body(cur_ref, halo_ref, w_ref, o_ref):
        j, i = pl.program_id(0), pl.program_id(1)
        c, _ = _conv_pre(cur_ref, halo_ref, w_ref, i, tq)
        y = c * _sigmoid(c)
        scale = jnp.where(j == 0, DN_DK ** -0.5, 1.0)
        for h in range(W // DN_DK):
            sl = slice(h * DN_DK, (h + 1) * DN_DK)
            yh = y[:, sl]
            rs = lax.rsqrt(jnp.sum(yh * yh, axis=-1, keepdims=True) + EPS)
            o_ref[:, sl] = jnp.where(j == 2, yh, yh * rs * scale)

    return _pcall(body, name="dn_prep", grid=(3, S // tq),
                  in_specs=[pl.BlockSpec((tq, W), lambda j, i: (i, j)),
                            pl.BlockSpec((HALO, W), lambda j, i: (jnp.maximum(i * hb - 1, 0), j)),
                            pl.BlockSpec((DN_CONV, W), lambda j, i: (0, j))],
                  out_specs=pl.BlockSpec((tq, W), lambda j, i: (i, j)), out_shape=_sds((S, 3 * W), F32),
                  compiler_params=_cp("arbitrary", "arbitrary"))(proj, proj, conv_w)


def _dn_prep_bwd_a(proj, conv_w, dqkv, W):
    S = proj.shape[0]
    tq = _tile(S, 256, HALO)
    hb = tq // HALO

    def body(cur_ref, halo_ref, w_ref, d_ref, dc_ref, dw_ref):
        j, i = pl.program_id(0), pl.program_id(1)
        c, xs = _conv_pre(cur_ref, halo_ref, w_ref, i, tq)
        sg = _sigmoid(c)
        y = c * sg
        scale = jnp.where(j == 0, DN_DK ** -0.5, 1.0)
        dout = d_ref[0]
        dys = []
        for h in range(W // DN_DK):
            sl = slice(h * DN_DK, (h + 1) * DN_DK)
            yh, dh = y[:, sl], dout[:, sl]
            rs = lax.rsqrt(jnp.sum(yh * yh, axis=-1, keepdims=True) + EPS)
            yn = yh * rs
            dn = scale * rs * (dh - yn * jnp.sum(dh * yn, axis=-1, keepdims=True))
            dys.append(jnp.where(j == 2, dh, dn))
        dy = jnp.concatenate(dys, axis=1)
        dc = dy * (sg * (1.0 + c * (1.0 - sg)))
        dc_ref[...] = dc
        dw = jnp.concatenate([jnp.sum(dc * xs[DN_CONV - 1 - r], axis=0, keepdims=True) for r in range(DN_CONV)], axis=0)
        _acc(i, dw_ref, dw)

    return _pcall(body, name="dn_prep_bwd_a", grid=(3, S // tq),
                  in_specs=[pl.BlockSpec((tq, W), lambda j, i: (i, j)),
                            pl.BlockSpec((HALO, W), lambda j, i: (jnp.maximum(i * hb - 1, 0), j)),
                            pl.BlockSpec((DN_CONV, W), lambda j, i: (0, j)),
                            pl.BlockSpec((1, tq, W), lambda j, i: (j, i, 0))],
                  out_specs=(pl.BlockSpec((tq, W), lambda j, i: (i, j)), pl.BlockSpec((DN_CONV, W), lambda j, i: (0, j))),
                  out_shape=(_sds((S, 3 * W), F32), _sds((DN_CONV, 3 * W), F32)),
                  compiler_params=_cp("arbitrary", "arbitrary"))(proj, proj, conv_w, dqkv)


def _dn_prep_bwd_b(dc, conv_w, W):
    S = dc.shape[0]
    tq = _tile(S, 256, SUBLANES)
    hb = tq // SUBLANES
    nblk = S // tq

    def body(cur_ref, nxt_ref, w_ref, o_ref):
        i = pl.program_id(1)
        d = cur_ref[...]
        nxt = jnp.where(i < nblk - 1, nxt_ref[...], 0.0)
        de = jnp.concatenate([d, nxt], axis=0)
        w = w_ref[...]
        out = w[DN_CONV - 1:DN_CONV, :] * d
        for k in range(1, DN_CONV):
            out = out + w[DN_CONV - 1 - k:DN_CONV - k, :] * pltpu.roll(de, tq + SUBLANES - k, 0)[0:tq]
        o_ref[...] = out.astype(BF16)

    return _pcall(body, name="dn_prep_bwd_b", grid=(3, nblk),
                  in_specs=[pl.BlockSpec((tq, W), lambda j, i: (i, j)),
                            pl.BlockSpec((SUBLANES, W), lambda j, i: (jnp.minimum((i + 1) * hb, S // SUBLANES - 1), j)),
                            pl.BlockSpec((DN_CONV, W), lambda j, i: (0, j))],
                  out_specs=pl.BlockSpec((tq, W), lambda j, i: (i, j)), out_shape=_sds((S, 3 * W), BF16),
                  compiler_params=_cp("arbitrary", "arbitrary"))(dc, dc, conv_w)


def _gate_terms(ba, al, dt):
    u = ba + dt
    sp = jnp.maximum(u, 0.0) + jnp.log(1.0 + jnp.exp(-jnp.abs(u)))
    return _sigmoid(ba), -jnp.exp(al) * sp, u


def _dn_gates(ba, alog_row, dt_row, H):
    S = ba.shape[0]
    tq = _tile(S, 512, SUBLANES)
    W = H * DN_DK

    def fn(i, ba_ref, al_ref, dt_ref, be_ref, g_ref):
        bet, gg, _ = _gate_terms(ba_ref[...], al_ref[...], dt_ref[...])
        for h in range(H):
            sl = slice(h * DN_DK, (h + 1) * DN_DK)
            be_ref[:, sl] = jnp.broadcast_to(bet[:, h:h + 1], (tq, DN_DK))
            g_ref[:, sl] = jnp.broadcast_to(gg[:, H + h:H + h + 1], (tq, DN_DK))

    return _rows("dn_gates", fn, S, tq, (ba, alog_row, dt_row), [_rb(tq, LANES), _full((1, LANES)), _full((1, LANES))],
                 (_sds((S, W), F32), _sds((S, W), F32)), (_rb(tq, W), _rb(tq, W)))


def _dn_gates_bwd(ba, alog_row, dt_row, dbeta_b, dg_b, H):
    S = ba.shape[0]
    tq = _tile(S, 512, SUBLANES)
    W = H * DN_DK

    def fn(i, ba_ref, al_ref, dt_ref, db_ref, dg_ref, o_ref, dal_ref, ddt_ref):
        bet, gg, u = _gate_terms(ba_ref[...], al_ref[...], dt_ref[...])
        lane = lax.broadcasted_iota(jnp.int32, (tq, LANES), 1)
        d = jnp.zeros((tq, LANES), F32)
        for h in range(H):
            d = jnp.where(lane == h, db_ref[:, h * DN_DK:h * DN_DK + 1], d)
            d = jnp.where(lane == H + h, dg_ref[:, h * DN_DK:h * DN_DK + 1], d)
        is_a = (lane >= H) & (lane < 2 * H)
        da = jnp.where(is_a, d * (-jnp.exp(al_ref[...]) * _sigmoid(u)), 0.0)
        dlog = jnp.where(lane < H, d * bet * (1.0 - bet), da)
        o_ref[...] = jnp.concatenate([dlog, jnp.zeros((tq, BA_W - LANES), F32)], axis=1).astype(BF16)
        _acc(i, dal_ref, jnp.sum(jnp.where(is_a, d * gg, 0.0), axis=0, keepdims=True))
        _acc(i, ddt_ref, jnp.sum(da, axis=0, keepdims=True))

    f = _full((1, LANES))
    return _rows("dn_gates_bwd", fn, S, tq, (ba, alog_row, dt_row, dbeta_b, dg_b),
                 [_rb(tq, LANES), f, f, _rb(tq, W), _rb(tq, W)],
                 (_sds((S, BA_W), BF16), _sds((1, LANES), F32), _sds((1, LANES), F32)), (_rb(tq, BA_W), f, f))


def _dn_out(o, proj, ng, W, cb_z):
    S = o.shape[0]
    tq = _tile(S, 256, SUBLANES)

    def fn(i, o_ref, z_ref, g_ref, y_ref):
        for h in range(W // DN_DK):
            sl = slice(h * DN_DK, (h + 1) * DN_DK)
            z = z_ref[:, sl].astype(F32)
            y_ref[:, sl] = (_rms_fwd(o_ref[:, sl], g_ref[...]) * (z * _sigmoid(z))).astype(BF16)

    return _rows("dn_out", fn, S, tq, (o, proj, ng), [_rb(tq, W), _rb(tq, W, cb_z), _full((1, DN_DK))], _sds((S, W), BF16), _rb(tq, W))


def _dn_out_bwd(o, proj, ng, dy, W, cb_z):
    S = o.shape[0]
    tq = _tile(S, 256, SUBLANES)

    def fn(i, o_ref, z_ref, g_ref, d_ref, do_ref, dz_ref, dg_ref):
        g = g_ref[...]
        dg = jnp.zeros((1, DN_DK), F32)
        for h in range(W // DN_DK):
            sl = slice(h * DN_DK, (h + 1) * DN_DK)
            oh, z, d = o_ref[:, sl], z_ref[:, sl].astype(F32), d_ref[:, sl].astype(F32)
            sg = _sigmoid(z)
            dn = d * (z * sg)
            dz_ref[:, sl] = (d * _rms_fwd(oh, g) * (sg * (1.0 + z * (1.0 - sg)))).astype(BF16)
            dx, dgh = _rms_bwd(oh, g, dn)
            do_ref[:, sl] = dx
            dg = dg + dgh
        _acc(i, dg_ref, dg)

    r = _rb(tq, W)
    return _rows("dn_out_bwd", fn, S, tq, (o, proj, ng, dy), [r, _rb(tq, W, cb_z), _full((1, DN_DK)), r],
                 (_sds((S, W), F32), _sds((S, W), BF16), _sds((1, DN_DK), F32)), (r, r, _full((1, DN_DK))))


def _bdot(a, b, mode="nn"):
    return lax.dot_general(a.astype(BF16), b.astype(BF16), (_DIMS[mode], ((), ())), preferred_element_type=F32)


def _rsum(x):
    return jnp.broadcast_to(jnp.sum(x, axis=-1, keepdims=True), x.shape)


def _dot3(a, b, mode="nn"):
    ah, bh = a.astype(BF16), b.astype(BF16)
    al, bl = (a - ah.astype(F32)).astype(BF16), (b - bh.astype(F32)).astype(BF16)
    d = lambda x, y: lax.dot_general(x, y, (_DIMS[mode], ((), ())), preferred_element_type=F32)
    return d(ah, bh) + (d(al, bh) + d(ah, bl))


def _cumsum_rows(x, reverse=False):
    n = x.shape[0]
    row = lax.broadcasted_iota(jnp.int32, x.shape, 0)
    s = 1
    while s < n:
        if reverse:
            x = x + jnp.where(row < n - s, pltpu.roll(x, n - s, 0), 0.0)
        else:
            x = x + jnp.where(row >= s, pltpu.roll(x, s, 0), 0.0)
        s *= 2
    return x


def _each(f, *lists):
    return [f(*a) for a in zip(*lists)]


def _delta_local(qs, ks, vs, bes, grs):
    C = DN_CHUNK
    ri = lax.broadcasted_iota(jnp.int32, (C, C), 0)
    ci = lax.broadcasted_iota(jnp.int32, (C, C), 1)
    causal, strict = ri >= ci, ri > ci
    gcs = [_cumsum_rows(g) for g in grs]
    decays = [jnp.where(causal, jnp.exp(jnp.where(causal, gc[:, :C] - gc.T[:C, :], 0.0)), 0.0) for gc in gcs]
    egs = [jnp.exp(gc) for gc in gcs]
    eks = [jnp.exp(gc[C - 1:C, :] - gc) for gc in gcs]
    gams = [jnp.exp(gc[C - 1:C, :]) for gc in gcs]
    kbs = _each(lambda k, be: k * be, ks, bes)
    kks = _each(lambda kb, k: _bdot(kb, k, "nt"), kbs, ks)
    nls = _each(lambda kk, dc: jnp.where(strict, -kk * dc, 0.0), kks, decays)
    eye = (ri == ci).astype(F32)
    ts = [eye + nl for nl in nls]
    pws = [_dot3(nl, nl) for nl in nls]
    for s in range(4):
        both = _each(lambda t, pw: _dot3(jnp.concatenate([t, pw], axis=0), pw), ts, pws)
        ts = _each(lambda t, b: t + b[:C], ts, both)
        pws = [b[C:] for b in both]
    ts = _each(lambda t, pw: t + _dot3(t, pw), ts, pws)
    vbs = _each(lambda v, be: v * be, vs, bes)
    kbes = _each(lambda kb, eg: kb * eg, kbs, egs)
    uws = _each(lambda t, vb, kbe: _dot3(t, jnp.concatenate([vb, kbe], axis=1)), ts, vbs, kbes)
    us, ws = [uw[:, :DN_DK] for uw in uws], [uw[:, DN_DK:] for uw in uws]
    qks = _each(lambda q, k: _bdot(q, k, "nt"), qs, ks)
    return dict(decay=decays, eg=egs, ek=eks, gam=gams, kb=kbs, kk=kks, t=ts, vb=vbs, kbe=kbes, u=us, w=ws, qk=qks,
                a=_each(lambda qk, dc: qk * dc, qks, decays), qd=_each(lambda q, eg: q * eg, qs, egs),
                kd=_each(lambda k, ek: k * ek, ks, eks), strict=strict)


def _delta_items(refs, CB, HB):
    C, dk = DN_CHUNK, DN_DK
    return [[r[c * C:(c + 1) * C, h * dk:(h + 1) * dk] for h in range(HB) for c in range(CB)] for r in refs]


def _delta_fwd(qkv, beta_b, g_b, H, CB, HB):
    S = qkv.shape[0]
    C, dk = DN_CHUNK, DN_DK
    N = S // C
    R = CB * C
    G = H // HB

    def body(q_ref, k_ref, v_ref, b_ref, g_ref, o_ref, st_ref, s_ref):
        @pl.when(pl.program_id(1) == 0)
        def _():
            s_ref[...] = jnp.zeros((HB, dk, dk), F32)

        L = _delta_local(*_delta_items((q_ref, k_ref, v_ref, b_ref, g_ref), CB, HB))
        ss = [s_ref[h] for h in range(HB)]
        for c in range(CB):
            it = [h * CB + c for h in range(HB)]
            for h in range(HB):
                st_ref[h, c] = ss[h]
            wq = [_bdot(jnp.concatenate([L["w"][i], L["qd"][i]], axis=0), s) for i, s in zip(it, ss)]
            vns = [L["u"][i] - x[:C] for i, x in zip(it, wq)]
            outs = [x[C:] + _bdot(L["a"][i], vn) for i, x, vn in zip(it, wq, vns)]
            ss = [s * L["gam"][i] + _bdot(L["kd"][i], vn, "tn") for i, s, vn in zip(it, ss, vns)]
            for h in range(HB):
                o_ref[c * C:(c + 1) * C, h * dk:(h + 1) * dk] = outs[h]
        for h in range(HB):
            s_ref[h] = ss[h]

    blk = lambda off: pl.BlockSpec((R, HB * dk), lambda h, n: (n, off + h))
    return _pcall(body, name="delta_fwd", grid=(G, N // CB),
                  in_specs=[blk(0), blk(G), blk(2 * G), blk(0), blk(0)],
                  out_specs=(blk(0), pl.BlockSpec((HB, CB, dk, dk), lambda h, n: (h, n, 0, 0))),
                  out_shape=(_sds((S, H * dk), F32), _sds((H, N, dk, dk), F32)),
                  scratch_shapes=[pltpu.VMEM((HB, dk, dk), F32)],
                  compiler_params=_cp("arbitrary", "arbitrary"))(qkv, qkv, qkv, beta_b, g_b)


def _delta_bwd(qkv, beta_b, g_b, states, do, H, CB, HB):
    S = qkv.shape[0]
    C, dk = DN_CHUNK, DN_DK
    N = S // C
    R = CB * C
    NB = N // CB
    G = H // HB

    def body(q_ref, k_ref, v_ref, b_ref, g_ref, st_ref, do_ref, dqkv_ref, db_ref, dg_ref, ds_ref):
        @pl.when(pl.program_id(1) == 0)
        def _():
            ds_ref[...] = jnp.zeros((HB, dk, dk), F32)

        qs, ks, vs, bes, grs, dos = _delta_items((q_ref, k_ref, v_ref, b_ref, g_ref, do_ref), CB, HB)
        L = _delta_local(qs, ks, vs, bes, grs)
        ts, decays, kbs, egs, eks, gams, qds, kds = (L[n] for n in ("t", "decay", "kb", "eg", "ek", "gam", "qd", "kd"))
        s0s = [st_ref[h, c] for h in range(HB) for c in range(CB)]
        vns = _each(lambda u, w, s0: u - _bdot(w, s0), L["u"], L["w"], s0s)
        pre_dvn = _each(lambda a, d: _bdot(a, d, "tn"), L["a"], dos)
        pre_ds = _each(lambda qd, d: _bdot(qd, d, "tn"), qds, dos)
        das = _each(lambda d, vn: _bdot(d, vn, "nt"), dos, vns)
        ds = [ds_ref[h] for h in range(HB)]
        ds1s, dvns = [None] * (HB * CB), [None] * (HB * CB)
        for c in reversed(range(CB)):
            it = [h * CB + c for h in range(HB)]
            new = [pre_dvn[i] + _bdot(kds[i], d) for i, d in zip(it, ds)]
            for i, d, dv in zip(it, ds, new):
                ds1s[i], dvns[i] = d, dv
            ds = [pre_ds[i] + d * gams[i] - _bdot(L["w"][i], dv, "tn") for i, d, dv in zip(it, ds, new)]
        for h in range(HB):
            ds_ref[h] = ds[h]
        dkds = _each(lambda vn, d1: _bdot(vn, d1, "nt"), vns, ds1s)
        dgams = _each(lambda s0, d1: jnp.sum(jnp.sum(s0 * d1, axis=1, keepdims=True), axis=0, keepdims=True), s0s, ds1s)
        ost = _each(lambda d, dv, s0: _bdot(jnp.concatenate([d, dv], axis=0), s0, "nt"), dos, dvns, s0s)
        dqds, dws = [x[:C] for x in ost], [-x[C:] for x in ost]
        dvw = _each(lambda dv, dw: jnp.concatenate([dv, dw], axis=1), dvns, dws)
        tdvw = _each(lambda t, x: _dot3(t, x, "tn"), ts, dvw)
        dvbs, dkbes = [x[:, :dk] for x in tdvw], [x[:, dk:] for x in tdvw]
        dts = _each(lambda x, vb, kbe: _dot3(x, jnp.concatenate([vb, kbe], axis=1), "nt"), dvw, L["vb"], L["kbe"])
        tmp = _each(lambda dt, t: _dot3(dt, t, "nt"), dts, ts)
        dls = _each(lambda t, x: -_dot3(t, x, "tn"), ts, tmp)
        ms = _each(lambda dl, dc: jnp.where(L["strict"], dl * dc, 0.0), dls, decays)
        mas = _each(lambda da, dc: da * dc, das, decays)
        dkbs = _each(lambda m, k, dkbe, eg: _bdot(m, k) + dkbe * eg, ms, ks, dkbes, egs)
        dks = _each(lambda m, kb, ma, q, dkd, ek, dkb, be: _bdot(m, kb, "tn") + _bdot(ma, q, "tn") + dkd * ek + dkb * be,
                    ms, kbs, mas, qs, dkds, eks, dkbs, bes)
        dqs = _each(lambda ma, k, dqd, eg: _bdot(ma, k) + dqd * eg, mas, ks, dqds, egs)
        es = _each(lambda m, kk, ma, qk: m * kk + ma * qk, ms, L["kk"], mas, L["qk"])
        ones = jnp.ones((C, dk), BF16)
        row = lax.broadcasted_iota(jnp.int32, (C, dk), 0)
        for i in range(HB * CB):
            h, c = divmod(i, CB)
            rs, cs = slice(c * C, (c + 1) * C), slice(h * dk, (h + 1) * dk)
            e = es[i]
            e_hi = e.astype(BF16)
            col = _bdot(e_hi, ones, "tn") + _bdot(e - e_hi.astype(F32), ones, "tn")
            t_kd = _rsum(dkds[i] * kds[i])
            dgc = (jnp.broadcast_to(jnp.sum(e, axis=1, keepdims=True), (C, dk)) - col + _rsum(dqds[i] * qds[i]) - t_kd
                   + _rsum(dkbes[i] * L["kbe"][i]))
            dglast = jnp.sum(t_kd[:, 0:1], axis=0, keepdims=True) + dgams[i] * gams[i][:, 0:1]
            dgc = dgc + jnp.where(row == C - 1, dglast, 0.0)
            dqkv_ref[0, rs, cs] = dqs[i]
            dqkv_ref[1, rs, cs] = dks[i]
            dqkv_ref[2, rs, cs] = dvbs[i] * bes[i]
            db_ref[rs, cs] = _rsum(dkbs[i] * ks[i]) + _rsum(dvbs[i] * vs[i])
            dg_ref[rs, cs] = _cumsum_rows(dgc, reverse=True)

    blk = lambda off: pl.BlockSpec((R, HB * dk), lambda h, n: (NB - 1 - n, off + h))
    W = H * dk
    return _pcall(body, name="delta_bwd", grid=(G, NB),
                  in_specs=[blk(0), blk(G), blk(2 * G), blk(0), blk(0),
                            pl.BlockSpec((HB, CB, dk, dk), lambda h, n: (h, NB - 1 - n, 0, 0)), blk(0)],
                  out_specs=(pl.BlockSpec((3, R, HB * dk), lambda h, n: (0, NB - 1 - n, h)), blk(0), blk(0)),
                  out_shape=(_sds((3, S, W), F32), _sds((S, W), F32), _sds((S, W), F32)),
                  scratch_shapes=[pltpu.VMEM((HB, dk, dk), F32)],
                  compiler_params=_cp("arbitrary", "arbitrary"))(qkv, qkv, qkv, beta_b, g_b, states, do)


def _rope_consts():
    lane = np.arange(LANES) % SW_HD
    half = ROT_DIM // 2
    inv = (ROPE_THETA ** (-np.arange(half, dtype=np.float32) * np.float32(2.0 / ROT_DIM))).astype(np.float32)
    freq = np.where(lane < ROT_DIM, inv[lane % half], 0.0).astype(np.float32)
    lo = (lane < half).astype(np.float32)
    hi = ((lane >= half) & (lane < ROT_DIM)).astype(np.float32)
    return jnp.asarray(np.stack([freq, -lo, hi] + [np.zeros(LANES, np.float32)] * 5))


def _rope_tables(pos_col):
    S = pos_col.shape[0]
    tq = _tile(S, 1024, SUBLANES)

    def fn(i, p_ref, c_ref, cos_ref, s1_ref, s2_ref):
        ang = p_ref[...].astype(F32) * c_ref[0:1, :]
        sn = jnp.sin(ang)
        cos_ref[...] = jnp.cos(ang)
        s1_ref[...] = sn * c_ref[1:2, :]
        s2_ref[...] = sn * c_ref[2:3, :]

    o, r = _sds((S, LANES), F32), _rb(tq, LANES)
    return _rows("rope_tables", fn, S, tq, (pos_col, _rope_consts()), [_rb(tq, 1), _full((SUBLANES, LANES))], (o, o, o), (r, r, r))


def _wide(a, w):
    return a if w == LANES else jnp.tile(a, (1, w // LANES))


def _rope(x, cos, s1, s2):
    w, h = x.shape[1], ROT_DIM // 2
    return x * _wide(cos, w) + pltpu.roll(x, w - h, 1) * _wide(s1, w) + pltpu.roll(x, h, 1) * _wide(s2, w)


def _unrope(d, cos, s1, s2):
    w, h = d.shape[1], ROT_DIM // 2
    return d * _wide(cos, w) + pltpu.roll(d * _wide(s1, w), h, 1) + pltpu.roll(d * _wide(s2, w), w - h, 1)


def _swa_setup(n, q_ref, kc_ref, kp_ref, vc_ref, vp_ref, tc, tp):
    B = SW_BLOCK
    qr = _rope(q_ref[...].astype(F32), tc[0][...], tc[1][...], tc[2][...]) * (SW_HD ** -0.5)
    kw = jnp.concatenate([_rope(kp_ref[...].astype(F32), tp[0][...], tp[1][...], tp[2][...]),
                          _rope(kc_ref[...].astype(F32), tc[0][...], tc[1][...], tc[2][...])], axis=0)
    vw = jnp.concatenate([vp_ref[...], vc_ref[...]], axis=0).astype(F32)
    lane = lax.broadcasted_iota(jnp.int32, (2 * B, LANES), 1)
    heads = []
    for hk in range(SW_KV_HEADS):
        kh, vh = kw[:, hk * SW_HD:(hk + 1) * SW_HD], vw[:, hk * SW_HD:(hk + 1) * SW_HD]
        kk, vv = jnp.concatenate([kh, kh], axis=1), jnp.concatenate([vh, vh], axis=1)
        heads.append(tuple(jnp.where(sel, t, 0.0).astype(BF16) for t in (kk, vv) for sel in (lane < SW_HD, lane >= SW_HD)))
    qi = lax.broadcasted_iota(jnp.int32, (B, 2 * B), 0) + B
    ki = lax.broadcasted_iota(jnp.int32, (B, 2 * B), 1)
    off = qi - ki
    ok = (off >= 0) & (off < SW_BLOCK) & ((ki >= B) | (n > 0))
    return qr, heads, jnp.where(ok, 0.0, -1e30), lane


SWA_GROUPS = 4
SWA_GROUPS_BWD = 2


def _swa_probs(items, qs, heads, bias, sk_ref, G2):
    ss = [_bdot(qs[j], heads[j // G2][half], "nt") + bias for j, half in items]
    sks = [sk_ref[0:1, 2 * j + half:2 * j + half + 1] for j, half in items]
    ms = [jnp.maximum(jnp.max(s, axis=-1, keepdims=True), sk) for s, sk in zip(ss, sks)]
    ps = [jnp.exp(s - m) for s, m in zip(ss, ms)]
    es = [jnp.exp(sk - m) for sk, m in zip(sks, ms)]
    inv = [1.0 / (jnp.sum(p, axis=-1, keepdims=True) + e) for p, e in zip(ps, es)]
    return [p * i for p, i in zip(ps, inv)], [e * i for e, i in zip(es, inv)]


def _swa_specs(W, cb_q, cb_k):
    B = SW_BLOCK
    assert (W // LANES) % SWA_GROUPS == 0 and (W // LANES) % SWA_GROUPS_BWD == 0
    cur = lambda w, cb: pl.BlockSpec((B, w), lambda n: (n, cb))
    prv = lambda w, cb: pl.BlockSpec((B, w), lambda n: (jnp.maximum(n - 1, 0), cb))
    specs = [cur(W, cb_q), cur(LANES, cb_k), prv(LANES, cb_k), cur(LANES, cb_k + 1), prv(LANES, cb_k + 1)]
    return specs + [cur(LANES, 0)] * 3 + [prv(LANES, 0)] * 3 + [_full((1, LANES))]


def _swa_fwd(proj, tabs, sinks_row, W, cb_q, cb_k):
    S = proj.shape[0]
    G2 = SW_Q_HEADS // SW_KV_HEADS // 2

    def body(q_ref, kc_ref, kp_ref, vc_ref, vp_ref, c0, c1, c2, p0, p1, p2, sk_ref, o_ref):
        n = pl.program_id(0)
        qr, heads, bias, _ = _swa_setup(n, q_ref, kc_ref, kp_ref, vc_ref, vp_ref, (c0, c1, c2), (p0, p1, p2))
        qs = [qr[:, j * LANES:(j + 1) * LANES].astype(BF16) for j in range(W // LANES)]
        for j0 in range(0, W // LANES, SWA_GROUPS):
            items = [(j, half) for j in range(j0, j0 + SWA_GROUPS) for half in range(2)]
            probs, _ = _swa_probs(items, qs, heads, bias, sk_ref, G2)
            pv = [_bdot(p, heads[j // G2][2 + half]) for p, (j, half) in zip(probs, items)]
            for g in range(SWA_GROUPS):
                o_ref[:, (j0 + g) * LANES:(j0 + g + 1) * LANES] = (pv[2 * g] + pv[2 * g + 1]).astype(BF16)

    t = tuple(tabs)
    return _pcall(body, name="swa_fwd", grid=(S // SW_BLOCK,), in_specs=_swa_specs(W, cb_q, cb_k),
                  out_specs=pl.BlockSpec((SW_BLOCK, W), lambda n: (n, 0)), out_shape=_sds((S, W), BF16),
                  compiler_params=_cp("arbitrary"))(proj, proj, proj, proj, proj, *t, *t, sinks_row)


def _swa_bwd(proj, tabs, sinks_row, do, W, cb_q, cb_k):
    S = proj.shape[0]
    B = SW_BLOCK
    G2 = SW_Q_HEADS // SW_KV_HEADS // 2
    SKR = -(-SW_Q_HEADS // SUBLANES) * SUBLANES

    def body(q_ref, kc_ref, kp_ref, vc_ref, vp_ref, c0, c1, c2, p0, p1, p2, sk_ref, do_ref,
             dq_ref, dkc_ref, dkp_ref, dvc_ref, dvp_ref, dsk_ref):
        n = pl.program_id(0)
        qr, heads, bias, lane = _swa_setup(n, q_ref, kc_ref, kp_ref, vc_ref, vp_ref, (c0, c1, c2), (p0, p1, p2))

        @pl.when(n == 0)
        def _():
            dsk_ref[...] = jnp.zeros((SKR, LANES), F32)

        acc_k = [jnp.zeros((2 * B, LANES), F32) for _ in range(SW_KV_HEADS)]
        acc_v = [jnp.zeros((2 * B, LANES), F32) for _ in range(SW_KV_HEADS)]
        qs = [qr[:, j * LANES:(j + 1) * LANES].astype(BF16) for j in range(W // LANES)]
        dos = [do_ref[:, j * LANES:(j + 1) * LANES].astype(BF16) for j in range(W // LANES)]
        dqs = []
        for j0 in range(0, W // LANES, SWA_GROUPS_BWD):
            items = [(j, half) for j in range(j0, j0 + SWA_GROUPS_BWD) for half in range(2)]
            probs, psinks = _swa_probs(items, qs, heads, bias, sk_ref, G2)
            dps = [_bdot(dos[j], heads[j // G2][2 + half], "nt") for j, half in items]
            deltas = [jnp.sum(p * dp, axis=-1, keepdims=True) for p, dp in zip(probs, dps)]
            dss = [(p * (dp - dl)).astype(BF16) for p, dp, dl in zip(probs, dps, deltas)]
            pbs = [p.astype(BF16) for p in probs]
            dqp = [_bdot(ds, heads[j // G2][half]) for ds, (j, half) in zip(dss, items)]
            dkk = [_bdot(ds, qs[j], "tn") for ds, (j, half) in zip(dss, items)]
            dvv = [_bdot(p, dos[j], "tn") for p, (j, half) in zip(pbs, items)]
            for i, (j, half) in enumerate(items):
                hk, h = j // G2, 2 * j + half
                sel = (lane < SW_HD) if half == 0 else (lane >= SW_HD)
                acc_k[hk] = acc_k[hk] + jnp.where(sel, dkk[i], 0.0)
                acc_v[hk] = acc_v[hk] + jnp.where(sel, dvv[i], 0.0)
                dsk_ref[h:h + 1, :] += jnp.broadcast_to(-jnp.sum(psinks[i] * deltas[i], axis=0, keepdims=True), (1, LANES))
            dqs += [dqp[2 * g] + dqp[2 * g + 1] for g in range(SWA_GROUPS_BWD)]
        dq = jnp.concatenate(dqs, axis=1) * (SW_HD ** -0.5)
        dq_ref[...] = _unrope(dq, c0[...], c1[...], c2[...]).astype(BF16)
        fold = lambda a: a[:, :SW_HD] + a[:, SW_HD:]
        dkw = jnp.concatenate([fold(a) for a in acc_k], axis=1)
        dvw = jnp.concatenate([fold(a) for a in acc_v], axis=1)
        dkp_ref[...], dkc_ref[...] = dkw[:B], dkw[B:]
        dvp_ref[...], dvc_ref[...] = dvw[:B], dvw[B:]

    t = tuple(tabs)
    blk = lambda w: pl.BlockSpec((B, w), lambda n: (n, 0))
    o = _sds((S, LANES), F32)
    return _pcall(body, name="swa_bwd", grid=(S // B,), in_specs=_swa_specs(W, cb_q, cb_k) + [blk(W)],
                  out_specs=(blk(W), blk(LANES), blk(LANES), blk(LANES), blk(LANES), _full((SKR, LANES))),
                  out_shape=(_sds((S, W), BF16), o, o, o, o, _sds((SKR, LANES), F32)),
                  compiler_params=_cp("arbitrary"))(proj, proj, proj, proj, proj, *t, *t, sinks_row, do)


def _swa_kv_combine(dkc, dkp, dvc, dvp, tabs):
    S = dkc.shape[0]
    B = SW_BLOCK
    nb = S // B

    def fn(n, kc_ref, kp_ref, vc_ref, vp_ref, c0, c1, c2, o_ref):
        more = n < nb - 1
        dk = kc_ref[...] + jnp.where(more, kp_ref[...], 0.0)
        dv = vc_ref[...] + jnp.where(more, vp_ref[...], 0.0)
        o_ref[...] = jnp.concatenate([_unrope(dk, c0[...], c1[...], c2[...]), dv], axis=1).astype(BF16)

    cur = _rb(B, LANES)
    nxt = pl.BlockSpec((B, LANES), lambda n: (jnp.minimum(n + 1, nb - 1), 0))
    return _rows("swa_kv_combine", fn, S, B, (dkc, dkp, dvc, dvp, *tabs), [cur, nxt, cur, nxt, cur, cur, cur],
                 _sds((S, 2 * LANES), BF16), _rb(B, 2 * LANES))


ANY = pl.BlockSpec(memory_space=pl.ANY)


def _place():
    x, y, c = lax.axis_index("x"), lax.axis_index("y"), lax.axis_index("c")
    return x, y, c, [(1 - x, y), (x, 1 - y), (1 - x, 1 - y)]


def _comm_call(name, body, out_shapes, n_sems, n_local, *ins):
    return _pcall(body, name=name, out_shape=tuple(out_shapes), in_specs=[ANY] * len(ins), out_specs=tuple(ANY for _ in out_shapes),
                  scratch_shapes=[pltpu.SemaphoreType.DMA((n_sems,)), pltpu.SemaphoreType.DMA((n_sems,)),
                                  pltpu.SemaphoreType.DMA((n_local,))])(*ins)


def _remote(src, dst, send, recv, k, to):
    return pltpu.make_async_remote_copy(src_ref=src, dst_ref=dst, send_sem=send.at[k], recv_sem=recv.at[k], device_id=to,
                                        device_id_type=MESH)


def _gather_chips(name, arrs):
    n = len(arrs)
    Lh = arrs[0].shape[0] // 2

    def body(*refs):
        w, o, (send, recv, _) = refs[:n], refs[n:2 * n], refs[2 * n:]
        x, y, c, chips = _place()
        me, sib = 2 * x + y, (x, y, 1 - c)
        own, other = pl.ds(c * Lh, Lh), pl.ds((1 - c) * Lh, Lh)
        idx = [2 * cx + cy for cx, cy in chips]
        first = [[_remote(w[a].at[own], o[a].at[me, own], send, recv, 6 * a + j, (*chips[j], c)) for j in range(3)] for a in range(n)]
        passed = [[_remote(o[a].at[idx[j], own], o[a].at[idx[j], own], send, recv, 6 * a + 3 + j, sib) for j in range(3)] for a in range(n)]
        for cp in [cp for row in first for cp in row]:
            cp.start()
        for j in range(3):
            for a in range(n):
                _remote(w[a].at[own], o[a].at[idx[j], own], send, recv, 6 * a + j, (*chips[j], c)).wait_recv()
                passed[a][j].start()
        for j in range(3):
            for a in range(n):
                _remote(w[a].at[other], o[a].at[idx[j], other], send, recv, 6 * a + 3 + j, sib).wait_recv()
        for cp in [cp for row in first + passed for cp in row]:
            cp.wait_send()

    return _comm_call(name, body, [_sds((4,) + a.shape, a.dtype) for a in arrs], 6 * n, 1, *arrs)


def _pair_swap(name, arrs, whole=False):
    n = len(arrs)
    Lh = arrs[0].shape[0] if whole else arrs[0].shape[0] // 2

    def body(*refs):
        g, o, (send, recv, _) = refs[:n], refs[n:2 * n], refs[2 * n:]
        x, y, c, _ = _place()
        cps = [_remote(g[a] if whole else g[a].at[pl.ds((1 - c) * Lh, Lh)], o[a], send, recv, a, (x, y, 1 - c)) for a in range(n)]
        for cp in cps:
            cp.start()
        for cp in cps:
            cp.wait()

    return _comm_call(name, body, [_sds((Lh,) + a.shape[1:], a.dtype) for a in arrs], n, 1, *arrs)


def _chip_slice(ref, axis, s):
    if axis is None:
        return ref.at[s]
    q = ref.shape[axis] // 4
    start = s * q if isinstance(s, int) else pl.multiple_of(s * q, q)
    return ref.at[tuple([slice(None)] * axis + [pl.ds(start, q)])]


def _scatter_chips(name, items):
    n = len(items)
    part = lambda a, ax: a.shape[1:] if ax is None else tuple(d // 4 if i == ax else d for i, d in enumerate(a.shape))

    def body(*refs):
        p, o, (send, recv, _) = refs[:n], refs[n:2 * n], refs[2 * n:]
        x, y, c, chips = _place()
        me = 2 * x + y
        idx = [2 * cx + cy for cx, cy in chips]
        cps = [_remote(_chip_slice(p[a], items[a][1], idx[j]), o[a].at[me], send, recv, 3 * a + j, (*chips[j], c))
               for a in range(n) for j in range(3)]
        for cp in cps:
            cp.start()
        for a in range(n):
            for j in range(3):
                _remote(_chip_slice(p[a], items[a][1], me), o[a].at[idx[j]], send, recv, 3 * a + j, (*chips[j], c)).wait_recv()
        for cp in cps:
            cp.wait_send()

    return _comm_call(name, body, [_sds((4,) + part(a, ax), a.dtype) for a, ax in items], 3 * n, 1, *[a for a, _ in items])


def _own_part(a, axis, me):
    if axis is None:
        return lax.dynamic_index_in_dim(a, me, 0, keepdims=False)
    q = a.shape[axis] // 4
    return lax.dynamic_slice_in_dim(a, me * q, q, axis)


HBM = pl.BlockSpec(memory_space=pltpu.HBM)
SEM = pl.BlockSpec(memory_space=pltpu.SEMAPHORE)
EFFECT = pltpu.SideEffectType.DATAFLOW_SIDE_EFFECTING


def _split_start(name, arrs, land_shapes, plan, nc, after=None):
    n = len(arrs)
    lands = [lax.empty(s.shape, s.dtype) for s in land_shapes]
    ins = list(arrs) + lands + ([] if after is None else [after])

    def body(*refs):
        outs = refs[len(ins):]
        for k, (src, dst, _, peer) in enumerate(plan(refs[:n], refs[n:n + len(lands)])):
            pltpu.make_async_remote_copy(src_ref=src, dst_ref=dst, send_sem=outs[k], recv_sem=outs[nc + k], device_id=peer,
                                         device_id_type=MESH).start()
        outs[-1][...] = jnp.zeros((SUBLANES, LANES), F32)

    nt = n + len(lands)
    thru = [pltpu.HBM(a.shape, a.dtype) for a in list(arrs) + lands]
    outs = _pcall(body, name=name, out_shape=tuple([pltpu.SemaphoreType.DMA(())] * (2 * nc) + thru + [_sds((SUBLANES, LANES), F32)]),
                  in_specs=[HBM] * nt + [ANY] * (len(ins) - nt),
                  out_specs=tuple([SEM] * (2 * nc) + [HBM] * nt + [pl.BlockSpec(memory_space=pltpu.VMEM)]),
                  input_output_aliases={i: 2 * nc + i for i in range(nt)},
                  compiler_params=pltpu.CompilerParams(has_side_effects=EFFECT))(
        *[pltpu.with_memory_space_constraint(a, pltpu.HBM) for a in ins[:nt]], *ins[nt:])
    return dict(sems=outs[:2 * nc], arrs=outs[2 * nc:2 * nc + n], lands=outs[2 * nc + n:2 * nc + nt], token=outs[-1], plan=plan, nc=nc)


def _split_wait(name, handle, after):
    arrs, lands, sems, nc = list(handle["arrs"]), list(handle["lands"]), list(handle["sems"]), handle["nc"]
    n, nt = len(arrs), len(arrs) + len(lands)

    def body(*refs):
        sem = refs[nt:nt + 2 * nc]
        for k, (src, _, landing, peer) in enumerate(handle["plan"](refs[:n], refs[n:nt])):
            cp = pltpu.make_async_remote_copy(src_ref=src, dst_ref=landing, send_sem=sem[k], recv_sem=sem[nc + k], device_id=peer,
                                              device_id_type=MESH)
            cp.wait_send()
            cp.wait_recv()

    thru = tuple(pltpu.HBM(a.shape, a.dtype) for a in arrs + lands)
    outs = _pcall(body, name=name, out_shape=thru, in_specs=[HBM] * nt + [SEM] * (2 * nc) + [ANY], out_specs=tuple([HBM] * nt),
                  input_output_aliases={i: i for i in range(nt)},
                  compiler_params=pltpu.CompilerParams(has_side_effects=EFFECT))(*arrs, *lands, *sems, after)
    return list(outs[:n]), list(outs[n:])


WHOLE = "whole"


def _plan_chips(axes):
    def plan(src, land):
        x, y, c, chips = _place()
        idx = [2 * cx + cy for cx, cy in chips]
        part = lambda a, s: src[a] if axes[a] is WHOLE else _chip_slice(src[a], axes[a], s)
        return [(part(a, idx[j]), land[a].at[2 * x + y], land[a].at[idx[j]], (*chips[j], c))
                for a in range(len(land)) for j in range(3)]
    return plan


def _plan_sibling(half):
    def plan(src, land):
        x, y, c, _ = _place()
        lh = lambda a: src[a].shape[0] // 2
        return [(src[a].at[pl.ds((1 - c) * lh(a), lh(a))] if half else src[a], land[a], land[a], (x, y, 1 - c))
                for a in range(len(land))]
    return plan


def _chips_start(name, arrs, axes, after=None):
    part = lambda a, ax: a.shape if ax is WHOLE else a.shape[1:] if ax is None else tuple(d // 4 if i == ax else d for i, d in enumerate(a.shape))
    return _split_start(name, arrs, [_sds((4,) + part(a, ax), a.dtype) for a, ax in zip(arrs, axes)], _plan_chips(axes), 3 * len(arrs), after)


def _sibling_start(name, arrs, half, after=None):
    shp = lambda a: (a.shape[0] // 2,) + a.shape[1:] if half else a.shape
    return _split_start(name, arrs, [_sds(shp(a), a.dtype) for a in arrs], _plan_sibling(half), len(arrs), after)


def _gather_all(name, b):
    R, C = b.shape
    flips = [(dx, dy, dc) for dx in (0, 1) for dy in (0, 1) for dc in (0, 1)][1:]

    def body(b_ref, o_ref, send, recv, lsem):
        x, y, c, _ = _place()
        me = 4 * x + 2 * y + c
        peers = [(x ^ dx, y ^ dy, c ^ dc) for dx, dy, dc in flips]
        mine = pltpu.make_async_copy(b_ref, o_ref.at[me], lsem.at[0])
        mine.start()
        cps = [_remote(b_ref, o_ref.at[me], send, recv, k, peer) for k, peer in enumerate(peers)]
        for cp in cps:
            cp.start()
        for k, (px, py, pc) in enumerate(peers):
            _remote(b_ref, o_ref.at[4 * px + 2 * py + pc], send, recv, k, (px, py, pc)).wait_recv()
        for cp in cps:
            cp.wait_send()
        mine.wait()

    return _comm_call(name, body, [_sds((8, R, C), b.dtype)], 7, 1, b)[0]


def _block_rows(rows, width):
    return _tile(rows, max(SUBLANES, (1 << 19) // width), SUBLANES)


def _add_half(name, g, got):
    L, A, B = g.shape
    Lh = L // 2
    tq = _block_rows(A, B)

    def body(c_ref, g_ref, r_ref, o_ref):
        o_ref[...] = (g_ref[...] + r_ref[...]).astype(BF16)

    spec = pltpu.PrefetchScalarGridSpec(
        num_scalar_prefetch=1, grid=(Lh, A // tq),
        in_specs=[pl.BlockSpec((1, tq, B), lambda l, i, c_ref: (c_ref[0] * Lh + l, i, 0)),
                  pl.BlockSpec((1, tq, B), lambda l, i, c_ref: (l, i, 0))],
        out_specs=pl.BlockSpec((1, tq, B), lambda l, i, c_ref: (l, i, 0)))
    return _pcall(body, name=name, grid_spec=spec, out_shape=_sds((Lh, A, B), BF16),
                  compiler_params=_cp("arbitrary", "arbitrary"))(lax.axis_index("c").reshape(1).astype(jnp.int32), g, got)


def _sum_slots(name, a):
    n, R, C = a.shape
    tq = _block_rows(R, n * C)

    def fn(i, a_ref, o_ref):
        t = a_ref[0].astype(F32)
        for s in range(1, n):
            t = t + a_ref[s].astype(F32)
        o_ref[...] = t

    return _rows(name, fn, R, tq, (a,), [pl.BlockSpec((n, tq, C), lambda i: (0, i, 0))], _sds((R, C), F32), _rb(tq, C))


def _adam_update(w, g, m, v):
    mn = ADAM_B1 * m + (1.0 - ADAM_B1) * g
    vn = ADAM_B2 * v + (1.0 - ADAM_B2) * (g * g)
    m_hat = mn / (1.0 - ADAM_B1 ** ADAM_STEP)
    v_hat = vn / (1.0 - ADAM_B2 ** ADAM_STEP)
    return -ADAM_LR * (m_hat / (jnp.sqrt(v_hat) + ADAM_EPS) + ADAM_WD * w), mn, vn


def _adamw(name, w, g, m, v):
    R, C = w.shape
    tq = _tile(R, 256, SUBLANES)

    def fn(i, w_ref, g_ref, m_ref, v_ref, d_ref, mo_ref, vo_ref):
        d_ref[...], mo_ref[...], vo_ref[...] = _adam_update(w_ref[...], g_ref[...], m_ref[...], v_ref[...])

    r, o = _rb(tq, C), _sds((R, C), F32)
    return _rows(name, fn, R, tq, (w, g, m, v), [r, r, r, r], (o, o, o), (r, r, r))


def _adamw_halves(name, w, mine, theirs, m, v, l0, prev=None):
    L, A, B = w.shape
    Lh = mine.shape[0]
    tq = _tile(A, 256, SUBLANES)

    def body(c_ref, w_ref, a_ref, b_ref, m_ref, v_ref, *refs):
        g_ref, d_ref, mo_ref, vo_ref = refs[-4:]
        is_mine = pl.program_id(0) // Lh == c_ref[0]
        g = jnp.where(is_mine, a_ref[...], b_ref[...])
        g_ref[...] = g
        d_ref[...], mo_ref[...], vo_ref[...] = _adam_update(w_ref[...], g, m_ref[...], v_ref[...])

    full = pl.BlockSpec((1, tq, B), lambda l, i, c_ref: (l0 + l, i, 0))
    half = pl.BlockSpec((1, tq, B), lambda l, i, c_ref: (l % Lh, i, 0))
    o = _sds((L, A, B), F32)
    prev = list(prev or ())
    spec = pltpu.PrefetchScalarGridSpec(num_scalar_prefetch=1, grid=(2 * Lh, A // tq), in_specs=[full, half, half, full, full] + [ANY] * len(prev),
                                        out_specs=(full, full, full, full))
    return _pcall(body, name=name, grid_spec=spec, out_shape=(o, o, o, o), input_output_aliases={6 + i: i for i in range(len(prev))},
                  compiler_params=_cp("arbitrary", "arbitrary"))(lax.axis_index("c").reshape(1).astype(jnp.int32), w, mine, theirs, m, v, *prev)


def _pack(arrs, width, lead=()):
    nl = len(lead)
    flat = jnp.concatenate([a.reshape(lead + (-1,)) for a in arrs], axis=nl)
    n = flat.shape[-1]
    unit = PACK_ROWS * width
    tot = -(-n // unit) * unit
    flat = jnp.pad(flat, [(0, 0)] * nl + [(0, tot - n)])
    return flat.reshape(lead + (tot // width, width))


def _unpack(buf, shapes, lead=()):
    flat = buf.reshape(lead + (-1,))
    out, off = [], 0
    for s in shapes:
        n = int(np.prod(s))
        out.append(flat[..., off:off + n].reshape(lead + tuple(s)))
        off += n
    return out


def _in_groups(W, H):
    o_sq = 4 * W + 2 * H
    o_k = o_sq + W
    o_g = o_k + 2 * KV_W
    return [(0, 4 * W), (o_sq, o_k), (o_g, o_g + 2 * W), (o_k, o_g), (4 * W, o_sq)]


def _relayout_in(shards, W, H):
    c4 = sum(hi - lo for lo, hi in _in_groups(W, H)) // 4
    parts = []
    for lo, hi in _in_groups(W, H):
        for s in range(4):
            a, b = max(lo, s * c4), min(hi, (s + 1) * c4)
            if a < b:
                parts.append(shards[s][:, a - s * c4:b - s * c4])
    parts.append(jnp.zeros((shards.shape[1], BA_W - 2 * H), shards.dtype))
    return jnp.concatenate(parts, axis=1)


def _shard_in(d, W, H):
    groups = _in_groups(W, H)
    starts = [sum(hi - lo for lo, hi in groups[:i]) for i in range(len(groups))]
    stored = sorted(zip(groups, starts))
    c4 = sum(hi - lo for lo, hi in groups) // 4
    out = []
    for s in range(4):
        parts = []
        for (lo, hi), at in stored:
            a, b = max(lo, s * c4), min(hi, (s + 1) * c4)
            if a < b:
                parts.append(d[:, :, at + a - lo:at + b - lo])
        out.append(jnp.concatenate(parts, axis=2))
    return jnp.stack(out)


def _lane_row(vals, at):
    return jnp.pad(vals, (at, LANES - at - vals.shape[0]))[None]


def _layer_fwd(x, lw, tabs, W, H, more=None):
    D = x.shape[1]
    cbk = 7 * W // LANES
    h = _pre_norm(x, lw["g1"])
    proj = _mm("mm_in", h, lw["win"], "nn", BF16, tn=768)
    ba = _mm("mm_ba", h, lw["win"][:, 7 * W + 2 * KV_W:], "nn", F32)
    qkv = _dn_prep(proj, lw["conv"], W)
    beta_b, g_b = _dn_gates(ba, lw["alog"], lw["dt"], H)
    o, st = _delta_fwd(qkv, beta_b, g_b, H, DELTA_CB, DELTA_HB)
    oa = _dn_out(o, proj, lw["ng"], W, 3)
    ob = _swa_fwd(proj, tabs, lw["sinks"], W, 4, cbk)
    if more is not None:
        lw.update(more(ob))
    ya = _mm("mm_up_dn", oa, lw["wup_dn"], "nn", BF16)
    yb = _mm("mm_up_sw", ob, lw["wup_sw"], "nn", BF16)
    mixin = _mix(proj, ya, yb, D, 5)
    mix = _mm("mm_o", mixin, lw["wo"], "nn", F32)
    x1, h2 = _post_mix(x, mix, lw["g2"], lw["g3"])
    f1, act = _mm("mm_ff1", h2, lw["wff1"], "nn", out_dtypes=(BF16, BF16), epi=lambda acc: (acc, jnp.square(jnp.maximum(acc, 0.0))))
    ff = _mm("mm_ff2", act, lw["wff2"], "nn", F32)
    x2 = _post_mlp(x1, ff, lw["g4"])
    saved = dict(x=x, h=h, proj=proj, ba=ba, qkv=qkv, beta_b=beta_b, g_b=g_b, o=o, st=st, oa=oa, ob=ob, ya=ya, yb=yb,
                 mixin=mixin, mix=mix, x1=x1, h2=h2, f1=f1, act=act, ff=ff)
    return x2, saved


def _layer_bwd(dx2, lw, sv, tabs, W, H, l, big):
    D = dx2.shape[1]
    cbk = 7 * W // LANES
    big = dict(big)
    dff, dg4 = _post_mlp_bwd(sv["ff"], lw["g4"], dx2)
    df1 = _mm("mm_ff2_dx", dff, lw["wff2"], "nt", BF16, extras=(sv["f1"],),
              epi=lambda acc, f1: (acc * 2.0 * jnp.maximum(f1.astype(F32), 0.0),))
    big["w_ff2"] = _mm("mm_ff2_dw", sv["act"], dff, "tn", slab=(big["w_ff2"], l))
    dh2 = _mm("mm_ff1_dx", df1, lw["wff1"], "nt", F32)
    big["w_ff1"] = _mm("mm_ff1_dw", sv["h2"], df1, "tn", slab=(big["w_ff1"], l))
    dx1, dmix, dg3, dg2 = _mid_bwd(sv["x1"], lw["g3"], dh2, dx2, sv["mix"], lw["g2"])
    dmixin = _mm("mm_o_dx", dmix, lw["wo"], "nt", BF16)
    big["w_o"] = _mm("mm_o_dw", sv["mixin"], dmix, "tn", slab=(big["w_o"], l))
    dya, dyb, dga, dgb = _mix_bwd(sv["proj"], sv["ya"], sv["yb"], dmixin, D, 5)
    doa = _mm("mm_up_dn_dx", dya, lw["wup_dn"], "nt", BF16)
    big["w_up_dn"] = _mm("mm_up_dn_dw", sv["oa"], dya, "tn", slab=(big["w_up_dn"], l))
    dob = _mm("mm_up_sw_dx", dyb, lw["wup_sw"], "nt", BF16)
    big["w_up_sw"] = _mm("mm_up_sw_dw", sv["ob"], dyb, "tn", slab=(big["w_up_sw"], l))
    do, dz, dng = _dn_out_bwd(sv["o"], sv["proj"], lw["ng"], doa, W, 3)
    dqkvn, dbeta_b, dg_b = _delta_bwd(sv["qkv"], sv["beta_b"], sv["g_b"], sv["st"], do, H, DELTA_CB, DELTA_HB)
    dba, dalog, ddt = _dn_gates_bwd(sv["ba"], lw["alog"], lw["dt"], dbeta_b, dg_b, H)
    dc, dconv = _dn_prep_bwd_a(sv["proj"], lw["conv"], dqkvn, W)
    dqkv = _dn_prep_bwd_b(dc, lw["conv"], W)
    dq_sw, dkc, dkp, dvc, dvp, dsk = _swa_bwd(sv["proj"], tabs, lw["sinks"], dob, W, 4, cbk)
    dkv = _swa_kv_combine(dkc, dkp, dvc, dvp, tabs)
    dproj = jnp.concatenate([dqkv, dz, dq_sw, dga, dgb, dkv, dba], axis=1)
    dh = _mm("mm_in_dx", dproj, lw["win"], "nt", F32, tk=768)
    big["w_in"] = _mm("mm_in_dw", sv["h"], dproj, "tn", tn=768, slab=(big["w_in"], l))
    dx, dg1 = _pre_norm_bwd(sv["x"], lw["g1"], dh, dx1)
    grads = dict(pre_mix_g=dg1[0], dn_conv_w=dconv, dn_a_log=dalog[0, H:2 * H], dn_dt_bias=ddt[0, H:2 * H], dn_norm_g=dng[0],
                 sw_sinks=dsk[:SW_Q_HEADS, 0], post_mix_g=dg2[0], pre_mlp_g=dg3[0], post_mlp_g=dg4[0])
    return dx, grads, big


_WEIGHTS = ["pre_mix_g", "w_in", "dn_conv_w", "dn_a_log", "dn_dt_bias", "dn_norm_g", "sw_sinks", "w_up_dn", "w_up_sw", "w_o",
            "post_mix_g", "pre_mlp_g", "w_ff1", "w_ff2", "post_mlp_g"]
_BIG = {"w_in": 2, "w_up_dn": 1, "w_up_sw": 1, "w_o": 1, "w_ff1": 2, "w_ff2": 1}
_SMALL = [n for n in _WEIGHTS if n not in _BIG]


def _step(P):
    x, target = P["x"][0], P["loss_target"][0]
    S, D = x.shape
    L = P["pre_mix_g"].shape[0]
    H, W = DN_HEADS, DN_HEADS * DN_DK
    assert W == D == SW_Q_HEADS * SW_HD and KV_W == LANES
    me = 2 * lax.axis_index("x") + lax.axis_index("y")

    assert L % 4 == 0
    names = list(_BIG) + ["dn_conv_w"]
    local = [P[n].astype(BF16) for n in _BIG] + [P["dn_conv_w"]]
    early_names = ("w_in", "dn_conv_w")
    tail_names = [n for n in names if n not in early_names]
    own_slot = lambda gathered, mine: [lax.dynamic_update_slice_in_dim(g, w[None], me, 0) for g, w in zip(gathered, mine)]
    gather = lambda name, arrs, after=None: _chips_start(name, arrs, [WHOLE] * len(arrs), after)
    arrived = lambda name, h, after, keys: dict(zip(keys, own_slot(*reversed(_split_wait(name, h, after)))))
    h_first = gather("weights_first_start", [a[:1] for n, a in zip(names, local) if n in early_names])
    early = arrived("weights_first_wait", h_first, x, early_names)
    h_tail = gather("weights_tail_start", [a[:1] for n, a in zip(names, local) if n in tail_names], early["w_in"])
    h_rest = gather("weights_rest_start", [a[1:] for a in local], h_tail["token"])

    def head(full, l, k):
        return dict(
            g1=P["pre_mix_g"][l][None], win=_relayout_in(full["w_in"][:, k], W, H),
            conv=jnp.concatenate([full["dn_conv_w"][s, k] for s in range(4)], axis=-1),
            alog=_lane_row(P["dn_a_log"][l], H), dt=_lane_row(P["dn_dt_bias"][l], H), ng=P["dn_norm_g"][l][None],
            sinks=_lane_row(P["sw_sinks"][l], 0), g2=P["post_mix_g"][l][None], g3=P["pre_mlp_g"][l][None], g4=P["post_mlp_g"][l][None])

    def tail(full, k):
        rows = lambda n: full[n][:, k].reshape(-1, full[n].shape[-1])
        return dict(wup_dn=rows("w_up_dn"), wup_sw=rows("w_up_sw"), wo=rows("w_o"), wff2=rows("w_ff2"),
                    wff1=jnp.concatenate([full["w_ff1"][s, k] for s in range(4)], axis=-1))

    tabs = _rope_tables(P["positions"].reshape(S, 1))
    lws = [head(early, 0, 0)]
    lws[0]["g1"] = lws[0]["g1"] + h_rest["token"][0, 0]

    saved = []
    for l in range(L):
        if l == 1:
            late = arrived("weights_rest_wait", h_rest, x, names)
            lws.extend({**head(late, k + 1, k), **tail(late, k)} for k in range(L - 1))
        first_tail = lambda after: tail(arrived("weights_tail_wait", h_tail, after, tail_names), 0)
        x, sv = _layer_fwd(x, lws[l], tabs, W, H, first_tail if l == 0 else None)
        saved.append(sv)
    loss_row, dx = _loss_head(x, target)

    Lb = L // 2
    layer_grads = [None] * L
    F = 4 * P["w_ff1"].shape[2]
    per_layer = dict(w_in=(D, 7 * W + 2 * KV_W + BA_W), w_up_dn=(W, D), w_up_sw=(W, D), w_o=(D, D), w_ff1=(D, F), w_ff2=(F, D))
    batch = [{n: lax.empty((Lb,) + per_layer[n], F32) for n in _BIG} for _ in range(2)]
    axes = [None if n == "w_in" else ax for n, ax in _BIG.items()]

    def pair_sums(tag, h_swap, after):
        g, got = _split_wait("grad_swap_wait_" + tag, h_swap, after)
        part = {n: _add_half("grad_pair_add_%s_%s" % (tag, n), a, r) for n, a, r in zip(_BIG, g, got)}
        return [_shard_in(part[n], W, H) if n == "w_in" else part[n] for n in _BIG]

    def chip_sums(tag, h_scat, after):
        parts, slots = _split_wait("grad_scatter_wait_" + tag, h_scat, after)
        halves = []
        for n, s, a, ax in zip(_BIG, slots, parts, axes):
            s = lax.dynamic_update_slice_in_dim(s, _own_part(a, ax, me)[None], me, 0)
            halves.append(_sum_slots("grad_chip_sum_%s_%s" % (tag, n), s.reshape(4, -1, s.shape[-1])).reshape(s.shape[1:]))
        h = _sibling_start("grad_share_start_" + tag, halves, False)
        return _split_wait("grad_share_wait_" + tag, h, halves[0])

    for l in reversed(range(L)):
        dx, layer_grads[l], batch[l // Lb] = _layer_bwd(dx, lws[l], saved[l], tabs, W, H, l % Lb, batch[l // Lb])
        if l == Lb:
            h_swap = _sibling_start("grad_swap_start_hi", [batch[1][n] for n in _BIG], True)
            lws[l - 1]["g4"] = lws[l - 1]["g4"] + h_swap["token"][0, 0]
        if l == Lb - 1:
            h_scat_hi = _chips_start("grad_scatter_start_hi", pair_sums("hi", h_swap, dx), axes)
            if l > 0:
                lws[l - 1]["g4"] = lws[l - 1]["g4"] + h_scat_hi["token"][0, 0]
    h_swap = _sibling_start("grad_swap_start_lo", [batch[0][n] for n in _BIG], True)
    h_scat_lo = _chips_start("grad_scatter_start_lo", pair_sums("lo", h_swap, dx), axes)
    upper = {n: _adamw_halves("adamw_hi_" + n, P[n], mine, their, P["m_" + n], P["v_" + n], Lb)
             for n, mine, their in zip(_BIG, *chip_sums("hi", h_scat_hi, h_scat_lo["token"]))}
    gsum, delta, new_m, new_v = {}, {}, {}, {}
    for n, mine, their in zip(_BIG, *chip_sums("lo", h_scat_lo, upper["w_in"][0])):
        gsum[n], delta[n], new_m[n], new_v[n] = _adamw_halves("adamw_lo_" + n, P[n], mine, their, P["m_" + n], P["v_" + n], 0, upper[n])
    grads = {n: jnp.stack([layer_grads[l][n] for l in range(L)]) for n in _SMALL}
    small_shapes = [(1,)] + [grads[n].shape for n in _SMALL]
    tot = _sum_slots("small_sum", _gather_all("small_gather", _pack([loss_row[0, :1]] + [grads[n] for n in _SMALL], LANES)))
    small = _unpack(tot, small_shapes)
    loss = small[0][0]
    gsum.update(zip(_SMALL, small[1:]))
    cw = P["dn_conv_w"].shape[2]
    gsum["dn_conv_w"] = lax.dynamic_slice_in_dim(gsum["dn_conv_w"], me * cw, cw, axis=2)

    sm_shapes = [P[n].shape for n in _SMALL]
    outs = _adamw("adamw_small", *(_pack([src[pre + n] for n in _SMALL], LANES)
                                   for src, pre in ((P, ""), (gsum, ""), (P, "m_"), (P, "v_"))))
    for d, o in zip((delta, new_m, new_v), outs):
        d.update(zip(_SMALL, _unpack(o, sm_shapes)))

    return (loss, dx[None], *[gsum[n] for n in _WEIGHTS], *[delta[n] for n in _WEIGHTS],
            *[new_m[n] for n in _WEIGHTS], *[new_v[n] for n in _WEIGHTS])


def kernel(x, positions, pre_mix_g, w_in, dn_conv_w, dn_a_log, dn_dt_bias, dn_norm_g, sw_sinks, w_up_dn, w_up_sw, w_o, post_mix_g, pre_mlp_g, w_ff1, w_ff2, post_mlp_g, loss_target, m_pre_mix_g, m_w_in, m_dn_conv_w, m_dn_a_log, m_dn_dt_bias, m_dn_norm_g, m_sw_sinks, m_w_up_dn, m_w_up_sw, m_w_o, m_post_mix_g, m_pre_mlp_g, m_w_ff1, m_w_ff2, m_post_mlp_g, v_pre_mix_g, v_w_in, v_dn_conv_w, v_dn_a_log, v_dn_dt_bias, v_dn_norm_g, v_sw_sinks, v_w_up_dn, v_w_up_sw, v_w_o, v_post_mix_g, v_pre_mlp_g, v_w_ff1, v_w_ff2, v_post_mlp_g):
    vals = (x, positions, pre_mix_g, w_in, dn_conv_w, dn_a_log, dn_dt_bias, dn_norm_g, sw_sinks, w_up_dn, w_up_sw, w_o, post_mix_g, pre_mlp_g, w_ff1, w_ff2, post_mlp_g, loss_target, m_pre_mix_g, m_w_in, m_dn_conv_w, m_dn_a_log, m_dn_dt_bias, m_dn_norm_g, m_sw_sinks, m_w_up_dn, m_w_up_sw, m_w_o, m_post_mix_g, m_pre_mlp_g, m_w_ff1, m_w_ff2, m_post_mlp_g, v_pre_mix_g, v_w_in, v_dn_conv_w, v_dn_a_log, v_dn_dt_bias, v_dn_norm_g, v_sw_sinks, v_w_up_dn, v_w_up_sw, v_w_o, v_post_mix_g, v_pre_mlp_g, v_w_ff1, v_w_ff2, v_post_mlp_g)
    names = ["x", "positions"] + _WEIGHTS + ["loss_target"] + ["m_" + n for n in _WEIGHTS] + ["v_" + n for n in _WEIGHTS]
    return _step(dict(zip(names, vals)))
```

```python
import functools

import numpy as np
import jax
import jax.numpy as jnp
from jax import lax
from jax.experimental import pallas as pl
from jax.experimental.pallas import tpu as pltpu

F32, BF16 = jnp.float32, jnp.bfloat16
MESH = pl.DeviceIdType.MESH

DN_HEADS = 8
DN_DK = 128
DN_CONV = 4
DN_CHUNK = 64
SW_Q_HEADS = 16
SW_KV_HEADS = 2
SW_HD = 64
SW_BLOCK = 128
ROPE_THETA = 500000.0
ROT_DIM = SW_HD // 4
EPS = 1e-6
ADAM_LR, ADAM_B1, ADAM_B2, ADAM_EPS, ADAM_WD, ADAM_STEP = 0.001, 0.9, 0.999, 1e-08, 0.01, 10

LANES = 128
SUBLANES = 8
VMEM_LIMIT = 48 * 1024 * 1024
KV_W = SW_KV_HEADS * SW_HD
BA_W = 256
PACK_ROWS = 512
DELTA_CB = 4
DELTA_HB = 8


def _pcall(body, **kw):
    return pl.pallas_call(body, **kw)


def _cp(*sem):
    return pltpu.CompilerParams(dimension_semantics=sem, vmem_limit_bytes=VMEM_LIMIT)


def _tile(n, pref, unit=LANES):
    if n <= pref:
        return n
    t = (pref // unit) * unit
    while t > unit and n % t:
        t -= unit
    assert n % t == 0, (n, pref)
    return t


def _sds(shape, dtype):
    return jax.ShapeDtypeStruct(tuple(shape), dtype)


_DIMS = {"nn": ((1,), (0,)), "nt": ((1,), (1,)), "tn": ((0,), (0,))}


def _mm(name, a, b, mode, out_dtype=F32, tm=1024, tn=1024, tk=1024, extras=(), epi=None, out_dtypes=None, slab=None):
    if mode == "nn":
        (M, K), (_, N) = a.shape, b.shape
    elif mode == "nt":
        (M, K), (N, _) = a.shape, b.shape
    else:
        (K, M), (_, N) = a.shape, b.shape
    tm, tn, tk = _tile(M, tm), _tile(N, tn), _tile(K, tk)
    nk = K // tk
    a_spec = {"nn": pl.BlockSpec((tm, tk), lambda i, j, k: (i, k)),
              "nt": pl.BlockSpec((tm, tk), lambda i, j, k: (i, k)),
              "tn": pl.BlockSpec((tk, tm), lambda i, j, k: (k, i))}[mode]
    b_spec = {"nn": pl.BlockSpec((tk, tn), lambda i, j, k: (k, j)),
              "nt": pl.BlockSpec((tn, tk), lambda i, j, k: (j, k)),
              "tn": pl.BlockSpec((tk, tn), lambda i, j, k: (k, j))}[mode]
    dims = (_DIMS[mode], ((), ()))
    out_dtypes = tuple(out_dtypes or (out_dtype,))
    ne, no = len(extras), len(out_dtypes)
    o_spec = pl.BlockSpec((tm, tn), lambda i, j, k: (i, j))

    def body(*refs):
        a_ref, b_ref, ex = refs[0], refs[1], refs[2:2 + ne]
        outs = refs[-no:] if nk == 1 else refs[-1 - no:-1]
        part = lax.dot_general(a_ref[...], b_ref[...], dims, preferred_element_type=F32)

        def finish(acc):
            res = epi(acc, *[e[...] for e in ex]) if epi else (acc,)
            for o, r, dt in zip(outs, res, out_dtypes):
                if slab is None:
                    o[...] = r.astype(dt)
                else:
                    o[0] = r.astype(dt)

        if nk == 1:
            finish(part)
            return
        acc_ref, k = refs[-1], pl.program_id(2)

        @pl.when(k == 0)
        def _():
            acc_ref[...] = part

        @pl.when((k > 0) & (k < nk - 1))
        def _():
            acc_ref[...] += part

        @pl.when(k == nk - 1)
        def _():
            finish(acc_ref[...] + part)

    kw = dict(name=name, grid=(M // tm, N // tn, nk), scratch_shapes=[] if nk == 1 else [pltpu.VMEM((tm, tn), F32)],
              compiler_params=_cp("parallel", "parallel", "arbitrary"))
    if slab is not None:
        buf, l = slab
        return _pcall(body, in_specs=[a_spec, b_spec, ANY], out_specs=pl.BlockSpec((1, tm, tn), lambda i, j, k: (l, i, j)),
                      out_shape=_sds(buf.shape, buf.dtype), input_output_aliases={2: 0}, **kw)(a, b, buf)
    out = _pcall(body, in_specs=[a_spec, b_spec] + [o_spec] * ne, out_specs=tuple(o_spec for _ in out_dtypes),
                 out_shape=tuple(_sds((M, N), dt) for dt in out_dtypes), **kw)(a, b, *extras)
    return out if no > 1 else out[0]


def _rows(name, fn, n_rows, tq, ins, in_specs, out_shapes, out_specs):
    def body(*refs):
        fn(pl.program_id(0), *refs)

    return _pcall(body, name=name, grid=(n_rows // tq,), in_specs=in_specs, out_specs=out_specs,
                  out_shape=out_shapes, compiler_params=_cp("arbitrary"))(*ins)


def _rb(tq, w, cb=0):
    return pl.BlockSpec((tq, w), lambda i: (i, cb))


def _full(shape):
    return pl.BlockSpec(tuple(shape), lambda *_: (0,) * len(shape))


def _rms_fwd(x, g):
    r = lax.rsqrt(jnp.mean(x * x, axis=-1, keepdims=True) + EPS)
    return x * r * g


def _rms_bwd(x, g, dy):
    r = lax.rsqrt(jnp.mean(x * x, axis=-1, keepdims=True) + EPS)
    xh = x * r
    t = dy * g
    dx = r * (t - xh * jnp.mean(t * xh, axis=-1, keepdims=True))
    return dx, jnp.sum(dy * xh, axis=0, keepdims=True)


def _acc(i, ref, val):
    @pl.when(i == 0)
    def _():
        ref[...] = val

    @pl.when(i > 0)
    def _():
        ref[...] += val


def _sigmoid(x):
    return 0.5 * jnp.tanh(0.5 * x) + 0.5


def _pre_norm(x, g):
    S, D = x.shape
    tq = _tile(S, 512, SUBLANES)

    def fn(i, x_ref, g_ref, h_ref):
        h_ref[...] = _rms_fwd(x_ref[...], g_ref[...]).astype(BF16)

    return _rows("pre_norm", fn, S, tq, (x, g), [_rb(tq, D), _full((1, D))], _sds((S, D), BF16), _rb(tq, D))


def _post_mix(x, mix, g2, g3):
    S, D = x.shape
    tq = _tile(S, 512, SUBLANES)

    def fn(i, x_ref, m_ref, g2_ref, g3_ref, x1_ref, h2_ref):
        x1 = x_ref[...] + _rms_fwd(m_ref[...], g2_ref[...])
        x1_ref[...] = x1
        h2_ref[...] = _rms_fwd(x1, g3_ref[...]).astype(BF16)

    return _rows("post_mix", fn, S, tq, (x, mix, g2, g3), [_rb(tq, D), _rb(tq, D), _full((1, D)), _full((1, D))],
                 (_sds((S, D), F32), _sds((S, D), BF16)), (_rb(tq, D), _rb(tq, D)))


def _post_mlp(x1, ff, g4):
    S, D = x1.shape
    tq = _tile(S, 512, SUBLANES)

    def fn(i, x_ref, f_ref, g_ref, o_ref):
        o_ref[...] = x_ref[...] + _rms_fwd(f_ref[...], g_ref[...])

    return _rows("post_mlp", fn, S, tq, (x1, ff, g4), [_rb(tq, D), _rb(tq, D), _full((1, D))], _sds((S, D), F32), _rb(tq, D))


def _loss_head(y, target):
    S, D = y.shape
    tq = _tile(S, 512, SUBLANES)

    def fn(i, y_ref, t_ref, l_ref, d_ref):
        e = y_ref[...] - t_ref[...]
        d_ref[...] = e * (1.0 / D)
        part = jnp.sum(jnp.sum(e * e, axis=1, keepdims=True), axis=0, keepdims=True) * (0.5 / D)
        _acc(i, l_ref, jnp.broadcast_to(part, (1, LANES)))

    return _rows("loss_head", fn, S, tq, (y, target), [_rb(tq, D), _rb(tq, D)],
                 (_sds((1, LANES), F32), _sds((S, D), F32)), (_full((1, LANES)), _rb(tq, D)))


def _post_mlp_bwd(ff, g4, dx2):
    S, D = ff.shape
    tq = _tile(S, 512, SUBLANES)

    def fn(i, f_ref, g_ref, d_ref, o_ref, dg_ref):
        dx, dg = _rms_bwd(f_ref[...], g_ref[...], d_ref[...])
        o_ref[...] = dx.astype(BF16)
        _acc(i, dg_ref, dg)

    return _rows("post_mlp_bwd", fn, S, tq, (ff, g4, dx2), [_rb(tq, D), _full((1, D)), _rb(tq, D)],
                 (_sds((S, D), BF16), _sds((1, D), F32)), (_rb(tq, D), _full((1, D))))


def _mid_bwd(x1, g3, dh2, dx2, mix, g2):
    S, D = x1.shape
    tq = _tile(S, 256, SUBLANES)

    def fn(i, x_ref, g3_ref, dh_ref, dx2_ref, m_ref, g2_ref, dx1_ref, dm_ref, dg3_ref, dg2_ref):
        d, dg3 = _rms_bwd(x_ref[...], g3_ref[...], dh_ref[...])
        dx1 = dx2_ref[...] + d
        dx1_ref[...] = dx1
        dm, dg2 = _rms_bwd(m_ref[...], g2_ref[...], dx1)
        dm_ref[...] = dm.astype(BF16)
        _acc(i, dg3_ref, dg3)
        _acc(i, dg2_ref, dg2)

    r, f = _rb(tq, D), _full((1, D))
    return _rows("mid_bwd", fn, S, tq, (x1, g3, dh2, dx2, mix, g2), [r, f, r, r, r, f],
                 (_sds((S, D), F32), _sds((S, D), BF16), _sds((1, D), F32), _sds((1, D), F32)), (r, r, f, f))


def _pre_norm_bwd(x, g1, dh, dx1):
    S, D = x.shape
    tq = _tile(S, 512, SUBLANES)

    def fn(i, x_ref, g_ref, dh_ref, dx1_ref, dx_ref, dg_ref):
        d, dg = _rms_bwd(x_ref[...], g_ref[...], dh_ref[...])
        dx_ref[...] = dx1_ref[...] + d
        _acc(i, dg_ref, dg)

    r, f = _rb(tq, D), _full((1, D))
    return _rows("pre_norm_bwd", fn, S, tq, (x, g1, dh, dx1), [r, f, r, r], (_sds((S, D), F32), _sds((1, D), F32)), (r, f))


def _mix(proj, ya, yb, D, cb_a):
    S = ya.shape[0]
    tq = _tile(S, 256, SUBLANES)

    def fn(i, ga_ref, gb_ref, ya_ref, yb_ref, o_ref):
        ga, gb, ya, yb = (r[...].astype(F32) for r in (ga_ref, gb_ref, ya_ref, yb_ref))
        o_ref[...] = (_sigmoid(ga) * ya + _sigmoid(gb) * yb).astype(BF16)

    return _rows("mix", fn, S, tq, (proj, proj, ya, yb), [_rb(tq, D, cb_a), _rb(tq, D, cb_a + 1), _rb(tq, D), _rb(tq, D)],
                 _sds((S, D), BF16), _rb(tq, D))


def _mix_bwd(proj, ya, yb, dmixin, D, cb_a):
    S = ya.shape[0]
    tq = _tile(S, 256, SUBLANES)

    def fn(i, ga_ref, gb_ref, ya_ref, yb_ref, d_ref, dya_ref, dyb_ref, dga_ref, dgb_ref):
        ga, gb, ya, yb, d = (r[...].astype(F32) for r in (ga_ref, gb_ref, ya_ref, yb_ref, d_ref))
        sa, sb = _sigmoid(ga), _sigmoid(gb)
        dya_ref[...] = (d * sa).astype(BF16)
        dyb_ref[...] = (d * sb).astype(BF16)
        dga_ref[...] = (d * ya * sa * (1.0 - sa)).astype(BF16)
        dgb_ref[...] = (d * yb * sb * (1.0 - sb)).astype(BF16)

    r = _rb(tq, D)
    o = _sds((S, D), BF16)
    return _rows("mix_bwd", fn, S, tq, (proj, proj, ya, yb, dmixin), [_rb(tq, D, cb_a), _rb(tq, D, cb_a + 1), r, r, r],
                 (o, o, o, o), (r, r, r, r))


HALO = 16


def _shift_down(xe, k, tq):
    return pltpu.roll(xe, k, 0)[HALO:HALO + tq]


def _conv_pre(cur_ref, halo_ref, w_ref, i, tq):
    x = cur_ref[...].astype(F32)
    halo = jnp.where(i > 0, halo_ref[...].astype(F32), 0.0)
    xe = jnp.concatenate([halo, x], axis=0)
    xs = [x] + [_shift_down(xe, k, tq) for k in range(1, DN_CONV)]
    w = w_ref[...]
    c = sum(w[DN_CONV - 1 - k:DN_CONV - k, :] * xs[k] for k in range(DN_CONV))
    return c, xs


def _dn_prep(proj, conv_w, W):
    S = proj.shape[0]
    tq = _tile(S, 256, HALO)
    hb = tq // HALO

    def body(cur_ref, halo_ref, w_ref, o_ref):
        j, i = pl.program_id(0), pl.program_id(1)
        c, _ = _conv_pre(cur_ref, halo_ref, w_ref, i, tq)
        y = c * _sigmoid(c)
        scale = jnp.where(j == 0, DN_DK ** -0.5, 1.0)
        for h in range(W // DN_DK):
            sl = slice(h * DN_DK, (h + 1) * DN_DK)
            yh = y[:, sl]
            rs = lax.rsqrt(jnp.sum(yh * yh, axis=-1, keepdims=True) + EPS)
            o_ref[:, sl] = jnp.where(j == 2, yh, yh * rs * scale)

    return _pcall(body, name="dn_prep", grid=(3, S // tq),
                  in_specs=[pl.BlockSpec((tq, W), lambda j, i: (i, j)),
                            pl.BlockSpec((HALO, W), lambda j, i: (jnp.maximum(i * hb - 1, 0), j)),
                            pl.BlockSpec((DN_CONV, W), lambda j, i: (0, j))],
                  out_specs=pl.BlockSpec((tq, W), lambda j, i: (i, j)), out_shape=_sds((S, 3 * W), F32),
                  compiler_params=_cp("arbitrary", "arbitrary"))(proj, proj, conv_w)


def _dn_prep_bwd_a(proj, conv_w, dqkv, W):
    S = proj.shape[0]
    tq = _tile(S, 256, HALO)
    hb = tq // HALO

    def body(cur_ref, halo_ref, w_ref, d_ref, dc_ref, dw_ref):
        j, i = pl.program_id(0), pl.program_id(1)
        c, xs = _conv_pre(cur_ref, halo_ref, w_ref, i, tq)
        sg = _sigmoid(c)
        y = c * sg
        scale = jnp.where(j == 0, DN_DK ** -0.5, 1.0)
        dout = d_ref[0]
        dys = []
        for h in range(W // DN_DK):
            sl = slice(h * DN_DK, (h + 1) * DN_DK)
            yh, dh = y[:, sl], dout[:, sl]
            rs = lax.rsqrt(jnp.sum(yh * yh, axis=-1, keepdims=True) + EPS)
            yn = yh * rs
            dn = scale * rs * (dh - yn * jnp.sum(dh * yn, axis=-1, keepdims=True))
            dys.append(jnp.where(j == 2, dh, dn))
        dy = jnp.concatenate(dys, axis=1)
        dc = dy * (sg * (1.0 + c * (1.0 - sg)))
        dc_ref[...] = dc
        dw = jnp.concatenate([jnp.sum(dc * xs[DN_CONV - 1 - r], axis=0, keepdims=True) for r in range(DN_CONV)], axis=0)
        _acc(i, dw_ref, dw)

    return _pcall(body, name="dn_prep_bwd_a", grid=(3, S // tq),
                  in_specs=[pl.BlockSpec((tq, W), lambda j, i: (i, j)),
                            pl.BlockSpec((HALO, W), lambda j, i: (jnp.maximum(i * hb - 1, 0), j)),
                            pl.BlockSpec((DN_CONV, W), lambda j, i: (0, j)),
                            pl.BlockSpec((1, tq, W), lambda j, i: (j, i, 0))],
                  out_specs=(pl.BlockSpec((tq, W), lambda j, i: (i, j)), pl.BlockSpec((DN_CONV, W), lambda j, i: (0, j))),
                  out_shape=(_sds((S, 3 * W), F32), _sds((DN_CONV, 3 * W), F32)),
                  compiler_params=_cp("arbitrary", "arbitrary"))(proj, proj, conv_w, dqkv)


def _dn_prep_bwd_b(dc, conv_w, W):
    S = dc.shape[0]
    tq = _tile(S, 256, SUBLANES)
    hb = tq // SUBLANES
    nblk = S // tq

    def body(cur_ref, nxt_ref, w_ref, o_ref):
        i = pl.program_id(1)
        d = cur_ref[...]
        nxt = jnp.where(i < nblk - 1, nxt_ref[...], 0.0)
        de = jnp.concatenate([d, nxt], axis=0)
        w = w_ref[...]
        out = w[DN_CONV - 1:DN_CONV, :] * d
        for k in range(1, DN_CONV):
            out = out + w[DN_CONV - 1 - k:DN_CONV - k, :] * pltpu.roll(de, tq + SUBLANES - k, 0)[0:tq]
        o_ref[...] = out.astype(BF16)

    return _pcall(body, name="dn_prep_bwd_b", grid=(3, nblk),
                  in_specs=[pl.BlockSpec((tq, W), lambda j, i: (i, j)),
                            pl.BlockSpec((SUBLANES, W), lambda j, i: (jnp.minimum((i + 1) * hb, S // SUBLANES - 1), j)),
                            pl.BlockSpec((DN_CONV, W), lambda j, i: (0, j))],
                  out_specs=pl.BlockSpec((tq, W), lambda j, i: (i, j)), out_shape=_sds((S, 3 * W), BF16),
                  compiler_params=_cp("arbitrary", "arbitrary"))(dc, dc, conv_w)


def _gate_terms(ba, al, dt):
    u = ba + dt
    sp = jnp.maximum(u, 0.0) + jnp.log(1.0 + jnp.exp(-jnp.abs(u)))
    return _sigmoid(ba), -jnp.exp(al) * sp, u


def _dn_gates(ba, alog_row, dt_row, H):
    S = ba.shape[0]
    tq = _tile(S, 512, SUBLANES)
    W = H * DN_DK

    def fn(i, ba_ref, al_ref, dt_ref, be_ref, g_ref):
        bet, gg, _ = _gate_terms(ba_ref[...], al_ref[...], dt_ref[...])
        for h in range(H):
            sl = slice(h * DN_DK, (h + 1) * DN_DK)
            be_ref[:, sl] = jnp.broadcast_to(bet[:, h:h + 1], (tq, DN_DK))
            g_ref[:, sl] = jnp.broadcast_to(gg[:, H + h:H + h + 1], (tq, DN_DK))

    return _rows("dn_gates", fn, S, tq, (ba, alog_row, dt_row), [_rb(tq, LANES), _full((1, LANES)), _full((1, LANES))],
                 (_sds((S, W), F32), _sds((S, W), F32)), (_rb(tq, W), _rb(tq, W)))


def _dn_gates_bwd(ba, alog_row, dt_row, dbeta_b, dg_b, H):
    S = ba.shape[0]
    tq = _tile(S, 512, SUBLANES)
    W = H * DN_DK

    def fn(i, ba_ref, al_ref, dt_ref, db_ref, dg_ref, o_ref, dal_ref, ddt_ref):
        bet, gg, u = _gate_terms(ba_ref[...], al_ref[...], dt_ref[...])
        lane = lax.broadcasted_iota(jnp.int32, (tq, LANES), 1)
        d = jnp.zeros((tq, LANES), F32)
        for h in range(H):
            d = jnp.where(lane == h, db_ref[:, h * DN_DK:h * DN_DK + 1], d)
            d = jnp.where(lane == H + h, dg_ref[:, h * DN_DK:h * DN_DK + 1], d)
        is_a = (lane >= H) & (lane < 2 * H)
        da = jnp.where(is_a, d * (-jnp.exp(al_ref[...]) * _sigmoid(u)), 0.0)
        dlog = jnp.where(lane < H, d * bet * (1.0 - bet), da)
        o_ref[...] = jnp.concatenate([dlog, jnp.zeros((tq, BA_W - LANES), F32)], axis=1).astype(BF16)
        _acc(i, dal_ref, jnp.sum(jnp.where(is_a, d * gg, 0.0), axis=0, keepdims=True))
        _acc(i, ddt_ref, jnp.sum(da, axis=0, keepdims=True))

    f = _full((1, LANES))
    return _rows("dn_gates_bwd", fn, S, tq, (ba, alog_row, dt_row, dbeta_b, dg_b),
                 [_rb(tq, LANES), f, f, _rb(tq, W), _rb(tq, W)],
                 (_sds((S, BA_W), BF16), _sds((1, LANES), F32), _sds((1, LANES), F32)), (_rb(tq, BA_W), f, f))


def _dn_out(o, proj, ng, W, cb_z):
    S = o.shape[0]
    tq = _tile(S, 256, SUBLANES)

    def fn(i, o_ref, z_ref, g_ref, y_ref):
        for h in range(W // DN_DK):
            sl = slice(h * DN_DK, (h + 1) * DN_DK)
            z = z_ref[:, sl].astype(F32)
            y_ref[:, sl] = (_rms_fwd(o_ref[:, sl], g_ref[...]) * (z * _sigmoid(z))).astype(BF16)

    return _rows("dn_out", fn, S, tq, (o, proj, ng), [_rb(tq, W), _rb(tq, W, cb_z), _full((1, DN_DK))], _sds((S, W), BF16), _rb(tq, W))


def _dn_out_bwd(o, proj, ng, dy, W, cb_z):
    S = o.shape[0]
    tq = _tile(S, 256, SUBLANES)

    def fn(i, o_ref, z_ref, g_ref, d_ref, do_ref, dz_ref, dg_ref):
        g = g_ref[...]
        dg = jnp.zeros((1, DN_DK), F32)
        for h in range(W // DN_DK):
            sl = slice(h * DN_DK, (h + 1) * DN_DK)
            oh, z, d = o_ref[:, sl], z_ref[:, sl].astype(F32), d_ref[:, sl].astype(F32)
            sg = _sigmoid(z)
            dn = d * (z * sg)
            dz_ref[:, sl] = (d * _rms_fwd(oh, g) * (sg * (1.0 + z * (1.0 - sg)))).astype(BF16)
            dx, dgh = _rms_bwd(oh, g, dn)
            do_ref[:, sl] = dx
            dg = dg + dgh
        _acc(i, dg_ref, dg)

    r = _rb(tq, W)
    return _rows("dn_out_bwd", fn, S, tq, (o, proj, ng, dy), [r, _rb(tq, W, cb_z), _full((1, DN_DK)), r],
                 (_sds((S, W), F32), _sds((S, W), BF16), _sds((1, DN_DK), F32)), (r, r, _full((1, DN_DK))))


def _bdot(a, b, mode="nn"):
    return lax.dot_general(a.astype(BF16), b.astype(BF16), (_DIMS[mode], ((), ())), preferred_element_type=F32)


def _rsum(x):
    return jnp.broadcast_to(jnp.sum(x, axis=-1, keepdims=True), x.shape)


def _dot3(a, b, mode="nn"):
    ah, bh = a.astype(BF16), b.astype(BF16)
    al, bl = (a - ah.astype(F32)).astype(BF16), (b - bh.astype(F32)).astype(BF16)
    d = lambda x, y: lax.dot_general(x, y, (_DIMS[mode], ((), ())), preferred_element_type=F32)
    return d(ah, bh) + (d(al, bh) + d(ah, bl))


def _cumsum_rows(x, reverse=False):
    n = x.shape[0]
    row = lax.broadcasted_iota(jnp.int32, x.shape, 0)
    s = 1
    while s < n:
        if reverse:
            x = x + jnp.where(row < n - s, pltpu.roll(x, n - s, 0), 0.0)
        else:
            x = x + jnp.where(row >= s, pltpu.roll(x, s, 0), 0.0)
        s *= 2
    return x


def _each(f, *lists):
    return [f(*a) for a in zip(*lists)]


def _delta_local(qs, ks, vs, bes, grs):
    C = DN_CHUNK
    ri = lax.broadcasted_iota(jnp.int32, (C, C), 0)
    ci = lax.broadcasted_iota(jnp.int32, (C, C), 1)
    causal, strict = ri >= ci, ri > ci
    gcs = [_cumsum_rows(g) for g in grs]
    decays = [jnp.where(causal, jnp.exp(jnp.where(causal, gc[:, :C] - gc.T[:C, :], 0.0)), 0.0) for gc in gcs]
    egs = [jnp.exp(gc) for gc in gcs]
    eks = [jnp.exp(gc[C - 1:C, :] - gc) for gc in gcs]
    gams = [jnp.exp(gc[C - 1:C, :]) for gc in gcs]
    kbs = _each(lambda k, be: k * be, ks, bes)
    kks = _each(lambda kb, k: _bdot(kb, k, "nt"), kbs, ks)
    nls = _each(lambda kk, dc: jnp.where(strict, -kk * dc, 0.0), kks, decays)
    eye = (ri == ci).astype(F32)
    ts = [eye + nl for nl in nls]
    pws = [_dot3(nl, nl) for nl in nls]
    for s in range(4):
        both = _each(lambda t, pw: _dot3(jnp.concatenate([t, pw], axis=0), pw), ts, pws)
        ts = _each(lambda t, b: t + b[:C], ts, both)
        pws = [b[C:] for b in both]
    ts = _each(lambda t, pw: t + _dot3(t, pw), ts, pws)
    vbs = _each(lambda v, be: v * be, vs, bes)
    kbes = _each(lambda kb, eg: kb * eg, kbs, egs)
    uws = _each(lambda t, vb, kbe: _dot3(t, jnp.concatenate([vb, kbe], axis=1)), ts, vbs, kbes)
    us, ws = [uw[:, :DN_DK] for uw in uws], [uw[:, DN_DK:] for uw in uws]
    qks = _each(lambda q, k: _bdot(q, k, "nt"), qs, ks)
    return dict(decay=decays, eg=egs, ek=eks, gam=gams, kb=kbs, kk=kks, t=ts, vb=vbs, kbe=kbes, u=us, w=ws, qk=qks,
                a=_each(lambda qk, dc: qk * dc, qks, decays), qd=_each(lambda q, eg: q * eg, qs, egs),
                kd=_each(lambda k, ek: k * ek, ks, eks), strict=strict)


def _delta_items(refs, CB, HB):
    C, dk = DN_CHUNK, DN_DK
    return [[r[c * C:(c + 1) * C, h * dk:(h + 1) * dk] for h in range(HB) for c in range(CB)] for r in refs]


def _delta_fwd(qkv, beta_b, g_b, H, CB, HB):
    S = qkv.shape[0]
    C, dk = DN_CHUNK, DN_DK
    N = S // C
    R = CB * C
    G = H // HB

    def body(q_ref, k_ref, v_ref, b_ref, g_ref, o_ref, st_ref, s_ref):
        @pl.when(pl.program_id(1) == 0)
        def _():
            s_ref[...] = jnp.zeros((HB, dk, dk), F32)

        L = _delta_local(*_delta_items((q_ref, k_ref, v_ref, b_ref, g_ref), CB, HB))
        ss = [s_ref[h] for h in range(HB)]
        for c in range(CB):
            it = [h * CB + c for h in range(HB)]
            for h in range(HB):
                st_ref[h, c] = ss[h]
            wq = [_bdot(jnp.concatenate([L["w"][i], L["qd"][i]], axis=0), s) for i, s in zip(it, ss)]
            vns = [L["u"][i] - x[:C] for i, x in zip(it, wq)]
            outs = [x[C:] + _bdot(L["a"][i], vn) for i, x, vn in zip(it, wq, vns)]
            ss = [s * L["gam"][i] + _bdot(L["kd"][i], vn, "tn") for i, s, vn in zip(it, ss, vns)]
            for h in range(HB):
                o_ref[c * C:(c + 1) * C, h * dk:(h + 1) * dk] = outs[h]
        for h in range(HB):
            s_ref[h] = ss[h]

    blk = lambda off: pl.BlockSpec((R, HB * dk), lambda h, n: (n, off + h))
    return _pcall(body, name="delta_fwd", grid=(G, N // CB),
                  in_specs=[blk(0), blk(G), blk(2 * G), blk(0), blk(0)],
                  out_specs=(blk(0), pl.BlockSpec((HB, CB, dk, dk), lambda h, n: (h, n, 0, 0))),
                  out_shape=(_sds((S, H * dk), F32), _sds((H, N, dk, dk), F32)),
                  scratch_shapes=[pltpu.VMEM((HB, dk, dk), F32)],
                  compiler_params=_cp("arbitrary", "arbitrary"))(qkv, qkv, qkv, beta_b, g_b)


def _delta_bwd(qkv, beta_b, g_b, states, do, H, CB, HB):
    S = qkv.shape[0]
    C, dk = DN_CHUNK, DN_DK
    N = S // C
    R = CB * C
    NB = N // CB
    G = H // HB

    def body(q_ref, k_ref, v_ref, b_ref, g_ref, st_ref, do_ref, dqkv_ref, db_ref, dg_ref, ds_ref):
        @pl.when(pl.program_id(1) == 0)
        def _():
            ds_ref[...] = jnp.zeros((HB, dk, dk), F32)

        qs, ks, vs, bes, grs, dos = _delta_items((q_ref, k_ref, v_ref, b_ref, g_ref, do_ref), CB, HB)
        L = _delta_local(qs, ks, vs, bes, grs)
        ts, decays, kbs, egs, eks, gams, qds, kds = (L[n] for n in ("t", "decay", "kb", "eg", "ek", "gam", "qd", "kd"))
        s0s = [st_ref[h, c] for h in range(HB) for c in range(CB)]
        vns = _each(lambda u, w, s0: u - _bdot(w, s0), L["u"], L["w"], s0s)
        pre_dvn = _each(lambda a, d: _bdot(a, d, "tn"), L["a"], dos)
        pre_ds = _each(lambda qd, d: _bdot(qd, d, "tn"), qds, dos)
        das = _each(lambda d, vn: _bdot(d, vn, "nt"), dos, vns)
        ds = [ds_ref[h] for h in range(HB)]
        ds1s, dvns = [None] * (HB * CB), [None] * (HB * CB)
        for c in reversed(range(CB)):
            it = [h * CB + c for h in range(HB)]
            new = [pre_dvn[i] + _bdot(kds[i], d) for i, d in zip(it, ds)]
            for i, d, dv in zip(it, ds, new):
                ds1s[i], dvns[i] = d, dv
            ds = [pre_ds[i] + d * gams[i] - _bdot(L["w"][i], dv, "tn") for i, d, dv in zip(it, ds, new)]
        for h in range(HB):
            ds_ref[h] = ds[h]
        dkds = _each(lambda vn, d1: _bdot(vn, d1, "nt"), vns, ds1s)
        dgams = _each(lambda s0, d1: jnp.sum(jnp.sum(s0 * d1, axis=1, keepdims=True), axis=0, keepdims=True), s0s, ds1s)
        ost = _each(lambda d, dv, s0: _bdot(jnp.concatenate([d, dv], axis=0), s0, "nt"), dos, dvns, s0s)
        dqds, dws = [x[:C] for x in ost], [-x[C:] for x in ost]
        dvw = _each(lambda dv, dw: jnp.concatenate([dv, dw], axis=1), dvns, dws)
        tdvw = _each(lambda t, x: _dot3(t, x, "tn"), ts, dvw)
        dvbs, dkbes = [x[:, :dk] for x in tdvw], [x[:, dk:] for x in tdvw]
        dts = _each(lambda x, vb, kbe: _dot3(x, jnp.concatenate([vb, kbe], axis=1), "nt"), dvw, L["vb"], L["kbe"])
        tmp = _each(lambda dt, t: _dot3(dt, t, "nt"), dts, ts)
        dls = _each(lambda t, x: -_dot3(t, x, "tn"), ts, tmp)
        ms = _each(lambda dl, dc: jnp.where(L["strict"], dl * dc, 0.0), dls, decays)
        mas = _each(lambda da, dc: da * dc, das, decays)
        dkbs = _each(lambda m, k, dkbe, eg: _bdot(m, k) + dkbe * eg, ms, ks, dkbes, egs)
        dks = _each(lambda m, kb, ma, q, dkd, ek, dkb, be: _bdot(m, kb, "tn") + _bdot(ma, q, "tn") + dkd * ek + dkb * be,
                    ms, kbs, mas, qs, dkds, eks, dkbs, bes)
        dqs = _each(lambda ma, k, dqd, eg: _bdot(ma, k) + dqd * eg, mas, ks, dqds, egs)
        es = _each(lambda m, kk, ma, qk: m * kk + ma * qk, ms, L["kk"], mas, L["qk"])
        ones = jnp.ones((C, dk), BF16)
        row = lax.broadcasted_iota(jnp.int32, (C, dk), 0)
        for i in range(HB * CB):
            h, c = divmod(i, CB)
            rs, cs = slice(c * C, (c + 1) * C), slice(h * dk, (h + 1) * dk)
            e = es[i]
            e_hi = e.astype(BF16)
            col = _bdot(e_hi, ones, "tn") + _bdot(e - e_hi.astype(F32), ones, "tn")
            t_kd = _rsum(dkds[i] * kds[i])
            dgc = (jnp.broadcast_to(jnp.sum(e, axis=1, keepdims=True), (C, dk)) - col + _rsum(dqds[i] * qds[i]) - t_kd
                   + _rsum(dkbes[i] * L["kbe"][i]))
            dglast = jnp.sum(t_kd[:, 0:1], axis=0, keepdims=True) + dgams[i] * gams[i][:, 0:1]
            dgc = dgc + jnp.where(row == C - 1, dglast, 0.0)
            dqkv_ref[0, rs, cs] = dqs[i]
            dqkv_ref[1, rs, cs] = dks[i]
            dqkv_ref[2, rs, cs] = dvbs[i] * bes[i]
            db_ref[rs, cs] = _rsum(dkbs[i] * ks[i]) + _rsum(dvbs[i] * vs[i])
            dg_ref[rs, cs] = _cumsum_rows(dgc, reverse=True)

    blk = lambda off: pl.BlockSpec((R, HB * dk), lambda h, n: (NB - 1 - n, off + h))
    W = H * dk
    return _pcall(body, name="delta_bwd", grid=(G, NB),
                  in_specs=[blk(0), blk(G), blk(2 * G), blk(0), blk(0),
                            pl.BlockSpec((HB, CB, dk, dk), lambda h, n: (h, NB - 1 - n, 0, 0)), blk(0)],
                  out_specs=(pl.BlockSpec((3, R, HB * dk), lambda h, n: (0, NB - 1 - n, h)), blk(0), blk(0)),
                  out_shape=(_sds((3, S, W), F32), _sds((S, W), F32), _sds((S, W), F32)),
                  scratch_shapes=[pltpu.VMEM((HB, dk, dk), F32)],
                  compiler_params=_cp("arbitrary", "arbitrary"))(qkv, qkv, qkv, beta_b, g_b, states, do)


def _rope_consts():
    lane = np.arange(LANES) % SW_HD
    half = ROT_DIM // 2
    inv = (ROPE_THETA ** (-np.arange(half, dtype=np.float32) * np.float32(2.0 / ROT_DIM))).astype(np.float32)
    freq = np.where(lane < ROT_DIM, inv[lane % half], 0.0).astype(np.float32)
    lo = (lane < half).astype(np.float32)
    hi = ((lane >= half) & (lane < ROT_DIM)).astype(np.float32)
    return jnp.asarray(np.stack([freq, -lo, hi] + [np.zeros(LANES, np.float32)] * 5))


def _rope_tables(pos_col):
    S = pos_col.shape[0]
    tq = _tile(S, 1024, SUBLANES)

    def fn(i, p_ref, c_ref, cos_ref, s1_ref, s2_ref):
        ang = p_ref[...].astype(F32) * c_ref[0:1, :]
        sn = jnp.sin(ang)
        cos_ref[...] = jnp.cos(ang)
        s1_ref[...] = sn * c_ref[1:2, :]
        s2_ref[...] = sn * c_ref[2:3, :]

    o, r = _sds((S, LANES), F32), _rb(tq, LANES)
    return _rows("rope_tables", fn, S, tq, (pos_col, _rope_consts()), [_rb(tq, 1), _full((SUBLANES, LANES))], (o, o, o), (r, r, r))


def _wide(a, w):
    return a if w == LANES else jnp.tile(a, (1, w // LANES))


def _rope(x, cos, s1, s2):
    w, h = x.shape[1], ROT_DIM // 2
    return x * _wide(cos, w) + pltpu.roll(x, w - h, 1) * _wide(s1, w) + pltpu.roll(x, h, 1) * _wide(s2, w)


def _unrope(d, cos, s1, s2):
    w, h = d.shape[1], ROT_DIM // 2
    return d * _wide(cos, w) + pltpu.roll(d * _wide(s1, w), h, 1) + pltpu.roll(d * _wide(s2, w), w - h, 1)


def _swa_setup(n, q_ref, kc_ref, kp_ref, vc_ref, vp_ref, tc, tp):
    B = SW_BLOCK
    qr = _rope(q_ref[...].astype(F32), tc[0][...], tc[1][...], tc[2][...]) * (SW_HD ** -0.5)
    kw = jnp.concatenate([_rope(kp_ref[...].astype(F32), tp[0][...], tp[1][...], tp[2][...]),
                          _rope(kc_ref[...].astype(F32), tc[0][...], tc[1][...], tc[2][...])], axis=0)
    vw = jnp.concatenate([vp_ref[...], vc_ref[...]], axis=0).astype(F32)
    lane = lax.broadcasted_iota(jnp.int32, (2 * B, LANES), 1)
    heads = []
    for hk in range(SW_KV_HEADS):
        kh, vh = kw[:, hk * SW_HD:(hk + 1) * SW_HD], vw[:, hk * SW_HD:(hk + 1) * SW_HD]
        kk, vv = jnp.concatenate([kh, kh], axis=1), jnp.concatenate([vh, vh], axis=1)
        heads.append(tuple(jnp.where(sel, t, 0.0).astype(BF16) for t in (kk, vv) for sel in (lane < SW_HD, lane >= SW_HD)))
    prev = lax.broadcasted_iota(jnp.int32, (B, B), 1) > lax.broadcasted_iota(jnp.int32, (B, B), 0)
    return qr, heads, (prev, jnp.where(prev & (n == 0), -1e30, 0.0)), lane


def _fold(x, prev):
    return jnp.where(prev, x[:, :SW_BLOCK], x[:, SW_BLOCK:])


def _unfold(x, prev):
    return jnp.concatenate([jnp.where(prev, x, 0.0), jnp.where(prev, 0.0, x)], axis=1)


SWA_GROUPS = 4
SWA_GROUPS_BWD = 2


def _swa_probs(items, qs, heads, fold, sk_ref, G2):
    prev, bias = fold
    ss = [_fold(_bdot(qs[j], heads[j // G2][half], "nt"), prev) + bias for j, half in items]
    sks = [sk_ref[0:1, 2 * j + half:2 * j + half + 1] for j, half in items]
    ms = [jnp.maximum(jnp.max(s, axis=-1, keepdims=True), sk) for s, sk in zip(ss, sks)]
    ps = [jnp.exp(s - m) for s, m in zip(ss, ms)]
    es = [jnp.exp(sk - m) for sk, m in zip(sks, ms)]
    inv = [1.0 / (jnp.sum(p, axis=-1, keepdims=True) + e) for p, e in zip(ps, es)]
    return [p * i for p, i in zip(ps, inv)], [e * i for e, i in zip(es, inv)]


def _swa_specs(W, cb_q, cb_k):
    B = SW_BLOCK
    assert (W // LANES) % SWA_GROUPS == 0 and (W // LANES) % SWA_GROUPS_BWD == 0
    cur = lambda w, cb: pl.BlockSpec((B, w), lambda n: (n, cb))
    prv = lambda w, cb: pl.BlockSpec((B, w), lambda n: (jnp.maximum(n - 1, 0), cb))
    specs = [cur(W, cb_q), cur(LANES, cb_k), prv(LANES, cb_k), cur(LANES, cb_k + 1), prv(LANES, cb_k + 1)]
    return specs + [cur(LANES, 0)] * 3 + [prv(LANES, 0)] * 3 + [_full((1, LANES))]


def _swa_fwd(proj, tabs, sinks_row, W, cb_q, cb_k):
    S = proj.shape[0]
    G2 = SW_Q_HEADS // SW_KV_HEADS // 2

    def body(q_ref, kc_ref, kp_ref, vc_ref, vp_ref, c0, c1, c2, p0, p1, p2, sk_ref, o_ref):
        n = pl.program_id(0)
        qr, heads, fold, _ = _swa_setup(n, q_ref, kc_ref, kp_ref, vc_ref, vp_ref, (c0, c1, c2), (p0, p1, p2))
        qs = [qr[:, j * LANES:(j + 1) * LANES].astype(BF16) for j in range(W // LANES)]
        for j0 in range(0, W // LANES, SWA_GROUPS):
            items = [(j, half) for j in range(j0, j0 + SWA_GROUPS) for half in range(2)]
            probs, _ = _swa_probs(items, qs, heads, fold, sk_ref, G2)
            pv = [_bdot(_unfold(p, fold[0]), heads[j // G2][2 + half]) for p, (j, half) in zip(probs, items)]
            for g in range(SWA_GROUPS):
                o_ref[:, (j0 + g) * LANES:(j0 + g + 1) * LANES] = (pv[2 * g] + pv[2 * g + 1]).astype(BF16)

    t = tuple(tabs)
    return _pcall(body, name="swa_fwd", grid=(S // SW_BLOCK,), in_specs=_swa_specs(W, cb_q, cb_k),
                  out_specs=pl.BlockSpec((SW_BLOCK, W), lambda n: (n, 0)), out_shape=_sds((S, W), BF16),
                  compiler_params=_cp("arbitrary"))(proj, proj, proj, proj, proj, *t, *t, sinks_row)


def _swa_bwd(proj, tabs, sinks_row, do, W, cb_q, cb_k):
    S = proj.shape[0]
    B = SW_BLOCK
    G2 = SW_Q_HEADS // SW_KV_HEADS // 2
    SKR = -(-SW_Q_HEADS // SUBLANES) * SUBLANES

    def body(q_ref, kc_ref, kp_ref, vc_ref, vp_ref, c0, c1, c2, p0, p1, p2, sk_ref, do_ref,
             dq_ref, dkc_ref, dkp_ref, dvc_ref, dvp_ref, dsk_ref):
        n = pl.program_id(0)
        qr, heads, fold, lane = _swa_setup(n, q_ref, kc_ref, kp_ref, vc_ref, vp_ref, (c0, c1, c2), (p0, p1, p2))
        prev = fold[0]

        @pl.when(n == 0)
        def _():
            dsk_ref[...] = jnp.zeros((SKR, LANES), F32)

        acc_k = [jnp.zeros((2 * B, LANES), F32) for _ in range(SW_KV_HEADS)]
        acc_v = [jnp.zeros((2 * B, LANES), F32) for _ in range(SW_KV_HEADS)]
        qs = [qr[:, j * LANES:(j + 1) * LANES].astype(BF16) for j in range(W // LANES)]
        dos = [do_ref[:, j * LANES:(j + 1) * LANES].astype(BF16) for j in range(W // LANES)]
        dqs = []
        for j0 in range(0, W // LANES, SWA_GROUPS_BWD):
            items = [(j, half) for j in range(j0, j0 + SWA_GROUPS_BWD) for half in range(2)]
            probs, psinks = _swa_probs(items, qs, heads, fold, sk_ref, G2)
            dps = [_fold(_bdot(dos[j], heads[j // G2][2 + half], "nt"), prev) for j, half in items]
            deltas = [jnp.sum(p * dp, axis=-1, keepdims=True) for p, dp in zip(probs, dps)]
            dss = [_unfold(p * (dp - dl), prev).astype(BF16) for p, dp, dl in zip(probs, dps, deltas)]
            pbs = [_unfold(p, prev).astype(BF16) for p in probs]
            dqp = [_bdot(ds, heads[j // G2][half]) for ds, (j, half) in zip(dss, items)]
            dkk = [_bdot(ds, qs[j], "tn") for ds, (j, half) in zip(dss, items)]
            dvv = [_bdot(p, dos[j], "tn") for p, (j, half) in zip(pbs, items)]
            for i, (j, half) in enumerate(items):
                hk, h = j // G2, 2 * j + half
                sel = (lane < SW_HD) if half == 0 else (lane >= SW_HD)
                acc_k[hk] = acc_k[hk] + jnp.where(sel, dkk[i], 0.0)
                acc_v[hk] = acc_v[hk] + jnp.where(sel, dvv[i], 0.0)
                dsk_ref[h:h + 1, :] += jnp.broadcast_to(-jnp.sum(psinks[i] * deltas[i], axis=0, keepdims=True), (1, LANES))
            dqs += [dqp[2 * g] + dqp[2 * g + 1] for g in range(SWA_GROUPS_BWD)]
        dq = jnp.concatenate(dqs, axis=1) * (SW_HD ** -0.5)
        dq_ref[...] = _unrope(dq, c0[...], c1[...], c2[...]).astype(BF16)
        fold = lambda a: a[:, :SW_HD] + a[:, SW_HD:]
        dkw = jnp.concatenate([fold(a) for a in acc_k], axis=1)
        dvw = jnp.concatenate([fold(a) for a in acc_v], axis=1)
        dkp_ref[...], dkc_ref[...] = dkw[:B], dkw[B:]
        dvp_ref[...], dvc_ref[...] = dvw[:B], dvw[B:]

    t = tuple(tabs)
    blk = lambda w: pl.BlockSpec((B, w), lambda n: (n, 0))
    o = _sds((S, LANES), F32)
    return _pcall(body, name="swa_bwd", grid=(S // B,), in_specs=_swa_specs(W, cb_q, cb_k) + [blk(W)],
                  out_specs=(blk(W), blk(LANES), blk(LANES), blk(LANES), blk(LANES), _full((SKR, LANES))),
                  out_shape=(_sds((S, W), BF16), o, o, o, o, _sds((SKR, LANES), F32)),
                  compiler_params=_cp("arbitrary"))(proj, proj, proj, proj, proj, *t, *t, sinks_row, do)


def _swa_kv_combine(dkc, dkp, dvc, dvp, tabs):
    S = dkc.shape[0]
    B = SW_BLOCK
    nb = S // B

    def fn(n, kc_ref, kp_ref, vc_ref, vp_ref, c0, c1, c2, o_ref):
        more = n < nb - 1
        dk = kc_ref[...] + jnp.where(more, kp_ref[...], 0.0)
        dv = vc_ref[...] + jnp.where(more, vp_ref[...], 0.0)
        o_ref[...] = jnp.concatenate([_unrope(dk, c0[...], c1[...], c2[...]), dv], axis=1).astype(BF16)

    cur = _rb(B, LANES)
    nxt = pl.BlockSpec((B, LANES), lambda n: (jnp.minimum(n + 1, nb - 1), 0))
    return _rows("swa_kv_combine", fn, S, B, (dkc, dkp, dvc, dvp, *tabs), [cur, nxt, cur, nxt, cur, cur, cur],
                 _sds((S, 2 * LANES), BF16), _rb(B, 2 * LANES))


ANY = pl.BlockSpec(memory_space=pl.ANY)


def _place():
    x, y, c = lax.axis_index("x"), lax.axis_index("y"), lax.axis_index("c")
    return x, y, c, [(1 - x, y), (x, 1 - y), (1 - x, 1 - y)]


def _comm_call(name, body, out_shapes, n_sems, n_local, *ins):
    return _pcall(body, name=name, out_shape=tuple(out_shapes), in_specs=[ANY] * len(ins), out_specs=tuple(ANY for _ in out_shapes),
                  scratch_shapes=[pltpu.SemaphoreType.DMA((n_sems,)), pltpu.SemaphoreType.DMA((n_sems,)),
                                  pltpu.SemaphoreType.DMA((n_local,))])(*ins)


def _remote(src, dst, send, recv, k, to):
    return pltpu.make_async_remote_copy(src_ref=src, dst_ref=dst, send_sem=send.at[k], recv_sem=recv.at[k], device_id=to,
                                        device_id_type=MESH)


def _gather_chips(name, arrs):
    n = len(arrs)
    Lh = arrs[0].shape[0] // 2

    def body(*refs):
        w, o, (send, recv, _) = refs[:n], refs[n:2 * n], refs[2 * n:]
        x, y, c, chips = _place()
        me, sib = 2 * x + y, (x, y, 1 - c)
        own, other = pl.ds(c * Lh, Lh), pl.ds((1 - c) * Lh, Lh)
        idx = [2 * cx + cy for cx, cy in chips]
        first = [[_remote(w[a].at[own], o[a].at[me, own], send, recv, 6 * a + j, (*chips[j], c)) for j in range(3)] for a in range(n)]
        passed = [[_remote(o[a].at[idx[j], own], o[a].at[idx[j], own], send, recv, 6 * a + 3 + j, sib) for j in range(3)] for a in range(n)]
        for cp in [cp for row in first for cp in row]:
            cp.start()
        for j in range(3):
            for a in range(n):
                _remote(w[a].at[own], o[a].at[idx[j], own], send, recv, 6 * a + j, (*chips[j], c)).wait_recv()
                passed[a][j].start()
        for j in range(3):
            for a in range(n):
                _remote(w[a].at[other], o[a].at[idx[j], other], send, recv, 6 * a + 3 + j, sib).wait_recv()
        for cp in [cp for row in first + passed for cp in row]:
            cp.wait_send()

    return _comm_call(name, body, [_sds((4,) + a.shape, a.dtype) for a in arrs], 6 * n, 1, *arrs)


def _pair_swap(name, arrs, whole=False):
    n = len(arrs)
    Lh = arrs[0].shape[0] if whole else arrs[0].shape[0] // 2

    def body(*refs):
        g, o, (send, recv, _) = refs[:n], refs[n:2 * n], refs[2 * n:]
        x, y, c, _ = _place()
        cps = [_remote(g[a] if whole else g[a].at[pl.ds((1 - c) * Lh, Lh)], o[a], send, recv, a, (x, y, 1 - c)) for a in range(n)]
        for cp in cps:
            cp.start()
        for cp in cps:
            cp.wait()

    return _comm_call(name, body, [_sds((Lh,) + a.shape[1:], a.dtype) for a in arrs], n, 1, *arrs)


def _chip_slice(ref, axis, s):
    if axis is None:
        return ref.at[s]
    q = ref.shape[axis] // 4
    start = s * q if isinstance(s, int) else pl.multiple_of(s * q, q)
    return ref.at[tuple([slice(None)] * axis + [pl.ds(start, q)])]


def _scatter_chips(name, items):
    n = len(items)
    part = lambda a, ax: a.shape[1:] if ax is None else tuple(d // 4 if i == ax else d for i, d in enumerate(a.shape))

    def body(*refs):
        p, o, (send, recv, _) = refs[:n], refs[n:2 * n], refs[2 * n:]
        x, y, c, chips = _place()
        me = 2 * x + y
        idx = [2 * cx + cy for cx, cy in chips]
        cps = [_remote(_chip_slice(p[a], items[a][1], idx[j]), o[a].at[me], send, recv, 3 * a + j, (*chips[j], c))
               for a in range(n) for j in range(3)]
        for cp in cps:
            cp.start()
        for a in range(n):
            for j in range(3):
                _remote(_chip_slice(p[a], items[a][1], me), o[a].at[idx[j]], send, recv, 3 * a + j, (*chips[j], c)).wait_recv()
        for cp in cps:
            cp.wait_send()

    return _comm_call(name, body, [_sds((4,) + part(a, ax), a.dtype) for a, ax in items], 3 * n, 1, *[a for a, _ in items])


def _own_part(a, axis, me):
    if axis is None:
        return lax.dynamic_index_in_dim(a, me, 0, keepdims=False)
    q = a.shape[axis] // 4
    return lax.dynamic_slice_in_dim(a, me * q, q, axis)


HBM = pl.BlockSpec(memory_space=pltpu.HBM)
SEM = pl.BlockSpec(memory_space=pltpu.SEMAPHORE)
EFFECT = pltpu.SideEffectType.DATAFLOW_SIDE_EFFECTING


def _split_start(name, arrs, land_shapes, plan, nc, after=None):
    n = len(arrs)
    lands = [lax.empty(s.shape, s.dtype) for s in land_shapes]
    ins = list(arrs) + lands + ([] if after is None else [after])

    def body(*refs):
        outs = refs[len(ins):]
        for k, (src, dst, _, peer) in enumerate(plan(refs[:n], refs[n:n + len(lands)])):
            pltpu.make_async_remote_copy(src_ref=src, dst_ref=dst, send_sem=outs[k], recv_sem=outs[nc + k], device_id=peer,
                                         device_id_type=MESH).start()
        outs[-1][...] = jnp.zeros((SUBLANES, LANES), F32)

    nt = n + len(lands)
    thru = [pltpu.HBM(a.shape, a.dtype) for a in list(arrs) + lands]
    outs = _pcall(body, name=name, out_shape=tuple([pltpu.SemaphoreType.DMA(())] * (2 * nc) + thru + [_sds((SUBLANES, LANES), F32)]),
                  in_specs=[HBM] * nt + [ANY] * (len(ins) - nt),
                  out_specs=tuple([SEM] * (2 * nc) + [HBM] * nt + [pl.BlockSpec(memory_space=pltpu.VMEM)]),
                  input_output_aliases={i: 2 * nc + i for i in range(nt)},
                  compiler_params=pltpu.CompilerParams(has_side_effects=EFFECT))(
        *[pltpu.with_memory_space_constraint(a, pltpu.HBM) for a in ins[:nt]], *ins[nt:])
    return dict(sems=outs[:2 * nc], arrs=outs[2 * nc:2 * nc + n], lands=outs[2 * nc + n:2 * nc + nt], token=outs[-1], plan=plan, nc=nc)


def _split_wait(name, handle, after):
    arrs, lands, sems, nc = list(handle["arrs"]), list(handle["lands"]), list(handle["sems"]), handle["nc"]
    n, nt = len(arrs), len(arrs) + len(lands)

    def body(*refs):
        sem = refs[nt:nt + 2 * nc]
        for k, (src, _, landing, peer) in enumerate(handle["plan"](refs[:n], refs[n:nt])):
            cp = pltpu.make_async_remote_copy(src_ref=src, dst_ref=landing, send_sem=sem[k], recv_sem=sem[nc + k], device_id=peer,
                                              device_id_type=MESH)
            cp.wait_send()
            cp.wait_recv()

    thru = tuple(pltpu.HBM(a.shape, a.dtype) for a in arrs + lands)
    outs = _pcall(body, name=name, out_shape=thru, in_specs=[HBM] * nt + [SEM] * (2 * nc) + [ANY], out_specs=tuple([HBM] * nt),
                  input_output_aliases={i: i for i in range(nt)},
                  compiler_params=pltpu.CompilerParams(has_side_effects=EFFECT))(*arrs, *lands, *sems, after)
    return list(outs[:n]), list(outs[n:])


WHOLE = "whole"


def _plan_chips(axes):
    def plan(src, land):
        x, y, c, chips = _place()
        idx = [2 * cx + cy for cx, cy in chips]
        part = lambda a, s: src[a] if axes[a] is WHOLE else _chip_slice(src[a], axes[a], s)
        return [(part(a, idx[j]), land[a].at[2 * x + y], land[a].at[idx[j]], (*chips[j], c))
                for a in range(len(land)) for j in range(3)]
    return plan


def _plan_sibling(half):
    def plan(src, land):
        x, y, c, _ = _place()
        lh = lambda a: src[a].shape[0] // 2
        return [(src[a].at[pl.ds((1 - c) * lh(a), lh(a))] if half else src[a], land[a], land[a], (x, y, 1 - c))
                for a in range(len(land))]
    return plan


def _chips_start(name, arrs, axes, after=None):
    part = lambda a, ax: a.shape if ax is WHOLE else a.shape[1:] if ax is None else tuple(d // 4 if i == ax else d for i, d in enumerate(a.shape))
    return _split_start(name, arrs, [_sds((4,) + part(a, ax), a.dtype) for a, ax in zip(arrs, axes)], _plan_chips(axes), 3 * len(arrs), after)


def _sibling_start(name, arrs, half, after=None):
    shp = lambda a: (a.shape[0] // 2,) + a.shape[1:] if half else a.shape
    return _split_start(name, arrs, [_sds(shp(a), a.dtype) for a in arrs], _plan_sibling(half), len(arrs), after)


def _gather_all(name, b):
    R, C = b.shape
    flips = [(dx, dy, dc) for dx in (0, 1) for dy in (0, 1) for dc in (0, 1)][1:]

    def body(b_ref, o_ref, send, recv, lsem):
        x, y, c, _ = _place()
        me = 4 * x + 2 * y + c
        peers = [(x ^ dx, y ^ dy, c ^ dc) for dx, dy, dc in flips]
        mine = pltpu.make_async_copy(b_ref, o_ref.at[me], lsem.at[0])
        mine.start()
        cps = [_remote(b_ref, o_ref.at[me], send, recv, k, peer) for k, peer in enumerate(peers)]
        for cp in cps:
            cp.start()
        for k, (px, py, pc) in enumerate(peers):
            _remote(b_ref, o_ref.at[4 * px + 2 * py + pc], send, recv, k, (px, py, pc)).wait_recv()
        for cp in cps:
            cp.wait_send()
        mine.wait()

    return _comm_call(name, body, [_sds((8, R, C), b.dtype)], 7, 1, b)[0]


def _block_rows(rows, width):
    return _tile(rows, max(SUBLANES, (1 << 19) // width), SUBLANES)


def _add_half(name, g, got):
    L, A, B = g.shape
    Lh = L // 2
    tq = _block_rows(A, B)

    def body(c_ref, g_ref, r_ref, o_ref):
        o_ref[...] = (g_ref[...] + r_ref[...]).astype(BF16)

    spec = pltpu.PrefetchScalarGridSpec(
        num_scalar_prefetch=1, grid=(Lh, A // tq),
        in_specs=[pl.BlockSpec((1, tq, B), lambda l, i, c_ref: (c_ref[0] * Lh + l, i, 0)),
                  pl.BlockSpec((1, tq, B), lambda l, i, c_ref: (l, i, 0))],
        out_specs=pl.BlockSpec((1, tq, B), lambda l, i, c_ref: (l, i, 0)))
    return _pcall(body, name=name, grid_spec=spec, out_shape=_sds((Lh, A, B), BF16),
                  compiler_params=_cp("arbitrary", "arbitrary"))(lax.axis_index("c").reshape(1).astype(jnp.int32), g, got)


def _sum_slots(name, a):
    n, R, C = a.shape
    tq = _block_rows(R, n * C)

    def fn(i, a_ref, o_ref):
        t = a_ref[0].astype(F32)
        for s in range(1, n):
            t = t + a_ref[s].astype(F32)
        o_ref[...] = t

    return _rows(name, fn, R, tq, (a,), [pl.BlockSpec((n, tq, C), lambda i: (0, i, 0))], _sds((R, C), F32), _rb(tq, C))


def _adam_update(w, g, m, v):
    mn = ADAM_B1 * m + (1.0 - ADAM_B1) * g
    vn = ADAM_B2 * v + (1.0 - ADAM_B2) * (g * g)
    m_hat = mn / (1.0 - ADAM_B1 ** ADAM_STEP)
    v_hat = vn / (1.0 - ADAM_B2 ** ADAM_STEP)
    return -ADAM_LR * (m_hat / (jnp.sqrt(v_hat) + ADAM_EPS) + ADAM_WD * w), mn, vn


def _adamw(name, w, g, m, v):
    R, C = w.shape
    tq = _tile(R, 256, SUBLANES)

    def fn(i, w_ref, g_ref, m_ref, v_ref, d_ref, mo_ref, vo_ref):
        d_ref[...], mo_ref[...], vo_ref[...] = _adam_update(w_ref[...], g_ref[...], m_ref[...], v_ref[...])

    r, o = _rb(tq, C), _sds((R, C), F32)
    return _rows(name, fn, R, tq, (w, g, m, v), [r, r, r, r], (o, o, o), (r, r, r))


def _adamw_halves(name, w, mine, theirs, m, v, l0, prev=None):
    L, A, B = w.shape
    Lh = mine.shape[0]
    tq = _tile(A, 256, SUBLANES)

    def body(c_ref, w_ref, a_ref, b_ref, m_ref, v_ref, *refs):
        g_ref, d_ref, mo_ref, vo_ref = refs[-4:]
        is_mine = pl.program_id(0) // Lh == c_ref[0]
        g = jnp.where(is_mine, a_ref[...], b_ref[...])
        g_ref[...] = g
        d_ref[...], mo_ref[...], vo_ref[...] = _adam_update(w_ref[...], g, m_ref[...], v_ref[...])

    full = pl.BlockSpec((1, tq, B), lambda l, i, c_ref: (l0 + l, i, 0))
    half = pl.BlockSpec((1, tq, B), lambda l, i, c_ref: (l % Lh, i, 0))
    o = _sds((L, A, B), F32)
    prev = list(prev or ())
    spec = pltpu.PrefetchScalarGridSpec(num_scalar_prefetch=1, grid=(2 * Lh, A // tq), in_specs=[full, half, half, full, full] + [ANY] * len(prev),
                                        out_specs=(full, full, full, full))
    return _pcall(body, name=name, grid_spec=spec, out_shape=(o, o, o, o), input_output_aliases={6 + i: i for i in range(len(prev))},
                  compiler_params=_cp("arbitrary", "arbitrary"))(lax.axis_index("c").reshape(1).astype(jnp.int32), w, mine, theirs, m, v, *prev)


def _pack(arrs, width, lead=()):
    nl = len(lead)
    flat = jnp.concatenate([a.reshape(lead + (-1,)) for a in arrs], axis=nl)
    n = flat.shape[-1]
    unit = PACK_ROWS * width
    tot = -(-n // unit) * unit
    flat = jnp.pad(flat, [(0, 0)] * nl + [(0, tot - n)])
    return flat.reshape(lead + (tot // width, width))


def _unpack(buf, shapes, lead=()):
    flat = buf.reshape(lead + (-1,))
    out, off = [], 0
    for s in shapes:
        n = int(np.prod(s))
        out.append(flat[..., off:off + n].reshape(lead + tuple(s)))
        off += n
    return out


def _in_groups(W, H):
    o_sq = 4 * W + 2 * H
    o_k = o_sq + W
    o_g = o_k + 2 * KV_W
    return [(0, 4 * W), (o_sq, o_k), (o_g, o_g + 2 * W), (o_k, o_g), (4 * W, o_sq)]


def _relayout_in(shards, W, H):
    c4 = sum(hi - lo for lo, hi in _in_groups(W, H)) // 4
    parts = []
    for lo, hi in _in_groups(W, H):
        for s in range(4):
            a, b = max(lo, s * c4), min(hi, (s + 1) * c4)
            if a < b:
                parts.append(shards[s][:, a - s * c4:b - s * c4])
    parts.append(jnp.zeros((shards.shape[1], BA_W - 2 * H), shards.dtype))
    return jnp.concatenate(parts, axis=1)


def _shard_in(d, W, H):
    groups = _in_groups(W, H)
    starts = [sum(hi - lo for lo, hi in groups[:i]) for i in range(len(groups))]
    stored = sorted(zip(groups, starts))
    c4 = sum(hi - lo for lo, hi in groups) // 4
    out = []
    for s in range(4):
        parts = []
        for (lo, hi), at in stored:
            a, b = max(lo, s * c4), min(hi, (s + 1) * c4)
            if a < b:
                parts.append(d[:, :, at + a - lo:at + b - lo])
        out.append(jnp.concatenate(parts, axis=2))
    return jnp.stack(out)


def _lane_row(vals, at):
    return jnp.pad(vals, (at, LANES - at - vals.shape[0]))[None]


def _layer_fwd(x, lw, tabs, W, H, more=None):
    D = x.shape[1]
    cbk = 7 * W // LANES
    h = _pre_norm(x, lw["g1"])
    proj = _mm("mm_in", h, lw["win"], "nn", BF16, tn=768)
    ba = _mm("mm_ba", h, lw["win"][:, 7 * W + 2 * KV_W:], "nn", F32)
    qkv = _dn_prep(proj, lw["conv"], W)
    beta_b, g_b = _dn_gates(ba, lw["alog"], lw["dt"], H)
    o, st = _delta_fwd(qkv, beta_b, g_b, H, DELTA_CB, DELTA_HB)
    oa = _dn_out(o, proj, lw["ng"], W, 3)
    ob = _swa_fwd(proj, tabs, lw["sinks"], W, 4, cbk)
    if more is not None:
        lw.update(more(ob))
    ya = _mm("mm_up_dn", oa, lw["wup_dn"], "nn", BF16)
    yb = _mm("mm_up_sw", ob, lw["wup_sw"], "nn", BF16)
    mixin = _mix(proj, ya, yb, D, 5)
    mix = _mm("mm_o", mixin, lw["wo"], "nn", F32)
    x1, h2 = _post_mix(x, mix, lw["g2"], lw["g3"])
    f1, act = _mm("mm_ff1", h2, lw["wff1"], "nn", out_dtypes=(BF16, BF16), epi=lambda acc: (acc, jnp.square(jnp.maximum(acc, 0.0))))
    ff = _mm("mm_ff2", act, lw["wff2"], "nn", F32)
    x2 = _post_mlp(x1, ff, lw["g4"])
    saved = dict(x=x, h=h, proj=proj, ba=ba, qkv=qkv, beta_b=beta_b, g_b=g_b, o=o, st=st, oa=oa, ob=ob, ya=ya, yb=yb,
                 mixin=mixin, mix=mix, x1=x1, h2=h2, f1=f1, act=act, ff=ff)
    return x2, saved


def _layer_bwd(dx2, lw, sv, tabs, W, H, l, big, weights_done=None):
    D = dx2.shape[1]
    cbk = 7 * W // LANES
    big = dict(big)
    dff, dg4 = _post_mlp_bwd(sv["ff"], lw["g4"], dx2)
    df1 = _mm("mm_ff2_dx", dff, lw["wff2"], "nt", BF16, extras=(sv["f1"],),
              epi=lambda acc, f1: (acc * 2.0 * jnp.maximum(f1.astype(F32), 0.0),))
    big["w_ff2"] = _mm("mm_ff2_dw", sv["act"], dff, "tn", slab=(big["w_ff2"], l))
    dh2 = _mm("mm_ff1_dx", df1, lw["wff1"], "nt", F32)
    big["w_ff1"] = _mm("mm_ff1_dw", sv["h2"], df1, "tn", slab=(big["w_ff1"], l))
    dx1, dmix, dg3, dg2 = _mid_bwd(sv["x1"], lw["g3"], dh2, dx2, sv["mix"], lw["g2"])
    dmixin = _mm("mm_o_dx", dmix, lw["wo"], "nt", BF16)
    big["w_o"] = _mm("mm_o_dw", sv["mixin"], dmix, "tn", slab=(big["w_o"], l))
    dya, dyb, dga, dgb = _mix_bwd(sv["proj"], sv["ya"], sv["yb"], dmixin, D, 5)
    doa = _mm("mm_up_dn_dx", dya, lw["wup_dn"], "nt", BF16)
    big["w_up_dn"] = _mm("mm_up_dn_dw", sv["oa"], dya, "tn", slab=(big["w_up_dn"], l))
    dob = _mm("mm_up_sw_dx", dyb, lw["wup_sw"], "nt", BF16)
    big["w_up_sw"] = _mm("mm_up_sw_dw", sv["ob"], dyb, "tn", slab=(big["w_up_sw"], l))
    do, dz, dng = _dn_out_bwd(sv["o"], sv["proj"], lw["ng"], doa, W, 3)
    dqkvn, dbeta_b, dg_b = _delta_bwd(sv["qkv"], sv["beta_b"], sv["g_b"], sv["st"], do, H, DELTA_CB, DELTA_HB)
    dba, dalog, ddt = _dn_gates_bwd(sv["ba"], lw["alog"], lw["dt"], dbeta_b, dg_b, H)
    dc, dconv = _dn_prep_bwd_a(sv["proj"], lw["conv"], dqkvn, W)
    dqkv = _dn_prep_bwd_b(dc, lw["conv"], W)
    dq_sw, dkc, dkp, dvc, dvp, dsk = _swa_bwd(sv["proj"], tabs, lw["sinks"], dob, W, 4, cbk)
    dkv = _swa_kv_combine(dkc, dkp, dvc, dvp, tabs)
    dproj = jnp.concatenate([dqkv, dz, dq_sw, dga, dgb, dkv, dba], axis=1)
    big["w_in"] = _mm("mm_in_dw", sv["h"], dproj, "tn", tn=768, slab=(big["w_in"], l))
    win = lw["win"]
    if weights_done is not None:
        win = win + weights_done(big).astype(BF16)
    dh = _mm("mm_in_dx", dproj, win, "nt", F32, tk=768)
    dx, dg1 = _pre_norm_bwd(sv["x"], lw["g1"], dh, dx1)
    grads = dict(pre_mix_g=dg1[0], dn_conv_w=dconv, dn_a_log=dalog[0, H:2 * H], dn_dt_bias=ddt[0, H:2 * H], dn_norm_g=dng[0],
                 sw_sinks=dsk[:SW_Q_HEADS, 0], post_mix_g=dg2[0], pre_mlp_g=dg3[0], post_mlp_g=dg4[0])
    return dx, grads, big


_WEIGHTS = ["pre_mix_g", "w_in", "dn_conv_w", "dn_a_log", "dn_dt_bias", "dn_norm_g", "sw_sinks", "w_up_dn", "w_up_sw", "w_o",
            "post_mix_g", "pre_mlp_g", "w_ff1", "w_ff2", "post_mlp_g"]
_BIG = {"w_in": 2, "w_up_dn": 1, "w_up_sw": 1, "w_o": 1, "w_ff1": 2, "w_ff2": 1}
_SMALL = [n for n in _WEIGHTS if n not in _BIG]


def _step(P):
    x, target = P["x"][0], P["loss_target"][0]
    S, D = x.shape
    L = P["pre_mix_g"].shape[0]
    H, W = DN_HEADS, DN_HEADS * DN_DK
    assert W == D == SW_Q_HEADS * SW_HD and KV_W == LANES
    me = 2 * lax.axis_index("x") + lax.axis_index("y")

    assert L % 4 == 0
    names = list(_BIG) + ["dn_conv_w"]
    local = [P[n].astype(BF16) for n in _BIG] + [P["dn_conv_w"]]
    early_names = ("w_in", "dn_conv_w")
    tail_names = [n for n in names if n not in early_names]
    own_slot = lambda gathered, mine: [lax.dynamic_update_slice_in_dim(g, w[None], me, 0) for g, w in zip(gathered, mine)]
    gather = lambda name, arrs, after=None: _chips_start(name, arrs, [WHOLE] * len(arrs), after)
    arrived = lambda name, h, after, keys: dict(zip(keys, own_slot(*reversed(_split_wait(name, h, after)))))
    h_first = gather("weights_first_start", [a[:1] for n, a in zip(names, local) if n in early_names])
    early = arrived("weights_first_wait", h_first, x, early_names)
    h_tail = gather("weights_tail_start", [a[:1] for n, a in zip(names, local) if n in tail_names], early["w_in"])
    h_next = gather("weights_next_start", [a[1:2] for a in local], h_tail["token"])
    h_rest = gather("weights_rest_start", [a[2:] for a in local], h_next["token"])

    def head(full, l, k):
        return dict(
            g1=P["pre_mix_g"][l][None], win=_relayout_in(full["w_in"][:, k], W, H),
            conv=jnp.concatenate([full["dn_conv_w"][s, k] for s in range(4)], axis=-1),
            alog=_lane_row(P["dn_a_log"][l], H), dt=_lane_row(P["dn_dt_bias"][l], H), ng=P["dn_norm_g"][l][None],
            sinks=_lane_row(P["sw_sinks"][l], 0), g2=P["post_mix_g"][l][None], g3=P["pre_mlp_g"][l][None], g4=P["post_mlp_g"][l][None])

    def tail(full, k):
        rows = lambda n: full[n][:, k].reshape(-1, full[n].shape[-1])
        return dict(wup_dn=rows("w_up_dn"), wup_sw=rows("w_up_sw"), wo=rows("w_o"), wff2=rows("w_ff2"),
                    wff1=jnp.concatenate([full["w_ff1"][s, k] for s in range(4)], axis=-1))

    tabs = _rope_tables(P["positions"].reshape(S, 1))
    lws = [head(early, 0, 0)]
    lws[0]["g1"] = lws[0]["g1"] + h_rest["token"][0, 0]

    saved = []
    for l in range(L):
        if l == 1:
            late = arrived("weights_next_wait", h_next, x, names)
            lws.append({**head(late, 1, 0), **tail(late, 0)})
        if l == 2:
            late = arrived("weights_rest_wait", h_rest, x, names)
            lws.extend({**head(late, k + 2, k), **tail(late, k)} for k in range(L - 2))
        first_tail = lambda after: tail(arrived("weights_tail_wait", h_tail, after, tail_names), 0)
        x, sv = _layer_fwd(x, lws[l], tabs, W, H, first_tail if l == 0 else None)
        saved.append(sv)
    loss_row, dx = _loss_head(x, target)

    Lb = L // 2
    layer_grads = [None] * L
    F = 4 * P["w_ff1"].shape[2]
    per_layer = dict(w_in=(D, 7 * W + 2 * KV_W + BA_W), w_up_dn=(W, D), w_up_sw=(W, D), w_o=(D, D), w_ff1=(D, F), w_ff2=(F, D))
    batch = [{n: lax.empty((Lb,) + per_layer[n], F32) for n in _BIG} for _ in range(2)]
    axes = [None if n == "w_in" else ax for n, ax in _BIG.items()]

    def pair_sums(tag, h_swap, after):
        g, got = _split_wait("grad_swap_wait_" + tag, h_swap, after)
        part = {n: _add_half("grad_pair_add_%s_%s" % (tag, n), a, r) for n, a, r in zip(_BIG, g, got)}
        return [_shard_in(part[n], W, H) if n == "w_in" else part[n] for n in _BIG]

    def chip_sums(tag, h_scat, after):
        parts, slots = _split_wait("grad_scatter_wait_" + tag, h_scat, after)
        halves = []
        for n, s, a, ax in zip(_BIG, slots, parts, axes):
            s = lax.dynamic_update_slice_in_dim(s, _own_part(a, ax, me)[None], me, 0)
            halves.append(_sum_slots("grad_chip_sum_%s_%s" % (tag, n), s.reshape(4, -1, s.shape[-1])).reshape(s.shape[1:]))
        h = _sibling_start("grad_share_start_" + tag, halves, False)
        return _split_wait("grad_share_wait_" + tag, h, halves[0])

    swaps = {}

    def swap_start(tag):
        def hook(big):
            swaps[tag] = _sibling_start("grad_swap_start_" + tag, [big[n] for n in _BIG], True)
            return swaps[tag]["token"][0, 0]
        return hook

    for l in reversed(range(L)):
        hook = swap_start("hi") if l == Lb else swap_start("lo") if l == 0 else None
        dx, layer_grads[l], batch[l // Lb] = _layer_bwd(dx, lws[l], saved[l], tabs, W, H, l % Lb, batch[l // Lb], hook)
        if l == Lb - 1:
            h_scat_hi = _chips_start("grad_scatter_start_hi", pair_sums("hi", swaps["hi"], dx), axes)
            if l > 0:
                lws[l - 1]["g4"] = lws[l - 1]["g4"] + h_scat_hi["token"][0, 0]
    h_scat_lo = _chips_start("grad_scatter_start_lo", pair_sums("lo", swaps["lo"], dx), axes)

    grads = {n: jnp.stack([layer_grads[l][n] for l in range(L)]) for n in _SMALL}
    small_shapes = [(1,)] + [grads[n].shape for n in _SMALL]
    packed = _pack([loss_row[0, :1] + h_scat_lo["token"][0, 0]] + [grads[n] for n in _SMALL], LANES)
    tot = _sum_slots("small_sum", _gather_all("small_gather", packed))
    small = _unpack(tot, small_shapes)
    loss = small[0][0]
    gsum, delta, new_m, new_v = dict(zip(_SMALL, small[1:])), {}, {}, {}
    cw = P["dn_conv_w"].shape[2]
    gsum["dn_conv_w"] = lax.dynamic_slice_in_dim(gsum["dn_conv_w"], me * cw, cw, axis=2)
    sm_shapes = [P[n].shape for n in _SMALL]
    outs = _adamw("adamw_small", *(_pack([src[pre + n] for n in _SMALL], LANES)
                                   for src, pre in ((P, ""), (gsum, ""), (P, "m_"), (P, "v_"))))
    for d, o in zip((delta, new_m, new_v), outs):
        d.update(zip(_SMALL, _unpack(o, sm_shapes)))

    upper = {n: _adamw_halves("adamw_hi_" + n, P[n], mine, their, P["m_" + n], P["v_" + n], Lb)
             for n, mine, their in zip(_BIG, *chip_sums("hi", h_scat_hi, outs[0]))}
    for n, mine, their in zip(_BIG, *chip_sums("lo", h_scat_lo, upper["w_in"][0])):
        gsum[n], delta[n], new_m[n], new_v[n] = _adamw_halves("adamw_lo_" + n, P[n], mine, their, P["m_" + n], P["v_" + n], 0, upper[n])

    return (loss, dx[None], *[gsum[n] for n in _WEIGHTS], *[delta[n] for n in _WEIGHTS],
            *[new_m[n] for n in _WEIGHTS], *[new_v[n] for n in _WEIGHTS])


def kernel(x, positions, pre_mix_g, w_in, dn_conv_w, dn_a_log, dn_dt_bias, dn_norm_g, sw_sinks, w_up_dn, w_up_sw, w_o, post_mix_g, pre_mlp_g, w_ff1, w_ff2, post_mlp_g, loss_target, m_pre_mix_g, m_w_in, m_dn_conv_w, m_dn_a_log, m_dn_dt_bias, m_dn_norm_g, m_sw_sinks, m_w_up_dn, m_w_up_sw, m_w_o, m_post_mix_g, m_pre_mlp_g, m_w_ff1, m_w_ff2, m_post_mlp_g, v_pre_mix_g, v_w_in, v_dn_conv_w, v_dn_a_log, v_dn_dt_bias, v_dn_norm_g, v_sw_sinks, v_w_up_dn, v_w_up_sw, v_w_o, v_post_mix_g, v_pre_mlp_g, v_w_ff1, v_w_ff2, v_post_mlp_g):
    vals = (x, positions, pre_mix_g, w_in, dn_conv_w, dn_a_log, dn_dt_bias, dn_norm_g, sw_sinks, w_up_dn, w_up_sw, w_o, post_mix_g, pre_mlp_g, w_ff1, w_ff2, post_mlp_g, loss_target, m_pre_mix_g, m_w_in, m_dn_conv_w, m_dn_a_log, m_dn_dt_bias, m_dn_norm_g, m_sw_sinks, m_w_up_dn, m_w_up_sw, m_w_o, m_post_mix_g, m_pre_mlp_g, m_w_ff1, m_w_ff2, m_post_mlp_g, v_pre_mix_g, v_w_in, v_dn_conv_w, v_dn_a_log, v_dn_dt_bias, v_dn_norm_g, v_sw_sinks, v_w_up_dn, v_w_up_sw, v_w_o, v_post_mix_g, v_pre_mlp_g, v_w_ff1, v_w_ff2, v_post_mlp_g)
    names = ["x", "positions"] + _WEIGHTS + ["loss_target"] + ["m_" + n for n in _WEIGHTS] + ["v_" + n for n in _WEIGHTS]
    return _step(dict(zip(names, vals)))
```

```python
import functools

import numpy as np
import jax
import jax.numpy as jnp
from jax import lax
from jax.experimental import pallas as pl
from jax.experimental.pallas import tpu as pltpu

F32, BF16 = jnp.float32, jnp.bfloat16
MESH = pl.DeviceIdType.MESH

DN_HEADS = 8
DN_DK = 128
DN_CONV = 4
DN_CHUNK = 64
SW_Q_HEADS = 16
SW_KV_HEADS = 2
SW_HD = 64
SW_BLOCK = 128
ROPE_THETA = 500000.0
ROT_DIM = SW_HD // 4
EPS = 1e-6
ADAM_LR, ADAM_B1, ADAM_B2, ADAM_EPS, ADAM_WD, ADAM_STEP = 0.001, 0.9, 0.999, 1e-08, 0.01, 10

LANES = 128
SUBLANES = 8
VMEM_LIMIT = 48 * 1024 * 1024
KV_W = SW_KV_HEADS * SW_HD
BA_W = 256
PACK_ROWS = 512
DELTA_CB = 4
DELTA_HB = 8


def _pcall(body, **kw):
    return pl.pallas_call(body, **kw)


def _cp(*sem):
    return pltpu.CompilerParams(dimension_semantics=sem, vmem_limit_bytes=VMEM_LIMIT)


def _tile(n, pref, unit=LANES):
    if n <= pref:
        return n
    t = (pref // unit) * unit
    while t > unit and n % t:
        t -= unit
    assert n % t == 0, (n, pref)
    return t


def _sds(shape, dtype):
    return jax.ShapeDtypeStruct(tuple(shape), dtype)


_DIMS = {"nn": ((1,), (0,)), "nt": ((1,), (1,)), "tn": ((0,), (0,))}


def _mm(name, a, b, mode, out_dtype=F32, tm=1024, tn=1024, tk=1024, extras=(), epi=None, out_dtypes=None, slab=None):
    if mode == "nn":
        (M, K), (_, N) = a.shape, b.shape
    elif mode == "nt":
        (M, K), (N, _) = a.shape, b.shape
    else:
        (K, M), (_, N) = a.shape, b.shape
    tm, tn, tk = _tile(M, tm), _tile(N, tn), _tile(K, tk)
    nk = K // tk
    a_spec = {"nn": pl.BlockSpec((tm, tk), lambda i, j, k: (i, k)),
              "nt": pl.BlockSpec((tm, tk), lambda i, j, k: (i, k)),
              "tn": pl.BlockSpec((tk, tm), lambda i, j, k: (k, i))}[mode]
    b_spec = {"nn": pl.BlockSpec((tk, tn), lambda i, j, k: (k, j)),
              "nt": pl.BlockSpec((tn, tk), lambda i, j, k: (j, k)),
              "tn": pl.BlockSpec((tk, tn), lambda i, j, k: (k, j))}[mode]
    dims = (_DIMS[mode], ((), ()))
    out_dtypes = tuple(out_dtypes or (out_dtype,))
    ne, no = len(extras), len(out_dtypes)
    o_spec = pl.BlockSpec((tm, tn), lambda i, j, k: (i, j))

    def body(*refs):
        a_ref, b_ref, ex = refs[0], refs[1], refs[2:2 + ne]
        outs = refs[-no:] if nk == 1 else refs[-1 - no:-1]
        part = lax.dot_general(a_ref[...], b_ref[...], dims, preferred_element_type=F32)

        def finish(acc):
            res = epi(acc, *[e[...] for e in ex]) if epi else (acc,)
            for o, r, dt in zip(outs, res, out_dtypes):
                if slab is None:
                    o[...] = r.astype(dt)
                else:
                    o[0] = r.astype(dt)

        if nk == 1:
            finish(part)
            return
        acc_ref, k = refs[-1], pl.program_id(2)

        @pl.when(k == 0)
        def _():
            acc_ref[...] = part

        @pl.when((k > 0) & (k < nk - 1))
        def _():
            acc_ref[...] += part

        @pl.when(k == nk - 1)
        def _():
            finish(acc_ref[...] + part)

    kw = dict(name=name, grid=(M // tm, N // tn, nk), scratch_shapes=[] if nk == 1 else [pltpu.VMEM((tm, tn), F32)],
              compiler_params=_cp("parallel", "parallel", "arbitrary"))
    if slab is not None:
        buf, l = slab
        return _pcall(body, in_specs=[a_spec, b_spec, ANY], out_specs=pl.BlockSpec((1, tm, tn), lambda i, j, k: (l, i, j)),
                      out_shape=_sds(buf.shape, buf.dtype), input_output_aliases={2: 0}, **kw)(a, b, buf)
    out = _pcall(body, in_specs=[a_spec, b_spec] + [o_spec] * ne, out_specs=tuple(o_spec for _ in out_dtypes),
                 out_shape=tuple(_sds((M, N), dt) for dt in out_dtypes), **kw)(a, b, *extras)
    return out if no > 1 else out[0]


def _rows(name, fn, n_rows, tq, ins, in_specs, out_shapes, out_specs):
    def body(*refs):
        fn(pl.program_id(0), *refs)

    return _pcall(body, name=name, grid=(n_rows // tq,), in_specs=in_specs, out_specs=out_specs,
                  out_shape=out_shapes, compiler_params=_cp("arbitrary"))(*ins)


def _rb(tq, w, cb=0):
    return pl.BlockSpec((tq, w), lambda i: (i, cb))


def _full(shape):
    return pl.BlockSpec(tuple(shape), lambda *_: (0,) * len(shape))


def _rms_fwd(x, g):
    r = lax.rsqrt(jnp.mean(x * x, axis=-1, keepdims=True) + EPS)
    return x * r * g


def _rms_bwd(x, g, dy):
    r = lax.rsqrt(jnp.mean(x * x, axis=-1, keepdims=True) + EPS)
    xh = x * r
    t = dy * g
    dx = r * (t - xh * jnp.mean(t * xh, axis=-1, keepdims=True))
    return dx, jnp.sum(dy * xh, axis=0, keepdims=True)


def _acc(i, ref, val):
    @pl.when(i == 0)
    def _():
        ref[...] = val

    @pl.when(i > 0)
    def _():
        ref[...] += val


def _sigmoid(x):
    return 0.5 * jnp.tanh(0.5 * x) + 0.5


def _pre_norm(x, g):
    S, D = x.shape
    tq = _tile(S, 512, SUBLANES)

    def fn(i, x_ref, g_ref, h_ref):
        h_ref[...] = _rms_fwd(x_ref[...], g_ref[...]).astype(BF16)

    return _rows("pre_norm", fn, S, tq, (x, g), [_rb(tq, D), _full((1, D))], _sds((S, D), BF16), _rb(tq, D))


def _post_mix(x, mix, g2, g3):
    S, D = x.shape
    tq = _tile(S, 512, SUBLANES)

    def fn(i, x_ref, m_ref, g2_ref, g3_ref, x1_ref, h2_ref):
        x1 = x_ref[...] + _rms_fwd(m_ref[...], g2_ref[...])
        x1_ref[...] = x1
        h2_ref[...] = _rms_fwd(x1, g3_ref[...]).astype(BF16)

    return _rows("post_mix", fn, S, tq, (x, mix, g2, g3), [_rb(tq, D), _rb(tq, D), _full((1, D)), _full((1, D))],
                 (_sds((S, D), F32), _sds((S, D), BF16)), (_rb(tq, D), _rb(tq, D)))


def _post_mlp(x1, ff, g4, g1_next):
    S, D = x1.shape
    tq = _tile(S, 512, SUBLANES)

    def fn(i, x_ref, f_ref, g_ref, gn_ref, o_ref, h_ref):
        x2 = x_ref[...] + _rms_fwd(f_ref[...], g_ref[...])
        o_ref[...] = x2
        h_ref[...] = _rms_fwd(x2, gn_ref[...]).astype(BF16)

    r, f = _rb(tq, D), _full((1, D))
    return _rows("post_mlp", fn, S, tq, (x1, ff, g4, g1_next), [r, r, f, f], (_sds((S, D), F32), _sds((S, D), BF16)), (r, r))


def _loss_head(y, target):
    S, D = y.shape
    tq = _tile(S, 512, SUBLANES)

    def fn(i, y_ref, t_ref, l_ref, d_ref):
        e = y_ref[...] - t_ref[...]
        d_ref[...] = e * (1.0 / D)
        part = jnp.sum(jnp.sum(e * e, axis=1, keepdims=True), axis=0, keepdims=True) * (0.5 / D)
        _acc(i, l_ref, jnp.broadcast_to(part, (1, LANES)))

    return _rows("loss_head", fn, S, tq, (y, target), [_rb(tq, D), _rb(tq, D)],
                 (_sds((1, LANES), F32), _sds((S, D), F32)), (_full((1, LANES)), _rb(tq, D)))


def _post_mlp_bwd(ff, g4, dx2):
    S, D = ff.shape
    tq = _tile(S, 512, SUBLANES)

    def fn(i, f_ref, g_ref, d_ref, o_ref, dg_ref):
        dx, dg = _rms_bwd(f_ref[...], g_ref[...], d_ref[...])
        o_ref[...] = dx.astype(BF16)
        _acc(i, dg_ref, dg)

    return _rows("post_mlp_bwd", fn, S, tq, (ff, g4, dx2), [_rb(tq, D), _full((1, D)), _rb(tq, D)],
                 (_sds((S, D), BF16), _sds((1, D), F32)), (_rb(tq, D), _full((1, D))))


def _mid_bwd(x1, g3, dh2, dx2, mix, g2):
    S, D = x1.shape
    tq = _tile(S, 256, SUBLANES)

    def fn(i, x_ref, g3_ref, dh_ref, dx2_ref, m_ref, g2_ref, dx1_ref, dm_ref, dg3_ref, dg2_ref):
        d, dg3 = _rms_bwd(x_ref[...], g3_ref[...], dh_ref[...])
        dx1 = dx2_ref[...] + d
        dx1_ref[...] = dx1
        dm, dg2 = _rms_bwd(m_ref[...], g2_ref[...], dx1)
        dm_ref[...] = dm.astype(BF16)
        _acc(i, dg3_ref, dg3)
        _acc(i, dg2_ref, dg2)

    r, f = _rb(tq, D), _full((1, D))
    return _rows("mid_bwd", fn, S, tq, (x1, g3, dh2, dx2, mix, g2), [r, f, r, r, r, f],
                 (_sds((S, D), F32), _sds((S, D), BF16), _sds((1, D), F32), _sds((1, D), F32)), (r, r, f, f))


def _pre_norm_bwd(x, g1, dh, dx1):
    S, D = x.shape
    tq = _tile(S, 512, SUBLANES)

    def fn(i, x_ref, g_ref, dh_ref, dx1_ref, dx_ref, dg_ref):
        d, dg = _rms_bwd(x_ref[...], g_ref[...], dh_ref[...])
        dx_ref[...] = dx1_ref[...] + d
        _acc(i, dg_ref, dg)

    r, f = _rb(tq, D), _full((1, D))
    return _rows("pre_norm_bwd", fn, S, tq, (x, g1, dh, dx1), [r, f, r, r], (_sds((S, D), F32), _sds((1, D), F32)), (r, f))


def _mix(proj, ya, yb, D, cb_a):
    S = ya.shape[0]
    tq = _tile(S, 256, SUBLANES)

    def fn(i, ga_ref, gb_ref, ya_ref, yb_ref, o_ref):
        ga, gb, ya, yb = (r[...].astype(F32) for r in (ga_ref, gb_ref, ya_ref, yb_ref))
        o_ref[...] = (_sigmoid(ga) * ya + _sigmoid(gb) * yb).astype(BF16)

    return _rows("mix", fn, S, tq, (proj, proj, ya, yb), [_rb(tq, D, cb_a), _rb(tq, D, cb_a + 1), _rb(tq, D), _rb(tq, D)],
                 _sds((S, D), BF16), _rb(tq, D))


def _mix_bwd(proj, ya, yb, dmixin, D, cb_a):
    S = ya.shape[0]
    tq = _tile(S, 256, SUBLANES)

    def fn(i, ga_ref, gb_ref, ya_ref, yb_ref, d_ref, dya_ref, dyb_ref, dga_ref, dgb_ref):
        ga, gb, ya, yb, d = (r[...].astype(F32) for r in (ga_ref, gb_ref, ya_ref, yb_ref, d_ref))
        sa, sb = _sigmoid(ga), _sigmoid(gb)
        dya_ref[...] = (d * sa).astype(BF16)
        dyb_ref[...] = (d * sb).astype(BF16)
        dga_ref[...] = (d * ya * sa * (1.0 - sa)).astype(BF16)
        dgb_ref[...] = (d * yb * sb * (1.0 - sb)).astype(BF16)

    r = _rb(tq, D)
    o = _sds((S, D), BF16)
    return _rows("mix_bwd", fn, S, tq, (proj, proj, ya, yb, dmixin), [_rb(tq, D, cb_a), _rb(tq, D, cb_a + 1), r, r, r],
                 (o, o, o, o), (r, r, r, r))


HALO = 16


def _shift_down(xe, k, tq):
    return pltpu.roll(xe, k, 0)[HALO:HALO + tq]


def _conv_pre(cur_ref, halo_ref, w_ref, i, tq):
    x = cur_ref[...].astype(F32)
    halo = jnp.where(i > 0, halo_ref[...].astype(F32), 0.0)
    xe = jnp.concatenate([halo, x], axis=0)
    xs = [x] + [_shift_down(xe, k, tq) for k in range(1, DN_CONV)]
    w = w_ref[...]
    c = sum(w[DN_CONV - 1 - k:DN_CONV - k, :] * xs[k] for k in range(DN_CONV))
    return c, xs


def _dn_prep(proj, conv_w, W):
    S = proj.shape[0]
    tq = _tile(S, 256, HALO)
    hb = tq // HALO

    def body(cur_ref, halo_ref, w_ref, o_ref):
        j, i = pl.program_id(0), pl.program_id(1)
        c, _ = _conv_pre(cur_ref, halo_ref, w_ref, i, tq)
        y = c * _sigmoid(c)
        scale = jnp.where(j == 0, DN_DK ** -0.5, 1.0)
        for h in range(W // DN_DK):
            sl = slice(h * DN_DK, (h + 1) * DN_DK)
            yh = y[:, sl]
            rs = lax.rsqrt(jnp.sum(yh * yh, axis=-1, keepdims=True) + EPS)
            o_ref[:, sl] = jnp.where(j == 2, yh, yh * rs * scale)

    return _pcall(body, name="dn_prep", grid=(3, S // tq),
                  in_specs=[pl.BlockSpec((tq, W), lambda j, i: (i, j)),
                            pl.BlockSpec((HALO, W), lambda j, i: (jnp.maximum(i * hb - 1, 0), j)),
                            pl.BlockSpec((DN_CONV, W), lambda j, i: (0, j))],
                  out_specs=pl.BlockSpec((tq, W), lambda j, i: (i, j)), out_shape=_sds((S, 3 * W), F32),
                  compiler_params=_cp("arbitrary", "arbitrary"))(proj, proj, conv_w)


def _dn_prep_bwd_a(proj, conv_w, dqkv, W):
    S = proj.shape[0]
    tq = _tile(S, 256, HALO)
    hb = tq // HALO

    def body(cur_ref, halo_ref, w_ref, d_ref, dc_ref, dw_ref):
        j, i = pl.program_id(0), pl.program_id(1)
        c, xs = _conv_pre(cur_ref, halo_ref, w_ref, i, tq)
        sg = _sigmoid(c)
        y = c * sg
        scale = jnp.where(j == 0, DN_DK ** -0.5, 1.0)
        dout = d_ref[0]
        dys = []
        for h in range(W // DN_DK):
            sl = slice(h * DN_DK, (h + 1) * DN_DK)
            yh, dh = y[:, sl], dout[:, sl]
            rs = lax.rsqrt(jnp.sum(yh * yh, axis=-1, keepdims=True) + EPS)
            yn = yh * rs
            dn = scale * rs * (dh - yn * jnp.sum(dh * yn, axis=-1, keepdims=True))
            dys.append(jnp.where(j == 2, dh, dn))
        dy = jnp.concatenate(dys, axis=1)
        dc = dy * (sg * (1.0 + c * (1.0 - sg)))
        dc_ref[...] = dc
        dw = jnp.concatenate([jnp.sum(dc * xs[DN_CONV - 1 - r], axis=0, keepdims=True) for r in range(DN_CONV)], axis=0)
        _acc(i, dw_ref, dw)

    return _pcall(body, name="dn_prep_bwd_a", grid=(3, S // tq),
                  in_specs=[pl.BlockSpec((tq, W), lambda j, i: (i, j)),
                            pl.BlockSpec((HALO, W), lambda j, i: (jnp.maximum(i * hb - 1, 0), j)),
                            pl.BlockSpec((DN_CONV, W), lambda j, i: (0, j)),
                            pl.BlockSpec((1, tq, W), lambda j, i: (j, i, 0))],
                  out_specs=(pl.BlockSpec((tq, W), lambda j, i: (i, j)), pl.BlockSpec((DN_CONV, W), lambda j, i: (0, j))),
                  out_shape=(_sds((S, 3 * W), F32), _sds((DN_CONV, 3 * W), F32)),
                  compiler_params=_cp("arbitrary", "arbitrary"))(proj, proj, conv_w, dqkv)


def _dn_prep_bwd_b(dc, conv_w, W):
    S = dc.shape[0]
    tq = _tile(S, 256, SUBLANES)
    hb = tq // SUBLANES
    nblk = S // tq

    def body(cur_ref, nxt_ref, w_ref, o_ref):
        i = pl.program_id(1)
        d = cur_ref[...]
        nxt = jnp.where(i < nblk - 1, nxt_ref[...], 0.0)
        de = jnp.concatenate([d, nxt], axis=0)
        w = w_ref[...]
        out = w[DN_CONV - 1:DN_CONV, :] * d
        for k in range(1, DN_CONV):
            out = out + w[DN_CONV - 1 - k:DN_CONV - k, :] * pltpu.roll(de, tq + SUBLANES - k, 0)[0:tq]
        o_ref[...] = out.astype(BF16)

    return _pcall(body, name="dn_prep_bwd_b", grid=(3, nblk),
                  in_specs=[pl.BlockSpec((tq, W), lambda j, i: (i, j)),
                            pl.BlockSpec((SUBLANES, W), lambda j, i: (jnp.minimum((i + 1) * hb, S // SUBLANES - 1), j)),
                            pl.BlockSpec((DN_CONV, W), lambda j, i: (0, j))],
                  out_specs=pl.BlockSpec((tq, W), lambda j, i: (i, j)), out_shape=_sds((S, 3 * W), BF16),
                  compiler_params=_cp("arbitrary", "arbitrary"))(dc, dc, conv_w)


def _gate_terms(ba, al, dt):
    u = ba + dt
    sp = jnp.maximum(u, 0.0) + jnp.log(1.0 + jnp.exp(-jnp.abs(u)))
    return _sigmoid(ba), -jnp.exp(al) * sp, u


def _dn_gates(ba, alog_row, dt_row, H):
    S = ba.shape[0]
    tq = _tile(S, 512, SUBLANES)
    W = H * DN_DK

    def fn(i, ba_ref, al_ref, dt_ref, be_ref, g_ref):
        bet, gg, _ = _gate_terms(ba_ref[...], al_ref[...], dt_ref[...])
        for h in range(H):
            sl = slice(h * DN_DK, (h + 1) * DN_DK)
            be_ref[:, sl] = jnp.broadcast_to(bet[:, h:h + 1], (tq, DN_DK))
            g_ref[:, sl] = jnp.broadcast_to(gg[:, H + h:H + h + 1], (tq, DN_DK))

    return _rows("dn_gates", fn, S, tq, (ba, alog_row, dt_row), [_rb(tq, LANES), _full((1, LANES)), _full((1, LANES))],
                 (_sds((S, W), F32), _sds((S, W), F32)), (_rb(tq, W), _rb(tq, W)))


def _dn_gates_bwd(ba, alog_row, dt_row, dbeta_b, dg_b, H):
    S = ba.shape[0]
    tq = _tile(S, 512, SUBLANES)
    W = H * DN_DK

    def fn(i, ba_ref, al_ref, dt_ref, db_ref, dg_ref, o_ref, dal_ref, ddt_ref):
        bet, gg, u = _gate_terms(ba_ref[...], al_ref[...], dt_ref[...])
        lane = lax.broadcasted_iota(jnp.int32, (tq, LANES), 1)
        d = jnp.zeros((tq, LANES), F32)
        for h in range(H):
            d = jnp.where(lane == h, db_ref[:, h * DN_DK:h * DN_DK + 1], d)
            d = jnp.where(lane == H + h, dg_ref[:, h * DN_DK:h * DN_DK + 1], d)
        is_a = (lane >= H) & (lane < 2 * H)
        da = jnp.where(is_a, d * (-jnp.exp(al_ref[...]) * _sigmoid(u)), 0.0)
        dlog = jnp.where(lane < H, d * bet * (1.0 - bet), da)
        o_ref[...] = jnp.concatenate([dlog, jnp.zeros((tq, BA_W - LANES), F32)], axis=1).astype(BF16)
        _acc(i, dal_ref, jnp.sum(jnp.where(is_a, d * gg, 0.0), axis=0, keepdims=True))
        _acc(i, ddt_ref, jnp.sum(da, axis=0, keepdims=True))

    f = _full((1, LANES))
    return _rows("dn_gates_bwd", fn, S, tq, (ba, alog_row, dt_row, dbeta_b, dg_b),
                 [_rb(tq, LANES), f, f, _rb(tq, W), _rb(tq, W)],
                 (_sds((S, BA_W), BF16), _sds((1, LANES), F32), _sds((1, LANES), F32)), (_rb(tq, BA_W), f, f))


def _dn_out(o, proj, ng, W, cb_z):
    S = o.shape[0]
    tq = _tile(S, 256, SUBLANES)

    def fn(i, o_ref, z_ref, g_ref, y_ref):
        for h in range(W // DN_DK):
            sl = slice(h * DN_DK, (h + 1) * DN_DK)
            z = z_ref[:, sl].astype(F32)
            y_ref[:, sl] = (_rms_fwd(o_ref[:, sl], g_ref[...]) * (z * _sigmoid(z))).astype(BF16)

    return _rows("dn_out", fn, S, tq, (o, proj, ng), [_rb(tq, W), _rb(tq, W, cb_z), _full((1, DN_DK))], _sds((S, W), BF16), _rb(tq, W))


def _dn_out_bwd(o, proj, ng, dy, W, cb_z):
    S = o.shape[0]
    tq = _tile(S, 256, SUBLANES)

    def fn(i, o_ref, z_ref, g_ref, d_ref, do_ref, dz_ref, dg_ref):
        g = g_ref[...]
        dg = jnp.zeros((1, DN_DK), F32)
        for h in range(W // DN_DK):
            sl = slice(h * DN_DK, (h + 1) * DN_DK)
            oh, z, d = o_ref[:, sl], z_ref[:, sl].astype(F32), d_ref[:, sl].astype(F32)
            sg = _sigmoid(z)
            dn = d * (z * sg)
            dz_ref[:, sl] = (d * _rms_fwd(oh, g) * (sg * (1.0 + z * (1.0 - sg)))).astype(BF16)
            dx, dgh = _rms_bwd(oh, g, dn)
            do_ref[:, sl] = dx
            dg = dg + dgh
        _acc(i, dg_ref, dg)

    r = _rb(tq, W)
    return _rows("dn_out_bwd", fn, S, tq, (o, proj, ng, dy), [r, _rb(tq, W, cb_z), _full((1, DN_DK)), r],
                 (_sds((S, W), F32), _sds((S, W), BF16), _sds((1, DN_DK), F32)), (r, r, _full((1, DN_DK))))


def _bdot(a, b, mode="nn"):
    return lax.dot_general(a.astype(BF16), b.astype(BF16), (_DIMS[mode], ((), ())), preferred_element_type=F32)


def _rsum(x):
    return jnp.broadcast_to(jnp.sum(x, axis=-1, keepdims=True), x.shape)


def _dot3(a, b, mode="nn"):
    ah, bh = a.astype(BF16), b.astype(BF16)
    al, bl = (a - ah.astype(F32)).astype(BF16), (b - bh.astype(F32)).astype(BF16)
    d = lambda x, y: lax.dot_general(x, y, (_DIMS[mode], ((), ())), preferred_element_type=F32)
    return d(ah, bh) + (d(al, bh) + d(ah, bl))


def _cumsum_rows(x, reverse=False):
    n = x.shape[0]
    row = lax.broadcasted_iota(jnp.int32, x.shape, 0)
    s = 1
    while s < n:
        if reverse:
            x = x + jnp.where(row < n - s, pltpu.roll(x, n - s, 0), 0.0)
        else:
            x = x + jnp.where(row >= s, pltpu.roll(x, s, 0), 0.0)
        s *= 2
    return x


def _each(f, *lists):
    return [f(*a) for a in zip(*lists)]


def _delta_local(qs, ks, vs, bes, grs):
    C = DN_CHUNK
    ri = lax.broadcasted_iota(jnp.int32, (C, C), 0)
    ci = lax.broadcasted_iota(jnp.int32, (C, C), 1)
    causal, strict = ri >= ci, ri > ci
    gcs = [_cumsum_rows(g) for g in grs]
    decays = [jnp.where(causal, jnp.exp(jnp.where(causal, gc[:, :C] - gc.T[:C, :], 0.0)), 0.0) for gc in gcs]
    egs = [jnp.exp(gc) for gc in gcs]
    eks = [jnp.exp(gc[C - 1:C, :] - gc) for gc in gcs]
    gams = [jnp.exp(gc[C - 1:C, :]) for gc in gcs]
    kbs = _each(lambda k, be: k * be, ks, bes)
    kks = _each(lambda kb, k: _bdot(kb, k, "nt"), kbs, ks)
    nls = _each(lambda kk, dc: jnp.where(strict, -kk * dc, 0.0), kks, decays)
    eye = (ri == ci).astype(F32)
    ts = [eye + nl for nl in nls]
    pws = [_dot3(nl, nl) for nl in nls]
    for s in range(4):
        both = _each(lambda t, pw: _dot3(jnp.concatenate([t, pw], axis=0), pw), ts, pws)
        ts = _each(lambda t, b: t + b[:C], ts, both)
        pws = [b[C:] for b in both]
    ts = _each(lambda t, pw: t + _dot3(t, pw), ts, pws)
    vbs = _each(lambda v, be: v * be, vs, bes)
    kbes = _each(lambda kb, eg: kb * eg, kbs, egs)
    uws = _each(lambda t, vb, kbe: _dot3(t, jnp.concatenate([vb, kbe], axis=1)), ts, vbs, kbes)
    us, ws = [uw[:, :DN_DK] for uw in uws], [uw[:, DN_DK:] for uw in uws]
    qks = _each(lambda q, k: _bdot(q, k, "nt"), qs, ks)
    return dict(decay=decays, eg=egs, ek=eks, gam=gams, kb=kbs, kk=kks, t=ts, vb=vbs, kbe=kbes, u=us, w=ws, qk=qks,
                a=_each(lambda qk, dc: qk * dc, qks, decays), qd=_each(lambda q, eg: q * eg, qs, egs),
                kd=_each(lambda k, ek: k * ek, ks, eks), strict=strict)


def _delta_items(refs, CB, HB):
    C, dk = DN_CHUNK, DN_DK
    return [[r[c * C:(c + 1) * C, h * dk:(h + 1) * dk] for h in range(HB) for c in range(CB)] for r in refs]


def _delta_fwd(qkv, beta_b, g_b, H, CB, HB):
    S = qkv.shape[0]
    C, dk = DN_CHUNK, DN_DK
    N = S // C
    R = CB * C
    G = H // HB

    def body(q_ref, k_ref, v_ref, b_ref, g_ref, o_ref, st_ref, s_ref):
        @pl.when(pl.program_id(1) == 0)
        def _():
            s_ref[...] = jnp.zeros((HB, dk, dk), F32)

        L = _delta_local(*_delta_items((q_ref, k_ref, v_ref, b_ref, g_ref), CB, HB))
        ss = [s_ref[h] for h in range(HB)]
        for c in range(CB):
            it = [h * CB + c for h in range(HB)]
            for h in range(HB):
                st_ref[h, c] = ss[h]
            wq = [_bdot(jnp.concatenate([L["w"][i], L["qd"][i]], axis=0), s) for i, s in zip(it, ss)]
            vns = [L["u"][i] - x[:C] for i, x in zip(it, wq)]
            outs = [x[C:] + _bdot(L["a"][i], vn) for i, x, vn in zip(it, wq, vns)]
            ss = [s * L["gam"][i] + _bdot(L["kd"][i], vn, "tn") for i, s, vn in zip(it, ss, vns)]
            for h in range(HB):
                o_ref[c * C:(c + 1) * C, h * dk:(h + 1) * dk] = outs[h]
        for h in range(HB):
            s_ref[h] = ss[h]

    blk = lambda off: pl.BlockSpec((R, HB * dk), lambda h, n: (n, off + h))
    return _pcall(body, name="delta_fwd", grid=(G, N // CB),
                  in_specs=[blk(0), blk(G), blk(2 * G), blk(0), blk(0)],
                  out_specs=(blk(0), pl.BlockSpec((HB, CB, dk, dk), lambda h, n: (h, n, 0, 0))),
                  out_shape=(_sds((S, H * dk), F32), _sds((H, N, dk, dk), F32)),
                  scratch_shapes=[pltpu.VMEM((HB, dk, dk), F32)],
                  compiler_params=_cp("arbitrary", "arbitrary"))(qkv, qkv, qkv, beta_b, g_b)


def _delta_bwd(qkv, beta_b, g_b, states, do, H, CB, HB):
    S = qkv.shape[0]
    C, dk = DN_CHUNK, DN_DK
    N = S // C
    R = CB * C
    NB = N // CB
    G = H // HB

    def body(q_ref, k_ref, v_ref, b_ref, g_ref, st_ref, do_ref, dqkv_ref, db_ref, dg_ref, ds_ref):
        @pl.when(pl.program_id(1) == 0)
        def _():
            ds_ref[...] = jnp.zeros((HB, dk, dk), F32)

        qs, ks, vs, bes, grs, dos = _delta_items((q_ref, k_ref, v_ref, b_ref, g_ref, do_ref), CB, HB)
        L = _delta_local(qs, ks, vs, bes, grs)
        ts, decays, kbs, egs, eks, gams, qds, kds = (L[n] for n in ("t", "decay", "kb", "eg", "ek", "gam", "qd", "kd"))
        s0s = [st_ref[h, c] for h in range(HB) for c in range(CB)]
        vns = _each(lambda u, w, s0: u - _bdot(w, s0), L["u"], L["w"], s0s)
        pre_dvn = _each(lambda a, d: _bdot(a, d, "tn"), L["a"], dos)
        pre_ds = _each(lambda qd, d: _bdot(qd, d, "tn"), qds, dos)
        das = _each(lambda d, vn: _bdot(d, vn, "nt"), dos, vns)
        ds = [ds_ref[h] for h in range(HB)]
        ds1s, dvns = [None] * (HB * CB), [None] * (HB * CB)
        for c in reversed(range(CB)):
            it = [h * CB + c for h in range(HB)]
            new = [pre_dvn[i] + _bdot(kds[i], d) for i, d in zip(it, ds)]
            for i, d, dv in zip(it, ds, new):
                ds1s[i], dvns[i] = d, dv
            ds = [pre_ds[i] + d * gams[i] - _bdot(L["w"][i], dv, "tn") for i, d, dv in zip(it, ds, new)]
        for h in range(HB):
            ds_ref[h] = ds[h]
        dkds = _each(lambda vn, d1: _bdot(vn, d1, "nt"), vns, ds1s)
        dgams = _each(lambda s0, d1: jnp.sum(jnp.sum(s0 * d1, axis=1, keepdims=True), axis=0, keepdims=True), s0s, ds1s)
        ost = _each(lambda d, dv, s0: _bdot(jnp.concatenate([d, dv], axis=0), s0, "nt"), dos, dvns, s0s)
        dqds, dws = [x[:C] for x in ost], [-x[C:] for x in ost]
        dvw = _each(lambda dv, dw: jnp.concatenate([dv, dw], axis=1), dvns, dws)
        tdvw = _each(lambda t, x: _dot3(t, x, "tn"), ts, dvw)
        dvbs, dkbes = [x[:, :dk] for x in tdvw], [x[:, dk:] for x in tdvw]
        dts = _each(lambda x, vb, kbe: _dot3(x, jnp.concatenate([vb, kbe], axis=1), "nt"), dvw, L["vb"], L["kbe"])
        tmp = _each(lambda dt, t: _dot3(dt, t, "nt"), dts, ts)
        dls = _each(lambda t, x: -_dot3(t, x, "tn"), ts, tmp)
        ms = _each(lambda dl, dc: jnp.where(L["strict"], dl * dc, 0.0), dls, decays)
        mas = _each(lambda da, dc: da * dc, das, decays)
        dkbs = _each(lambda m, k, dkbe, eg: _bdot(m, k) + dkbe * eg, ms, ks, dkbes, egs)
        dks = _each(lambda m, kb, ma, q, dkd, ek, dkb, be: _bdot(m, kb, "tn") + _bdot(ma, q, "tn") + dkd * ek + dkb * be,
                    ms, kbs, mas, qs, dkds, eks, dkbs, bes)
        dqs = _each(lambda ma, k, dqd, eg: _bdot(ma, k) + dqd * eg, mas, ks, dqds, egs)
        es = _each(lambda m, kk, ma, qk: m * kk + ma * qk, ms, L["kk"], mas, L["qk"])
        ones = jnp.ones((C, dk), BF16)
        row = lax.broadcasted_iota(jnp.int32, (C, dk), 0)
        for i in range(HB * CB):
            h, c = divmod(i, CB)
            rs, cs = slice(c * C, (c + 1) * C), slice(h * dk, (h + 1) * dk)
            e = es[i]
            e_hi = e.astype(BF16)
            col = _bdot(e_hi, ones, "tn") + _bdot(e - e_hi.astype(F32), ones, "tn")
            t_kd = _rsum(dkds[i] * kds[i])
            dgc = (jnp.broadcast_to(jnp.sum(e, axis=1, keepdims=True), (C, dk)) - col + _rsum(dqds[i] * qds[i]) - t_kd
                   + _rsum(dkbes[i] * L["kbe"][i]))
            dglast = jnp.sum(t_kd[:, 0:1], axis=0, keepdims=True) + dgams[i] * gams[i][:, 0:1]
            dgc = dgc + jnp.where(row == C - 1, dglast, 0.0)
            dqkv_ref[0, rs, cs] = dqs[i]
            dqkv_ref[1, rs, cs] = dks[i]
            dqkv_ref[2, rs, cs] = dvbs[i] * bes[i]
            db_ref[rs, cs] = _rsum(dkbs[i] * ks[i]) + _rsum(dvbs[i] * vs[i])
            dg_ref[rs, cs] = _cumsum_rows(dgc, reverse=True)

    blk = lambda off: pl.BlockSpec((R, HB * dk), lambda h, n: (NB - 1 - n, off + h))
    W = H * dk
    return _pcall(body, name="delta_bwd", grid=(G, NB),
                  in_specs=[blk(0), blk(G), blk(2 * G), blk(0), blk(0),
                            pl.BlockSpec((HB, CB, dk, dk), lambda h, n: (h, NB - 1 - n, 0, 0)), blk(0)],
                  out_specs=(pl.BlockSpec((3, R, HB * dk), lambda h, n: (0, NB - 1 - n, h)), blk(0), blk(0)),
                  out_shape=(_sds((3, S, W), F32), _sds((S, W), F32), _sds((S, W), F32)),
                  scratch_shapes=[pltpu.VMEM((HB, dk, dk), F32)],
                  compiler_params=_cp("arbitrary", "arbitrary"))(qkv, qkv, qkv, beta_b, g_b, states, do)


def _rope_consts():
    lane = np.arange(LANES) % SW_HD
    half = ROT_DIM // 2
    inv = (ROPE_THETA ** (-np.arange(half, dtype=np.float32) * np.float32(2.0 / ROT_DIM))).astype(np.float32)
    freq = np.where(lane < ROT_DIM, inv[lane % half], 0.0).astype(np.float32)
    lo = (lane < half).astype(np.float32)
    hi = ((lane >= half) & (lane < ROT_DIM)).astype(np.float32)
    return jnp.asarray(np.stack([freq, -lo, hi] + [np.zeros(LANES, np.float32)] * 5))


def _rope_tables(pos_col):
    S = pos_col.shape[0]
    tq = _tile(S, 1024, SUBLANES)

    def fn(i, p_ref, c_ref, cos_ref, s1_ref, s2_ref):
        ang = p_ref[...].astype(F32) * c_ref[0:1, :]
        sn = jnp.sin(ang)
        cos_ref[...] = jnp.cos(ang)
        s1_ref[...] = sn * c_ref[1:2, :]
        s2_ref[...] = sn * c_ref[2:3, :]

    o, r = _sds((S, LANES), F32), _rb(tq, LANES)
    return _rows("rope_tables", fn, S, tq, (pos_col, _rope_consts()), [_rb(tq, 1), _full((SUBLANES, LANES))], (o, o, o), (r, r, r))


def _wide(a, w):
    return a if w == LANES else jnp.tile(a, (1, w // LANES))


def _rope(x, cos, s1, s2):
    w, h = x.shape[1], ROT_DIM // 2
    return x * _wide(cos, w) + pltpu.roll(x, w - h, 1) * _wide(s1, w) + pltpu.roll(x, h, 1) * _wide(s2, w)


def _unrope(d, cos, s1, s2):
    w, h = d.shape[1], ROT_DIM // 2
    return d * _wide(cos, w) + pltpu.roll(d * _wide(s1, w), h, 1) + pltpu.roll(d * _wide(s2, w), w - h, 1)


def _swa_setup(n, q_ref, kc_ref, kp_ref, vc_ref, vp_ref, tc, tp):
    B = SW_BLOCK
    qr = _rope(q_ref[...].astype(F32), tc[0][...], tc[1][...], tc[2][...]) * (SW_HD ** -0.5)
    kw = jnp.concatenate([_rope(kp_ref[...].astype(F32), tp[0][...], tp[1][...], tp[2][...]),
                          _rope(kc_ref[...].astype(F32), tc[0][...], tc[1][...], tc[2][...])], axis=0)
    vw = jnp.concatenate([vp_ref[...], vc_ref[...]], axis=0).astype(F32)
    lane = lax.broadcasted_iota(jnp.int32, (2 * B, LANES), 1)
    heads = []
    for hk in range(SW_KV_HEADS):
        kh, vh = kw[:, hk * SW_HD:(hk + 1) * SW_HD], vw[:, hk * SW_HD:(hk + 1) * SW_HD]
        kk, vv = jnp.concatenate([kh, kh], axis=1), jnp.concatenate([vh, vh], axis=1)
        heads.append(tuple(jnp.where(sel, t, 0.0).astype(BF16) for t in (kk, vv) for sel in (lane < SW_HD, lane >= SW_HD)))
    prev = lax.broadcasted_iota(jnp.int32, (B, B), 1) > lax.broadcasted_iota(jnp.int32, (B, B), 0)
    return qr, heads, (prev, jnp.where(prev & (n == 0), -1e30, 0.0)), lane


def _fold(x, prev):
    return jnp.where(prev, x[:, :SW_BLOCK], x[:, SW_BLOCK:])


def _unfold(x, prev):
    return jnp.concatenate([jnp.where(prev, x, 0.0), jnp.where(prev, 0.0, x)], axis=1)


SWA_GROUPS = 4
SWA_GROUPS_BWD = 2


def _swa_probs(items, qs, heads, fold, sk_ref, G2):
    prev, bias = fold
    ss = [_fold(_bdot(qs[j], heads[j // G2][half], "nt"), prev) + bias for j, half in items]
    sks = [sk_ref[0:1, 2 * j + half:2 * j + half + 1] for j, half in items]
    ms = [jnp.maximum(jnp.max(s, axis=-1, keepdims=True), sk) for s, sk in zip(ss, sks)]
    ps = [jnp.exp(s - m) for s, m in zip(ss, ms)]
    es = [jnp.exp(sk - m) for sk, m in zip(sks, ms)]
    inv = [1.0 / (jnp.sum(p, axis=-1, keepdims=True) + e) for p, e in zip(ps, es)]
    return [p * i for p, i in zip(ps, inv)], [e * i for e, i in zip(es, inv)]


def _swa_specs(W, cb_q, cb_k):
    B = SW_BLOCK
    assert (W // LANES) % SWA_GROUPS == 0 and (W // LANES) % SWA_GROUPS_BWD == 0
    cur = lambda w, cb: pl.BlockSpec((B, w), lambda n: (n, cb))
    prv = lambda w, cb: pl.BlockSpec((B, w), lambda n: (jnp.maximum(n - 1, 0), cb))
    specs = [cur(W, cb_q), cur(LANES, cb_k), prv(LANES, cb_k), cur(LANES, cb_k + 1), prv(LANES, cb_k + 1)]
    return specs + [cur(LANES, 0)] * 3 + [prv(LANES, 0)] * 3 + [_full((1, LANES))]


def _swa_fwd(proj, tabs, sinks_row, W, cb_q, cb_k):
    S = proj.shape[0]
    G2 = SW_Q_HEADS // SW_KV_HEADS // 2

    def body(q_ref, kc_ref, kp_ref, vc_ref, vp_ref, c0, c1, c2, p0, p1, p2, sk_ref, o_ref):
        n = pl.program_id(0)
        qr, heads, fold, _ = _swa_setup(n, q_ref, kc_ref, kp_ref, vc_ref, vp_ref, (c0, c1, c2), (p0, p1, p2))
        qs = [qr[:, j * LANES:(j + 1) * LANES].astype(BF16) for j in range(W // LANES)]
        for j0 in range(0, W // LANES, SWA_GROUPS):
            items = [(j, half) for j in range(j0, j0 + SWA_GROUPS) for half in range(2)]
            probs, _ = _swa_probs(items, qs, heads, fold, sk_ref, G2)
            pv = [_bdot(_unfold(p, fold[0]), heads[j // G2][2 + half]) for p, (j, half) in zip(probs, items)]
            for g in range(SWA_GROUPS):
                o_ref[:, (j0 + g) * LANES:(j0 + g + 1) * LANES] = (pv[2 * g] + pv[2 * g + 1]).astype(BF16)

    t = tuple(tabs)
    return _pcall(body, name="swa_fwd", grid=(S // SW_BLOCK,), in_specs=_swa_specs(W, cb_q, cb_k),
                  out_specs=pl.BlockSpec((SW_BLOCK, W), lambda n: (n, 0)), out_shape=_sds((S, W), BF16),
                  compiler_params=_cp("arbitrary"))(proj, proj, proj, proj, proj, *t, *t, sinks_row)


def _swa_bwd(proj, tabs, sinks_row, do, W, cb_q, cb_k):
    S = proj.shape[0]
    B = SW_BLOCK
    G2 = SW_Q_HEADS // SW_KV_HEADS // 2
    SKR = -(-SW_Q_HEADS // SUBLANES) * SUBLANES

    def body(q_ref, kc_ref, kp_ref, vc_ref, vp_ref, c0, c1, c2, p0, p1, p2, sk_ref, do_ref,
             dq_ref, dkc_ref, dkp_ref, dvc_ref, dvp_ref, dsk_ref):
        n = pl.program_id(0)
        qr, heads, fold, lane = _swa_setup(n, q_ref, kc_ref, kp_ref, vc_ref, vp_ref, (c0, c1, c2), (p0, p1, p2))
        prev = fold[0]

        @pl.when(n == 0)
        def _():
            dsk_ref[...] = jnp.zeros((SKR, LANES), F32)

        acc_k = [jnp.zeros((2 * B, LANES), F32) for _ in range(SW_KV_HEADS)]
        acc_v = [jnp.zeros((2 * B, LANES), F32) for _ in range(SW_KV_HEADS)]
        qs = [qr[:, j * LANES:(j + 1) * LANES].astype(BF16) for j in range(W // LANES)]
        dos = [do_ref[:, j * LANES:(j + 1) * LANES].astype(BF16) for j in range(W // LANES)]
        dqs = []
        for j0 in range(0, W // LANES, SWA_GROUPS_BWD):
            items = [(j, half) for j in range(j0, j0 + SWA_GROUPS_BWD) for half in range(2)]
            probs, psinks = _swa_probs(items, qs, heads, fold, sk_ref, G2)
            dps = [_fold(_bdot(dos[j], heads[j // G2][2 + half], "nt"), prev) for j, half in items]
            deltas = [jnp.sum(p * dp, axis=-1, keepdims=True) for p, dp in zip(probs, dps)]
            dss = [_unfold(p * (dp - dl), prev).astype(BF16) for p, dp, dl in zip(probs, dps, deltas)]
            pbs = [_unfold(p, prev).astype(BF16) for p in probs]
            dqp = [_bdot(ds, heads[j // G2][half]) for ds, (j, half) in zip(dss, items)]
            dkk = [_bdot(ds, qs[j], "tn") for ds, (j, half) in zip(dss, items)]
            dvv = [_bdot(p, dos[j], "tn") for p, (j, half) in zip(pbs, items)]
            for i, (j, half) in enumerate(items):
                hk, h = j // G2, 2 * j + half
                sel = (lane < SW_HD) if half == 0 else (lane >= SW_HD)
                acc_k[hk] = acc_k[hk] + jnp.where(sel, dkk[i], 0.0)
                acc_v[hk] = acc_v[hk] + jnp.where(sel, dvv[i], 0.0)
                dsk_ref[h:h + 1, :] += jnp.broadcast_to(-jnp.sum(psinks[i] * deltas[i], axis=0, keepdims=True), (1, LANES))
            dqs += [dqp[2 * g] + dqp[2 * g + 1] for g in range(SWA_GROUPS_BWD)]
        dq = jnp.concatenate(dqs, axis=1) * (SW_HD ** -0.5)
        dq_ref[...] = _unrope(dq, c0[...], c1[...], c2[...]).astype(BF16)
        fold = lambda a: a[:, :SW_HD] + a[:, SW_HD:]
        dkw = jnp.concatenate([fold(a) for a in acc_k], axis=1)
        dvw = jnp.concatenate([fold(a) for a in acc_v], axis=1)
        dkp_ref[...], dkc_ref[...] = dkw[:B], dkw[B:]
        dvp_ref[...], dvc_ref[...] = dvw[:B], dvw[B:]

    t = tuple(tabs)
    blk = lambda w: pl.BlockSpec((B, w), lambda n: (n, 0))
    o = _sds((S, LANES), F32)
    return _pcall(body, name="swa_bwd", grid=(S // B,), in_specs=_swa_specs(W, cb_q, cb_k) + [blk(W)],
                  out_specs=(blk(W), blk(LANES), blk(LANES), blk(LANES), blk(LANES), _full((SKR, LANES))),
                  out_shape=(_sds((S, W), BF16), o, o, o, o, _sds((SKR, LANES), F32)),
                  compiler_params=_cp("arbitrary"))(proj, proj, proj, proj, proj, *t, *t, sinks_row, do)


def _swa_kv_combine(dkc, dkp, dvc, dvp, tabs):
    S = dkc.shape[0]
    B = SW_BLOCK
    nb = S // B

    def fn(n, kc_ref, kp_ref, vc_ref, vp_ref, c0, c1, c2, o_ref):
        more = n < nb - 1
        dk = kc_ref[...] + jnp.where(more, kp_ref[...], 0.0)
        dv = vc_ref[...] + jnp.where(more, vp_ref[...], 0.0)
        o_ref[...] = jnp.concatenate([_unrope(dk, c0[...], c1[...], c2[...]), dv], axis=1).astype(BF16)

    cur = _rb(B, LANES)
    nxt = pl.BlockSpec((B, LANES), lambda n: (jnp.minimum(n + 1, nb - 1), 0))
    return _rows("swa_kv_combine", fn, S, B, (dkc, dkp, dvc, dvp, *tabs), [cur, nxt, cur, nxt, cur, cur, cur],
                 _sds((S, 2 * LANES), BF16), _rb(B, 2 * LANES))


ANY = pl.BlockSpec(memory_space=pl.ANY)


def _place():
    x, y, c = lax.axis_index("x"), lax.axis_index("y"), lax.axis_index("c")
    return x, y, c, [(1 - x, y), (x, 1 - y), (1 - x, 1 - y)]


def _comm_call(name, body, out_shapes, n_sems, n_local, *ins):
    return _pcall(body, name=name, out_shape=tuple(out_shapes), in_specs=[ANY] * len(ins), out_specs=tuple(ANY for _ in out_shapes),
                  scratch_shapes=[pltpu.SemaphoreType.DMA((n_sems,)), pltpu.SemaphoreType.DMA((n_sems,)),
                                  pltpu.SemaphoreType.DMA((n_local,))])(*ins)


def _remote(src, dst, send, recv, k, to):
    return pltpu.make_async_remote_copy(src_ref=src, dst_ref=dst, send_sem=send.at[k], recv_sem=recv.at[k], device_id=to,
                                        device_id_type=MESH)


def _gather_chips(name, arrs):
    n = len(arrs)
    Lh = arrs[0].shape[0] // 2

    def body(*refs):
        w, o, (send, recv, _) = refs[:n], refs[n:2 * n], refs[2 * n:]
        x, y, c, chips = _place()
        me, sib = 2 * x + y, (x, y, 1 - c)
        own, other = pl.ds(c * Lh, Lh), pl.ds((1 - c) * Lh, Lh)
        idx = [2 * cx + cy for cx, cy in chips]
        first = [[_remote(w[a].at[own], o[a].at[me, own], send, recv, 6 * a + j, (*chips[j], c)) for j in range(3)] for a in range(n)]
        passed = [[_remote(o[a].at[idx[j], own], o[a].at[idx[j], own], send, recv, 6 * a + 3 + j, sib) for j in range(3)] for a in range(n)]
        for cp in [cp for row in first for cp in row]:
            cp.start()
        for j in range(3):
            for a in range(n):
                _remote(w[a].at[own], o[a].at[idx[j], own], send, recv, 6 * a + j, (*chips[j], c)).wait_recv()
                passed[a][j].start()
        for j in range(3):
            for a in range(n):
                _remote(w[a].at[other], o[a].at[idx[j], other], send, recv, 6 * a + 3 + j, sib).wait_recv()
        for cp in [cp for row in first + passed for cp in row]:
            cp.wait_send()

    return _comm_call(name, body, [_sds((4,) + a.shape, a.dtype) for a in arrs], 6 * n, 1, *arrs)


def _pair_swap(name, arrs, whole=False):
    n = len(arrs)
    Lh = arrs[0].shape[0] if whole else arrs[0].shape[0] // 2

    def body(*refs):
        g, o, (send, recv, _) = refs[:n], refs[n:2 * n], refs[2 * n:]
        x, y, c, _ = _place()
        cps = [_remote(g[a] if whole else g[a].at[pl.ds((1 - c) * Lh, Lh)], o[a], send, recv, a, (x, y, 1 - c)) for a in range(n)]
        for cp in cps:
            cp.start()
        for cp in cps:
            cp.wait()

    return _comm_call(name, body, [_sds((Lh,) + a.shape[1:], a.dtype) for a in arrs], n, 1, *arrs)


def _chip_slice(ref, axis, s):
    if axis is None:
        return ref.at[s]
    q = ref.shape[axis] // 4
    start = s * q if isinstance(s, int) else pl.multiple_of(s * q, q)
    return ref.at[tuple([slice(None)] * axis + [pl.ds(start, q)])]


def _scatter_chips(name, items):
    n = len(items)
    part = lambda a, ax: a.shape[1:] if ax is None else tuple(d // 4 if i == ax else d for i, d in enumerate(a.shape))

    def body(*refs):
        p, o, (send, recv, _) = refs[:n], refs[n:2 * n], refs[2 * n:]
        x, y, c, chips = _place()
        me = 2 * x + y
        idx = [2 * cx + cy for cx, cy in chips]
        cps = [_remote(_chip_slice(p[a], items[a][1], idx[j]), o[a].at[me], send, recv, 3 * a + j, (*chips[j], c))
               for a in range(n) for j in range(3)]
        for cp in cps:
            cp.start()
        for a in range(n):
            for j in range(3):
                _remote(_chip_slice(p[a], items[a][1], me), o[a].at[idx[j]], send, recv, 3 * a + j, (*chips[j], c)).wait_recv()
        for cp in cps:
            cp.wait_send()

    return _comm_call(name, body, [_sds((4,) + part(a, ax), a.dtype) for a, ax in items], 3 * n, 1, *[a for a, _ in items])


def _own_part(a, axis, me):
    if axis is None:
        return lax.dynamic_index_in_dim(a, me, 0, keepdims=False)
    q = a.shape[axis] // 4
    return lax.dynamic_slice_in_dim(a, me * q, q, axis)


HBM = pl.BlockSpec(memory_space=pltpu.HBM)
SEM = pl.BlockSpec(memory_space=pltpu.SEMAPHORE)
EFFECT = pltpu.SideEffectType.DATAFLOW_SIDE_EFFECTING


def _split_start(name, arrs, land_shapes, plan, nc, after=None):
    n = len(arrs)
    lands = [lax.empty(s.shape, s.dtype) for s in land_shapes]
    ins = list(arrs) + lands + ([] if after is None else [after])

    def body(*refs):
        outs = refs[len(ins):]
        for k, (src, dst, _, peer) in enumerate(plan(refs[:n], refs[n:n + len(lands)])):
            pltpu.make_async_remote_copy(src_ref=src, dst_ref=dst, send_sem=outs[k], recv_sem=outs[nc + k], device_id=peer,
                                         device_id_type=MESH).start()
        outs[-1][...] = jnp.zeros((SUBLANES, LANES), F32)

    nt = n + len(lands)
    thru = [pltpu.HBM(a.shape, a.dtype) for a in list(arrs) + lands]
    outs = _pcall(body, name=name, out_shape=tuple([pltpu.SemaphoreType.DMA(())] * (2 * nc) + thru + [_sds((SUBLANES, LANES), F32)]),
                  in_specs=[HBM] * nt + [ANY] * (len(ins) - nt),
                  out_specs=tuple([SEM] * (2 * nc) + [HBM] * nt + [pl.BlockSpec(memory_space=pltpu.VMEM)]),
                  input_output_aliases={i: 2 * nc + i for i in range(nt)},
                  compiler_params=pltpu.CompilerParams(has_side_effects=EFFECT))(
        *[pltpu.with_memory_space_constraint(a, pltpu.HBM) for a in ins[:nt]], *ins[nt:])
    return dict(sems=outs[:2 * nc], arrs=outs[2 * nc:2 * nc + n], lands=outs[2 * nc + n:2 * nc + nt], token=outs[-1], plan=plan, nc=nc)


def _split_wait(name, handle, after):
    arrs, lands, sems, nc = list(handle["arrs"]), list(handle["lands"]), list(handle["sems"]), handle["nc"]
    n, nt = len(arrs), len(arrs) + len(lands)

    def body(*refs):
        sem = refs[nt:nt + 2 * nc]
        for k, (src, _, landing, peer) in enumerate(handle["plan"](refs[:n], refs[n:nt])):
            cp = pltpu.make_async_remote_copy(src_ref=src, dst_ref=landing, send_sem=sem[k], recv_sem=sem[nc + k], device_id=peer,
                                              device_id_type=MESH)
            cp.wait_send()
            cp.wait_recv()

    thru = tuple(pltpu.HBM(a.shape, a.dtype) for a in arrs + lands)
    outs = _pcall(body, name=name, out_shape=thru, in_specs=[HBM] * nt + [SEM] * (2 * nc) + [ANY], out_specs=tuple([HBM] * nt),
                  input_output_aliases={i: i for i in range(nt)},
                  compiler_params=pltpu.CompilerParams(has_side_effects=EFFECT))(*arrs, *lands, *sems, after)
    return list(outs[:n]), list(outs[n:])


WHOLE = "whole"


def _plan_chips(axes):
    def plan(src, land):
        x, y, c, chips = _place()
        idx = [2 * cx + cy for cx, cy in chips]
        part = lambda a, s: src[a] if axes[a] is WHOLE else _chip_slice(src[a], axes[a], s)
        return [(part(a, idx[j]), land[a].at[2 * x + y], land[a].at[idx[j]], (*chips[j], c))
                for a in range(len(land)) for j in range(3)]
    return plan


def _plan_sibling(half):
    def plan(src, land):
        x, y, c, _ = _place()
        lh = lambda a: src[a].shape[0] // 2
        return [(src[a].at[pl.ds((1 - c) * lh(a), lh(a))] if half else src[a], land[a], land[a], (x, y, 1 - c))
                for a in range(len(land))]
    return plan


def _chips_start(name, arrs, axes, after=None):
    part = lambda a, ax: a.shape if ax is WHOLE else a.shape[1:] if ax is None else tuple(d // 4 if i == ax else d for i, d in enumerate(a.shape))
    return _split_start(name, arrs, [_sds((4,) + part(a, ax), a.dtype) for a, ax in zip(arrs, axes)], _plan_chips(axes), 3 * len(arrs), after)


def _sibling_start(name, arrs, half, after=None):
    shp = lambda a: (a.shape[0] // 2,) + a.shape[1:] if half else a.shape
    return _split_start(name, arrs, [_sds(shp(a), a.dtype) for a in arrs], _plan_sibling(half), len(arrs), after)


def _gather_all(name, b):
    R, C = b.shape
    flips = [(dx, dy, dc) for dx in (0, 1) for dy in (0, 1) for dc in (0, 1)][1:]

    def body(b_ref, o_ref, send, recv, lsem):
        x, y, c, _ = _place()
        me = 4 * x + 2 * y + c
        peers = [(x ^ dx, y ^ dy, c ^ dc) for dx, dy, dc in flips]
        mine = pltpu.make_async_copy(b_ref, o_ref.at[me], lsem.at[0])
        mine.start()
        cps = [_remote(b_ref, o_ref.at[me], send, recv, k, peer) for k, peer in enumerate(peers)]
        for cp in cps:
            cp.start()
        for k, (px, py, pc) in enumerate(peers):
            _remote(b_ref, o_ref.at[4 * px + 2 * py + pc], send, recv, k, (px, py, pc)).wait_recv()
        for cp in cps:
            cp.wait_send()
        mine.wait()

    return _comm_call(name, body, [_sds((8, R, C), b.dtype)], 7, 1, b)[0]


def _block_rows(rows, width):
    return _tile(rows, max(SUBLANES, (1 << 19) // width), SUBLANES)


def _add_half(name, g, got):
    L, A, B = g.shape
    Lh = L // 2
    tq = _block_rows(A, B)

    def body(c_ref, g_ref, r_ref, o_ref):
        o_ref[...] = (g_ref[...] + r_ref[...]).astype(BF16)

    spec = pltpu.PrefetchScalarGridSpec(
        num_scalar_prefetch=1, grid=(Lh, A // tq),
        in_specs=[pl.BlockSpec((1, tq, B), lambda l, i, c_ref: (c_ref[0] * Lh + l, i, 0)),
                  pl.BlockSpec((1, tq, B), lambda l, i, c_ref: (l, i, 0))],
        out_specs=pl.BlockSpec((1, tq, B), lambda l, i, c_ref: (l, i, 0)))
    return _pcall(body, name=name, grid_spec=spec, out_shape=_sds((Lh, A, B), BF16),
                  compiler_params=_cp("arbitrary", "arbitrary"))(lax.axis_index("c").reshape(1).astype(jnp.int32), g, got)


def _sum_slots(name, a):
    n, R, C = a.shape
    tq = _block_rows(R, n * C)

    def fn(i, a_ref, o_ref):
        t = a_ref[0].astype(F32)
        for s in range(1, n):
            t = t + a_ref[s].astype(F32)
        o_ref[...] = t

    return _rows(name, fn, R, tq, (a,), [pl.BlockSpec((n, tq, C), lambda i: (0, i, 0))], _sds((R, C), F32), _rb(tq, C))


def _adam_update(w, g, m, v):
    mn = ADAM_B1 * m + (1.0 - ADAM_B1) * g
    vn = ADAM_B2 * v + (1.0 - ADAM_B2) * (g * g)
    m_hat = mn / (1.0 - ADAM_B1 ** ADAM_STEP)
    v_hat = vn / (1.0 - ADAM_B2 ** ADAM_STEP)
    return -ADAM_LR * (m_hat / (jnp.sqrt(v_hat) + ADAM_EPS) + ADAM_WD * w), mn, vn


def _adamw(name, w, g, m, v):
    R, C = w.shape
    tq = _tile(R, 256, SUBLANES)

    def fn(i, w_ref, g_ref, m_ref, v_ref, d_ref, mo_ref, vo_ref):
        d_ref[...], mo_ref[...], vo_ref[...] = _adam_update(w_ref[...], g_ref[...], m_ref[...], v_ref[...])

    r, o = _rb(tq, C), _sds((R, C), F32)
    return _rows(name, fn, R, tq, (w, g, m, v), [r, r, r, r], (o, o, o), (r, r, r))


def _adamw_halves(name, w, mine, theirs, m, v, l0, prev=None):
    L, A, B = w.shape
    Lh = mine.shape[0]
    tq = _tile(A, 256, SUBLANES)

    def body(c_ref, w_ref, a_ref, b_ref, m_ref, v_ref, *refs):
        g_ref, d_ref, mo_ref, vo_ref = refs[-4:]
        is_mine = pl.program_id(0) // Lh == c_ref[0]
        g = jnp.where(is_mine, a_ref[...], b_ref[...])
        g_ref[...] = g
        d_ref[...], mo_ref[...], vo_ref[...] = _adam_update(w_ref[...], g, m_ref[...], v_ref[...])

    full = pl.BlockSpec((1, tq, B), lambda l, i, c_ref: (l0 + l, i, 0))
    half = pl.BlockSpec((1, tq, B), lambda l, i, c_ref: (l % Lh, i, 0))
    o = _sds((L, A, B), F32)
    prev = list(prev or ())
    spec = pltpu.PrefetchScalarGridSpec(num_scalar_prefetch=1, grid=(2 * Lh, A // tq), in_specs=[full, half, half, full, full] + [ANY] * len(prev),
                                        out_specs=(full, full, full, full))
    return _pcall(body, name=name, grid_spec=spec, out_shape=(o, o, o, o), input_output_aliases={6 + i: i for i in range(len(prev))},
                  compiler_params=_cp("arbitrary", "arbitrary"))(lax.axis_index("c").reshape(1).astype(jnp.int32), w, mine, theirs, m, v, *prev)


def _pack(arrs, width, lead=()):
    nl = len(lead)
    flat = jnp.concatenate([a.reshape(lead + (-1,)) for a in arrs], axis=nl)
    n = flat.shape[-1]
    unit = PACK_ROWS * width
    tot = -(-n // unit) * unit
    flat = jnp.pad(flat, [(0, 0)] * nl + [(0, tot - n)])
    return flat.reshape(lead + (tot // width, width))


def _unpack(buf, shapes, lead=()):
    flat = buf.reshape(lead + (-1,))
    out, off = [], 0
    for s in shapes:
        n = int(np.prod(s))
        out.append(flat[..., off:off + n].reshape(lead + tuple(s)))
        off += n
    return out


def _in_groups(W, H):
    o_sq = 4 * W + 2 * H
    o_k = o_sq + W
    o_g = o_k + 2 * KV_W
    return [(0, 4 * W), (o_sq, o_k), (o_g, o_g + 2 * W), (o_k, o_g), (4 * W, o_sq)]


def _relayout_in(shards, W, H):
    c4 = sum(hi - lo for lo, hi in _in_groups(W, H)) // 4
    parts = []
    for lo, hi in _in_groups(W, H):
        for s in range(4):
            a, b = max(lo, s * c4), min(hi, (s + 1) * c4)
            if a < b:
                parts.append(shards[s][:, a - s * c4:b - s * c4])
    parts.append(jnp.zeros((shards.shape[1], BA_W - 2 * H), shards.dtype))
    return jnp.concatenate(parts, axis=1)


def _shard_in(d, W, H):
    groups = _in_groups(W, H)
    starts = [sum(hi - lo for lo, hi in groups[:i]) for i in range(len(groups))]
    stored = sorted(zip(groups, starts))
    c4 = sum(hi - lo for lo, hi in groups) // 4
    out = []
    for s in range(4):
        parts = []
        for (lo, hi), at in stored:
            a, b = max(lo, s * c4), min(hi, (s + 1) * c4)
            if a < b:
                parts.append(d[:, :, at + a - lo:at + b - lo])
        out.append(jnp.concatenate(parts, axis=2))
    return jnp.stack(out)


def _lane_row(vals, at):
    return jnp.pad(vals, (at, LANES - at - vals.shape[0]))[None]


def _layer_fwd(x, lw, tabs, W, H, more=None, h=None, g1_next=None):
    D = x.shape[1]
    cbk = 7 * W // LANES
    if h is None:
        h = _pre_norm(x, lw["g1"])
    proj = _mm("mm_in", h, lw["win"], "nn", BF16, tn=768)
    ba = _mm("mm_ba", h, lw["win"][:, 7 * W + 2 * KV_W:], "nn", F32)
    qkv = _dn_prep(proj, lw["conv"], W)
    beta_b, g_b = _dn_gates(ba, lw["alog"], lw["dt"], H)
    o, st = _delta_fwd(qkv, beta_b, g_b, H, DELTA_CB, DELTA_HB)
    oa = _dn_out(o, proj, lw["ng"], W, 3)
    ob = _swa_fwd(proj, tabs, lw["sinks"], W, 4, cbk)
    if more is not None:
        lw.update(more(ob))
    ya = _mm("mm_up_dn", oa, lw["wup_dn"], "nn", BF16)
    yb = _mm("mm_up_sw", ob, lw["wup_sw"], "nn", BF16)
    mixin = _mix(proj, ya, yb, D, 5)
    mix = _mm("mm_o", mixin, lw["wo"], "nn", F32)
    x1, h2 = _post_mix(x, mix, lw["g2"], lw["g3"])
    f1, act = _mm("mm_ff1", h2, lw["wff1"], "nn", out_dtypes=(BF16, BF16), epi=lambda acc: (acc, jnp.square(jnp.maximum(acc, 0.0))))
    ff = _mm("mm_ff2", act, lw["wff2"], "nn", F32)
    x2, h_next = _post_mlp(x1, ff, lw["g4"], lw["g4"] if g1_next is None else g1_next)
    saved = dict(x=x, h=h, proj=proj, ba=ba, qkv=qkv, beta_b=beta_b, g_b=g_b, o=o, st=st, oa=oa, ob=ob, ya=ya, yb=yb,
                 mixin=mixin, mix=mix, x1=x1, h2=h2, f1=f1, act=act, ff=ff)
    return x2, h_next, saved


def _layer_bwd(dx2, lw, sv, tabs, W, H, l, big, weights_done=None):
    D = dx2.shape[1]
    cbk = 7 * W // LANES
    big = dict(big)
    dff, dg4 = _post_mlp_bwd(sv["ff"], lw["g4"], dx2)
    df1 = _mm("mm_ff2_dx", dff, lw["wff2"], "nt", BF16, extras=(sv["f1"],),
              epi=lambda acc, f1: (acc * 2.0 * jnp.maximum(f1.astype(F32), 0.0),))
    big["w_ff2"] = _mm("mm_ff2_dw", sv["act"], dff, "tn", slab=(big["w_ff2"], l))
    dh2 = _mm("mm_ff1_dx", df1, lw["wff1"], "nt", F32)
    big["w_ff1"] = _mm("mm_ff1_dw", sv["h2"], df1, "tn", slab=(big["w_ff1"], l))
    dx1, dmix, dg3, dg2 = _mid_bwd(sv["x1"], lw["g3"], dh2, dx2, sv["mix"], lw["g2"])
    dmixin = _mm("mm_o_dx", dmix, lw["wo"], "nt", BF16)
    big["w_o"] = _mm("mm_o_dw", sv["mixin"], dmix, "tn", slab=(big["w_o"], l))
    dya, dyb, dga, dgb = _mix_bwd(sv["proj"], sv["ya"], sv["yb"], dmixin, D, 5)
    doa = _mm("mm_up_dn_dx", dya, lw["wup_dn"], "nt", BF16)
    big["w_up_dn"] = _mm("mm_up_dn_dw", sv["oa"], dya, "tn", slab=(big["w_up_dn"], l))
    dob = _mm("mm_up_sw_dx", dyb, lw["wup_sw"], "nt", BF16)
    big["w_up_sw"] = _mm("mm_up_sw_dw", sv["ob"], dyb, "tn", slab=(big["w_up_sw"], l))
    do, dz, dng = _dn_out_bwd(sv["o"], sv["proj"], lw["ng"], doa, W, 3)
    dqkvn, dbeta_b, dg_b = _delta_bwd(sv["qkv"], sv["beta_b"], sv["g_b"], sv["st"], do, H, DELTA_CB, DELTA_HB)
    dba, dalog, ddt = _dn_gates_bwd(sv["ba"], lw["alog"], lw["dt"], dbeta_b, dg_b, H)
    dc, dconv = _dn_prep_bwd_a(sv["proj"], lw["conv"], dqkvn, W)
    dqkv = _dn_prep_bwd_b(dc, lw["conv"], W)
    dq_sw, dkc, dkp, dvc, dvp, dsk = _swa_bwd(sv["proj"], tabs, lw["sinks"], dob, W, 4, cbk)
    dkv = _swa_kv_combine(dkc, dkp, dvc, dvp, tabs)
    dproj = jnp.concatenate([dqkv, dz, dq_sw, dga, dgb, dkv, dba], axis=1)
    big["w_in"] = _mm("mm_in_dw", sv["h"], dproj, "tn", tn=768, slab=(big["w_in"], l))
    win = lw["win"]
    if weights_done is not None:
        win = win + weights_done(big).astype(BF16)
    dh = _mm("mm_in_dx", dproj, win, "nt", F32, tk=768)
    dx, dg1 = _pre_norm_bwd(sv["x"], lw["g1"], dh, dx1)
    grads = dict(pre_mix_g=dg1[0], dn_conv_w=dconv, dn_a_log=dalog[0, H:2 * H], dn_dt_bias=ddt[0, H:2 * H], dn_norm_g=dng[0],
                 sw_sinks=dsk[:SW_Q_HEADS, 0], post_mix_g=dg2[0], pre_mlp_g=dg3[0], post_mlp_g=dg4[0])
    return dx, grads, big


_WEIGHTS = ["pre_mix_g", "w_in", "dn_conv_w", "dn_a_log", "dn_dt_bias", "dn_norm_g", "sw_sinks", "w_up_dn", "w_up_sw", "w_o",
            "post_mix_g", "pre_mlp_g", "w_ff1", "w_ff2", "post_mlp_g"]
_BIG = {"w_in": 2, "w_up_dn": 1, "w_up_sw": 1, "w_o": 1, "w_ff1": 2, "w_ff2": 1}
_SMALL = [n for n in _WEIGHTS if n not in _BIG]


def _step(P):
    x, target = P["x"][0], P["loss_target"][0]
    S, D = x.shape
    L = P["pre_mix_g"].shape[0]
    H, W = DN_HEADS, DN_HEADS * DN_DK
    assert W == D == SW_Q_HEADS * SW_HD and KV_W == LANES
    me = 2 * lax.axis_index("x") + lax.axis_index("y")

    assert L % 4 == 0
    names = list(_BIG) + ["dn_conv_w"]
    local = [P[n].astype(BF16) for n in _BIG] + [P["dn_conv_w"]]
    early_names = ("w_in", "dn_conv_w")
    tail_names = [n for n in names if n not in early_names]
    own_slot = lambda gathered, mine: [lax.dynamic_update_slice_in_dim(g, w[None], me, 0) for g, w in zip(gathered, mine)]
    gather = lambda name, arrs, after=None: _chips_start(name, arrs, [WHOLE] * len(arrs), after)
    arrived = lambda name, h, after, keys: dict(zip(keys, own_slot(*reversed(_split_wait(name, h, after)))))
    h_first = gather("weights_first_start", [a[:1] for n, a in zip(names, local) if n in early_names])
    early = arrived("weights_first_wait", h_first, x, early_names)
    h_tail = gather("weights_tail_start", [a[:1] for n, a in zip(names, local) if n in tail_names], early["w_in"])
    h_next = gather("weights_next_start", [a[1:2] for a in local], h_tail["token"])
    h_rest = gather("weights_rest_start", [a[2:] for a in local], h_next["token"])

    def head(full, l, k):
        return dict(
            g1=P["pre_mix_g"][l][None], win=_relayout_in(full["w_in"][:, k], W, H),
            conv=jnp.concatenate([full["dn_conv_w"][s, k] for s in range(4)], axis=-1),
            alog=_lane_row(P["dn_a_log"][l], H), dt=_lane_row(P["dn_dt_bias"][l], H), ng=P["dn_norm_g"][l][None],
            sinks=_lane_row(P["sw_sinks"][l], 0), g2=P["post_mix_g"][l][None], g3=P["pre_mlp_g"][l][None], g4=P["post_mlp_g"][l][None])

    def tail(full, k):
        rows = lambda n: full[n][:, k].reshape(-1, full[n].shape[-1])
        return dict(wup_dn=rows("w_up_dn"), wup_sw=rows("w_up_sw"), wo=rows("w_o"), wff2=rows("w_ff2"),
                    wff1=jnp.concatenate([full["w_ff1"][s, k] for s in range(4)], axis=-1))

    tabs = _rope_tables(P["positions"].reshape(S, 1))
    lws = [head(early, 0, 0)]
    lws[0]["g1"] = lws[0]["g1"] + h_rest["token"][0, 0]

    saved, h = [], None
    for l in range(L):
        if l == 1:
            late = arrived("weights_next_wait", h_next, x, names)
            lws.append({**head(late, 1, 0), **tail(late, 0)})
        if l == 2:
            late = arrived("weights_rest_wait", h_rest, x, names)
            lws.extend({**head(late, k + 2, k), **tail(late, k)} for k in range(L - 2))
        first_tail = lambda after: tail(arrived("weights_tail_wait", h_tail, after, tail_names), 0)
        g1_next = P["pre_mix_g"][l + 1][None] if l + 1 < L else None
        x, h, sv = _layer_fwd(x, lws[l], tabs, W, H, first_tail if l == 0 else None, h, g1_next)
        saved.append(sv)
    loss_row, dx = _loss_head(x, target)

    Lb = L // 2
    layer_grads = [None] * L
    F = 4 * P["w_ff1"].shape[2]
    per_layer = dict(w_in=(D, 7 * W + 2 * KV_W + BA_W), w_up_dn=(W, D), w_up_sw=(W, D), w_o=(D, D), w_ff1=(D, F), w_ff2=(F, D))
    batch = [{n: lax.empty((Lb,) + per_layer[n], F32) for n in _BIG} for _ in range(2)]
    axes = [None if n == "w_in" else ax for n, ax in _BIG.items()]

    def pair_sums(tag, h_swap, after):
        g, got = _split_wait("grad_swap_wait_" + tag, h_swap, after)
        part = {n: _add_half("grad_pair_add_%s_%s" % (tag, n), a, r) for n, a, r in zip(_BIG, g, got)}
        return [_shard_in(part[n], W, H) if n == "w_in" else part[n] for n in _BIG]

    def chip_sums(tag, h_scat, after):
        parts, slots = _split_wait("grad_scatter_wait_" + tag, h_scat, after)
        halves = []
        for n, s, a, ax in zip(_BIG, slots, parts, axes):
            s = lax.dynamic_update_slice_in_dim(s, _own_part(a, ax, me)[None], me, 0)
            halves.append(_sum_slots("grad_chip_sum_%s_%s" % (tag, n), s.reshape(4, -1, s.shape[-1])).reshape(s.shape[1:]))
        h = _sibling_start("grad_share_start_" + tag, halves, False)
        return _split_wait("grad_share_wait_" + tag, h, halves[0])

    swaps = {}

    def swap_start(tag):
        def hook(big):
            swaps[tag] = _sibling_start("grad_swap_start_" + tag, [big[n] for n in _BIG], True)
            return swaps[tag]["token"][0, 0]
        return hook

    for l in reversed(range(L)):
        hook = swap_start("hi") if l == Lb else swap_start("lo") if l == 0 else None
        dx, layer_grads[l], batch[l // Lb] = _layer_bwd(dx, lws[l], saved[l], tabs, W, H, l % Lb, batch[l // Lb], hook)
        if l == Lb - 1:
            h_scat_hi = _chips_start("grad_scatter_start_hi", pair_sums("hi", swaps["hi"], dx), axes)
            if l > 0:
                lws[l - 1]["g4"] = lws[l - 1]["g4"] + h_scat_hi["token"][0, 0]

    grads = {n: jnp.stack([layer_grads[l][n] for l in range(L)]) for n in _SMALL}
    small_shapes = [(1,)] + [grads[n].shape for n in _SMALL]
    slots = _gather_all("small_gather", _pack([loss_row[0, :1]] + [grads[n] for n in _SMALL], LANES))
    h_scat_lo = _chips_start("grad_scatter_start_lo", pair_sums("lo", swaps["lo"], slots), axes)
    tot = _sum_slots("small_sum", slots + h_scat_lo["token"][0, 0])
    small = _unpack(tot, small_shapes)
    loss = small[0][0]
    gsum, delta, new_m, new_v = dict(zip(_SMALL, small[1:])), {}, {}, {}
    cw = P["dn_conv_w"].shape[2]
    gsum["dn_conv_w"] = lax.dynamic_slice_in_dim(gsum["dn_conv_w"], me * cw, cw, axis=2)
    sm_shapes = [P[n].shape for n in _SMALL]
    outs = _adamw("adamw_small", *(_pack([src[pre + n] for n in _SMALL], LANES)
                                   for src, pre in ((P, ""), (gsum, ""), (P, "m_"), (P, "v_"))))
    for d, o in zip((delta, new_m, new_v), outs):
        d.update(zip(_SMALL, _unpack(o, sm_shapes)))

    upper = {n: _adamw_halves("adamw_hi_" + n, P[n], mine, their, P["m_" + n], P["v_" + n], Lb)
             for n, mine, their in zip(_BIG, *chip_sums("hi", h_scat_hi, outs[0]))}
    for n, mine, their in zip(_BIG, *chip_sums("lo", h_scat_lo, upper["w_in"][0])):
        gsum[n], delta[n], new_m[n], new_v[n] = _adamw_halves("adamw_lo_" + n, P[n], mine, their, P["m_" + n], P["v_" + n], 0, upper[n])

    return (loss, dx[None], *[gsum[n] for n in _WEIGHTS], *[delta[n] for n in _WEIGHTS],
            *[new_m[n] for n in _WEIGHTS], *[new_v[n] for n in _WEIGHTS])


def kernel(x, positions, pre_mix_g, w_in, dn_conv_w, dn_a_log, dn_dt_bias, dn_norm_g, sw_sinks, w_up_dn, w_up_sw, w_o, post_mix_g, pre_mlp_g, w_ff1, w_ff2, post_mlp_g, loss_target, m_pre_mix_g, m_w_in, m_dn_conv_w, m_dn_a_log, m_dn_dt_bias, m_dn_norm_g, m_sw_sinks, m_w_up_dn, m_w_up_sw, m_w_o, m_post_mix_g, m_pre_mlp_g, m_w_ff1, m_w_ff2, m_post_mlp_g, v_pre_mix_g, v_w_in, v_dn_conv_w, v_dn_a_log, v_dn_dt_bias, v_dn_norm_g, v_sw_sinks, v_w_up_dn, v_w_up_sw, v_w_o, v_post_mix_g, v_pre_mlp_g, v_w_ff1, v_w_ff2, v_post_mlp_g):
    vals = (x, positions, pre_mix_g, w_in, dn_conv_w, dn_a_log, dn_dt_bias, dn_norm_g, sw_sinks, w_up_dn, w_up_sw, w_o, post_mix_g, pre_mlp_g, w_ff1, w_ff2, post_mlp_g, loss_target, m_pre_mix_g, m_w_in, m_dn_conv_w, m_dn_a_log, m_dn_dt_bias, m_dn_norm_g, m_sw_sinks, m_w_up_dn, m_w_up_sw, m_w_o, m_post_mix_g, m_pre_mlp_g, m_w_ff1, m_w_ff2, m_post_mlp_g, v_pre_mix_g, v_w_in, v_dn_conv_w, v_dn_a_log, v_dn_dt_bias, v_dn_norm_g, v_sw_sinks, v_w_up_dn, v_w_up_sw, v_w_o, v_post_mix_g, v_pre_mlp_g, v_w_ff1, v_w_ff2, v_post_mlp_g)
    names = ["x", "positions"] + _WEIGHTS + ["loss_target"] + ["m_" + n for n in _WEIGHTS] + ["v_" + n for n in _WEIGHTS]
    return _step(dict(zip(names, vals)))
```

```python
import functools

import numpy as np
import jax
import jax.numpy as jnp
from jax import lax
from jax.experimental import pallas as pl
from jax.experimental.pallas import tpu as pltpu

F32, BF16 = jnp.float32, jnp.bfloat16
MESH = pl.DeviceIdType.MESH

DN_HEADS = 8
DN_DK = 128
DN_CONV = 4
DN_CHUNK = 64
SW_Q_HEADS = 16
SW_KV_HEADS = 2
SW_HD = 64
SW_BLOCK = 128
ROPE_THETA = 500000.0
ROT_DIM = SW_HD // 4
EPS = 1e-6
ADAM_LR, ADAM_B1, ADAM_B2, ADAM_EPS, ADAM_WD, ADAM_STEP = 0.001, 0.9, 0.999, 1e-08, 0.01, 10

LANES = 128
SUBLANES = 8
VMEM_LIMIT = 48 * 1024 * 1024
KV_W = SW_KV_HEADS * SW_HD
BA_W = 256
PACK_ROWS = 512
DELTA_CB = 4
DELTA_HB = 8


def _pcall(body, **kw):
    return pl.pallas_call(body, **kw)


def _cp(*sem):
    return pltpu.CompilerParams(dimension_semantics=sem, vmem_limit_bytes=VMEM_LIMIT)


def _tile(n, pref, unit=LANES):
    if n <= pref:
        return n
    t = (pref // unit) * unit
    while t > unit and n % t:
        t -= unit
    assert n % t == 0, (n, pref)
    return t


def _sds(shape, dtype):
    return jax.ShapeDtypeStruct(tuple(shape), dtype)


_DIMS = {"nn": ((1,), (0,)), "nt": ((1,), (1,)), "tn": ((0,), (0,))}


def _mm(name, a, b, mode, out_dtype=F32, tm=2048, tn=1024, tk=1024, extras=(), epi=None, out_dtypes=None, slab=None):
    if mode == "nn":
        (M, K), (_, N) = a.shape, b.shape
    elif mode == "nt":
        (M, K), (N, _) = a.shape, b.shape
    else:
        (K, M), (_, N) = a.shape, b.shape
    tm, tn, tk = _tile(M, tm), _tile(N, tn), _tile(K, tk)
    nk = K // tk
    a_spec = {"nn": pl.BlockSpec((tm, tk), lambda i, j, k: (i, k)),
              "nt": pl.BlockSpec((tm, tk), lambda i, j, k: (i, k)),
              "tn": pl.BlockSpec((tk, tm), lambda i, j, k: (k, i))}[mode]
    b_spec = {"nn": pl.BlockSpec((tk, tn), lambda i, j, k: (k, j)),
              "nt": pl.BlockSpec((tn, tk), lambda i, j, k: (j, k)),
              "tn": pl.BlockSpec((tk, tn), lambda i, j, k: (k, j))}[mode]
    dims = (_DIMS[mode], ((), ()))
    out_dtypes = tuple(out_dtypes or (out_dtype,))
    ne, no = len(extras), len(out_dtypes)
    o_spec = pl.BlockSpec((tm, tn), lambda i, j, k: (i, j))

    def body(*refs):
        a_ref, b_ref, ex = refs[0], refs[1], refs[2:2 + ne]
        outs = refs[-no:] if nk == 1 else refs[-1 - no:-1]
        part = lax.dot_general(a_ref[...], b_ref[...], dims, preferred_element_type=F32)

        def finish(acc):
            res = epi(acc, *[e[...] for e in ex]) if epi else (acc,)
            for o, r, dt in zip(outs, res, out_dtypes):
                if slab is None:
                    o[...] = r.astype(dt)
                else:
                    o[0] = r.astype(dt)

        if nk == 1:
            finish(part)
            return
        acc_ref, k = refs[-1], pl.program_id(2)

        @pl.when(k == 0)
        def _():
            acc_ref[...] = part

        @pl.when((k > 0) & (k < nk - 1))
        def _():
            acc_ref[...] += part

        @pl.when(k == nk - 1)
        def _():
            finish(acc_ref[...] + part)

    kw = dict(name=name, grid=(M // tm, N // tn, nk), scratch_shapes=[] if nk == 1 else [pltpu.VMEM((tm, tn), F32)],
              compiler_params=_cp("parallel", "parallel", "arbitrary"))
    if slab is not None:
        buf, l = slab
        return _pcall(body, in_specs=[a_spec, b_spec, ANY], out_specs=pl.BlockSpec((1, tm, tn), lambda i, j, k: (l, i, j)),
                      out_shape=_sds(buf.shape, buf.dtype), input_output_aliases={2: 0}, **kw)(a, b, buf)
    out = _pcall(body, in_specs=[a_spec, b_spec] + [o_spec] * ne, out_specs=tuple(o_spec for _ in out_dtypes),
                 out_shape=tuple(_sds((M, N), dt) for dt in out_dtypes), **kw)(a, b, *extras)
    return out if no > 1 else out[0]


def _rows(name, fn, n_rows, tq, ins, in_specs, out_shapes, out_specs):
    def body(*refs):
        fn(pl.program_id(0), *refs)

    return _pcall(body, name=name, grid=(n_rows // tq,), in_specs=in_specs, out_specs=out_specs,
                  out_shape=out_shapes, compiler_params=_cp("arbitrary"))(*ins)


def _rb(tq, w, cb=0):
    return pl.BlockSpec((tq, w), lambda i: (i, cb))


def _full(shape):
    return pl.BlockSpec(tuple(shape), lambda *_: (0,) * len(shape))


def _rms_fwd(x, g):
    r = lax.rsqrt(jnp.mean(x * x, axis=-1, keepdims=True) + EPS)
    return x * r * g


def _rms_bwd(x, g, dy):
    r = lax.rsqrt(jnp.mean(x * x, axis=-1, keepdims=True) + EPS)
    xh = x * r
    t = dy * g
    dx = r * (t - xh * jnp.mean(t * xh, axis=-1, keepdims=True))
    return dx, jnp.sum(dy * xh, axis=0, keepdims=True)


def _acc(i, ref, val):
    @pl.when(i == 0)
    def _():
        ref[...] = val

    @pl.when(i > 0)
    def _():
        ref[...] += val


def _sigmoid(x):
    return 0.5 * jnp.tanh(0.5 * x) + 0.5


def _pre_norm(x, g):
    S, D = x.shape
    tq = _tile(S, 512, SUBLANES)

    def fn(i, x_ref, g_ref, h_ref):
        h_ref[...] = _rms_fwd(x_ref[...], g_ref[...]).astype(BF16)

    return _rows("pre_norm", fn, S, tq, (x, g), [_rb(tq, D), _full((1, D))], _sds((S, D), BF16), _rb(tq, D))


def _post_mix(x, mix, g2, g3):
    S, D = x.shape
    tq = _tile(S, 512, SUBLANES)

    def fn(i, x_ref, m_ref, g2_ref, g3_ref, x1_ref, h2_ref):
        x1 = x_ref[...] + _rms_fwd(m_ref[...], g2_ref[...])
        x1_ref[...] = x1
        h2_ref[...] = _rms_fwd(x1, g3_ref[...]).astype(BF16)

    return _rows("post_mix", fn, S, tq, (x, mix, g2, g3), [_rb(tq, D), _rb(tq, D), _full((1, D)), _full((1, D))],
                 (_sds((S, D), F32), _sds((S, D), BF16)), (_rb(tq, D), _rb(tq, D)))


def _post_mlp(x1, ff, g4, g1_next):
    S, D = x1.shape
    tq = _tile(S, 512, SUBLANES)

    def fn(i, x_ref, f_ref, g_ref, gn_ref, o_ref, h_ref):
        x2 = x_ref[...] + _rms_fwd(f_ref[...], g_ref[...])
        o_ref[...] = x2
        h_ref[...] = _rms_fwd(x2, gn_ref[...]).astype(BF16)

    r, f = _rb(tq, D), _full((1, D))
    return _rows("post_mlp", fn, S, tq, (x1, ff, g4, g1_next), [r, r, f, f], (_sds((S, D), F32), _sds((S, D), BF16)), (r, r))


def _loss_head(y, target):
    S, D = y.shape
    tq = _tile(S, 512, SUBLANES)

    def fn(i, y_ref, t_ref, l_ref, d_ref):
        e = y_ref[...] - t_ref[...]
        d_ref[...] = e * (1.0 / D)
        part = jnp.sum(jnp.sum(e * e, axis=1, keepdims=True), axis=0, keepdims=True) * (0.5 / D)
        _acc(i, l_ref, jnp.broadcast_to(part, (1, LANES)))

    return _rows("loss_head", fn, S, tq, (y, target), [_rb(tq, D), _rb(tq, D)],
                 (_sds((1, LANES), F32), _sds((S, D), F32)), (_full((1, LANES)), _rb(tq, D)))


def _post_mlp_bwd(ff, g4, dx2):
    S, D = ff.shape
    tq = _tile(S, 512, SUBLANES)

    def fn(i, f_ref, g_ref, d_ref, o_ref, dg_ref):
        dx, dg = _rms_bwd(f_ref[...], g_ref[...], d_ref[...])
        o_ref[...] = dx.astype(BF16)
        _acc(i, dg_ref, dg)

    return _rows("post_mlp_bwd", fn, S, tq, (ff, g4, dx2), [_rb(tq, D), _full((1, D)), _rb(tq, D)],
                 (_sds((S, D), BF16), _sds((1, D), F32)), (_rb(tq, D), _full((1, D))))


def _mid_bwd(x1, g3, dh2, dx2, mix, g2):
    S, D = x1.shape
    tq = _tile(S, 256, SUBLANES)

    def fn(i, x_ref, g3_ref, dh_ref, dx2_ref, m_ref, g2_ref, dx1_ref, dm_ref, dg3_ref, dg2_ref):
        d, dg3 = _rms_bwd(x_ref[...], g3_ref[...], dh_ref[...])
        dx1 = dx2_ref[...] + d
        dx1_ref[...] = dx1
        dm, dg2 = _rms_bwd(m_ref[...], g2_ref[...], dx1)
        dm_ref[...] = dm.astype(BF16)
        _acc(i, dg3_ref, dg3)
        _acc(i, dg2_ref, dg2)

    r, f = _rb(tq, D), _full((1, D))
    return _rows("mid_bwd", fn, S, tq, (x1, g3, dh2, dx2, mix, g2), [r, f, r, r, r, f],
                 (_sds((S, D), F32), _sds((S, D), BF16), _sds((1, D), F32), _sds((1, D), F32)), (r, r, f, f))


def _pre_norm_bwd(x, g1, dh, dx1):
    S, D = x.shape
    tq = _tile(S, 512, SUBLANES)

    def fn(i, x_ref, g_ref, dh_ref, dx1_ref, dx_ref, dg_ref):
        d, dg = _rms_bwd(x_ref[...], g_ref[...], dh_ref[...])
        dx_ref[...] = dx1_ref[...] + d
        _acc(i, dg_ref, dg)

    r, f = _rb(tq, D), _full((1, D))
    return _rows("pre_norm_bwd", fn, S, tq, (x, g1, dh, dx1), [r, f, r, r], (_sds((S, D), F32), _sds((1, D), F32)), (r, f))


def _mix(proj, ya, yb, D, cb_a):
    S = ya.shape[0]
    tq = _tile(S, 256, SUBLANES)

    def fn(i, ga_ref, gb_ref, ya_ref, yb_ref, o_ref):
        ga, gb, ya, yb = (r[...].astype(F32) for r in (ga_ref, gb_ref, ya_ref, yb_ref))
        o_ref[...] = (_sigmoid(ga) * ya + _sigmoid(gb) * yb).astype(BF16)

    return _rows("mix", fn, S, tq, (proj, proj, ya, yb), [_rb(tq, D, cb_a), _rb(tq, D, cb_a + 1), _rb(tq, D), _rb(tq, D)],
                 _sds((S, D), BF16), _rb(tq, D))


def _mix_bwd(proj, ya, yb, dmixin, D, cb_a):
    S = ya.shape[0]
    tq = _tile(S, 256, SUBLANES)

    def fn(i, ga_ref, gb_ref, ya_ref, yb_ref, d_ref, dya_ref, dyb_ref, dga_ref, dgb_ref):
        ga, gb, ya, yb, d = (r[...].astype(F32) for r in (ga_ref, gb_ref, ya_ref, yb_ref, d_ref))
        sa, sb = _sigmoid(ga), _sigmoid(gb)
        dya_ref[...] = (d * sa).astype(BF16)
        dyb_ref[...] = (d * sb).astype(BF16)
        dga_ref[...] = (d * ya * sa * (1.0 - sa)).astype(BF16)
        dgb_ref[...] = (d * yb * sb * (1.0 - sb)).astype(BF16)

    r = _rb(tq, D)
    o = _sds((S, D), BF16)
    return _rows("mix_bwd", fn, S, tq, (proj, proj, ya, yb, dmixin), [_rb(tq, D, cb_a), _rb(tq, D, cb_a + 1), r, r, r],
                 (o, o, o, o), (r, r, r, r))


HALO = 16


def _shift_down(xe, k, tq):
    return pltpu.roll(xe, k, 0)[HALO:HALO + tq]


def _conv_pre(cur_ref, halo_ref, w_ref, i, tq):
    x = cur_ref[...].astype(F32)
    halo = jnp.where(i > 0, halo_ref[...].astype(F32), 0.0)
    xe = jnp.concatenate([halo, x], axis=0)
    xs = [x] + [_shift_down(xe, k, tq) for k in range(1, DN_CONV)]
    w = w_ref[...]
    c = sum(w[DN_CONV - 1 - k:DN_CONV - k, :] * xs[k] for k in range(DN_CONV))
    return c, xs


def _dn_prep(proj, conv_w, W):
    S = proj.shape[0]
    tq = _tile(S, 256, HALO)
    hb = tq // HALO

    def body(cur_ref, halo_ref, w_ref, o_ref):
        j, i = pl.program_id(0), pl.program_id(1)
        c, _ = _conv_pre(cur_ref, halo_ref, w_ref, i, tq)
        y = c * _sigmoid(c)
        scale = jnp.where(j == 0, DN_DK ** -0.5, 1.0)
        for h in range(W // DN_DK):
            sl = slice(h * DN_DK, (h + 1) * DN_DK)
            yh = y[:, sl]
            rs = lax.rsqrt(jnp.sum(yh * yh, axis=-1, keepdims=True) + EPS)
            o_ref[:, sl] = jnp.where(j == 2, yh, yh * rs * scale)

    return _pcall(body, name="dn_prep", grid=(3, S // tq),
                  in_specs=[pl.BlockSpec((tq, W), lambda j, i: (i, j)),
                            pl.BlockSpec((HALO, W), lambda j, i: (jnp.maximum(i * hb - 1, 0), j)),
                            pl.BlockSpec((DN_CONV, W), lambda j, i: (0, j))],
                  out_specs=pl.BlockSpec((tq, W), lambda j, i: (i, j)), out_shape=_sds((S, 3 * W), F32),
                  compiler_params=_cp("arbitrary", "arbitrary"))(proj, proj, conv_w)


def _dn_prep_bwd_a(proj, conv_w, dqkv, W):
    S = proj.shape[0]
    tq = _tile(S, 256, HALO)
    hb = tq // HALO

    def body(cur_ref, halo_ref, w_ref, d_ref, dc_ref, dw_ref):
        j, i = pl.program_id(0), pl.program_id(1)
        c, xs = _conv_pre(cur_ref, halo_ref, w_ref, i, tq)
        sg = _sigmoid(c)
        y = c * sg
        scale = jnp.where(j == 0, DN_DK ** -0.5, 1.0)
        dout = d_ref[0]
        dys = []
        for h in range(W // DN_DK):
            sl = slice(h * DN_DK, (h + 1) * DN_DK)
            yh, dh = y[:, sl], dout[:, sl]
            rs = lax.rsqrt(jnp.sum(yh * yh, axis=-1, keepdims=True) + EPS)
            yn = yh * rs
            dn = scale * rs * (dh - yn * jnp.sum(dh * yn, axis=-1, keepdims=True))
            dys.append(jnp.where(j == 2, dh, dn))
        dy = jnp.concatenate(dys, axis=1)
        dc = dy * (sg * (1.0 + c * (1.0 - sg)))
        dc_ref[...] = dc
        dw = jnp.concatenate([jnp.sum(dc * xs[DN_CONV - 1 - r], axis=0, keepdims=True) for r in range(DN_CONV)], axis=0)
        _acc(i, dw_ref, dw)

    return _pcall(body, name="dn_prep_bwd_a", grid=(3, S // tq),
                  in_specs=[pl.BlockSpec((tq, W), lambda j, i: (i, j)),
                            pl.BlockSpec((HALO, W), lambda j, i: (jnp.maximum(i * hb - 1, 0), j)),
                            pl.BlockSpec((DN_CONV, W), lambda j, i: (0, j)),
                            pl.BlockSpec((1, tq, W), lambda j, i: (j, i, 0))],
                  out_specs=(pl.BlockSpec((tq, W), lambda j, i: (i, j)), pl.BlockSpec((DN_CONV, W), lambda j, i: (0, j))),
                  out_shape=(_sds((S, 3 * W), F32), _sds((DN_CONV, 3 * W), F32)),
                  compiler_params=_cp("arbitrary", "arbitrary"))(proj, proj, conv_w, dqkv)


def _dn_prep_bwd_b(dc, conv_w, W):
    S = dc.shape[0]
    tq = _tile(S, 256, SUBLANES)
    hb = tq // SUBLANES
    nblk = S // tq

    def body(cur_ref, nxt_ref, w_ref, o_ref):
        i = pl.program_id(1)
        d = cur_ref[...]
        nxt = jnp.where(i < nblk - 1, nxt_ref[...], 0.0)
        de = jnp.concatenate([d, nxt], axis=0)
        w = w_ref[...]
        out = w[DN_CONV - 1:DN_CONV, :] * d
        for k in range(1, DN_CONV):
            out = out + w[DN_CONV - 1 - k:DN_CONV - k, :] * pltpu.roll(de, tq + SUBLANES - k, 0)[0:tq]
        o_ref[...] = out.astype(BF16)

    return _pcall(body, name="dn_prep_bwd_b", grid=(3, nblk),
                  in_specs=[pl.BlockSpec((tq, W), lambda j, i: (i, j)),
                            pl.BlockSpec((SUBLANES, W), lambda j, i: (jnp.minimum((i + 1) * hb, S // SUBLANES - 1), j)),
                            pl.BlockSpec((DN_CONV, W), lambda j, i: (0, j))],
                  out_specs=pl.BlockSpec((tq, W), lambda j, i: (i, j)), out_shape=_sds((S, 3 * W), BF16),
                  compiler_params=_cp("arbitrary", "arbitrary"))(dc, dc, conv_w)


def _gate_terms(ba, al, dt):
    u = ba + dt
    sp = jnp.maximum(u, 0.0) + jnp.log(1.0 + jnp.exp(-jnp.abs(u)))
    return _sigmoid(ba), -jnp.exp(al) * sp, u


def _dn_gates(ba, alog_row, dt_row, H):
    S = ba.shape[0]
    tq = _tile(S, 512, SUBLANES)
    W = H * DN_DK

    def fn(i, ba_ref, al_ref, dt_ref, be_ref, g_ref):
        bet, gg, _ = _gate_terms(ba_ref[...], al_ref[...], dt_ref[...])
        for h in range(H):
            sl = slice(h * DN_DK, (h + 1) * DN_DK)
            be_ref[:, sl] = jnp.broadcast_to(bet[:, h:h + 1], (tq, DN_DK))
            g_ref[:, sl] = jnp.broadcast_to(gg[:, H + h:H + h + 1], (tq, DN_DK))

    return _rows("dn_gates", fn, S, tq, (ba, alog_row, dt_row), [_rb(tq, LANES), _full((1, LANES)), _full((1, LANES))],
                 (_sds((S, W), F32), _sds((S, W), F32)), (_rb(tq, W), _rb(tq, W)))


def _dn_gates_bwd(ba, alog_row, dt_row, dbeta_b, dg_b, H):
    S = ba.shape[0]
    tq = _tile(S, 512, SUBLANES)
    W = H * DN_DK

    def fn(i, ba_ref, al_ref, dt_ref, db_ref, dg_ref, o_ref, dal_ref, ddt_ref):
        bet, gg, u = _gate_terms(ba_ref[...], al_ref[...], dt_ref[...])
        lane = lax.broadcasted_iota(jnp.int32, (tq, LANES), 1)
        d = jnp.zeros((tq, LANES), F32)
        for h in range(H):
            d = jnp.where(lane == h, db_ref[:, h * DN_DK:h * DN_DK + 1], d)
            d = jnp.where(lane == H + h, dg_ref[:, h * DN_DK:h * DN_DK + 1], d)
        is_a = (lane >= H) & (lane < 2 * H)
        da = jnp.where(is_a, d * (-jnp.exp(al_ref[...]) * _sigmoid(u)), 0.0)
        dlog = jnp.where(lane < H, d * bet * (1.0 - bet), da)
        o_ref[...] = jnp.concatenate([dlog, jnp.zeros((tq, BA_W - LANES), F32)], axis=1).astype(BF16)
        _acc(i, dal_ref, jnp.sum(jnp.where(is_a, d * gg, 0.0), axis=0, keepdims=True))
        _acc(i, ddt_ref, jnp.sum(da, axis=0, keepdims=True))

    f = _full((1, LANES))
    return _rows("dn_gates_bwd", fn, S, tq, (ba, alog_row, dt_row, dbeta_b, dg_b),
                 [_rb(tq, LANES), f, f, _rb(tq, W), _rb(tq, W)],
                 (_sds((S, BA_W), BF16), _sds((1, LANES), F32), _sds((1, LANES), F32)), (_rb(tq, BA_W), f, f))


def _dn_out(o, proj, ng, W, cb_z):
    S = o.shape[0]
    tq = _tile(S, 256, SUBLANES)

    def fn(i, o_ref, z_ref, g_ref, y_ref):
        for h in range(W // DN_DK):
            sl = slice(h * DN_DK, (h + 1) * DN_DK)
            z = z_ref[:, sl].astype(F32)
            y_ref[:, sl] = (_rms_fwd(o_ref[:, sl], g_ref[...]) * (z * _sigmoid(z))).astype(BF16)

    return _rows("dn_out", fn, S, tq, (o, proj, ng), [_rb(tq, W), _rb(tq, W, cb_z), _full((1, DN_DK))], _sds((S, W), BF16), _rb(tq, W))


def _dn_out_bwd(o, proj, ng, dy, W, cb_z):
    S = o.shape[0]
    tq = _tile(S, 256, SUBLANES)

    def fn(i, o_ref, z_ref, g_ref, d_ref, do_ref, dz_ref, dg_ref):
        g = g_ref[...]
        dg = jnp.zeros((1, DN_DK), F32)
        for h in range(W // DN_DK):
            sl = slice(h * DN_DK, (h + 1) * DN_DK)
            oh, z, d = o_ref[:, sl], z_ref[:, sl].astype(F32), d_ref[:, sl].astype(F32)
            sg = _sigmoid(z)
            dn = d * (z * sg)
            dz_ref[:, sl] = (d * _rms_fwd(oh, g) * (sg * (1.0 + z * (1.0 - sg)))).astype(BF16)
            dx, dgh = _rms_bwd(oh, g, dn)
            do_ref[:, sl] = dx
            dg = dg + dgh
        _acc(i, dg_ref, dg)

    r = _rb(tq, W)
    return _rows("dn_out_bwd", fn, S, tq, (o, proj, ng, dy), [r, _rb(tq, W, cb_z), _full((1, DN_DK)), r],
                 (_sds((S, W), F32), _sds((S, W), BF16), _sds((1, DN_DK), F32)), (r, r, _full((1, DN_DK))))


def _bdot(a, b, mode="nn"):
    return lax.dot_general(a.astype(BF16), b.astype(BF16), (_DIMS[mode], ((), ())), preferred_element_type=F32)


def _rsum(x):
    return jnp.broadcast_to(jnp.sum(x, axis=-1, keepdims=True), x.shape)


def _dot3(a, b, mode="nn"):
    ah, bh = a.astype(BF16), b.astype(BF16)
    al, bl = (a - ah.astype(F32)).astype(BF16), (b - bh.astype(F32)).astype(BF16)
    d = lambda x, y: lax.dot_general(x, y, (_DIMS[mode], ((), ())), preferred_element_type=F32)
    return d(ah, bh) + (d(al, bh) + d(ah, bl))


def _cumsum_rows(x, reverse=False):
    n = x.shape[0]
    row = lax.broadcasted_iota(jnp.int32, x.shape, 0)
    s = 1
    while s < n:
        if reverse:
            x = x + jnp.where(row < n - s, pltpu.roll(x, n - s, 0), 0.0)
        else:
            x = x + jnp.where(row >= s, pltpu.roll(x, s, 0), 0.0)
        s *= 2
    return x


def _each(f, *lists):
    return [f(*a) for a in zip(*lists)]


def _delta_local(qs, ks, vs, bes, grs):
    C = DN_CHUNK
    ri = lax.broadcasted_iota(jnp.int32, (C, C), 0)
    ci = lax.broadcasted_iota(jnp.int32, (C, C), 1)
    causal, strict = ri >= ci, ri > ci
    gcs = [_cumsum_rows(g) for g in grs]
    decays = [jnp.where(causal, jnp.exp(jnp.where(causal, gc[:, :C] - gc.T[:C, :], 0.0)), 0.0) for gc in gcs]
    egs = [jnp.exp(gc) for gc in gcs]
    eks = [jnp.exp(gc[C - 1:C, :] - gc) for gc in gcs]
    gams = [jnp.exp(gc[C - 1:C, :]) for gc in gcs]
    kbs = _each(lambda k, be: k * be, ks, bes)
    kks = _each(lambda kb, k: _bdot(kb, k, "nt"), kbs, ks)
    nls = _each(lambda kk, dc: jnp.where(strict, -kk * dc, 0.0), kks, decays)
    eye = (ri == ci).astype(F32)
    ts = [eye + nl for nl in nls]
    pws = [_dot3(nl, nl) for nl in nls]
    for s in range(4):
        both = _each(lambda t, pw: _dot3(jnp.concatenate([t, pw], axis=0), pw), ts, pws)
        ts = _each(lambda t, b: t + b[:C], ts, both)
        pws = [b[C:] for b in both]
    ts = _each(lambda t, pw: t + _dot3(t, pw), ts, pws)
    vbs = _each(lambda v, be: v * be, vs, bes)
    kbes = _each(lambda kb, eg: kb * eg, kbs, egs)
    uws = _each(lambda t, vb, kbe: _dot3(t, jnp.concatenate([vb, kbe], axis=1)), ts, vbs, kbes)
    us, ws = [uw[:, :DN_DK] for uw in uws], [uw[:, DN_DK:] for uw in uws]
    qks = _each(lambda q, k: _bdot(q, k, "nt"), qs, ks)
    return dict(decay=decays, eg=egs, ek=eks, gam=gams, kb=kbs, kk=kks, t=ts, vb=vbs, kbe=kbes, u=us, w=ws, qk=qks,
                a=_each(lambda qk, dc: qk * dc, qks, decays), qd=_each(lambda q, eg: q * eg, qs, egs),
                kd=_each(lambda k, ek: k * ek, ks, eks), strict=strict)


def _delta_items(refs, CB, HB):
    C, dk = DN_CHUNK, DN_DK
    return [[r[c * C:(c + 1) * C, h * dk:(h + 1) * dk] for h in range(HB) for c in range(CB)] for r in refs]


def _delta_fwd(qkv, beta_b, g_b, H, CB, HB):
    S = qkv.shape[0]
    C, dk = DN_CHUNK, DN_DK
    N = S // C
    R = CB * C
    G = H // HB

    def body(q_ref, k_ref, v_ref, b_ref, g_ref, o_ref, st_ref, s_ref):
        @pl.when(pl.program_id(1) == 0)
        def _():
            s_ref[...] = jnp.zeros((HB, dk, dk), F32)

        L = _delta_local(*_delta_items((q_ref, k_ref, v_ref, b_ref, g_ref), CB, HB))
        ss = [s_ref[h] for h in range(HB)]
        for c in range(CB):
            it = [h * CB + c for h in range(HB)]
            for h in range(HB):
                st_ref[h, c] = ss[h]
            wq = [_bdot(jnp.concatenate([L["w"][i], L["qd"][i]], axis=0), s) for i, s in zip(it, ss)]
            vns = [L["u"][i] - x[:C] for i, x in zip(it, wq)]
            outs = [x[C:] + _bdot(L["a"][i], vn) for i, x, vn in zip(it, wq, vns)]
            ss = [s * L["gam"][i] + _bdot(L["kd"][i], vn, "tn") for i, s, vn in zip(it, ss, vns)]
            for h in range(HB):
                o_ref[c * C:(c + 1) * C, h * dk:(h + 1) * dk] = outs[h]
        for h in range(HB):
            s_ref[h] = ss[h]

    blk = lambda off: pl.BlockSpec((R, HB * dk), lambda h, n: (n, off + h))
    return _pcall(body, name="delta_fwd", grid=(G, N // CB),
                  in_specs=[blk(0), blk(G), blk(2 * G), blk(0), blk(0)],
                  out_specs=(blk(0), pl.BlockSpec((HB, CB, dk, dk), lambda h, n: (h, n, 0, 0))),
                  out_shape=(_sds((S, H * dk), F32), _sds((H, N, dk, dk), F32)),
                  scratch_shapes=[pltpu.VMEM((HB, dk, dk), F32)],
                  compiler_params=_cp("arbitrary", "arbitrary"))(qkv, qkv, qkv, beta_b, g_b)


def _delta_bwd(qkv, beta_b, g_b, states, do, H, CB, HB):
    S = qkv.shape[0]
    C, dk = DN_CHUNK, DN_DK
    N = S // C
    R = CB * C
    NB = N // CB
    G = H // HB

    def body(q_ref, k_ref, v_ref, b_ref, g_ref, st_ref, do_ref, dqkv_ref, db_ref, dg_ref, ds_ref):
        @pl.when(pl.program_id(1) == 0)
        def _():
            ds_ref[...] = jnp.zeros((HB, dk, dk), F32)

        qs, ks, vs, bes, grs, dos = _delta_items((q_ref, k_ref, v_ref, b_ref, g_ref, do_ref), CB, HB)
        L = _delta_local(qs, ks, vs, bes, grs)
        ts, decays, kbs, egs, eks, gams, qds, kds = (L[n] for n in ("t", "decay", "kb", "eg", "ek", "gam", "qd", "kd"))
        s0s = [st_ref[h, c] for h in range(HB) for c in range(CB)]
        vns = _each(lambda u, w, s0: u - _bdot(w, s0), L["u"], L["w"], s0s)
        pre_dvn = _each(lambda a, d: _bdot(a, d, "tn"), L["a"], dos)
        pre_ds = _each(lambda qd, d: _bdot(qd, d, "tn"), qds, dos)
        das = _each(lambda d, vn: _bdot(d, vn, "nt"), dos, vns)
        ds = [ds_ref[h] for h in range(HB)]
        ds1s, dvns = [None] * (HB * CB), [None] * (HB * CB)
        for c in reversed(range(CB)):
            it = [h * CB + c for h in range(HB)]
            new = [pre_dvn[i] + _bdot(kds[i], d) for i, d in zip(it, ds)]
            for i, d, dv in zip(it, ds, new):
                ds1s[i], dvns[i] = d, dv
            ds = [pre_ds[i] + d * gams[i] - _bdot(L["w"][i], dv, "tn") for i, d, dv in zip(it, ds, new)]
        for h in range(HB):
            ds_ref[h] = ds[h]
        dkds = _each(lambda vn, d1: _bdot(vn, d1, "nt"), vns, ds1s)
        dgams = _each(lambda s0, d1: jnp.sum(jnp.sum(s0 * d1, axis=1, keepdims=True), axis=0, keepdims=True), s0s, ds1s)
        ost = _each(lambda d, dv, s0: _bdot(jnp.concatenate([d, dv], axis=0), s0, "nt"), dos, dvns, s0s)
        dqds, dws = [x[:C] for x in ost], [-x[C:] for x in ost]
        dvw = _each(lambda dv, dw: jnp.concatenate([dv, dw], axis=1), dvns, dws)
        tdvw = _each(lambda t, x: _dot3(t, x, "tn"), ts, dvw)
        dvbs, dkbes = [x[:, :dk] for x in tdvw], [x[:, dk:] for x in tdvw]
        dts = _each(lambda x, vb, kbe: _dot3(x, jnp.concatenate([vb, kbe], axis=1), "nt"), dvw, L["vb"], L["kbe"])
        tmp = _each(lambda dt, t: _dot3(dt, t, "nt"), dts, ts)
        dls = _each(lambda t, x: -_dot3(t, x, "tn"), ts, tmp)
        ms = _each(lambda dl, dc: jnp.where(L["strict"], dl * dc, 0.0), dls, decays)
        mas = _each(lambda da, dc: da * dc, das, decays)
        dkbs = _each(lambda m, k, dkbe, eg: _bdot(m, k) + dkbe * eg, ms, ks, dkbes, egs)
        dks = _each(lambda m, kb, ma, q, dkd, ek, dkb, be: _bdot(m, kb, "tn") + _bdot(ma, q, "tn") + dkd * ek + dkb * be,
                    ms, kbs, mas, qs, dkds, eks, dkbs, bes)
        dqs = _each(lambda ma, k, dqd, eg: _bdot(ma, k) + dqd * eg, mas, ks, dqds, egs)
        es = _each(lambda m, kk, ma, qk: m * kk + ma * qk, ms, L["kk"], mas, L["qk"])
        ones = jnp.ones((C, dk), BF16)
        row = lax.broadcasted_iota(jnp.int32, (C, dk), 0)
        for i in range(HB * CB):
            h, c = divmod(i, CB)
            rs, cs = slice(c * C, (c + 1) * C), slice(h * dk, (h + 1) * dk)
            e = es[i]
            e_hi = e.astype(BF16)
            col = _bdot(e_hi, ones, "tn") + _bdot(e - e_hi.astype(F32), ones, "tn")
            t_kd = _rsum(dkds[i] * kds[i])
            dgc = (jnp.broadcast_to(jnp.sum(e, axis=1, keepdims=True), (C, dk)) - col + _rsum(dqds[i] * qds[i]) - t_kd
                   + _rsum(dkbes[i] * L["kbe"][i]))
            dglast = jnp.sum(t_kd[:, 0:1], axis=0, keepdims=True) + dgams[i] * gams[i][:, 0:1]
            dgc = dgc + jnp.where(row == C - 1, dglast, 0.0)
            dqkv_ref[0, rs, cs] = dqs[i]
            dqkv_ref[1, rs, cs] = dks[i]
            dqkv_ref[2, rs, cs] = dvbs[i] * bes[i]
            db_ref[rs, cs] = _rsum(dkbs[i] * ks[i]) + _rsum(dvbs[i] * vs[i])
            dg_ref[rs, cs] = _cumsum_rows(dgc, reverse=True)

    blk = lambda off: pl.BlockSpec((R, HB * dk), lambda h, n: (NB - 1 - n, off + h))
    W = H * dk
    return _pcall(body, name="delta_bwd", grid=(G, NB),
                  in_specs=[blk(0), blk(G), blk(2 * G), blk(0), blk(0),
                            pl.BlockSpec((HB, CB, dk, dk), lambda h, n: (h, NB - 1 - n, 0, 0)), blk(0)],
                  out_specs=(pl.BlockSpec((3, R, HB * dk), lambda h, n: (0, NB - 1 - n, h)), blk(0), blk(0)),
                  out_shape=(_sds((3, S, W), F32), _sds((S, W), F32), _sds((S, W), F32)),
                  scratch_shapes=[pltpu.VMEM((HB, dk, dk), F32)],
                  compiler_params=_cp("arbitrary", "arbitrary"))(qkv, qkv, qkv, beta_b, g_b, states, do)


def _rope_consts():
    lane = np.arange(LANES) % SW_HD
    half = ROT_DIM // 2
    inv = (ROPE_THETA ** (-np.arange(half, dtype=np.float32) * np.float32(2.0 / ROT_DIM))).astype(np.float32)
    freq = np.where(lane < ROT_DIM, inv[lane % half], 0.0).astype(np.float32)
    lo = (lane < half).astype(np.float32)
    hi = ((lane >= half) & (lane < ROT_DIM)).astype(np.float32)
    return jnp.asarray(np.stack([freq, -lo, hi] + [np.zeros(LANES, np.float32)] * 5))


def _rope_tables(pos_col):
    S = pos_col.shape[0]
    tq = _tile(S, 1024, SUBLANES)

    def fn(i, p_ref, c_ref, cos_ref, s1_ref, s2_ref):
        ang = p_ref[...].astype(F32) * c_ref[0:1, :]
        sn = jnp.sin(ang)
        cos_ref[...] = jnp.cos(ang)
        s1_ref[...] = sn * c_ref[1:2, :]
        s2_ref[...] = sn * c_ref[2:3, :]

    o, r = _sds((S, LANES), F32), _rb(tq, LANES)
    return _rows("rope_tables", fn, S, tq, (pos_col, _rope_consts()), [_rb(tq, 1), _full((SUBLANES, LANES))], (o, o, o), (r, r, r))


def _wide(a, w):
    return a if w == LANES else jnp.tile(a, (1, w // LANES))


def _rope(x, cos, s1, s2):
    w, h = x.shape[1], ROT_DIM // 2
    return x * _wide(cos, w) + pltpu.roll(x, w - h, 1) * _wide(s1, w) + pltpu.roll(x, h, 1) * _wide(s2, w)


def _unrope(d, cos, s1, s2):
    w, h = d.shape[1], ROT_DIM // 2
    return d * _wide(cos, w) + pltpu.roll(d * _wide(s1, w), h, 1) + pltpu.roll(d * _wide(s2, w), w - h, 1)


def _swa_setup(n, q_ref, kc_ref, kp_ref, vc_ref, vp_ref, tc, tp):
    B = SW_BLOCK
    qr = _rope(q_ref[...].astype(F32), tc[0][...], tc[1][...], tc[2][...]) * (SW_HD ** -0.5)
    kw = jnp.concatenate([_rope(kp_ref[...].astype(F32), tp[0][...], tp[1][...], tp[2][...]),
                          _rope(kc_ref[...].astype(F32), tc[0][...], tc[1][...], tc[2][...])], axis=0)
    vw = jnp.concatenate([vp_ref[...], vc_ref[...]], axis=0).astype(F32)
    lane = lax.broadcasted_iota(jnp.int32, (2 * B, LANES), 1)
    heads = []
    for hk in range(SW_KV_HEADS):
        kh, vh = kw[:, hk * SW_HD:(hk + 1) * SW_HD], vw[:, hk * SW_HD:(hk + 1) * SW_HD]
        kk, vv = jnp.concatenate([kh, kh], axis=1), jnp.concatenate([vh, vh], axis=1)
        heads.append(tuple(jnp.where(sel, t, 0.0).astype(BF16) for t in (kk, vv) for sel in (lane < SW_HD, lane >= SW_HD)))
    prev = lax.broadcasted_iota(jnp.int32, (B, B), 1) > lax.broadcasted_iota(jnp.int32, (B, B), 0)
    return qr, heads, (prev, jnp.where(prev & (n == 0), -1e30, 0.0)), lane


def _fold(x, prev):
    return jnp.where(prev, x[:, :SW_BLOCK], x[:, SW_BLOCK:])


def _unfold(x, prev):
    return jnp.concatenate([jnp.where(prev, x, 0.0), jnp.where(prev, 0.0, x)], axis=1)


SWA_GROUPS = 4
SWA_GROUPS_BWD = 2


def _swa_probs(items, qs, heads, fold, sk_ref, G2):
    prev, bias = fold
    ss = [_fold(_bdot(qs[j], heads[j // G2][half], "nt"), prev) + bias for j, half in items]
    sks = [sk_ref[0:1, 2 * j + half:2 * j + half + 1] for j, half in items]
    ms = [jnp.maximum(jnp.max(s, axis=-1, keepdims=True), sk) for s, sk in zip(ss, sks)]
    ps = [jnp.exp(s - m) for s, m in zip(ss, ms)]
    es = [jnp.exp(sk - m) for sk, m in zip(sks, ms)]
    inv = [1.0 / (jnp.sum(p, axis=-1, keepdims=True) + e) for p, e in zip(ps, es)]
    return [p * i for p, i in zip(ps, inv)], [e * i for e, i in zip(es, inv)]


def _swa_specs(W, cb_q, cb_k):
    B = SW_BLOCK
    assert (W // LANES) % SWA_GROUPS == 0 and (W // LANES) % SWA_GROUPS_BWD == 0
    cur = lambda w, cb: pl.BlockSpec((B, w), lambda n: (n, cb))
    prv = lambda w, cb: pl.BlockSpec((B, w), lambda n: (jnp.maximum(n - 1, 0), cb))
    specs = [cur(W, cb_q), cur(LANES, cb_k), prv(LANES, cb_k), cur(LANES, cb_k + 1), prv(LANES, cb_k + 1)]
    return specs + [cur(LANES, 0)] * 3 + [prv(LANES, 0)] * 3 + [_full((1, LANES))]


def _swa_fwd(proj, tabs, sinks_row, W, cb_q, cb_k):
    S = proj.shape[0]
    G2 = SW_Q_HEADS // SW_KV_HEADS // 2

    def body(q_ref, kc_ref, kp_ref, vc_ref, vp_ref, c0, c1, c2, p0, p1, p2, sk_ref, o_ref):
        n = pl.program_id(0)
        qr, heads, fold, _ = _swa_setup(n, q_ref, kc_ref, kp_ref, vc_ref, vp_ref, (c0, c1, c2), (p0, p1, p2))
        qs = [qr[:, j * LANES:(j + 1) * LANES].astype(BF16) for j in range(W // LANES)]
        for j0 in range(0, W // LANES, SWA_GROUPS):
            items = [(j, half) for j in range(j0, j0 + SWA_GROUPS) for half in range(2)]
            probs, _ = _swa_probs(items, qs, heads, fold, sk_ref, G2)
            pv = [_bdot(_unfold(p, fold[0]), heads[j // G2][2 + half]) for p, (j, half) in zip(probs, items)]
            for g in range(SWA_GROUPS):
                o_ref[:, (j0 + g) * LANES:(j0 + g + 1) * LANES] = (pv[2 * g] + pv[2 * g + 1]).astype(BF16)

    t = tuple(tabs)
    return _pcall(body, name="swa_fwd", grid=(S // SW_BLOCK,), in_specs=_swa_specs(W, cb_q, cb_k),
                  out_specs=pl.BlockSpec((SW_BLOCK, W), lambda n: (n, 0)), out_shape=_sds((S, W), BF16),
                  compiler_params=_cp("arbitrary"))(proj, proj, proj, proj, proj, *t, *t, sinks_row)


def _swa_bwd(proj, tabs, sinks_row, do, W, cb_q, cb_k):
    S = proj.shape[0]
    B = SW_BLOCK
    G2 = SW_Q_HEADS // SW_KV_HEADS // 2
    SKR = -(-SW_Q_HEADS // SUBLANES) * SUBLANES

    def body(q_ref, kc_ref, kp_ref, vc_ref, vp_ref, c0, c1, c2, p0, p1, p2, sk_ref, do_ref,
             dq_ref, dkc_ref, dkp_ref, dvc_ref, dvp_ref, dsk_ref):
        n = pl.program_id(0)
        qr, heads, fold, lane = _swa_setup(n, q_ref, kc_ref, kp_ref, vc_ref, vp_ref, (c0, c1, c2), (p0, p1, p2))
        prev = fold[0]

        @pl.when(n == 0)
        def _():
            dsk_ref[...] = jnp.zeros((SKR, LANES), F32)

        acc_k = [jnp.zeros((2 * B, LANES), F32) for _ in range(SW_KV_HEADS)]
        acc_v = [jnp.zeros((2 * B, LANES), F32) for _ in range(SW_KV_HEADS)]
        qs = [qr[:, j * LANES:(j + 1) * LANES].astype(BF16) for j in range(W // LANES)]
        dos = [do_ref[:, j * LANES:(j + 1) * LANES].astype(BF16) for j in range(W // LANES)]
        dqs = []
        for j0 in range(0, W // LANES, SWA_GROUPS_BWD):
            items = [(j, half) for j in range(j0, j0 + SWA_GROUPS_BWD) for half in range(2)]
            probs, psinks = _swa_probs(items, qs, heads, fold, sk_ref, G2)
            dps = [_fold(_bdot(dos[j], heads[j // G2][2 + half], "nt"), prev) for j, half in items]
            deltas = [jnp.sum(p * dp, axis=-1, keepdims=True) for p, dp in zip(probs, dps)]
            dss = [_unfold(p * (dp - dl), prev).astype(BF16) for p, dp, dl in zip(probs, dps, deltas)]
            pbs = [_unfold(p, prev).astype(BF16) for p in probs]
            dqp = [_bdot(ds, heads[j // G2][half]) for ds, (j, half) in zip(dss, items)]
            dkk = [_bdot(ds, qs[j], "tn") for ds, (j, half) in zip(dss, items)]
            dvv = [_bdot(p, dos[j], "tn") for p, (j, half) in zip(pbs, items)]
            for i, (j, half) in enumerate(items):
                hk, h = j // G2, 2 * j + half
                sel = (lane < SW_HD) if half == 0 else (lane >= SW_HD)
                acc_k[hk] = acc_k[hk] + jnp.where(sel, dkk[i], 0.0)
                acc_v[hk] = acc_v[hk] + jnp.where(sel, dvv[i], 0.0)
                dsk_ref[h:h + 1, :] += jnp.broadcast_to(-jnp.sum(psinks[i] * deltas[i], axis=0, keepdims=True), (1, LANES))
            dqs += [dqp[2 * g] + dqp[2 * g + 1] for g in range(SWA_GROUPS_BWD)]
        dq = jnp.concatenate(dqs, axis=1) * (SW_HD ** -0.5)
        dq_ref[...] = _unrope(dq, c0[...], c1[...], c2[...]).astype(BF16)
        fold = lambda a: a[:, :SW_HD] + a[:, SW_HD:]
        dkw = jnp.concatenate([fold(a) for a in acc_k], axis=1)
        dvw = jnp.concatenate([fold(a) for a in acc_v], axis=1)
        dkp_ref[...], dkc_ref[...] = dkw[:B], dkw[B:]
        dvp_ref[...], dvc_ref[...] = dvw[:B], dvw[B:]

    t = tuple(tabs)
    blk = lambda w: pl.BlockSpec((B, w), lambda n: (n, 0))
    o = _sds((S, LANES), F32)
    return _pcall(body, name="swa_bwd", grid=(S // B,), in_specs=_swa_specs(W, cb_q, cb_k) + [blk(W)],
                  out_specs=(blk(W), blk(LANES), blk(LANES), blk(LANES), blk(LANES), _full((SKR, LANES))),
                  out_shape=(_sds((S, W), BF16), o, o, o, o, _sds((SKR, LANES), F32)),
                  compiler_params=_cp("arbitrary"))(proj, proj, proj, proj, proj, *t, *t, sinks_row, do)


def _swa_kv_combine(dkc, dkp, dvc, dvp, tabs):
    S = dkc.shape[0]
    B = SW_BLOCK
    nb = S // B

    def fn(n, kc_ref, kp_ref, vc_ref, vp_ref, c0, c1, c2, o_ref):
        more = n < nb - 1
        dk = kc_ref[...] + jnp.where(more, kp_ref[...], 0.0)
        dv = vc_ref[...] + jnp.where(more, vp_ref[...], 0.0)
        o_ref[...] = jnp.concatenate([_unrope(dk, c0[...], c1[...], c2[...]), dv], axis=1).astype(BF16)

    cur = _rb(B, LANES)
    nxt = pl.BlockSpec((B, LANES), lambda n: (jnp.minimum(n + 1, nb - 1), 0))
    return _rows("swa_kv_combine", fn, S, B, (dkc, dkp, dvc, dvp, *tabs), [cur, nxt, cur, nxt, cur, cur, cur],
                 _sds((S, 2 * LANES), BF16), _rb(B, 2 * LANES))


ANY = pl.BlockSpec(memory_space=pl.ANY)


def _place():
    x, y, c = lax.axis_index("x"), lax.axis_index("y"), lax.axis_index("c")
    return x, y, c, [(1 - x, y), (x, 1 - y), (1 - x, 1 - y)]


def _comm_call(name, body, out_shapes, n_sems, n_local, *ins):
    return _pcall(body, name=name, out_shape=tuple(out_shapes), in_specs=[ANY] * len(ins), out_specs=tuple(ANY for _ in out_shapes),
                  scratch_shapes=[pltpu.SemaphoreType.DMA((n_sems,)), pltpu.SemaphoreType.DMA((n_sems,)),
                                  pltpu.SemaphoreType.DMA((n_local,))])(*ins)


def _remote(src, dst, send, recv, k, to):
    return pltpu.make_async_remote_copy(src_ref=src, dst_ref=dst, send_sem=send.at[k], recv_sem=recv.at[k], device_id=to,
                                        device_id_type=MESH)


def _gather_chips(name, arrs):
    n = len(arrs)
    Lh = arrs[0].shape[0] // 2

    def body(*refs):
        w, o, (send, recv, _) = refs[:n], refs[n:2 * n], refs[2 * n:]
        x, y, c, chips = _place()
        me, sib = 2 * x + y, (x, y, 1 - c)
        own, other = pl.ds(c * Lh, Lh), pl.ds((1 - c) * Lh, Lh)
        idx = [2 * cx + cy for cx, cy in chips]
        first = [[_remote(w[a].at[own], o[a].at[me, own], send, recv, 6 * a + j, (*chips[j], c)) for j in range(3)] for a in range(n)]
        passed = [[_remote(o[a].at[idx[j], own], o[a].at[idx[j], own], send, recv, 6 * a + 3 + j, sib) for j in range(3)] for a in range(n)]
        for cp in [cp for row in first for cp in row]:
            cp.start()
        for j in range(3):
            for a in range(n):
                _remote(w[a].at[own], o[a].at[idx[j], own], send, recv, 6 * a + j, (*chips[j], c)).wait_recv()
                passed[a][j].start()
        for j in range(3):
            for a in range(n):
                _remote(w[a].at[other], o[a].at[idx[j], other], send, recv, 6 * a + 3 + j, sib).wait_recv()
        for cp in [cp for row in first + passed for cp in row]:
            cp.wait_send()

    return _comm_call(name, body, [_sds((4,) + a.shape, a.dtype) for a in arrs], 6 * n, 1, *arrs)


def _pair_swap(name, arrs, whole=False):
    n = len(arrs)
    Lh = arrs[0].shape[0] if whole else arrs[0].shape[0] // 2

    def body(*refs):
        g, o, (send, recv, _) = refs[:n], refs[n:2 * n], refs[2 * n:]
        x, y, c, _ = _place()
        cps = [_remote(g[a] if whole else g[a].at[pl.ds((1 - c) * Lh, Lh)], o[a], send, recv, a, (x, y, 1 - c)) for a in range(n)]
        for cp in cps:
            cp.start()
        for cp in cps:
            cp.wait()

    return _comm_call(name, body, [_sds((Lh,) + a.shape[1:], a.dtype) for a in arrs], n, 1, *arrs)


def _chip_slice(ref, axis, s):
    if axis is None:
        return ref.at[s]
    q = ref.shape[axis] // 4
    start = s * q if isinstance(s, int) else pl.multiple_of(s * q, q)
    return ref.at[tuple([slice(None)] * axis + [pl.ds(start, q)])]


def _scatter_chips(name, items):
    n = len(items)
    part = lambda a, ax: a.shape[1:] if ax is None else tuple(d // 4 if i == ax else d for i, d in enumerate(a.shape))

    def body(*refs):
        p, o, (send, recv, _) = refs[:n], refs[n:2 * n], refs[2 * n:]
        x, y, c, chips = _place()
        me = 2 * x + y
        idx = [2 * cx + cy for cx, cy in chips]
        cps = [_remote(_chip_slice(p[a], items[a][1], idx[j]), o[a].at[me], send, recv, 3 * a + j, (*chips[j], c))
               for a in range(n) for j in range(3)]
        for cp in cps:
            cp.start()
        for a in range(n):
            for j in range(3):
                _remote(_chip_slice(p[a], items[a][1], me), o[a].at[idx[j]], send, recv, 3 * a + j, (*chips[j], c)).wait_recv()
        for cp in cps:
            cp.wait_send()

    return _comm_call(name, body, [_sds((4,) + part(a, ax), a.dtype) for a, ax in items], 3 * n, 1, *[a for a, _ in items])


def _own_part(a, axis, me):
    if axis is None:
        return lax.dynamic_index_in_dim(a, me, 0, keepdims=False)
    q = a.shape[axis] // 4
    return lax.dynamic_slice_in_dim(a, me * q, q, axis)


HBM = pl.BlockSpec(memory_space=pltpu.HBM)
SEM = pl.BlockSpec(memory_space=pltpu.SEMAPHORE)
EFFECT = pltpu.SideEffectType.DATAFLOW_SIDE_EFFECTING


def _split_start(name, arrs, land_shapes, plan, nc, after=None):
    n = len(arrs)
    lands = [lax.empty(s.shape, s.dtype) for s in land_shapes]
    ins = list(arrs) + lands + ([] if after is None else [after])

    def body(*refs):
        outs = refs[len(ins):]
        for k, (src, dst, _, peer) in enumerate(plan(refs[:n], refs[n:n + len(lands)])):
            pltpu.make_async_remote_copy(src_ref=src, dst_ref=dst, send_sem=outs[k], recv_sem=outs[nc + k], device_id=peer,
                                         device_id_type=MESH).start()
        outs[-1][...] = jnp.zeros((SUBLANES, LANES), F32)

    nt = n + len(lands)
    thru = [pltpu.HBM(a.shape, a.dtype) for a in list(arrs) + lands]
    outs = _pcall(body, name=name, out_shape=tuple([pltpu.SemaphoreType.DMA(())] * (2 * nc) + thru + [_sds((SUBLANES, LANES), F32)]),
                  in_specs=[HBM] * nt + [ANY] * (len(ins) - nt),
                  out_specs=tuple([SEM] * (2 * nc) + [HBM] * nt + [pl.BlockSpec(memory_space=pltpu.VMEM)]),
                  input_output_aliases={i: 2 * nc + i for i in range(nt)},
                  compiler_params=pltpu.CompilerParams(has_side_effects=EFFECT))(
        *[pltpu.with_memory_space_constraint(a, pltpu.HBM) for a in ins[:nt]], *ins[nt:])
    return dict(sems=outs[:2 * nc], arrs=outs[2 * nc:2 * nc + n], lands=outs[2 * nc + n:2 * nc + nt], token=outs[-1], plan=plan, nc=nc)


def _split_wait(name, handle, after):
    arrs, lands, sems, nc = list(handle["arrs"]), list(handle["lands"]), list(handle["sems"]), handle["nc"]
    n, nt = len(arrs), len(arrs) + len(lands)

    def body(*refs):
        sem = refs[nt:nt + 2 * nc]
        for k, (src, _, landing, peer) in enumerate(handle["plan"](refs[:n], refs[n:nt])):
            cp = pltpu.make_async_remote_copy(src_ref=src, dst_ref=landing, send_sem=sem[k], recv_sem=sem[nc + k], device_id=peer,
                                              device_id_type=MESH)
            cp.wait_send()
            cp.wait_recv()

    thru = tuple(pltpu.HBM(a.shape, a.dtype) for a in arrs + lands)
    outs = _pcall(body, name=name, out_shape=thru, in_specs=[HBM] * nt + [SEM] * (2 * nc) + [ANY], out_specs=tuple([HBM] * nt),
                  input_output_aliases={i: i for i in range(nt)},
                  compiler_params=pltpu.CompilerParams(has_side_effects=EFFECT))(*arrs, *lands, *sems, after)
    return list(outs[:n]), list(outs[n:])


WHOLE = "whole"


def _plan_chips(axes):
    def plan(src, land):
        x, y, c, chips = _place()
        idx = [2 * cx + cy for cx, cy in chips]
        part = lambda a, s: src[a] if axes[a] is WHOLE else _chip_slice(src[a], axes[a], s)
        return [(part(a, idx[j]), land[a].at[2 * x + y], land[a].at[idx[j]], (*chips[j], c))
                for a in range(len(land)) for j in range(3)]
    return plan


def _plan_sibling(half):
    def plan(src, land):
        x, y, c, _ = _place()
        lh = lambda a: src[a].shape[0] // 2
        return [(src[a].at[pl.ds((1 - c) * lh(a), lh(a))] if half else src[a], land[a], land[a], (x, y, 1 - c))
                for a in range(len(land))]
    return plan


def _chips_start(name, arrs, axes, after=None):
    part = lambda a, ax: a.shape if ax is WHOLE else a.shape[1:] if ax is None else tuple(d // 4 if i == ax else d for i, d in enumerate(a.shape))
    return _split_start(name, arrs, [_sds((4,) + part(a, ax), a.dtype) for a, ax in zip(arrs, axes)], _plan_chips(axes), 3 * len(arrs), after)


def _sibling_start(name, arrs, half, after=None):
    shp = lambda a: (a.shape[0] // 2,) + a.shape[1:] if half else a.shape
    return _split_start(name, arrs, [_sds(shp(a), a.dtype) for a in arrs], _plan_sibling(half), len(arrs), after)


def _gather_all(name, b):
    R, C = b.shape
    flips = [(dx, dy, dc) for dx in (0, 1) for dy in (0, 1) for dc in (0, 1)][1:]

    def body(b_ref, o_ref, send, recv, lsem):
        x, y, c, _ = _place()
        me = 4 * x + 2 * y + c
        peers = [(x ^ dx, y ^ dy, c ^ dc) for dx, dy, dc in flips]
        mine = pltpu.make_async_copy(b_ref, o_ref.at[me], lsem.at[0])
        mine.start()
        cps = [_remote(b_ref, o_ref.at[me], send, recv, k, peer) for k, peer in enumerate(peers)]
        for cp in cps:
            cp.start()
        for k, (px, py, pc) in enumerate(peers):
            _remote(b_ref, o_ref.at[4 * px + 2 * py + pc], send, recv, k, (px, py, pc)).wait_recv()
        for cp in cps:
            cp.wait_send()
        mine.wait()

    return _comm_call(name, body, [_sds((8, R, C), b.dtype)], 7, 1, b)[0]


def _block_rows(rows, width):
    return _tile(rows, max(SUBLANES, (1 << 19) // width), SUBLANES)


def _add_half(name, g, got):
    L, A, B = g.shape
    Lh = L // 2
    tq = _block_rows(A, B)

    def body(c_ref, g_ref, r_ref, o_ref):
        o_ref[...] = (g_ref[...] + r_ref[...]).astype(BF16)

    spec = pltpu.PrefetchScalarGridSpec(
        num_scalar_prefetch=1, grid=(Lh, A // tq),
        in_specs=[pl.BlockSpec((1, tq, B), lambda l, i, c_ref: (c_ref[0] * Lh + l, i, 0)),
                  pl.BlockSpec((1, tq, B), lambda l, i, c_ref: (l, i, 0))],
        out_specs=pl.BlockSpec((1, tq, B), lambda l, i, c_ref: (l, i, 0)))
    return _pcall(body, name=name, grid_spec=spec, out_shape=_sds((Lh, A, B), BF16),
                  compiler_params=_cp("arbitrary", "arbitrary"))(lax.axis_index("c").reshape(1).astype(jnp.int32), g, got)


def _sum_slots(name, a):
    n, R, C = a.shape
    tq = _block_rows(R, n * C)

    def fn(i, a_ref, o_ref):
        t = a_ref[0].astype(F32)
        for s in range(1, n):
            t = t + a_ref[s].astype(F32)
        o_ref[...] = t

    return _rows(name, fn, R, tq, (a,), [pl.BlockSpec((n, tq, C), lambda i: (0, i, 0))], _sds((R, C), F32), _rb(tq, C))


def _adam_update(w, g, m, v):
    mn = ADAM_B1 * m + (1.0 - ADAM_B1) * g
    vn = ADAM_B2 * v + (1.0 - ADAM_B2) * (g * g)
    m_hat = mn / (1.0 - ADAM_B1 ** ADAM_STEP)
    v_hat = vn / (1.0 - ADAM_B2 ** ADAM_STEP)
    return -ADAM_LR * (m_hat / (jnp.sqrt(v_hat) + ADAM_EPS) + ADAM_WD * w), mn, vn


def _adamw(name, w, g, m, v):
    R, C = w.shape
    tq = _tile(R, 256, SUBLANES)

    def fn(i, w_ref, g_ref, m_ref, v_ref, d_ref, mo_ref, vo_ref):
        d_ref[...], mo_ref[...], vo_ref[...] = _adam_update(w_ref[...], g_ref[...], m_ref[...], v_ref[...])

    r, o = _rb(tq, C), _sds((R, C), F32)
    return _rows(name, fn, R, tq, (w, g, m, v), [r, r, r, r], (o, o, o), (r, r, r))


def _adamw_halves(name, w, mine, theirs, m, v, l0, prev=None):
    L, A, B = w.shape
    Lh = mine.shape[0]
    tq = _tile(A, 256, SUBLANES)

    def body(c_ref, w_ref, a_ref, b_ref, m_ref, v_ref, *refs):
        g_ref, d_ref, mo_ref, vo_ref = refs[-4:]
        is_mine = pl.program_id(0) // Lh == c_ref[0]
        g = jnp.where(is_mine, a_ref[...], b_ref[...])
        g_ref[...] = g
        d_ref[...], mo_ref[...], vo_ref[...] = _adam_update(w_ref[...], g, m_ref[...], v_ref[...])

    full = pl.BlockSpec((1, tq, B), lambda l, i, c_ref: (l0 + l, i, 0))
    half = pl.BlockSpec((1, tq, B), lambda l, i, c_ref: (l % Lh, i, 0))
    o = _sds((L, A, B), F32)
    prev = list(prev or ())
    spec = pltpu.PrefetchScalarGridSpec(num_scalar_prefetch=1, grid=(2 * Lh, A // tq), in_specs=[full, half, half, full, full] + [ANY] * len(prev),
                                        out_specs=(full, full, full, full))
    return _pcall(body, name=name, grid_spec=spec, out_shape=(o, o, o, o), input_output_aliases={6 + i: i for i in range(len(prev))},
                  compiler_params=_cp("arbitrary", "arbitrary"))(lax.axis_index("c").reshape(1).astype(jnp.int32), w, mine, theirs, m, v, *prev)


def _pack(arrs, width, lead=()):
    nl = len(lead)
    flat = jnp.concatenate([a.reshape(lead + (-1,)) for a in arrs], axis=nl)
    n = flat.shape[-1]
    unit = PACK_ROWS * width
    tot = -(-n // unit) * unit
    flat = jnp.pad(flat, [(0, 0)] * nl + [(0, tot - n)])
    return flat.reshape(lead + (tot // width, width))


def _unpack(buf, shapes, lead=()):
    flat = buf.reshape(lead + (-1,))
    out, off = [], 0
    for s in shapes:
        n = int(np.prod(s))
        out.append(flat[..., off:off + n].reshape(lead + tuple(s)))
        off += n
    return out


def _in_groups(W, H):
    o_sq = 4 * W + 2 * H
    o_k = o_sq + W
    o_g = o_k + 2 * KV_W
    return [(0, 4 * W), (o_sq, o_k), (o_g, o_g + 2 * W), (o_k, o_g), (4 * W, o_sq)]


def _relayout_in(shards, W, H):
    c4 = sum(hi - lo for lo, hi in _in_groups(W, H)) // 4
    parts = []
    for lo, hi in _in_groups(W, H):
        for s in range(4):
            a, b = max(lo, s * c4), min(hi, (s + 1) * c4)
            if a < b:
                parts.append(shards[s][:, a - s * c4:b - s * c4])
    parts.append(jnp.zeros((shards.shape[1], BA_W - 2 * H), shards.dtype))
    return jnp.concatenate(parts, axis=1)


def _shard_in(d, W, H):
    groups = _in_groups(W, H)
    starts = [sum(hi - lo for lo, hi in groups[:i]) for i in range(len(groups))]
    stored = sorted(zip(groups, starts))
    c4 = sum(hi - lo for lo, hi in groups) // 4
    out = []
    for s in range(4):
        parts = []
        for (lo, hi), at in stored:
            a, b = max(lo, s * c4), min(hi, (s + 1) * c4)
            if a < b:
                parts.append(d[:, :, at + a - lo:at + b - lo])
        out.append(jnp.concatenate(parts, axis=2))
    return jnp.stack(out)


def _lane_row(vals, at):
    return jnp.pad(vals, (at, LANES - at - vals.shape[0]))[None]


def _layer_fwd(x, lw, tabs, W, H, more=None, h=None, g1_next=None):
    D = x.shape[1]
    cbk = 7 * W // LANES
    if h is None:
        h = _pre_norm(x, lw["g1"])
    proj = _mm("mm_in", h, lw["win"], "nn", BF16, tm=4096, tn=512)
    ba = _mm("mm_ba", h, lw["win"][:, 7 * W + 2 * KV_W:], "nn", F32)
    qkv = _dn_prep(proj, lw["conv"], W)
    beta_b, g_b = _dn_gates(ba, lw["alog"], lw["dt"], H)
    o, st = _delta_fwd(qkv, beta_b, g_b, H, DELTA_CB, DELTA_HB)
    oa = _dn_out(o, proj, lw["ng"], W, 3)
    ob = _swa_fwd(proj, tabs, lw["sinks"], W, 4, cbk)
    if more is not None:
        lw.update(more(ob))
    ya = _mm("mm_up_dn", oa, lw["wup_dn"], "nn", BF16)
    yb = _mm("mm_up_sw", ob, lw["wup_sw"], "nn", BF16)
    mixin = _mix(proj, ya, yb, D, 5)
    mix = _mm("mm_o", mixin, lw["wo"], "nn", F32)
    x1, h2 = _post_mix(x, mix, lw["g2"], lw["g3"])
    f1, act = _mm("mm_ff1", h2, lw["wff1"], "nn", tn=512, out_dtypes=(BF16, BF16), epi=lambda acc: (acc, jnp.square(jnp.maximum(acc, 0.0))))
    ff = _mm("mm_ff2", act, lw["wff2"], "nn", F32, tm=2048)
    x2, h_next = _post_mlp(x1, ff, lw["g4"], lw["g4"] if g1_next is None else g1_next)
    saved = dict(x=x, h=h, proj=proj, ba=ba, qkv=qkv, beta_b=beta_b, g_b=g_b, o=o, st=st, oa=oa, ob=ob, ya=ya, yb=yb,
                 mixin=mixin, mix=mix, x1=x1, h2=h2, f1=f1, act=act, ff=ff)
    return x2, h_next, saved


def _layer_bwd(dx2, lw, sv, tabs, W, H, l, big, weights_done=None):
    D = dx2.shape[1]
    cbk = 7 * W // LANES
    big = dict(big)
    dff, dg4 = _post_mlp_bwd(sv["ff"], lw["g4"], dx2)
    df1 = _mm("mm_ff2_dx", dff, lw["wff2"], "nt", BF16, extras=(sv["f1"],),
              epi=lambda acc, f1: (acc * 2.0 * jnp.maximum(f1.astype(F32), 0.0),))
    big["w_ff2"] = _mm("mm_ff2_dw", sv["act"], dff, "tn", slab=(big["w_ff2"], l))
    dh2 = _mm("mm_ff1_dx", df1, lw["wff1"], "nt", F32, tm=2048)
    big["w_ff1"] = _mm("mm_ff1_dw", sv["h2"], df1, "tn", slab=(big["w_ff1"], l))
    dx1, dmix, dg3, dg2 = _mid_bwd(sv["x1"], lw["g3"], dh2, dx2, sv["mix"], lw["g2"])
    dmixin = _mm("mm_o_dx", dmix, lw["wo"], "nt", BF16)
    big["w_o"] = _mm("mm_o_dw", sv["mixin"], dmix, "tn", slab=(big["w_o"], l))
    dya, dyb, dga, dgb = _mix_bwd(sv["proj"], sv["ya"], sv["yb"], dmixin, D, 5)
    doa = _mm("mm_up_dn_dx", dya, lw["wup_dn"], "nt", BF16)
    big["w_up_dn"] = _mm("mm_up_dn_dw", sv["oa"], dya, "tn", slab=(big["w_up_dn"], l))
    dob = _mm("mm_up_sw_dx", dyb, lw["wup_sw"], "nt", BF16)
    big["w_up_sw"] = _mm("mm_up_sw_dw", sv["ob"], dyb, "tn", slab=(big["w_up_sw"], l))
    do, dz, dng = _dn_out_bwd(sv["o"], sv["proj"], lw["ng"], doa, W, 3)
    dqkvn, dbeta_b, dg_b = _delta_bwd(sv["qkv"], sv["beta_b"], sv["g_b"], sv["st"], do, H, DELTA_CB, DELTA_HB)
    dba, dalog, ddt = _dn_gates_bwd(sv["ba"], lw["alog"], lw["dt"], dbeta_b, dg_b, H)
    dc, dconv = _dn_prep_bwd_a(sv["proj"], lw["conv"], dqkvn, W)
    dqkv = _dn_prep_bwd_b(dc, lw["conv"], W)
    dq_sw, dkc, dkp, dvc, dvp, dsk = _swa_bwd(sv["proj"], tabs, lw["sinks"], dob, W, 4, cbk)
    dkv = _swa_kv_combine(dkc, dkp, dvc, dvp, tabs)
    dproj = jnp.concatenate([dqkv, dz, dq_sw, dga, dgb, dkv, dba], axis=1)
    big["w_in"] = _mm("mm_in_dw", sv["h"], dproj, "tn", tn=768, slab=(big["w_in"], l))
    win = lw["win"]
    if weights_done is not None:
        win = win + weights_done(big).astype(BF16)
    dh = _mm("mm_in_dx", dproj, win, "nt", F32, tm=2048, tk=768)
    dx, dg1 = _pre_norm_bwd(sv["x"], lw["g1"], dh, dx1)
    grads = dict(pre_mix_g=dg1[0], dn_conv_w=dconv, dn_a_log=dalog[0, H:2 * H], dn_dt_bias=ddt[0, H:2 * H], dn_norm_g=dng[0],
                 sw_sinks=dsk[:SW_Q_HEADS, 0], post_mix_g=dg2[0], pre_mlp_g=dg3[0], post_mlp_g=dg4[0])
    return dx, grads, big


_WEIGHTS = ["pre_mix_g", "w_in", "dn_conv_w", "dn_a_log", "dn_dt_bias", "dn_norm_g", "sw_sinks", "w_up_dn", "w_up_sw", "w_o",
            "post_mix_g", "pre_mlp_g", "w_ff1", "w_ff2", "post_mlp_g"]
_BIG = {"w_in": 2, "w_up_dn": 1, "w_up_sw": 1, "w_o": 1, "w_ff1": 2, "w_ff2": 1}
_SMALL = [n for n in _WEIGHTS if n not in _BIG]


def _step(P):
    x, target = P["x"][0], P["loss_target"][0]
    S, D = x.shape
    L = P["pre_mix_g"].shape[0]
    H, W = DN_HEADS, DN_HEADS * DN_DK
    assert W == D == SW_Q_HEADS * SW_HD and KV_W == LANES
    me = 2 * lax.axis_index("x") + lax.axis_index("y")

    assert L % 4 == 0
    names = list(_BIG) + ["dn_conv_w"]
    local = [P[n].astype(BF16) for n in _BIG] + [P["dn_conv_w"]]
    early_names = ("w_in", "dn_conv_w")
    tail_names = [n for n in names if n not in early_names]
    own_slot = lambda gathered, mine: [lax.dynamic_update_slice_in_dim(g, w[None], me, 0) for g, w in zip(gathered, mine)]
    gather = lambda name, arrs, after=None: _chips_start(name, arrs, [WHOLE] * len(arrs), after)
    arrived = lambda name, h, after, keys: dict(zip(keys, own_slot(*reversed(_split_wait(name, h, after)))))
    h_first = gather("weights_first_start", [a[:1] for n, a in zip(names, local) if n in early_names])
    early = arrived("weights_first_wait", h_first, x, early_names)
    h_tail = gather("weights_tail_start", [a[:1] for n, a in zip(names, local) if n in tail_names], early["w_in"])
    h_next = gather("weights_next_start", [a[1:2] for a in local], h_tail["token"])
    h_rest = gather("weights_rest_start", [a[2:] for a in local], h_next["token"])

    def head(full, l, k):
        return dict(
            g1=P["pre_mix_g"][l][None], win=_relayout_in(full["w_in"][:, k], W, H),
            conv=jnp.concatenate([full["dn_conv_w"][s, k] for s in range(4)], axis=-1),
            alog=_lane_row(P["dn_a_log"][l], H), dt=_lane_row(P["dn_dt_bias"][l], H), ng=P["dn_norm_g"][l][None],
            sinks=_lane_row(P["sw_sinks"][l], 0), g2=P["post_mix_g"][l][None], g3=P["pre_mlp_g"][l][None], g4=P["post_mlp_g"][l][None])

    def tail(full, k):
        rows = lambda n: full[n][:, k].reshape(-1, full[n].shape[-1])
        return dict(wup_dn=rows("w_up_dn"), wup_sw=rows("w_up_sw"), wo=rows("w_o"), wff2=rows("w_ff2"),
                    wff1=jnp.concatenate([full["w_ff1"][s, k] for s in range(4)], axis=-1))

    tabs = _rope_tables(P["positions"].reshape(S, 1))
    lws = [head(early, 0, 0)]
    lws[0]["g1"] = lws[0]["g1"] + h_rest["token"][0, 0]

    saved, h = [], None
    for l in range(L):
        if l == 1:
            late = arrived("weights_next_wait", h_next, x, names)
            lws.append({**head(late, 1, 0), **tail(late, 0)})
        if l == 2:
            late = arrived("weights_rest_wait", h_rest, x, names)
            lws.extend({**head(late, k + 2, k), **tail(late, k)} for k in range(L - 2))
        first_tail = lambda after: tail(arrived("weights_tail_wait", h_tail, after, tail_names), 0)
        g1_next = P["pre_mix_g"][l + 1][None] if l + 1 < L else None
        x, h, sv = _layer_fwd(x, lws[l], tabs, W, H, first_tail if l == 0 else None, h, g1_next)
        saved.append(sv)
    loss_row, dx = _loss_head(x, target)

    Lb = L // 2
    layer_grads = [None] * L
    F = 4 * P["w_ff1"].shape[2]
    per_layer = dict(w_in=(D, 7 * W + 2 * KV_W + BA_W), w_up_dn=(W, D), w_up_sw=(W, D), w_o=(D, D), w_ff1=(D, F), w_ff2=(F, D))
    batch = [{n: lax.empty((Lb,) + per_layer[n], F32) for n in _BIG} for _ in range(2)]
    axes = [None if n == "w_in" else ax for n, ax in _BIG.items()]

    def pair_sums(tag, h_swap, after):
        g, got = _split_wait("grad_swap_wait_" + tag, h_swap, after)
        part = {n: _add_half("grad_pair_add_%s_%s" % (tag, n), a, r) for n, a, r in zip(_BIG, g, got)}
        return [_shard_in(part[n], W, H) if n == "w_in" else part[n] for n in _BIG]

    def chip_sums(tag, h_scat, after):
        parts, slots = _split_wait("grad_scatter_wait_" + tag, h_scat, after)
        halves = []
        for n, s, a, ax in zip(_BIG, slots, parts, axes):
            s = lax.dynamic_update_slice_in_dim(s, _own_part(a, ax, me)[None], me, 0)
            halves.append(_sum_slots("grad_chip_sum_%s_%s" % (tag, n), s.reshape(4, -1, s.shape[-1])).reshape(s.shape[1:]))
        h = _sibling_start("grad_share_start_" + tag, halves, False)
        return _split_wait("grad_share_wait_" + tag, h, halves[0])

    swaps = {}

    def swap_start(tag):
        def hook(big):
            swaps[tag] = _sibling_start("grad_swap_start_" + tag, [big[n] for n in _BIG], True)
            return swaps[tag]["token"][0, 0]
        return hook

    for l in reversed(range(L)):
        hook = swap_start("hi") if l == Lb else swap_start("lo") if l == 0 else None
        dx, layer_grads[l], batch[l // Lb] = _layer_bwd(dx, lws[l], saved[l], tabs, W, H, l % Lb, batch[l // Lb], hook)
        if l == Lb - 1:
            h_scat_hi = _chips_start("grad_scatter_start_hi", pair_sums("hi", swaps["hi"], dx), axes)
            if l > 0:
                lws[l - 1]["g4"] = lws[l - 1]["g4"] + h_scat_hi["token"][0, 0]

    grads = {n: jnp.stack([layer_grads[l][n] for l in range(L)]) for n in _SMALL}
    small_shapes = [(1,)] + [grads[n].shape for n in _SMALL]
    slots = _gather_all("small_gather", _pack([loss_row[0, :1]] + [grads[n] for n in _SMALL], LANES))
    h_scat_lo = _chips_start("grad_scatter_start_lo", pair_sums("lo", swaps["lo"], slots), axes)
    tot = _sum_slots("small_sum", slots + h_scat_lo["token"][0, 0])
    small = _unpack(tot, small_shapes)
    loss = small[0][0]
    gsum, delta, new_m, new_v = dict(zip(_SMALL, small[1:])), {}, {}, {}
    cw = P["dn_conv_w"].shape[2]
    gsum["dn_conv_w"] = lax.dynamic_slice_in_dim(gsum["dn_conv_w"], me * cw, cw, axis=2)
    sm_shapes = [P[n].shape for n in _SMALL]
    outs = _adamw("adamw_small", *(_pack([src[pre + n] for n in _SMALL], LANES)
                                   for src, pre in ((P, ""), (gsum, ""), (P, "m_"), (P, "v_"))))
    for d, o in zip((delta, new_m, new_v), outs):
        d.update(zip(_SMALL, _unpack(o, sm_shapes)))

    upper = {n: _adamw_halves("adamw_hi_" + n, P[n], mine, their, P["m_" + n], P["v_" + n], Lb)
             for n, mine, their in zip(_BIG, *chip_sums("hi", h_scat_hi, outs[0]))}
    for n, mine, their in zip(_BIG, *chip_sums("lo", h_scat_lo, upper["w_in"][0])):
        gsum[n], delta[n], new_m[n], new_v[n] = _adamw_halves("adamw_lo_" + n, P[n], mine, their, P["m_" + n], P["v_" + n], 0, upper[n])

    return (loss, dx[None], *[gsum[n] for n in _WEIGHTS], *[delta[n] for n in _WEIGHTS],
            *[new_m[n] for n in _WEIGHTS], *[new_v[n] for n in _WEIGHTS])


def kernel(x, positions, pre_mix_g, w_in, dn_conv_w, dn_a_log, dn_dt_bias, dn_norm_g, sw_sinks, w_up_dn, w_up_sw, w_o, post_mix_g, pre_mlp_g, w_ff1, w_ff2, post_mlp_g, loss_target, m_pre_mix_g, m_w_in, m_dn_conv_w, m_dn_a_log, m_dn_dt_bias, m_dn_norm_g, m_sw_sinks, m_w_up_dn, m_w_up_sw, m_w_o, m_post_mix_g, m_pre_mlp_g, m_w_ff1, m_w_ff2, m_post_mlp_g, v_pre_mix_g, v_w_in, v_dn_conv_w, v_dn_a_log, v_dn_dt_bias, v_dn_norm_g, v_sw_sinks, v_w_up_dn, v_w_up_sw, v_w_o, v_post_mix_g, v_pre_mlp_g, v_w_ff1, v_w_ff2, v_post_mlp_g):
    vals = (x, positions, pre_mix_g, w_in, dn_conv_w, dn_a_log, dn_dt_bias, dn_norm_g, sw_sinks, w_up_dn, w_up_sw, w_o, post_mix_g, pre_mlp_g, w_ff1, w_ff2, post_mlp_g, loss_target, m_pre_mix_g, m_w_in, m_dn_conv_w, m_dn_a_log, m_dn_dt_bias, m_dn_norm_g, m_sw_sinks, m_w_up_dn, m_w_up_sw, m_w_o, m_post_mix_g, m_pre_mlp_g, m_w_ff1, m_w_ff2, m_post_mlp_g, v_pre_mix_g, v_w_in, v_dn_conv_w, v_dn_a_log, v_dn_dt_bias, v_dn_norm_g, v_sw_sinks, v_w_up_dn, v_w_up_sw, v_w_o, v_post_mix_g, v_pre_mlp_g, v_w_ff1, v_w_ff2, v_post_mlp_g)
    names = ["x", "positions"] + _WEIGHTS + ["loss_target"] + ["m_" + n for n in _WEIGHTS] + ["v_" + n for n in _WEIGHTS]
    return _step(dict(zip(names, vals)))
```

```python
import numpy as np
import jax
import jax.numpy as jnp
from jax import lax
from jax.experimental import pallas as pl
from jax.experimental.pallas import tpu as pltpu

F32, BF16 = jnp.float32, jnp.bfloat16
MESH = pl.DeviceIdType.MESH

DN_HEADS = 8
DN_DK = 128
DN_CONV = 4
DN_CHUNK = 64
SW_Q_HEADS = 16
SW_KV_HEADS = 2
SW_HD = 64
SW_BLOCK = 128
ROPE_THETA = 500000.0
ROT_DIM = SW_HD // 4
EPS = 1e-6
ADAM_LR, ADAM_B1, ADAM_B2, ADAM_EPS, ADAM_WD, ADAM_STEP = 0.001, 0.9, 0.999, 1e-08, 0.01, 10

LANES = 128
SUBLANES = 8
VMEM_LIMIT = 48 * 1024 * 1024
KV_W = SW_KV_HEADS * SW_HD
BA_W = 256
PACK_ROWS = 512
DELTA_CB = 4
DELTA_HB = 8


def _pcall(body, **kw):
    return pl.pallas_call(body, **kw)


def _cp(*sem):
    return pltpu.CompilerParams(dimension_semantics=sem, vmem_limit_bytes=VMEM_LIMIT)


def _tile(n, pref, unit=LANES):
    if n <= pref:
        return n
    t = (pref // unit) * unit
    while t > unit and n % t:
        t -= unit
    assert n % t == 0, (n, pref)
    return t


def _sds(shape, dtype):
    return jax.ShapeDtypeStruct(tuple(shape), dtype)


_DIMS = {"nn": ((1,), (0,)), "nt": ((1,), (1,)), "tn": ((0,), (0,))}


def _mm(name, a, b, mode, out_dtype=F32, tm=2048, tn=1024, tk=1024, extras=(), epi=None, out_dtypes=None, slab=None):
    if mode == "nn":
        (M, K), (_, N) = a.shape, b.shape
    elif mode == "nt":
        (M, K), (N, _) = a.shape, b.shape
    else:
        (K, M), (_, N) = a.shape, b.shape
    tm, tn, tk = _tile(M, tm), _tile(N, tn), _tile(K, tk)
    nk = K // tk
    a_spec = {"nn": pl.BlockSpec((tm, tk), lambda i, j, k: (i, k)),
              "nt": pl.BlockSpec((tm, tk), lambda i, j, k: (i, k)),
              "tn": pl.BlockSpec((tk, tm), lambda i, j, k: (k, i))}[mode]
    b_spec = {"nn": pl.BlockSpec((tk, tn), lambda i, j, k: (k, j)),
              "nt": pl.BlockSpec((tn, tk), lambda i, j, k: (j, k)),
              "tn": pl.BlockSpec((tk, tn), lambda i, j, k: (k, j))}[mode]
    dims = (_DIMS[mode], ((), ()))
    out_dtypes = tuple(out_dtypes or (out_dtype,))
    ne, no = len(extras), len(out_dtypes)
    o_spec = pl.BlockSpec((tm, tn), lambda i, j, k: (i, j))

    def body(*refs):
        a_ref, b_ref, ex = refs[0], refs[1], refs[2:2 + ne]
        outs = refs[-no:] if nk == 1 else refs[-1 - no:-1]
        part = lax.dot_general(a_ref[...], b_ref[...], dims, preferred_element_type=F32)

        def finish(acc):
            res = epi(acc, *[e[...] for e in ex]) if epi else (acc,)
            for o, r, dt in zip(outs, res, out_dtypes):
                if slab is None:
                    o[...] = r.astype(dt)
                else:
                    o[0] = r.astype(dt)

        if nk == 1:
            finish(part)
            return
        acc_ref, k = refs[-1], pl.program_id(2)

        @pl.when(k == 0)
        def _():
            acc_ref[...] = part

        @pl.when((k > 0) & (k < nk - 1))
        def _():
            acc_ref[...] += part

        @pl.when(k == nk - 1)
        def _():
            finish(acc_ref[...] + part)

    kw = dict(name=name, grid=(M // tm, N // tn, nk), scratch_shapes=[] if nk == 1 else [pltpu.VMEM((tm, tn), F32)],
              compiler_params=_cp("parallel", "parallel", "arbitrary"))
    if slab is not None:
        buf, l = slab
        return _pcall(body, in_specs=[a_spec, b_spec, ANY], out_specs=pl.BlockSpec((1, tm, tn), lambda i, j, k: (l, i, j)),
                      out_shape=_sds(buf.shape, buf.dtype), input_output_aliases={2: 0}, **kw)(a, b, buf)
    out = _pcall(body, in_specs=[a_spec, b_spec] + [o_spec] * ne, out_specs=tuple(o_spec for _ in out_dtypes),
                 out_shape=tuple(_sds((M, N), dt) for dt in out_dtypes), **kw)(a, b, *extras)
    return out if no > 1 else out[0]


def _rows(name, fn, n_rows, tq, ins, in_specs, out_shapes, out_specs):
    def body(*refs):
        fn(pl.program_id(0), *refs)

    return _pcall(body, name=name, grid=(n_rows // tq,), in_specs=in_specs, out_specs=out_specs,
                  out_shape=out_shapes, compiler_params=_cp("arbitrary"))(*ins)


def _rb(tq, w, cb=0):
    return pl.BlockSpec((tq, w), lambda i: (i, cb))


def _full(shape):
    return pl.BlockSpec(tuple(shape), lambda *_: (0,) * len(shape))


def _rms_fwd(x, g):
    r = lax.rsqrt(jnp.mean(x * x, axis=-1, keepdims=True) + EPS)
    return x * r * g


def _rms_bwd(x, g, dy):
    r = lax.rsqrt(jnp.mean(x * x, axis=-1, keepdims=True) + EPS)
    xh = x * r
    t = dy * g
    dx = r * (t - xh * jnp.mean(t * xh, axis=-1, keepdims=True))
    return dx, jnp.sum(dy * xh, axis=0, keepdims=True)


def _acc(i, ref, val):
    @pl.when(i == 0)
    def _():
        ref[...] = val

    @pl.when(i > 0)
    def _():
        ref[...] += val


def _sigmoid(x):
    return 0.5 * jnp.tanh(0.5 * x) + 0.5


def _pre_norm(x, g):
    S, D = x.shape
    tq = _tile(S, 512, SUBLANES)

    def fn(i, x_ref, g_ref, h_ref):
        h_ref[...] = _rms_fwd(x_ref[...], g_ref[...]).astype(BF16)

    return _rows("pre_norm", fn, S, tq, (x, g), [_rb(tq, D), _full((1, D))], _sds((S, D), BF16), _rb(tq, D))


def _post_mix(x, mix, g2, g3):
    S, D = x.shape
    tq = _tile(S, 512, SUBLANES)

    def fn(i, x_ref, m_ref, g2_ref, g3_ref, x1_ref, h2_ref):
        x1 = x_ref[...] + _rms_fwd(m_ref[...], g2_ref[...])
        x1_ref[...] = x1
        h2_ref[...] = _rms_fwd(x1, g3_ref[...]).astype(BF16)

    return _rows("post_mix", fn, S, tq, (x, mix, g2, g3), [_rb(tq, D), _rb(tq, D), _full((1, D)), _full((1, D))],
                 (_sds((S, D), F32), _sds((S, D), BF16)), (_rb(tq, D), _rb(tq, D)))


def _post_mlp(x1, ff, g4, g1_next):
    S, D = x1.shape
    tq = _tile(S, 512, SUBLANES)

    def fn(i, x_ref, f_ref, g_ref, gn_ref, o_ref, h_ref):
        x2 = x_ref[...] + _rms_fwd(f_ref[...], g_ref[...])
        o_ref[...] = x2
        h_ref[...] = _rms_fwd(x2, gn_ref[...]).astype(BF16)

    r, f = _rb(tq, D), _full((1, D))
    return _rows("post_mlp", fn, S, tq, (x1, ff, g4, g1_next), [r, r, f, f], (_sds((S, D), F32), _sds((S, D), BF16)), (r, r))


def _loss_head(y, target):
    S, D = y.shape
    tq = _tile(S, 512, SUBLANES)

    def fn(i, y_ref, t_ref, l_ref, d_ref):
        e = y_ref[...] - t_ref[...]
        d_ref[...] = e * (1.0 / D)
        part = jnp.sum(jnp.sum(e * e, axis=1, keepdims=True), axis=0, keepdims=True) * (0.5 / D)
        _acc(i, l_ref, jnp.broadcast_to(part, (1, LANES)))

    return _rows("loss_head", fn, S, tq, (y, target), [_rb(tq, D), _rb(tq, D)],
                 (_sds((1, LANES), F32), _sds((S, D), F32)), (_full((1, LANES)), _rb(tq, D)))


def _post_mlp_bwd(ff, g4, dx2):
    S, D = ff.shape
    tq = _tile(S, 512, SUBLANES)

    def fn(i, f_ref, g_ref, d_ref, o_ref, dg_ref):
        dx, dg = _rms_bwd(f_ref[...], g_ref[...], d_ref[...])
        o_ref[...] = dx.astype(BF16)
        _acc(i, dg_ref, dg)

    return _rows("post_mlp_bwd", fn, S, tq, (ff, g4, dx2), [_rb(tq, D), _full((1, D)), _rb(tq, D)],
                 (_sds((S, D), BF16), _sds((1, D), F32)), (_rb(tq, D), _full((1, D))))


def _mid_bwd(x1, g3, dh2, dx2, mix, g2):
    S, D = x1.shape
    tq = _tile(S, 256, SUBLANES)

    def fn(i, x_ref, g3_ref, dh_ref, dx2_ref, m_ref, g2_ref, dx1_ref, dm_ref, dg3_ref, dg2_ref):
        d, dg3 = _rms_bwd(x_ref[...], g3_ref[...], dh_ref[...])
        dx1 = dx2_ref[...] + d
        dx1_ref[...] = dx1
        dm, dg2 = _rms_bwd(m_ref[...], g2_ref[...], dx1)
        dm_ref[...] = dm.astype(BF16)
        _acc(i, dg3_ref, dg3)
        _acc(i, dg2_ref, dg2)

    r, f = _rb(tq, D), _full((1, D))
    return _rows("mid_bwd", fn, S, tq, (x1, g3, dh2, dx2, mix, g2), [r, f, r, r, r, f],
                 (_sds((S, D), F32), _sds((S, D), BF16), _sds((1, D), F32), _sds((1, D), F32)), (r, r, f, f))


def _pre_norm_bwd(x, g1, dh, dx1):
    S, D = x.shape
    tq = _tile(S, 512, SUBLANES)

    def fn(i, x_ref, g_ref, dh_ref, dx1_ref, dx_ref, dg_ref):
        d, dg = _rms_bwd(x_ref[...], g_ref[...], dh_ref[...])
        dx_ref[...] = dx1_ref[...] + d
        _acc(i, dg_ref, dg)

    r, f = _rb(tq, D), _full((1, D))
    return _rows("pre_norm_bwd", fn, S, tq, (x, g1, dh, dx1), [r, f, r, r], (_sds((S, D), F32), _sds((1, D), F32)), (r, f))


def _mix(proj, ya, yb, D, cb_a):
    S = ya.shape[0]
    tq = _tile(S, 256, SUBLANES)

    def fn(i, ga_ref, gb_ref, ya_ref, yb_ref, o_ref):
        ga, gb, ya, yb = (r[...].astype(F32) for r in (ga_ref, gb_ref, ya_ref, yb_ref))
        o_ref[...] = (_sigmoid(ga) * ya + _sigmoid(gb) * yb).astype(BF16)

    return _rows("mix", fn, S, tq, (proj, proj, ya, yb), [_rb(tq, D, cb_a), _rb(tq, D, cb_a + 1), _rb(tq, D), _rb(tq, D)],
                 _sds((S, D), BF16), _rb(tq, D))


def _mix_bwd(proj, ya, yb, dmixin, D, cb_a):
    S = ya.shape[0]
    tq = _tile(S, 256, SUBLANES)

    def fn(i, ga_ref, gb_ref, ya_ref, yb_ref, d_ref, dya_ref, dyb_ref, dga_ref, dgb_ref):
        ga, gb, ya, yb, d = (r[...].astype(F32) for r in (ga_ref, gb_ref, ya_ref, yb_ref, d_ref))
        sa, sb = _sigmoid(ga), _sigmoid(gb)
        dya_ref[...] = (d * sa).astype(BF16)
        dyb_ref[...] = (d * sb).astype(BF16)
        dga_ref[...] = (d * ya * sa * (1.0 - sa)).astype(BF16)
        dgb_ref[...] = (d * yb * sb * (1.0 - sb)).astype(BF16)

    r = _rb(tq, D)
    o = _sds((S, D), BF16)
    return _rows("mix_bwd", fn, S, tq, (proj, proj, ya, yb, dmixin), [_rb(tq, D, cb_a), _rb(tq, D, cb_a + 1), r, r, r],
                 (o, o, o, o), (r, r, r, r))


HALO = 16


def _shift_down(xe, k, tq):
    return pltpu.roll(xe, k, 0)[HALO:HALO + tq]


def _conv_pre(cur_ref, halo_ref, w_ref, i, tq):
    x = cur_ref[...].astype(F32)
    halo = jnp.where(i > 0, halo_ref[...].astype(F32), 0.0)
    xe = jnp.concatenate([halo, x], axis=0)
    xs = [x] + [_shift_down(xe, k, tq) for k in range(1, DN_CONV)]
    w = w_ref[...]
    c = sum(w[DN_CONV - 1 - k:DN_CONV - k, :] * xs[k] for k in range(DN_CONV))
    return c, xs


def _dn_prep(proj, conv_w, W):
    S = proj.shape[0]
    tq = _tile(S, 256, HALO)
    hb = tq // HALO

    def body(cur_ref, halo_ref, w_ref, o_ref):
        j, i = pl.program_id(0), pl.program_id(1)
        c, _ = _conv_pre(cur_ref, halo_ref, w_ref, i, tq)
        y = c * _sigmoid(c)
        scale = jnp.where(j == 0, DN_DK ** -0.5, 1.0)
        for h in range(W // DN_DK):
            sl = slice(h * DN_DK, (h + 1) * DN_DK)
            yh = y[:, sl]
            rs = lax.rsqrt(jnp.sum(yh * yh, axis=-1, keepdims=True) + EPS)
            o_ref[:, sl] = jnp.where(j == 2, yh, yh * rs * scale)

    return _pcall(body, name="dn_prep", grid=(3, S // tq),
                  in_specs=[pl.BlockSpec((tq, W), lambda j, i: (i, j)),
                            pl.BlockSpec((HALO, W), lambda j, i: (jnp.maximum(i * hb - 1, 0), j)),
                            pl.BlockSpec((DN_CONV, W), lambda j, i: (0, j))],
                  out_specs=pl.BlockSpec((tq, W), lambda j, i: (i, j)), out_shape=_sds((S, 3 * W), F32),
                  compiler_params=_cp("arbitrary", "arbitrary"))(proj, proj, conv_w)


def _dn_prep_bwd_a(proj, conv_w, dqkv, W):
    S = proj.shape[0]
    tq = _tile(S, 256, HALO)
    hb = tq // HALO

    def body(cur_ref, halo_ref, w_ref, d_ref, dc_ref, dw_ref):
        j, i = pl.program_id(0), pl.program_id(1)
        c, xs = _conv_pre(cur_ref, halo_ref, w_ref, i, tq)
        sg = _sigmoid(c)
        y = c * sg
        scale = jnp.where(j == 0, DN_DK ** -0.5, 1.0)
        dout = d_ref[0]
        dys = []
        for h in range(W // DN_DK):
            sl = slice(h * DN_DK, (h + 1) * DN_DK)
            yh, dh = y[:, sl], dout[:, sl]
            rs = lax.rsqrt(jnp.sum(yh * yh, axis=-1, keepdims=True) + EPS)
            yn = yh * rs
            dn = scale * rs * (dh - yn * jnp.sum(dh * yn, axis=-1, keepdims=True))
            dys.append(jnp.where(j == 2, dh, dn))
        dy = jnp.concatenate(dys, axis=1)
        dc = dy * (sg * (1.0 + c * (1.0 - sg)))
        dc_ref[...] = dc
        dw = jnp.concatenate([jnp.sum(dc * xs[DN_CONV - 1 - r], axis=0, keepdims=True) for r in range(DN_CONV)], axis=0)
        _acc(i, dw_ref, dw)

    return _pcall(body, name="dn_prep_bwd_a", grid=(3, S // tq),
                  in_specs=[pl.BlockSpec((tq, W), lambda j, i: (i, j)),
                            pl.BlockSpec((HALO, W), lambda j, i: (jnp.maximum(i * hb - 1, 0), j)),
                            pl.BlockSpec((DN_CONV, W), lambda j, i: (0, j)),
                            pl.BlockSpec((1, tq, W), lambda j, i: (j, i, 0))],
                  out_specs=(pl.BlockSpec((tq, W), lambda j, i: (i, j)), pl.BlockSpec((DN_CONV, W), lambda j, i: (0, j))),
                  out_shape=(_sds((S, 3 * W), F32), _sds((DN_CONV, 3 * W), F32)),
                  compiler_params=_cp("arbitrary", "arbitrary"))(proj, proj, conv_w, dqkv)


def _dn_prep_bwd_b(dc, conv_w, W):
    S = dc.shape[0]
    tq = _tile(S, 256, SUBLANES)
    hb = tq // SUBLANES
    nblk = S // tq

    def body(cur_ref, nxt_ref, w_ref, o_ref):
        i = pl.program_id(1)
        d = cur_ref[...]
        nxt = jnp.where(i < nblk - 1, nxt_ref[...], 0.0)
        de = jnp.concatenate([d, nxt], axis=0)
        w = w_ref[...]
        out = w[DN_CONV - 1:DN_CONV, :] * d
        for k in range(1, DN_CONV):
            out = out + w[DN_CONV - 1 - k:DN_CONV - k, :] * pltpu.roll(de, tq + SUBLANES - k, 0)[0:tq]
        o_ref[...] = out.astype(BF16)

    return _pcall(body, name="dn_prep_bwd_b", grid=(3, nblk),
                  in_specs=[pl.BlockSpec((tq, W), lambda j, i: (i, j)),
                            pl.BlockSpec((SUBLANES, W), lambda j, i: (jnp.minimum((i + 1) * hb, S // SUBLANES - 1), j)),
                            pl.BlockSpec((DN_CONV, W), lambda j, i: (0, j))],
                  out_specs=pl.BlockSpec((tq, W), lambda j, i: (i, j)), out_shape=_sds((S, 3 * W), BF16),
                  compiler_params=_cp("arbitrary", "arbitrary"))(dc, dc, conv_w)


def _gate_terms(ba, al, dt):
    u = ba + dt
    sp = jnp.maximum(u, 0.0) + jnp.log(1.0 + jnp.exp(-jnp.abs(u)))
    return _sigmoid(ba), -jnp.exp(al) * sp, u


def _dn_gates(ba, alog_row, dt_row, H):
    S = ba.shape[0]
    tq = _tile(S, 512, SUBLANES)
    W = H * DN_DK

    def fn(i, ba_ref, al_ref, dt_ref, be_ref, g_ref):
        bet, gg, _ = _gate_terms(ba_ref[...], al_ref[...], dt_ref[...])
        for h in range(H):
            sl = slice(h * DN_DK, (h + 1) * DN_DK)
            be_ref[:, sl] = jnp.broadcast_to(bet[:, h:h + 1], (tq, DN_DK))
            g_ref[:, sl] = jnp.broadcast_to(gg[:, H + h:H + h + 1], (tq, DN_DK))

    return _rows("dn_gates", fn, S, tq, (ba, alog_row, dt_row), [_rb(tq, LANES), _full((1, LANES)), _full((1, LANES))],
                 (_sds((S, W), F32), _sds((S, W), F32)), (_rb(tq, W), _rb(tq, W)))


def _dn_gates_bwd(ba, alog_row, dt_row, dbeta_b, dg_b, H):
    S = ba.shape[0]
    tq = _tile(S, 512, SUBLANES)
    W = H * DN_DK

    def fn(i, ba_ref, al_ref, dt_ref, db_ref, dg_ref, o_ref, dal_ref, ddt_ref):
        bet, gg, u = _gate_terms(ba_ref[...], al_ref[...], dt_ref[...])
        lane = lax.broadcasted_iota(jnp.int32, (tq, LANES), 1)
        d = jnp.zeros((tq, LANES), F32)
        for h in range(H):
            d = jnp.where(lane == h, db_ref[:, h * DN_DK:h * DN_DK + 1], d)
            d = jnp.where(lane == H + h, dg_ref[:, h * DN_DK:h * DN_DK + 1], d)
        is_a = (lane >= H) & (lane < 2 * H)
        da = jnp.where(is_a, d * (-jnp.exp(al_ref[...]) * _sigmoid(u)), 0.0)
        dlog = jnp.where(lane < H, d * bet * (1.0 - bet), da)
        o_ref[...] = jnp.concatenate([dlog, jnp.zeros((tq, BA_W - LANES), F32)], axis=1).astype(BF16)
        _acc(i, dal_ref, jnp.sum(jnp.where(is_a, d * gg, 0.0), axis=0, keepdims=True))
        _acc(i, ddt_ref, jnp.sum(da, axis=0, keepdims=True))

    f = _full((1, LANES))
    return _rows("dn_gates_bwd", fn, S, tq, (ba, alog_row, dt_row, dbeta_b, dg_b),
                 [_rb(tq, LANES), f, f, _rb(tq, W), _rb(tq, W)],
                 (_sds((S, BA_W), BF16), _sds((1, LANES), F32), _sds((1, LANES), F32)), (_rb(tq, BA_W), f, f))


def _dn_out(o, proj, ng, W, cb_z):
    S = o.shape[0]
    tq = _tile(S, 256, SUBLANES)

    def fn(i, o_ref, z_ref, g_ref, y_ref):
        for h in range(W // DN_DK):
            sl = slice(h * DN_DK, (h + 1) * DN_DK)
            z = z_ref[:, sl].astype(F32)
            y_ref[:, sl] = (_rms_fwd(o_ref[:, sl], g_ref[...]) * (z * _sigmoid(z))).astype(BF16)

    return _rows("dn_out", fn, S, tq, (o, proj, ng), [_rb(tq, W), _rb(tq, W, cb_z), _full((1, DN_DK))], _sds((S, W), BF16), _rb(tq, W))


def _dn_out_bwd(o, proj, ng, dy, W, cb_z):
    S = o.shape[0]
    tq = _tile(S, 256, SUBLANES)

    def fn(i, o_ref, z_ref, g_ref, d_ref, do_ref, dz_ref, dg_ref):
        g = g_ref[...]
        dg = jnp.zeros((1, DN_DK), F32)
        for h in range(W // DN_DK):
            sl = slice(h * DN_DK, (h + 1) * DN_DK)
            oh, z, d = o_ref[:, sl], z_ref[:, sl].astype(F32), d_ref[:, sl].astype(F32)
            sg = _sigmoid(z)
            dn = d * (z * sg)
            dz_ref[:, sl] = (d * _rms_fwd(oh, g) * (sg * (1.0 + z * (1.0 - sg)))).astype(BF16)
            dx, dgh = _rms_bwd(oh, g, dn)
            do_ref[:, sl] = dx
            dg = dg + dgh
        _acc(i, dg_ref, dg)

    r = _rb(tq, W)
    return _rows("dn_out_bwd", fn, S, tq, (o, proj, ng, dy), [r, _rb(tq, W, cb_z), _full((1, DN_DK)), r],
                 (_sds((S, W), F32), _sds((S, W), BF16), _sds((1, DN_DK), F32)), (r, r, _full((1, DN_DK))))


def _bdot(a, b, mode="nn"):
    return lax.dot_general(a.astype(BF16), b.astype(BF16), (_DIMS[mode], ((), ())), preferred_element_type=F32)


def _rsum(x):
    return jnp.broadcast_to(jnp.sum(x, axis=-1, keepdims=True), x.shape)


def _dot3(a, b, mode="nn"):
    ah, bh = a.astype(BF16), b.astype(BF16)
    al, bl = (a - ah.astype(F32)).astype(BF16), (b - bh.astype(F32)).astype(BF16)
    d = lambda x, y: lax.dot_general(x, y, (_DIMS[mode], ((), ())), preferred_element_type=F32)
    return d(ah, bh) + (d(al, bh) + d(ah, bl))


def _cumsum_rows(x, reverse=False):
    n = x.shape[0]
    row = lax.broadcasted_iota(jnp.int32, x.shape, 0)
    s = 1
    while s < n:
        if reverse:
            x = x + jnp.where(row < n - s, pltpu.roll(x, n - s, 0), 0.0)
        else:
            x = x + jnp.where(row >= s, pltpu.roll(x, s, 0), 0.0)
        s *= 2
    return x


def _each(f, *lists):
    return [f(*a) for a in zip(*lists)]


def _delta_local(qs, ks, vs, bes, grs):
    C = DN_CHUNK
    ri = lax.broadcasted_iota(jnp.int32, (C, C), 0)
    ci = lax.broadcasted_iota(jnp.int32, (C, C), 1)
    causal, strict = ri >= ci, ri > ci
    gcs = [_cumsum_rows(g) for g in grs]
    decays = [jnp.where(causal, jnp.exp(jnp.where(causal, gc[:, :C] - gc.T[:C, :], 0.0)), 0.0) for gc in gcs]
    egs = [jnp.exp(gc) for gc in gcs]
    eks = [jnp.exp(gc[C - 1:C, :] - gc) for gc in gcs]
    gams = [jnp.exp(gc[C - 1:C, :]) for gc in gcs]
    kbs = _each(lambda k, be: k * be, ks, bes)
    kks = _each(lambda kb, k: _bdot(kb, k, "nt"), kbs, ks)
    nls = _each(lambda kk, dc: jnp.where(strict, -kk * dc, 0.0), kks, decays)
    eye = (ri == ci).astype(F32)
    ts = [eye + nl for nl in nls]
    pws = [_dot3(nl, nl) for nl in nls]
    for s in range(4):
        both = _each(lambda t, pw: _dot3(jnp.concatenate([t, pw], axis=0), pw), ts, pws)
        ts = _each(lambda t, b: t + b[:C], ts, both)
        pws = [b[C:] for b in both]
    ts = _each(lambda t, pw: t + _dot3(t, pw), ts, pws)
    vbs = _each(lambda v, be: v * be, vs, bes)
    kbes = _each(lambda kb, eg: kb * eg, kbs, egs)
    uws = _each(lambda t, vb, kbe: _dot3(t, jnp.concatenate([vb, kbe], axis=1)), ts, vbs, kbes)
    us, ws = [uw[:, :DN_DK] for uw in uws], [uw[:, DN_DK:] for uw in uws]
    qks = _each(lambda q, k: _bdot(q, k, "nt"), qs, ks)
    return dict(decay=decays, eg=egs, ek=eks, gam=gams, kb=kbs, kk=kks, t=ts, vb=vbs, kbe=kbes, u=us, w=ws, qk=qks,
                a=_each(lambda qk, dc: qk * dc, qks, decays), qd=_each(lambda q, eg: q * eg, qs, egs),
                kd=_each(lambda k, ek: k * ek, ks, eks), strict=strict)


def _delta_items(refs, CB, HB):
    C, dk = DN_CHUNK, DN_DK
    return [[r[c * C:(c + 1) * C, h * dk:(h + 1) * dk] for h in range(HB) for c in range(CB)] for r in refs]


def _delta_fwd(qkv, beta_b, g_b, H, CB, HB):
    S = qkv.shape[0]
    C, dk = DN_CHUNK, DN_DK
    N = S // C
    R = CB * C
    G = H // HB

    def body(q_ref, k_ref, v_ref, b_ref, g_ref, o_ref, st_ref, s_ref):
        @pl.when(pl.program_id(1) == 0)
        def _():
            s_ref[...] = jnp.zeros((HB, dk, dk), F32)

        L = _delta_local(*_delta_items((q_ref, k_ref, v_ref, b_ref, g_ref), CB, HB))
        ss = [s_ref[h] for h in range(HB)]
        for c in range(CB):
            it = [h * CB + c for h in range(HB)]
            for h in range(HB):
                st_ref[h, c] = ss[h]
            wq = [_bdot(jnp.concatenate([L["w"][i], L["qd"][i]], axis=0), s) for i, s in zip(it, ss)]
            vns = [L["u"][i] - x[:C] for i, x in zip(it, wq)]
            outs = [x[C:] + _bdot(L["a"][i], vn) for i, x, vn in zip(it, wq, vns)]
            ss = [s * L["gam"][i] + _bdot(L["kd"][i], vn, "tn") for i, s, vn in zip(it, ss, vns)]
            for h in range(HB):
                o_ref[c * C:(c + 1) * C, h * dk:(h + 1) * dk] = outs[h]
        for h in range(HB):
            s_ref[h] = ss[h]

    blk = lambda off: pl.BlockSpec((R, HB * dk), lambda h, n: (n, off + h))
    return _pcall(body, name="delta_fwd", grid=(G, N // CB),
                  in_specs=[blk(0), blk(G), blk(2 * G), blk(0), blk(0)],
                  out_specs=(blk(0), pl.BlockSpec((HB, CB, dk, dk), lambda h, n: (h, n, 0, 0))),
                  out_shape=(_sds((S, H * dk), F32), _sds((H, N, dk, dk), F32)),
                  scratch_shapes=[pltpu.VMEM((HB, dk, dk), F32)],
                  compiler_params=_cp("arbitrary", "arbitrary"))(qkv, qkv, qkv, beta_b, g_b)


def _delta_bwd(qkv, beta_b, g_b, states, do, H, CB, HB):
    S = qkv.shape[0]
    C, dk = DN_CHUNK, DN_DK
    N = S // C
    R = CB * C
    NB = N // CB
    G = H // HB

    def body(q_ref, k_ref, v_ref, b_ref, g_ref, st_ref, do_ref, dqkv_ref, db_ref, dg_ref, ds_ref):
        @pl.when(pl.program_id(1) == 0)
        def _():
            ds_ref[...] = jnp.zeros((HB, dk, dk), F32)

        qs, ks, vs, bes, grs, dos = _delta_items((q_ref, k_ref, v_ref, b_ref, g_ref, do_ref), CB, HB)
        L = _delta_local(qs, ks, vs, bes, grs)
        ts, decays, kbs, egs, eks, gams, qds, kds = (L[n] for n in ("t", "decay", "kb", "eg", "ek", "gam", "qd", "kd"))
        s0s = [st_ref[h, c] for h in range(HB) for c in range(CB)]
        vns = _each(lambda u, w, s0: u - _bdot(w, s0), L["u"], L["w"], s0s)
        pre_dvn = _each(lambda a, d: _bdot(a, d, "tn"), L["a"], dos)
        pre_ds = _each(lambda qd, d: _bdot(qd, d, "tn"), qds, dos)
        das = _each(lambda d, vn: _bdot(d, vn, "nt"), dos, vns)
        ds = [ds_ref[h] for h in range(HB)]
        ds1s, dvns = [None] * (HB * CB), [None] * (HB * CB)
        for c in reversed(range(CB)):
            it = [h * CB + c for h in range(HB)]
            new = [pre_dvn[i] + _bdot(kds[i], d) for i, d in zip(it, ds)]
            for i, d, dv in zip(it, ds, new):
                ds1s[i], dvns[i] = d, dv
            ds = [pre_ds[i] + d * gams[i] - _bdot(L["w"][i], dv, "tn") for i, d, dv in zip(it, ds, new)]
        for h in range(HB):
            ds_ref[h] = ds[h]
        dkds = _each(lambda vn, d1: _bdot(vn, d1, "nt"), vns, ds1s)
        dgams = _each(lambda s0, d1: jnp.sum(jnp.sum(s0 * d1, axis=1, keepdims=True), axis=0, keepdims=True), s0s, ds1s)
        ost = _each(lambda d, dv, s0: _bdot(jnp.concatenate([d, dv], axis=0), s0, "nt"), dos, dvns, s0s)
        dqds, dws = [x[:C] for x in ost], [-x[C:] for x in ost]
        dvw = _each(lambda dv, dw: jnp.concatenate([dv, dw], axis=1), dvns, dws)
        tdvw = _each(lambda t, x: _dot3(t, x, "tn"), ts, dvw)
        dvbs, dkbes = [x[:, :dk] for x in tdvw], [x[:, dk:] for x in tdvw]
        dts = _each(lambda x, vb, kbe: _dot3(x, jnp.concatenate([vb, kbe], axis=1), "nt"), dvw, L["vb"], L["kbe"])
        tmp = _each(lambda dt, t: _dot3(dt, t, "nt"), dts, ts)
        dls = _each(lambda t, x: -_dot3(t, x, "tn"), ts, tmp)
        ms = _each(lambda dl, dc: jnp.where(L["strict"], dl * dc, 0.0), dls, decays)
        mas = _each(lambda da, dc: da * dc, das, decays)
        dkbs = _each(lambda m, k, dkbe, eg: _bdot(m, k) + dkbe * eg, ms, ks, dkbes, egs)
        dks = _each(lambda m, kb, ma, q, dkd, ek, dkb, be: _bdot(m, kb, "tn") + _bdot(ma, q, "tn") + dkd * ek + dkb * be,
                    ms, kbs, mas, qs, dkds, eks, dkbs, bes)
        dqs = _each(lambda ma, k, dqd, eg: _bdot(ma, k) + dqd * eg, mas, ks, dqds, egs)
        es = _each(lambda m, kk, ma, qk: m * kk + ma * qk, ms, L["kk"], mas, L["qk"])
        ones = jnp.ones((C, dk), BF16)
        row = lax.broadcasted_iota(jnp.int32, (C, dk), 0)
        for i in range(HB * CB):
            h, c = divmod(i, CB)
            rs, cs = slice(c * C, (c + 1) * C), slice(h * dk, (h + 1) * dk)
            e = es[i]
            e_hi = e.astype(BF16)
            col = _bdot(e_hi, ones, "tn") + _bdot(e - e_hi.astype(F32), ones, "tn")
            t_kd = _rsum(dkds[i] * kds[i])
            dgc = (jnp.broadcast_to(jnp.sum(e, axis=1, keepdims=True), (C, dk)) - col + _rsum(dqds[i] * qds[i]) - t_kd
                   + _rsum(dkbes[i] * L["kbe"][i]))
            dglast = jnp.sum(t_kd[:, 0:1], axis=0, keepdims=True) + dgams[i] * gams[i][:, 0:1]
            dgc = dgc + jnp.where(row == C - 1, dglast, 0.0)
            dqkv_ref[0, rs, cs] = dqs[i]
            dqkv_ref[1, rs, cs] = dks[i]
            dqkv_ref[2, rs, cs] = dvbs[i] * bes[i]
            db_ref[rs, cs] = _rsum(dkbs[i] * ks[i]) + _rsum(dvbs[i] * vs[i])
            dg_ref[rs, cs] = _cumsum_rows(dgc, reverse=True)

    blk = lambda off: pl.BlockSpec((R, HB * dk), lambda h, n: (NB - 1 - n, off + h))
    W = H * dk
    return _pcall(body, name="delta_bwd", grid=(G, NB),
                  in_specs=[blk(0), blk(G), blk(2 * G), blk(0), blk(0),
                            pl.BlockSpec((HB, CB, dk, dk), lambda h, n: (h, NB - 1 - n, 0, 0)), blk(0)],
                  out_specs=(pl.BlockSpec((3, R, HB * dk), lambda h, n: (0, NB - 1 - n, h)), blk(0), blk(0)),
                  out_shape=(_sds((3, S, W), F32), _sds((S, W), F32), _sds((S, W), F32)),
                  scratch_shapes=[pltpu.VMEM((HB, dk, dk), F32)],
                  compiler_params=_cp("arbitrary", "arbitrary"))(qkv, qkv, qkv, beta_b, g_b, states, do)


def _rope_consts():
    lane = np.arange(LANES) % SW_HD
    half = ROT_DIM // 2
    inv = (ROPE_THETA ** (-np.arange(half, dtype=np.float32) * np.float32(2.0 / ROT_DIM))).astype(np.float32)
    freq = np.where(lane < ROT_DIM, inv[lane % half], 0.0).astype(np.float32)
    lo = (lane < half).astype(np.float32)
    hi = ((lane >= half) & (lane < ROT_DIM)).astype(np.float32)
    return jnp.asarray(np.stack([freq, -lo, hi] + [np.zeros(LANES, np.float32)] * 5))


def _rope_tables(pos_col):
    S = pos_col.shape[0]
    tq = _tile(S, 1024, SUBLANES)

    def fn(i, p_ref, c_ref, cos_ref, s1_ref, s2_ref):
        ang = p_ref[...].astype(F32) * c_ref[0:1, :]
        sn = jnp.sin(ang)
        cos_ref[...] = jnp.cos(ang)
        s1_ref[...] = sn * c_ref[1:2, :]
        s2_ref[...] = sn * c_ref[2:3, :]

    o, r = _sds((S, LANES), F32), _rb(tq, LANES)
    return _rows("rope_tables", fn, S, tq, (pos_col, _rope_consts()), [_rb(tq, 1), _full((SUBLANES, LANES))], (o, o, o), (r, r, r))


def _wide(a, w):
    return a if w == LANES else jnp.tile(a, (1, w // LANES))


def _rope(x, cos, s1, s2):
    w, h = x.shape[1], ROT_DIM // 2
    return x * _wide(cos, w) + pltpu.roll(x, w - h, 1) * _wide(s1, w) + pltpu.roll(x, h, 1) * _wide(s2, w)


def _unrope(d, cos, s1, s2):
    w, h = d.shape[1], ROT_DIM // 2
    return d * _wide(cos, w) + pltpu.roll(d * _wide(s1, w), h, 1) + pltpu.roll(d * _wide(s2, w), w - h, 1)


def _swa_setup(n, q_ref, kc_ref, kp_ref, vc_ref, vp_ref, tc, tp):
    B = SW_BLOCK
    qr = _rope(q_ref[...].astype(F32), tc[0][...], tc[1][...], tc[2][...]) * (SW_HD ** -0.5)
    kw = jnp.concatenate([_rope(kp_ref[...].astype(F32), tp[0][...], tp[1][...], tp[2][...]),
                          _rope(kc_ref[...].astype(F32), tc[0][...], tc[1][...], tc[2][...])], axis=0)
    vw = jnp.concatenate([vp_ref[...], vc_ref[...]], axis=0).astype(F32)
    lane = lax.broadcasted_iota(jnp.int32, (2 * B, LANES), 1)
    heads = []
    for hk in range(SW_KV_HEADS):
        kh, vh = kw[:, hk * SW_HD:(hk + 1) * SW_HD], vw[:, hk * SW_HD:(hk + 1) * SW_HD]
        kk, vv = jnp.concatenate([kh, kh], axis=1), jnp.concatenate([vh, vh], axis=1)
        heads.append(tuple(jnp.where(sel, t, 0.0).astype(BF16) for t in (kk, vv) for sel in (lane < SW_HD, lane >= SW_HD)))
    prev = lax.broadcasted_iota(jnp.int32, (B, B), 1) > lax.broadcasted_iota(jnp.int32, (B, B), 0)
    return qr, heads, (prev, jnp.where(prev & (n == 0), -1e30, 0.0)), lane


def _fold(x, prev):
    return jnp.where(prev, x[:, :SW_BLOCK], x[:, SW_BLOCK:])


def _unfold(x, prev):
    return jnp.concatenate([jnp.where(prev, x, 0.0), jnp.where(prev, 0.0, x)], axis=1)


SWA_GROUPS = 4
SWA_GROUPS_BWD = 2


def _swa_probs(items, qs, heads, fold, sk_ref, G2):
    prev, bias = fold
    ss = [_fold(_bdot(qs[j], heads[j // G2][half], "nt"), prev) + bias for j, half in items]
    sks = [sk_ref[0:1, 2 * j + half:2 * j + half + 1] for j, half in items]
    ms = [jnp.maximum(jnp.max(s, axis=-1, keepdims=True), sk) for s, sk in zip(ss, sks)]
    ps = [jnp.exp(s - m) for s, m in zip(ss, ms)]
    es = [jnp.exp(sk - m) for sk, m in zip(sks, ms)]
    inv = [1.0 / (jnp.sum(p, axis=-1, keepdims=True) + e) for p, e in zip(ps, es)]
    return [p * i for p, i in zip(ps, inv)], [e * i for e, i in zip(es, inv)]


def _swa_specs(W, cb_q, cb_k):
    B = SW_BLOCK
    assert (W // LANES) % SWA_GROUPS == 0 and (W // LANES) % SWA_GROUPS_BWD == 0
    cur = lambda w, cb: pl.BlockSpec((B, w), lambda n: (n, cb))
    prv = lambda w, cb: pl.BlockSpec((B, w), lambda n: (jnp.maximum(n - 1, 0), cb))
    specs = [cur(W, cb_q), cur(LANES, cb_k), prv(LANES, cb_k), cur(LANES, cb_k + 1), prv(LANES, cb_k + 1)]
    return specs + [cur(LANES, 0)] * 3 + [prv(LANES, 0)] * 3 + [_full((1, LANES))]


def _swa_fwd(proj, tabs, sinks_row, W, cb_q, cb_k):
    S = proj.shape[0]
    G2 = SW_Q_HEADS // SW_KV_HEADS // 2

    def body(q_ref, kc_ref, kp_ref, vc_ref, vp_ref, c0, c1, c2, p0, p1, p2, sk_ref, o_ref):
        n = pl.program_id(0)
        qr, heads, fold, _ = _swa_setup(n, q_ref, kc_ref, kp_ref, vc_ref, vp_ref, (c0, c1, c2), (p0, p1, p2))
        qs = [qr[:, j * LANES:(j + 1) * LANES].astype(BF16) for j in range(W // LANES)]
        for j0 in range(0, W // LANES, SWA_GROUPS):
            items = [(j, half) for j in range(j0, j0 + SWA_GROUPS) for half in range(2)]
            probs, _ = _swa_probs(items, qs, heads, fold, sk_ref, G2)
            pv = [_bdot(_unfold(p, fold[0]), heads[j // G2][2 + half]) for p, (j, half) in zip(probs, items)]
            for g in range(SWA_GROUPS):
                o_ref[:, (j0 + g) * LANES:(j0 + g + 1) * LANES] = (pv[2 * g] + pv[2 * g + 1]).astype(BF16)

    t = tuple(tabs)
    return _pcall(body, name="swa_fwd", grid=(S // SW_BLOCK,), in_specs=_swa_specs(W, cb_q, cb_k),
                  out_specs=pl.BlockSpec((SW_BLOCK, W), lambda n: (n, 0)), out_shape=_sds((S, W), BF16),
                  compiler_params=_cp("arbitrary"))(proj, proj, proj, proj, proj, *t, *t, sinks_row)


def _swa_bwd(proj, tabs, sinks_row, do, W, cb_q, cb_k):
    S = proj.shape[0]
    B = SW_BLOCK
    G2 = SW_Q_HEADS // SW_KV_HEADS // 2
    SKR = -(-SW_Q_HEADS // SUBLANES) * SUBLANES

    def body(q_ref, kc_ref, kp_ref, vc_ref, vp_ref, c0, c1, c2, p0, p1, p2, sk_ref, do_ref,
             dq_ref, dkc_ref, dkp_ref, dvc_ref, dvp_ref, dsk_ref):
        n = pl.program_id(0)
        qr, heads, fold, lane = _swa_setup(n, q_ref, kc_ref, kp_ref, vc_ref, vp_ref, (c0, c1, c2), (p0, p1, p2))
        prev = fold[0]

        @pl.when(n == 0)
        def _():
            dsk_ref[...] = jnp.zeros((SKR, LANES), F32)

        acc_k = [jnp.zeros((2 * B, LANES), F32) for _ in range(SW_KV_HEADS)]
        acc_v = [jnp.zeros((2 * B, LANES), F32) for _ in range(SW_KV_HEADS)]
        qs = [qr[:, j * LANES:(j + 1) * LANES].astype(BF16) for j in range(W // LANES)]
        dos = [do_ref[:, j * LANES:(j + 1) * LANES].astype(BF16) for j in range(W // LANES)]
        dqs = []
        for j0 in range(0, W // LANES, SWA_GROUPS_BWD):
            items = [(j, half) for j in range(j0, j0 + SWA_GROUPS_BWD) for half in range(2)]
            probs, psinks = _swa_probs(items, qs, heads, fold, sk_ref, G2)
            dps = [_fold(_bdot(dos[j], heads[j // G2][2 + half], "nt"), prev) for j, half in items]
            deltas = [jnp.sum(p * dp, axis=-1, keepdims=True) for p, dp in zip(probs, dps)]
            dss = [_unfold(p * (dp - dl), prev).astype(BF16) for p, dp, dl in zip(probs, dps, deltas)]
            pbs = [_unfold(p, prev).astype(BF16) for p in probs]
            dqp = [_bdot(ds, heads[j // G2][half]) for ds, (j, half) in zip(dss, items)]
            dkk = [_bdot(ds, qs[j], "tn") for ds, (j, half) in zip(dss, items)]
            dvv = [_bdot(p, dos[j], "tn") for p, (j, half) in zip(pbs, items)]
            for i, (j, half) in enumerate(items):
                hk, h = j // G2, 2 * j + half
                sel = (lane < SW_HD) if half == 0 else (lane >= SW_HD)
                acc_k[hk] = acc_k[hk] + jnp.where(sel, dkk[i], 0.0)
                acc_v[hk] = acc_v[hk] + jnp.where(sel, dvv[i], 0.0)
                dsk_ref[h:h + 1, :] += jnp.broadcast_to(-jnp.sum(psinks[i] * deltas[i], axis=0, keepdims=True), (1, LANES))
            dqs += [dqp[2 * g] + dqp[2 * g + 1] for g in range(SWA_GROUPS_BWD)]
        dq = jnp.concatenate(dqs, axis=1) * (SW_HD ** -0.5)
        dq_ref[...] = _unrope(dq, c0[...], c1[...], c2[...]).astype(BF16)
        fold = lambda a: a[:, :SW_HD] + a[:, SW_HD:]
        dkw = jnp.concatenate([fold(a) for a in acc_k], axis=1)
        dvw = jnp.concatenate([fold(a) for a in acc_v], axis=1)
        dkp_ref[...], dkc_ref[...] = dkw[:B], dkw[B:]
        dvp_ref[...], dvc_ref[...] = dvw[:B], dvw[B:]

    t = tuple(tabs)
    blk = lambda w: pl.BlockSpec((B, w), lambda n: (n, 0))
    o = _sds((S, LANES), F32)
    return _pcall(body, name="swa_bwd", grid=(S // B,), in_specs=_swa_specs(W, cb_q, cb_k) + [blk(W)],
                  out_specs=(blk(W), blk(LANES), blk(LANES), blk(LANES), blk(LANES), _full((SKR, LANES))),
                  out_shape=(_sds((S, W), BF16), o, o, o, o, _sds((SKR, LANES), F32)),
                  compiler_params=_cp("arbitrary"))(proj, proj, proj, proj, proj, *t, *t, sinks_row, do)


def _swa_kv_combine(dkc, dkp, dvc, dvp, tabs):
    S = dkc.shape[0]
    B = SW_BLOCK
    nb = S // B

    def fn(n, kc_ref, kp_ref, vc_ref, vp_ref, c0, c1, c2, o_ref):
        more = n < nb - 1
        dk = kc_ref[...] + jnp.where(more, kp_ref[...], 0.0)
        dv = vc_ref[...] + jnp.where(more, vp_ref[...], 0.0)
        o_ref[...] = jnp.concatenate([_unrope(dk, c0[...], c1[...], c2[...]), dv], axis=1).astype(BF16)

    cur = _rb(B, LANES)
    nxt = pl.BlockSpec((B, LANES), lambda n: (jnp.minimum(n + 1, nb - 1), 0))
    return _rows("swa_kv_combine", fn, S, B, (dkc, dkp, dvc, dvp, *tabs), [cur, nxt, cur, nxt, cur, cur, cur],
                 _sds((S, 2 * LANES), BF16), _rb(B, 2 * LANES))


ANY = pl.BlockSpec(memory_space=pl.ANY)


def _place():
    x, y, c = lax.axis_index("x"), lax.axis_index("y"), lax.axis_index("c")
    return x, y, c, [(1 - x, y), (x, 1 - y), (1 - x, 1 - y)]


def _comm_call(name, body, out_shapes, n_sems, n_local, *ins):
    return _pcall(body, name=name, out_shape=tuple(out_shapes), in_specs=[ANY] * len(ins), out_specs=tuple(ANY for _ in out_shapes),
                  scratch_shapes=[pltpu.SemaphoreType.DMA((n_sems,)), pltpu.SemaphoreType.DMA((n_sems,)),
                                  pltpu.SemaphoreType.DMA((n_local,))])(*ins)


def _remote(src, dst, send, recv, k, to):
    return pltpu.make_async_remote_copy(src_ref=src, dst_ref=dst, send_sem=send.at[k], recv_sem=recv.at[k], device_id=to,
                                        device_id_type=MESH)


def _chip_slice(ref, axis, s):
    if axis is None:
        return ref.at[s]
    q = ref.shape[axis] // 4
    start = s * q if isinstance(s, int) else pl.multiple_of(s * q, q)
    return ref.at[tuple([slice(None)] * axis + [pl.ds(start, q)])]


def _own_part(a, axis, me):
    if axis is None:
        return lax.dynamic_index_in_dim(a, me, 0, keepdims=False)
    q = a.shape[axis] // 4
    return lax.dynamic_slice_in_dim(a, me * q, q, axis)


HBM = pl.BlockSpec(memory_space=pltpu.HBM)
SEM = pl.BlockSpec(memory_space=pltpu.SEMAPHORE)
EFFECT = pltpu.SideEffectType.DATAFLOW_SIDE_EFFECTING


def _split_start(name, arrs, land_shapes, plan, nc, after=None):
    n = len(arrs)
    lands = [lax.empty(s.shape, s.dtype) for s in land_shapes]
    ins = list(arrs) + lands + ([] if after is None else [after])

    def body(*refs):
        outs = refs[len(ins):]
        for k, (src, dst, _, peer) in enumerate(plan(refs[:n], refs[n:n + len(lands)])):
            pltpu.make_async_remote_copy(src_ref=src, dst_ref=dst, send_sem=outs[k], recv_sem=outs[nc + k], device_id=peer,
                                         device_id_type=MESH).start()
        outs[-1][...] = jnp.zeros((SUBLANES, LANES), F32)

    nt = n + len(lands)
    thru = [pltpu.HBM(a.shape, a.dtype) for a in list(arrs) + lands]
    outs = _pcall(body, name=name, out_shape=tuple([pltpu.SemaphoreType.DMA(())] * (2 * nc) + thru + [_sds((SUBLANES, LANES), F32)]),
                  in_specs=[HBM] * nt + [ANY] * (len(ins) - nt),
                  out_specs=tuple([SEM] * (2 * nc) + [HBM] * nt + [pl.BlockSpec(memory_space=pltpu.VMEM)]),
                  input_output_aliases={i: 2 * nc + i for i in range(nt)},
                  compiler_params=pltpu.CompilerParams(has_side_effects=EFFECT))(
        *[pltpu.with_memory_space_constraint(a, pltpu.HBM) for a in ins[:nt]], *ins[nt:])
    return dict(sems=outs[:2 * nc], arrs=outs[2 * nc:2 * nc + n], lands=outs[2 * nc + n:2 * nc + nt], token=outs[-1], plan=plan, nc=nc)


def _split_wait(name, handle, after):
    arrs, lands, sems, nc = list(handle["arrs"]), list(handle["lands"]), list(handle["sems"]), handle["nc"]
    n, nt = len(arrs), len(arrs) + len(lands)

    def body(*refs):
        sem = refs[nt:nt + 2 * nc]
        for k, (src, _, landing, peer) in enumerate(handle["plan"](refs[:n], refs[n:nt])):
            cp = pltpu.make_async_remote_copy(src_ref=src, dst_ref=landing, send_sem=sem[k], recv_sem=sem[nc + k], device_id=peer,
                                              device_id_type=MESH)
            cp.wait_send()
            cp.wait_recv()

    thru = tuple(pltpu.HBM(a.shape, a.dtype) for a in arrs + lands)
    outs = _pcall(body, name=name, out_shape=thru, in_specs=[HBM] * nt + [SEM] * (2 * nc) + [ANY], out_specs=tuple([HBM] * nt),
                  input_output_aliases={i: i for i in range(nt)},
                  compiler_params=pltpu.CompilerParams(has_side_effects=EFFECT))(*arrs, *lands, *sems, after)
    return list(outs[:n]), list(outs[n:])


WHOLE = "whole"


def _plan_chips(axes):
    def plan(src, land):
        x, y, c, chips = _place()
        idx = [2 * cx + cy for cx, cy in chips]
        part = lambda a, s: src[a] if axes[a] is WHOLE else _chip_slice(src[a], axes[a], s)
        return [(part(a, idx[j]), land[a].at[2 * x + y], land[a].at[idx[j]], (*chips[j], c))
                for a in range(len(land)) for j in range(3)]
    return plan


def _plan_sibling(half):
    def plan(src, land):
        x, y, c, _ = _place()
        lh = lambda a: src[a].shape[0] // 2
        return [(src[a].at[pl.ds((1 - c) * lh(a), lh(a))] if half else src[a], land[a], land[a], (x, y, 1 - c))
                for a in range(len(land))]
    return plan


def _chips_start(name, arrs, axes, after=None):
    part = lambda a, ax: a.shape if ax is WHOLE else a.shape[1:] if ax is None else tuple(d // 4 if i == ax else d for i, d in enumerate(a.shape))
    return _split_start(name, arrs, [_sds((4,) + part(a, ax), a.dtype) for a, ax in zip(arrs, axes)], _plan_chips(axes), 3 * len(arrs), after)


def _sibling_start(name, arrs, half, after=None):
    shp = lambda a: (a.shape[0] // 2,) + a.shape[1:] if half else a.shape
    return _split_start(name, arrs, [_sds(shp(a), a.dtype) for a in arrs], _plan_sibling(half), len(arrs), after)


def _gather_all(name, b):
    R, C = b.shape
    flips = [(dx, dy, dc) for dx in (0, 1) for dy in (0, 1) for dc in (0, 1)][1:]

    def body(b_ref, o_ref, send, recv, lsem):
        x, y, c, _ = _place()
        me = 4 * x + 2 * y + c
        peers = [(x ^ dx, y ^ dy, c ^ dc) for dx, dy, dc in flips]
        mine = pltpu.make_async_copy(b_ref, o_ref.at[me], lsem.at[0])
        mine.start()
        cps = [_remote(b_ref, o_ref.at[me], send, recv, k, peer) for k, peer in enumerate(peers)]
        for cp in cps:
            cp.start()
        for k, (px, py, pc) in enumerate(peers):
            _remote(b_ref, o_ref.at[4 * px + 2 * py + pc], send, recv, k, (px, py, pc)).wait_recv()
        for cp in cps:
            cp.wait_send()
        mine.wait()

    return _comm_call(name, body, [_sds((8, R, C), b.dtype)], 7, 1, b)[0]


def _block_rows(rows, width):
    return _tile(rows, max(SUBLANES, (1 << 19) // width), SUBLANES)


def _add_half(name, g, got):
    L, A, B = g.shape
    Lh = L // 2
    tq = _block_rows(A, B)

    def body(c_ref, g_ref, r_ref, o_ref):
        o_ref[...] = (g_ref[...] + r_ref[...]).astype(BF16)

    spec = pltpu.PrefetchScalarGridSpec(
        num_scalar_prefetch=1, grid=(Lh, A // tq),
        in_specs=[pl.BlockSpec((1, tq, B), lambda l, i, c_ref: (c_ref[0] * Lh + l, i, 0)),
                  pl.BlockSpec((1, tq, B), lambda l, i, c_ref: (l, i, 0))],
        out_specs=pl.BlockSpec((1, tq, B), lambda l, i, c_ref: (l, i, 0)))
    return _pcall(body, name=name, grid_spec=spec, out_shape=_sds((Lh, A, B), BF16),
                  compiler_params=_cp("arbitrary", "arbitrary"))(lax.axis_index("c").reshape(1).astype(jnp.int32), g, got)


def _sum_slots(name, a):
    n, R, C = a.shape
    tq = _block_rows(R, n * C)

    def fn(i, a_ref, o_ref):
        t = a_ref[0].astype(F32)
        for s in range(1, n):
            t = t + a_ref[s].astype(F32)
        o_ref[...] = t

    return _rows(name, fn, R, tq, (a,), [pl.BlockSpec((n, tq, C), lambda i: (0, i, 0))], _sds((R, C), F32), _rb(tq, C))


def _adam_update(w, g, m, v):
    mn = ADAM_B1 * m + (1.0 - ADAM_B1) * g
    vn = ADAM_B2 * v + (1.0 - ADAM_B2) * (g * g)
    m_hat = mn / (1.0 - ADAM_B1 ** ADAM_STEP)
    v_hat = vn / (1.0 - ADAM_B2 ** ADAM_STEP)
    return -ADAM_LR * (m_hat / (jnp.sqrt(v_hat) + ADAM_EPS) + ADAM_WD * w), mn, vn


def _adamw(name, w, g, m, v):
    R, C = w.shape
    tq = _tile(R, 256, SUBLANES)

    def fn(i, w_ref, g_ref, m_ref, v_ref, d_ref, mo_ref, vo_ref):
        d_ref[...], mo_ref[...], vo_ref[...] = _adam_update(w_ref[...], g_ref[...], m_ref[...], v_ref[...])

    r, o = _rb(tq, C), _sds((R, C), F32)
    return _rows(name, fn, R, tq, (w, g, m, v), [r, r, r, r], (o, o, o), (r, r, r))


def _adamw_halves(name, w, mine, theirs, m, v, l0, prev=None):
    L, A, B = w.shape
    Lh = mine.shape[0]
    tq = _tile(A, 256, SUBLANES)

    def body(c_ref, w_ref, a_ref, b_ref, m_ref, v_ref, *refs):
        g_ref, d_ref, mo_ref, vo_ref = refs[-4:]
        is_mine = pl.program_id(0) // Lh == c_ref[0]
        g = jnp.where(is_mine, a_ref[...], b_ref[...])
        g_ref[...] = g
        d_ref[...], mo_ref[...], vo_ref[...] = _adam_update(w_ref[...], g, m_ref[...], v_ref[...])

    full = pl.BlockSpec((1, tq, B), lambda l, i, c_ref: (l0 + l, i, 0))
    half = pl.BlockSpec((1, tq, B), lambda l, i, c_ref: (l % Lh, i, 0))
    o = _sds((L, A, B), F32)
    prev = list(prev or ())
    spec = pltpu.PrefetchScalarGridSpec(num_scalar_prefetch=1, grid=(2 * Lh, A // tq), in_specs=[full, half, half, full, full] + [ANY] * len(prev),
                                        out_specs=(full, full, full, full))
    return _pcall(body, name=name, grid_spec=spec, out_shape=(o, o, o, o), input_output_aliases={6 + i: i for i in range(len(prev))},
                  compiler_params=_cp("arbitrary", "arbitrary"))(lax.axis_index("c").reshape(1).astype(jnp.int32), w, mine, theirs, m, v, *prev)


def _pack(arrs, width, lead=()):
    nl = len(lead)
    flat = jnp.concatenate([a.reshape(lead + (-1,)) for a in arrs], axis=nl)
    n = flat.shape[-1]
    unit = PACK_ROWS * width
    tot = -(-n // unit) * unit
    flat = jnp.pad(flat, [(0, 0)] * nl + [(0, tot - n)])
    return flat.reshape(lead + (tot // width, width))


def _unpack(buf, shapes, lead=()):
    flat = buf.reshape(lead + (-1,))
    out, off = [], 0
    for s in shapes:
        n = int(np.prod(s))
        out.append(flat[..., off:off + n].reshape(lead + tuple(s)))
        off += n
    return out


def _in_groups(W, H):
    o_sq = 4 * W + 2 * H
    o_k = o_sq + W
    o_g = o_k + 2 * KV_W
    return [(0, 4 * W), (o_sq, o_k), (o_g, o_g + 2 * W), (o_k, o_g), (4 * W, o_sq)]


def _relayout_in(shards, W, H):
    c4 = sum(hi - lo for lo, hi in _in_groups(W, H)) // 4
    parts = []
    for lo, hi in _in_groups(W, H):
        for s in range(4):
            a, b = max(lo, s * c4), min(hi, (s + 1) * c4)
            if a < b:
                parts.append(shards[s][:, a - s * c4:b - s * c4])
    parts.append(jnp.zeros((shards.shape[1], BA_W - 2 * H), shards.dtype))
    return jnp.concatenate(parts, axis=1)


def _shard_in(d, W, H):
    groups = _in_groups(W, H)
    starts = [sum(hi - lo for lo, hi in groups[:i]) for i in range(len(groups))]
    stored = sorted(zip(groups, starts))
    c4 = sum(hi - lo for lo, hi in groups) // 4
    out = []
    for s in range(4):
        parts = []
        for (lo, hi), at in stored:
            a, b = max(lo, s * c4), min(hi, (s + 1) * c4)
            if a < b:
                parts.append(d[:, :, at + a - lo:at + b - lo])
        out.append(jnp.concatenate(parts, axis=2))
    return jnp.stack(out)


def _lane_row(vals, at):
    return jnp.pad(vals, (at, LANES - at - vals.shape[0]))[None]


def _layer_fwd(x, lw, tabs, W, H, more=None, h=None, g1_next=None):
    D = x.shape[1]
    cbk = 7 * W // LANES
    if h is None:
        h = _pre_norm(x, lw["g1"])
    proj = _mm("mm_in", h, lw["win"], "nn", BF16, tm=4096, tn=512)
    ba = _mm("mm_ba", h, lw["win"][:, 7 * W + 2 * KV_W:], "nn", F32)
    qkv = _dn_prep(proj, lw["conv"], W)
    beta_b, g_b = _dn_gates(ba, lw["alog"], lw["dt"], H)
    o, st = _delta_fwd(qkv, beta_b, g_b, H, DELTA_CB, DELTA_HB)
    oa = _dn_out(o, proj, lw["ng"], W, 3)
    ob = _swa_fwd(proj, tabs, lw["sinks"], W, 4, cbk)
    if more is not None:
        lw.update(more(ob))
    ya = _mm("mm_up_dn", oa, lw["wup_dn"], "nn", BF16)
    yb = _mm("mm_up_sw", ob, lw["wup_sw"], "nn", BF16)
    mixin = _mix(proj, ya, yb, D, 5)
    mix = _mm("mm_o", mixin, lw["wo"], "nn", F32)
    x1, h2 = _post_mix(x, mix, lw["g2"], lw["g3"])
    f1, act = _mm("mm_ff1", h2, lw["wff1"], "nn", tn=512, out_dtypes=(BF16, BF16), epi=lambda acc: (acc, jnp.square(jnp.maximum(acc, 0.0))))
    ff = _mm("mm_ff2", act, lw["wff2"], "nn", F32)
    x2, h_next = _post_mlp(x1, ff, lw["g4"], lw["g4"] if g1_next is None else g1_next)
    saved = dict(x=x, h=h, proj=proj, ba=ba, qkv=qkv, beta_b=beta_b, g_b=g_b, o=o, st=st, oa=oa, ob=ob, ya=ya, yb=yb,
                 mixin=mixin, mix=mix, x1=x1, h2=h2, f1=f1, act=act, ff=ff)
    return x2, h_next, saved


def _layer_bwd(dx2, lw, sv, tabs, W, H, l, big, weights_done=None):
    D = dx2.shape[1]
    cbk = 7 * W // LANES
    big = dict(big)
    dff, dg4 = _post_mlp_bwd(sv["ff"], lw["g4"], dx2)
    df1 = _mm("mm_ff2_dx", dff, lw["wff2"], "nt", BF16, extras=(sv["f1"],),
              epi=lambda acc, f1: (acc * 2.0 * jnp.maximum(f1.astype(F32), 0.0),))
    big["w_ff2"] = _mm("mm_ff2_dw", sv["act"], dff, "tn", slab=(big["w_ff2"], l))
    dh2 = _mm("mm_ff1_dx", df1, lw["wff1"], "nt", F32)
    big["w_ff1"] = _mm("mm_ff1_dw", sv["h2"], df1, "tn", tk=2048, slab=(big["w_ff1"], l))
    dx1, dmix, dg3, dg2 = _mid_bwd(sv["x1"], lw["g3"], dh2, dx2, sv["mix"], lw["g2"])
    dmixin = _mm("mm_o_dx", dmix, lw["wo"], "nt", BF16)
    big["w_o"] = _mm("mm_o_dw", sv["mixin"], dmix, "tn", slab=(big["w_o"], l))
    dya, dyb, dga, dgb = _mix_bwd(sv["proj"], sv["ya"], sv["yb"], dmixin, D, 5)
    doa = _mm("mm_up_dn_dx", dya, lw["wup_dn"], "nt", BF16)
    big["w_up_dn"] = _mm("mm_up_dn_dw", sv["oa"], dya, "tn", slab=(big["w_up_dn"], l))
    dob = _mm("mm_up_sw_dx", dyb, lw["wup_sw"], "nt", BF16)
    big["w_up_sw"] = _mm("mm_up_sw_dw", sv["ob"], dyb, "tn", slab=(big["w_up_sw"], l))
    do, dz, dng = _dn_out_bwd(sv["o"], sv["proj"], lw["ng"], doa, W, 3)
    dqkvn, dbeta_b, dg_b = _delta_bwd(sv["qkv"], sv["beta_b"], sv["g_b"], sv["st"], do, H, DELTA_CB, DELTA_HB)
    dba, dalog, ddt = _dn_gates_bwd(sv["ba"], lw["alog"], lw["dt"], dbeta_b, dg_b, H)
    dc, dconv = _dn_prep_bwd_a(sv["proj"], lw["conv"], dqkvn, W)
    dqkv = _dn_prep_bwd_b(dc, lw["conv"], W)
    dq_sw, dkc, dkp, dvc, dvp, dsk = _swa_bwd(sv["proj"], tabs, lw["sinks"], dob, W, 4, cbk)
    dkv = _swa_kv_combine(dkc, dkp, dvc, dvp, tabs)
    dproj = jnp.concatenate([dqkv, dz, dq_sw, dga, dgb, dkv, dba], axis=1)
    big["w_in"] = _mm("mm_in_dw", sv["h"], dproj, "tn", tn=768, tk=2048, slab=(big["w_in"], l))
    win = lw["win"]
    if weights_done is not None:
        win = win + weights_done(big).astype(BF16)
    dh = _mm("mm_in_dx", dproj, win, "nt", F32, tk=768)
    dx, dg1 = _pre_norm_bwd(sv["x"], lw["g1"], dh, dx1)
    grads = dict(pre_mix_g=dg1[0], dn_conv_w=dconv, dn_a_log=dalog[0, H:2 * H], dn_dt_bias=ddt[0, H:2 * H], dn_norm_g=dng[0],
                 sw_sinks=dsk[:SW_Q_HEADS, 0], post_mix_g=dg2[0], pre_mlp_g=dg3[0], post_mlp_g=dg4[0])
    return dx, grads, big


_WEIGHTS = ["pre_mix_g", "w_in", "dn_conv_w", "dn_a_log", "dn_dt_bias", "dn_norm_g", "sw_sinks", "w_up_dn", "w_up_sw", "w_o",
            "post_mix_g", "pre_mlp_g", "w_ff1", "w_ff2", "post_mlp_g"]
_BIG = {"w_in": 2, "w_up_dn": 1, "w_up_sw": 1, "w_o": 1, "w_ff1": 2, "w_ff2": 1}
_SMALL = [n for n in _WEIGHTS if n not in _BIG]


def _step(P):
    x, target = P["x"][0], P["loss_target"][0]
    S, D = x.shape
    L = P["pre_mix_g"].shape[0]
    H, W = DN_HEADS, DN_HEADS * DN_DK
    assert W == D == SW_Q_HEADS * SW_HD and KV_W == LANES
    me = 2 * lax.axis_index("x") + lax.axis_index("y")

    assert L % 4 == 0
    names = list(_BIG) + ["dn_conv_w"]
    local = [P[n].astype(BF16) for n in _BIG] + [P["dn_conv_w"]]
    early_names = ("w_in", "dn_conv_w")
    tail_names = [n for n in names if n not in early_names]
    own_slot = lambda gathered, mine: [lax.dynamic_update_slice_in_dim(g, w[None], me, 0) for g, w in zip(gathered, mine)]
    gather = lambda name, arrs, after=None: _chips_start(name, arrs, [WHOLE] * len(arrs), after)
    arrived = lambda name, h, after, keys: dict(zip(keys, own_slot(*reversed(_split_wait(name, h, after)))))
    h_first = gather("weights_first_start", [a[:1] for n, a in zip(names, local) if n in early_names])
    early = arrived("weights_first_wait", h_first, x, early_names)
    h_tail = gather("weights_tail_start", [a[:1] for n, a in zip(names, local) if n in tail_names], early["w_in"])
    h_next = gather("weights_next_start", [a[1:2] for a in local], h_tail["token"])
    h_rest = gather("weights_rest_start", [a[2:] for a in local], h_next["token"])

    def head(full, l, k):
        return dict(
            g1=P["pre_mix_g"][l][None], win=_relayout_in(full["w_in"][:, k], W, H),
            conv=jnp.concatenate([full["dn_conv_w"][s, k] for s in range(4)], axis=-1),
            alog=_lane_row(P["dn_a_log"][l], H), dt=_lane_row(P["dn_dt_bias"][l], H), ng=P["dn_norm_g"][l][None],
            sinks=_lane_row(P["sw_sinks"][l], 0), g2=P["post_mix_g"][l][None], g3=P["pre_mlp_g"][l][None], g4=P["post_mlp_g"][l][None])

    def tail(full, k):
        rows = lambda n: full[n][:, k].reshape(-1, full[n].shape[-1])
        return dict(wup_dn=rows("w_up_dn"), wup_sw=rows("w_up_sw"), wo=rows("w_o"), wff2=rows("w_ff2"),
                    wff1=jnp.concatenate([full["w_ff1"][s, k] for s in range(4)], axis=-1))

    tabs = _rope_tables(P["positions"].reshape(S, 1))
    lws = [head(early, 0, 0)]
    lws[0]["g1"] = lws[0]["g1"] + h_rest["token"][0, 0]

    saved, h = [], None
    for l in range(L):
        if l == 1:
            late = arrived("weights_next_wait", h_next, x, names)
            lws.append({**head(late, 1, 0), **tail(late, 0)})
        if l == 2:
            late = arrived("weights_rest_wait", h_rest, x, names)
            lws.extend({**head(late, k + 2, k), **tail(late, k)} for k in range(L - 2))
        first_tail = lambda after: tail(arrived("weights_tail_wait", h_tail, after, tail_names), 0)
        g1_next = P["pre_mix_g"][l + 1][None] if l + 1 < L else None
        x, h, sv = _layer_fwd(x, lws[l], tabs, W, H, first_tail if l == 0 else None, h, g1_next)
        saved.append(sv)
    loss_row, dx = _loss_head(x, target)

    Lb = L // 2
    layer_grads = [None] * L
    F = 4 * P["w_ff1"].shape[2]
    per_layer = dict(w_in=(D, 7 * W + 2 * KV_W + BA_W), w_up_dn=(W, D), w_up_sw=(W, D), w_o=(D, D), w_ff1=(D, F), w_ff2=(F, D))
    batch = [{n: lax.empty((Lb,) + per_layer[n], F32) for n in _BIG} for _ in range(2)]
    axes = [None if n == "w_in" else ax for n, ax in _BIG.items()]

    def pair_sums(tag, h_swap, after):
        g, got = _split_wait("grad_swap_wait_" + tag, h_swap, after)
        part = {n: _add_half("grad_pair_add_%s_%s" % (tag, n), a, r) for n, a, r in zip(_BIG, g, got)}
        return [_shard_in(part[n], W, H) if n == "w_in" else part[n] for n in _BIG]

    def chip_sums(tag, h_scat, after):
        parts, slots = _split_wait("grad_scatter_wait_" + tag, h_scat, after)
        halves = []
        for n, s, a, ax in zip(_BIG, slots, parts, axes):
            s = lax.dynamic_update_slice_in_dim(s, _own_part(a, ax, me)[None], me, 0)
            halves.append(_sum_slots("grad_chip_sum_%s_%s" % (tag, n), s.reshape(4, -1, s.shape[-1])).reshape(s.shape[1:]))
        h = _sibling_start("grad_share_start_" + tag, halves, False)
        return _split_wait("grad_share_wait_" + tag, h, halves[0])

    swaps = {}

    def swap_start(tag):
        def hook(big):
            swaps[tag] = _sibling_start("grad_swap_start_" + tag, [big[n] for n in _BIG], True)
            return swaps[tag]["token"][0, 0]
        return hook

    for l in reversed(range(L)):
        hook = swap_start("hi") if l == Lb else swap_start("lo") if l == 0 else None
        dx, layer_grads[l], batch[l // Lb] = _layer_bwd(dx, lws[l], saved[l], tabs, W, H, l % Lb, batch[l // Lb], hook)
        if l == Lb - 1:
            h_scat_hi = _chips_start("grad_scatter_start_hi", pair_sums("hi", swaps["hi"], dx), axes)
            if l > 0:
                lws[l - 1]["g4"] = lws[l - 1]["g4"] + h_scat_hi["token"][0, 0]

    grads = {n: jnp.stack([layer_grads[l][n] for l in range(L)]) for n in _SMALL}
    small_shapes = [(1,)] + [grads[n].shape for n in _SMALL]
    slots = _gather_all("small_gather", _pack([loss_row[0, :1]] + [grads[n] for n in _SMALL], LANES))
    h_scat_lo = _chips_start("grad_scatter_start_lo", pair_sums("lo", swaps["lo"], slots), axes)
    tot = _sum_slots("small_sum", slots + h_scat_lo["token"][0, 0])
    small = _unpack(tot, small_shapes)
    loss = small[0][0]
    gsum, delta, new_m, new_v = dict(zip(_SMALL, small[1:])), {}, {}, {}
    cw = P["dn_conv_w"].shape[2]
    gsum["dn_conv_w"] = lax.dynamic_slice_in_dim(gsum["dn_conv_w"], me * cw, cw, axis=2)
    sm_shapes = [P[n].shape for n in _SMALL]
    outs = _adamw("adamw_small", *(_pack([src[pre + n] for n in _SMALL], LANES)
                                   for src, pre in ((P, ""), (gsum, ""), (P, "m_"), (P, "v_"))))
    for d, o in zip((delta, new_m, new_v), outs):
        d.update(zip(_SMALL, _unpack(o, sm_shapes)))

    upper = {n: _adamw_halves("adamw_hi_" + n, P[n], mine, their, P["m_" + n], P["v_" + n], Lb)
             for n, mine, their in zip(_BIG, *chip_sums("hi", h_scat_hi, outs[0]))}
    for n, mine, their in zip(_BIG, *chip_sums("lo", h_scat_lo, upper["w_in"][0])):
        gsum[n], delta[n], new_m[n], new_v[n] = _adamw_halves("adamw_lo_" + n, P[n], mine, their, P["m_" + n], P["v_" + n], 0, upper[n])

    return (loss, dx[None], *[gsum[n] for n in _WEIGHTS], *[delta[n] for n in _WEIGHTS],
            *[new_m[n] for n in _WEIGHTS], *[new_v[n] for n in _WEIGHTS])


def kernel(x, positions, pre_mix_g, w_in, dn_conv_w, dn_a_log, dn_dt_bias, dn_norm_g, sw_sinks, w_up_dn, w_up_sw, w_o, post_mix_g, pre_mlp_g, w_ff1, w_ff2, post_mlp_g, loss_target, m_pre_mix_g, m_w_in, m_dn_conv_w, m_dn_a_log, m_dn_dt_bias, m_dn_norm_g, m_sw_sinks, m_w_up_dn, m_w_up_sw, m_w_o, m_post_mix_g, m_pre_mlp_g, m_w_ff1, m_w_ff2, m_post_mlp_g, v_pre_mix_g, v_w_in, v_dn_conv_w, v_dn_a_log, v_dn_dt_bias, v_dn_norm_g, v_sw_sinks, v_w_up_dn, v_w_up_sw, v_w_o, v_post_mix_g, v_pre_mlp_g, v_w_ff1, v_w_ff2, v_post_mlp_g):
    vals = (x, positions, pre_mix_g, w_in, dn_conv_w, dn_a_log, dn_dt_bias, dn_norm_g, sw_sinks, w_up_dn, w_up_sw, w_o, post_mix_g, pre_mlp_g, w_ff1, w_ff2, post_mlp_g, loss_target, m_pre_mix_g, m_w_in, m_dn_conv_w, m_dn_a_log, m_dn_dt_bias, m_dn_norm_g, m_sw_sinks, m_w_up_dn, m_w_up_sw, m_w_o, m_post_mix_g, m_pre_mlp_g, m_w_ff1, m_w_ff2, m_post_mlp_g, v_pre_mix_g, v_w_in, v_dn_conv_w, v_dn_a_log, v_dn_dt_bias, v_dn_norm_g, v_sw_sinks, v_w_up_dn, v_w_up_sw, v_w_o, v_post_mix_g, v_pre_mlp_g, v_w_ff1, v_w_ff2, v_post_mlp_g)
    names = ["x", "positions"] + _WEIGHTS + ["loss_target"] + ["m_" + n for n in _WEIGHTS] + ["v_" + n for n in _WEIGHTS]
    return _step(dict(zip(names, vals)))
```

```python
import numpy as np
import jax
import jax.numpy as jnp
from jax import lax
from jax.experimental import pallas as pl
from jax.experimental.pallas import tpu as pltpu

F32, BF16 = jnp.float32, jnp.bfloat16
MESH = pl.DeviceIdType.MESH

DN_HEADS = 8
DN_DK = 128
DN_CONV = 4
DN_CHUNK = 64
SW_Q_HEADS = 16
SW_KV_HEADS = 2
SW_HD = 64
SW_BLOCK = 128
ROPE_THETA = 500000.0
ROT_DIM = SW_HD // 4
EPS = 1e-6
ADAM_LR, ADAM_B1, ADAM_B2, ADAM_EPS, ADAM_WD, ADAM_STEP = 0.001, 0.9, 0.999, 1e-08, 0.01, 10

LANES = 128
SUBLANES = 8
VMEM_LIMIT = 48 * 1024 * 1024
KV_W = SW_KV_HEADS * SW_HD
BA_W = 256
PACK_ROWS = 512
DELTA_CB = 4
DELTA_HB = 8


def _pcall(body, **kw):
    return pl.pallas_call(body, **kw)


def _cp(*sem):
    return pltpu.CompilerParams(dimension_semantics=sem, vmem_limit_bytes=VMEM_LIMIT)


def _tile(n, pref, unit=LANES):
    if n <= pref:
        return n
    t = (pref // unit) * unit
    while t > unit and n % t:
        t -= unit
    assert n % t == 0, (n, pref)
    return t


def _sds(shape, dtype):
    return jax.ShapeDtypeStruct(tuple(shape), dtype)


_DIMS = {"nn": ((1,), (0,)), "nt": ((1,), (1,)), "tn": ((0,), (0,))}


def _mm(name, a, b, mode, out_dtype=F32, tm=2048, tn=1024, tk=1024, extras=(), epi=None, out_dtypes=None, slab=None):
    if mode == "nn":
        (M, K), (_, N) = a.shape, b.shape
    elif mode == "nt":
        (M, K), (N, _) = a.shape, b.shape
    else:
        (K, M), (_, N) = a.shape, b.shape
    tm, tn, tk = _tile(M, tm), _tile(N, tn), _tile(K, tk)
    nk = K // tk
    a_spec = {"nn": pl.BlockSpec((tm, tk), lambda i, j, k: (i, k)),
              "nt": pl.BlockSpec((tm, tk), lambda i, j, k: (i, k)),
              "tn": pl.BlockSpec((tk, tm), lambda i, j, k: (k, i))}[mode]
    b_spec = {"nn": pl.BlockSpec((tk, tn), lambda i, j, k: (k, j)),
              "nt": pl.BlockSpec((tn, tk), lambda i, j, k: (j, k)),
              "tn": pl.BlockSpec((tk, tn), lambda i, j, k: (k, j))}[mode]
    dims = (_DIMS[mode], ((), ()))
    out_dtypes = tuple(out_dtypes or (out_dtype,))
    ne, no = len(extras), len(out_dtypes)
    o_spec = pl.BlockSpec((tm, tn), lambda i, j, k: (i, j))

    def body(*refs):
        a_ref, b_ref, ex = refs[0], refs[1], refs[2:2 + ne]
        outs = refs[-no:] if nk == 1 else refs[-1 - no:-1]
        part = lax.dot_general(a_ref[...], b_ref[...], dims, preferred_element_type=F32)

        def finish(acc):
            res = epi(acc, *[e[...] for e in ex]) if epi else (acc,)
            for o, r, dt in zip(outs, res, out_dtypes):
                if slab is None:
                    o[...] = r.astype(dt)
                else:
                    o[0] = r.astype(dt)

        if nk == 1:
            finish(part)
            return
        acc_ref, k = refs[-1], pl.program_id(2)

        @pl.when(k == 0)
        def _():
            acc_ref[...] = part

        @pl.when((k > 0) & (k < nk - 1))
        def _():
            acc_ref[...] += part

        @pl.when(k == nk - 1)
        def _():
            finish(acc_ref[...] + part)

    kw = dict(name=name, grid=(M // tm, N // tn, nk), scratch_shapes=[] if nk == 1 else [pltpu.VMEM((tm, tn), F32)],
              compiler_params=_cp("parallel", "parallel", "arbitrary"))
    if slab is not None:
        buf, l = slab
        return _pcall(body, in_specs=[a_spec, b_spec, ANY], out_specs=pl.BlockSpec((1, tm, tn), lambda i, j, k: (l, i, j)),
                      out_shape=_sds(buf.shape, buf.dtype), input_output_aliases={2: 0}, **kw)(a, b, buf)
    out = _pcall(body, in_specs=[a_spec, b_spec] + [o_spec] * ne, out_specs=tuple(o_spec for _ in out_dtypes),
                 out_shape=tuple(_sds((M, N), dt) for dt in out_dtypes), **kw)(a, b, *extras)
    return out if no > 1 else out[0]


def _rows(name, fn, n_rows, tq, ins, in_specs, out_shapes, out_specs):
    def body(*refs):
        fn(pl.program_id(0), *refs)

    return _pcall(body, name=name, grid=(n_rows // tq,), in_specs=in_specs, out_specs=out_specs,
                  out_shape=out_shapes, compiler_params=_cp("arbitrary"))(*ins)


def _rb(tq, w, cb=0):
    return pl.BlockSpec((tq, w), lambda i: (i, cb))


def _full(shape):
    return pl.BlockSpec(tuple(shape), lambda *_: (0,) * len(shape))


def _rms_fwd(x, g):
    r = lax.rsqrt(jnp.mean(x * x, axis=-1, keepdims=True) + EPS)
    return x * r * g


def _rms_bwd(x, g, dy):
    r = lax.rsqrt(jnp.mean(x * x, axis=-1, keepdims=True) + EPS)
    xh = x * r
    t = dy * g
    dx = r * (t - xh * jnp.mean(t * xh, axis=-1, keepdims=True))
    return dx, jnp.sum(dy * xh, axis=0, keepdims=True)


def _acc(i, ref, val):
    @pl.when(i == 0)
    def _():
        ref[...] = val

    @pl.when(i > 0)
    def _():
        ref[...] += val


def _sigmoid(x):
    return 0.5 * jnp.tanh(0.5 * x) + 0.5


def _pre_norm(x, g):
    S, D = x.shape
    tq = _tile(S, 512, SUBLANES)

    def fn(i, x_ref, g_ref, h_ref):
        h_ref[...] = _rms_fwd(x_ref[...], g_ref[...]).astype(BF16)

    return _rows("pre_norm", fn, S, tq, (x, g), [_rb(tq, D), _full((1, D))], _sds((S, D), BF16), _rb(tq, D))


def _post_mix(x, mix, g2, g3):
    S, D = x.shape
    tq = _tile(S, 512, SUBLANES)

    def fn(i, x_ref, m_ref, g2_ref, g3_ref, x1_ref, h2_ref):
        x1 = x_ref[...] + _rms_fwd(m_ref[...], g2_ref[...])
        x1_ref[...] = x1
        h2_ref[...] = _rms_fwd(x1, g3_ref[...]).astype(BF16)

    return _rows("post_mix", fn, S, tq, (x, mix, g2, g3), [_rb(tq, D), _rb(tq, D), _full((1, D)), _full((1, D))],
                 (_sds((S, D), F32), _sds((S, D), BF16)), (_rb(tq, D), _rb(tq, D)))


def _post_mlp(x1, ff, g4, g1_next):
    S, D = x1.shape
    tq = _tile(S, 512, SUBLANES)

    def fn(i, x_ref, f_ref, g_ref, gn_ref, o_ref, h_ref):
        x2 = x_ref[...] + _rms_fwd(f_ref[...], g_ref[...])
        o_ref[...] = x2
        h_ref[...] = _rms_fwd(x2, gn_ref[...]).astype(BF16)

    r, f = _rb(tq, D), _full((1, D))
    return _rows("post_mlp", fn, S, tq, (x1, ff, g4, g1_next), [r, r, f, f], (_sds((S, D), F32), _sds((S, D), BF16)), (r, r))


def _loss_head(y, target):
    S, D = y.shape
    tq = _tile(S, 512, SUBLANES)

    def fn(i, y_ref, t_ref, l_ref, d_ref):
        e = y_ref[...] - t_ref[...]
        d_ref[...] = e * (1.0 / D)
        part = jnp.sum(jnp.sum(e * e, axis=1, keepdims=True), axis=0, keepdims=True) * (0.5 / D)
        _acc(i, l_ref, jnp.broadcast_to(part, (1, LANES)))

    return _rows("loss_head", fn, S, tq, (y, target), [_rb(tq, D), _rb(tq, D)],
                 (_sds((1, LANES), F32), _sds((S, D), F32)), (_full((1, LANES)), _rb(tq, D)))


def _post_mlp_bwd(ff, g4, dx2):
    S, D = ff.shape
    tq = _tile(S, 512, SUBLANES)

    def fn(i, f_ref, g_ref, d_ref, o_ref, dg_ref):
        dx, dg = _rms_bwd(f_ref[...], g_ref[...], d_ref[...])
        o_ref[...] = dx.astype(BF16)
        _acc(i, dg_ref, dg)

    return _rows("post_mlp_bwd", fn, S, tq, (ff, g4, dx2), [_rb(tq, D), _full((1, D)), _rb(tq, D)],
                 (_sds((S, D), BF16), _sds((1, D), F32)), (_rb(tq, D), _full((1, D))))


def _mid_bwd(x1, g3, dh2, dx2, mix, g2):
    S, D = x1.shape
    tq = _tile(S, 256, SUBLANES)

    def fn(i, x_ref, g3_ref, dh_ref, dx2_ref, m_ref, g2_ref, dx1_ref, dm_ref, dg3_ref, dg2_ref):
        d, dg3 = _rms_bwd(x_ref[...], g3_ref[...], dh_ref[...])
        dx1 = dx2_ref[...] + d
        dx1_ref[...] = dx1
        dm, dg2 = _rms_bwd(m_ref[...], g2_ref[...], dx1)
        dm_ref[...] = dm.astype(BF16)
        _acc(i, dg3_ref, dg3)
        _acc(i, dg2_ref, dg2)

    r, f = _rb(tq, D), _full((1, D))
    return _rows("mid_bwd", fn, S, tq, (x1, g3, dh2, dx2, mix, g2), [r, f, r, r, r, f],
                 (_sds((S, D), F32), _sds((S, D), BF16), _sds((1, D), F32), _sds((1, D), F32)), (r, r, f, f))


def _pre_norm_bwd(x, g1, dh, dx1):
    S, D = x.shape
    tq = _tile(S, 512, SUBLANES)

    def fn(i, x_ref, g_ref, dh_ref, dx1_ref, dx_ref, dg_ref):
        d, dg = _rms_bwd(x_ref[...], g_ref[...], dh_ref[...])
        dx_ref[...] = dx1_ref[...] + d
        _acc(i, dg_ref, dg)

    r, f = _rb(tq, D), _full((1, D))
    return _rows("pre_norm_bwd", fn, S, tq, (x, g1, dh, dx1), [r, f, r, r], (_sds((S, D), F32), _sds((1, D), F32)), (r, f))


def _mix(proj, ya, yb, D, cb_a):
    S = ya.shape[0]
    tq = _tile(S, 256, SUBLANES)

    def fn(i, ga_ref, gb_ref, ya_ref, yb_ref, o_ref):
        ga, gb, ya, yb = (r[...].astype(F32) for r in (ga_ref, gb_ref, ya_ref, yb_ref))
        o_ref[...] = (_sigmoid(ga) * ya + _sigmoid(gb) * yb).astype(BF16)

    return _rows("mix", fn, S, tq, (proj, proj, ya, yb), [_rb(tq, D, cb_a), _rb(tq, D, cb_a + 1), _rb(tq, D), _rb(tq, D)],
                 _sds((S, D), BF16), _rb(tq, D))


def _mix_bwd(proj, ya, yb, dmixin, D, cb_a):
    S = ya.shape[0]
    tq = _tile(S, 256, SUBLANES)

    def fn(i, ga_ref, gb_ref, ya_ref, yb_ref, d_ref, dya_ref, dyb_ref, dga_ref, dgb_ref):
        ga, gb, ya, yb, d = (r[...].astype(F32) for r in (ga_ref, gb_ref, ya_ref, yb_ref, d_ref))
        sa, sb = _sigmoid(ga), _sigmoid(gb)
        dya_ref[...] = (d * sa).astype(BF16)
        dyb_ref[...] = (d * sb).astype(BF16)
        dga_ref[...] = (d * ya * sa * (1.0 - sa)).astype(BF16)
        dgb_ref[...] = (d * yb * sb * (1.0 - sb)).astype(BF16)

    r = _rb(tq, D)
    o = _sds((S, D), BF16)
    return _rows("mix_bwd", fn, S, tq, (proj, proj, ya, yb, dmixin), [_rb(tq, D, cb_a), _rb(tq, D, cb_a + 1), r, r, r],
                 (o, o, o, o), (r, r, r, r))


HALO = 16


def _shift_down(xe, k, tq):
    return pltpu.roll(xe, k, 0)[HALO:HALO + tq]


def _conv_pre(cur_ref, halo_ref, w_ref, i, tq):
    x = cur_ref[...].astype(F32)
    halo = jnp.where(i > 0, halo_ref[...].astype(F32), 0.0)
    xe = jnp.concatenate([halo, x], axis=0)
    xs = [x] + [_shift_down(xe, k, tq) for k in range(1, DN_CONV)]
    w = w_ref[...]
    c = sum(w[DN_CONV - 1 - k:DN_CONV - k, :] * xs[k] for k in range(DN_CONV))
    return c, xs


def _dn_prep(proj, conv_w, W):
    S = proj.shape[0]
    tq = _tile(S, 256, HALO)
    hb = tq // HALO

    def body(cur_ref, halo_ref, w_ref, o_ref):
        j, i = pl.program_id(0), pl.program_id(1)
        c, _ = _conv_pre(cur_ref, halo_ref, w_ref, i, tq)
        y = c * _sigmoid(c)
        scale = jnp.where(j == 0, DN_DK ** -0.5, 1.0)
        for h in range(W // DN_DK):
            sl = slice(h * DN_DK, (h + 1) * DN_DK)
            yh = y[:, sl]
            rs = lax.rsqrt(jnp.sum(yh * yh, axis=-1, keepdims=True) + EPS)
            o_ref[:, sl] = jnp.where(j == 2, yh, yh * rs * scale)

    return _pcall(body, name="dn_prep", grid=(3, S // tq),
                  in_specs=[pl.BlockSpec((tq, W), lambda j, i: (i, j)),
                            pl.BlockSpec((HALO, W), lambda j, i: (jnp.maximum(i * hb - 1, 0), j)),
                            pl.BlockSpec((DN_CONV, W), lambda j, i: (0, j))],
                  out_specs=pl.BlockSpec((tq, W), lambda j, i: (i, j)), out_shape=_sds((S, 3 * W), F32),
                  compiler_params=_cp("arbitrary", "arbitrary"))(proj, proj, conv_w)


def _dn_prep_bwd_a(proj, conv_w, dqkv, W):
    S = proj.shape[0]
    tq = _tile(S, 256, HALO)
    hb = tq // HALO

    def body(cur_ref, halo_ref, w_ref, d_ref, dc_ref, dw_ref):
        j, i = pl.program_id(0), pl.program_id(1)
        c, xs = _conv_pre(cur_ref, halo_ref, w_ref, i, tq)
        sg = _sigmoid(c)
        y = c * sg
        scale = jnp.where(j == 0, DN_DK ** -0.5, 1.0)
        dout = d_ref[0]
        dys = []
        for h in range(W // DN_DK):
            sl = slice(h * DN_DK, (h + 1) * DN_DK)
            yh, dh = y[:, sl], dout[:, sl]
            rs = lax.rsqrt(jnp.sum(yh * yh, axis=-1, keepdims=True) + EPS)
            yn = yh * rs
            dn = scale * rs * (dh - yn * jnp.sum(dh * yn, axis=-1, keepdims=True))
            dys.append(jnp.where(j == 2, dh, dn))
        dy = jnp.concatenate(dys, axis=1)
        dc = dy * (sg * (1.0 + c * (1.0 - sg)))
        dc_ref[...] = dc
        dw = jnp.concatenate([jnp.sum(dc * xs[DN_CONV - 1 - r], axis=0, keepdims=True) for r in range(DN_CONV)], axis=0)
        _acc(i, dw_ref, dw)

    return _pcall(body, name="dn_prep_bwd_a", grid=(3, S // tq),
                  in_specs=[pl.BlockSpec((tq, W), lambda j, i: (i, j)),
                            pl.BlockSpec((HALO, W), lambda j, i: (jnp.maximum(i * hb - 1, 0), j)),
                            pl.BlockSpec((DN_CONV, W), lambda j, i: (0, j)),
                            pl.BlockSpec((1, tq, W), lambda j, i: (j, i, 0))],
                  out_specs=(pl.BlockSpec((tq, W), lambda j, i: (i, j)), pl.BlockSpec((DN_CONV, W), lambda j, i: (0, j))),
                  out_shape=(_sds((S, 3 * W), F32), _sds((DN_CONV, 3 * W), F32)),
                  compiler_params=_cp("arbitrary", "arbitrary"))(proj, proj, conv_w, dqkv)


def _dn_prep_bwd_b(dc, conv_w, W):
    S = dc.shape[0]
    tq = _tile(S, 256, SUBLANES)
    hb = tq // SUBLANES
    nblk = S // tq

    def body(cur_ref, nxt_ref, w_ref, o_ref):
        i = pl.program_id(1)
        d = cur_ref[...]
        nxt = jnp.where(i < nblk - 1, nxt_ref[...], 0.0)
        de = jnp.concatenate([d, nxt], axis=0)
        w = w_ref[...]
        out = w[DN_CONV - 1:DN_CONV, :] * d
        for k in range(1, DN_CONV):
            out = out + w[DN_CONV - 1 - k:DN_CONV - k, :] * pltpu.roll(de, tq + SUBLANES - k, 0)[0:tq]
        o_ref[...] = out.astype(BF16)

    return _pcall(body, name="dn_prep_bwd_b", grid=(3, nblk),
                  in_specs=[pl.BlockSpec((tq, W), lambda j, i: (i, j)),
                            pl.BlockSpec((SUBLANES, W), lambda j, i: (jnp.minimum((i + 1) * hb, S // SUBLANES - 1), j)),
                            pl.BlockSpec((DN_CONV, W), lambda j, i: (0, j))],
                  out_specs=pl.BlockSpec((tq, W), lambda j, i: (i, j)), out_shape=_sds((S, 3 * W), BF16),
                  compiler_params=_cp("arbitrary", "arbitrary"))(dc, dc, conv_w)


def _gate_terms(ba, al, dt):
    u = ba + dt
    sp = jnp.maximum(u, 0.0) + jnp.log(1.0 + jnp.exp(-jnp.abs(u)))
    return _sigmoid(ba), -jnp.exp(al) * sp, u


def _dn_gates(ba, alog_row, dt_row, H):
    S = ba.shape[0]
    tq = _tile(S, 512, SUBLANES)
    W = H * DN_DK

    def fn(i, ba_ref, al_ref, dt_ref, be_ref, g_ref):
        bet, gg, _ = _gate_terms(ba_ref[...], al_ref[...], dt_ref[...])
        for h in range(H):
            sl = slice(h * DN_DK, (h + 1) * DN_DK)
            be_ref[:, sl] = jnp.broadcast_to(bet[:, h:h + 1], (tq, DN_DK))
            g_ref[:, sl] = jnp.broadcast_to(gg[:, H + h:H + h + 1], (tq, DN_DK))

    return _rows("dn_gates", fn, S, tq, (ba, alog_row, dt_row), [_rb(tq, LANES), _full((1, LANES)), _full((1, LANES))],
                 (_sds((S, W), F32), _sds((S, W), F32)), (_rb(tq, W), _rb(tq, W)))


def _dn_gates_bwd(ba, alog_row, dt_row, dbeta_b, dg_b, H):
    S = ba.shape[0]
    tq = _tile(S, 512, SUBLANES)
    W = H * DN_DK

    def fn(i, ba_ref, al_ref, dt_ref, db_ref, dg_ref, o_ref, dal_ref, ddt_ref):
        bet, gg, u = _gate_terms(ba_ref[...], al_ref[...], dt_ref[...])
        lane = lax.broadcasted_iota(jnp.int32, (tq, LANES), 1)
        d = jnp.zeros((tq, LANES), F32)
        for h in range(H):
            d = jnp.where(lane == h, db_ref[:, h * DN_DK:h * DN_DK + 1], d)
            d = jnp.where(lane == H + h, dg_ref[:, h * DN_DK:h * DN_DK + 1], d)
        is_a = (lane >= H) & (lane < 2 * H)
        da = jnp.where(is_a, d * (-jnp.exp(al_ref[...]) * _sigmoid(u)), 0.0)
        dlog = jnp.where(lane < H, d * bet * (1.0 - bet), da)
        o_ref[...] = jnp.concatenate([dlog, jnp.zeros((tq, BA_W - LANES), F32)], axis=1).astype(BF16)
        _acc(i, dal_ref, jnp.sum(jnp.where(is_a, d * gg, 0.0), axis=0, keepdims=True))
        _acc(i, ddt_ref, jnp.sum(da, axis=0, keepdims=True))

    f = _full((1, LANES))
    return _rows("dn_gates_bwd", fn, S, tq, (ba, alog_row, dt_row, dbeta_b, dg_b),
                 [_rb(tq, LANES), f, f, _rb(tq, W), _rb(tq, W)],
                 (_sds((S, BA_W), BF16), _sds((1, LANES), F32), _sds((1, LANES), F32)), (_rb(tq, BA_W), f, f))


def _dn_out(o, proj, ng, W, cb_z):
    S = o.shape[0]
    tq = _tile(S, 256, SUBLANES)

    def fn(i, o_ref, z_ref, g_ref, y_ref):
        for h in range(W // DN_DK):
            sl = slice(h * DN_DK, (h + 1) * DN_DK)
            z = z_ref[:, sl].astype(F32)
            y_ref[:, sl] = (_rms_fwd(o_ref[:, sl], g_ref[...]) * (z * _sigmoid(z))).astype(BF16)

    return _rows("dn_out", fn, S, tq, (o, proj, ng), [_rb(tq, W), _rb(tq, W, cb_z), _full((1, DN_DK))], _sds((S, W), BF16), _rb(tq, W))


def _dn_out_bwd(o, proj, ng, dy, W, cb_z):
    S = o.shape[0]
    tq = _tile(S, 256, SUBLANES)

    def fn(i, o_ref, z_ref, g_ref, d_ref, do_ref, dz_ref, dg_ref):
        g = g_ref[...]
        dg = jnp.zeros((1, DN_DK), F32)
        for h in range(W // DN_DK):
            sl = slice(h * DN_DK, (h + 1) * DN_DK)
            oh, z, d = o_ref[:, sl], z_ref[:, sl].astype(F32), d_ref[:, sl].astype(F32)
            sg = _sigmoid(z)
            dn = d * (z * sg)
            dz_ref[:, sl] = (d * _rms_fwd(oh, g) * (sg * (1.0 + z * (1.0 - sg)))).astype(BF16)
            dx, dgh = _rms_bwd(oh, g, dn)
            do_ref[:, sl] = dx
            dg = dg + dgh
        _acc(i, dg_ref, dg)

    r = _rb(tq, W)
    return _rows("dn_out_bwd", fn, S, tq, (o, proj, ng, dy), [r, _rb(tq, W, cb_z), _full((1, DN_DK)), r],
                 (_sds((S, W), F32), _sds((S, W), BF16), _sds((1, DN_DK), F32)), (r, r, _full((1, DN_DK))))


def _bdot(a, b, mode="nn"):
    return lax.dot_general(a.astype(BF16), b.astype(BF16), (_DIMS[mode], ((), ())), preferred_element_type=F32)


def _rsum(x):
    return jnp.broadcast_to(jnp.sum(x, axis=-1, keepdims=True), x.shape)


def _dot3(a, b, mode="nn"):
    ah, bh = a.astype(BF16), b.astype(BF16)
    al, bl = (a - ah.astype(F32)).astype(BF16), (b - bh.astype(F32)).astype(BF16)
    d = lambda x, y: lax.dot_general(x, y, (_DIMS[mode], ((), ())), preferred_element_type=F32)
    return d(ah, bh) + (d(al, bh) + d(ah, bl))


def _cumsum_rows(x, reverse=False):
    n = x.shape[0]
    row = lax.broadcasted_iota(jnp.int32, x.shape, 0)
    s = 1
    while s < n:
        if reverse:
            x = x + jnp.where(row < n - s, pltpu.roll(x, n - s, 0), 0.0)
        else:
            x = x + jnp.where(row >= s, pltpu.roll(x, s, 0), 0.0)
        s *= 2
    return x


def _each(f, *lists):
    return [f(*a) for a in zip(*lists)]


def _delta_local(qs, ks, vs, bes, grs):
    C = DN_CHUNK
    ri = lax.broadcasted_iota(jnp.int32, (C, C), 0)
    ci = lax.broadcasted_iota(jnp.int32, (C, C), 1)
    causal, strict = ri >= ci, ri > ci
    gcs = [_cumsum_rows(g) for g in grs]
    decays = [jnp.where(causal, jnp.exp(jnp.where(causal, gc[:, :C] - gc.T[:C, :], 0.0)), 0.0) for gc in gcs]
    egs = [jnp.exp(gc) for gc in gcs]
    eks = [jnp.exp(gc[C - 1:C, :] - gc) for gc in gcs]
    gams = [jnp.exp(gc[C - 1:C, :]) for gc in gcs]
    kbs = _each(lambda k, be: k * be, ks, bes)
    kks = _each(lambda kb, k: _bdot(kb, k, "nt"), kbs, ks)
    nls = _each(lambda kk, dc: jnp.where(strict, -kk * dc, 0.0), kks, decays)
    eye = (ri == ci).astype(F32)
    ts = [eye + nl for nl in nls]
    pws = [_dot3(nl, nl) for nl in nls]
    for s in range(4):
        both = _each(lambda t, pw: _dot3(jnp.concatenate([t, pw], axis=0), pw), ts, pws)
        ts = _each(lambda t, b: t + b[:C], ts, both)
        pws = [b[C:] for b in both]
    ts = _each(lambda t, pw: t + _dot3(t, pw), ts, pws)
    vbs = _each(lambda v, be: v * be, vs, bes)
    kbes = _each(lambda kb, eg: kb * eg, kbs, egs)
    uws = _each(lambda t, vb, kbe: _dot3(t, jnp.concatenate([vb, kbe], axis=1)), ts, vbs, kbes)
    us, ws = [uw[:, :DN_DK] for uw in uws], [uw[:, DN_DK:] for uw in uws]
    qks = _each(lambda q, k: _bdot(q, k, "nt"), qs, ks)
    return dict(decay=decays, eg=egs, ek=eks, gam=gams, kb=kbs, kk=kks, t=ts, vb=vbs, kbe=kbes, u=us, w=ws, qk=qks,
                a=_each(lambda qk, dc: qk * dc, qks, decays), qd=_each(lambda q, eg: q * eg, qs, egs),
                kd=_each(lambda k, ek: k * ek, ks, eks), strict=strict)


def _delta_items(refs, CB, HB):
    C, dk = DN_CHUNK, DN_DK
    return [[r[c * C:(c + 1) * C, h * dk:(h + 1) * dk] for h in range(HB) for c in range(CB)] for r in refs]


def _delta_fwd(qkv, beta_b, g_b, H, CB, HB):
    S = qkv.shape[0]
    C, dk = DN_CHUNK, DN_DK
    N = S // C
    R = CB * C
    G = H // HB

    def body(q_ref, k_ref, v_ref, b_ref, g_ref, o_ref, st_ref, s_ref):
        @pl.when(pl.program_id(1) == 0)
        def _():
            s_ref[...] = jnp.zeros((HB, dk, dk), F32)

        L = _delta_local(*_delta_items((q_ref, k_ref, v_ref, b_ref, g_ref), CB, HB))
        ss = [s_ref[h] for h in range(HB)]
        for c in range(CB):
            it = [h * CB + c for h in range(HB)]
            for h in range(HB):
                st_ref[h, c] = ss[h]
            wq = [_bdot(jnp.concatenate([L["w"][i], L["qd"][i]], axis=0), s) for i, s in zip(it, ss)]
            vns = [L["u"][i] - x[:C] for i, x in zip(it, wq)]
            outs = [x[C:] + _bdot(L["a"][i], vn) for i, x, vn in zip(it, wq, vns)]
            ss = [s * L["gam"][i] + _bdot(L["kd"][i], vn, "tn") for i, s, vn in zip(it, ss, vns)]
            for h in range(HB):
                o_ref[c * C:(c + 1) * C, h * dk:(h + 1) * dk] = outs[h]
        for h in range(HB):
            s_ref[h] = ss[h]

    blk = lambda off: pl.BlockSpec((R, HB * dk), lambda h, n: (n, off + h))
    return _pcall(body, name="delta_fwd", grid=(G, N // CB),
                  in_specs=[blk(0), blk(G), blk(2 * G), blk(0), blk(0)],
                  out_specs=(blk(0), pl.BlockSpec((HB, CB, dk, dk), lambda h, n: (h, n, 0, 0))),
                  out_shape=(_sds((S, H * dk), F32), _sds((H, N, dk, dk), F32)),
                  scratch_shapes=[pltpu.VMEM((HB, dk, dk), F32)],
                  compiler_params=_cp("arbitrary", "arbitrary"))(qkv, qkv, qkv, beta_b, g_b)


def _delta_bwd(qkv, beta_b, g_b, states, do, H, CB, HB):
    S = qkv.shape[0]
    C, dk = DN_CHUNK, DN_DK
    N = S // C
    R = CB * C
    NB = N // CB
    G = H // HB

    def body(q_ref, k_ref, v_ref, b_ref, g_ref, st_ref, do_ref, dqkv_ref, db_ref, dg_ref, ds_ref):
        @pl.when(pl.program_id(1) == 0)
        def _():
            ds_ref[...] = jnp.zeros((HB, dk, dk), F32)

        qs, ks, vs, bes, grs, dos = _delta_items((q_ref, k_ref, v_ref, b_ref, g_ref, do_ref), CB, HB)
        L = _delta_local(qs, ks, vs, bes, grs)
        ts, decays, kbs, egs, eks, gams, qds, kds = (L[n] for n in ("t", "decay", "kb", "eg", "ek", "gam", "qd", "kd"))
        s0s = [st_ref[h, c] for h in range(HB) for c in range(CB)]
        vns = _each(lambda u, w, s0: u - _bdot(w, s0), L["u"], L["w"], s0s)
        pre_dvn = _each(lambda a, d: _bdot(a, d, "tn"), L["a"], dos)
        pre_ds = _each(lambda qd, d: _bdot(qd, d, "tn"), qds, dos)
        das = _each(lambda d, vn: _bdot(d, vn, "nt"), dos, vns)
        ds = [ds_ref[h] for h in range(HB)]
        ds1s, dvns = [None] * (HB * CB), [None] * (HB * CB)
        for c in reversed(range(CB)):
            it = [h * CB + c for h in range(HB)]
            new = [pre_dvn[i] + _bdot(kds[i], d) for i, d in zip(it, ds)]
            for i, d, dv in zip(it, ds, new):
                ds1s[i], dvns[i] = d, dv
            ds = [pre_ds[i] + d * gams[i] - _bdot(L["w"][i], dv, "tn") for i, d, dv in zip(it, ds, new)]
        for h in range(HB):
            ds_ref[h] = ds[h]
        dkds = _each(lambda vn, d1: _bdot(vn, d1, "nt"), vns, ds1s)
        dgams = _each(lambda s0, d1: jnp.sum(jnp.sum(s0 * d1, axis=1, keepdims=True), axis=0, keepdims=True), s0s, ds1s)
        ost = _each(lambda d, dv, s0: _bdot(jnp.concatenate([d, dv], axis=0), s0, "nt"), dos, dvns, s0s)
        dqds, dws = [x[:C] for x in ost], [-x[C:] for x in ost]
        dvw = _each(lambda dv, dw: jnp.concatenate([dv, dw], axis=1), dvns, dws)
        tdvw = _each(lambda t, x: _dot3(t, x, "tn"), ts, dvw)
        dvbs, dkbes = [x[:, :dk] for x in tdvw], [x[:, dk:] for x in tdvw]
        dts = _each(lambda x, vb, kbe: _dot3(x, jnp.concatenate([vb, kbe], axis=1), "nt"), dvw, L["vb"], L["kbe"])
        tmp = _each(lambda dt, t: _dot3(dt, t, "nt"), dts, ts)
        dls = _each(lambda t, x: -_dot3(t, x, "tn"), ts, tmp)
        ms = _each(lambda dl, dc: jnp.where(L["strict"], dl * dc, 0.0), dls, decays)
        mas = _each(lambda da, dc: da * dc, das, decays)
        dkbs = _each(lambda m, k, dkbe, eg: _bdot(m, k) + dkbe * eg, ms, ks, dkbes, egs)
        dks = _each(lambda m, kb, ma, q, dkd, ek, dkb, be: _bdot(m, kb, "tn") + _bdot(ma, q, "tn") + dkd * ek + dkb * be,
                    ms, kbs, mas, qs, dkds, eks, dkbs, bes)
        dqs = _each(lambda ma, k, dqd, eg: _bdot(ma, k) + dqd * eg, mas, ks, dqds, egs)
        es = _each(lambda m, kk, ma, qk: m * kk + ma * qk, ms, L["kk"], mas, L["qk"])
        ones = jnp.ones((C, dk), BF16)
        row = lax.broadcasted_iota(jnp.int32, (C, dk), 0)
        for i in range(HB * CB):
            h, c = divmod(i, CB)
            rs, cs = slice(c * C, (c + 1) * C), slice(h * dk, (h + 1) * dk)
            e = es[i]
            e_hi = e.astype(BF16)
            col = _bdot(e_hi, ones, "tn") + _bdot(e - e_hi.astype(F32), ones, "tn")
            t_kd = _rsum(dkds[i] * kds[i])
            dgc = (jnp.broadcast_to(jnp.sum(e, axis=1, keepdims=True), (C, dk)) - col + _rsum(dqds[i] * qds[i]) - t_kd
                   + _rsum(dkbes[i] * L["kbe"][i]))
            dglast = jnp.sum(t_kd[:, 0:1], axis=0, keepdims=True) + dgams[i] * gams[i][:, 0:1]
            dgc = dgc + jnp.where(row == C - 1, dglast, 0.0)
            dqkv_ref[0, rs, cs] = dqs[i]
            dqkv_ref[1, rs, cs] = dks[i]
            dqkv_ref[2, rs, cs] = dvbs[i] * bes[i]
            db_ref[rs, cs] = _rsum(dkbs[i] * ks[i]) + _rsum(dvbs[i] * vs[i])
            dg_ref[rs, cs] = _cumsum_rows(dgc, reverse=True)

    blk = lambda off: pl.BlockSpec((R, HB * dk), lambda h, n: (NB - 1 - n, off + h))
    W = H * dk
    return _pcall(body, name="delta_bwd", grid=(G, NB),
                  in_specs=[blk(0), blk(G), blk(2 * G), blk(0), blk(0),
                            pl.BlockSpec((HB, CB, dk, dk), lambda h, n: (h, NB - 1 - n, 0, 0)), blk(0)],
                  out_specs=(pl.BlockSpec((3, R, HB * dk), lambda h, n: (0, NB - 1 - n, h)), blk(0), blk(0)),
                  out_shape=(_sds((3, S, W), F32), _sds((S, W), F32), _sds((S, W), F32)),
                  scratch_shapes=[pltpu.VMEM((HB, dk, dk), F32)],
                  compiler_params=_cp("arbitrary", "arbitrary"))(qkv, qkv, qkv, beta_b, g_b, states, do)


def _rope_consts():
    lane = np.arange(LANES) % SW_HD
    half = ROT_DIM // 2
    inv = (ROPE_THETA ** (-np.arange(half, dtype=np.float32) * np.float32(2.0 / ROT_DIM))).astype(np.float32)
    freq = np.where(lane < ROT_DIM, inv[lane % half], 0.0).astype(np.float32)
    lo = (lane < half).astype(np.float32)
    hi = ((lane >= half) & (lane < ROT_DIM)).astype(np.float32)
    return jnp.asarray(np.stack([freq, -lo, hi] + [np.zeros(LANES, np.float32)] * 5))


def _rope_tables(pos_col):
    S = pos_col.shape[0]
    tq = _tile(S, 1024, SUBLANES)

    def fn(i, p_ref, c_ref, cos_ref, s1_ref, s2_ref):
        ang = p_ref[...].astype(F32) * c_ref[0:1, :]
        sn = jnp.sin(ang)
        cos_ref[...] = jnp.cos(ang)
        s1_ref[...] = sn * c_ref[1:2, :]
        s2_ref[...] = sn * c_ref[2:3, :]

    o, r = _sds((S, LANES), F32), _rb(tq, LANES)
    return _rows("rope_tables", fn, S, tq, (pos_col, _rope_consts()), [_rb(tq, 1), _full((SUBLANES, LANES))], (o, o, o), (r, r, r))


def _wide(a, w):
    return a if w == LANES else jnp.tile(a, (1, w // LANES))


def _rope(x, cos, s1, s2):
    w, h = x.shape[1], ROT_DIM // 2
    return x * _wide(cos, w) + pltpu.roll(x, w - h, 1) * _wide(s1, w) + pltpu.roll(x, h, 1) * _wide(s2, w)


def _unrope(d, cos, s1, s2):
    w, h = d.shape[1], ROT_DIM // 2
    return d * _wide(cos, w) + pltpu.roll(d * _wide(s1, w), h, 1) + pltpu.roll(d * _wide(s2, w), w - h, 1)


def _swa_setup(n, q_ref, kc_ref, kp_ref, vc_ref, vp_ref, tc, tp):
    B = SW_BLOCK
    qr = _rope(q_ref[...].astype(F32), tc[0][...], tc[1][...], tc[2][...]) * (SW_HD ** -0.5)
    kw = jnp.concatenate([_rope(kp_ref[...].astype(F32), tp[0][...], tp[1][...], tp[2][...]),
                          _rope(kc_ref[...].astype(F32), tc[0][...], tc[1][...], tc[2][...])], axis=0)
    vw = jnp.concatenate([vp_ref[...], vc_ref[...]], axis=0).astype(F32)
    lane = lax.broadcasted_iota(jnp.int32, (2 * B, LANES), 1)
    heads = []
    for hk in range(SW_KV_HEADS):
        kh, vh = kw[:, hk * SW_HD:(hk + 1) * SW_HD], vw[:, hk * SW_HD:(hk + 1) * SW_HD]
        kk, vv = jnp.concatenate([kh, kh], axis=1), jnp.concatenate([vh, vh], axis=1)
        heads.append(tuple(jnp.where(sel, t, 0.0).astype(BF16) for t in (kk, vv) for sel in (lane < SW_HD, lane >= SW_HD)))
    prev = lax.broadcasted_iota(jnp.int32, (B, B), 1) > lax.broadcasted_iota(jnp.int32, (B, B), 0)
    return qr, heads, (prev, jnp.where(prev & (n == 0), -1e30, 0.0)), lane


def _fold(x, prev):
    return jnp.where(prev, x[:, :SW_BLOCK], x[:, SW_BLOCK:])


def _unfold(x, prev):
    return jnp.concatenate([jnp.where(prev, x, 0.0), jnp.where(prev, 0.0, x)], axis=1)


SWA_GROUPS = 4
SWA_GROUPS_BWD = 2


def _swa_probs(items, qs, heads, fold, sk_ref, G2):
    prev, bias = fold
    ss = [_fold(_bdot(qs[j], heads[j // G2][half], "nt"), prev) + bias for j, half in items]
    sks = [sk_ref[0:1, 2 * j + half:2 * j + half + 1] for j, half in items]
    ms = [jnp.maximum(jnp.max(s, axis=-1, keepdims=True), sk) for s, sk in zip(ss, sks)]
    ps = [jnp.exp(s - m) for s, m in zip(ss, ms)]
    es = [jnp.exp(sk - m) for sk, m in zip(sks, ms)]
    inv = [1.0 / (jnp.sum(p, axis=-1, keepdims=True) + e) for p, e in zip(ps, es)]
    return [p * i for p, i in zip(ps, inv)], [e * i for e, i in zip(es, inv)]


def _swa_specs(W, cb_q, cb_k):
    B = SW_BLOCK
    assert (W // LANES) % SWA_GROUPS == 0 and (W // LANES) % SWA_GROUPS_BWD == 0
    cur = lambda w, cb: pl.BlockSpec((B, w), lambda n: (n, cb))
    prv = lambda w, cb: pl.BlockSpec((B, w), lambda n: (jnp.maximum(n - 1, 0), cb))
    specs = [cur(W, cb_q), cur(LANES, cb_k), prv(LANES, cb_k), cur(LANES, cb_k + 1), prv(LANES, cb_k + 1)]
    return specs + [cur(LANES, 0)] * 3 + [prv(LANES, 0)] * 3 + [_full((1, LANES))]


def _swa_fwd(proj, tabs, sinks_row, W, cb_q, cb_k):
    S = proj.shape[0]
    G2 = SW_Q_HEADS // SW_KV_HEADS // 2

    def body(q_ref, kc_ref, kp_ref, vc_ref, vp_ref, c0, c1, c2, p0, p1, p2, sk_ref, o_ref):
        n = pl.program_id(0)
        qr, heads, fold, _ = _swa_setup(n, q_ref, kc_ref, kp_ref, vc_ref, vp_ref, (c0, c1, c2), (p0, p1, p2))
        qs = [qr[:, j * LANES:(j + 1) * LANES].astype(BF16) for j in range(W // LANES)]
        for j0 in range(0, W // LANES, SWA_GROUPS):
            items = [(j, half) for j in range(j0, j0 + SWA_GROUPS) for half in range(2)]
            probs, _ = _swa_probs(items, qs, heads, fold, sk_ref, G2)
            pv = [_bdot(_unfold(p, fold[0]), heads[j // G2][2 + half]) for p, (j, half) in zip(probs, items)]
            for g in range(SWA_GROUPS):
                o_ref[:, (j0 + g) * LANES:(j0 + g + 1) * LANES] = (pv[2 * g] + pv[2 * g + 1]).astype(BF16)

    t = tuple(tabs)
    return _pcall(body, name="swa_fwd", grid=(S // SW_BLOCK,), in_specs=_swa_specs(W, cb_q, cb_k),
                  out_specs=pl.BlockSpec((SW_BLOCK, W), lambda n: (n, 0)), out_shape=_sds((S, W), BF16),
                  compiler_params=_cp("arbitrary"))(proj, proj, proj, proj, proj, *t, *t, sinks_row)


def _swa_bwd(proj, tabs, sinks_row, do, W, cb_q, cb_k):
    S = proj.shape[0]
    B = SW_BLOCK
    G2 = SW_Q_HEADS // SW_KV_HEADS // 2
    SKR = -(-SW_Q_HEADS // SUBLANES) * SUBLANES

    def body(q_ref, kc_ref, kp_ref, vc_ref, vp_ref, c0, c1, c2, p0, p1, p2, sk_ref, do_ref,
             dq_ref, dkc_ref, dkp_ref, dvc_ref, dvp_ref, dsk_ref):
        n = pl.program_id(0)
        qr, heads, fold, lane = _swa_setup(n, q_ref, kc_ref, kp_ref, vc_ref, vp_ref, (c0, c1, c2), (p0, p1, p2))
        prev = fold[0]

        @pl.when(n == 0)
        def _():
            dsk_ref[...] = jnp.zeros((SKR, LANES), F32)

        acc_k = [jnp.zeros((2 * B, LANES), F32) for _ in range(SW_KV_HEADS)]
        acc_v = [jnp.zeros((2 * B, LANES), F32) for _ in range(SW_KV_HEADS)]
        qs = [qr[:, j * LANES:(j + 1) * LANES].astype(BF16) for j in range(W // LANES)]
        dos = [do_ref[:, j * LANES:(j + 1) * LANES].astype(BF16) for j in range(W // LANES)]
        dqs = []
        for j0 in range(0, W // LANES, SWA_GROUPS_BWD):
            items = [(j, half) for j in range(j0, j0 + SWA_GROUPS_BWD) for half in range(2)]
            probs, psinks = _swa_probs(items, qs, heads, fold, sk_ref, G2)
            dps = [_fold(_bdot(dos[j], heads[j // G2][2 + half], "nt"), prev) for j, half in items]
            deltas = [jnp.sum(p * dp, axis=-1, keepdims=True) for p, dp in zip(probs, dps)]
            dss = [_unfold(p * (dp - dl), prev).astype(BF16) for p, dp, dl in zip(probs, dps, deltas)]
            pbs = [_unfold(p, prev).astype(BF16) for p in probs]
            dqp = [_bdot(ds, heads[j // G2][half]) for ds, (j, half) in zip(dss, items)]
            dkk = [_bdot(ds, qs[j], "tn") for ds, (j, half) in zip(dss, items)]
            dvv = [_bdot(p, dos[j], "tn") for p, (j, half) in zip(pbs, items)]
            for i, (j, half) in enumerate(items):
                hk, h = j // G2, 2 * j + half
                sel = (lane < SW_HD) if half == 0 else (lane >= SW_HD)
                acc_k[hk] = acc_k[hk] + jnp.where(sel, dkk[i], 0.0)
                acc_v[hk] = acc_v[hk] + jnp.where(sel, dvv[i], 0.0)
                dsk_ref[h:h + 1, :] += jnp.broadcast_to(-jnp.sum(psinks[i] * deltas[i], axis=0, keepdims=True), (1, LANES))
            dqs += [dqp[2 * g] + dqp[2 * g + 1] for g in range(SWA_GROUPS_BWD)]
        dq = jnp.concatenate(dqs, axis=1) * (SW_HD ** -0.5)
        dq_ref[...] = _unrope(dq, c0[...], c1[...], c2[...]).astype(BF16)
        fold = lambda a: a[:, :SW_HD] + a[:, SW_HD:]
        dkw = jnp.concatenate([fold(a) for a in acc_k], axis=1)
        dvw = jnp.concatenate([fold(a) for a in acc_v], axis=1)
        dkp_ref[...], dkc_ref[...] = dkw[:B], dkw[B:]
        dvp_ref[...], dvc_ref[...] = dvw[:B], dvw[B:]

    t = tuple(tabs)
    blk = lambda w: pl.BlockSpec((B, w), lambda n: (n, 0))
    o = _sds((S, LANES), F32)
    return _pcall(body, name="swa_bwd", grid=(S // B,), in_specs=_swa_specs(W, cb_q, cb_k) + [blk(W)],
                  out_specs=(blk(W), blk(LANES), blk(LANES), blk(LANES), blk(LANES), _full((SKR, LANES))),
                  out_shape=(_sds((S, W), BF16), o, o, o, o, _sds((SKR, LANES), F32)),
                  compiler_params=_cp("arbitrary"))(proj, proj, proj, proj, proj, *t, *t, sinks_row, do)


def _swa_kv_combine(dkc, dkp, dvc, dvp, tabs):
    S = dkc.shape[0]
    B = SW_BLOCK
    nb = S // B

    def fn(n, kc_ref, kp_ref, vc_ref, vp_ref, c0, c1, c2, o_ref):
        more = n < nb - 1
        dk = kc_ref[...] + jnp.where(more, kp_ref[...], 0.0)
        dv = vc_ref[...] + jnp.where(more, vp_ref[...], 0.0)
        o_ref[...] = jnp.concatenate([_unrope(dk, c0[...], c1[...], c2[...]), dv], axis=1).astype(BF16)

    cur = _rb(B, LANES)
    nxt = pl.BlockSpec((B, LANES), lambda n: (jnp.minimum(n + 1, nb - 1), 0))
    return _rows("swa_kv_combine", fn, S, B, (dkc, dkp, dvc, dvp, *tabs), [cur, nxt, cur, nxt, cur, cur, cur],
                 _sds((S, 2 * LANES), BF16), _rb(B, 2 * LANES))


ANY = pl.BlockSpec(memory_space=pl.ANY)


def _place():
    x, y, c = lax.axis_index("x"), lax.axis_index("y"), lax.axis_index("c")
    return x, y, c, [(1 - x, y), (x, 1 - y), (1 - x, 1 - y)]


def _comm_call(name, body, out_shapes, n_sems, n_local, *ins):
    return _pcall(body, name=name, out_shape=tuple(out_shapes), in_specs=[ANY] * len(ins), out_specs=tuple(ANY for _ in out_shapes),
                  scratch_shapes=[pltpu.SemaphoreType.DMA((n_sems,)), pltpu.SemaphoreType.DMA((n_sems,)),
                                  pltpu.SemaphoreType.DMA((n_local,))])(*ins)


def _remote(src, dst, send, recv, k, to):
    return pltpu.make_async_remote_copy(src_ref=src, dst_ref=dst, send_sem=send.at[k], recv_sem=recv.at[k], device_id=to,
                                        device_id_type=MESH)


def _chip_slice(ref, axis, s):
    if axis is None:
        return ref.at[s]
    q = ref.shape[axis] // 4
    start = s * q if isinstance(s, int) else pl.multiple_of(s * q, q)
    return ref.at[tuple([slice(None)] * axis + [pl.ds(start, q)])]


def _own_part(a, axis, me):
    if axis is None:
        return lax.dynamic_index_in_dim(a, me, 0, keepdims=False)
    q = a.shape[axis] // 4
    return lax.dynamic_slice_in_dim(a, me * q, q, axis)


HBM = pl.BlockSpec(memory_space=pltpu.HBM)
SEM = pl.BlockSpec(memory_space=pltpu.SEMAPHORE)
EFFECT = pltpu.SideEffectType.DATAFLOW_SIDE_EFFECTING


def _split_start(name, arrs, land_shapes, plan, nc, after=None):
    n = len(arrs)
    lands = [lax.empty(s.shape, s.dtype) for s in land_shapes]
    ins = list(arrs) + lands + ([] if after is None else [after])

    def body(*refs):
        outs = refs[len(ins):]
        for k, (src, dst, _, peer) in enumerate(plan(refs[:n], refs[n:n + len(lands)])):
            pltpu.make_async_remote_copy(src_ref=src, dst_ref=dst, send_sem=outs[k], recv_sem=outs[nc + k], device_id=peer,
                                         device_id_type=MESH).start()
        outs[-1][...] = jnp.zeros((SUBLANES, LANES), F32)

    nt = n + len(lands)
    thru = [pltpu.HBM(a.shape, a.dtype) for a in list(arrs) + lands]
    outs = _pcall(body, name=name, out_shape=tuple([pltpu.SemaphoreType.DMA(())] * (2 * nc) + thru + [_sds((SUBLANES, LANES), F32)]),
                  in_specs=[HBM] * nt + [ANY] * (len(ins) - nt),
                  out_specs=tuple([SEM] * (2 * nc) + [HBM] * nt + [pl.BlockSpec(memory_space=pltpu.VMEM)]),
                  input_output_aliases={i: 2 * nc + i for i in range(nt)},
                  compiler_params=pltpu.CompilerParams(has_side_effects=EFFECT))(
        *[pltpu.with_memory_space_constraint(a, pltpu.HBM) for a in ins[:nt]], *ins[nt:])
    return dict(sems=outs[:2 * nc], arrs=outs[2 * nc:2 * nc + n], lands=outs[2 * nc + n:2 * nc + nt], token=outs[-1], plan=plan, nc=nc)


def _split_wait(name, handle, after):
    arrs, lands, sems, nc = list(handle["arrs"]), list(handle["lands"]), list(handle["sems"]), handle["nc"]
    n, nt = len(arrs), len(arrs) + len(lands)

    def body(*refs):
        sem = refs[nt:nt + 2 * nc]
        for k, (src, _, landing, peer) in enumerate(handle["plan"](refs[:n], refs[n:nt])):
            cp = pltpu.make_async_remote_copy(src_ref=src, dst_ref=landing, send_sem=sem[k], recv_sem=sem[nc + k], device_id=peer,
                                              device_id_type=MESH)
            cp.wait_send()
            cp.wait_recv()

    thru = tuple(pltpu.HBM(a.shape, a.dtype) for a in arrs + lands)
    outs = _pcall(body, name=name, out_shape=thru, in_specs=[HBM] * nt + [SEM] * (2 * nc) + [ANY], out_specs=tuple([HBM] * nt),
                  input_output_aliases={i: i for i in range(nt)},
                  compiler_params=pltpu.CompilerParams(has_side_effects=EFFECT))(*arrs, *lands, *sems, after)
    return list(outs[:n]), list(outs[n:])


WHOLE = "whole"


def _plan_chips(axes):
    def plan(src, land):
        x, y, c, chips = _place()
        idx = [2 * cx + cy for cx, cy in chips]
        part = lambda a, s: src[a] if axes[a] is WHOLE else _chip_slice(src[a], axes[a], s)
        return [(part(a, idx[j]), land[a].at[2 * x + y], land[a].at[idx[j]], (*chips[j], c))
                for a in range(len(land)) for j in range(3)]
    return plan


def _plan_sibling(half):
    def plan(src, land):
        x, y, c, _ = _place()
        lh = lambda a: src[a].shape[0] // 2
        return [(src[a].at[pl.ds((1 - c) * lh(a), lh(a))] if half else src[a], land[a], land[a], (x, y, 1 - c))
                for a in range(len(land))]
    return plan


def _chips_start(name, arrs, axes, after=None):
    part = lambda a, ax: a.shape if ax is WHOLE else a.shape[1:] if ax is None else tuple(d // 4 if i == ax else d for i, d in enumerate(a.shape))
    return _split_start(name, arrs, [_sds((4,) + part(a, ax), a.dtype) for a, ax in zip(arrs, axes)], _plan_chips(axes), 3 * len(arrs), after)


def _sibling_start(name, arrs, half, after=None):
    shp = lambda a: (a.shape[0] // 2,) + a.shape[1:] if half else a.shape
    return _split_start(name, arrs, [_sds(shp(a), a.dtype) for a in arrs], _plan_sibling(half), len(arrs), after)


def _gather_all(name, b):
    R, C = b.shape
    flips = [(dx, dy, dc) for dx in (0, 1) for dy in (0, 1) for dc in (0, 1)][1:]

    def body(b_ref, o_ref, send, recv, lsem):
        x, y, c, _ = _place()
        me = 4 * x + 2 * y + c
        peers = [(x ^ dx, y ^ dy, c ^ dc) for dx, dy, dc in flips]
        mine = pltpu.make_async_copy(b_ref, o_ref.at[me], lsem.at[0])
        mine.start()
        cps = [_remote(b_ref, o_ref.at[me], send, recv, k, peer) for k, peer in enumerate(peers)]
        for cp in cps:
            cp.start()
        for k, (px, py, pc) in enumerate(peers):
            _remote(b_ref, o_ref.at[4 * px + 2 * py + pc], send, recv, k, (px, py, pc)).wait_recv()
        for cp in cps:
            cp.wait_send()
        mine.wait()

    return _comm_call(name, body, [_sds((8, R, C), b.dtype)], 7, 1, b)[0]


def _block_rows(rows, width):
    return _tile(rows, max(SUBLANES, (1 << 19) // width), SUBLANES)


def _add_half(name, g, got):
    L, A, B = g.shape
    Lh = L // 2
    tq = _block_rows(A, B)

    def body(c_ref, g_ref, r_ref, o_ref):
        o_ref[...] = (g_ref[...] + r_ref[...]).astype(BF16)

    spec = pltpu.PrefetchScalarGridSpec(
        num_scalar_prefetch=1, grid=(Lh, A // tq),
        in_specs=[pl.BlockSpec((1, tq, B), lambda l, i, c_ref: (c_ref[0] * Lh + l, i, 0)),
                  pl.BlockSpec((1, tq, B), lambda l, i, c_ref: (l, i, 0))],
        out_specs=pl.BlockSpec((1, tq, B), lambda l, i, c_ref: (l, i, 0)))
    return _pcall(body, name=name, grid_spec=spec, out_shape=_sds((Lh, A, B), BF16),
                  compiler_params=_cp("arbitrary", "arbitrary"))(lax.axis_index("c").reshape(1).astype(jnp.int32), g, got)


def _sum_slots(name, a):
    n, R, C = a.shape
    tq = _block_rows(R, n * C)

    def fn(i, a_ref, o_ref):
        t = a_ref[0].astype(F32)
        for s in range(1, n):
            t = t + a_ref[s].astype(F32)
        o_ref[...] = t

    return _rows(name, fn, R, tq, (a,), [pl.BlockSpec((n, tq, C), lambda i: (0, i, 0))], _sds((R, C), F32), _rb(tq, C))


def _adam_update(w, g, m, v):
    mn = ADAM_B1 * m + (1.0 - ADAM_B1) * g
    vn = ADAM_B2 * v + (1.0 - ADAM_B2) * (g * g)
    m_hat = mn / (1.0 - ADAM_B1 ** ADAM_STEP)
    v_hat = vn / (1.0 - ADAM_B2 ** ADAM_STEP)
    return -ADAM_LR * (m_hat / (jnp.sqrt(v_hat) + ADAM_EPS) + ADAM_WD * w), mn, vn


def _adamw(name, w, g, m, v):
    R, C = w.shape
    tq = _tile(R, 256, SUBLANES)

    def fn(i, w_ref, g_ref, m_ref, v_ref, d_ref, mo_ref, vo_ref):
        d_ref[...], mo_ref[...], vo_ref[...] = _adam_update(w_ref[...], g_ref[...], m_ref[...], v_ref[...])

    r, o = _rb(tq, C), _sds((R, C), F32)
    return _rows(name, fn, R, tq, (w, g, m, v), [r, r, r, r], (o, o, o), (r, r, r))


def _adamw_halves(name, w, mine, theirs, m, v, l0, prev=None):
    L, A, B = w.shape
    Lh = mine.shape[0]
    tq = _tile(A, 256, SUBLANES)

    def body(c_ref, w_ref, a_ref, b_ref, m_ref, v_ref, *refs):
        g_ref, d_ref, mo_ref, vo_ref = refs[-4:]
        is_mine = pl.program_id(0) // Lh == c_ref[0]
        g = jnp.where(is_mine, a_ref[...], b_ref[...])
        g_ref[...] = g
        d_ref[...], mo_ref[...], vo_ref[...] = _adam_update(w_ref[...], g, m_ref[...], v_ref[...])

    full = pl.BlockSpec((1, tq, B), lambda l, i, c_ref: (l0 + l, i, 0))
    half = pl.BlockSpec((1, tq, B), lambda l, i, c_ref: (l % Lh, i, 0))
    o = _sds((L, A, B), F32)
    prev = list(prev or ())
    spec = pltpu.PrefetchScalarGridSpec(num_scalar_prefetch=1, grid=(2 * Lh, A // tq), in_specs=[full, half, half, full, full] + [ANY] * len(prev),
                                        out_specs=(full, full, full, full))
    return _pcall(body, name=name, grid_spec=spec, out_shape=(o, o, o, o), input_output_aliases={6 + i: i for i in range(len(prev))},
                  compiler_params=_cp("arbitrary", "arbitrary"))(lax.axis_index("c").reshape(1).astype(jnp.int32), w, mine, theirs, m, v, *prev)


def _pack(arrs, width, lead=()):
    nl = len(lead)
    flat = jnp.concatenate([a.reshape(lead + (-1,)) for a in arrs], axis=nl)
    n = flat.shape[-1]
    unit = PACK_ROWS * width
    tot = -(-n // unit) * unit
    flat = jnp.pad(flat, [(0, 0)] * nl + [(0, tot - n)])
    return flat.reshape(lead + (tot // width, width))


def _unpack(buf, shapes, lead=()):
    flat = buf.reshape(lead + (-1,))
    out, off = [], 0
    for s in shapes:
        n = int(np.prod(s))
        out.append(flat[..., off:off + n].reshape(lead + tuple(s)))
        off += n
    return out


def _in_groups(W, H):
    o_sq = 4 * W + 2 * H
    o_k = o_sq + W
    o_g = o_k + 2 * KV_W
    return [(0, 4 * W), (o_sq, o_k), (o_g, o_g + 2 * W), (o_k, o_g), (4 * W, o_sq)]


def _relayout_in(shards, W, H):
    c4 = sum(hi - lo for lo, hi in _in_groups(W, H)) // 4
    parts = []
    for lo, hi in _in_groups(W, H):
        for s in range(4):
            a, b = max(lo, s * c4), min(hi, (s + 1) * c4)
            if a < b:
                parts.append(shards[s][:, a - s * c4:b - s * c4])
    parts.append(jnp.zeros((shards.shape[1], BA_W - 2 * H), shards.dtype))
    return jnp.concatenate(parts, axis=1)


def _shard_in(d, W, H):
    groups = _in_groups(W, H)
    starts = [sum(hi - lo for lo, hi in groups[:i]) for i in range(len(groups))]
    stored = sorted(zip(groups, starts))
    c4 = sum(hi - lo for lo, hi in groups) // 4
    out = []
    for s in range(4):
        parts = []
        for (lo, hi), at in stored:
            a, b = max(lo, s * c4), min(hi, (s + 1) * c4)
            if a < b:
                parts.append(d[:, :, at + a - lo:at + b - lo])
        out.append(jnp.concatenate(parts, axis=2))
    return jnp.stack(out)


def _lane_row(vals, at):
    return jnp.pad(vals, (at, LANES - at - vals.shape[0]))[None]


def _layer_fwd(x, lw, tabs, W, H, more=None, h=None, g1_next=None):
    D = x.shape[1]
    cbk = 7 * W // LANES
    if h is None:
        h = _pre_norm(x, lw["g1"])
    proj = _mm("mm_in", h, lw["win"], "nn", BF16, tm=4096, tn=512)
    ba = _mm("mm_ba", h, lw["win"][:, 7 * W + 2 * KV_W:], "nn", F32)
    qkv = _dn_prep(proj, lw["conv"], W)
    beta_b, g_b = _dn_gates(ba, lw["alog"], lw["dt"], H)
    o, st = _delta_fwd(qkv, beta_b, g_b, H, DELTA_CB, DELTA_HB)
    oa = _dn_out(o, proj, lw["ng"], W, 3)
    ob = _swa_fwd(proj, tabs, lw["sinks"], W, 4, cbk)
    if more is not None:
        lw.update(more(ob))
    ya = _mm("mm_up_dn", oa, lw["wup_dn"], "nn", BF16)
    yb = _mm("mm_up_sw", ob, lw["wup_sw"], "nn", BF16)
    mixin = _mix(proj, ya, yb, D, 5)
    mix = _mm("mm_o", mixin, lw["wo"], "nn", F32)
    x1, h2 = _post_mix(x, mix, lw["g2"], lw["g3"])
    f1, act = _mm("mm_ff1", h2, lw["wff1"], "nn", tn=512, out_dtypes=(BF16, BF16), epi=lambda acc: (acc, jnp.square(jnp.maximum(acc, 0.0))))
    ff = _mm("mm_ff2", act, lw["wff2"], "nn", F32)
    x2, h_next = _post_mlp(x1, ff, lw["g4"], lw["g4"] if g1_next is None else g1_next)
    saved = dict(x=x, h=h, proj=proj, ba=ba, qkv=qkv, beta_b=beta_b, g_b=g_b, o=o, st=st, oa=oa, ob=ob, ya=ya, yb=yb,
                 mixin=mixin, mix=mix, x1=x1, h2=h2, f1=f1, act=act, ff=ff)
    return x2, h_next, saved


def _layer_bwd(dx2, lw, sv, tabs, W, H, l, big, weights_done=None):
    D = dx2.shape[1]
    cbk = 7 * W // LANES
    big = dict(big)
    dff, dg4 = _post_mlp_bwd(sv["ff"], lw["g4"], dx2)
    df1 = _mm("mm_ff2_dx", dff, lw["wff2"], "nt", BF16, extras=(sv["f1"],),
              epi=lambda acc, f1: (acc * 2.0 * jnp.maximum(f1.astype(F32), 0.0),))
    big["w_ff2"] = _mm("mm_ff2_dw", sv["act"], dff, "tn", tm=1024, tk=2048, slab=(big["w_ff2"], l))
    dh2 = _mm("mm_ff1_dx", df1, lw["wff1"], "nt", F32)
    big["w_ff1"] = _mm("mm_ff1_dw", sv["h2"], df1, "tn", tk=2048, slab=(big["w_ff1"], l))
    dx1, dmix, dg3, dg2 = _mid_bwd(sv["x1"], lw["g3"], dh2, dx2, sv["mix"], lw["g2"])
    dmixin = _mm("mm_o_dx", dmix, lw["wo"], "nt", BF16)
    big["w_o"] = _mm("mm_o_dw", sv["mixin"], dmix, "tn", tk=2048, slab=(big["w_o"], l))
    dya, dyb, dga, dgb = _mix_bwd(sv["proj"], sv["ya"], sv["yb"], dmixin, D, 5)
    doa = _mm("mm_up_dn_dx", dya, lw["wup_dn"], "nt", BF16)
    big["w_up_dn"] = _mm("mm_up_dn_dw", sv["oa"], dya, "tn", tk=2048, slab=(big["w_up_dn"], l))
    dob = _mm("mm_up_sw_dx", dyb, lw["wup_sw"], "nt", BF16)
    big["w_up_sw"] = _mm("mm_up_sw_dw", sv["ob"], dyb, "tn", tk=2048, slab=(big["w_up_sw"], l))
    do, dz, dng = _dn_out_bwd(sv["o"], sv["proj"], lw["ng"], doa, W, 3)
    dqkvn, dbeta_b, dg_b = _delta_bwd(sv["qkv"], sv["beta_b"], sv["g_b"], sv["st"], do, H, DELTA_CB, DELTA_HB)
    dba, dalog, ddt = _dn_gates_bwd(sv["ba"], lw["alog"], lw["dt"], dbeta_b, dg_b, H)
    dc, dconv = _dn_prep_bwd_a(sv["proj"], lw["conv"], dqkvn, W)
    dqkv = _dn_prep_bwd_b(dc, lw["conv"], W)
    dq_sw, dkc, dkp, dvc, dvp, dsk = _swa_bwd(sv["proj"], tabs, lw["sinks"], dob, W, 4, cbk)
    dkv = _swa_kv_combine(dkc, dkp, dvc, dvp, tabs)
    dproj = jnp.concatenate([dqkv, dz, dq_sw, dga, dgb, dkv, dba], axis=1)
    big["w_in"] = _mm("mm_in_dw", sv["h"], dproj, "tn", tn=768, tk=4096, slab=(big["w_in"], l))
    win = lw["win"]
    if weights_done is not None:
        win = win + weights_done(big).astype(BF16)
    dh = _mm("mm_in_dx", dproj, win, "nt", F32, tk=768)
    dx, dg1 = _pre_norm_bwd(sv["x"], lw["g1"], dh, dx1)
    grads = dict(pre_mix_g=dg1[0], dn_conv_w=dconv, dn_a_log=dalog[0, H:2 * H], dn_dt_bias=ddt[0, H:2 * H], dn_norm_g=dng[0],
                 sw_sinks=dsk[:SW_Q_HEADS, 0], post_mix_g=dg2[0], pre_mlp_g=dg3[0], post_mlp_g=dg4[0])
    return dx, grads, big


_WEIGHTS = ["pre_mix_g", "w_in", "dn_conv_w", "dn_a_log", "dn_dt_bias", "dn_norm_g", "sw_sinks", "w_up_dn", "w_up_sw", "w_o",
            "post_mix_g", "pre_mlp_g", "w_ff1", "w_ff2", "post_mlp_g"]
_BIG = {"w_in": 2, "w_up_dn": 1, "w_up_sw": 1, "w_o": 1, "w_ff1": 2, "w_ff2": 1}
_SMALL = [n for n in _WEIGHTS if n not in _BIG]


def _step(P):
    x, target = P["x"][0], P["loss_target"][0]
    S, D = x.shape
    L = P["pre_mix_g"].shape[0]
    H, W = DN_HEADS, DN_HEADS * DN_DK
    assert W == D == SW_Q_HEADS * SW_HD and KV_W == LANES
    me = 2 * lax.axis_index("x") + lax.axis_index("y")

    assert L % 4 == 0
    names = list(_BIG) + ["dn_conv_w"]
    local = [P[n].astype(BF16) for n in _BIG] + [P["dn_conv_w"]]
    early_names = ("w_in", "dn_conv_w")
    tail_names = [n for n in names if n not in early_names]
    own_slot = lambda gathered, mine: [lax.dynamic_update_slice_in_dim(g, w[None], me, 0) for g, w in zip(gathered, mine)]
    gather = lambda name, arrs, after=None: _chips_start(name, arrs, [WHOLE] * len(arrs), after)
    arrived = lambda name, h, after, keys: dict(zip(keys, own_slot(*reversed(_split_wait(name, h, after)))))
    h_first = gather("weights_first_start", [a[:1] for n, a in zip(names, local) if n in early_names])
    early = arrived("weights_first_wait", h_first, x, early_names)
    h_tail = gather("weights_tail_start", [a[:1] for n, a in zip(names, local) if n in tail_names], early["w_in"])
    h_next = gather("weights_next_start", [a[1:2] for a in local], h_tail["token"])
    h_rest = gather("weights_rest_start", [a[2:] for a in local], h_next["token"])

    def head(full, l, k):
        return dict(
            g1=P["pre_mix_g"][l][None], win=_relayout_in(full["w_in"][:, k], W, H),
            conv=jnp.concatenate([full["dn_conv_w"][s, k] for s in range(4)], axis=-1),
            alog=_lane_row(P["dn_a_log"][l], H), dt=_lane_row(P["dn_dt_bias"][l], H), ng=P["dn_norm_g"][l][None],
            sinks=_lane_row(P["sw_sinks"][l], 0), g2=P["post_mix_g"][l][None], g3=P["pre_mlp_g"][l][None], g4=P["post_mlp_g"][l][None])

    def tail(full, k):
        rows = lambda n: full[n][:, k].reshape(-1, full[n].shape[-1])
        return dict(wup_dn=rows("w_up_dn"), wup_sw=rows("w_up_sw"), wo=rows("w_o"), wff2=rows("w_ff2"),
                    wff1=jnp.concatenate([full["w_ff1"][s, k] for s in range(4)], axis=-1))

    tabs = _rope_tables(P["positions"].reshape(S, 1))
    lws = [head(early, 0, 0)]
    lws[0]["g1"] = lws[0]["g1"] + h_rest["token"][0, 0]

    saved, h = [], None
    for l in range(L):
        if l == 1:
            late = arrived("weights_next_wait", h_next, x, names)
            lws.append({**head(late, 1, 0), **tail(late, 0)})
        if l == 2:
            late = arrived("weights_rest_wait", h_rest, x, names)
            lws.extend({**head(late, k + 2, k), **tail(late, k)} for k in range(L - 2))
        first_tail = lambda after: tail(arrived("weights_tail_wait", h_tail, after, tail_names), 0)
        g1_next = P["pre_mix_g"][l + 1][None] if l + 1 < L else None
        x, h, sv = _layer_fwd(x, lws[l], tabs, W, H, first_tail if l == 0 else None, h, g1_next)
        saved.append(sv)
    loss_row, dx = _loss_head(x, target)

    Lb = L // 2
    layer_grads = [None] * L
    F = 4 * P["w_ff1"].shape[2]
    per_layer = dict(w_in=(D, 7 * W + 2 * KV_W + BA_W), w_up_dn=(W, D), w_up_sw=(W, D), w_o=(D, D), w_ff1=(D, F), w_ff2=(F, D))
    batch = [{n: lax.empty((Lb,) + per_layer[n], F32) for n in _BIG} for _ in range(2)]
    axes = [None if n == "w_in" else ax for n, ax in _BIG.items()]

    def pair_sums(tag, h_swap, after):
        g, got = _split_wait("grad_swap_wait_" + tag, h_swap, after)
        part = {n: _add_half("grad_pair_add_%s_%s" % (tag, n), a, r) for n, a, r in zip(_BIG, g, got)}
        return [_shard_in(part[n], W, H) if n == "w_in" else part[n] for n in _BIG]

    def chip_sums(tag, h_scat, after):
        parts, slots = _split_wait("grad_scatter_wait_" + tag, h_scat, after)
        halves = []
        for n, s, a, ax in zip(_BIG, slots, parts, axes):
            s = lax.dynamic_update_slice_in_dim(s, _own_part(a, ax, me)[None], me, 0)
            halves.append(_sum_slots("grad_chip_sum_%s_%s" % (tag, n), s.reshape(4, -1, s.shape[-1])).reshape(s.shape[1:]))
        h = _sibling_start("grad_share_start_" + tag, halves, False)
        return _split_wait("grad_share_wait_" + tag, h, halves[0])

    swaps = {}

    def swap_start(tag):
        def hook(big):
            swaps[tag] = _sibling_start("grad_swap_start_" + tag, [big[n] for n in _BIG], True)
            return swaps[tag]["token"][0, 0]
        return hook

    for l in reversed(range(L)):
        hook = swap_start("hi") if l == Lb else swap_start("lo") if l == 0 else None
        dx, layer_grads[l], batch[l // Lb] = _layer_bwd(dx, lws[l], saved[l], tabs, W, H, l % Lb, batch[l // Lb], hook)
        if l == Lb - 1:
            h_scat_hi = _chips_start("grad_scatter_start_hi", pair_sums("hi", swaps["hi"], dx), axes)
            if l > 0:
                lws[l - 1]["g4"] = lws[l - 1]["g4"] + h_scat_hi["token"][0, 0]

    grads = {n: jnp.stack([layer_grads[l][n] for l in range(L)]) for n in _SMALL}
    small_shapes = [(1,)] + [grads[n].shape for n in _SMALL]
    slots = _gather_all("small_gather", _pack([loss_row[0, :1]] + [grads[n] for n in _SMALL], LANES))
    h_scat_lo = _chips_start("grad_scatter_start_lo", pair_sums("lo", swaps["lo"], slots), axes)
    tot = _sum_slots("small_sum", slots + h_scat_lo["token"][0, 0])
    small = _unpack(tot, small_shapes)
    loss = small[0][0]
    gsum, delta, new_m, new_v = dict(zip(_SMALL, small[1:])), {}, {}, {}
    cw = P["dn_conv_w"].shape[2]
    gsum["dn_conv_w"] = lax.dynamic_slice_in_dim(gsum["dn_conv_w"], me * cw, cw, axis=2)
    sm_shapes = [P[n].shape for n in _SMALL]
    outs = _adamw("adamw_small", *(_pack([src[pre + n] for n in _SMALL], LANES)
                                   for src, pre in ((P, ""), (gsum, ""), (P, "m_"), (P, "v_"))))
    for d, o in zip((delta, new_m, new_v), outs):
        d.update(zip(_SMALL, _unpack(o, sm_shapes)))

    upper = {n: _adamw_halves("adamw_hi_" + n, P[n], mine, their, P["m_" + n], P["v_" + n], Lb)
             for n, mine, their in zip(_BIG, *chip_sums("hi", h_scat_hi, outs[0]))}
    for n, mine, their in zip(_BIG, *chip_sums("lo", h_scat_lo, upper["w_in"][0])):
        gsum[n], delta[n], new_m[n], new_v[n] = _adamw_halves("adamw_lo_" + n, P[n], mine, their, P["m_" + n], P["v_" + n], 0, upper[n])

    return (loss, dx[None], *[gsum[n] for n in _WEIGHTS], *[delta[n] for n in _WEIGHTS],
            *[new_m[n] for n in _WEIGHTS], *[new_v[n] for n in _WEIGHTS])


def kernel(x, positions, pre_mix_g, w_in, dn_conv_w, dn_a_log, dn_dt_bias, dn_norm_g, sw_sinks, w_up_dn, w_up_sw, w_o, post_mix_g, pre_mlp_g, w_ff1, w_ff2, post_mlp_g, loss_target, m_pre_mix_g, m_w_in, m_dn_conv_w, m_dn_a_log, m_dn_dt_bias, m_dn_norm_g, m_sw_sinks, m_w_up_dn, m_w_up_sw, m_w_o, m_post_mix_g, m_pre_mlp_g, m_w_ff1, m_w_ff2, m_post_mlp_g, v_pre_mix_g, v_w_in, v_dn_conv_w, v_dn_a_log, v_dn_dt_bias, v_dn_norm_g, v_sw_sinks, v_w_up_dn, v_w_up_sw, v_w_o, v_post_mix_g, v_pre_mlp_g, v_w_ff1, v_w_ff2, v_post_mlp_g):
    vals = (x, positions, pre_mix_g, w_in, dn_conv_w, dn_a_log, dn_dt_bias, dn_norm_g, sw_sinks, w_up_dn, w_up_sw, w_o, post_mix_g, pre_mlp_g, w_ff1, w_ff2, post_mlp_g, loss_target, m_pre_mix_g, m_w_in, m_dn_conv_w, m_dn_a_log, m_dn_dt_bias, m_dn_norm_g, m_sw_sinks, m_w_up_dn, m_w_up_sw, m_w_o, m_post_mix_g, m_pre_mlp_g, m_w_ff1, m_w_ff2, m_post_mlp_g, v_pre_mix_g, v_w_in, v_dn_conv_w, v_dn_a_log, v_dn_dt_bias, v_dn_norm_g, v_sw_sinks, v_w_up_dn, v_w_up_sw, v_w_o, v_post_mix_g, v_pre_mlp_g, v_w_ff1, v_w_ff2, v_post_mlp_g)
    names = ["x", "positions"] + _WEIGHTS + ["loss_target"] + ["m_" + n for n in _WEIGHTS] + ["v_" + n for n in _WEIGHTS]
    return _step(dict(zip(names, vals)))
```

```python
import numpy as np
import jax
import jax.numpy as jnp
from jax import lax
from jax.experimental import pallas as pl
from jax.experimental.pallas import tpu as pltpu

F32, BF16 = jnp.float32, jnp.bfloat16
MESH = pl.DeviceIdType.MESH

DN_HEADS = 8
DN_DK = 128
DN_CONV = 4
DN_CHUNK = 64
SW_Q_HEADS = 16
SW_KV_HEADS = 2
SW_HD = 64
SW_BLOCK = 128
ROPE_THETA = 500000.0
ROT_DIM = SW_HD // 4
EPS = 1e-6
ADAM_LR, ADAM_B1, ADAM_B2, ADAM_EPS, ADAM_WD, ADAM_STEP = 0.001, 0.9, 0.999, 1e-08, 0.01, 10

LANES = 128
SUBLANES = 8
VMEM_LIMIT = 48 * 1024 * 1024
KV_W = SW_KV_HEADS * SW_HD
BA_W = 256
PACK_ROWS = 512
DELTA_CB = 4
DELTA_HB = 8


def _pcall(body, **kw):
    return pl.pallas_call(body, **kw)


def _cp(*sem):
    return pltpu.CompilerParams(dimension_semantics=sem, vmem_limit_bytes=VMEM_LIMIT)


def _tile(n, pref, unit=LANES):
    if n <= pref:
        return n
    t = (pref // unit) * unit
    while t > unit and n % t:
        t -= unit
    assert n % t == 0, (n, pref)
    return t


def _sds(shape, dtype):
    return jax.ShapeDtypeStruct(tuple(shape), dtype)


_DIMS = {"nn": ((1,), (0,)), "nt": ((1,), (1,)), "tn": ((0,), (0,))}


def _mm(name, a, b, mode, out_dtype=F32, tm=2048, tn=1024, tk=1024, extras=(), epi=None, out_dtypes=None, slab=None):
    if mode == "nn":
        (M, K), (_, N) = a.shape, b.shape
    elif mode == "nt":
        (M, K), (N, _) = a.shape, b.shape
    else:
        (K, M), (_, N) = a.shape, b.shape
    tm, tn, tk = _tile(M, tm), _tile(N, tn), _tile(K, tk)
    nk = K // tk
    a_spec = {"nn": pl.BlockSpec((tm, tk), lambda i, j, k: (i, k)),
              "nt": pl.BlockSpec((tm, tk), lambda i, j, k: (i, k)),
              "tn": pl.BlockSpec((tk, tm), lambda i, j, k: (k, i))}[mode]
    b_spec = {"nn": pl.BlockSpec((tk, tn), lambda i, j, k: (k, j)),
              "nt": pl.BlockSpec((tn, tk), lambda i, j, k: (j, k)),
              "tn": pl.BlockSpec((tk, tn), lambda i, j, k: (k, j))}[mode]
    dims = (_DIMS[mode], ((), ()))
    out_dtypes = tuple(out_dtypes or (out_dtype,))
    ne, no = len(extras), len(out_dtypes)
    o_spec = pl.BlockSpec((tm, tn), lambda i, j, k: (i, j))

    def body(*refs):
        a_ref, b_ref, ex = refs[0], refs[1], refs[2:2 + ne]
        outs = refs[-no:] if nk == 1 else refs[-1 - no:-1]
        part = lax.dot_general(a_ref[...], b_ref[...], dims, preferred_element_type=F32)

        def finish(acc):
            res = epi(acc, *[e[...] for e in ex]) if epi else (acc,)
            for o, r, dt in zip(outs, res, out_dtypes):
                if slab is None:
                    o[...] = r.astype(dt)
                else:
                    o[0] = r.astype(dt)

        if nk == 1:
            finish(part)
            return
        acc_ref, k = refs[-1], pl.program_id(2)

        @pl.when(k == 0)
        def _():
            acc_ref[...] = part

        @pl.when((k > 0) & (k < nk - 1))
        def _():
            acc_ref[...] += part

        @pl.when(k == nk - 1)
        def _():
            finish(acc_ref[...] + part)

    kw = dict(name=name, grid=(M // tm, N // tn, nk), scratch_shapes=[] if nk == 1 else [pltpu.VMEM((tm, tn), F32)],
              compiler_params=_cp("parallel", "parallel", "arbitrary"))
    if slab is not None:
        buf, l = slab
        return _pcall(body, in_specs=[a_spec, b_spec, ANY], out_specs=pl.BlockSpec((1, tm, tn), lambda i, j, k: (l, i, j)),
                      out_shape=_sds(buf.shape, buf.dtype), input_output_aliases={2: 0}, **kw)(a, b, buf)
    out = _pcall(body, in_specs=[a_spec, b_spec] + [o_spec] * ne, out_specs=tuple(o_spec for _ in out_dtypes),
                 out_shape=tuple(_sds((M, N), dt) for dt in out_dtypes), **kw)(a, b, *extras)
    return out if no > 1 else out[0]


def _rows(name, fn, n_rows, tq, ins, in_specs, out_shapes, out_specs):
    def body(*refs):
        fn(pl.program_id(0), *refs)

    return _pcall(body, name=name, grid=(n_rows // tq,), in_specs=in_specs, out_specs=out_specs,
                  out_shape=out_shapes, compiler_params=_cp("arbitrary"))(*ins)


def _rb(tq, w, cb=0):
    return pl.BlockSpec((tq, w), lambda i: (i, cb))


def _full(shape):
    return pl.BlockSpec(tuple(shape), lambda *_: (0,) * len(shape))


def _rms_fwd(x, g):
    r = lax.rsqrt(jnp.mean(x * x, axis=-1, keepdims=True) + EPS)
    return x * r * g


def _rms_bwd(x, g, dy):
    r = lax.rsqrt(jnp.mean(x * x, axis=-1, keepdims=True) + EPS)
    xh = x * r
    t = dy * g
    dx = r * (t - xh * jnp.mean(t * xh, axis=-1, keepdims=True))
    return dx, jnp.sum(dy * xh, axis=0, keepdims=True)


def _acc(i, ref, val):
    @pl.when(i == 0)
    def _():
        ref[...] = val

    @pl.when(i > 0)
    def _():
        ref[...] += val


def _sigmoid(x):
    return 0.5 * jnp.tanh(0.5 * x) + 0.5


def _pre_norm(x, g):
    S, D = x.shape
    tq = _tile(S, 512, SUBLANES)

    def fn(i, x_ref, g_ref, h_ref):
        h_ref[...] = _rms_fwd(x_ref[...], g_ref[...]).astype(BF16)

    return _rows("pre_norm", fn, S, tq, (x, g), [_rb(tq, D), _full((1, D))], _sds((S, D), BF16), _rb(tq, D))


def _post_mix(x, mix, g2, g3):
    S, D = x.shape
    tq = _tile(S, 512, SUBLANES)

    def fn(i, x_ref, m_ref, g2_ref, g3_ref, x1_ref, h2_ref):
        x1 = x_ref[...] + _rms_fwd(m_ref[...], g2_ref[...])
        x1_ref[...] = x1
        h2_ref[...] = _rms_fwd(x1, g3_ref[...]).astype(BF16)

    return _rows("post_mix", fn, S, tq, (x, mix, g2, g3), [_rb(tq, D), _rb(tq, D), _full((1, D)), _full((1, D))],
                 (_sds((S, D), F32), _sds((S, D), BF16)), (_rb(tq, D), _rb(tq, D)))


def _post_mlp(x1, ff, g4, g1_next):
    S, D = x1.shape
    tq = _tile(S, 512, SUBLANES)

    def fn(i, x_ref, f_ref, g_ref, gn_ref, o_ref, h_ref):
        x2 = x_ref[...] + _rms_fwd(f_ref[...], g_ref[...])
        o_ref[...] = x2
        h_ref[...] = _rms_fwd(x2, gn_ref[...]).astype(BF16)

    r, f = _rb(tq, D), _full((1, D))
    return _rows("post_mlp", fn, S, tq, (x1, ff, g4, g1_next), [r, r, f, f], (_sds((S, D), F32), _sds((S, D), BF16)), (r, r))


def _loss_head(y, target):
    S, D = y.shape
    tq = _tile(S, 512, SUBLANES)

    def fn(i, y_ref, t_ref, l_ref, d_ref):
        e = y_ref[...] - t_ref[...]
        d_ref[...] = e * (1.0 / D)
        part = jnp.sum(jnp.sum(e * e, axis=1, keepdims=True), axis=0, keepdims=True) * (0.5 / D)
        _acc(i, l_ref, jnp.broadcast_to(part, (1, LANES)))

    return _rows("loss_head", fn, S, tq, (y, target), [_rb(tq, D), _rb(tq, D)],
                 (_sds((1, LANES), F32), _sds((S, D), F32)), (_full((1, LANES)), _rb(tq, D)))


def _post_mlp_bwd(ff, g4, dx2):
    S, D = ff.shape
    tq = _tile(S, 512, SUBLANES)

    def fn(i, f_ref, g_ref, d_ref, o_ref, dg_ref):
        dx, dg = _rms_bwd(f_ref[...], g_ref[...], d_ref[...])
        o_ref[...] = dx.astype(BF16)
        _acc(i, dg_ref, dg)

    return _rows("post_mlp_bwd", fn, S, tq, (ff, g4, dx2), [_rb(tq, D), _full((1, D)), _rb(tq, D)],
                 (_sds((S, D), BF16), _sds((1, D), F32)), (_rb(tq, D), _full((1, D))))


def _mid_bwd(x1, g3, dh2, dx2, mix, g2):
    S, D = x1.shape
    tq = _tile(S, 256, SUBLANES)

    def fn(i, x_ref, g3_ref, dh_ref, dx2_ref, m_ref, g2_ref, dx1_ref, dm_ref, dg3_ref, dg2_ref):
        d, dg3 = _rms_bwd(x_ref[...], g3_ref[...], dh_ref[...])
        dx1 = dx2_ref[...] + d
        dx1_ref[...] = dx1
        dm, dg2 = _rms_bwd(m_ref[...], g2_ref[...], dx1)
        dm_ref[...] = dm.astype(BF16)
        _acc(i, dg3_ref, dg3)
        _acc(i, dg2_ref, dg2)

    r, f = _rb(tq, D), _full((1, D))
    return _rows("mid_bwd", fn, S, tq, (x1, g3, dh2, dx2, mix, g2), [r, f, r, r, r, f],
                 (_sds((S, D), F32), _sds((S, D), BF16), _sds((1, D), F32), _sds((1, D), F32)), (r, r, f, f))


def _pre_norm_bwd(x, g1, dh, dx1):
    S, D = x.shape
    tq = _tile(S, 512, SUBLANES)

    def fn(i, x_ref, g_ref, dh_ref, dx1_ref, dx_ref, dg_ref):
        d, dg = _rms_bwd(x_ref[...], g_ref[...], dh_ref[...])
        dx_ref[...] = dx1_ref[...] + d
        _acc(i, dg_ref, dg)

    r, f = _rb(tq, D), _full((1, D))
    return _rows("pre_norm_bwd", fn, S, tq, (x, g1, dh, dx1), [r, f, r, r], (_sds((S, D), F32), _sds((1, D), F32)), (r, f))


def _mix(proj, ya, yb, D, cb_a):
    S = ya.shape[0]
    tq = _tile(S, 256, SUBLANES)

    def fn(i, ga_ref, gb_ref, ya_ref, yb_ref, o_ref):
        ga, gb, ya, yb = (r[...].astype(F32) for r in (ga_ref, gb_ref, ya_ref, yb_ref))
        o_ref[...] = (_sigmoid(ga) * ya + _sigmoid(gb) * yb).astype(BF16)

    return _rows("mix", fn, S, tq, (proj, proj, ya, yb), [_rb(tq, D, cb_a), _rb(tq, D, cb_a + 1), _rb(tq, D), _rb(tq, D)],
                 _sds((S, D), BF16), _rb(tq, D))


def _mix_bwd(proj, ya, yb, dmixin, D, cb_a):
    S = ya.shape[0]
    tq = _tile(S, 256, SUBLANES)

    def fn(i, ga_ref, gb_ref, ya_ref, yb_ref, d_ref, dya_ref, dyb_ref, dga_ref, dgb_ref):
        ga, gb, ya, yb, d = (r[...].astype(F32) for r in (ga_ref, gb_ref, ya_ref, yb_ref, d_ref))
        sa, sb = _sigmoid(ga), _sigmoid(gb)
        dya_ref[...] = (d * sa).astype(BF16)
        dyb_ref[...] = (d * sb).astype(BF16)
        dga_ref[...] = (d * ya * sa * (1.0 - sa)).astype(BF16)
        dgb_ref[...] = (d * yb * sb * (1.0 - sb)).astype(BF16)

    r = _rb(tq, D)
    o = _sds((S, D), BF16)
    return _rows("mix_bwd", fn, S, tq, (proj, proj, ya, yb, dmixin), [_rb(tq, D, cb_a), _rb(tq, D, cb_a + 1), r, r, r],
                 (o, o, o, o), (r, r, r, r))


HALO = 16


def _shift_down(xe, k, tq):
    return pltpu.roll(xe, k, 0)[HALO:HALO + tq]


def _conv_pre(cur_ref, halo_ref, w_ref, i, tq):
    x = cur_ref[...].astype(F32)
    halo = jnp.where(i > 0, halo_ref[...].astype(F32), 0.0)
    xe = jnp.concatenate([halo, x], axis=0)
    xs = [x] + [_shift_down(xe, k, tq) for k in range(1, DN_CONV)]
    w = w_ref[...]
    c = sum(w[DN_CONV - 1 - k:DN_CONV - k, :] * xs[k] for k in range(DN_CONV))
    return c, xs


def _dn_prep(proj, conv_w, W):
    S = proj.shape[0]
    tq = _tile(S, 256, HALO)
    hb = tq // HALO

    def body(cur_ref, halo_ref, w_ref, o_ref):
        j, i = pl.program_id(0), pl.program_id(1)
        c, _ = _conv_pre(cur_ref, halo_ref, w_ref, i, tq)
        y = c * _sigmoid(c)

        @pl.when(j == 2)
        def _():
            o_ref[...] = y

        @pl.when(j < 2)
        def _():
            scale = jnp.where(j == 0, DN_DK ** -0.5, 1.0)
            for h in range(W // DN_DK):
                sl = slice(h * DN_DK, (h + 1) * DN_DK)
                yh = y[:, sl]
                o_ref[:, sl] = yh * (lax.rsqrt(jnp.sum(yh * yh, axis=-1, keepdims=True) + EPS) * scale)

    return _pcall(body, name="dn_prep", grid=(3, S // tq),
                  in_specs=[pl.BlockSpec((tq, W), lambda j, i: (i, j)),
                            pl.BlockSpec((HALO, W), lambda j, i: (jnp.maximum(i * hb - 1, 0), j)),
                            pl.BlockSpec((DN_CONV, W), lambda j, i: (0, j))],
                  out_specs=pl.BlockSpec((tq, W), lambda j, i: (i, j)), out_shape=_sds((S, 3 * W), F32),
                  compiler_params=_cp("arbitrary", "arbitrary"))(proj, proj, conv_w)


def _dn_prep_bwd_a(proj, conv_w, dqkv, W):
    S = proj.shape[0]
    tq = _tile(S, 256, HALO)
    hb = tq // HALO

    def body(cur_ref, halo_ref, w_ref, d_ref, dc_ref, dw_ref):
        j, i = pl.program_id(0), pl.program_id(1)
        c, xs = _conv_pre(cur_ref, halo_ref, w_ref, i, tq)
        sg = _sigmoid(c)

        @pl.when(j == 2)
        def _():
            dc_ref[...] = d_ref[0]

        @pl.when(j < 2)
        def _():
            y = c * sg
            scale = jnp.where(j == 0, DN_DK ** -0.5, 1.0)
            for h in range(W // DN_DK):
                sl = slice(h * DN_DK, (h + 1) * DN_DK)
                yh, dh = y[:, sl], d_ref[0, :, sl]
                rs = lax.rsqrt(jnp.sum(yh * yh, axis=-1, keepdims=True) + EPS)
                yn = yh * rs
                dc_ref[:, sl] = (scale * rs) * (dh - yn * jnp.sum(dh * yn, axis=-1, keepdims=True))

        dc = dc_ref[...] * (sg * (1.0 + c * (1.0 - sg)))
        dc_ref[...] = dc
        dw = jnp.concatenate([jnp.sum(dc * xs[DN_CONV - 1 - r], axis=0, keepdims=True) for r in range(DN_CONV)], axis=0)
        _acc(i, dw_ref, dw)

    return _pcall(body, name="dn_prep_bwd_a", grid=(3, S // tq),
                  in_specs=[pl.BlockSpec((tq, W), lambda j, i: (i, j)),
                            pl.BlockSpec((HALO, W), lambda j, i: (jnp.maximum(i * hb - 1, 0), j)),
                            pl.BlockSpec((DN_CONV, W), lambda j, i: (0, j)),
                            pl.BlockSpec((1, tq, W), lambda j, i: (j, i, 0))],
                  out_specs=(pl.BlockSpec((tq, W), lambda j, i: (i, j)), pl.BlockSpec((DN_CONV, W), lambda j, i: (0, j))),
                  out_shape=(_sds((S, 3 * W), F32), _sds((DN_CONV, 3 * W), F32)),
                  compiler_params=_cp("arbitrary", "arbitrary"))(proj, proj, conv_w, dqkv)


def _dn_prep_bwd_b(dc, conv_w, W):
    S = dc.shape[0]
    tq = _tile(S, 256, SUBLANES)
    hb = tq // SUBLANES
    nblk = S // tq

    def body(cur_ref, nxt_ref, w_ref, o_ref):
        i = pl.program_id(1)
        d = cur_ref[...]
        nxt = jnp.where(i < nblk - 1, nxt_ref[...], 0.0)
        de = jnp.concatenate([d, nxt], axis=0)
        w = w_ref[...]
        out = w[DN_CONV - 1:DN_CONV, :] * d
        for k in range(1, DN_CONV):
            out = out + w[DN_CONV - 1 - k:DN_CONV - k, :] * pltpu.roll(de, tq + SUBLANES - k, 0)[0:tq]
        o_ref[...] = out.astype(BF16)

    return _pcall(body, name="dn_prep_bwd_b", grid=(3, nblk),
                  in_specs=[pl.BlockSpec((tq, W), lambda j, i: (i, j)),
                            pl.BlockSpec((SUBLANES, W), lambda j, i: (jnp.minimum((i + 1) * hb, S // SUBLANES - 1), j)),
                            pl.BlockSpec((DN_CONV, W), lambda j, i: (0, j))],
                  out_specs=pl.BlockSpec((tq, W), lambda j, i: (i, j)), out_shape=_sds((S, 3 * W), BF16),
                  compiler_params=_cp("arbitrary", "arbitrary"))(dc, dc, conv_w)


def _gate_terms(ba, al, dt):
    u = ba + dt
    sp = jnp.maximum(u, 0.0) + jnp.log(1.0 + jnp.exp(-jnp.abs(u)))
    return _sigmoid(ba), -jnp.exp(al) * sp, u


def _dn_gates(ba, alog_row, dt_row, H):
    S = ba.shape[0]
    tq = _tile(S, 512, SUBLANES)
    W = H * DN_DK

    def fn(i, ba_ref, al_ref, dt_ref, be_ref, g_ref):
        bet, gg, _ = _gate_terms(ba_ref[...], al_ref[...], dt_ref[...])
        for h in range(H):
            sl = slice(h * DN_DK, (h + 1) * DN_DK)
            be_ref[:, sl] = jnp.broadcast_to(bet[:, h:h + 1], (tq, DN_DK))
            g_ref[:, sl] = jnp.broadcast_to(gg[:, H + h:H + h + 1], (tq, DN_DK))

    return _rows("dn_gates", fn, S, tq, (ba, alog_row, dt_row), [_rb(tq, LANES), _full((1, LANES)), _full((1, LANES))],
                 (_sds((S, W), F32), _sds((S, W), F32)), (_rb(tq, W), _rb(tq, W)))


def _dn_gates_bwd(ba, alog_row, dt_row, dbeta_b, dg_b, H):
    S = ba.shape[0]
    tq = _tile(S, 512, SUBLANES)
    W = H * DN_DK

    def fn(i, ba_ref, al_ref, dt_ref, db_ref, dg_ref, o_ref, dal_ref, ddt_ref):
        bet, gg, u = _gate_terms(ba_ref[...], al_ref[...], dt_ref[...])
        lane = lax.broadcasted_iota(jnp.int32, (tq, LANES), 1)
        d = jnp.zeros((tq, LANES), F32)
        for h in range(H):
            d = jnp.where(lane == h, db_ref[:, h * DN_DK:h * DN_DK + 1], d)
            d = jnp.where(lane == H + h, dg_ref[:, h * DN_DK:h * DN_DK + 1], d)
        is_a = (lane >= H) & (lane < 2 * H)
        da = jnp.where(is_a, d * (-jnp.exp(al_ref[...]) * _sigmoid(u)), 0.0)
        dlog = jnp.where(lane < H, d * bet * (1.0 - bet), da)
        o_ref[...] = jnp.concatenate([dlog, jnp.zeros((tq, BA_W - LANES), F32)], axis=1).astype(BF16)
        _acc(i, dal_ref, jnp.sum(jnp.where(is_a, d * gg, 0.0), axis=0, keepdims=True))
        _acc(i, ddt_ref, jnp.sum(da, axis=0, keepdims=True))

    f = _full((1, LANES))
    return _rows("dn_gates_bwd", fn, S, tq, (ba, alog_row, dt_row, dbeta_b, dg_b),
                 [_rb(tq, LANES), f, f, _rb(tq, W), _rb(tq, W)],
                 (_sds((S, BA_W), BF16), _sds((1, LANES), F32), _sds((1, LANES), F32)), (_rb(tq, BA_W), f, f))


def _dn_out(o, proj, ng, W, cb_z):
    S = o.shape[0]
    tq = _tile(S, 256, SUBLANES)

    def fn(i, o_ref, z_ref, g_ref, y_ref):
        for h in range(W // DN_DK):
            sl = slice(h * DN_DK, (h + 1) * DN_DK)
            z = z_ref[:, sl].astype(F32)
            y_ref[:, sl] = (_rms_fwd(o_ref[:, sl], g_ref[...]) * (z * _sigmoid(z))).astype(BF16)

    return _rows("dn_out", fn, S, tq, (o, proj, ng), [_rb(tq, W), _rb(tq, W, cb_z), _full((1, DN_DK))], _sds((S, W), BF16), _rb(tq, W))


def _dn_out_bwd(o, proj, ng, dy, W, cb_z):
    S = o.shape[0]
    tq = _tile(S, 256, SUBLANES)

    def fn(i, o_ref, z_ref, g_ref, d_ref, do_ref, dz_ref, dg_ref):
        g = g_ref[...]
        dg = jnp.zeros((1, DN_DK), F32)
        for h in range(W // DN_DK):
            sl = slice(h * DN_DK, (h + 1) * DN_DK)
            oh, z, d = o_ref[:, sl], z_ref[:, sl].astype(F32), d_ref[:, sl].astype(F32)
            sg = _sigmoid(z)
            dn = d * (z * sg)
            dz_ref[:, sl] = (d * _rms_fwd(oh, g) * (sg * (1.0 + z * (1.0 - sg)))).astype(BF16)
            dx, dgh = _rms_bwd(oh, g, dn)
            do_ref[:, sl] = dx
            dg = dg + dgh
        _acc(i, dg_ref, dg)

    r = _rb(tq, W)
    return _rows("dn_out_bwd", fn, S, tq, (o, proj, ng, dy), [r, _rb(tq, W, cb_z), _full((1, DN_DK)), r],
                 (_sds((S, W), F32), _sds((S, W), BF16), _sds((1, DN_DK), F32)), (r, r, _full((1, DN_DK))))


def _bdot(a, b, mode="nn"):
    return lax.dot_general(a.astype(BF16), b.astype(BF16), (_DIMS[mode], ((), ())), preferred_element_type=F32)


def _rsum(x):
    return jnp.broadcast_to(jnp.sum(x, axis=-1, keepdims=True), x.shape)


def _dot3(a, b, mode="nn"):
    ah, bh = a.astype(BF16), b.astype(BF16)
    al, bl = (a - ah.astype(F32)).astype(BF16), (b - bh.astype(F32)).astype(BF16)
    d = lambda x, y: lax.dot_general(x, y, (_DIMS[mode], ((), ())), preferred_element_type=F32)
    return d(ah, bh) + (d(al, bh) + d(ah, bl))


def _cumsum_rows(x, reverse=False):
    n = x.shape[0]
    row = lax.broadcasted_iota(jnp.int32, x.shape, 0)
    s = 1
    while s < n:
        if reverse:
            x = x + jnp.where(row < n - s, pltpu.roll(x, n - s, 0), 0.0)
        else:
            x = x + jnp.where(row >= s, pltpu.roll(x, s, 0), 0.0)
        s *= 2
    return x


def _each(f, *lists):
    return [f(*a) for a in zip(*lists)]


def _delta_local(qs, ks, vs, bes, grs):
    C = DN_CHUNK
    ri = lax.broadcasted_iota(jnp.int32, (C, C), 0)
    ci = lax.broadcasted_iota(jnp.int32, (C, C), 1)
    causal, strict = ri >= ci, ri > ci
    gcs = [_cumsum_rows(g) for g in grs]
    decays = [jnp.where(causal, jnp.exp(jnp.where(causal, gc[:, :C] - gc.T[:C, :], 0.0)), 0.0) for gc in gcs]
    egs = [jnp.exp(gc) for gc in gcs]
    eks = [jnp.exp(gc[C - 1:C, :] - gc) for gc in gcs]
    gams = [jnp.exp(gc[C - 1:C, :]) for gc in gcs]
    kbs = _each(lambda k, be: k * be, ks, bes)
    kks = _each(lambda kb, k: _bdot(kb, k, "nt"), kbs, ks)
    nls = _each(lambda kk, dc: jnp.where(strict, -kk * dc, 0.0), kks, decays)
    eye = (ri == ci).astype(F32)
    ts = [eye + nl for nl in nls]
    pws = [_dot3(nl, nl) for nl in nls]
    for s in range(4):
        both = _each(lambda t, pw: _dot3(jnp.concatenate([t, pw], axis=0), pw), ts, pws)
        ts = _each(lambda t, b: t + b[:C], ts, both)
        pws = [b[C:] for b in both]
    ts = _each(lambda t, pw: t + _dot3(t, pw), ts, pws)
    vbs = _each(lambda v, be: v * be, vs, bes)
    kbes = _each(lambda kb, eg: kb * eg, kbs, egs)
    uws = _each(lambda t, vb, kbe: _dot3(t, jnp.concatenate([vb, kbe], axis=1)), ts, vbs, kbes)
    us, ws = [uw[:, :DN_DK] for uw in uws], [uw[:, DN_DK:] for uw in uws]
    qks = _each(lambda q, k: _bdot(q, k, "nt"), qs, ks)
    return dict(decay=decays, eg=egs, ek=eks, gam=gams, kb=kbs, kk=kks, t=ts, vb=vbs, kbe=kbes, u=us, w=ws, qk=qks,
                a=_each(lambda qk, dc: qk * dc, qks, decays), qd=_each(lambda q, eg: q * eg, qs, egs),
                kd=_each(lambda k, ek: k * ek, ks, eks), strict=strict)


def _delta_items(refs, CB, HB):
    C, dk = DN_CHUNK, DN_DK
    return [[r[c * C:(c + 1) * C, h * dk:(h + 1) * dk] for h in range(HB) for c in range(CB)] for r in refs]


def _delta_fwd(qkv, beta_b, g_b, H, CB, HB):
    S = qkv.shape[0]
    C, dk = DN_CHUNK, DN_DK
    N = S // C
    R = CB * C
    G = H // HB

    def body(q_ref, k_ref, v_ref, b_ref, g_ref, o_ref, st_ref, s_ref):
        @pl.when(pl.program_id(1) == 0)
        def _():
            s_ref[...] = jnp.zeros((HB, dk, dk), F32)

        L = _delta_local(*_delta_items((q_ref, k_ref, v_ref, b_ref, g_ref), CB, HB))
        ss = [s_ref[h] for h in range(HB)]
        for c in range(CB):
            it = [h * CB + c for h in range(HB)]
            for h in range(HB):
                st_ref[h, c] = ss[h]
            wq = [_bdot(jnp.concatenate([L["w"][i], L["qd"][i]], axis=0), s) for i, s in zip(it, ss)]
            vns = [L["u"][i] - x[:C] for i, x in zip(it, wq)]
            outs = [x[C:] + _bdot(L["a"][i], vn) for i, x, vn in zip(it, wq, vns)]
            ss = [s * L["gam"][i] + _bdot(L["kd"][i], vn, "tn") for i, s, vn in zip(it, ss, vns)]
            for h in range(HB):
                o_ref[c * C:(c + 1) * C, h * dk:(h + 1) * dk] = outs[h]
        for h in range(HB):
            s_ref[h] = ss[h]

    blk = lambda off: pl.BlockSpec((R, HB * dk), lambda h, n: (n, off + h))
    return _pcall(body, name="delta_fwd", grid=(G, N // CB),
                  in_specs=[blk(0), blk(G), blk(2 * G), blk(0), blk(0)],
                  out_specs=(blk(0), pl.BlockSpec((HB, CB, dk, dk), lambda h, n: (h, n, 0, 0))),
                  out_shape=(_sds((S, H * dk), F32), _sds((H, N, dk, dk), F32)),
                  scratch_shapes=[pltpu.VMEM((HB, dk, dk), F32)],
                  compiler_params=_cp("arbitrary", "arbitrary"))(qkv, qkv, qkv, beta_b, g_b)


def _delta_bwd(qkv, beta_b, g_b, states, do, H, CB, HB):
    S = qkv.shape[0]
    C, dk = DN_CHUNK, DN_DK
    N = S // C
    R = CB * C
    NB = N // CB
    G = H // HB

    def body(q_ref, k_ref, v_ref, b_ref, g_ref, st_ref, do_ref, dqkv_ref, db_ref, dg_ref, ds_ref):
        @pl.when(pl.program_id(1) == 0)
        def _():
            ds_ref[...] = jnp.zeros((HB, dk, dk), F32)

        qs, ks, vs, bes, grs, dos = _delta_items((q_ref, k_ref, v_ref, b_ref, g_ref, do_ref), CB, HB)
        L = _delta_local(qs, ks, vs, bes, grs)
        ts, decays, kbs, egs, eks, gams, qds, kds = (L[n] for n in ("t", "decay", "kb", "eg", "ek", "gam", "qd", "kd"))
        s0s = [st_ref[h, c] for h in range(HB) for c in range(CB)]
        vns = _each(lambda u, w, s0: u - _bdot(w, s0), L["u"], L["w"], s0s)
        pre_dvn = _each(lambda a, d: _bdot(a, d, "tn"), L["a"], dos)
        pre_ds = _each(lambda qd, d: _bdot(qd, d, "tn"), qds, dos)
        das = _each(lambda d, vn: _bdot(d, vn, "nt"), dos, vns)
        ds = [ds_ref[h] for h in range(HB)]
        ds1s, dvns = [None] * (HB * CB), [None] * (HB * CB)
        for c in reversed(range(CB)):
            it = [h * CB + c for h in range(HB)]
            new = [pre_dvn[i] + _bdot(kds[i], d) for i, d in zip(it, ds)]
            for i, d, dv in zip(it, ds, new):
                ds1s[i], dvns[i] = d, dv
            ds = [pre_ds[i] + d * gams[i] - _bdot(L["w"][i], dv, "tn") for i, d, dv in zip(it, ds, new)]
        for h in range(HB):
            ds_ref[h] = ds[h]
        dkds = _each(lambda vn, d1: _bdot(vn, d1, "nt"), vns, ds1s)
        dgams = _each(lambda s0, d1: jnp.sum(jnp.sum(s0 * d1, axis=1, keepdims=True), axis=0, keepdims=True), s0s, ds1s)
        ost = _each(lambda d, dv, s0: _bdot(jnp.concatenate([d, dv], axis=0), s0, "nt"), dos, dvns, s0s)
        dqds, dws = [x[:C] for x in ost], [-x[C:] for x in ost]
        dvw = _each(lambda dv, dw: jnp.concatenate([dv, dw], axis=1), dvns, dws)
        tdvw = _each(lambda t, x: _dot3(t, x, "tn"), ts, dvw)
        dvbs, dkbes = [x[:, :dk] for x in tdvw], [x[:, dk:] for x in tdvw]
        dts = _each(lambda x, vb, kbe: _dot3(x, jnp.concatenate([vb, kbe], axis=1), "nt"), dvw, L["vb"], L["kbe"])
        tmp = _each(lambda dt, t: _dot3(dt, t, "nt"), dts, ts)
        dls = _each(lambda t, x: -_dot3(t, x, "tn"), ts, tmp)
        ms = _each(lambda dl, dc: jnp.where(L["strict"], dl * dc, 0.0), dls, decays)
        mas = _each(lambda da, dc: da * dc, das, decays)
        dkbs = _each(lambda m, k, dkbe, eg: _bdot(m, k) + dkbe * eg, ms, ks, dkbes, egs)
        dks = _each(lambda m, kb, ma, q, dkd, ek, dkb, be: _bdot(m, kb, "tn") + _bdot(ma, q, "tn") + dkd * ek + dkb * be,
                    ms, kbs, mas, qs, dkds, eks, dkbs, bes)
        dqs = _each(lambda ma, k, dqd, eg: _bdot(ma, k) + dqd * eg, mas, ks, dqds, egs)
        es = _each(lambda m, kk, ma, qk: m * kk + ma * qk, ms, L["kk"], mas, L["qk"])
        ones = jnp.ones((C, dk), BF16)
        row = lax.broadcasted_iota(jnp.int32, (C, dk), 0)
        for i in range(HB * CB):
            h, c = divmod(i, CB)
            rs, cs = slice(c * C, (c + 1) * C), slice(h * dk, (h + 1) * dk)
            e = es[i]
            e_hi = e.astype(BF16)
            col = _bdot(e_hi, ones, "tn") + _bdot(e - e_hi.astype(F32), ones, "tn")
            t_kd = _rsum(dkds[i] * kds[i])
            dgc = (jnp.broadcast_to(jnp.sum(e, axis=1, keepdims=True), (C, dk)) - col + _rsum(dqds[i] * qds[i]) - t_kd
                   + _rsum(dkbes[i] * L["kbe"][i]))
            dglast = jnp.sum(t_kd[:, 0:1], axis=0, keepdims=True) + dgams[i] * gams[i][:, 0:1]
            dgc = dgc + jnp.where(row == C - 1, dglast, 0.0)
            dqkv_ref[0, rs, cs] = dqs[i]
            dqkv_ref[1, rs, cs] = dks[i]
            dqkv_ref[2, rs, cs] = dvbs[i] * bes[i]
            db_ref[rs, cs] = _rsum(dkbs[i] * ks[i]) + _rsum(dvbs[i] * vs[i])
            dg_ref[rs, cs] = _cumsum_rows(dgc, reverse=True)

    blk = lambda off: pl.BlockSpec((R, HB * dk), lambda h, n: (NB - 1 - n, off + h))
    W = H * dk
    return _pcall(body, name="delta_bwd", grid=(G, NB),
                  in_specs=[blk(0), blk(G), blk(2 * G), blk(0), blk(0),
                            pl.BlockSpec((HB, CB, dk, dk), lambda h, n: (h, NB - 1 - n, 0, 0)), blk(0)],
                  out_specs=(pl.BlockSpec((3, R, HB * dk), lambda h, n: (0, NB - 1 - n, h)), blk(0), blk(0)),
                  out_shape=(_sds((3, S, W), F32), _sds((S, W), F32), _sds((S, W), F32)),
                  scratch_shapes=[pltpu.VMEM((HB, dk, dk), F32)],
                  compiler_params=_cp("arbitrary", "arbitrary"))(qkv, qkv, qkv, beta_b, g_b, states, do)


def _rope_consts():
    lane = np.arange(LANES) % SW_HD
    half = ROT_DIM // 2
    inv = (ROPE_THETA ** (-np.arange(half, dtype=np.float32) * np.float32(2.0 / ROT_DIM))).astype(np.float32)
    freq = np.where(lane < ROT_DIM, inv[lane % half], 0.0).astype(np.float32)
    lo = (lane < half).astype(np.float32)
    hi = ((lane >= half) & (lane < ROT_DIM)).astype(np.float32)
    return jnp.asarray(np.stack([freq, -lo, hi] + [np.zeros(LANES, np.float32)] * 5))


def _rope_tables(pos_col):
    S = pos_col.shape[0]
    tq = _tile(S, 1024, SUBLANES)

    def fn(i, p_ref, c_ref, cos_ref, s1_ref, s2_ref):
        ang = p_ref[...].astype(F32) * c_ref[0:1, :]
        sn = jnp.sin(ang)
        cos_ref[...] = jnp.cos(ang)
        s1_ref[...] = sn * c_ref[1:2, :]
        s2_ref[...] = sn * c_ref[2:3, :]

    o, r = _sds((S, LANES), F32), _rb(tq, LANES)
    return _rows("rope_tables", fn, S, tq, (pos_col, _rope_consts()), [_rb(tq, 1), _full((SUBLANES, LANES))], (o, o, o), (r, r, r))


def _wide(a, w):
    return a if w == LANES else jnp.tile(a, (1, w // LANES))


def _rope(x, cos, s1, s2):
    w, h = x.shape[1], ROT_DIM // 2
    return x * _wide(cos, w) + pltpu.roll(x, w - h, 1) * _wide(s1, w) + pltpu.roll(x, h, 1) * _wide(s2, w)


def _unrope(d, cos, s1, s2):
    w, h = d.shape[1], ROT_DIM // 2
    return d * _wide(cos, w) + pltpu.roll(d * _wide(s1, w), h, 1) + pltpu.roll(d * _wide(s2, w), w - h, 1)


def _swa_setup(n, q_ref, kc_ref, kp_ref, vc_ref, vp_ref, tc, tp):
    B = SW_BLOCK
    qr = _rope(q_ref[...].astype(F32), tc[0][...], tc[1][...], tc[2][...]) * (SW_HD ** -0.5)
    kw = jnp.concatenate([_rope(kp_ref[...].astype(F32), tp[0][...], tp[1][...], tp[2][...]),
                          _rope(kc_ref[...].astype(F32), tc[0][...], tc[1][...], tc[2][...])], axis=0)
    vw = jnp.concatenate([vp_ref[...], vc_ref[...]], axis=0).astype(F32)
    lane = lax.broadcasted_iota(jnp.int32, (2 * B, LANES), 1)
    heads = []
    for hk in range(SW_KV_HEADS):
        kh, vh = kw[:, hk * SW_HD:(hk + 1) * SW_HD], vw[:, hk * SW_HD:(hk + 1) * SW_HD]
        kk, vv = jnp.concatenate([kh, kh], axis=1), jnp.concatenate([vh, vh], axis=1)
        heads.append(tuple(jnp.where(sel, t, 0.0).astype(BF16) for t in (kk, vv) for sel in (lane < SW_HD, lane >= SW_HD)))
    prev = lax.broadcasted_iota(jnp.int32, (B, B), 1) > lax.broadcasted_iota(jnp.int32, (B, B), 0)
    return qr, heads, (prev, jnp.where(prev & (n == 0), -1e30, 0.0)), lane


def _fold(x, prev):
    return jnp.where(prev, x[:, :SW_BLOCK], x[:, SW_BLOCK:])


def _unfold(x, prev):
    return jnp.concatenate([jnp.where(prev, x, 0.0), jnp.where(prev, 0.0, x)], axis=1)


SWA_GROUPS = 4
SWA_GROUPS_BWD = 2


def _swa_probs(items, qs, heads, fold, sk_ref, G2):
    prev, bias = fold
    ss = [_fold(_bdot(qs[j], heads[j // G2][half], "nt"), prev) + bias for j, half in items]
    sks = [sk_ref[0:1, 2 * j + half:2 * j + half + 1] for j, half in items]
    ms = [jnp.maximum(jnp.max(s, axis=-1, keepdims=True), sk) for s, sk in zip(ss, sks)]
    ps = [jnp.exp(s - m) for s, m in zip(ss, ms)]
    es = [jnp.exp(sk - m) for sk, m in zip(sks, ms)]
    inv = [1.0 / (jnp.sum(p, axis=-1, keepdims=True) + e) for p, e in zip(ps, es)]
    return [p * i for p, i in zip(ps, inv)], [e * i for e, i in zip(es, inv)]


def _swa_specs(W, cb_q, cb_k):
    B = SW_BLOCK
    assert (W // LANES) % SWA_GROUPS == 0 and (W // LANES) % SWA_GROUPS_BWD == 0
    cur = lambda w, cb: pl.BlockSpec((B, w), lambda n: (n, cb))
    prv = lambda w, cb: pl.BlockSpec((B, w), lambda n: (jnp.maximum(n - 1, 0), cb))
    specs = [cur(W, cb_q), cur(LANES, cb_k), prv(LANES, cb_k), cur(LANES, cb_k + 1), prv(LANES, cb_k + 1)]
    return specs + [cur(LANES, 0)] * 3 + [prv(LANES, 0)] * 3 + [_full((1, LANES))]


def _swa_fwd(proj, tabs, sinks_row, W, cb_q, cb_k):
    S = proj.shape[0]
    G2 = SW_Q_HEADS // SW_KV_HEADS // 2

    def body(q_ref, kc_ref, kp_ref, vc_ref, vp_ref, c0, c1, c2, p0, p1, p2, sk_ref, o_ref):
        n = pl.program_id(0)
        qr, heads, fold, _ = _swa_setup(n, q_ref, kc_ref, kp_ref, vc_ref, vp_ref, (c0, c1, c2), (p0, p1, p2))
        qs = [qr[:, j * LANES:(j + 1) * LANES].astype(BF16) for j in range(W // LANES)]
        for j0 in range(0, W // LANES, SWA_GROUPS):
            items = [(j, half) for j in range(j0, j0 + SWA_GROUPS) for half in range(2)]
            probs, _ = _swa_probs(items, qs, heads, fold, sk_ref, G2)
            pv = [_bdot(_unfold(p, fold[0]), heads[j // G2][2 + half]) for p, (j, half) in zip(probs, items)]
            for g in range(SWA_GROUPS):
                o_ref[:, (j0 + g) * LANES:(j0 + g + 1) * LANES] = (pv[2 * g] + pv[2 * g + 1]).astype(BF16)

    t = tuple(tabs)
    return _pcall(body, name="swa_fwd", grid=(S // SW_BLOCK,), in_specs=_swa_specs(W, cb_q, cb_k),
                  out_specs=pl.BlockSpec((SW_BLOCK, W), lambda n: (n, 0)), out_shape=_sds((S, W), BF16),
                  compiler_params=_cp("arbitrary"))(proj, proj, proj, proj, proj, *t, *t, sinks_row)


def _swa_bwd(proj, tabs, sinks_row, do, W, cb_q, cb_k):
    S = proj.shape[0]
    B = SW_BLOCK
    G2 = SW_Q_HEADS // SW_KV_HEADS // 2
    SKR = -(-SW_Q_HEADS // SUBLANES) * SUBLANES

    def body(q_ref, kc_ref, kp_ref, vc_ref, vp_ref, c0, c1, c2, p0, p1, p2, sk_ref, do_ref,
             dq_ref, dkc_ref, dkp_ref, dvc_ref, dvp_ref, dsk_ref):
        n = pl.program_id(0)
        qr, heads, fold, lane = _swa_setup(n, q_ref, kc_ref, kp_ref, vc_ref, vp_ref, (c0, c1, c2), (p0, p1, p2))
        prev = fold[0]

        @pl.when(n == 0)
        def _():
            dsk_ref[...] = jnp.zeros((SKR, LANES), F32)

        acc_k = [jnp.zeros((2 * B, LANES), F32) for _ in range(SW_KV_HEADS)]
        acc_v = [jnp.zeros((2 * B, LANES), F32) for _ in range(SW_KV_HEADS)]
        qs = [qr[:, j * LANES:(j + 1) * LANES].astype(BF16) for j in range(W // LANES)]
        dos = [do_ref[:, j * LANES:(j + 1) * LANES].astype(BF16) for j in range(W // LANES)]
        dqs = []
        for j0 in range(0, W // LANES, SWA_GROUPS_BWD):
            items = [(j, half) for j in range(j0, j0 + SWA_GROUPS_BWD) for half in range(2)]
            probs, psinks = _swa_probs(items, qs, heads, fold, sk_ref, G2)
            dps = [_fold(_bdot(dos[j], heads[j // G2][2 + half], "nt"), prev) for j, half in items]
            deltas = [jnp.sum(p * dp, axis=-1, keepdims=True) for p, dp in zip(probs, dps)]
            dss = [_unfold(p * (dp - dl), prev).astype(BF16) for p, dp, dl in zip(probs, dps, deltas)]
            pbs = [_unfold(p, prev).astype(BF16) for p in probs]
            dqp = [_bdot(ds, heads[j // G2][half]) for ds, (j, half) in zip(dss, items)]
            dkk = [_bdot(ds, qs[j], "tn") for ds, (j, half) in zip(dss, items)]
            dvv = [_bdot(p, dos[j], "tn") for p, (j, half) in zip(pbs, items)]
            for i, (j, half) in enumerate(items):
                hk, h = j // G2, 2 * j + half
                sel = (lane < SW_HD) if half == 0 else (lane >= SW_HD)
                acc_k[hk] = acc_k[hk] + jnp.where(sel, dkk[i], 0.0)
                acc_v[hk] = acc_v[hk] + jnp.where(sel, dvv[i], 0.0)
                dsk_ref[h:h + 1, :] += jnp.broadcast_to(-jnp.sum(psinks[i] * deltas[i], axis=0, keepdims=True), (1, LANES))
            dqs += [dqp[2 * g] + dqp[2 * g + 1] for g in range(SWA_GROUPS_BWD)]
        dq = jnp.concatenate(dqs, axis=1) * (SW_HD ** -0.5)
        dq_ref[...] = _unrope(dq, c0[...], c1[...], c2[...]).astype(BF16)
        fold = lambda a: a[:, :SW_HD] + a[:, SW_HD:]
        dkw = jnp.concatenate([fold(a) for a in acc_k], axis=1)
        dvw = jnp.concatenate([fold(a) for a in acc_v], axis=1)
        dkp_ref[...], dkc_ref[...] = dkw[:B], dkw[B:]
        dvp_ref[...], dvc_ref[...] = dvw[:B], dvw[B:]

    t = tuple(tabs)
    blk = lambda w: pl.BlockSpec((B, w), lambda n: (n, 0))
    o = _sds((S, LANES), F32)
    return _pcall(body, name="swa_bwd", grid=(S // B,), in_specs=_swa_specs(W, cb_q, cb_k) + [blk(W)],
                  out_specs=(blk(W), blk(LANES), blk(LANES), blk(LANES), blk(LANES), _full((SKR, LANES))),
                  out_shape=(_sds((S, W), BF16), o, o, o, o, _sds((SKR, LANES), F32)),
                  compiler_params=_cp("arbitrary"))(proj, proj, proj, proj, proj, *t, *t, sinks_row, do)


def _swa_kv_combine(dkc, dkp, dvc, dvp, tabs):
    S = dkc.shape[0]
    B = SW_BLOCK
    nb = S // B

    def fn(n, kc_ref, kp_ref, vc_ref, vp_ref, c0, c1, c2, o_ref):
        more = n < nb - 1
        dk = kc_ref[...] + jnp.where(more, kp_ref[...], 0.0)
        dv = vc_ref[...] + jnp.where(more, vp_ref[...], 0.0)
        o_ref[...] = jnp.concatenate([_unrope(dk, c0[...], c1[...], c2[...]), dv], axis=1).astype(BF16)

    cur = _rb(B, LANES)
    nxt = pl.BlockSpec((B, LANES), lambda n: (jnp.minimum(n + 1, nb - 1), 0))
    return _rows("swa_kv_combine", fn, S, B, (dkc, dkp, dvc, dvp, *tabs), [cur, nxt, cur, nxt, cur, cur, cur],
                 _sds((S, 2 * LANES), BF16), _rb(B, 2 * LANES))


ANY = pl.BlockSpec(memory_space=pl.ANY)


def _place():
    x, y, c = lax.axis_index("x"), lax.axis_index("y"), lax.axis_index("c")
    return x, y, c, [(1 - x, y), (x, 1 - y), (1 - x, 1 - y)]


def _comm_call(name, body, out_shapes, n_sems, n_local, *ins):
    return _pcall(body, name=name, out_shape=tuple(out_shapes), in_specs=[ANY] * len(ins), out_specs=tuple(ANY for _ in out_shapes),
                  scratch_shapes=[pltpu.SemaphoreType.DMA((n_sems,)), pltpu.SemaphoreType.DMA((n_sems,)),
                                  pltpu.SemaphoreType.DMA((n_local,))])(*ins)


def _remote(src, dst, send, recv, k, to):
    return pltpu.make_async_remote_copy(src_ref=src, dst_ref=dst, send_sem=send.at[k], recv_sem=recv.at[k], device_id=to,
                                        device_id_type=MESH)


def _chip_slice(ref, axis, s):
    if axis is None:
        return ref.at[s]
    q = ref.shape[axis] // 4
    start = s * q if isinstance(s, int) else pl.multiple_of(s * q, q)
    return ref.at[tuple([slice(None)] * axis + [pl.ds(start, q)])]


def _own_part(a, axis, me):
    if axis is None:
        return lax.dynamic_index_in_dim(a, me, 0, keepdims=False)
    q = a.shape[axis] // 4
    return lax.dynamic_slice_in_dim(a, me * q, q, axis)


HBM = pl.BlockSpec(memory_space=pltpu.HBM)
SEM = pl.BlockSpec(memory_space=pltpu.SEMAPHORE)
EFFECT = pltpu.SideEffectType.DATAFLOW_SIDE_EFFECTING


def _split_start(name, arrs, land_shapes, plan, nc, after=None):
    n = len(arrs)
    lands = [lax.empty(s.shape, s.dtype) for s in land_shapes]
    ins = list(arrs) + lands + ([] if after is None else [after])

    def body(*refs):
        outs = refs[len(ins):]
        for k, (src, dst, _, peer) in enumerate(plan(refs[:n], refs[n:n + len(lands)])):
            pltpu.make_async_remote_copy(src_ref=src, dst_ref=dst, send_sem=outs[k], recv_sem=outs[nc + k], device_id=peer,
                                         device_id_type=MESH).start()
        outs[-1][...] = jnp.zeros((SUBLANES, LANES), F32)

    nt = n + len(lands)
    thru = [pltpu.HBM(a.shape, a.dtype) for a in list(arrs) + lands]
    outs = _pcall(body, name=name, out_shape=tuple([pltpu.SemaphoreType.DMA(())] * (2 * nc) + thru + [_sds((SUBLANES, LANES), F32)]),
                  in_specs=[HBM] * nt + [ANY] * (len(ins) - nt),
                  out_specs=tuple([SEM] * (2 * nc) + [HBM] * nt + [pl.BlockSpec(memory_space=pltpu.VMEM)]),
                  input_output_aliases={i: 2 * nc + i for i in range(nt)},
                  compiler_params=pltpu.CompilerParams(has_side_effects=EFFECT))(
        *[pltpu.with_memory_space_constraint(a, pltpu.HBM) for a in ins[:nt]], *ins[nt:])
    return dict(sems=outs[:2 * nc], arrs=outs[2 * nc:2 * nc + n], lands=outs[2 * nc + n:2 * nc + nt], token=outs[-1], plan=plan, nc=nc)


def _split_wait(name, handle, after):
    arrs, lands, sems, nc = list(handle["arrs"]), list(handle["lands"]), list(handle["sems"]), handle["nc"]
    n, nt = len(arrs), len(arrs) + len(lands)

    def body(*refs):
        sem = refs[nt:nt + 2 * nc]
        for k, (src, _, landing, peer) in enumerate(handle["plan"](refs[:n], refs[n:nt])):
            cp = pltpu.make_async_remote_copy(src_ref=src, dst_ref=landing, send_sem=sem[k], recv_sem=sem[nc + k], device_id=peer,
                                              device_id_type=MESH)
            cp.wait_send()
            cp.wait_recv()

    thru = tuple(pltpu.HBM(a.shape, a.dtype) for a in arrs + lands)
    outs = _pcall(body, name=name, out_shape=thru, in_specs=[HBM] * nt + [SEM] * (2 * nc) + [ANY], out_specs=tuple([HBM] * nt),
                  input_output_aliases={i: i for i in range(nt)},
                  compiler_params=pltpu.CompilerParams(has_side_effects=EFFECT))(*arrs, *lands, *sems, after)
    return list(outs[:n]), list(outs[n:])


WHOLE = "whole"


def _plan_chips(axes):
    def plan(src, land):
        x, y, c, chips = _place()
        idx = [2 * cx + cy for cx, cy in chips]
        part = lambda a, s: src[a] if axes[a] is WHOLE else _chip_slice(src[a], axes[a], s)
        return [(part(a, idx[j]), land[a].at[2 * x + y], land[a].at[idx[j]], (*chips[j], c))
                for a in range(len(land)) for j in range(3)]
    return plan


def _plan_sibling(half):
    def plan(src, land):
        x, y, c, _ = _place()
        lh = lambda a: src[a].shape[0] // 2
        return [(src[a].at[pl.ds((1 - c) * lh(a), lh(a))] if half else src[a], land[a], land[a], (x, y, 1 - c))
                for a in range(len(land))]
    return plan


def _chips_start(name, arrs, axes, after=None):
    part = lambda a, ax: a.shape if ax is WHOLE else a.shape[1:] if ax is None else tuple(d // 4 if i == ax else d for i, d in enumerate(a.shape))
    return _split_start(name, arrs, [_sds((4,) + part(a, ax), a.dtype) for a, ax in zip(arrs, axes)], _plan_chips(axes), 3 * len(arrs), after)


def _sibling_start(name, arrs, half, after=None):
    shp = lambda a: (a.shape[0] // 2,) + a.shape[1:] if half else a.shape
    return _split_start(name, arrs, [_sds(shp(a), a.dtype) for a in arrs], _plan_sibling(half), len(arrs), after)


def _gather_all(name, b):
    R, C = b.shape
    flips = [(dx, dy, dc) for dx in (0, 1) for dy in (0, 1) for dc in (0, 1)][1:]

    def body(b_ref, o_ref, send, recv, lsem):
        x, y, c, _ = _place()
        me = 4 * x + 2 * y + c
        peers = [(x ^ dx, y ^ dy, c ^ dc) for dx, dy, dc in flips]
        mine = pltpu.make_async_copy(b_ref, o_ref.at[me], lsem.at[0])
        mine.start()
        cps = [_remote(b_ref, o_ref.at[me], send, recv, k, peer) for k, peer in enumerate(peers)]
        for cp in cps:
            cp.start()
        for k, (px, py, pc) in enumerate(peers):
            _remote(b_ref, o_ref.at[4 * px + 2 * py + pc], send, recv, k, (px, py, pc)).wait_recv()
        for cp in cps:
            cp.wait_send()
        mine.wait()

    return _comm_call(name, body, [_sds((8, R, C), b.dtype)], 7, 1, b)[0]


def _block_rows(rows, width):
    return _tile(rows, max(SUBLANES, (1 << 19) // width), SUBLANES)


def _add_half(name, g, got):
    L, A, B = g.shape
    Lh = L // 2
    tq = _block_rows(A, B)

    def body(c_ref, g_ref, r_ref, o_ref):
        o_ref[...] = (g_ref[...] + r_ref[...]).astype(BF16)

    spec = pltpu.PrefetchScalarGridSpec(
        num_scalar_prefetch=1, grid=(Lh, A // tq),
        in_specs=[pl.BlockSpec((1, tq, B), lambda l, i, c_ref: (c_ref[0] * Lh + l, i, 0)),
                  pl.BlockSpec((1, tq, B), lambda l, i, c_ref: (l, i, 0))],
        out_specs=pl.BlockSpec((1, tq, B), lambda l, i, c_ref: (l, i, 0)))
    return _pcall(body, name=name, grid_spec=spec, out_shape=_sds((Lh, A, B), BF16),
                  compiler_params=_cp("arbitrary", "arbitrary"))(lax.axis_index("c").reshape(1).astype(jnp.int32), g, got)


def _sum_slots(name, a):
    n, R, C = a.shape
    tq = _block_rows(R, n * C)

    def fn(i, a_ref, o_ref):
        t = a_ref[0].astype(F32)
        for s in range(1, n):
            t = t + a_ref[s].astype(F32)
        o_ref[...] = t

    return _rows(name, fn, R, tq, (a,), [pl.BlockSpec((n, tq, C), lambda i: (0, i, 0))], _sds((R, C), F32), _rb(tq, C))


def _adam_update(w, g, m, v):
    mn = ADAM_B1 * m + (1.0 - ADAM_B1) * g
    vn = ADAM_B2 * v + (1.0 - ADAM_B2) * (g * g)
    m_hat = mn / (1.0 - ADAM_B1 ** ADAM_STEP)
    v_hat = vn / (1.0 - ADAM_B2 ** ADAM_STEP)
    return -ADAM_LR * (m_hat / (jnp.sqrt(v_hat) + ADAM_EPS) + ADAM_WD * w), mn, vn


def _adamw(name, w, g, m, v):
    R, C = w.shape
    tq = _tile(R, 256, SUBLANES)

    def fn(i, w_ref, g_ref, m_ref, v_ref, d_ref, mo_ref, vo_ref):
        d_ref[...], mo_ref[...], vo_ref[...] = _adam_update(w_ref[...], g_ref[...], m_ref[...], v_ref[...])

    r, o = _rb(tq, C), _sds((R, C), F32)
    return _rows(name, fn, R, tq, (w, g, m, v), [r, r, r, r], (o, o, o), (r, r, r))


def _adamw_halves(name, w, mine, theirs, m, v, l0, prev=None):
    L, A, B = w.shape
    Lh = mine.shape[0]
    tq = _tile(A, 256, SUBLANES)

    def body(c_ref, w_ref, a_ref, b_ref, m_ref, v_ref, *refs):
        g_ref, d_ref, mo_ref, vo_ref = refs[-4:]
        is_mine = pl.program_id(0) // Lh == c_ref[0]
        g = jnp.where(is_mine, a_ref[...], b_ref[...])
        g_ref[...] = g
        d_ref[...], mo_ref[...], vo_ref[...] = _adam_update(w_ref[...], g, m_ref[...], v_ref[...])

    full = pl.BlockSpec((1, tq, B), lambda l, i, c_ref: (l0 + l, i, 0))
    half = pl.BlockSpec((1, tq, B), lambda l, i, c_ref: (l % Lh, i, 0))
    o = _sds((L, A, B), F32)
    prev = list(prev or ())
    spec = pltpu.PrefetchScalarGridSpec(num_scalar_prefetch=1, grid=(2 * Lh, A // tq), in_specs=[full, half, half, full, full] + [ANY] * len(prev),
                                        out_specs=(full, full, full, full))
    return _pcall(body, name=name, grid_spec=spec, out_shape=(o, o, o, o), input_output_aliases={6 + i: i for i in range(len(prev))},
                  compiler_params=_cp("arbitrary", "arbitrary"))(lax.axis_index("c").reshape(1).astype(jnp.int32), w, mine, theirs, m, v, *prev)


def _pack(arrs, width, lead=()):
    nl = len(lead)
    flat = jnp.concatenate([a.reshape(lead + (-1,)) for a in arrs], axis=nl)
    n = flat.shape[-1]
    unit = PACK_ROWS * width
    tot = -(-n // unit) * unit
    flat = jnp.pad(flat, [(0, 0)] * nl + [(0, tot - n)])
    return flat.reshape(lead + (tot // width, width))


def _unpack(buf, shapes, lead=()):
    flat = buf.reshape(lead + (-1,))
    out, off = [], 0
    for s in shapes:
        n = int(np.prod(s))
        out.append(flat[..., off:off + n].reshape(lead + tuple(s)))
        off += n
    return out


def _in_groups(W, H):
    o_sq = 4 * W + 2 * H
    o_k = o_sq + W
    o_g = o_k + 2 * KV_W
    return [(0, 4 * W), (o_sq, o_k), (o_g, o_g + 2 * W), (o_k, o_g), (4 * W, o_sq)]


def _relayout_in(shards, W, H):
    c4 = sum(hi - lo for lo, hi in _in_groups(W, H)) // 4
    parts = []
    for lo, hi in _in_groups(W, H):
        for s in range(4):
            a, b = max(lo, s * c4), min(hi, (s + 1) * c4)
            if a < b:
                parts.append(shards[s][:, a - s * c4:b - s * c4])
    parts.append(jnp.zeros((shards.shape[1], BA_W - 2 * H), shards.dtype))
    return jnp.concatenate(parts, axis=1)


def _shard_in(d, W, H):
    groups = _in_groups(W, H)
    starts = [sum(hi - lo for lo, hi in groups[:i]) for i in range(len(groups))]
    stored = sorted(zip(groups, starts))
    c4 = sum(hi - lo for lo, hi in groups) // 4
    out = []
    for s in range(4):
        parts = []
        for (lo, hi), at in stored:
            a, b = max(lo, s * c4), min(hi, (s + 1) * c4)
            if a < b:
                parts.append(d[:, :, at + a - lo:at + b - lo])
        out.append(jnp.concatenate(parts, axis=2))
    return jnp.stack(out)


def _lane_row(vals, at):
    return jnp.pad(vals, (at, LANES - at - vals.shape[0]))[None]


def _layer_fwd(x, lw, tabs, W, H, more=None, h=None, g1_next=None):
    D = x.shape[1]
    cbk = 7 * W // LANES
    if h is None:
        h = _pre_norm(x, lw["g1"])
    proj = _mm("mm_in", h, lw["win"], "nn", BF16, tm=4096, tn=512)
    ba = _mm("mm_ba", h, lw["win"][:, 7 * W + 2 * KV_W:], "nn", F32)
    qkv = _dn_prep(proj, lw["conv"], W)
    beta_b, g_b = _dn_gates(ba, lw["alog"], lw["dt"], H)
    o, st = _delta_fwd(qkv, beta_b, g_b, H, DELTA_CB, DELTA_HB)
    oa = _dn_out(o, proj, lw["ng"], W, 3)
    ob = _swa_fwd(proj, tabs, lw["sinks"], W, 4, cbk)
    if more is not None:
        lw.update(more(ob))
    ya = _mm("mm_up_dn", oa, lw["wup_dn"], "nn", BF16)
    yb = _mm("mm_up_sw", ob, lw["wup_sw"], "nn", BF16)
    mixin = _mix(proj, ya, yb, D, 5)
    mix = _mm("mm_o", mixin, lw["wo"], "nn", F32)
    x1, h2 = _post_mix(x, mix, lw["g2"], lw["g3"])
    f1, act = _mm("mm_ff1", h2, lw["wff1"], "nn", tn=512, out_dtypes=(BF16, BF16), epi=lambda acc: (acc, jnp.square(jnp.maximum(acc, 0.0))))
    ff = _mm("mm_ff2", act, lw["wff2"], "nn", F32)
    x2, h_next = _post_mlp(x1, ff, lw["g4"], lw["g4"] if g1_next is None else g1_next)
    saved = dict(x=x, h=h, proj=proj, ba=ba, qkv=qkv, beta_b=beta_b, g_b=g_b, o=o, st=st, oa=oa, ob=ob, ya=ya, yb=yb,
                 mixin=mixin, mix=mix, x1=x1, h2=h2, f1=f1, act=act, ff=ff)
    return x2, h_next, saved


def _layer_bwd(dx2, lw, sv, tabs, W, H, l, big, weights_done=None):
    D = dx2.shape[1]
    cbk = 7 * W // LANES
    big = dict(big)
    dff, dg4 = _post_mlp_bwd(sv["ff"], lw["g4"], dx2)
    df1 = _mm("mm_ff2_dx", dff, lw["wff2"], "nt", BF16, extras=(sv["f1"],),
              epi=lambda acc, f1: (acc * 2.0 * jnp.maximum(f1.astype(F32), 0.0),))
    big["w_ff2"] = _mm("mm_ff2_dw", sv["act"], dff, "tn", tm=1024, tk=2048, slab=(big["w_ff2"], l))
    dh2 = _mm("mm_ff1_dx", df1, lw["wff1"], "nt", F32)
    big["w_ff1"] = _mm("mm_ff1_dw", sv["h2"], df1, "tn", tk=2048, slab=(big["w_ff1"], l))
    dx1, dmix, dg3, dg2 = _mid_bwd(sv["x1"], lw["g3"], dh2, dx2, sv["mix"], lw["g2"])
    dmixin = _mm("mm_o_dx", dmix, lw["wo"], "nt", BF16)
    big["w_o"] = _mm("mm_o_dw", sv["mixin"], dmix, "tn", tk=2048, slab=(big["w_o"], l))
    dya, dyb, dga, dgb = _mix_bwd(sv["proj"], sv["ya"], sv["yb"], dmixin, D, 5)
    doa = _mm("mm_up_dn_dx", dya, lw["wup_dn"], "nt", BF16)
    big["w_up_dn"] = _mm("mm_up_dn_dw", sv["oa"], dya, "tn", tk=2048, slab=(big["w_up_dn"], l))
    dob = _mm("mm_up_sw_dx", dyb, lw["wup_sw"], "nt", BF16)
    big["w_up_sw"] = _mm("mm_up_sw_dw", sv["ob"], dyb, "tn", tk=2048, slab=(big["w_up_sw"], l))
    do, dz, dng = _dn_out_bwd(sv["o"], sv["proj"], lw["ng"], doa, W, 3)
    dqkvn, dbeta_b, dg_b = _delta_bwd(sv["qkv"], sv["beta_b"], sv["g_b"], sv["st"], do, H, DELTA_CB, DELTA_HB)
    dba, dalog, ddt = _dn_gates_bwd(sv["ba"], lw["alog"], lw["dt"], dbeta_b, dg_b, H)
    dc, dconv = _dn_prep_bwd_a(sv["proj"], lw["conv"], dqkvn, W)
    dqkv = _dn_prep_bwd_b(dc, lw["conv"], W)
    dq_sw, dkc, dkp, dvc, dvp, dsk = _swa_bwd(sv["proj"], tabs, lw["sinks"], dob, W, 4, cbk)
    dkv = _swa_kv_combine(dkc, dkp, dvc, dvp, tabs)
    dproj = jnp.concatenate([dqkv, dz, dq_sw, dga, dgb, dkv, dba], axis=1)
    big["w_in"] = _mm("mm_in_dw", sv["h"], dproj, "tn", tn=768, tk=4096, slab=(big["w_in"], l))
    win = lw["win"]
    if weights_done is not None:
        win = win + weights_done(big).astype(BF16)
    dh = _mm("mm_in_dx", dproj, win, "nt", F32, tk=768)
    dx, dg1 = _pre_norm_bwd(sv["x"], lw["g1"], dh, dx1)
    grads = dict(pre_mix_g=dg1[0], dn_conv_w=dconv, dn_a_log=dalog[0, H:2 * H], dn_dt_bias=ddt[0, H:2 * H], dn_norm_g=dng[0],
                 sw_sinks=dsk[:SW_Q_HEADS, 0], post_mix_g=dg2[0], pre_mlp_g=dg3[0], post_mlp_g=dg4[0])
    return dx, grads, big


_WEIGHTS = ["pre_mix_g", "w_in", "dn_conv_w", "dn_a_log", "dn_dt_bias", "dn_norm_g", "sw_sinks", "w_up_dn", "w_up_sw", "w_o",
            "post_mix_g", "pre_mlp_g", "w_ff1", "w_ff2", "post_mlp_g"]
_BIG = {"w_in": 2, "w_up_dn": 1, "w_up_sw": 1, "w_o": 1, "w_ff1": 2, "w_ff2": 1}
_SMALL = [n for n in _WEIGHTS if n not in _BIG]


def _step(P):
    x, target = P["x"][0], P["loss_target"][0]
    S, D = x.shape
    L = P["pre_mix_g"].shape[0]
    H, W = DN_HEADS, DN_HEADS * DN_DK
    assert W == D == SW_Q_HEADS * SW_HD and KV_W == LANES
    me = 2 * lax.axis_index("x") + lax.axis_index("y")

    assert L % 4 == 0
    names = list(_BIG) + ["dn_conv_w"]
    local = [P[n].astype(BF16) for n in _BIG] + [P["dn_conv_w"]]
    early_names = ("w_in", "dn_conv_w")
    tail_names = [n for n in names if n not in early_names]
    own_slot = lambda gathered, mine: [lax.dynamic_update_slice_in_dim(g, w[None], me, 0) for g, w in zip(gathered, mine)]
    gather = lambda name, arrs, after=None: _chips_start(name, arrs, [WHOLE] * len(arrs), after)
    arrived = lambda name, h, after, keys: dict(zip(keys, own_slot(*reversed(_split_wait(name, h, after)))))
    h_first = gather("weights_first_start", [a[:1] for n, a in zip(names, local) if n in early_names])
    early = arrived("weights_first_wait", h_first, x, early_names)
    h_tail = gather("weights_tail_start", [a[:1] for n, a in zip(names, local) if n in tail_names], early["w_in"])
    h_next = gather("weights_next_start", [a[1:2] for a in local], h_tail["token"])
    h_rest = gather("weights_rest_start", [a[2:] for a in local], h_next["token"])

    def head(full, l, k):
        return dict(
            g1=P["pre_mix_g"][l][None], win=_relayout_in(full["w_in"][:, k], W, H),
            conv=jnp.concatenate([full["dn_conv_w"][s, k] for s in range(4)], axis=-1),
            alog=_lane_row(P["dn_a_log"][l], H), dt=_lane_row(P["dn_dt_bias"][l], H), ng=P["dn_norm_g"][l][None],
            sinks=_lane_row(P["sw_sinks"][l], 0), g2=P["post_mix_g"][l][None], g3=P["pre_mlp_g"][l][None], g4=P["post_mlp_g"][l][None])

    def tail(full, k):
        rows = lambda n: full[n][:, k].reshape(-1, full[n].shape[-1])
        return dict(wup_dn=rows("w_up_dn"), wup_sw=rows("w_up_sw"), wo=rows("w_o"), wff2=rows("w_ff2"),
                    wff1=jnp.concatenate([full["w_ff1"][s, k] for s in range(4)], axis=-1))

    tabs = _rope_tables(P["positions"].reshape(S, 1))
    lws = [head(early, 0, 0)]
    lws[0]["g1"] = lws[0]["g1"] + h_rest["token"][0, 0]

    saved, h = [], None
    for l in range(L):
        if l == 1:
            late = arrived("weights_next_wait", h_next, x, names)
            lws.append({**head(late, 1, 0), **tail(late, 0)})
        if l == 2:
            late = arrived("weights_rest_wait", h_rest, x, names)
            lws.extend({**head(late, k + 2, k), **tail(late, k)} for k in range(L - 2))
        first_tail = lambda after: tail(arrived("weights_tail_wait", h_tail, after, tail_names), 0)
        g1_next = P["pre_mix_g"][l + 1][None] if l + 1 < L else None
        x, h, sv = _layer_fwd(x, lws[l], tabs, W, H, first_tail if l == 0 else None, h, g1_next)
        saved.append(sv)
    loss_row, dx = _loss_head(x, target)

    Lb = L // 2
    layer_grads = [None] * L
    F = 4 * P["w_ff1"].shape[2]
    per_layer = dict(w_in=(D, 7 * W + 2 * KV_W + BA_W), w_up_dn=(W, D), w_up_sw=(W, D), w_o=(D, D), w_ff1=(D, F), w_ff2=(F, D))
    batch = [{n: lax.empty((Lb,) + per_layer[n], F32) for n in _BIG} for _ in range(2)]
    axes = [None if n == "w_in" else ax for n, ax in _BIG.items()]

    def pair_sums(tag, h_swap, after):
        g, got = _split_wait("grad_swap_wait_" + tag, h_swap, after)
        part = {n: _add_half("grad_pair_add_%s_%s" % (tag, n), a, r) for n, a, r in zip(_BIG, g, got)}
        return [_shard_in(part[n], W, H) if n == "w_in" else part[n] for n in _BIG]

    def chip_sums(tag, h_scat, after):
        parts, slots = _split_wait("grad_scatter_wait_" + tag, h_scat, after)
        halves = []
        for n, s, a, ax in zip(_BIG, slots, parts, axes):
            s = lax.dynamic_update_slice_in_dim(s, _own_part(a, ax, me)[None], me, 0)
            halves.append(_sum_slots("grad_chip_sum_%s_%s" % (tag, n), s.reshape(4, -1, s.shape[-1])).reshape(s.shape[1:]))
        h = _sibling_start("grad_share_start_" + tag, halves, False)
        return _split_wait("grad_share_wait_" + tag, h, halves[0])

    swaps = {}

    def swap_start(tag):
        def hook(big):
            swaps[tag] = _sibling_start("grad_swap_start_" + tag, [big[n] for n in _BIG], True)
            return swaps[tag]["token"][0, 0]
        return hook

    for l in reversed(range(L)):
        hook = swap_start("hi") if l == Lb else swap_start("lo") if l == 0 else None
        dx, layer_grads[l], batch[l // Lb] = _layer_bwd(dx, lws[l], saved[l], tabs, W, H, l % Lb, batch[l // Lb], hook)
        if l == Lb - 1:
            h_scat_hi = _chips_start("grad_scatter_start_hi", pair_sums("hi", swaps["hi"], dx), axes)
            if l > 0:
                lws[l - 1]["g4"] = lws[l - 1]["g4"] + h_scat_hi["token"][0, 0]

    grads = {n: jnp.stack([layer_grads[l][n] for l in range(L)]) for n in _SMALL}
    small_shapes = [(1,)] + [grads[n].shape for n in _SMALL]
    slots = _gather_all("small_gather", _pack([loss_row[0, :1]] + [grads[n] for n in _SMALL], LANES))
    h_scat_lo = _chips_start("grad_scatter_start_lo", pair_sums("lo", swaps["lo"], slots), axes)
    tot = _sum_slots("small_sum", slots + h_scat_lo["token"][0, 0])
    small = _unpack(tot, small_shapes)
    loss = small[0][0]
    gsum, delta, new_m, new_v = dict(zip(_SMALL, small[1:])), {}, {}, {}
    cw = P["dn_conv_w"].shape[2]
    gsum["dn_conv_w"] = lax.dynamic_slice_in_dim(gsum["dn_conv_w"], me * cw, cw, axis=2)
    sm_shapes = [P[n].shape for n in _SMALL]
    outs = _adamw("adamw_small", *(_pack([src[pre + n] for n in _SMALL], LANES)
                                   for src, pre in ((P, ""), (gsum, ""), (P, "m_"), (P, "v_"))))
    for d, o in zip((delta, new_m, new_v), outs):
        d.update(zip(_SMALL, _unpack(o, sm_shapes)))

    upper = {n: _adamw_halves("adamw_hi_" + n, P[n], mine, their, P["m_" + n], P["v_" + n], Lb)
             for n, mine, their in zip(_BIG, *chip_sums("hi", h_scat_hi, outs[0]))}
    for n, mine, their in zip(_BIG, *chip_sums("lo", h_scat_lo, upper["w_in"][0])):
        gsum[n], delta[n], new_m[n], new_v[n] = _adamw_halves("adamw_lo_" + n, P[n], mine, their, P["m_" + n], P["v_" + n], 0, upper[n])

    return (loss, dx[None], *[gsum[n] for n in _WEIGHTS], *[delta[n] for n in _WEIGHTS],
            *[new_m[n] for n in _WEIGHTS], *[new_v[n] for n in _WEIGHTS])


def kernel(x, positions, pre_mix_g, w_in, dn_conv_w, dn_a_log, dn_dt_bias, dn_norm_g, sw_sinks, w_up_dn, w_up_sw, w_o, post_mix_g, pre_mlp_g, w_ff1, w_ff2, post_mlp_g, loss_target, m_pre_mix_g, m_w_in, m_dn_conv_w, m_dn_a_log, m_dn_dt_bias, m_dn_norm_g, m_sw_sinks, m_w_up_dn, m_w_up_sw, m_w_o, m_post_mix_g, m_pre_mlp_g, m_w_ff1, m_w_ff2, m_post_mlp_g, v_pre_mix_g, v_w_in, v_dn_conv_w, v_dn_a_log, v_dn_dt_bias, v_dn_norm_g, v_sw_sinks, v_w_up_dn, v_w_up_sw, v_w_o, v_post_mix_g, v_pre_mlp_g, v_w_ff1, v_w_ff2, v_post_mlp_g):
    vals = (x, positions, pre_mix_g, w_in, dn_conv_w, dn_a_log, dn_dt_bias, dn_norm_g, sw_sinks, w_up_dn, w_up_sw, w_o, post_mix_g, pre_mlp_g, w_ff1, w_ff2, post_mlp_g, loss_target, m_pre_mix_g, m_w_in, m_dn_conv_w, m_dn_a_log, m_dn_dt_bias, m_dn_norm_g, m_sw_sinks, m_w_up_dn, m_w_up_sw, m_w_o, m_post_mix_g, m_pre_mlp_g, m_w_ff1, m_w_ff2, m_post_mlp_g, v_pre_mix_g, v_w_in, v_dn_conv_w, v_dn_a_log, v_dn_dt_bias, v_dn_norm_g, v_sw_sinks, v_w_up_dn, v_w_up_sw, v_w_o, v_post_mix_g, v_pre_mlp_g, v_w_ff1, v_w_ff2, v_post_mlp_g)
    names = ["x", "positions"] + _WEIGHTS + ["loss_target"] + ["m_" + n for n in _WEIGHTS] + ["v_" + n for n in _WEIGHTS]
    return _step(dict(zip(names, vals)))
```

```python
import numpy as np
import jax
import jax.numpy as jnp
from jax import lax
from jax.experimental import pallas as pl
from jax.experimental.pallas import tpu as pltpu

F32, BF16 = jnp.float32, jnp.bfloat16
MESH = pl.DeviceIdType.MESH

DN_HEADS = 8
DN_DK = 128
DN_CONV = 4
DN_CHUNK = 64
SW_Q_HEADS = 16
SW_KV_HEADS = 2
SW_HD = 64
SW_BLOCK = 128
ROPE_THETA = 500000.0
ROT_DIM = SW_HD // 4
EPS = 1e-6
ADAM_LR, ADAM_B1, ADAM_B2, ADAM_EPS, ADAM_WD, ADAM_STEP = 0.001, 0.9, 0.999, 1e-08, 0.01, 10

LANES = 128
SUBLANES = 8
VMEM_LIMIT = 48 * 1024 * 1024
KV_W = SW_KV_HEADS * SW_HD
BA_W = 256
PACK_ROWS = 512
DELTA_CB = 4
DELTA_HB = 8


def _pcall(body, **kw):
    return pl.pallas_call(body, **kw)


def _cp(*sem):
    return pltpu.CompilerParams(dimension_semantics=sem, vmem_limit_bytes=VMEM_LIMIT)


def _tile(n, pref, unit=LANES):
    if n <= pref:
        return n
    t = (pref // unit) * unit
    while t > unit and n % t:
        t -= unit
    assert n % t == 0, (n, pref)
    return t


def _sds(shape, dtype):
    return jax.ShapeDtypeStruct(tuple(shape), dtype)


_DIMS = {"nn": ((1,), (0,)), "nt": ((1,), (1,)), "tn": ((0,), (0,))}


def _mm(name, a, b, mode, out_dtype=F32, tm=2048, tn=1024, tk=1024, extras=(), epi=None, out_dtypes=None, slab=None):
    if mode == "nn":
        (M, K), (_, N) = a.shape, b.shape
    elif mode == "nt":
        (M, K), (N, _) = a.shape, b.shape
    else:
        (K, M), (_, N) = a.shape, b.shape
    tm, tn, tk = _tile(M, tm), _tile(N, tn), _tile(K, tk)
    nk = K // tk
    a_spec = {"nn": pl.BlockSpec((tm, tk), lambda i, j, k: (i, k)),
              "nt": pl.BlockSpec((tm, tk), lambda i, j, k: (i, k)),
              "tn": pl.BlockSpec((tk, tm), lambda i, j, k: (k, i))}[mode]
    b_spec = {"nn": pl.BlockSpec((tk, tn), lambda i, j, k: (k, j)),
              "nt": pl.BlockSpec((tn, tk), lambda i, j, k: (j, k)),
              "tn": pl.BlockSpec((tk, tn), lambda i, j, k: (k, j))}[mode]
    dims = (_DIMS[mode], ((), ()))
    out_dtypes = tuple(out_dtypes or (out_dtype,))
    ne, no = len(extras), len(out_dtypes)
    o_spec = pl.BlockSpec((tm, tn), lambda i, j, k: (i, j))

    def body(*refs):
        a_ref, b_ref, ex = refs[0], refs[1], refs[2:2 + ne]
        outs = refs[-no:] if nk == 1 else refs[-1 - no:-1]
        part = lax.dot_general(a_ref[...], b_ref[...], dims, preferred_element_type=F32)

        def finish(acc):
            res = epi(acc, *[e[...] for e in ex]) if epi else (acc,)
            for o, r, dt in zip(outs, res, out_dtypes):
                if slab is None:
                    o[...] = r.astype(dt)
                else:
                    o[0] = r.astype(dt)

        if nk == 1:
            finish(part)
            return
        acc_ref, k = refs[-1], pl.program_id(2)

        @pl.when(k == 0)
        def _():
            acc_ref[...] = part

        @pl.when((k > 0) & (k < nk - 1))
        def _():
            acc_ref[...] += part

        @pl.when(k == nk - 1)
        def _():
            finish(acc_ref[...] + part)

    kw = dict(name=name, grid=(M // tm, N // tn, nk), scratch_shapes=[] if nk == 1 else [pltpu.VMEM((tm, tn), F32)],
              compiler_params=_cp("parallel", "parallel", "arbitrary"))
    if slab is not None:
        buf, l = slab
        return _pcall(body, in_specs=[a_spec, b_spec, ANY], out_specs=pl.BlockSpec((1, tm, tn), lambda i, j, k: (l, i, j)),
                      out_shape=_sds(buf.shape, buf.dtype), input_output_aliases={2: 0}, **kw)(a, b, buf)
    out = _pcall(body, in_specs=[a_spec, b_spec] + [o_spec] * ne, out_specs=tuple(o_spec for _ in out_dtypes),
                 out_shape=tuple(_sds((M, N), dt) for dt in out_dtypes), **kw)(a, b, *extras)
    return out if no > 1 else out[0]


def _rows(name, fn, n_rows, tq, ins, in_specs, out_shapes, out_specs):
    def body(*refs):
        fn(pl.program_id(0), *refs)

    return _pcall(body, name=name, grid=(n_rows // tq,), in_specs=in_specs, out_specs=out_specs,
                  out_shape=out_shapes, compiler_params=_cp("arbitrary"))(*ins)


def _rb(tq, w, cb=0):
    return pl.BlockSpec((tq, w), lambda i: (i, cb))


def _full(shape):
    return pl.BlockSpec(tuple(shape), lambda *_: (0,) * len(shape))


def _rms_fwd(x, g):
    r = lax.rsqrt(jnp.mean(x * x, axis=-1, keepdims=True) + EPS)
    return x * r * g


def _rms_bwd(x, g, dy):
    r = lax.rsqrt(jnp.mean(x * x, axis=-1, keepdims=True) + EPS)
    xh = x * r
    t = dy * g
    dx = r * (t - xh * jnp.mean(t * xh, axis=-1, keepdims=True))
    return dx, jnp.sum(dy * xh, axis=0, keepdims=True)


def _acc(i, ref, val):
    @pl.when(i == 0)
    def _():
        ref[...] = val

    @pl.when(i > 0)
    def _():
        ref[...] += val


def _sigmoid(x):
    return 0.5 * jnp.tanh(0.5 * x) + 0.5


def _pre_norm(x, g):
    S, D = x.shape
    tq = _tile(S, 512, SUBLANES)

    def fn(i, x_ref, g_ref, h_ref):
        h_ref[...] = _rms_fwd(x_ref[...], g_ref[...]).astype(BF16)

    return _rows("pre_norm", fn, S, tq, (x, g), [_rb(tq, D), _full((1, D))], _sds((S, D), BF16), _rb(tq, D))


def _post_mix(x, mix, g2, g3):
    S, D = x.shape
    tq = _tile(S, 512, SUBLANES)

    def fn(i, x_ref, m_ref, g2_ref, g3_ref, x1_ref, h2_ref):
        x1 = x_ref[...] + _rms_fwd(m_ref[...], g2_ref[...])
        x1_ref[...] = x1
        h2_ref[...] = _rms_fwd(x1, g3_ref[...]).astype(BF16)

    return _rows("post_mix", fn, S, tq, (x, mix, g2, g3), [_rb(tq, D), _rb(tq, D), _full((1, D)), _full((1, D))],
                 (_sds((S, D), F32), _sds((S, D), BF16)), (_rb(tq, D), _rb(tq, D)))


def _post_mlp(x1, ff, g4, g1_next):
    S, D = x1.shape
    tq = _tile(S, 512, SUBLANES)

    def fn(i, x_ref, f_ref, g_ref, gn_ref, o_ref, h_ref):
        x2 = x_ref[...] + _rms_fwd(f_ref[...], g_ref[...])
        o_ref[...] = x2
        h_ref[...] = _rms_fwd(x2, gn_ref[...]).astype(BF16)

    r, f = _rb(tq, D), _full((1, D))
    return _rows("post_mlp", fn, S, tq, (x1, ff, g4, g1_next), [r, r, f, f], (_sds((S, D), F32), _sds((S, D), BF16)), (r, r))


def _loss_head(y, target):
    S, D = y.shape
    tq = _tile(S, 512, SUBLANES)

    def fn(i, y_ref, t_ref, l_ref, d_ref):
        e = y_ref[...] - t_ref[...]
        d_ref[...] = e * (1.0 / D)
        part = jnp.sum(jnp.sum(e * e, axis=1, keepdims=True), axis=0, keepdims=True) * (0.5 / D)
        _acc(i, l_ref, jnp.broadcast_to(part, (1, LANES)))

    return _rows("loss_head", fn, S, tq, (y, target), [_rb(tq, D), _rb(tq, D)],
                 (_sds((1, LANES), F32), _sds((S, D), F32)), (_full((1, LANES)), _rb(tq, D)))


def _post_mlp_bwd(ff, g4, dx2):
    S, D = ff.shape
    tq = _tile(S, 512, SUBLANES)

    def fn(i, f_ref, g_ref, d_ref, o_ref, dg_ref):
        dx, dg = _rms_bwd(f_ref[...], g_ref[...], d_ref[...])
        o_ref[...] = dx.astype(BF16)
        _acc(i, dg_ref, dg)

    return _rows("post_mlp_bwd", fn, S, tq, (ff, g4, dx2), [_rb(tq, D), _full((1, D)), _rb(tq, D)],
                 (_sds((S, D), BF16), _sds((1, D), F32)), (_rb(tq, D), _full((1, D))))


def _mid_bwd(x1, g3, dh2, dx2, mix, g2):
    S, D = x1.shape
    tq = _tile(S, 256, SUBLANES)

    def fn(i, x_ref, g3_ref, dh_ref, dx2_ref, m_ref, g2_ref, dx1_ref, dm_ref, dg3_ref, dg2_ref):
        d, dg3 = _rms_bwd(x_ref[...], g3_ref[...], dh_ref[...])
        dx1 = dx2_ref[...] + d
        dx1_ref[...] = dx1
        dm, dg2 = _rms_bwd(m_ref[...], g2_ref[...], dx1)
        dm_ref[...] = dm.astype(BF16)
        _acc(i, dg3_ref, dg3)
        _acc(i, dg2_ref, dg2)

    r, f = _rb(tq, D), _full((1, D))
    return _rows("mid_bwd", fn, S, tq, (x1, g3, dh2, dx2, mix, g2), [r, f, r, r, r, f],
                 (_sds((S, D), F32), _sds((S, D), BF16), _sds((1, D), F32), _sds((1, D), F32)), (r, r, f, f))


def _pre_norm_bwd(x, g1, dh, dx1):
    S, D = x.shape
    tq = _tile(S, 512, SUBLANES)

    def fn(i, x_ref, g_ref, dh_ref, dx1_ref, dx_ref, dg_ref):
        d, dg = _rms_bwd(x_ref[...], g_ref[...], dh_ref[...])
        dx_ref[...] = dx1_ref[...] + d
        _acc(i, dg_ref, dg)

    r, f = _rb(tq, D), _full((1, D))
    return _rows("pre_norm_bwd", fn, S, tq, (x, g1, dh, dx1), [r, f, r, r], (_sds((S, D), F32), _sds((1, D), F32)), (r, f))


def _mix(proj, ya, yb, D, cb_a):
    S = ya.shape[0]
    tq = _tile(S, 256, SUBLANES)

    def fn(i, ga_ref, gb_ref, ya_ref, yb_ref, o_ref):
        ga, gb, ya, yb = (r[...].astype(F32) for r in (ga_ref, gb_ref, ya_ref, yb_ref))
        o_ref[...] = (_sigmoid(ga) * ya + _sigmoid(gb) * yb).astype(BF16)

    return _rows("mix", fn, S, tq, (proj, proj, ya, yb), [_rb(tq, D, cb_a), _rb(tq, D, cb_a + 1), _rb(tq, D), _rb(tq, D)],
                 _sds((S, D), BF16), _rb(tq, D))


def _mix_bwd(proj, ya, yb, dmixin, D, cb_a):
    S = ya.shape[0]
    tq = _tile(S, 256, SUBLANES)

    def fn(i, ga_ref, gb_ref, ya_ref, yb_ref, d_ref, dya_ref, dyb_ref, dga_ref, dgb_ref):
        ga, gb, ya, yb, d = (r[...].astype(F32) for r in (ga_ref, gb_ref, ya_ref, yb_ref, d_ref))
        sa, sb = _sigmoid(ga), _sigmoid(gb)
        dya_ref[...] = (d * sa).astype(BF16)
        dyb_ref[...] = (d * sb).astype(BF16)
        dga_ref[...] = (d * ya * sa * (1.0 - sa)).astype(BF16)
        dgb_ref[...] = (d * yb * sb * (1.0 - sb)).astype(BF16)

    r = _rb(tq, D)
    o = _sds((S, D), BF16)
    return _rows("mix_bwd", fn, S, tq, (proj, proj, ya, yb, dmixin), [_rb(tq, D, cb_a), _rb(tq, D, cb_a + 1), r, r, r],
                 (o, o, o, o), (r, r, r, r))


HALO = 16


def _shift_down(xe, k, tq):
    return pltpu.roll(xe, k, 0)[HALO:HALO + tq]


def _conv_pre(cur_ref, halo_ref, w_ref, i, tq):
    x = cur_ref[...].astype(F32)
    halo = jnp.where(i > 0, halo_ref[...].astype(F32), 0.0)
    xe = jnp.concatenate([halo, x], axis=0)
    xs = [x] + [_shift_down(xe, k, tq) for k in range(1, DN_CONV)]
    w = w_ref[...]
    c = sum(w[DN_CONV - 1 - k:DN_CONV - k, :] * xs[k] for k in range(DN_CONV))
    return c, xs


def _dn_prep(proj, conv_w, W):
    S = proj.shape[0]
    tq = _tile(S, 256, HALO)
    hb = tq // HALO

    def body(cur_ref, halo_ref, w_ref, o_ref):
        j, i = pl.program_id(0), pl.program_id(1)
        c, _ = _conv_pre(cur_ref, halo_ref, w_ref, i, tq)
        y = c * _sigmoid(c)
        scale = jnp.where(j == 0, DN_DK ** -0.5, 1.0)
        for h in range(W // DN_DK):
            sl = slice(h * DN_DK, (h + 1) * DN_DK)
            yh = y[:, sl]
            rs = lax.rsqrt(jnp.sum(yh * yh, axis=-1, keepdims=True) + EPS)
            o_ref[:, sl] = jnp.where(j == 2, yh, yh * rs * scale)

    return _pcall(body, name="dn_prep", grid=(3, S // tq),
                  in_specs=[pl.BlockSpec((tq, W), lambda j, i: (i, j)),
                            pl.BlockSpec((HALO, W), lambda j, i: (jnp.maximum(i * hb - 1, 0), j)),
                            pl.BlockSpec((DN_CONV, W), lambda j, i: (0, j))],
                  out_specs=pl.BlockSpec((tq, W), lambda j, i: (i, j)), out_shape=_sds((S, 3 * W), F32),
                  compiler_params=_cp("arbitrary", "arbitrary"))(proj, proj, conv_w)


def _dn_prep_bwd_a(proj, conv_w, dqkv, W):
    S = proj.shape[0]
    tq = _tile(S, 256, HALO)
    hb = tq // HALO

    def body(cur_ref, halo_ref, w_ref, d_ref, dc_ref, dw_ref):
        j, i = pl.program_id(0), pl.program_id(1)
        c, xs = _conv_pre(cur_ref, halo_ref, w_ref, i, tq)
        sg = _sigmoid(c)
        y = c * sg
        scale = jnp.where(j == 0, DN_DK ** -0.5, 1.0)
        dout = d_ref[0]
        dys = []
        for h in range(W // DN_DK):
            sl = slice(h * DN_DK, (h + 1) * DN_DK)
            yh, dh = y[:, sl], dout[:, sl]
            rs = lax.rsqrt(jnp.sum(yh * yh, axis=-1, keepdims=True) + EPS)
            yn = yh * rs
            dn = scale * rs * (dh - yn * jnp.sum(dh * yn, axis=-1, keepdims=True))
            dys.append(jnp.where(j == 2, dh, dn))
        dy = jnp.concatenate(dys, axis=1)
        dc = dy * (sg * (1.0 + c * (1.0 - sg)))
        dc_ref[...] = dc
        dw = jnp.concatenate([jnp.sum(dc * xs[DN_CONV - 1 - r], axis=0, keepdims=True) for r in range(DN_CONV)], axis=0)
        _acc(i, dw_ref, dw)

    return _pcall(body, name="dn_prep_bwd_a", grid=(3, S // tq),
                  in_specs=[pl.BlockSpec((tq, W), lambda j, i: (i, j)),
                            pl.BlockSpec((HALO, W), lambda j, i: (jnp.maximum(i * hb - 1, 0), j)),
                            pl.BlockSpec((DN_CONV, W), lambda j, i: (0, j)),
                            pl.BlockSpec((1, tq, W), lambda j, i: (j, i, 0))],
                  out_specs=(pl.BlockSpec((tq, W), lambda j, i: (i, j)), pl.BlockSpec((DN_CONV, W), lambda j, i: (0, j))),
                  out_shape=(_sds((S, 3 * W), F32), _sds((DN_CONV, 3 * W), F32)),
                  compiler_params=_cp("arbitrary", "arbitrary"))(proj, proj, conv_w, dqkv)


def _dn_prep_bwd_b(dc, conv_w, W):
    S = dc.shape[0]
    tq = _tile(S, 256, SUBLANES)
    hb = tq // SUBLANES
    nblk = S // tq

    def body(cur_ref, nxt_ref, w_ref, o_ref):
        i = pl.program_id(1)
        d = cur_ref[...]
        nxt = jnp.where(i < nblk - 1, nxt_ref[...], 0.0)
        de = jnp.concatenate([d, nxt], axis=0)
        w = w_ref[...]
        out = w[DN_CONV - 1:DN_CONV, :] * d
        for k in range(1, DN_CONV):
            out = out + w[DN_CONV - 1 - k:DN_CONV - k, :] * pltpu.roll(de, tq + SUBLANES - k, 0)[0:tq]
        o_ref[...] = out.astype(BF16)

    return _pcall(body, name="dn_prep_bwd_b", grid=(3, nblk),
                  in_specs=[pl.BlockSpec((tq, W), lambda j, i: (i, j)),
                            pl.BlockSpec((SUBLANES, W), lambda j, i: (jnp.minimum((i + 1) * hb, S // SUBLANES - 1), j)),
                            pl.BlockSpec((DN_CONV, W), lambda j, i: (0, j))],
                  out_specs=pl.BlockSpec((tq, W), lambda j, i: (i, j)), out_shape=_sds((S, 3 * W), BF16),
                  compiler_params=_cp("arbitrary", "arbitrary"))(dc, dc, conv_w)


def _gate_terms(ba, al, dt):
    u = ba + dt
    sp = jnp.maximum(u, 0.0) + jnp.log(1.0 + jnp.exp(-jnp.abs(u)))
    return _sigmoid(ba), -jnp.exp(al) * sp, u


def _dn_gates(ba, alog_row, dt_row, H):
    S = ba.shape[0]
    tq = _tile(S, 512, SUBLANES)
    W = H * DN_DK

    def fn(i, ba_ref, al_ref, dt_ref, be_ref, g_ref):
        bet, gg, _ = _gate_terms(ba_ref[...], al_ref[...], dt_ref[...])
        for h in range(H):
            sl = slice(h * DN_DK, (h + 1) * DN_DK)
            be_ref[:, sl] = jnp.broadcast_to(bet[:, h:h + 1], (tq, DN_DK))
            g_ref[:, sl] = jnp.broadcast_to(gg[:, H + h:H + h + 1], (tq, DN_DK))

    return _rows("dn_gates", fn, S, tq, (ba, alog_row, dt_row), [_rb(tq, LANES), _full((1, LANES)), _full((1, LANES))],
                 (_sds((S, W), F32), _sds((S, W), F32)), (_rb(tq, W), _rb(tq, W)))


def _dn_gates_bwd(ba, alog_row, dt_row, dbeta_b, dg_b, H):
    S = ba.shape[0]
    tq = _tile(S, 512, SUBLANES)
    W = H * DN_DK

    def fn(i, ba_ref, al_ref, dt_ref, db_ref, dg_ref, o_ref, dal_ref, ddt_ref):
        bet, gg, u = _gate_terms(ba_ref[...], al_ref[...], dt_ref[...])
        lane = lax.broadcasted_iota(jnp.int32, (tq, LANES), 1)
        d = jnp.zeros((tq, LANES), F32)
        for h in range(H):
            d = jnp.where(lane == h, db_ref[:, h * DN_DK:h * DN_DK + 1], d)
            d = jnp.where(lane == H + h, dg_ref[:, h * DN_DK:h * DN_DK + 1], d)
        is_a = (lane >= H) & (lane < 2 * H)
        da = jnp.where(is_a, d * (-jnp.exp(al_ref[...]) * _sigmoid(u)), 0.0)
        dlog = jnp.where(lane < H, d * bet * (1.0 - bet), da)
        o_ref[...] = jnp.concatenate([dlog, jnp.zeros((tq, BA_W - LANES), F32)], axis=1).astype(BF16)
        _acc(i, dal_ref, jnp.sum(jnp.where(is_a, d * gg, 0.0), axis=0, keepdims=True))
        _acc(i, ddt_ref, jnp.sum(da, axis=0, keepdims=True))

    f = _full((1, LANES))
    return _rows("dn_gates_bwd", fn, S, tq, (ba, alog_row, dt_row, dbeta_b, dg_b),
                 [_rb(tq, LANES), f, f, _rb(tq, W), _rb(tq, W)],
                 (_sds((S, BA_W), BF16), _sds((1, LANES), F32), _sds((1, LANES), F32)), (_rb(tq, BA_W), f, f))


def _dn_out(o, proj, ng, W, cb_z):
    S = o.shape[0]
    tq = _tile(S, 256, SUBLANES)

    def fn(i, o_ref, z_ref, g_ref, y_ref):
        for h in range(W // DN_DK):
            sl = slice(h * DN_DK, (h + 1) * DN_DK)
            z = z_ref[:, sl].astype(F32)
            y_ref[:, sl] = (_rms_fwd(o_ref[:, sl], g_ref[...]) * (z * _sigmoid(z))).astype(BF16)

    return _rows("dn_out", fn, S, tq, (o, proj, ng), [_rb(tq, W), _rb(tq, W, cb_z), _full((1, DN_DK))], _sds((S, W), BF16), _rb(tq, W))


def _dn_out_bwd(o, proj, ng, dy, W, cb_z):
    S = o.shape[0]
    tq = _tile(S, 256, SUBLANES)

    def fn(i, o_ref, z_ref, g_ref, d_ref, do_ref, dz_ref, dg_ref):
        g = g_ref[...]
        dg = jnp.zeros((1, DN_DK), F32)
        for h in range(W // DN_DK):
            sl = slice(h * DN_DK, (h + 1) * DN_DK)
            oh, z, d = o_ref[:, sl], z_ref[:, sl].astype(F32), d_ref[:, sl].astype(F32)
            sg = _sigmoid(z)
            dn = d * (z * sg)
            dz_ref[:, sl] = (d * _rms_fwd(oh, g) * (sg * (1.0 + z * (1.0 - sg)))).astype(BF16)
            dx, dgh = _rms_bwd(oh, g, dn)
            do_ref[:, sl] = dx
            dg = dg + dgh
        _acc(i, dg_ref, dg)

    r = _rb(tq, W)
    return _rows("dn_out_bwd", fn, S, tq, (o, proj, ng, dy), [r, _rb(tq, W, cb_z), _full((1, DN_DK)), r],
                 (_sds((S, W), F32), _sds((S, W), BF16), _sds((1, DN_DK), F32)), (r, r, _full((1, DN_DK))))


def _bdot(a, b, mode="nn"):
    return lax.dot_general(a.astype(BF16), b.astype(BF16), (_DIMS[mode], ((), ())), preferred_element_type=F32)


def _rsum(x):
    return jnp.broadcast_to(jnp.sum(x, axis=-1, keepdims=True), x.shape)


def _dot3(a, b, mode="nn"):
    ah, bh = a.astype(BF16), b.astype(BF16)
    al, bl = (a - ah.astype(F32)).astype(BF16), (b - bh.astype(F32)).astype(BF16)
    d = lambda x, y: lax.dot_general(x, y, (_DIMS[mode], ((), ())), preferred_element_type=F32)
    return d(ah, bh) + (d(al, bh) + d(ah, bl))


def _cumsum_rows(x, reverse=False):
    n = x.shape[0]
    row = lax.broadcasted_iota(jnp.int32, x.shape, 0)
    s = 1
    while s < n:
        if reverse:
            x = x + jnp.where(row < n - s, pltpu.roll(x, n - s, 0), 0.0)
        else:
            x = x + jnp.where(row >= s, pltpu.roll(x, s, 0), 0.0)
        s *= 2
    return x


def _each(f, *lists):
    return [f(*a) for a in zip(*lists)]


def _delta_local(qs, ks, vs, bes, grs):
    C = DN_CHUNK
    ri = lax.broadcasted_iota(jnp.int32, (C, C), 0)
    ci = lax.broadcasted_iota(jnp.int32, (C, C), 1)
    causal, strict = ri >= ci, ri > ci
    gcs = [_cumsum_rows(g) for g in grs]
    decays = [jnp.where(causal, jnp.exp(jnp.where(causal, gc[:, :C] - gc.T[:C, :], 0.0)), 0.0) for gc in gcs]
    egs = [jnp.exp(gc) for gc in gcs]
    eks = [jnp.exp(gc[C - 1:C, :] - gc) for gc in gcs]
    gams = [jnp.exp(gc[C - 1:C, :]) for gc in gcs]
    kbs = _each(lambda k, be: k * be, ks, bes)
    kks = _each(lambda kb, k: _bdot(kb, k, "nt"), kbs, ks)
    nls = _each(lambda kk, dc: jnp.where(strict, -kk * dc, 0.0), kks, decays)
    eye = (ri == ci).astype(F32)
    ts = [eye + nl for nl in nls]
    pws = [_dot3(nl, nl) for nl in nls]
    for s in range(4):
        both = _each(lambda t, pw: _dot3(jnp.concatenate([t, pw], axis=0), pw), ts, pws)
        ts = _each(lambda t, b: t + b[:C], ts, both)
        pws = [b[C:] for b in both]
    ts = _each(lambda t, pw: t + _dot3(t, pw), ts, pws)
    vbs = _each(lambda v, be: v * be, vs, bes)
    kbes = _each(lambda kb, eg: kb * eg, kbs, egs)
    uws = _each(lambda t, vb, kbe: _dot3(t, jnp.concatenate([vb, kbe], axis=1)), ts, vbs, kbes)
    us, ws = [uw[:, :DN_DK] for uw in uws], [uw[:, DN_DK:] for uw in uws]
    qks = _each(lambda q, k: _bdot(q, k, "nt"), qs, ks)
    return dict(decay=decays, eg=egs, ek=eks, gam=gams, kb=kbs, kk=kks, t=ts, vb=vbs, kbe=kbes, u=us, w=ws, qk=qks,
                a=_each(lambda qk, dc: qk * dc, qks, decays), qd=_each(lambda q, eg: q * eg, qs, egs),
                kd=_each(lambda k, ek: k * ek, ks, eks), strict=strict)


def _delta_items(refs, CB, HB):
    C, dk = DN_CHUNK, DN_DK
    return [[r[c * C:(c + 1) * C, h * dk:(h + 1) * dk] for h in range(HB) for c in range(CB)] for r in refs]


def _delta_fwd(qkv, beta_b, g_b, H, CB, HB):
    S = qkv.shape[0]
    C, dk = DN_CHUNK, DN_DK
    N = S // C
    R = CB * C
    G = H // HB

    def body(q_ref, k_ref, v_ref, b_ref, g_ref, o_ref, st_ref, s_ref):
        @pl.when(pl.program_id(1) == 0)
        def _():
            s_ref[...] = jnp.zeros((HB, dk, dk), F32)

        L = _delta_local(*_delta_items((q_ref, k_ref, v_ref, b_ref, g_ref), CB, HB))
        ss = [s_ref[h] for h in range(HB)]
        for c in range(CB):
            it = [h * CB + c for h in range(HB)]
            for h in range(HB):
                st_ref[h, c] = ss[h]
            wq = [_bdot(jnp.concatenate([L["w"][i], L["qd"][i]], axis=0), s) for i, s in zip(it, ss)]
            vns = [L["u"][i] - x[:C] for i, x in zip(it, wq)]
            outs = [x[C:] + _bdot(L["a"][i], vn) for i, x, vn in zip(it, wq, vns)]
            ss = [s * L["gam"][i] + _bdot(L["kd"][i], vn, "tn") for i, s, vn in zip(it, ss, vns)]
            for h in range(HB):
                o_ref[c * C:(c + 1) * C, h * dk:(h + 1) * dk] = outs[h]
        for h in range(HB):
            s_ref[h] = ss[h]

    blk = lambda off: pl.BlockSpec((R, HB * dk), lambda h, n: (n, off + h))
    return _pcall(body, name="delta_fwd", grid=(G, N // CB),
                  in_specs=[blk(0), blk(G), blk(2 * G), blk(0), blk(0)],
                  out_specs=(blk(0), pl.BlockSpec((HB, CB, dk, dk), lambda h, n: (h, n, 0, 0))),
                  out_shape=(_sds((S, H * dk), F32), _sds((H, N, dk, dk), F32)),
                  scratch_shapes=[pltpu.VMEM((HB, dk, dk), F32)],
                  compiler_params=_cp("arbitrary", "arbitrary"))(qkv, qkv, qkv, beta_b, g_b)


def _delta_bwd(qkv, beta_b, g_b, states, do, H, CB, HB):
    S = qkv.shape[0]
    C, dk = DN_CHUNK, DN_DK
    N = S // C
    R = CB * C
    NB = N // CB
    G = H // HB

    def body(q_ref, k_ref, v_ref, b_ref, g_ref, st_ref, do_ref, dqkv_ref, db_ref, dg_ref, ds_ref):
        @pl.when(pl.program_id(1) == 0)
        def _():
            ds_ref[...] = jnp.zeros((HB, dk, dk), F32)

        qs, ks, vs, bes, grs, dos = _delta_items((q_ref, k_ref, v_ref, b_ref, g_ref, do_ref), CB, HB)
        L = _delta_local(qs, ks, vs, bes, grs)
        ts, decays, kbs, egs, eks, gams, qds, kds = (L[n] for n in ("t", "decay", "kb", "eg", "ek", "gam", "qd", "kd"))
        s0s = [st_ref[h, c] for h in range(HB) for c in range(CB)]
        vns = _each(lambda u, w, s0: u - _bdot(w, s0), L["u"], L["w"], s0s)
        pre_dvn = _each(lambda a, d: _bdot(a, d, "tn"), L["a"], dos)
        pre_ds = _each(lambda qd, d: _bdot(qd, d, "tn"), qds, dos)
        das = _each(lambda d, vn: _bdot(d, vn, "nt"), dos, vns)
        ds = [ds_ref[h] for h in range(HB)]
        ds1s, dvns = [None] * (HB * CB), [None] * (HB * CB)
        for c in reversed(range(CB)):
            it = [h * CB + c for h in range(HB)]
            new = [pre_dvn[i] + _bdot(kds[i], d) for i, d in zip(it, ds)]
            for i, d, dv in zip(it, ds, new):
                ds1s[i], dvns[i] = d, dv
            ds = [pre_ds[i] + d * gams[i] - _bdot(L["w"][i], dv, "tn") for i, d, dv in zip(it, ds, new)]
        for h in range(HB):
            ds_ref[h] = ds[h]
        dkds = _each(lambda vn, d1: _bdot(vn, d1, "nt"), vns, ds1s)
        dgams = _each(lambda s0, d1: jnp.sum(jnp.sum(s0 * d1, axis=1, keepdims=True), axis=0, keepdims=True), s0s, ds1s)
        ost = _each(lambda d, dv, s0: _bdot(jnp.concatenate([d, dv], axis=0), s0, "nt"), dos, dvns, s0s)
        dqds, dws = [x[:C] for x in ost], [-x[C:] for x in ost]
        dvw = _each(lambda dv, dw: jnp.concatenate([dv, dw], axis=1), dvns, dws)
        tdvw = _each(lambda t, x: _dot3(t, x, "tn"), ts, dvw)
        dvbs, dkbes = [x[:, :dk] for x in tdvw], [x[:, dk:] for x in tdvw]
        dts = _each(lambda x, vb, kbe: _dot3(x, jnp.concatenate([vb, kbe], axis=1), "nt"), dvw, L["vb"], L["kbe"])
        tmp = _each(lambda dt, t: _dot3(dt, t, "nt"), dts, ts)
        dls = _each(lambda t, x: -_dot3(t, x, "tn"), ts, tmp)
        ms = _each(lambda dl, dc: jnp.where(L["strict"], dl * dc, 0.0), dls, decays)
        mas = _each(lambda da, dc: da * dc, das, decays)
        dkbs = _each(lambda m, k, dkbe, eg: _bdot(m, k) + dkbe * eg, ms, ks, dkbes, egs)
        dks = _each(lambda m, kb, ma, q, dkd, ek, dkb, be: _bdot(m, kb, "tn") + _bdot(ma, q, "tn") + dkd * ek + dkb * be,
                    ms, kbs, mas, qs, dkds, eks, dkbs, bes)
        dqs = _each(lambda ma, k, dqd, eg: _bdot(ma, k) + dqd * eg, mas, ks, dqds, egs)
        es = _each(lambda m, kk, ma, qk: m * kk + ma * qk, ms, L["kk"], mas, L["qk"])
        ones = jnp.ones((C, dk), BF16)
        row = lax.broadcasted_iota(jnp.int32, (C, dk), 0)
        for i in range(HB * CB):
            h, c = divmod(i, CB)
            rs, cs = slice(c * C, (c + 1) * C), slice(h * dk, (h + 1) * dk)
            e = es[i]
            e_hi = e.astype(BF16)
            col = _bdot(e_hi, ones, "tn") + _bdot(e - e_hi.astype(F32), ones, "tn")
            t_kd = _rsum(dkds[i] * kds[i])
            dgc = (jnp.broadcast_to(jnp.sum(e, axis=1, keepdims=True), (C, dk)) - col + _rsum(dqds[i] * qds[i]) - t_kd
                   + _rsum(dkbes[i] * L["kbe"][i]))
            dglast = jnp.sum(t_kd[:, 0:1], axis=0, keepdims=True) + dgams[i] * gams[i][:, 0:1]
            dgc = dgc + jnp.where(row == C - 1, dglast, 0.0)
            dqkv_ref[0, rs, cs] = dqs[i]
            dqkv_ref[1, rs, cs] = dks[i]
            dqkv_ref[2, rs, cs] = dvbs[i] * bes[i]
            db_ref[rs, cs] = _rsum(dkbs[i] * ks[i]) + _rsum(dvbs[i] * vs[i])
            dg_ref[rs, cs] = _cumsum_rows(dgc, reverse=True)

    blk = lambda off: pl.BlockSpec((R, HB * dk), lambda h, n: (NB - 1 - n, off + h))
    W = H * dk
    return _pcall(body, name="delta_bwd", grid=(G, NB),
                  in_specs=[blk(0), blk(G), blk(2 * G), blk(0), blk(0),
                            pl.BlockSpec((HB, CB, dk, dk), lambda h, n: (h, NB - 1 - n, 0, 0)), blk(0)],
                  out_specs=(pl.BlockSpec((3, R, HB * dk), lambda h, n: (0, NB - 1 - n, h)), blk(0), blk(0)),
                  out_shape=(_sds((3, S, W), F32), _sds((S, W), F32), _sds((S, W), F32)),
                  scratch_shapes=[pltpu.VMEM((HB, dk, dk), F32)],
                  compiler_params=_cp("arbitrary", "arbitrary"))(qkv, qkv, qkv, beta_b, g_b, states, do)


def _rope_consts():
    lane = np.arange(LANES) % SW_HD
    half = ROT_DIM // 2
    inv = (ROPE_THETA ** (-np.arange(half, dtype=np.float32) * np.float32(2.0 / ROT_DIM))).astype(np.float32)
    freq = np.where(lane < ROT_DIM, inv[lane % half], 0.0).astype(np.float32)
    lo = (lane < half).astype(np.float32)
    hi = ((lane >= half) & (lane < ROT_DIM)).astype(np.float32)
    return jnp.asarray(np.stack([freq, -lo, hi] + [np.zeros(LANES, np.float32)] * 5))


def _rope_tables(pos_col):
    S = pos_col.shape[0]
    tq = _tile(S, 1024, SUBLANES)

    def fn(i, p_ref, c_ref, cos_ref, s1_ref, s2_ref):
        ang = p_ref[...].astype(F32) * c_ref[0:1, :]
        sn = jnp.sin(ang)
        cos_ref[...] = jnp.cos(ang)
        s1_ref[...] = sn * c_ref[1:2, :]
        s2_ref[...] = sn * c_ref[2:3, :]

    o, r = _sds((S, LANES), F32), _rb(tq, LANES)
    return _rows("rope_tables", fn, S, tq, (pos_col, _rope_consts()), [_rb(tq, 1), _full((SUBLANES, LANES))], (o, o, o), (r, r, r))


def _wide(a, w):
    return a if w == LANES else jnp.tile(a, (1, w // LANES))


def _rope(x, cos, s1, s2):
    w, h = x.shape[1], ROT_DIM // 2
    return x * _wide(cos, w) + pltpu.roll(x, w - h, 1) * _wide(s1, w) + pltpu.roll(x, h, 1) * _wide(s2, w)


def _unrope(d, cos, s1, s2):
    w, h = d.shape[1], ROT_DIM // 2
    return d * _wide(cos, w) + pltpu.roll(d * _wide(s1, w), h, 1) + pltpu.roll(d * _wide(s2, w), w - h, 1)


def _swa_setup(n, q_ref, kc_ref, kp_ref, vc_ref, vp_ref, tc, tp):
    B = SW_BLOCK
    qr = _rope(q_ref[...].astype(F32), tc[0][...], tc[1][...], tc[2][...]) * (SW_HD ** -0.5)
    kw = jnp.concatenate([_rope(kp_ref[...].astype(F32), tp[0][...], tp[1][...], tp[2][...]),
                          _rope(kc_ref[...].astype(F32), tc[0][...], tc[1][...], tc[2][...])], axis=0)
    vw = jnp.concatenate([vp_ref[...], vc_ref[...]], axis=0).astype(F32)
    lane = lax.broadcasted_iota(jnp.int32, (2 * B, LANES), 1)
    heads = []
    for hk in range(SW_KV_HEADS):
        kh, vh = kw[:, hk * SW_HD:(hk + 1) * SW_HD], vw[:, hk * SW_HD:(hk + 1) * SW_HD]
        kk, vv = jnp.concatenate([kh, kh], axis=1), jnp.concatenate([vh, vh], axis=1)
        heads.append(tuple(jnp.where(sel, t, 0.0).astype(BF16) for t in (kk, vv) for sel in (lane < SW_HD, lane >= SW_HD)))
    prev = lax.broadcasted_iota(jnp.int32, (B, B), 1) > lax.broadcasted_iota(jnp.int32, (B, B), 0)
    return qr, heads, (prev, jnp.where(prev & (n == 0), -1e30, 0.0)), lane


def _fold(x, prev):
    return jnp.where(prev, x[:, :SW_BLOCK], x[:, SW_BLOCK:])


def _unfold(x, prev):
    return jnp.concatenate([jnp.where(prev, x, 0.0), jnp.where(prev, 0.0, x)], axis=1)


SWA_GROUPS = 4
SWA_GROUPS_BWD = 2


def _swa_probs(items, qs, heads, fold, sk_ref, G2):
    prev, bias = fold
    ss = [_fold(_bdot(qs[j], heads[j // G2][half], "nt"), prev) + bias for j, half in items]
    sks = [sk_ref[0:1, 2 * j + half:2 * j + half + 1] for j, half in items]
    ms = [jnp.maximum(jnp.max(s, axis=-1, keepdims=True), sk) for s, sk in zip(ss, sks)]
    ps = [jnp.exp(s - m) for s, m in zip(ss, ms)]
    es = [jnp.exp(sk - m) for sk, m in zip(sks, ms)]
    inv = [1.0 / (jnp.sum(p, axis=-1, keepdims=True) + e) for p, e in zip(ps, es)]
    return [p * i for p, i in zip(ps, inv)], [e * i for e, i in zip(es, inv)]


def _swa_specs(W, cb_q, cb_k):
    B = SW_BLOCK
    assert (W // LANES) % SWA_GROUPS == 0 and (W // LANES) % SWA_GROUPS_BWD == 0
    cur = lambda w, cb: pl.BlockSpec((B, w), lambda n: (n, cb))
    prv = lambda w, cb: pl.BlockSpec((B, w), lambda n: (jnp.maximum(n - 1, 0), cb))
    specs = [cur(W, cb_q), cur(LANES, cb_k), prv(LANES, cb_k), cur(LANES, cb_k + 1), prv(LANES, cb_k + 1)]
    return specs + [cur(LANES, 0)] * 3 + [prv(LANES, 0)] * 3 + [_full((1, LANES))]


def _swa_fwd(proj, tabs, sinks_row, W, cb_q, cb_k):
    S = proj.shape[0]
    G2 = SW_Q_HEADS // SW_KV_HEADS // 2

    def body(q_ref, kc_ref, kp_ref, vc_ref, vp_ref, c0, c1, c2, p0, p1, p2, sk_ref, o_ref):
        n = pl.program_id(0)
        qr, heads, fold, _ = _swa_setup(n, q_ref, kc_ref, kp_ref, vc_ref, vp_ref, (c0, c1, c2), (p0, p1, p2))
        qs = [qr[:, j * LANES:(j + 1) * LANES].astype(BF16) for j in range(W // LANES)]
        for j0 in range(0, W // LANES, SWA_GROUPS):
            items = [(j, half) for j in range(j0, j0 + SWA_GROUPS) for half in range(2)]
            probs, _ = _swa_probs(items, qs, heads, fold, sk_ref, G2)
            pv = [_bdot(_unfold(p, fold[0]), heads[j // G2][2 + half]) for p, (j, half) in zip(probs, items)]
            for g in range(SWA_GROUPS):
                o_ref[:, (j0 + g) * LANES:(j0 + g + 1) * LANES] = (pv[2 * g] + pv[2 * g + 1]).astype(BF16)

    t = tuple(tabs)
    return _pcall(body, name="swa_fwd", grid=(S // SW_BLOCK,), in_specs=_swa_specs(W, cb_q, cb_k),
                  out_specs=pl.BlockSpec((SW_BLOCK, W), lambda n: (n, 0)), out_shape=_sds((S, W), BF16),
                  compiler_params=_cp("arbitrary"))(proj, proj, proj, proj, proj, *t, *t, sinks_row)


def _swa_bwd(proj, tabs, sinks_row, do, W, cb_q, cb_k):
    S = proj.shape[0]
    B = SW_BLOCK
    G2 = SW_Q_HEADS // SW_KV_HEADS // 2
    SKR = -(-SW_Q_HEADS // SUBLANES) * SUBLANES

    def body(q_ref, kc_ref, kp_ref, vc_ref, vp_ref, c0, c1, c2, p0, p1, p2, sk_ref, do_ref,
             dq_ref, dkc_ref, dkp_ref, dvc_ref, dvp_ref, dsk_ref):
        n = pl.program_id(0)
        qr, heads, fold, lane = _swa_setup(n, q_ref, kc_ref, kp_ref, vc_ref, vp_ref, (c0, c1, c2), (p0, p1, p2))
        prev = fold[0]

        @pl.when(n == 0)
        def _():
            dsk_ref[...] = jnp.zeros((SKR, LANES), F32)

        acc_k = [jnp.zeros((2 * B, LANES), F32) for _ in range(SW_KV_HEADS)]
        acc_v = [jnp.zeros((2 * B, LANES), F32) for _ in range(SW_KV_HEADS)]
        qs = [qr[:, j * LANES:(j + 1) * LANES].astype(BF16) for j in range(W // LANES)]
        dos = [do_ref[:, j * LANES:(j + 1) * LANES].astype(BF16) for j in range(W // LANES)]
        dqs = []
        for j0 in range(0, W // LANES, SWA_GROUPS_BWD):
            items = [(j, half) for j in range(j0, j0 + SWA_GROUPS_BWD) for half in range(2)]
            probs, psinks = _swa_probs(items, qs, heads, fold, sk_ref, G2)
            dps = [_fold(_bdot(dos[j], heads[j // G2][2 + half], "nt"), prev) for j, half in items]
            deltas = [jnp.sum(p * dp, axis=-1, keepdims=True) for p, dp in zip(probs, dps)]
            dss = [_unfold(p * (dp - dl), prev).astype(BF16) for p, dp, dl in zip(probs, dps, deltas)]
            pbs = [_unfold(p, prev).astype(BF16) for p in probs]
            dqp = [_bdot(ds, heads[j // G2][half]) for ds, (j, half) in zip(dss, items)]
            dkk = [_bdot(ds, qs[j], "tn") for ds, (j, half) in zip(dss, items)]
            dvv = [_bdot(p, dos[j], "tn") for p, (j, half) in zip(pbs, items)]
            for i, (j, half) in enumerate(items):
                hk, h = j // G2, 2 * j + half
                sel = (lane < SW_HD) if half == 0 else (lane >= SW_HD)
                acc_k[hk] = acc_k[hk] + jnp.where(sel, dkk[i], 0.0)
                acc_v[hk] = acc_v[hk] + jnp.where(sel, dvv[i], 0.0)
                dsk_ref[h:h + 1, :] += jnp.broadcast_to(-jnp.sum(psinks[i] * deltas[i], axis=0, keepdims=True), (1, LANES))
            dqs += [dqp[2 * g] + dqp[2 * g + 1] for g in range(SWA_GROUPS_BWD)]
        dq = jnp.concatenate(dqs, axis=1) * (SW_HD ** -0.5)
        dq_ref[...] = _unrope(dq, c0[...], c1[...], c2[...]).astype(BF16)
        fold = lambda a: a[:, :SW_HD] + a[:, SW_HD:]
        dkw = jnp.concatenate([fold(a) for a in acc_k], axis=1)
        dvw = jnp.concatenate([fold(a) for a in acc_v], axis=1)
        dkp_ref[...], dkc_ref[...] = dkw[:B], dkw[B:]
        dvp_ref[...], dvc_ref[...] = dvw[:B], dvw[B:]

    t = tuple(tabs)
    blk = lambda w: pl.BlockSpec((B, w), lambda n: (n, 0))
    o = _sds((S, LANES), F32)
    return _pcall(body, name="swa_bwd", grid=(S // B,), in_specs=_swa_specs(W, cb_q, cb_k) + [blk(W)],
                  out_specs=(blk(W), blk(LANES), blk(LANES), blk(LANES), blk(LANES), _full((SKR, LANES))),
                  out_shape=(_sds((S, W), BF16), o, o, o, o, _sds((SKR, LANES), F32)),
                  compiler_params=_cp("arbitrary"))(proj, proj, proj, proj, proj, *t, *t, sinks_row, do)


def _swa_kv_combine(dkc, dkp, dvc, dvp, tabs):
    S = dkc.shape[0]
    B = SW_BLOCK
    nb = S // B

    def fn(n, kc_ref, kp_ref, vc_ref, vp_ref, c0, c1, c2, o_ref):
        more = n < nb - 1
        dk = kc_ref[...] + jnp.where(more, kp_ref[...], 0.0)
        dv = vc_ref[...] + jnp.where(more, vp_ref[...], 0.0)
        o_ref[...] = jnp.concatenate([_unrope(dk, c0[...], c1[...], c2[...]), dv], axis=1).astype(BF16)

    cur = _rb(B, LANES)
    nxt = pl.BlockSpec((B, LANES), lambda n: (jnp.minimum(n + 1, nb - 1), 0))
    return _rows("swa_kv_combine", fn, S, B, (dkc, dkp, dvc, dvp, *tabs), [cur, nxt, cur, nxt, cur, cur, cur],
                 _sds((S, 2 * LANES), BF16), _rb(B, 2 * LANES))


ANY = pl.BlockSpec(memory_space=pl.ANY)


def _place():
    x, y, c = lax.axis_index("x"), lax.axis_index("y"), lax.axis_index("c")
    return x, y, c, [(1 - x, y), (x, 1 - y), (1 - x, 1 - y)]


def _comm_call(name, body, out_shapes, n_sems, n_local, *ins):
    return _pcall(body, name=name, out_shape=tuple(out_shapes), in_specs=[ANY] * len(ins), out_specs=tuple(ANY for _ in out_shapes),
                  scratch_shapes=[pltpu.SemaphoreType.DMA((n_sems,)), pltpu.SemaphoreType.DMA((n_sems,)),
                                  pltpu.SemaphoreType.DMA((n_local,))])(*ins)


def _remote(src, dst, send, recv, k, to):
    return pltpu.make_async_remote_copy(src_ref=src, dst_ref=dst, send_sem=send.at[k], recv_sem=recv.at[k], device_id=to,
                                        device_id_type=MESH)


def _chip_slice(ref, axis, s):
    if axis is None:
        return ref.at[s]
    q = ref.shape[axis] // 4
    start = s * q if isinstance(s, int) else pl.multiple_of(s * q, q)
    return ref.at[tuple([slice(None)] * axis + [pl.ds(start, q)])]


def _own_part(a, axis, me):
    if axis is None:
        return lax.dynamic_index_in_dim(a, me, 0, keepdims=False)
    q = a.shape[axis] // 4
    return lax.dynamic_slice_in_dim(a, me * q, q, axis)


HBM = pl.BlockSpec(memory_space=pltpu.HBM)
SEM = pl.BlockSpec(memory_space=pltpu.SEMAPHORE)
EFFECT = pltpu.SideEffectType.DATAFLOW_SIDE_EFFECTING


def _split_start(name, arrs, land_shapes, plan, nc, after=None):
    n = len(arrs)
    lands = [lax.empty(s.shape, s.dtype) for s in land_shapes]
    ins = list(arrs) + lands + ([] if after is None else [after])

    def body(*refs):
        outs = refs[len(ins):]
        for k, (src, dst, _, peer) in enumerate(plan(refs[:n], refs[n:n + len(lands)])):
            pltpu.make_async_remote_copy(src_ref=src, dst_ref=dst, send_sem=outs[k], recv_sem=outs[nc + k], device_id=peer,
                                         device_id_type=MESH).start()
        outs[-1][...] = jnp.zeros((SUBLANES, LANES), F32)

    nt = n + len(lands)
    thru = [pltpu.HBM(a.shape, a.dtype) for a in list(arrs) + lands]
    outs = _pcall(body, name=name, out_shape=tuple([pltpu.SemaphoreType.DMA(())] * (2 * nc) + thru + [_sds((SUBLANES, LANES), F32)]),
                  in_specs=[HBM] * nt + [ANY] * (len(ins) - nt),
                  out_specs=tuple([SEM] * (2 * nc) + [HBM] * nt + [pl.BlockSpec(memory_space=pltpu.VMEM)]),
                  input_output_aliases={i: 2 * nc + i for i in range(nt)},
                  compiler_params=pltpu.CompilerParams(has_side_effects=EFFECT))(
        *[pltpu.with_memory_space_constraint(a, pltpu.HBM) for a in ins[:nt]], *ins[nt:])
    return dict(sems=outs[:2 * nc], arrs=outs[2 * nc:2 * nc + n], lands=outs[2 * nc + n:2 * nc + nt], token=outs[-1], plan=plan, nc=nc)


def _split_wait(name, handle, after):
    arrs, lands, sems, nc = list(handle["arrs"]), list(handle["lands"]), list(handle["sems"]), handle["nc"]
    n, nt = len(arrs), len(arrs) + len(lands)

    def body(*refs):
        sem = refs[nt:nt + 2 * nc]
        for k, (src, _, landing, peer) in enumerate(handle["plan"](refs[:n], refs[n:nt])):
            cp = pltpu.make_async_remote_copy(src_ref=src, dst_ref=landing, send_sem=sem[k], recv_sem=sem[nc + k], device_id=peer,
                                              device_id_type=MESH)
            cp.wait_send()
            cp.wait_recv()

    thru = tuple(pltpu.HBM(a.shape, a.dtype) for a in arrs + lands)
    outs = _pcall(body, name=name, out_shape=thru, in_specs=[HBM] * nt + [SEM] * (2 * nc) + [ANY], out_specs=tuple([HBM] * nt),
                  input_output_aliases={i: i for i in range(nt)},
                  compiler_params=pltpu.CompilerParams(has_side_effects=EFFECT))(*arrs, *lands, *sems, after)
    return list(outs[:n]), list(outs[n:])


WHOLE = "whole"


def _plan_chips(axes):
    def plan(src, land):
        x, y, c, chips = _place()
        idx = [2 * cx + cy for cx, cy in chips]
        part = lambda a, s: src[a] if axes[a] is WHOLE else _chip_slice(src[a], axes[a], s)
        return [(part(a, idx[j]), land[a].at[2 * x + y], land[a].at[idx[j]], (*chips[j], c))
                for a in range(len(land)) for j in range(3)]
    return plan


def _plan_sibling(half):
    def plan(src, land):
        x, y, c, _ = _place()
        lh = lambda a: src[a].shape[0] // 2
        return [(src[a].at[pl.ds((1 - c) * lh(a), lh(a))] if half else src[a], land[a], land[a], (x, y, 1 - c))
                for a in range(len(land))]
    return plan


def _chips_start(name, arrs, axes, after=None):
    part = lambda a, ax: a.shape if ax is WHOLE else a.shape[1:] if ax is None else tuple(d // 4 if i == ax else d for i, d in enumerate(a.shape))
    return _split_start(name, arrs, [_sds((4,) + part(a, ax), a.dtype) for a, ax in zip(arrs, axes)], _plan_chips(axes), 3 * len(arrs), after)


def _sibling_start(name, arrs, half, after=None):
    shp = lambda a: (a.shape[0] // 2,) + a.shape[1:] if half else a.shape
    return _split_start(name, arrs, [_sds(shp(a), a.dtype) for a in arrs], _plan_sibling(half), len(arrs), after)


def _gather_all(name, b):
    R, C = b.shape
    flips = [(dx, dy, dc) for dx in (0, 1) for dy in (0, 1) for dc in (0, 1)][1:]

    def body(b_ref, o_ref, send, recv, lsem):
        x, y, c, _ = _place()
        me = 4 * x + 2 * y + c
        peers = [(x ^ dx, y ^ dy, c ^ dc) for dx, dy, dc in flips]
        mine = pltpu.make_async_copy(b_ref, o_ref.at[me], lsem.at[0])
        mine.start()
        cps = [_remote(b_ref, o_ref.at[me], send, recv, k, peer) for k, peer in enumerate(peers)]
        for cp in cps:
            cp.start()
        for k, (px, py, pc) in enumerate(peers):
            _remote(b_ref, o_ref.at[4 * px + 2 * py + pc], send, recv, k, (px, py, pc)).wait_recv()
        for cp in cps:
            cp.wait_send()
        mine.wait()

    return _comm_call(name, body, [_sds((8, R, C), b.dtype)], 7, 1, b)[0]


def _block_rows(rows, width):
    return _tile(rows, max(SUBLANES, (1 << 19) // width), SUBLANES)


def _add_half(name, g, got):
    L, A, B = g.shape
    Lh = L // 2
    tq = _block_rows(A, B)

    def body(c_ref, g_ref, r_ref, o_ref):
        o_ref[...] = (g_ref[...] + r_ref[...]).astype(BF16)

    spec = pltpu.PrefetchScalarGridSpec(
        num_scalar_prefetch=1, grid=(Lh, A // tq),
        in_specs=[pl.BlockSpec((1, tq, B), lambda l, i, c_ref: (c_ref[0] * Lh + l, i, 0)),
                  pl.BlockSpec((1, tq, B), lambda l, i, c_ref: (l, i, 0))],
        out_specs=pl.BlockSpec((1, tq, B), lambda l, i, c_ref: (l, i, 0)))
    return _pcall(body, name=name, grid_spec=spec, out_shape=_sds((Lh, A, B), BF16),
                  compiler_params=_cp("arbitrary", "arbitrary"))(lax.axis_index("c").reshape(1).astype(jnp.int32), g, got)


def _sum_slots(name, a):
    n, R, C = a.shape
    tq = _block_rows(R, n * C)

    def fn(i, a_ref, o_ref):
        t = a_ref[0].astype(F32)
        for s in range(1, n):
            t = t + a_ref[s].astype(F32)
        o_ref[...] = t

    return _rows(name, fn, R, tq, (a,), [pl.BlockSpec((n, tq, C), lambda i: (0, i, 0))], _sds((R, C), F32), _rb(tq, C))


def _adam_update(w, g, m, v):
    mn = ADAM_B1 * m + (1.0 - ADAM_B1) * g
    vn = ADAM_B2 * v + (1.0 - ADAM_B2) * (g * g)
    m_hat = mn / (1.0 - ADAM_B1 ** ADAM_STEP)
    v_hat = vn / (1.0 - ADAM_B2 ** ADAM_STEP)
    return -ADAM_LR * (m_hat / (jnp.sqrt(v_hat) + ADAM_EPS) + ADAM_WD * w), mn, vn


def _adamw(name, w, g, m, v):
    R, C = w.shape
    tq = _tile(R, 256, SUBLANES)

    def fn(i, w_ref, g_ref, m_ref, v_ref, d_ref, mo_ref, vo_ref):
        d_ref[...], mo_ref[...], vo_ref[...] = _adam_update(w_ref[...], g_ref[...], m_ref[...], v_ref[...])

    r, o = _rb(tq, C), _sds((R, C), F32)
    return _rows(name, fn, R, tq, (w, g, m, v), [r, r, r, r], (o, o, o), (r, r, r))


def _adamw_halves(name, w, mine, theirs, m, v, l0, prev=None):
    L, A, B = w.shape
    Lh = mine.shape[0]
    tq = _tile(A, 256, SUBLANES)

    def body(c_ref, w_ref, a_ref, b_ref, m_ref, v_ref, *refs):
        g_ref, d_ref, mo_ref, vo_ref = refs[-4:]
        is_mine = pl.program_id(0) // Lh == c_ref[0]
        g = jnp.where(is_mine, a_ref[...], b_ref[...])
        g_ref[...] = g
        d_ref[...], mo_ref[...], vo_ref[...] = _adam_update(w_ref[...], g, m_ref[...], v_ref[...])

    full = pl.BlockSpec((1, tq, B), lambda l, i, c_ref: (l0 + l, i, 0))
    half = pl.BlockSpec((1, tq, B), lambda l, i, c_ref: (l % Lh, i, 0))
    o = _sds((L, A, B), F32)
    prev = list(prev or ())
    spec = pltpu.PrefetchScalarGridSpec(num_scalar_prefetch=1, grid=(2 * Lh, A // tq), in_specs=[full, half, half, full, full] + [ANY] * len(prev),
                                        out_specs=(full, full, full, full))
    return _pcall(body, name=name, grid_spec=spec, out_shape=(o, o, o, o), input_output_aliases={6 + i: i for i in range(len(prev))},
                  compiler_params=_cp("arbitrary", "arbitrary"))(lax.axis_index("c").reshape(1).astype(jnp.int32), w, mine, theirs, m, v, *prev)


def _pack(arrs, width, lead=()):
    nl = len(lead)
    flat = jnp.concatenate([a.reshape(lead + (-1,)) for a in arrs], axis=nl)
    n = flat.shape[-1]
    unit = PACK_ROWS * width
    tot = -(-n // unit) * unit
    flat = jnp.pad(flat, [(0, 0)] * nl + [(0, tot - n)])
    return flat.reshape(lead + (tot // width, width))


def _unpack(buf, shapes, lead=()):
    flat = buf.reshape(lead + (-1,))
    out, off = [], 0
    for s in shapes:
        n = int(np.prod(s))
        out.append(flat[..., off:off + n].reshape(lead + tuple(s)))
        off += n
    return out


def _in_groups(W, H):
    o_sq = 4 * W + 2 * H
    o_k = o_sq + W
    o_g = o_k + 2 * KV_W
    return [(0, 4 * W), (o_sq, o_k), (o_g, o_g + 2 * W), (o_k, o_g), (4 * W, o_sq)]


def _relayout_in(shards, W, H):
    c4 = sum(hi - lo for lo, hi in _in_groups(W, H)) // 4
    parts = []
    for lo, hi in _in_groups(W, H):
        for s in range(4):
            a, b = max(lo, s * c4), min(hi, (s + 1) * c4)
            if a < b:
                parts.append(shards[s][:, a - s * c4:b - s * c4])
    parts.append(jnp.zeros((shards.shape[1], BA_W - 2 * H), shards.dtype))
    return jnp.concatenate(parts, axis=1)


def _shard_in(d, W, H):
    groups = _in_groups(W, H)
    starts = [sum(hi - lo for lo, hi in groups[:i]) for i in range(len(groups))]
    stored = sorted(zip(groups, starts))
    c4 = sum(hi - lo for lo, hi in groups) // 4
    out = []
    for s in range(4):
        parts = []
        for (lo, hi), at in stored:
            a, b = max(lo, s * c4), min(hi, (s + 1) * c4)
            if a < b:
                parts.append(d[:, :, at + a - lo:at + b - lo])
        out.append(jnp.concatenate(parts, axis=2))
    return jnp.stack(out)


def _lane_row(vals, at):
    return jnp.pad(vals, (at, LANES - at - vals.shape[0]))[None]


def _layer_fwd(x, lw, tabs, W, H, more=None, h=None, g1_next=None):
    D = x.shape[1]
    cbk = 7 * W // LANES
    if h is None:
        h = _pre_norm(x, lw["g1"])
    proj = _mm("mm_in", h, lw["win"], "nn", BF16, tm=4096, tn=512)
    ba = _mm("mm_ba", h, lw["win"][:, 7 * W + 2 * KV_W:], "nn", F32)
    qkv = _dn_prep(proj, lw["conv"], W)
    beta_b, g_b = _dn_gates(ba, lw["alog"], lw["dt"], H)
    o, st = _delta_fwd(qkv, beta_b, g_b, H, DELTA_CB, DELTA_HB)
    oa = _dn_out(o, proj, lw["ng"], W, 3)
    ob = _swa_fwd(proj, tabs, lw["sinks"], W, 4, cbk)
    if more is not None:
        lw.update(more(ob))
    ya = _mm("mm_up_dn", oa, lw["wup_dn"], "nn", BF16)
    yb = _mm("mm_up_sw", ob, lw["wup_sw"], "nn", BF16)
    mixin = _mix(proj, ya, yb, D, 5)
    mix = _mm("mm_o", mixin, lw["wo"], "nn", F32)
    x1, h2 = _post_mix(x, mix, lw["g2"], lw["g3"])
    f1, act = _mm("mm_ff1", h2, lw["wff1"], "nn", tn=512, out_dtypes=(BF16, BF16), epi=lambda acc: (acc, jnp.square(jnp.maximum(acc, 0.0))))
    ff = _mm("mm_ff2", act, lw["wff2"], "nn", F32)
    x2, h_next = _post_mlp(x1, ff, lw["g4"], lw["g4"] if g1_next is None else g1_next)
    saved = dict(x=x, h=h, proj=proj, ba=ba, qkv=qkv, beta_b=beta_b, g_b=g_b, o=o, st=st, oa=oa, ob=ob, ya=ya, yb=yb,
                 mixin=mixin, mix=mix, x1=x1, h2=h2, f1=f1, act=act, ff=ff)
    return x2, h_next, saved


def _layer_bwd(dx2, lw, sv, tabs, W, H, l, big, weights_done=None):
    D = dx2.shape[1]
    cbk = 7 * W // LANES
    big = dict(big)
    dff, dg4 = _post_mlp_bwd(sv["ff"], lw["g4"], dx2)
    df1 = _mm("mm_ff2_dx", dff, lw["wff2"], "nt", BF16, extras=(sv["f1"],),
              epi=lambda acc, f1: (acc * 2.0 * jnp.maximum(f1.astype(F32), 0.0),))
    big["w_ff2"] = _mm("mm_ff2_dw", sv["act"], dff, "tn", tm=1024, tk=2048, slab=(big["w_ff2"], l))
    dh2 = _mm("mm_ff1_dx", df1, lw["wff1"], "nt", F32)
    big["w_ff1"] = _mm("mm_ff1_dw", sv["h2"], df1, "tn", tk=2048, slab=(big["w_ff1"], l))
    dx1, dmix, dg3, dg2 = _mid_bwd(sv["x1"], lw["g3"], dh2, dx2, sv["mix"], lw["g2"])
    dmixin = _mm("mm_o_dx", dmix, lw["wo"], "nt", BF16)
    big["w_o"] = _mm("mm_o_dw", sv["mixin"], dmix, "tn", tk=2048, slab=(big["w_o"], l))
    dya, dyb, dga, dgb = _mix_bwd(sv["proj"], sv["ya"], sv["yb"], dmixin, D, 5)
    doa = _mm("mm_up_dn_dx", dya, lw["wup_dn"], "nt", BF16)
    big["w_up_dn"] = _mm("mm_up_dn_dw", sv["oa"], dya, "tn", tk=2048, slab=(big["w_up_dn"], l))
    dob = _mm("mm_up_sw_dx", dyb, lw["wup_sw"], "nt", BF16)
    big["w_up_sw"] = _mm("mm_up_sw_dw", sv["ob"], dyb, "tn", tk=2048, slab=(big["w_up_sw"], l))
    do, dz, dng = _dn_out_bwd(sv["o"], sv["proj"], lw["ng"], doa, W, 3)
    dqkvn, dbeta_b, dg_b = _delta_bwd(sv["qkv"], sv["beta_b"], sv["g_b"], sv["st"], do, H, DELTA_CB, DELTA_HB)
    dba, dalog, ddt = _dn_gates_bwd(sv["ba"], lw["alog"], lw["dt"], dbeta_b, dg_b, H)
    dc, dconv = _dn_prep_bwd_a(sv["proj"], lw["conv"], dqkvn, W)
    dqkv = _dn_prep_bwd_b(dc, lw["conv"], W)
    dq_sw, dkc, dkp, dvc, dvp, dsk = _swa_bwd(sv["proj"], tabs, lw["sinks"], dob, W, 4, cbk)
    dkv = _swa_kv_combine(dkc, dkp, dvc, dvp, tabs)
    dproj = jnp.concatenate([dqkv, dz, dq_sw, dga, dgb, dkv, dba], axis=1)
    big["w_in"] = _mm("mm_in_dw", sv["h"], dproj, "tn", tn=768, tk=4096, slab=(big["w_in"], l))
    win = lw["win"]
    if weights_done is not None:
        win = win + weights_done(big).astype(BF16)
    dh = _mm("mm_in_dx", dproj, win, "nt", F32, tk=768)
    dx, dg1 = _pre_norm_bwd(sv["x"], lw["g1"], dh, dx1)
    grads = dict(pre_mix_g=dg1[0], dn_conv_w=dconv, dn_a_log=dalog[0, H:2 * H], dn_dt_bias=ddt[0, H:2 * H], dn_norm_g=dng[0],
                 sw_sinks=dsk[:SW_Q_HEADS, 0], post_mix_g=dg2[0], pre_mlp_g=dg3[0], post_mlp_g=dg4[0])
    return dx, grads, big


_WEIGHTS = ["pre_mix_g", "w_in", "dn_conv_w", "dn_a_log", "dn_dt_bias", "dn_norm_g", "sw_sinks", "w_up_dn", "w_up_sw", "w_o",
            "post_mix_g", "pre_mlp_g", "w_ff1", "w_ff2", "post_mlp_g"]
_BIG = {"w_in": 2, "w_up_dn": 1, "w_up_sw": 1, "w_o": 1, "w_ff1": 2, "w_ff2": 1}
_SMALL = [n for n in _WEIGHTS if n not in _BIG]


def _step(P):
    x, target = P["x"][0], P["loss_target"][0]
    S, D = x.shape
    L = P["pre_mix_g"].shape[0]
    H, W = DN_HEADS, DN_HEADS * DN_DK
    assert W == D == SW_Q_HEADS * SW_HD and KV_W == LANES
    me = 2 * lax.axis_index("x") + lax.axis_index("y")

    assert L % 4 == 0
    names = list(_BIG) + ["dn_conv_w"]
    local = [P[n].astype(BF16) for n in _BIG] + [P["dn_conv_w"]]
    early_names = ("w_in", "dn_conv_w")
    tail_names = [n for n in names if n not in early_names]
    own_slot = lambda gathered, mine: [lax.dynamic_update_slice_in_dim(g, w[None], me, 0) for g, w in zip(gathered, mine)]
    gather = lambda name, arrs, after=None: _chips_start(name, arrs, [WHOLE] * len(arrs), after)
    arrived = lambda name, h, after, keys: dict(zip(keys, own_slot(*reversed(_split_wait(name, h, after)))))
    h_first = gather("weights_first_start", [P["w_in"][:1].astype(BF16), P["dn_conv_w"][:1]])
    tabs = _rope_tables(P["positions"].reshape(S, 1))
    h0 = _pre_norm(x, P["pre_mix_g"][0][None])
    early = arrived("weights_first_wait", h_first, h0, early_names)
    h_tail = gather("weights_tail_start", [a[:1] for n, a in zip(names, local) if n in tail_names], early["w_in"])
    h_next = gather("weights_next_start", [a[1:2] for a in local], h_tail["token"])
    h_rest = gather("weights_rest_start", [a[2:] for a in local], h_next["token"])

    def head(full, l, k):
        return dict(
            g1=P["pre_mix_g"][l][None], win=_relayout_in(full["w_in"][:, k], W, H),
            conv=jnp.concatenate([full["dn_conv_w"][s, k] for s in range(4)], axis=-1),
            alog=_lane_row(P["dn_a_log"][l], H), dt=_lane_row(P["dn_dt_bias"][l], H), ng=P["dn_norm_g"][l][None],
            sinks=_lane_row(P["sw_sinks"][l], 0), g2=P["post_mix_g"][l][None], g3=P["pre_mlp_g"][l][None], g4=P["post_mlp_g"][l][None])

    def tail(full, k):
        rows = lambda n: full[n][:, k].reshape(-1, full[n].shape[-1])
        return dict(wup_dn=rows("w_up_dn"), wup_sw=rows("w_up_sw"), wo=rows("w_o"), wff2=rows("w_ff2"),
                    wff1=jnp.concatenate([full["w_ff1"][s, k] for s in range(4)], axis=-1))

    lws = [head(early, 0, 0)]
    lws[0]["conv"] = lws[0]["conv"] + h_rest["token"][0, 0]

    saved, h = [], h0
    for l in range(L):
        if l == 1:
            late = arrived("weights_next_wait", h_next, x, names)
            lws.append({**head(late, 1, 0), **tail(late, 0)})
        if l == 2:
            late = arrived("weights_rest_wait", h_rest, x, names)
            lws.extend({**head(late, k + 2, k), **tail(late, k)} for k in range(L - 2))
        first_tail = lambda after: tail(arrived("weights_tail_wait", h_tail, after, tail_names), 0)
        g1_next = P["pre_mix_g"][l + 1][None] if l + 1 < L else None
        x, h, sv = _layer_fwd(x, lws[l], tabs, W, H, first_tail if l == 0 else None, h, g1_next)
        saved.append(sv)
    loss_row, dx = _loss_head(x, target)

    Lb = L // 2
    layer_grads = [None] * L
    F = 4 * P["w_ff1"].shape[2]
    per_layer = dict(w_in=(D, 7 * W + 2 * KV_W + BA_W), w_up_dn=(W, D), w_up_sw=(W, D), w_o=(D, D), w_ff1=(D, F), w_ff2=(F, D))
    batch = [{n: lax.empty((Lb,) + per_layer[n], F32) for n in _BIG} for _ in range(2)]
    axes = [None if n == "w_in" else ax for n, ax in _BIG.items()]

    def pair_sums(tag, h_swap, after):
        g, got = _split_wait("grad_swap_wait_" + tag, h_swap, after)
        part = {n: _add_half("grad_pair_add_%s_%s" % (tag, n), a, r) for n, a, r in zip(_BIG, g, got)}
        return [_shard_in(part[n], W, H) if n == "w_in" else part[n] for n in _BIG]

    def chip_sums(tag, h_scat, after):
        parts, slots = _split_wait("grad_scatter_wait_" + tag, h_scat, after)
        halves = []
        for n, s, a, ax in zip(_BIG, slots, parts, axes):
            s = lax.dynamic_update_slice_in_dim(s, _own_part(a, ax, me)[None], me, 0)
            halves.append(_sum_slots("grad_chip_sum_%s_%s" % (tag, n), s.reshape(4, -1, s.shape[-1])).reshape(s.shape[1:]))
        h = _sibling_start("grad_share_start_" + tag, halves, False)
        return _split_wait("grad_share_wait_" + tag, h, halves[0])

    swaps = {}

    def swap_start(tag):
        def hook(big):
            swaps[tag] = _sibling_start("grad_swap_start_" + tag, [big[n] for n in _BIG], True)
            return swaps[tag]["token"][0, 0]
        return hook

    for l in reversed(range(L)):
        hook = swap_start("hi") if l == Lb else swap_start("lo") if l == 0 else None
        dx, layer_grads[l], batch[l // Lb] = _layer_bwd(dx, lws[l], saved[l], tabs, W, H, l % Lb, batch[l // Lb], hook)
        if l == Lb - 1:
            h_scat_hi = _chips_start("grad_scatter_start_hi", pair_sums("hi", swaps["hi"], dx), axes)
            if l > 0:
                lws[l - 1]["g4"] = lws[l - 1]["g4"] + h_scat_hi["token"][0, 0]

    grads = {n: jnp.stack([layer_grads[l][n] for l in range(L)]) for n in _SMALL}
    small_shapes = [(1,)] + [grads[n].shape for n in _SMALL]
    slots = _gather_all("small_gather", _pack([loss_row[0, :1]] + [grads[n] for n in _SMALL], LANES))
    h_scat_lo = _chips_start("grad_scatter_start_lo", pair_sums("lo", swaps["lo"], slots), axes)
    tot = _sum_slots("small_sum", slots + h_scat_lo["token"][0, 0])
    small = _unpack(tot, small_shapes)
    loss = small[0][0]
    gsum, delta, new_m, new_v = dict(zip(_SMALL, small[1:])), {}, {}, {}
    cw = P["dn_conv_w"].shape[2]
    gsum["dn_conv_w"] = lax.dynamic_slice_in_dim(gsum["dn_conv_w"], me * cw, cw, axis=2)
    sm_shapes = [P[n].shape for n in _SMALL]
    outs = _adamw("adamw_small", *(_pack([src[pre + n] for n in _SMALL], LANES)
                                   for src, pre in ((P, ""), (gsum, ""), (P, "m_"), (P, "v_"))))
    for d, o in zip((delta, new_m, new_v), outs):
        d.update(zip(_SMALL, _unpack(o, sm_shapes)))

    upper = {n: _adamw_halves("adamw_hi_" + n, P[n], mine, their, P["m_" + n], P["v_" + n], Lb)
             for n, mine, their in zip(_BIG, *chip_sums("hi", h_scat_hi, outs[0]))}
    for n, mine, their in zip(_BIG, *chip_sums("lo", h_scat_lo, upper["w_in"][0])):
        gsum[n], delta[n], new_m[n], new_v[n] = _adamw_halves("adamw_lo_" + n, P[n], mine, their, P["m_" + n], P["v_" + n], 0, upper[n])

    return (loss, dx[None], *[gsum[n] for n in _WEIGHTS], *[delta[n] for n in _WEIGHTS],
            *[new_m[n] for n in _WEIGHTS], *[new_v[n] for n in _WEIGHTS])


def kernel(x, positions, pre_mix_g, w_in, dn_conv_w, dn_a_log, dn_dt_bias, dn_norm_g, sw_sinks, w_up_dn, w_up_sw, w_o, post_mix_g, pre_mlp_g, w_ff1, w_ff2, post_mlp_g, loss_target, m_pre_mix_g, m_w_in, m_dn_conv_w, m_dn_a_log, m_dn_dt_bias, m_dn_norm_g, m_sw_sinks, m_w_up_dn, m_w_up_sw, m_w_o, m_post_mix_g, m_pre_mlp_g, m_w_ff1, m_w_ff2, m_post_mlp_g, v_pre_mix_g, v_w_in, v_dn_conv_w, v_dn_a_log, v_dn_dt_bias, v_dn_norm_g, v_sw_sinks, v_w_up_dn, v_w_up_sw, v_w_o, v_post_mix_g, v_pre_mlp_g, v_w_ff1, v_w_ff2, v_post_mlp_g):
    vals = (x, positions, pre_mix_g, w_in, dn_conv_w, dn_a_log, dn_dt_bias, dn_norm_g, sw_sinks, w_up_dn, w_up_sw, w_o, post_mix_g, pre_mlp_g, w_ff1, w_ff2, post_mlp_g, loss_target, m_pre_mix_g, m_w_in, m_dn_conv_w, m_dn_a_log, m_dn_dt_bias, m_dn_norm_g, m_sw_sinks, m_w_up_dn, m_w_up_sw, m_w_o, m_post_mix_g, m_pre_mlp_g, m_w_ff1, m_w_ff2, m_post_mlp_g, v_pre_mix_g, v_w_in, v_dn_conv_w, v_dn_a_log, v_dn_dt_bias, v_dn_norm_g, v_sw_sinks, v_w_up_dn, v_w_up_sw, v_w_o, v_post_mix_g, v_pre_mlp_g, v_w_ff1, v_w_ff2, v_post_mlp_g)
    names = ["x", "positions"] + _WEIGHTS + ["loss_target"] + ["m_" + n for n in _WEIGHTS] + ["v_" + n for n in _WEIGHTS]
    return _step(dict(zip(names, vals)))
```

```python
import numpy as np
import jax
import jax.numpy as jnp
from jax import lax
from jax.experimental import pallas as pl
from jax.experimental.pallas import tpu as pltpu

F32, BF16 = jnp.float32, jnp.bfloat16
MESH = pl.DeviceIdType.MESH

DN_HEADS = 8
DN_DK = 128
DN_CONV = 4
DN_CHUNK = 64
SW_Q_HEADS = 16
SW_KV_HEADS = 2
SW_HD = 64
SW_BLOCK = 128
ROPE_THETA = 500000.0
ROT_DIM = SW_HD // 4
EPS = 1e-6
ADAM_LR, ADAM_B1, ADAM_B2, ADAM_EPS, ADAM_WD, ADAM_STEP = 0.001, 0.9, 0.999, 1e-08, 0.01, 10

LANES = 128
SUBLANES = 8
VMEM_LIMIT = 48 * 1024 * 1024
KV_W = SW_KV_HEADS * SW_HD
BA_W = 256
PACK_ROWS = 512
DELTA_CB = 4
DELTA_HB = 8


def _pcall(body, **kw):
    return pl.pallas_call(body, **kw)


def _cp(*sem):
    return pltpu.CompilerParams(dimension_semantics=sem, vmem_limit_bytes=VMEM_LIMIT)


def _tile(n, pref, unit=LANES):
    if n <= pref:
        return n
    t = (pref // unit) * unit
    while t > unit and n % t:
        t -= unit
    assert n % t == 0, (n, pref)
    return t


def _sds(shape, dtype):
    return jax.ShapeDtypeStruct(tuple(shape), dtype)


_DIMS = {"nn": ((1,), (0,)), "nt": ((1,), (1,)), "tn": ((0,), (0,))}


def _mm(name, a, b, mode, out_dtype=F32, tm=2048, tn=1024, tk=1024, extras=(), epi=None, out_dtypes=None, slab=None, extra_specs=None):
    if mode == "nn":
        (M, K), (_, N) = a.shape, b.shape
    elif mode == "nt":
        (M, K), (N, _) = a.shape, b.shape
    else:
        (K, M), (_, N) = a.shape, b.shape
    tm, tn, tk = _tile(M, tm), _tile(N, tn), _tile(K, tk)
    nk = K // tk
    a_spec = {"nn": pl.BlockSpec((tm, tk), lambda i, j, k: (i, k)),
              "nt": pl.BlockSpec((tm, tk), lambda i, j, k: (i, k)),
              "tn": pl.BlockSpec((tk, tm), lambda i, j, k: (k, i))}[mode]
    b_spec = {"nn": pl.BlockSpec((tk, tn), lambda i, j, k: (k, j)),
              "nt": pl.BlockSpec((tn, tk), lambda i, j, k: (j, k)),
              "tn": pl.BlockSpec((tk, tn), lambda i, j, k: (k, j))}[mode]
    dims = (_DIMS[mode], ((), ()))
    out_dtypes = tuple(out_dtypes or (out_dtype,))
    ne, no = len(extras), len(out_dtypes)
    o_spec = pl.BlockSpec((tm, tn), lambda i, j, k: (i, j))

    def body(*refs):
        a_ref, b_ref, ex = refs[0], refs[1], refs[2:2 + ne]
        outs = refs[-no:] if nk == 1 else refs[-1 - no:-1]
        part = lax.dot_general(a_ref[...], b_ref[...], dims, preferred_element_type=F32)

        def finish(acc):
            res = epi(acc, *[e[...] for e in ex]) if epi else (acc,)
            for o, r, dt in zip(outs, res, out_dtypes):
                if slab is None:
                    o[...] = r.astype(dt)
                else:
                    o[0] = r.astype(dt)

        if nk == 1:
            finish(part)
            return
        acc_ref, k = refs[-1], pl.program_id(2)

        @pl.when(k == 0)
        def _():
            acc_ref[...] = part

        @pl.when((k > 0) & (k < nk - 1))
        def _():
            acc_ref[...] += part

        @pl.when(k == nk - 1)
        def _():
            finish(acc_ref[...] + part)

    kw = dict(name=name, grid=(M // tm, N // tn, nk), scratch_shapes=[] if nk == 1 else [pltpu.VMEM((tm, tn), F32)],
              compiler_params=_cp("parallel", "parallel", "arbitrary"))
    if slab is not None:
        buf, l = slab
        return _pcall(body, in_specs=[a_spec, b_spec, ANY], out_specs=pl.BlockSpec((1, tm, tn), lambda i, j, k: (l, i, j)),
                      out_shape=_sds(buf.shape, buf.dtype), input_output_aliases={2: 0}, **kw)(a, b, buf)
    out = _pcall(body, in_specs=[a_spec, b_spec] + list(extra_specs or [o_spec] * ne), out_specs=tuple(o_spec for _ in out_dtypes),
                 out_shape=tuple(_sds((M, N), dt) for dt in out_dtypes), **kw)(a, b, *extras)
    return out if no > 1 else out[0]


def _rows(name, fn, n_rows, tq, ins, in_specs, out_shapes, out_specs):
    def body(*refs):
        fn(pl.program_id(0), *refs)

    return _pcall(body, name=name, grid=(n_rows // tq,), in_specs=in_specs, out_specs=out_specs,
                  out_shape=out_shapes, compiler_params=_cp("arbitrary"))(*ins)


def _rb(tq, w, cb=0):
    return pl.BlockSpec((tq, w), lambda i: (i, cb))


def _full(shape):
    return pl.BlockSpec(tuple(shape), lambda *_: (0,) * len(shape))


def _rms_fwd(x, g):
    r = lax.rsqrt(jnp.mean(x * x, axis=-1, keepdims=True) + EPS)
    return x * r * g


def _rms_bwd(x, g, dy):
    r = lax.rsqrt(jnp.mean(x * x, axis=-1, keepdims=True) + EPS)
    xh = x * r
    t = dy * g
    dx = r * (t - xh * jnp.mean(t * xh, axis=-1, keepdims=True))
    return dx, jnp.sum(dy * xh, axis=0, keepdims=True)


def _acc(i, ref, val):
    @pl.when(i == 0)
    def _():
        ref[...] = val

    @pl.when(i > 0)
    def _():
        ref[...] += val


def _sigmoid(x):
    return 0.5 * jnp.tanh(0.5 * x) + 0.5


def _pre_norm(x, g):
    S, D = x.shape
    tq = _tile(S, 512, SUBLANES)

    def fn(i, x_ref, g_ref, h_ref):
        h_ref[...] = _rms_fwd(x_ref[...], g_ref[...]).astype(BF16)

    return _rows("pre_norm", fn, S, tq, (x, g), [_rb(tq, D), _full((1, D))], _sds((S, D), BF16), _rb(tq, D))


def _post_mix(x, mix, g2, g3):
    S, D = x.shape
    tq = _tile(S, 512, SUBLANES)

    def fn(i, x_ref, m_ref, g2_ref, g3_ref, x1_ref, h2_ref):
        x1 = x_ref[...] + _rms_fwd(m_ref[...], g2_ref[...])
        x1_ref[...] = x1
        h2_ref[...] = _rms_fwd(x1, g3_ref[...]).astype(BF16)

    return _rows("post_mix", fn, S, tq, (x, mix, g2, g3), [_rb(tq, D), _rb(tq, D), _full((1, D)), _full((1, D))],
                 (_sds((S, D), F32), _sds((S, D), BF16)), (_rb(tq, D), _rb(tq, D)))


def _post_mlp(x1, ff, g4, g1_next):
    S, D = x1.shape
    tq = _tile(S, 512, SUBLANES)

    def fn(i, x_ref, f_ref, g_ref, gn_ref, o_ref, h_ref):
        x2 = x_ref[...] + _rms_fwd(f_ref[...], g_ref[...])
        o_ref[...] = x2
        h_ref[...] = _rms_fwd(x2, gn_ref[...]).astype(BF16)

    r, f = _rb(tq, D), _full((1, D))
    return _rows("post_mlp", fn, S, tq, (x1, ff, g4, g1_next), [r, r, f, f], (_sds((S, D), F32), _sds((S, D), BF16)), (r, r))


def _loss_head(y, target):
    S, D = y.shape
    tq = _tile(S, 512, SUBLANES)

    def fn(i, y_ref, t_ref, l_ref, d_ref):
        e = y_ref[...] - t_ref[...]
        d_ref[...] = e * (1.0 / D)
        part = jnp.sum(jnp.sum(e * e, axis=1, keepdims=True), axis=0, keepdims=True) * (0.5 / D)
        _acc(i, l_ref, jnp.broadcast_to(part, (1, LANES)))

    return _rows("loss_head", fn, S, tq, (y, target), [_rb(tq, D), _rb(tq, D)],
                 (_sds((1, LANES), F32), _sds((S, D), F32)), (_full((1, LANES)), _rb(tq, D)))


def _post_mlp_bwd(ff, g4, dx2):
    S, D = ff.shape
    tq = _tile(S, 512, SUBLANES)

    def fn(i, f_ref, g_ref, d_ref, o_ref, dg_ref):
        dx, dg = _rms_bwd(f_ref[...], g_ref[...], d_ref[...])
        o_ref[...] = dx.astype(BF16)
        _acc(i, dg_ref, dg)

    return _rows("post_mlp_bwd", fn, S, tq, (ff, g4, dx2), [_rb(tq, D), _full((1, D)), _rb(tq, D)],
                 (_sds((S, D), BF16), _sds((1, D), F32)), (_rb(tq, D), _full((1, D))))


def _mid_bwd(x1, g3, dh2, dx2, mix, g2):
    S, D = x1.shape
    tq = _tile(S, 256, SUBLANES)

    def fn(i, x_ref, g3_ref, dh_ref, dx2_ref, m_ref, g2_ref, dx1_ref, dm_ref, dg3_ref, dg2_ref):
        d, dg3 = _rms_bwd(x_ref[...], g3_ref[...], dh_ref[...])
        dx1 = dx2_ref[...] + d
        dx1_ref[...] = dx1
        dm, dg2 = _rms_bwd(m_ref[...], g2_ref[...], dx1)
        dm_ref[...] = dm.astype(BF16)
        _acc(i, dg3_ref, dg3)
        _acc(i, dg2_ref, dg2)

    r, f = _rb(tq, D), _full((1, D))
    return _rows("mid_bwd", fn, S, tq, (x1, g3, dh2, dx2, mix, g2), [r, f, r, r, r, f],
                 (_sds((S, D), F32), _sds((S, D), BF16), _sds((1, D), F32), _sds((1, D), F32)), (r, r, f, f))


def _pre_norm_bwd(x, g1, dh, dx1):
    S, D = x.shape
    tq = _tile(S, 512, SUBLANES)

    def fn(i, x_ref, g_ref, dh_ref, dx1_ref, dx_ref, dg_ref):
        d, dg = _rms_bwd(x_ref[...], g_ref[...], dh_ref[...])
        dx_ref[...] = dx1_ref[...] + d
        _acc(i, dg_ref, dg)

    r, f = _rb(tq, D), _full((1, D))
    return _rows("pre_norm_bwd", fn, S, tq, (x, g1, dh, dx1), [r, f, r, r], (_sds((S, D), F32), _sds((1, D), F32)), (r, f))


def _mix(proj, ya, yb, D, cb_a):
    S = ya.shape[0]
    tq = _tile(S, 256, SUBLANES)

    def fn(i, ga_ref, gb_ref, ya_ref, yb_ref, o_ref):
        ga, gb, ya, yb = (r[...].astype(F32) for r in (ga_ref, gb_ref, ya_ref, yb_ref))
        o_ref[...] = (_sigmoid(ga) * ya + _sigmoid(gb) * yb).astype(BF16)

    return _rows("mix", fn, S, tq, (proj, proj, ya, yb), [_rb(tq, D, cb_a), _rb(tq, D, cb_a + 1), _rb(tq, D), _rb(tq, D)],
                 _sds((S, D), BF16), _rb(tq, D))


def _mix_bwd(proj, ya, yb, dmixin, D, cb_a):
    S = ya.shape[0]
    tq = _tile(S, 256, SUBLANES)

    def fn(i, ga_ref, gb_ref, ya_ref, yb_ref, d_ref, dya_ref, dyb_ref, dga_ref, dgb_ref):
        ga, gb, ya, yb, d = (r[...].astype(F32) for r in (ga_ref, gb_ref, ya_ref, yb_ref, d_ref))
        sa, sb = _sigmoid(ga), _sigmoid(gb)
        dya_ref[...] = (d * sa).astype(BF16)
        dyb_ref[...] = (d * sb).astype(BF16)
        dga_ref[...] = (d * ya * sa * (1.0 - sa)).astype(BF16)
        dgb_ref[...] = (d * yb * sb * (1.0 - sb)).astype(BF16)

    r = _rb(tq, D)
    o = _sds((S, D), BF16)
    return _rows("mix_bwd", fn, S, tq, (proj, proj, ya, yb, dmixin), [_rb(tq, D, cb_a), _rb(tq, D, cb_a + 1), r, r, r],
                 (o, o, o, o), (r, r, r, r))


HALO = 16


def _shift_down(xe, k, tq):
    return pltpu.roll(xe, k, 0)[HALO:HALO + tq]


def _conv_pre(cur_ref, halo_ref, w_ref, i, tq):
    x = cur_ref[...].astype(F32)
    halo = jnp.where(i > 0, halo_ref[...].astype(F32), 0.0)
    xe = jnp.concatenate([halo, x], axis=0)
    xs = [x] + [_shift_down(xe, k, tq) for k in range(1, DN_CONV)]
    w = w_ref[...]
    c = sum(w[DN_CONV - 1 - k:DN_CONV - k, :] * xs[k] for k in range(DN_CONV))
    return c, xs


def _dn_prep(proj, conv_w, W):
    S = proj.shape[0]
    tq = _tile(S, 256, HALO)
    hb = tq // HALO

    def body(cur_ref, halo_ref, w_ref, o_ref):
        j, i = pl.program_id(0), pl.program_id(1)
        c, _ = _conv_pre(cur_ref, halo_ref, w_ref, i, tq)
        y = c * _sigmoid(c)
        scale = jnp.where(j == 0, DN_DK ** -0.5, 1.0)
        for h in range(W // DN_DK):
            sl = slice(h * DN_DK, (h + 1) * DN_DK)
            yh = y[:, sl]
            rs = lax.rsqrt(jnp.sum(yh * yh, axis=-1, keepdims=True) + EPS)
            o_ref[:, sl] = jnp.where(j == 2, yh, yh * rs * scale)

    return _pcall(body, name="dn_prep", grid=(3, S // tq),
                  in_specs=[pl.BlockSpec((tq, W), lambda j, i: (i, j)),
                            pl.BlockSpec((HALO, W), lambda j, i: (jnp.maximum(i * hb - 1, 0), j)),
                            pl.BlockSpec((DN_CONV, W), lambda j, i: (0, j))],
                  out_specs=pl.BlockSpec((tq, W), lambda j, i: (i, j)), out_shape=_sds((S, 3 * W), F32),
                  compiler_params=_cp("arbitrary", "arbitrary"))(proj, proj, conv_w)


def _dn_prep_bwd_a(proj, conv_w, dqkv, W):
    S = proj.shape[0]
    tq = _tile(S, 256, HALO)
    hb = tq // HALO

    def body(cur_ref, halo_ref, w_ref, d_ref, dc_ref, dw_ref):
        j, i = pl.program_id(0), pl.program_id(1)
        c, xs = _conv_pre(cur_ref, halo_ref, w_ref, i, tq)
        sg = _sigmoid(c)
        y = c * sg
        scale = jnp.where(j == 0, DN_DK ** -0.5, 1.0)
        dout = d_ref[0]
        dys = []
        for h in range(W // DN_DK):
            sl = slice(h * DN_DK, (h + 1) * DN_DK)
            yh, dh = y[:, sl], dout[:, sl]
            rs = lax.rsqrt(jnp.sum(yh * yh, axis=-1, keepdims=True) + EPS)
            yn = yh * rs
            dn = scale * rs * (dh - yn * jnp.sum(dh * yn, axis=-1, keepdims=True))
            dys.append(jnp.where(j == 2, dh, dn))
        dy = jnp.concatenate(dys, axis=1)
        dc = dy * (sg * (1.0 + c * (1.0 - sg)))
        dc_ref[...] = dc
        dw = jnp.concatenate([jnp.sum(dc * xs[DN_CONV - 1 - r], axis=0, keepdims=True) for r in range(DN_CONV)], axis=0)
        _acc(i, dw_ref, dw)

    return _pcall(body, name="dn_prep_bwd_a", grid=(3, S // tq),
                  in_specs=[pl.BlockSpec((tq, W), lambda j, i: (i, j)),
                            pl.BlockSpec((HALO, W), lambda j, i: (jnp.maximum(i * hb - 1, 0), j)),
                            pl.BlockSpec((DN_CONV, W), lambda j, i: (0, j)),
                            pl.BlockSpec((1, tq, W), lambda j, i: (j, i, 0))],
                  out_specs=(pl.BlockSpec((tq, W), lambda j, i: (i, j)), pl.BlockSpec((DN_CONV, W), lambda j, i: (0, j))),
                  out_shape=(_sds((S, 3 * W), F32), _sds((DN_CONV, 3 * W), F32)),
                  compiler_params=_cp("arbitrary", "arbitrary"))(proj, proj, conv_w, dqkv)


def _dn_prep_bwd_b(dc, conv_w, W):
    S = dc.shape[0]
    tq = _tile(S, 256, SUBLANES)
    hb = tq // SUBLANES
    nblk = S // tq

    def body(cur_ref, nxt_ref, w_ref, o_ref):
        i = pl.program_id(1)
        d = cur_ref[...]
        nxt = jnp.where(i < nblk - 1, nxt_ref[...], 0.0)
        de = jnp.concatenate([d, nxt], axis=0)
        w = w_ref[...]
        out = w[DN_CONV - 1:DN_CONV, :] * d
        for k in range(1, DN_CONV):
            out = out + w[DN_CONV - 1 - k:DN_CONV - k, :] * pltpu.roll(de, tq + SUBLANES - k, 0)[0:tq]
        o_ref[...] = out.astype(BF16)

    return _pcall(body, name="dn_prep_bwd_b", grid=(3, nblk),
                  in_specs=[pl.BlockSpec((tq, W), lambda j, i: (i, j)),
                            pl.BlockSpec((SUBLANES, W), lambda j, i: (jnp.minimum((i + 1) * hb, S // SUBLANES - 1), j)),
                            pl.BlockSpec((DN_CONV, W), lambda j, i: (0, j))],
                  out_specs=pl.BlockSpec((tq, W), lambda j, i: (i, j)), out_shape=_sds((S, 3 * W), BF16),
                  compiler_params=_cp("arbitrary", "arbitrary"))(dc, dc, conv_w)


def _gate_terms(ba, al, dt):
    u = ba + dt
    sp = jnp.maximum(u, 0.0) + jnp.log(1.0 + jnp.exp(-jnp.abs(u)))
    return _sigmoid(ba), -jnp.exp(al) * sp, u


def _dn_gates(ba, alog_row, dt_row, H):
    S = ba.shape[0]
    tq = _tile(S, 512, SUBLANES)
    W = H * DN_DK

    def fn(i, ba_ref, al_ref, dt_ref, be_ref, g_ref):
        bet, gg, _ = _gate_terms(ba_ref[...], al_ref[...], dt_ref[...])
        for h in range(H):
            sl = slice(h * DN_DK, (h + 1) * DN_DK)
            be_ref[:, sl] = jnp.broadcast_to(bet[:, h:h + 1], (tq, DN_DK))
            g_ref[:, sl] = jnp.broadcast_to(gg[:, H + h:H + h + 1], (tq, DN_DK))

    return _rows("dn_gates", fn, S, tq, (ba, alog_row, dt_row), [_rb(tq, LANES), _full((1, LANES)), _full((1, LANES))],
                 (_sds((S, W), F32), _sds((S, W), F32)), (_rb(tq, W), _rb(tq, W)))


def _dn_gates_bwd(ba, alog_row, dt_row, dbeta_b, dg_b, H):
    S = ba.shape[0]
    tq = _tile(S, 512, SUBLANES)
    W = H * DN_DK

    def fn(i, ba_ref, al_ref, dt_ref, db_ref, dg_ref, o_ref, dal_ref, ddt_ref):
        bet, gg, u = _gate_terms(ba_ref[...], al_ref[...], dt_ref[...])
        lane = lax.broadcasted_iota(jnp.int32, (tq, LANES), 1)
        d = jnp.zeros((tq, LANES), F32)
        for h in range(H):
            d = jnp.where(lane == h, db_ref[:, h * DN_DK:h * DN_DK + 1], d)
            d = jnp.where(lane == H + h, dg_ref[:, h * DN_DK:h * DN_DK + 1], d)
        is_a = (lane >= H) & (lane < 2 * H)
        da = jnp.where(is_a, d * (-jnp.exp(al_ref[...]) * _sigmoid(u)), 0.0)
        dlog = jnp.where(lane < H, d * bet * (1.0 - bet), da)
        o_ref[...] = jnp.concatenate([dlog, jnp.zeros((tq, BA_W - LANES), F32)], axis=1).astype(BF16)
        _acc(i, dal_ref, jnp.sum(jnp.where(is_a, d * gg, 0.0), axis=0, keepdims=True))
        _acc(i, ddt_ref, jnp.sum(da, axis=0, keepdims=True))

    f = _full((1, LANES))
    return _rows("dn_gates_bwd", fn, S, tq, (ba, alog_row, dt_row, dbeta_b, dg_b),
                 [_rb(tq, LANES), f, f, _rb(tq, W), _rb(tq, W)],
                 (_sds((S, BA_W), BF16), _sds((1, LANES), F32), _sds((1, LANES), F32)), (_rb(tq, BA_W), f, f))


def _dn_out(o, proj, ng, W, cb_z):
    S = o.shape[0]
    tq = _tile(S, 256, SUBLANES)

    def fn(i, o_ref, z_ref, g_ref, y_ref):
        for h in range(W // DN_DK):
            sl = slice(h * DN_DK, (h + 1) * DN_DK)
            z = z_ref[:, sl].astype(F32)
            y_ref[:, sl] = (_rms_fwd(o_ref[:, sl], g_ref[...]) * (z * _sigmoid(z))).astype(BF16)

    return _rows("dn_out", fn, S, tq, (o, proj, ng), [_rb(tq, W), _rb(tq, W, cb_z), _full((1, DN_DK))], _sds((S, W), BF16), _rb(tq, W))


def _dn_out_bwd(o, proj, ng, dy, W, cb_z):
    S = o.shape[0]
    tq = _tile(S, 256, SUBLANES)

    def fn(i, o_ref, z_ref, g_ref, d_ref, do_ref, dz_ref, dg_ref):
        g = g_ref[...]
        dg = jnp.zeros((1, DN_DK), F32)
        for h in range(W // DN_DK):
            sl = slice(h * DN_DK, (h + 1) * DN_DK)
            oh, z, d = o_ref[:, sl], z_ref[:, sl].astype(F32), d_ref[:, sl].astype(F32)
            sg = _sigmoid(z)
            dn = d * (z * sg)
            dz_ref[:, sl] = (d * _rms_fwd(oh, g) * (sg * (1.0 + z * (1.0 - sg)))).astype(BF16)
            dx, dgh = _rms_bwd(oh, g, dn)
            do_ref[:, sl] = dx
            dg = dg + dgh
        _acc(i, dg_ref, dg)

    r = _rb(tq, W)
    return _rows("dn_out_bwd", fn, S, tq, (o, proj, ng, dy), [r, _rb(tq, W, cb_z), _full((1, DN_DK)), r],
                 (_sds((S, W), F32), _sds((S, W), BF16), _sds((1, DN_DK), F32)), (r, r, _full((1, DN_DK))))


def _bdot(a, b, mode="nn"):
    return lax.dot_general(a.astype(BF16), b.astype(BF16), (_DIMS[mode], ((), ())), preferred_element_type=F32)


def _rsum(x):
    return jnp.broadcast_to(jnp.sum(x, axis=-1, keepdims=True), x.shape)


def _dot3(a, b, mode="nn"):
    ah, bh = a.astype(BF16), b.astype(BF16)
    al, bl = (a - ah.astype(F32)).astype(BF16), (b - bh.astype(F32)).astype(BF16)
    d = lambda x, y: lax.dot_general(x, y, (_DIMS[mode], ((), ())), preferred_element_type=F32)
    return d(ah, bh) + (d(al, bh) + d(ah, bl))


def _cumsum_rows(x, reverse=False):
    n = x.shape[0]
    row = lax.broadcasted_iota(jnp.int32, x.shape, 0)
    s = 1
    while s < n:
        if reverse:
            x = x + jnp.where(row < n - s, pltpu.roll(x, n - s, 0), 0.0)
        else:
            x = x + jnp.where(row >= s, pltpu.roll(x, s, 0), 0.0)
        s *= 2
    return x


def _each(f, *lists):
    return [f(*a) for a in zip(*lists)]


def _delta_local(qs, ks, vs, bes, grs):
    C = DN_CHUNK
    ri = lax.broadcasted_iota(jnp.int32, (C, C), 0)
    ci = lax.broadcasted_iota(jnp.int32, (C, C), 1)
    causal, strict = ri >= ci, ri > ci
    gcs = [_cumsum_rows(g) for g in grs]
    decays = [jnp.where(causal, jnp.exp(jnp.where(causal, gc[:, :C] - gc.T[:C, :], 0.0)), 0.0) for gc in gcs]
    egs = [jnp.exp(gc) for gc in gcs]
    eks = [jnp.exp(gc[C - 1:C, :] - gc) for gc in gcs]
    gams = [jnp.exp(gc[C - 1:C, :]) for gc in gcs]
    kbs = _each(lambda k, be: k * be, ks, bes)
    kks = _each(lambda kb, k: _bdot(kb, k, "nt"), kbs, ks)
    nls = _each(lambda kk, dc: jnp.where(strict, -kk * dc, 0.0), kks, decays)
    eye = (ri == ci).astype(F32)
    ts = [eye + nl for nl in nls]
    pws = [_dot3(nl, nl) for nl in nls]
    for s in range(4):
        both = _each(lambda t, pw: _dot3(jnp.concatenate([t, pw], axis=0), pw), ts, pws)
        ts = _each(lambda t, b: t + b[:C], ts, both)
        pws = [b[C:] for b in both]
    ts = _each(lambda t, pw: t + _dot3(t, pw), ts, pws)
    vbs = _each(lambda v, be: v * be, vs, bes)
    kbes = _each(lambda kb, eg: kb * eg, kbs, egs)
    uws = _each(lambda t, vb, kbe: _dot3(t, jnp.concatenate([vb, kbe], axis=1)), ts, vbs, kbes)
    us, ws = [uw[:, :DN_DK] for uw in uws], [uw[:, DN_DK:] for uw in uws]
    qks = _each(lambda q, k: _bdot(q, k, "nt"), qs, ks)
    return dict(decay=decays, eg=egs, ek=eks, gam=gams, kb=kbs, kk=kks, t=ts, vb=vbs, kbe=kbes, u=us, w=ws, qk=qks,
                a=_each(lambda qk, dc: qk * dc, qks, decays), qd=_each(lambda q, eg: q * eg, qs, egs),
                kd=_each(lambda k, ek: k * ek, ks, eks), strict=strict)


def _delta_items(refs, CB, HB):
    C, dk = DN_CHUNK, DN_DK
    return [[r[c * C:(c + 1) * C, h * dk:(h + 1) * dk] for h in range(HB) for c in range(CB)] for r in refs]


def _delta_fwd(qkv, beta_b, g_b, H, CB, HB):
    S = qkv.shape[0]
    C, dk = DN_CHUNK, DN_DK
    N = S // C
    R = CB * C
    G = H // HB

    def body(q_ref, k_ref, v_ref, b_ref, g_ref, o_ref, st_ref, s_ref):
        @pl.when(pl.program_id(1) == 0)
        def _():
            s_ref[...] = jnp.zeros((HB, dk, dk), F32)

        L = _delta_local(*_delta_items((q_ref, k_ref, v_ref, b_ref, g_ref), CB, HB))
        ss = [s_ref[h] for h in range(HB)]
        for c in range(CB):
            it = [h * CB + c for h in range(HB)]
            for h in range(HB):
                st_ref[h, c] = ss[h]
            wq = [_bdot(jnp.concatenate([L["w"][i], L["qd"][i]], axis=0), s) for i, s in zip(it, ss)]
            vns = [L["u"][i] - x[:C] for i, x in zip(it, wq)]
            outs = [x[C:] + _bdot(L["a"][i], vn) for i, x, vn in zip(it, wq, vns)]
            ss = [s * L["gam"][i] + _bdot(L["kd"][i], vn, "tn") for i, s, vn in zip(it, ss, vns)]
            for h in range(HB):
                o_ref[c * C:(c + 1) * C, h * dk:(h + 1) * dk] = outs[h]
        for h in range(HB):
            s_ref[h] = ss[h]

    blk = lambda off: pl.BlockSpec((R, HB * dk), lambda h, n: (n, off + h))
    return _pcall(body, name="delta_fwd", grid=(G, N // CB),
                  in_specs=[blk(0), blk(G), blk(2 * G), blk(0), blk(0)],
                  out_specs=(blk(0), pl.BlockSpec((HB, CB, dk, dk), lambda h, n: (h, n, 0, 0))),
                  out_shape=(_sds((S, H * dk), F32), _sds((H, N, dk, dk), F32)),
                  scratch_shapes=[pltpu.VMEM((HB, dk, dk), F32)],
                  compiler_params=_cp("arbitrary", "arbitrary"))(qkv, qkv, qkv, beta_b, g_b)


def _delta_bwd(qkv, beta_b, g_b, states, do, H, CB, HB):
    S = qkv.shape[0]
    C, dk = DN_CHUNK, DN_DK
    N = S // C
    R = CB * C
    NB = N // CB
    G = H // HB

    def body(q_ref, k_ref, v_ref, b_ref, g_ref, st_ref, do_ref, dqkv_ref, db_ref, dg_ref, ds_ref):
        @pl.when(pl.program_id(1) == 0)
        def _():
            ds_ref[...] = jnp.zeros((HB, dk, dk), F32)

        qs, ks, vs, bes, grs, dos = _delta_items((q_ref, k_ref, v_ref, b_ref, g_ref, do_ref), CB, HB)
        L = _delta_local(qs, ks, vs, bes, grs)
        ts, decays, kbs, egs, eks, gams, qds, kds = (L[n] for n in ("t", "decay", "kb", "eg", "ek", "gam", "qd", "kd"))
        s0s = [st_ref[h, c] for h in range(HB) for c in range(CB)]
        vns = _each(lambda u, w, s0: u - _bdot(w, s0), L["u"], L["w"], s0s)
        pre_dvn = _each(lambda a, d: _bdot(a, d, "tn"), L["a"], dos)
        pre_ds = _each(lambda qd, d: _bdot(qd, d, "tn"), qds, dos)
        das = _each(lambda d, vn: _bdot(d, vn, "nt"), dos, vns)
        ds = [ds_ref[h] for h in range(HB)]
        ds1s, dvns = [None] * (HB * CB), [None] * (HB * CB)
        for c in reversed(range(CB)):
            it = [h * CB + c for h in range(HB)]
            new = [pre_dvn[i] + _bdot(kds[i], d) for i, d in zip(it, ds)]
            for i, d, dv in zip(it, ds, new):
                ds1s[i], dvns[i] = d, dv
            ds = [pre_ds[i] + d * gams[i] - _bdot(L["w"][i], dv, "tn") for i, d, dv in zip(it, ds, new)]
        for h in range(HB):
            ds_ref[h] = ds[h]
        dkds = _each(lambda vn, d1: _bdot(vn, d1, "nt"), vns, ds1s)
        dgams = _each(lambda s0, d1: jnp.sum(jnp.sum(s0 * d1, axis=1, keepdims=True), axis=0, keepdims=True), s0s, ds1s)
        ost = _each(lambda d, dv, s0: _bdot(jnp.concatenate([d, dv], axis=0), s0, "nt"), dos, dvns, s0s)
        dqds, dws = [x[:C] for x in ost], [-x[C:] for x in ost]
        dvw = _each(lambda dv, dw: jnp.concatenate([dv, dw], axis=1), dvns, dws)
        tdvw = _each(lambda t, x: _dot3(t, x, "tn"), ts, dvw)
        dvbs, dkbes = [x[:, :dk] for x in tdvw], [x[:, dk:] for x in tdvw]
        dts = _each(lambda x, vb, kbe: _dot3(x, jnp.concatenate([vb, kbe], axis=1), "nt"), dvw, L["vb"], L["kbe"])
        tmp = _each(lambda dt, t: _dot3(dt, t, "nt"), dts, ts)
        dls = _each(lambda t, x: -_dot3(t, x, "tn"), ts, tmp)
        ms = _each(lambda dl, dc: jnp.where(L["strict"], dl * dc, 0.0), dls, decays)
        mas = _each(lambda da, dc: da * dc, das, decays)
        dkbs = _each(lambda m, k, dkbe, eg: _bdot(m, k) + dkbe * eg, ms, ks, dkbes, egs)
        dks = _each(lambda m, kb, ma, q, dkd, ek, dkb, be: _bdot(m, kb, "tn") + _bdot(ma, q, "tn") + dkd * ek + dkb * be,
                    ms, kbs, mas, qs, dkds, eks, dkbs, bes)
        dqs = _each(lambda ma, k, dqd, eg: _bdot(ma, k) + dqd * eg, mas, ks, dqds, egs)
        es = _each(lambda m, kk, ma, qk: m * kk + ma * qk, ms, L["kk"], mas, L["qk"])
        ones = jnp.ones((C, dk), BF16)
        row = lax.broadcasted_iota(jnp.int32, (C, dk), 0)
        for i in range(HB * CB):
            h, c = divmod(i, CB)
            rs, cs = slice(c * C, (c + 1) * C), slice(h * dk, (h + 1) * dk)
            e = es[i]
            e_hi = e.astype(BF16)
            col = _bdot(e_hi, ones, "tn") + _bdot(e - e_hi.astype(F32), ones, "tn")
            t_kd = _rsum(dkds[i] * kds[i])
            dgc = (jnp.broadcast_to(jnp.sum(e, axis=1, keepdims=True), (C, dk)) - col + _rsum(dqds[i] * qds[i]) - t_kd
                   + _rsum(dkbes[i] * L["kbe"][i]))
            dglast = jnp.sum(t_kd[:, 0:1], axis=0, keepdims=True) + dgams[i] * gams[i][:, 0:1]
            dgc = dgc + jnp.where(row == C - 1, dglast, 0.0)
            dqkv_ref[0, rs, cs] = dqs[i]
            dqkv_ref[1, rs, cs] = dks[i]
            dqkv_ref[2, rs, cs] = dvbs[i] * bes[i]
            db_ref[rs, cs] = _rsum(dkbs[i] * ks[i]) + _rsum(dvbs[i] * vs[i])
            dg_ref[rs, cs] = _cumsum_rows(dgc, reverse=True)

    blk = lambda off: pl.BlockSpec((R, HB * dk), lambda h, n: (NB - 1 - n, off + h))
    W = H * dk
    return _pcall(body, name="delta_bwd", grid=(G, NB),
                  in_specs=[blk(0), blk(G), blk(2 * G), blk(0), blk(0),
                            pl.BlockSpec((HB, CB, dk, dk), lambda h, n: (h, NB - 1 - n, 0, 0)), blk(0)],
                  out_specs=(pl.BlockSpec((3, R, HB * dk), lambda h, n: (0, NB - 1 - n, h)), blk(0), blk(0)),
                  out_shape=(_sds((3, S, W), F32), _sds((S, W), F32), _sds((S, W), F32)),
                  scratch_shapes=[pltpu.VMEM((HB, dk, dk), F32)],
                  compiler_params=_cp("arbitrary", "arbitrary"))(qkv, qkv, qkv, beta_b, g_b, states, do)


def _rope_consts():
    lane = np.arange(LANES) % SW_HD
    half = ROT_DIM // 2
    inv = (ROPE_THETA ** (-np.arange(half, dtype=np.float32) * np.float32(2.0 / ROT_DIM))).astype(np.float32)
    freq = np.where(lane < ROT_DIM, inv[lane % half], 0.0).astype(np.float32)
    lo = (lane < half).astype(np.float32)
    hi = ((lane >= half) & (lane < ROT_DIM)).astype(np.float32)
    return jnp.asarray(np.stack([freq, -lo, hi] + [np.zeros(LANES, np.float32)] * 5))


def _rope_tables(pos_col):
    S = pos_col.shape[0]
    tq = _tile(S, 1024, SUBLANES)

    def fn(i, p_ref, c_ref, cos_ref, s1_ref, s2_ref):
        ang = p_ref[...].astype(F32) * c_ref[0:1, :]
        sn = jnp.sin(ang)
        cos_ref[...] = jnp.cos(ang)
        s1_ref[...] = sn * c_ref[1:2, :]
        s2_ref[...] = sn * c_ref[2:3, :]

    o, r = _sds((S, LANES), F32), _rb(tq, LANES)
    return _rows("rope_tables", fn, S, tq, (pos_col, _rope_consts()), [_rb(tq, 1), _full((SUBLANES, LANES))], (o, o, o), (r, r, r))


def _wide(a, w):
    return a if w == LANES else jnp.tile(a, (1, w // LANES))


def _rope(x, cos, s1, s2):
    w, h = x.shape[1], ROT_DIM // 2
    return x * _wide(cos, w) + pltpu.roll(x, w - h, 1) * _wide(s1, w) + pltpu.roll(x, h, 1) * _wide(s2, w)


def _unrope(d, cos, s1, s2):
    w, h = d.shape[1], ROT_DIM // 2
    return d * _wide(cos, w) + pltpu.roll(d * _wide(s1, w), h, 1) + pltpu.roll(d * _wide(s2, w), w - h, 1)


def _swa_setup(n, q_ref, kc_ref, kp_ref, vc_ref, vp_ref, tc, tp):
    B = SW_BLOCK
    qr = _rope(q_ref[...].astype(F32), tc[0][...], tc[1][...], tc[2][...]) * (SW_HD ** -0.5)
    kw = jnp.concatenate([_rope(kp_ref[...].astype(F32), tp[0][...], tp[1][...], tp[2][...]),
                          _rope(kc_ref[...].astype(F32), tc[0][...], tc[1][...], tc[2][...])], axis=0)
    vw = jnp.concatenate([vp_ref[...], vc_ref[...]], axis=0).astype(F32)
    lane = lax.broadcasted_iota(jnp.int32, (2 * B, LANES), 1)
    heads = []
    for hk in range(SW_KV_HEADS):
        kh, vh = kw[:, hk * SW_HD:(hk + 1) * SW_HD], vw[:, hk * SW_HD:(hk + 1) * SW_HD]
        kk, vv = jnp.concatenate([kh, kh], axis=1), jnp.concatenate([vh, vh], axis=1)
        heads.append(tuple(jnp.where(sel, t, 0.0).astype(BF16) for t in (kk, vv) for sel in (lane < SW_HD, lane >= SW_HD)))
    prev = lax.broadcasted_iota(jnp.int32, (B, B), 1) > lax.broadcasted_iota(jnp.int32, (B, B), 0)
    return qr, heads, (prev, jnp.where(prev & (n == 0), -1e30, 0.0)), lane


def _fold(x, prev):
    return jnp.where(prev, x[:, :SW_BLOCK], x[:, SW_BLOCK:])


def _unfold(x, prev):
    return jnp.concatenate([jnp.where(prev, x, 0.0), jnp.where(prev, 0.0, x)], axis=1)


SWA_GROUPS = 4
SWA_GROUPS_BWD = 2


def _swa_probs(items, qs, heads, fold, sk_ref, G2):
    prev, bias = fold
    ss = [_fold(_bdot(qs[j], heads[j // G2][half], "nt"), prev) + bias for j, half in items]
    sks = [sk_ref[0:1, 2 * j + half:2 * j + half + 1] for j, half in items]
    ms = [jnp.maximum(jnp.max(s, axis=-1, keepdims=True), sk) for s, sk in zip(ss, sks)]
    ps = [jnp.exp(s - m) for s, m in zip(ss, ms)]
    es = [jnp.exp(sk - m) for sk, m in zip(sks, ms)]
    inv = [1.0 / (jnp.sum(p, axis=-1, keepdims=True) + e) for p, e in zip(ps, es)]
    return [p * i for p, i in zip(ps, inv)], [e * i for e, i in zip(es, inv)]


def _swa_specs(W, cb_q, cb_k):
    B = SW_BLOCK
    assert (W // LANES) % SWA_GROUPS == 0 and (W // LANES) % SWA_GROUPS_BWD == 0
    cur = lambda w, cb: pl.BlockSpec((B, w), lambda n: (n, cb))
    prv = lambda w, cb: pl.BlockSpec((B, w), lambda n: (jnp.maximum(n - 1, 0), cb))
    specs = [cur(W, cb_q), cur(LANES, cb_k), prv(LANES, cb_k), cur(LANES, cb_k + 1), prv(LANES, cb_k + 1)]
    return specs + [cur(LANES, 0)] * 3 + [prv(LANES, 0)] * 3 + [_full((1, LANES))]


def _swa_fwd(proj, tabs, sinks_row, W, cb_q, cb_k):
    S = proj.shape[0]
    G2 = SW_Q_HEADS // SW_KV_HEADS // 2

    def body(q_ref, kc_ref, kp_ref, vc_ref, vp_ref, c0, c1, c2, p0, p1, p2, sk_ref, o_ref):
        n = pl.program_id(0)
        qr, heads, fold, _ = _swa_setup(n, q_ref, kc_ref, kp_ref, vc_ref, vp_ref, (c0, c1, c2), (p0, p1, p2))
        qs = [qr[:, j * LANES:(j + 1) * LANES].astype(BF16) for j in range(W // LANES)]
        for j0 in range(0, W // LANES, SWA_GROUPS):
            items = [(j, half) for j in range(j0, j0 + SWA_GROUPS) for half in range(2)]
            probs, _ = _swa_probs(items, qs, heads, fold, sk_ref, G2)
            pv = [_bdot(_unfold(p, fold[0]), heads[j // G2][2 + half]) for p, (j, half) in zip(probs, items)]
            for g in range(SWA_GROUPS):
                o_ref[:, (j0 + g) * LANES:(j0 + g + 1) * LANES] = (pv[2 * g] + pv[2 * g + 1]).astype(BF16)

    t = tuple(tabs)
    return _pcall(body, name="swa_fwd", grid=(S // SW_BLOCK,), in_specs=_swa_specs(W, cb_q, cb_k),
                  out_specs=pl.BlockSpec((SW_BLOCK, W), lambda n: (n, 0)), out_shape=_sds((S, W), BF16),
                  compiler_params=_cp("arbitrary"))(proj, proj, proj, proj, proj, *t, *t, sinks_row)


def _swa_bwd(proj, tabs, sinks_row, do, W, cb_q, cb_k):
    S = proj.shape[0]
    B = SW_BLOCK
    G2 = SW_Q_HEADS // SW_KV_HEADS // 2
    SKR = -(-SW_Q_HEADS // SUBLANES) * SUBLANES

    def body(q_ref, kc_ref, kp_ref, vc_ref, vp_ref, c0, c1, c2, p0, p1, p2, sk_ref, do_ref,
             dq_ref, dkc_ref, dkp_ref, dvc_ref, dvp_ref, dsk_ref):
        n = pl.program_id(0)
        qr, heads, fold, lane = _swa_setup(n, q_ref, kc_ref, kp_ref, vc_ref, vp_ref, (c0, c1, c2), (p0, p1, p2))
        prev = fold[0]

        @pl.when(n == 0)
        def _():
            dsk_ref[...] = jnp.zeros((SKR, LANES), F32)

        acc_k = [jnp.zeros((2 * B, LANES), F32) for _ in range(SW_KV_HEADS)]
        acc_v = [jnp.zeros((2 * B, LANES), F32) for _ in range(SW_KV_HEADS)]
        qs = [qr[:, j * LANES:(j + 1) * LANES].astype(BF16) for j in range(W // LANES)]
        dos = [do_ref[:, j * LANES:(j + 1) * LANES].astype(BF16) for j in range(W // LANES)]
        dqs = []
        for j0 in range(0, W // LANES, SWA_GROUPS_BWD):
            items = [(j, half) for j in range(j0, j0 + SWA_GROUPS_BWD) for half in range(2)]
            probs, psinks = _swa_probs(items, qs, heads, fold, sk_ref, G2)
            dps = [_fold(_bdot(dos[j], heads[j // G2][2 + half], "nt"), prev) for j, half in items]
            deltas = [jnp.sum(p * dp, axis=-1, keepdims=True) for p, dp in zip(probs, dps)]
            dss = [_unfold(p * (dp - dl), prev).astype(BF16) for p, dp, dl in zip(probs, dps, deltas)]
            pbs = [_unfold(p, prev).astype(BF16) for p in probs]
            dqp = [_bdot(ds, heads[j // G2][half]) for ds, (j, half) in zip(dss, items)]
            dkk = [_bdot(ds, qs[j], "tn") for ds, (j, half) in zip(dss, items)]
            dvv = [_bdot(p, dos[j], "tn") for p, (j, half) in zip(pbs, items)]
            for i, (j, half) in enumerate(items):
                hk, h = j // G2, 2 * j + half
                sel = (lane < SW_HD) if half == 0 else (lane >= SW_HD)
                acc_k[hk] = acc_k[hk] + jnp.where(sel, dkk[i], 0.0)
                acc_v[hk] = acc_v[hk] + jnp.where(sel, dvv[i], 0.0)
                dsk_ref[h:h + 1, :] += jnp.broadcast_to(-jnp.sum(psinks[i] * deltas[i], axis=0, keepdims=True), (1, LANES))
            dqs += [dqp[2 * g] + dqp[2 * g + 1] for g in range(SWA_GROUPS_BWD)]
        dq = jnp.concatenate(dqs, axis=1) * (SW_HD ** -0.5)
        dq_ref[...] = _unrope(dq, c0[...], c1[...], c2[...]).astype(BF16)
        fold = lambda a: a[:, :SW_HD] + a[:, SW_HD:]
        dkw = jnp.concatenate([fold(a) for a in acc_k], axis=1)
        dvw = jnp.concatenate([fold(a) for a in acc_v], axis=1)
        dkp_ref[...], dkc_ref[...] = dkw[:B], dkw[B:]
        dvp_ref[...], dvc_ref[...] = dvw[:B], dvw[B:]

    t = tuple(tabs)
    blk = lambda w: pl.BlockSpec((B, w), lambda n: (n, 0))
    o = _sds((S, LANES), F32)
    return _pcall(body, name="swa_bwd", grid=(S // B,), in_specs=_swa_specs(W, cb_q, cb_k) + [blk(W)],
                  out_specs=(blk(W), blk(LANES), blk(LANES), blk(LANES), blk(LANES), _full((SKR, LANES))),
                  out_shape=(_sds((S, W), BF16), o, o, o, o, _sds((SKR, LANES), F32)),
                  compiler_params=_cp("arbitrary"))(proj, proj, proj, proj, proj, *t, *t, sinks_row, do)


def _swa_kv_combine(dkc, dkp, dvc, dvp, tabs):
    S = dkc.shape[0]
    B = SW_BLOCK
    nb = S // B

    def fn(n, kc_ref, kp_ref, vc_ref, vp_ref, c0, c1, c2, o_ref):
        more = n < nb - 1
        dk = kc_ref[...] + jnp.where(more, kp_ref[...], 0.0)
        dv = vc_ref[...] + jnp.where(more, vp_ref[...], 0.0)
        o_ref[...] = jnp.concatenate([_unrope(dk, c0[...], c1[...], c2[...]), dv], axis=1).astype(BF16)

    cur = _rb(B, LANES)
    nxt = pl.BlockSpec((B, LANES), lambda n: (jnp.minimum(n + 1, nb - 1), 0))
    return _rows("swa_kv_combine", fn, S, B, (dkc, dkp, dvc, dvp, *tabs), [cur, nxt, cur, nxt, cur, cur, cur],
                 _sds((S, 2 * LANES), BF16), _rb(B, 2 * LANES))


ANY = pl.BlockSpec(memory_space=pl.ANY)


def _place():
    x, y, c = lax.axis_index("x"), lax.axis_index("y"), lax.axis_index("c")
    return x, y, c, [(1 - x, y), (x, 1 - y), (1 - x, 1 - y)]


def _comm_call(name, body, out_shapes, n_sems, n_local, *ins):
    return _pcall(body, name=name, out_shape=tuple(out_shapes), in_specs=[ANY] * len(ins), out_specs=tuple(ANY for _ in out_shapes),
                  scratch_shapes=[pltpu.SemaphoreType.DMA((n_sems,)), pltpu.SemaphoreType.DMA((n_sems,)),
                                  pltpu.SemaphoreType.DMA((n_local,))])(*ins)


def _remote(src, dst, send, recv, k, to):
    return pltpu.make_async_remote_copy(src_ref=src, dst_ref=dst, send_sem=send.at[k], recv_sem=recv.at[k], device_id=to,
                                        device_id_type=MESH)


def _chip_slice(ref, axis, s):
    if axis is None:
        return ref.at[s]
    q = ref.shape[axis] // 4
    start = s * q if isinstance(s, int) else pl.multiple_of(s * q, q)
    return ref.at[tuple([slice(None)] * axis + [pl.ds(start, q)])]


def _own_part(a, axis, me):
    if axis is None:
        return lax.dynamic_index_in_dim(a, me, 0, keepdims=False)
    q = a.shape[axis] // 4
    return lax.dynamic_slice_in_dim(a, me * q, q, axis)


HBM = pl.BlockSpec(memory_space=pltpu.HBM)
SEM = pl.BlockSpec(memory_space=pltpu.SEMAPHORE)
EFFECT = pltpu.SideEffectType.DATAFLOW_SIDE_EFFECTING


def _split_start(name, arrs, land_shapes, plan, nc, after=None):
    n = len(arrs)
    lands = [lax.empty(s.shape, s.dtype) for s in land_shapes]
    ins = list(arrs) + lands + ([] if after is None else [after])

    def body(*refs):
        outs = refs[len(ins):]
        for k, (src, dst, _, peer) in enumerate(plan(refs[:n], refs[n:n + len(lands)])):
            pltpu.make_async_remote_copy(src_ref=src, dst_ref=dst, send_sem=outs[k], recv_sem=outs[nc + k], device_id=peer,
                                         device_id_type=MESH).start()
        outs[-1][...] = jnp.zeros((SUBLANES, LANES), F32)

    nt = n + len(lands)
    thru = [pltpu.HBM(a.shape, a.dtype) for a in list(arrs) + lands]
    outs = _pcall(body, name=name, out_shape=tuple([pltpu.SemaphoreType.DMA(())] * (2 * nc) + thru + [_sds((SUBLANES, LANES), F32)]),
                  in_specs=[HBM] * nt + [ANY] * (len(ins) - nt),
                  out_specs=tuple([SEM] * (2 * nc) + [HBM] * nt + [pl.BlockSpec(memory_space=pltpu.VMEM)]),
                  input_output_aliases={i: 2 * nc + i for i in range(nt)},
                  compiler_params=pltpu.CompilerParams(has_side_effects=EFFECT))(
        *[pltpu.with_memory_space_constraint(a, pltpu.HBM) for a in ins[:nt]], *ins[nt:])
    return dict(sems=outs[:2 * nc], arrs=outs[2 * nc:2 * nc + n], lands=outs[2 * nc + n:2 * nc + nt], token=outs[-1], plan=plan, nc=nc)


def _split_wait(name, handle, after):
    arrs, lands, sems, nc = list(handle["arrs"]), list(handle["lands"]), list(handle["sems"]), handle["nc"]
    n, nt = len(arrs), len(arrs) + len(lands)

    def body(*refs):
        sem = refs[nt:nt + 2 * nc]
        for k, (src, _, landing, peer) in enumerate(handle["plan"](refs[:n], refs[n:nt])):
            cp = pltpu.make_async_remote_copy(src_ref=src, dst_ref=landing, send_sem=sem[k], recv_sem=sem[nc + k], device_id=peer,
                                              device_id_type=MESH)
            cp.wait_send()
            cp.wait_recv()

    thru = tuple(pltpu.HBM(a.shape, a.dtype) for a in arrs + lands)
    outs = _pcall(body, name=name, out_shape=thru, in_specs=[HBM] * nt + [SEM] * (2 * nc) + [ANY], out_specs=tuple([HBM] * nt),
                  input_output_aliases={i: i for i in range(nt)},
                  compiler_params=pltpu.CompilerParams(has_side_effects=EFFECT))(*arrs, *lands, *sems, after)
    return list(outs[:n]), list(outs[n:])


WHOLE = "whole"


def _plan_chips(axes):
    def plan(src, land):
        x, y, c, chips = _place()
        idx = [2 * cx + cy for cx, cy in chips]
        part = lambda a, s: src[a] if axes[a] is WHOLE else _chip_slice(src[a], axes[a], s)
        return [(part(a, idx[j]), land[a].at[2 * x + y], land[a].at[idx[j]], (*chips[j], c))
                for a in range(len(land)) for j in range(3)]
    return plan


def _plan_sibling(half):
    def plan(src, land):
        x, y, c, _ = _place()
        lh = lambda a: src[a].shape[0] // 2
        return [(src[a].at[pl.ds((1 - c) * lh(a), lh(a))] if half else src[a], land[a], land[a], (x, y, 1 - c))
                for a in range(len(land))]
    return plan


def _chips_start(name, arrs, axes, after=None):
    part = lambda a, ax: a.shape if ax is WHOLE else a.shape[1:] if ax is None else tuple(d // 4 if i == ax else d for i, d in enumerate(a.shape))
    return _split_start(name, arrs, [_sds((4,) + part(a, ax), a.dtype) for a, ax in zip(arrs, axes)], _plan_chips(axes), 3 * len(arrs), after)


def _sibling_start(name, arrs, half, after=None):
    shp = lambda a: (a.shape[0] // 2,) + a.shape[1:] if half else a.shape
    return _split_start(name, arrs, [_sds(shp(a), a.dtype) for a in arrs], _plan_sibling(half), len(arrs), after)


def _gather_all(name, b):
    R, C = b.shape
    flips = [(dx, dy, dc) for dx in (0, 1) for dy in (0, 1) for dc in (0, 1)][1:]

    def body(b_ref, o_ref, send, recv, lsem):
        x, y, c, _ = _place()
        me = 4 * x + 2 * y + c
        peers = [(x ^ dx, y ^ dy, c ^ dc) for dx, dy, dc in flips]
        mine = pltpu.make_async_copy(b_ref, o_ref.at[me], lsem.at[0])
        mine.start()
        cps = [_remote(b_ref, o_ref.at[me], send, recv, k, peer) for k, peer in enumerate(peers)]
        for cp in cps:
            cp.start()
        for k, (px, py, pc) in enumerate(peers):
            _remote(b_ref, o_ref.at[4 * px + 2 * py + pc], send, recv, k, (px, py, pc)).wait_recv()
        for cp in cps:
            cp.wait_send()
        mine.wait()

    return _comm_call(name, body, [_sds((8, R, C), b.dtype)], 7, 1, b)[0]


def _block_rows(rows, width):
    return _tile(rows, max(SUBLANES, (1 << 19) // width), SUBLANES)


def _add_half(name, g, got):
    L, A, B = g.shape
    Lh = L // 2
    tq = _block_rows(A, B)

    def body(c_ref, g_ref, r_ref, o_ref):
        o_ref[...] = (g_ref[...] + r_ref[...]).astype(BF16)

    spec = pltpu.PrefetchScalarGridSpec(
        num_scalar_prefetch=1, grid=(Lh, A // tq),
        in_specs=[pl.BlockSpec((1, tq, B), lambda l, i, c_ref: (c_ref[0] * Lh + l, i, 0)),
                  pl.BlockSpec((1, tq, B), lambda l, i, c_ref: (l, i, 0))],
        out_specs=pl.BlockSpec((1, tq, B), lambda l, i, c_ref: (l, i, 0)))
    return _pcall(body, name=name, grid_spec=spec, out_shape=_sds((Lh, A, B), BF16),
                  compiler_params=_cp("arbitrary", "arbitrary"))(lax.axis_index("c").reshape(1).astype(jnp.int32), g, got)


def _sum_slots(name, a):
    n, R, C = a.shape
    tq = _block_rows(R, n * C)

    def fn(i, a_ref, o_ref):
        t = a_ref[0].astype(F32)
        for s in range(1, n):
            t = t + a_ref[s].astype(F32)
        o_ref[...] = t

    return _rows(name, fn, R, tq, (a,), [pl.BlockSpec((n, tq, C), lambda i: (0, i, 0))], _sds((R, C), F32), _rb(tq, C))


def _adam_update(w, g, m, v):
    mn = ADAM_B1 * m + (1.0 - ADAM_B1) * g
    vn = ADAM_B2 * v + (1.0 - ADAM_B2) * (g * g)
    m_hat = mn / (1.0 - ADAM_B1 ** ADAM_STEP)
    v_hat = vn / (1.0 - ADAM_B2 ** ADAM_STEP)
    return -ADAM_LR * (m_hat / (jnp.sqrt(v_hat) + ADAM_EPS) + ADAM_WD * w), mn, vn


def _adamw(name, w, g, m, v):
    R, C = w.shape
    tq = _tile(R, 256, SUBLANES)

    def fn(i, w_ref, g_ref, m_ref, v_ref, d_ref, mo_ref, vo_ref):
        d_ref[...], mo_ref[...], vo_ref[...] = _adam_update(w_ref[...], g_ref[...], m_ref[...], v_ref[...])

    r, o = _rb(tq, C), _sds((R, C), F32)
    return _rows(name, fn, R, tq, (w, g, m, v), [r, r, r, r], (o, o, o), (r, r, r))


def _adamw_halves(name, w, mine, theirs, m, v, l0, prev=None):
    L, A, B = w.shape
    Lh = mine.shape[0]
    tq = _tile(A, 256, SUBLANES)

    def body(c_ref, w_ref, a_ref, b_ref, m_ref, v_ref, *refs):
        g_ref, d_ref, mo_ref, vo_ref = refs[-4:]
        is_mine = pl.program_id(0) // Lh == c_ref[0]
        g = jnp.where(is_mine, a_ref[...], b_ref[...])
        g_ref[...] = g
        d_ref[...], mo_ref[...], vo_ref[...] = _adam_update(w_ref[...], g, m_ref[...], v_ref[...])

    full = pl.BlockSpec((1, tq, B), lambda l, i, c_ref: (l0 + l, i, 0))
    half = pl.BlockSpec((1, tq, B), lambda l, i, c_ref: (l % Lh, i, 0))
    o = _sds((L, A, B), F32)
    prev = list(prev or ())
    spec = pltpu.PrefetchScalarGridSpec(num_scalar_prefetch=1, grid=(2 * Lh, A // tq), in_specs=[full, half, half, full, full] + [ANY] * len(prev),
                                        out_specs=(full, full, full, full))
    return _pcall(body, name=name, grid_spec=spec, out_shape=(o, o, o, o), input_output_aliases={6 + i: i for i in range(len(prev))},
                  compiler_params=_cp("arbitrary", "arbitrary"))(lax.axis_index("c").reshape(1).astype(jnp.int32), w, mine, theirs, m, v, *prev)


def _pack(arrs, width, lead=()):
    nl = len(lead)
    flat = jnp.concatenate([a.reshape(lead + (-1,)) for a in arrs], axis=nl)
    n = flat.shape[-1]
    unit = PACK_ROWS * width
    tot = -(-n // unit) * unit
    flat = jnp.pad(flat, [(0, 0)] * nl + [(0, tot - n)])
    return flat.reshape(lead + (tot // width, width))


def _unpack(buf, shapes, lead=()):
    flat = buf.reshape(lead + (-1,))
    out, off = [], 0
    for s in shapes:
        n = int(np.prod(s))
        out.append(flat[..., off:off + n].reshape(lead + tuple(s)))
        off += n
    return out


def _in_groups(W, H):
    o_sq = 4 * W + 2 * H
    o_k = o_sq + W
    o_g = o_k + 2 * KV_W
    return [(0, 4 * W), (o_sq, o_k), (o_g, o_g + 2 * W), (o_k, o_g), (4 * W, o_sq)]


def _relayout_in(shards, W, H):
    c4 = sum(hi - lo for lo, hi in _in_groups(W, H)) // 4
    parts = []
    for lo, hi in _in_groups(W, H):
        for s in range(4):
            a, b = max(lo, s * c4), min(hi, (s + 1) * c4)
            if a < b:
                parts.append(shards[s][:, a - s * c4:b - s * c4])
    parts.append(jnp.zeros((shards.shape[1], BA_W - 2 * H), shards.dtype))
    return jnp.concatenate(parts, axis=1)


def _shard_in(d, W, H):
    groups = _in_groups(W, H)
    starts = [sum(hi - lo for lo, hi in groups[:i]) for i in range(len(groups))]
    stored = sorted(zip(groups, starts))
    c4 = sum(hi - lo for lo, hi in groups) // 4
    out = []
    for s in range(4):
        parts = []
        for (lo, hi), at in stored:
            a, b = max(lo, s * c4), min(hi, (s + 1) * c4)
            if a < b:
                parts.append(d[:, :, at + a - lo:at + b - lo])
        out.append(jnp.concatenate(parts, axis=2))
    return jnp.stack(out)


def _lane_row(vals, at):
    return jnp.pad(vals, (at, LANES - at - vals.shape[0]))[None]


def _layer_fwd(x, lw, tabs, W, H, more=None, h=None, g1_next=None):
    D = x.shape[1]
    cbk = 7 * W // LANES
    if h is None:
        h = _pre_norm(x, lw["g1"])
    proj = _mm("mm_in", h, lw["win"], "nn", BF16, tm=4096, tn=512)
    ba = _mm("mm_ba", h, lw["win"][:, 7 * W + 2 * KV_W:], "nn", F32)
    qkv = _dn_prep(proj, lw["conv"], W)
    beta_b, g_b = _dn_gates(ba, lw["alog"], lw["dt"], H)
    o, st = _delta_fwd(qkv, beta_b, g_b, H, DELTA_CB, DELTA_HB)
    oa = _dn_out(o, proj, lw["ng"], W, 3)
    ob = _swa_fwd(proj, tabs, lw["sinks"], W, 4, cbk)
    if more is not None:
        lw.update(more(ob))
    ya = _mm("mm_up_dn", oa, lw["wup_dn"], "nn", BF16)
    tmx = _tile(x.shape[0], 1024)
    col = lambda cb: pl.BlockSpec((tmx, D), lambda i, j, k: (i, cb))
    f32 = lambda t: t.astype(F32)
    yb, mixin = _mm("mm_up_sw_mix", ob, lw["wup_sw"], "nn", tm=tmx, tn=D, extras=(proj, proj, ya), extra_specs=[col(5), col(6), col(0)],
                    out_dtypes=(BF16, BF16), epi=lambda acc, ga, gb, y: (acc, _sigmoid(f32(ga)) * f32(y) + _sigmoid(f32(gb)) * acc))
    mix = _mm("mm_o", mixin, lw["wo"], "nn", F32)
    x1, h2 = _post_mix(x, mix, lw["g2"], lw["g3"])
    f1, act = _mm("mm_ff1", h2, lw["wff1"], "nn", tn=512, out_dtypes=(BF16, BF16), epi=lambda acc: (acc, jnp.square(jnp.maximum(acc, 0.0))))
    ff = _mm("mm_ff2", act, lw["wff2"], "nn", F32)
    x2, h_next = _post_mlp(x1, ff, lw["g4"], lw["g4"] if g1_next is None else g1_next)
    saved = dict(x=x, h=h, proj=proj, ba=ba, qkv=qkv, beta_b=beta_b, g_b=g_b, o=o, st=st, oa=oa, ob=ob, ya=ya, yb=yb,
                 mixin=mixin, mix=mix, x1=x1, h2=h2, f1=f1, act=act, ff=ff)
    return x2, h_next, saved


def _layer_bwd(dx2, lw, sv, tabs, W, H, l, big, weights_done=None):
    D = dx2.shape[1]
    cbk = 7 * W // LANES
    big = dict(big)
    dff, dg4 = _post_mlp_bwd(sv["ff"], lw["g4"], dx2)
    df1 = _mm("mm_ff2_dx", dff, lw["wff2"], "nt", BF16, extras=(sv["f1"],),
              epi=lambda acc, f1: (acc * 2.0 * jnp.maximum(f1.astype(F32), 0.0),))
    big["w_ff2"] = _mm("mm_ff2_dw", sv["act"], dff, "tn", tm=1024, tk=2048, slab=(big["w_ff2"], l))
    dh2 = _mm("mm_ff1_dx", df1, lw["wff1"], "nt", F32)
    big["w_ff1"] = _mm("mm_ff1_dw", sv["h2"], df1, "tn", tk=2048, slab=(big["w_ff1"], l))
    dx1, dmix, dg3, dg2 = _mid_bwd(sv["x1"], lw["g3"], dh2, dx2, sv["mix"], lw["g2"])
    dmixin = _mm("mm_o_dx", dmix, lw["wo"], "nt", BF16)
    big["w_o"] = _mm("mm_o_dw", sv["mixin"], dmix, "tn", tk=2048, slab=(big["w_o"], l))
    dya, dyb, dga, dgb = _mix_bwd(sv["proj"], sv["ya"], sv["yb"], dmixin, D, 5)
    doa = _mm("mm_up_dn_dx", dya, lw["wup_dn"], "nt", BF16)
    big["w_up_dn"] = _mm("mm_up_dn_dw", sv["oa"], dya, "tn", tk=2048, slab=(big["w_up_dn"], l))
    dob = _mm("mm_up_sw_dx", dyb, lw["wup_sw"], "nt", BF16)
    big["w_up_sw"] = _mm("mm_up_sw_dw", sv["ob"], dyb, "tn", tk=2048, slab=(big["w_up_sw"], l))
    do, dz, dng = _dn_out_bwd(sv["o"], sv["proj"], lw["ng"], doa, W, 3)
    dqkvn, dbeta_b, dg_b = _delta_bwd(sv["qkv"], sv["beta_b"], sv["g_b"], sv["st"], do, H, DELTA_CB, DELTA_HB)
    dba, dalog, ddt = _dn_gates_bwd(sv["ba"], lw["alog"], lw["dt"], dbeta_b, dg_b, H)
    dc, dconv = _dn_prep_bwd_a(sv["proj"], lw["conv"], dqkvn, W)
    dqkv = _dn_prep_bwd_b(dc, lw["conv"], W)
    dq_sw, dkc, dkp, dvc, dvp, dsk = _swa_bwd(sv["proj"], tabs, lw["sinks"], dob, W, 4, cbk)
    dkv = _swa_kv_combine(dkc, dkp, dvc, dvp, tabs)
    dproj = jnp.concatenate([dqkv, dz, dq_sw, dga, dgb, dkv, dba], axis=1)
    big["w_in"] = _mm("mm_in_dw", sv["h"], dproj, "tn", tn=768, tk=4096, slab=(big["w_in"], l))
    win = lw["win"]
    if weights_done is not None:
        win = win + weights_done(big).astype(BF16)
    dh = _mm("mm_in_dx", dproj, win, "nt", F32, tk=768)
    dx, dg1 = _pre_norm_bwd(sv["x"], lw["g1"], dh, dx1)
    grads = dict(pre_mix_g=dg1[0], dn_conv_w=dconv, dn_a_log=dalog[0, H:2 * H], dn_dt_bias=ddt[0, H:2 * H], dn_norm_g=dng[0],
                 sw_sinks=dsk[:SW_Q_HEADS, 0], post_mix_g=dg2[0], pre_mlp_g=dg3[0], post_mlp_g=dg4[0])
    return dx, grads, big


_WEIGHTS = ["pre_mix_g", "w_in", "dn_conv_w", "dn_a_log", "dn_dt_bias", "dn_norm_g", "sw_sinks", "w_up_dn", "w_up_sw", "w_o",
            "post_mix_g", "pre_mlp_g", "w_ff1", "w_ff2", "post_mlp_g"]
_BIG = {"w_in": 2, "w_up_dn": 1, "w_up_sw": 1, "w_o": 1, "w_ff1": 2, "w_ff2": 1}
_SMALL = [n for n in _WEIGHTS if n not in _BIG]


def _step(P):
    x, target = P["x"][0], P["loss_target"][0]
    S, D = x.shape
    L = P["pre_mix_g"].shape[0]
    H, W = DN_HEADS, DN_HEADS * DN_DK
    assert W == D == SW_Q_HEADS * SW_HD and KV_W == LANES
    me = 2 * lax.axis_index("x") + lax.axis_index("y")

    assert L % 4 == 0
    names = list(_BIG) + ["dn_conv_w"]
    local = [P[n].astype(BF16) for n in _BIG] + [P["dn_conv_w"]]
    early_names = ("w_in", "dn_conv_w")
    tail_names = [n for n in names if n not in early_names]
    own_slot = lambda gathered, mine: [lax.dynamic_update_slice_in_dim(g, w[None], me, 0) for g, w in zip(gathered, mine)]
    gather = lambda name, arrs, after=None: _chips_start(name, arrs, [WHOLE] * len(arrs), after)
    arrived = lambda name, h, after, keys: dict(zip(keys, own_slot(*reversed(_split_wait(name, h, after)))))
    h_first = gather("weights_first_start", [a[:1] for n, a in zip(names, local) if n in early_names])
    early = arrived("weights_first_wait", h_first, x, early_names)
    h_tail = gather("weights_tail_start", [a[:1] for n, a in zip(names, local) if n in tail_names], early["w_in"])
    h_next = gather("weights_next_start", [a[1:2] for a in local], h_tail["token"])
    h_rest = gather("weights_rest_start", [a[2:] for a in local], h_next["token"])

    def head(full, l, k):
        return dict(
            g1=P["pre_mix_g"][l][None], win=_relayout_in(full["w_in"][:, k], W, H),
            conv=jnp.concatenate([full["dn_conv_w"][s, k] for s in range(4)], axis=-1),
            alog=_lane_row(P["dn_a_log"][l], H), dt=_lane_row(P["dn_dt_bias"][l], H), ng=P["dn_norm_g"][l][None],
            sinks=_lane_row(P["sw_sinks"][l], 0), g2=P["post_mix_g"][l][None], g3=P["pre_mlp_g"][l][None], g4=P["post_mlp_g"][l][None])

    def tail(full, k):
        rows = lambda n: full[n][:, k].reshape(-1, full[n].shape[-1])
        return dict(wup_dn=rows("w_up_dn"), wup_sw=rows("w_up_sw"), wo=rows("w_o"), wff2=rows("w_ff2"),
                    wff1=jnp.concatenate([full["w_ff1"][s, k] for s in range(4)], axis=-1))

    tabs = _rope_tables(P["positions"].reshape(S, 1))
    lws = [head(early, 0, 0)]
    lws[0]["g1"] = lws[0]["g1"] + h_rest["token"][0, 0]

    saved, h = [], None
    for l in range(L):
        if l == 1:
            late = arrived("weights_next_wait", h_next, x, names)
            lws.append({**head(late, 1, 0), **tail(late, 0)})
        if l == 2:
            late = arrived("weights_rest_wait", h_rest, x, names)
            lws.extend({**head(late, k + 2, k), **tail(late, k)} for k in range(L - 2))
        first_tail = lambda after: tail(arrived("weights_tail_wait", h_tail, after, tail_names), 0)
        g1_next = P["pre_mix_g"][l + 1][None] if l + 1 < L else None
        x, h, sv = _layer_fwd(x, lws[l], tabs, W, H, first_tail if l == 0 else None, h, g1_next)
        saved.append(sv)
    loss_row, dx = _loss_head(x, target)

    Lb = L // 2
    layer_grads = [None] * L
    F = 4 * P["w_ff1"].shape[2]
    per_layer = dict(w_in=(D, 7 * W + 2 * KV_W + BA_W), w_up_dn=(W, D), w_up_sw=(W, D), w_o=(D, D), w_ff1=(D, F), w_ff2=(F, D))
    batch = [{n: lax.empty((Lb,) + per_layer[n], F32) for n in _BIG} for _ in range(2)]
    axes = [None if n == "w_in" else ax for n, ax in _BIG.items()]

    def pair_sums(tag, h_swap, after):
        g, got = _split_wait("grad_swap_wait_" + tag, h_swap, after)
        part = {n: _add_half("grad_pair_add_%s_%s" % (tag, n), a, r) for n, a, r in zip(_BIG, g, got)}
        return [_shard_in(part[n], W, H) if n == "w_in" else part[n] for n in _BIG]

    def chip_sums(tag, h_scat, after):
        parts, slots = _split_wait("grad_scatter_wait_" + tag, h_scat, after)
        halves = []
        for n, s, a, ax in zip(_BIG, slots, parts, axes):
            s = lax.dynamic_update_slice_in_dim(s, _own_part(a, ax, me)[None], me, 0)
            halves.append(_sum_slots("grad_chip_sum_%s_%s" % (tag, n), s.reshape(4, -1, s.shape[-1])).reshape(s.shape[1:]))
        h = _sibling_start("grad_share_start_" + tag, halves, False)
        return _split_wait("grad_share_wait_" + tag, h, halves[0])

    swaps = {}

    def swap_start(tag):
        def hook(big):
            swaps[tag] = _sibling_start("grad_swap_start_" + tag, [big[n] for n in _BIG], True)
            return swaps[tag]["token"][0, 0]
        return hook

    for l in reversed(range(L)):
        hook = swap_start("hi") if l == Lb else swap_start("lo") if l == 0 else None
        dx, layer_grads[l], batch[l // Lb] = _layer_bwd(dx, lws[l], saved[l], tabs, W, H, l % Lb, batch[l // Lb], hook)
        if l == Lb - 1:
            h_scat_hi = _chips_start("grad_scatter_start_hi", pair_sums("hi", swaps["hi"], dx), axes)
            if l > 0:
                lws[l - 1]["g4"] = lws[l - 1]["g4"] + h_scat_hi["token"][0, 0]

    grads = {n: jnp.stack([layer_grads[l][n] for l in range(L)]) for n in _SMALL}
    small_shapes = [(1,)] + [grads[n].shape for n in _SMALL]
    slots = _gather_all("small_gather", _pack([loss_row[0, :1]] + [grads[n] for n in _SMALL], LANES))
    h_scat_lo = _chips_start("grad_scatter_start_lo", pair_sums("lo", swaps["lo"], slots), axes)
    tot = _sum_slots("small_sum", slots + h_scat_lo["token"][0, 0])
    small = _unpack(tot, small_shapes)
    loss = small[0][0]
    gsum, delta, new_m, new_v = dict(zip(_SMALL, small[1:])), {}, {}, {}
    cw = P["dn_conv_w"].shape[2]
    gsum["dn_conv_w"] = lax.dynamic_slice_in_dim(gsum["dn_conv_w"], me * cw, cw, axis=2)
    sm_shapes = [P[n].shape for n in _SMALL]
    outs = _adamw("adamw_small", *(_pack([src[pre + n] for n in _SMALL], LANES)
                                   for src, pre in ((P, ""), (gsum, ""), (P, "m_"), (P, "v_"))))
    for d, o in zip((delta, new_m, new_v), outs):
        d.update(zip(_SMALL, _unpack(o, sm_shapes)))

    upper = {n: _adamw_halves("adamw_hi_" + n, P[n], mine, their, P["m_" + n], P["v_" + n], Lb)
             for n, mine, their in zip(_BIG, *chip_sums("hi", h_scat_hi, outs[0]))}
    for n, mine, their in zip(_BIG, *chip_sums("lo", h_scat_lo, upper["w_in"][0])):
        gsum[n], delta[n], new_m[n], new_v[n] = _adamw_halves("adamw_lo_" + n, P[n], mine, their, P["m_" + n], P["v_" + n], 0, upper[n])

    return (loss, dx[None], *[gsum[n] for n in _WEIGHTS], *[delta[n] for n in _WEIGHTS],
            *[new_m[n] for n in _WEIGHTS], *[new_v[n] for n in _WEIGHTS])


def kernel(x, positions, pre_mix_g, w_in, dn_conv_w, dn_a_log, dn_dt_bias, dn_norm_g, sw_sinks, w_up_dn, w_up_sw, w_o, post_mix_g, pre_mlp_g, w_ff1, w_ff2, post_mlp_g, loss_target, m_pre_mix_g, m_w_in, m_dn_conv_w, m_dn_a_log, m_dn_dt_bias, m_dn_norm_g, m_sw_sinks, m_w_up_dn, m_w_up_sw, m_w_o, m_post_mix_g, m_pre_mlp_g, m_w_ff1, m_w_ff2, m_post_mlp_g, v_pre_mix_g, v_w_in, v_dn_conv_w, v_dn_a_log, v_dn_dt_bias, v_dn_norm_g, v_sw_sinks, v_w_up_dn, v_w_up_sw, v_w_o, v_post_mix_g, v_pre_mlp_g, v_w_ff1, v_w_ff2, v_post_mlp_g):
    vals = (x, positions, pre_mix_g, w_in, dn_conv_w, dn_a_log, dn_dt_bias, dn_norm_g, sw_sinks, w_up_dn, w_up_sw, w_o, post_mix_g, pre_mlp_g, w_ff1, w_ff2, post_mlp_g, loss_target, m_pre_mix_g, m_w_in, m_dn_conv_w, m_dn_a_log, m_dn_dt_bias, m_dn_norm_g, m_sw_sinks, m_w_up_dn, m_w_up_sw, m_w_o, m_post_mix_g, m_pre_mlp_g, m_w_ff1, m_w_ff2, m_post_mlp_g, v_pre_mix_g, v_w_in, v_dn_conv_w, v_dn_a_log, v_dn_dt_bias, v_dn_norm_g, v_sw_sinks, v_w_up_dn, v_w_up_sw, v_w_o, v_post_mix_g, v_pre_mlp_g, v_w_ff1, v_w_ff2, v_post_mlp_g)
    names = ["x", "positions"] + _WEIGHTS + ["loss_target"] + ["m_" + n for n in _WEIGHTS] + ["v_" + n for n in _WEIGHTS]
    return _step(dict(zip(names, vals)))
```
